```python
import jax, jax.numpy as jnp
from jax import lax
import numpy as np

D_MODEL = 1024
BATCH = 4
SEQ = 8192
DEPTH = 2

PLE_DIM = 256
NORM_EPS = 1e-6
NEG_INF = -1e30

MOBA_HEADS = 8
MOBA_HEAD_DIM = 64
MOBA_BLOCK = 256
MOBA_TOPK = 3
MOBA_Q_CHUNK = 32
POOL_GROUPS = 4
POOL_GROUP_DIM = 128
POOL_WINDOWS = (2, 4, 8, 16)
MLSTM_HEADS = 4
MLSTM_HEAD_DIM = 128
MLSTM_CHUNK = 64
CONV_WIDTH = 4
MLA_HEADS = 4
MLA_Q_RANK = 256
MLA_KV_RANK = 128
MLA_NOPE_DIM = 64
MLA_ROPE_DIM = 32
MLA_V_DIM = 128
ROPE_BASE = 10000.0
ATTN_Q_BLOCK = 128
FFN_DIM = 2816
N_EXPERTS = 8
MOE_TOPK = 2
EXPERT_DIM = 3584

A_WIDTH = MOBA_HEADS * MOBA_HEAD_DIM
B_WIDTH = POOL_GROUPS * POOL_GROUP_DIM
C_WIDTH = MLSTM_HEADS * MLSTM_HEAD_DIM
D_WIDTH = MLA_HEADS * MLA_V_DIM
EVEN_IN = 3 * A_WIDTH + B_WIDTH
ODD_IN = 4 * C_WIDTH + 2 * MLSTM_HEADS + MLA_Q_RANK + MLA_KV_RANK + MLA_ROPE_DIM
EVEN_MIX = A_WIDTH + B_WIDTH
ODD_MIX = C_WIDTH + D_WIDTH

kernel_name = "hybrid_moba_pool_mlstm_mla_moe"


def rmsnorm(x, g):
    xf = x.astype(jnp.float32)
    y = xf * lax.rsqrt(jnp.mean(xf * xf, axis=-1, keepdims=True) + NORM_EPS)
    return (y * g.astype(jnp.float32)).astype(x.dtype)


def swiglu(x, wg, wu, wd):
    return (jax.nn.silu(x @ wg) * (x @ wu)) @ wd


def alibi_slopes(n):
    return jnp.asarray(2.0 ** (-8.0 * np.arange(1, n + 1) / n), dtype=jnp.float32)


def moba_attention(q, k, v):
    B, H, S, dh = q.shape
    n_blk = -(-S // MOBA_BLOCK)
    s_pad = n_blk * MOBA_BLOCK
    pad = ((0, 0), (0, 0), (0, s_pad - S), (0, 0))
    q = jnp.pad(q, pad).astype(jnp.float32)
    k = jnp.pad(k, pad).astype(jnp.float32)
    v = jnp.pad(v, pad).astype(jnp.float32)
    k_blocks = k.reshape(B, H, n_blk, MOBA_BLOCK, dh)
    v_blocks = v.reshape(B, H, n_blk, MOBA_BLOCK, dh)
    k_mean = jnp.mean(k_blocks, axis=3)
    gate = jnp.einsum('bhsd,bhnd->bhsn', q, k_mean)
    q_blk = jnp.arange(s_pad) // MOBA_BLOCK
    past = jnp.arange(n_blk)[None, :] < q_blk[:, None]
    gate = jnp.where(past, gate, NEG_INF)
    n_sel = min(MOBA_TOPK, n_blk)
    _, sel_idx = lax.top_k(gate, n_sel)
    sel_valid = sel_idx < q_blk[:, None]
    slopes = alibi_slopes(H)[None, :, None, None]
    scale = dh ** -0.5
    b_ix = jnp.arange(B)[:, None, None, None]
    h_ix = jnp.arange(H)[None, :, None, None]
    key_off = jnp.arange(MOBA_BLOCK)

    def chunk(c):
        t0 = c * MOBA_Q_CHUNK
        qc = lax.dynamic_slice_in_dim(q, t0, MOBA_Q_CHUNK, axis=2)
        idx = lax.dynamic_slice_in_dim(sel_idx, t0, MOBA_Q_CHUNK, axis=2)
        ok = lax.dynamic_slice_in_dim(sel_valid, t0, MOBA_Q_CHUNK, axis=2)
        own0 = (t0 // MOBA_BLOCK) * MOBA_BLOCK
        k_own = lax.dynamic_slice_in_dim(k, own0, MOBA_BLOCK, axis=2)
        v_own = lax.dynamic_slice_in_dim(v, own0, MOBA_BLOCK, axis=2)
        k_sel = k_blocks[b_ix, h_ix, idx]
        v_sel = v_blocks[b_ix, h_ix, idx]
        t_pos = t0 + jnp.arange(MOBA_Q_CHUNK)
        d_own = t_pos[:, None] - (own0 + key_off)[None, :]
        s_own = jnp.einsum('bhqd,bhkd->bhqk', qc, k_own) * scale - slopes * d_own.astype(jnp.float32)
        s_own = jnp.where(d_own >= 0, s_own, NEG_INF)
        d_sel = t_pos[:, None, None] - (idx[..., None] * MOBA_BLOCK + key_off)
        s_sel = jnp.einsum('bhqd,bhqnkd->bhqnk', qc, k_sel) * scale - slopes[..., None] * d_sel.astype(jnp.float32)
        s_sel = jnp.where(ok[..., None], s_sel, NEG_INF)
        scores = jnp.concatenate([s_sel.reshape(B, H, MOBA_Q_CHUNK, n_sel * MOBA_BLOCK), s_own], axis=-1)
        probs = jax.nn.softmax(scores, axis=-1)
        p_sel = probs[..., :n_sel * MOBA_BLOCK].reshape(B, H, MOBA_Q_CHUNK, n_sel, MOBA_BLOCK)
        p_own = probs[..., n_sel * MOBA_BLOCK:]
        return (jnp.einsum('bhqnk,bhqnkd->bhqd', p_sel, v_sel)
                + jnp.einsum('bhqk,bhkd->bhqd', p_own, v_own))

    out = lax.map(chunk, jnp.arange(s_pad // MOBA_Q_CHUNK))
    out = jnp.moveaxis(out, 0, 2).reshape(B, H, s_pad, dh)
    return out[:, :, :S]


def multiscale_pool(x, w_group, scale):
    B, S, _ = x.shape
    xg = x.reshape(B, S, POOL_GROUPS, POOL_GROUP_DIM).astype(jnp.float32)
    csum = jnp.concatenate([jnp.zeros_like(xg[:, :1]), jnp.cumsum(xg, axis=1)], axis=1)
    t = jnp.arange(S)[:, None]
    win = jnp.asarray(POOL_WINDOWS, dtype=jnp.int32)[None, :]
    lo = jnp.maximum(t + 1 - win, 0)
    count = (t + 1 - lo).astype(jnp.float32)
    lagged = csum[:, lo, jnp.arange(POOL_GROUPS)[None, :]]
    pooled = (csum[:, 1:] - lagged) / count[None, :, :, None]
    mixed = jnp.einsum('bsgc,gcd->bsgd', pooled - xg, w_group.astype(jnp.float32))
    return (mixed.reshape(B, S, B_WIDTH) * scale).astype(x.dtype)


def causal_conv(x, w):
    K, C = w.shape
    return lax.conv_general_dilated(
        x, w[:, None, :].astype(x.dtype), window_strides=(1,), padding=[(K - 1, 0)],
        dimension_numbers=('NWC', 'WIO', 'NWC'), feature_group_count=C)


def mlstm(q, k, v, i_pre, f_pre):
    B, H, S, d = q.shape
    L = MLSTM_CHUNK
    nc = S // L

    def to_chunks(t):
        return jnp.moveaxis(t.reshape((B, H, nc, L) + t.shape[3:]), 2, 0)

    logf = jax.nn.log_sigmoid(f_pre)
    tril = jnp.tril(jnp.ones((L, L), dtype=bool))

    def step(carry, xs):
        C, n, m = carry
        qc, kc, vc, ic, fc = xs
        b = jnp.cumsum(fc, axis=-1)
        intra = jnp.where(tril, b[..., :, None] - b[..., None, :] + ic[..., None, :], NEG_INF)
        m_inter = b + m[..., None]
        m_t = jnp.maximum(m_inter, jnp.max(intra, axis=-1))
        w_inter = jnp.exp(m_inter - m_t)
        a = jnp.exp(intra - m_t[..., None]) * jnp.einsum('bhtd,bhsd->bhts', qc, kc)
        num = (w_inter[..., None] * jnp.einsum('bhvk,bhtk->bhtv', C, qc)
               + jnp.einsum('bhts,bhsv->bhtv', a, vc))
        den = w_inter * jnp.einsum('bhk,bhtk->bht', n, qc) + jnp.sum(a, axis=-1)
        h = num / jnp.maximum(jnp.abs(den), jnp.exp(-m_t))[..., None]
        b_end = b[..., -1]
        g = b_end[..., None] - b + ic
        m_new = jnp.maximum(b_end + m, jnp.max(g, axis=-1))
        decay = jnp.exp(b_end + m - m_new)
        w_s = jnp.exp(g - m_new[..., None])
        C_new = decay[..., None, None] * C + jnp.einsum('bhs,bhsv,bhsk->bhvk', w_s, vc, kc)
        n_new = decay[..., None] * n + jnp.einsum('bhs,bhsk->bhk', w_s, kc)
        return (C_new, n_new, m_new), h

    init = (jnp.zeros((B, H, d, d), jnp.float32), jnp.zeros((B, H, d), jnp.float32),
            jnp.zeros((B, H), jnp.float32))
    _, hs = lax.scan(step, init, (to_chunks(q), to_chunks(k), to_chunks(v),
                                  to_chunks(i_pre), to_chunks(logf)))
    return jnp.moveaxis(hs, 0, 2).reshape(B, H, S, d)


def rope(x, ang):
    x1, x2 = jnp.split(x, 2, axis=-1)
    cos = jnp.cos(ang).astype(x.dtype)
    sin = jnp.sin(ang).astype(x.dtype)
    return jnp.concatenate([x1 * cos - x2 * sin, x1 * sin + x2 * cos], axis=-1)


def causal_attention(q, k, v, scale):
    B, H, S, _ = q.shape
    k_pos = jnp.arange(S)

    def block(i):
        t0 = i * ATTN_Q_BLOCK
        qb = lax.dynamic_slice_in_dim(q, t0, ATTN_Q_BLOCK, axis=2)
        s = jnp.einsum('bhqd,bhkd->bhqk', qb, k).astype(jnp.float32) * scale
        q_pos = t0 + jnp.arange(ATTN_Q_BLOCK)
        s = jnp.where(k_pos[None, :] <= q_pos[:, None], s, NEG_INF)
        p = jax.nn.softmax(s, axis=-1)
        return jnp.einsum('bhqk,bhkd->bhqd', p.astype(v.dtype), v)

    out = lax.map(block, jnp.arange(S // ATTN_Q_BLOCK))
    return jnp.moveaxis(out, 0, 2).reshape(B, H, S, v.shape[-1])


def mla(c_q, c_kv, k_rope_in, q_norm, w_uq, kv_norm, w_ukv):
    B, S, _ = c_q.shape
    q = (rmsnorm(c_q, q_norm) @ w_uq).reshape(B, S, MLA_HEADS, MLA_NOPE_DIM + MLA_ROPE_DIM)
    kv = (rmsnorm(c_kv, kv_norm) @ w_ukv).reshape(B, S, MLA_HEADS, MLA_NOPE_DIM + MLA_V_DIM)
    half = MLA_ROPE_DIM // 2
    inv_freq = ROPE_BASE ** (-jnp.arange(half, dtype=jnp.float32) / half)
    ang = jnp.arange(S, dtype=jnp.float32)[:, None] * inv_freq[None, :]
    q_rot = rope(q[..., MLA_NOPE_DIM:], ang[:, None, :])
    k_rot = rope(k_rope_in, ang)
    q_full = jnp.concatenate([q[..., :MLA_NOPE_DIM], q_rot], axis=-1)
    k_full = jnp.concatenate(
        [kv[..., :MLA_NOPE_DIM], jnp.broadcast_to(k_rot[:, :, None, :], (B, S, MLA_HEADS, MLA_ROPE_DIM))], axis=-1)
    v = kv[..., MLA_NOPE_DIM:]
    out = causal_attention(q_full.transpose(0, 2, 1, 3), k_full.transpose(0, 2, 1, 3),
                           v.transpose(0, 2, 1, 3), (MLA_NOPE_DIM + MLA_ROPE_DIM) ** -0.5)
    return out.transpose(0, 2, 1, 3).reshape(B, S, D_WIDTH)


def moe_swiglu(x, router_w, router_b, wg, wu, wd):
    B, S, D = x.shape
    xt = x.reshape(B * S, D)
    logits = (xt @ router_w).astype(jnp.float32) + router_b.astype(jnp.float32)
    top_val, top_idx = lax.top_k(logits, MOE_TOPK)
    top_w = jax.nn.softmax(top_val, axis=-1)
    gates = jnp.einsum('tk,tke->te', top_w, jax.nn.one_hot(top_idx, N_EXPERTS, dtype=jnp.float32))
    y = jnp.zeros_like(xt)
    for e in range(N_EXPERTS):
        y = y + gates[:, e:e + 1].astype(x.dtype) * swiglu(xt, wg[e], wu[e], wd[e])
    return y.reshape(B, S, D)


def per_layer_embedding(h, p_i, g, w_gate, w_proj):
    gate = jax.nn.sigmoid(rmsnorm(h, g) @ w_gate)
    return h + gate * (p_i @ w_proj)


def even_layer(h, norm_mix, w_in, pool_w, pool_scale, w_out, norm_ffn, wg, wu, wd):
    B, S, _ = h.shape
    u = rmsnorm(h, norm_mix) @ w_in
    qa, ka, va, ub = jnp.split(u, [A_WIDTH, 2 * A_WIDTH, 3 * A_WIDTH], axis=-1)

    def heads(t):
        return t.reshape(B, S, MOBA_HEADS, MOBA_HEAD_DIM).transpose(0, 2, 1, 3)

    a_out = moba_attention(heads(qa), heads(ka), heads(va))
    a_out = a_out.transpose(0, 2, 1, 3).reshape(B, S, A_WIDTH).astype(h.dtype)
    b_out = multiscale_pool(ub, pool_w, pool_scale)
    h = h + jnp.concatenate([a_out, b_out], axis=-1) @ w_out
    return h + swiglu(rmsnorm(h, norm_ffn), wg, wu, wd)


def odd_layer(h, norm_mix, w_in, conv_w, b_i, b_f, head_norm, q_norm, w_uq, kv_norm, w_ukv,
              w_out, norm_ffn, router_w, router_b, wg, wu, wd):
    B, S, _ = h.shape
    u = rmsnorm(h, norm_mix) @ w_in
    cuts = np.cumsum([C_WIDTH, C_WIDTH, C_WIDTH, C_WIDTH, MLSTM_HEADS, MLSTM_HEADS,
                      MLA_Q_RANK, MLA_KV_RANK]).tolist()
    q_raw, k_raw, v_c, o_pre, i_pre, f_pre, c_q, c_kv, k_rope_in = jnp.split(u, cuts, axis=-1)
    qk = jax.nn.silu(causal_conv(jnp.concatenate([q_raw, k_raw], axis=-1), conv_w))
    q_c, k_c = jnp.split(qk, 2, axis=-1)

    def heads(t):
        return t.reshape(B, S, MLSTM_HEADS, MLSTM_HEAD_DIM).transpose(0, 2, 1, 3).astype(jnp.float32)

    i_g = (i_pre + b_i).astype(jnp.float32).transpose(0, 2, 1)
    f_g = (f_pre + b_f).astype(jnp.float32).transpose(0, 2, 1)
    hc = mlstm(heads(q_c), heads(k_c) * (MLSTM_HEAD_DIM ** -0.5), heads(v_c), i_g, f_g)
    hc = rmsnorm(hc.transpose(0, 2, 1, 3), head_norm.reshape(MLSTM_HEADS, MLSTM_HEAD_DIM))
    c_out = hc.reshape(B, S, C_WIDTH).astype(h.dtype) * jax.nn.sigmoid(o_pre)
    d_out = mla(c_q, c_kv, k_rope_in, q_norm, w_uq, kv_norm, w_ukv)
    h = h + jnp.concatenate([c_out, d_out], axis=-1) @ w_out
    return h + moe_swiglu(rmsnorm(h, norm_ffn), router_w, router_b, wg, wu, wd)


def setup_inputs(seed: int = 0) -> dict:
    key = jax.random.key(seed)
    ks = iter(jax.random.split(key, 40))
    ne = (DEPTH + 1) // 2
    no = DEPTH // 2
    f32 = jnp.float32

    def w(shape, fan_in):
        return jax.random.normal(next(ks), shape, f32) * fan_in ** -0.5

    def gain(shape):
        return 1.0 + 0.1 * jax.random.normal(next(ks), shape, f32)

    return {
        "x": jax.random.normal(next(ks), (BATCH, SEQ, D_MODEL), f32),
        "p": jax.random.normal(next(ks), (DEPTH, BATCH, SEQ, PLE_DIM), f32),
        "ev_norm_mix": gain((ne, D_MODEL)),
        "ev_w_in": w((ne, D_MODEL, EVEN_IN), D_MODEL),
        "pool_w": w((ne, POOL_GROUPS, POOL_GROUP_DIM, POOL_GROUP_DIM), POOL_GROUP_DIM),
        "pool_scale": gain((ne, B_WIDTH)),
        "ev_w_out": w((ne, EVEN_MIX, D_MODEL), EVEN_MIX),
        "ev_norm_ffn": gain((ne, D_MODEL)),
        "ffn_w_gate": w((ne, D_MODEL, FFN_DIM), D_MODEL),
        "ffn_w_up": w((ne, D_MODEL, FFN_DIM), D_MODEL),
        "ffn_w_down": w((ne, FFN_DIM, D_MODEL), FFN_DIM),
        "od_norm_mix": gain((no, D_MODEL)),
        "od_w_in": w((no, D_MODEL, ODD_IN), D_MODEL),
        "conv_w": w((no, CONV_WIDTH, 2 * C_WIDTH), CONV_WIDTH),
        "gate_b_i": 0.1 * jax.random.normal(next(ks), (no, MLSTM_HEADS), f32),
        "gate_b_f": 3.0 + 0.5 * jax.random.normal(next(ks), (no, MLSTM_HEADS), f32),
        "mlstm_norm": gain((no, C_WIDTH)),
        "mla_q_norm": gain((no, MLA_Q_RANK)),
        "mla_w_uq": w((no, MLA_Q_RANK, MLA_HEADS * (MLA_NOPE_DIM + MLA_ROPE_DIM)), MLA_Q_RANK),
        "mla_kv_norm": gain((no, MLA_KV_RANK)),
        "mla_w_ukv": w((no, MLA_KV_RANK, MLA_HEADS * (MLA_NOPE_DIM + MLA_V_DIM)), MLA_KV_RANK),
        "od_w_out": w((no, ODD_MIX, D_MODEL), ODD_MIX),
        "od_norm_ffn": gain((no, D_MODEL)),
        "router_w": w((no, D_MODEL, N_EXPERTS), D_MODEL),
        "router_b": 0.01 * jax.random.normal(next(ks), (no, N_EXPERTS), f32),
        "moe_w_gate": w((no, N_EXPERTS, D_MODEL, EXPERT_DIM), D_MODEL),
        "moe_w_up": w((no, N_EXPERTS, D_MODEL, EXPERT_DIM), D_MODEL),
        "moe_w_down": w((no, N_EXPERTS, EXPERT_DIM, D_MODEL), EXPERT_DIM),
        "ple_norm": gain((DEPTH, D_MODEL)),
        "ple_w_gate": w((DEPTH, D_MODEL, D_MODEL), D_MODEL),
        "ple_w_proj": w((DEPTH, PLE_DIM, D_MODEL), PLE_DIM),
        "final_norm": gain((D_MODEL,)),
    }


def reference(x, p, ev_norm_mix, ev_w_in, pool_w, pool_scale, ev_w_out, ev_norm_ffn,
              ffn_w_gate, ffn_w_up, ffn_w_down, od_norm_mix, od_w_in, conv_w, gate_b_i, gate_b_f,
              mlstm_norm, mla_q_norm, mla_w_uq, mla_kv_norm, mla_w_ukv, od_w_out, od_norm_ffn,
              router_w, router_b, moe_w_gate, moe_w_up, moe_w_down, ple_norm, ple_w_gate,
              ple_w_proj, final_norm):
    h = x
    for layer in range(DEPTH):
        j = layer // 2
        if layer % 2 == 0:
            h = even_layer(h, ev_norm_mix[j], ev_w_in[j], pool_w[j], pool_scale[j], ev_w_out[j],
                           ev_norm_ffn[j], ffn_w_gate[j], ffn_w_up[j], ffn_w_down[j])
        else:
            h = odd_layer(h, od_norm_mix[j], od_w_in[j], conv_w[j], gate_b_i[j], gate_b_f[j],
                          mlstm_norm[j], mla_q_norm[j], mla_w_uq[j], mla_kv_norm[j], mla_w_ukv[j],
                          od_w_out[j], od_norm_ffn[j], router_w[j], router_b[j],
                          moe_w_gate[j], moe_w_up[j], moe_w_down[j])
        h = per_layer_embedding(h, p[layer], ple_norm[layer], ple_w_gate[layer], ple_w_proj[layer])
    return rmsnorm(h, final_norm)
```

```python
import functools
import math

import numpy as np
import jax
import jax.numpy as jnp
from jax import lax
from jax.experimental import pallas as pl
from jax.experimental.pallas import tpu as pltpu

F32 = jnp.float32
BF16 = jnp.bfloat16
HIGHEST = lax.Precision.HIGHEST

D_MODEL = 1024
PLE_DIM = 256
NORM_EPS = 1e-6
NEG_INF = -1e30

MOBA_HEADS = 8
MOBA_HEAD_DIM = 64
MOBA_BLOCK = 256
MOBA_TOPK = 3
POOL_WINDOWS = (2, 4, 8, 16)
POOL_GROUP_DIM = 128
POOL_HALO = 16
MLSTM_HEADS = 4
MLSTM_HEAD_DIM = 128
MLSTM_CHUNK = 128
CONV_WIDTH = 4
MLA_HEADS = 4
MLA_Q_RANK = 256
MLA_KV_RANK = 128
MLA_NOPE_DIM = 64
MLA_ROPE_DIM = 32
MLA_V_DIM = 128
ROPE_BASE = 10000.0
FFN_DIM = 2816
N_EXPERTS = 8
EXPERT_DIM = 3584
MIX_WIDTH = 512

ATTN_BLOCK = 256
ROW_TILE = 512
LANES = 128
VMEM_LIMIT = 56 * 1024 * 1024

MISC_ROPE = 0
MISC_I = 32
MISC_F = 36


def _params(sem, vmem=VMEM_LIMIT):
    return pltpu.CompilerParams(dimension_semantics=sem, vmem_limit_bytes=vmem)


def _rms(x, g):
    ms = jnp.mean(x * x, axis=-1, keepdims=True)
    return x * lax.rsqrt(ms + NORM_EPS) * g


def _sigmoid(x):
    return 1.0 / (1.0 + jnp.exp(-x))


def _dot(a, b):
    return jnp.dot(a, b, preferred_element_type=F32)


def _dot_nt(a, b, precision=None):
    return lax.dot_general(a, b, (((1,), (1,)), ((), ())), precision=precision,
                           preferred_element_type=F32)


def _even_in_kernel(x_ref, g_ref, wn_ref, wqT_ref, wvT_ref,
                    kb_ref, ub_ref, km_ref, qa_ref, qT_ref, vT_ref):
    tm = x_ref.shape[1]
    xn = _rms(x_ref[0], g_ref[...]).astype(BF16)
    n = _dot(xn, wn_ref[...])
    k = n[:, :MIX_WIDTH]
    kb_ref[0] = k.astype(BF16)
    ub_ref[0] = n[:, MIX_WIDTH:]
    for j in range(tm // MOBA_BLOCK):
        km_ref[0, j] = jnp.mean(k[j * MOBA_BLOCK:(j + 1) * MOBA_BLOCK], axis=0, keepdims=True)
    qT = _dot_nt(wqT_ref[...], xn)
    qT_ref[0] = qT
    scale = MOBA_HEAD_DIM ** -0.5
    zeros = jnp.zeros((MOBA_HEAD_DIM, tm), BF16)
    for h in range(MOBA_HEADS):
        q_h = (qT[h * MOBA_HEAD_DIM:(h + 1) * MOBA_HEAD_DIM] * scale).astype(BF16)
        base = h * LANES
        if h % 2 == 0:
            qa_ref[0, base:base + MOBA_HEAD_DIM] = q_h
            qa_ref[0, base + MOBA_HEAD_DIM:base + LANES] = zeros
        else:
            qa_ref[0, base:base + MOBA_HEAD_DIM] = zeros
            qa_ref[0, base + MOBA_HEAD_DIM:base + LANES] = q_h
    vT = _dot_nt(wvT_ref[...], xn)
    for j in range(tm // ATTN_BLOCK):
        vT_ref[0, j] = vT[:, j * ATTN_BLOCK:(j + 1) * ATTN_BLOCK].astype(BF16)


def _even_in(x, g, w_in):
    B, S, D = x.shape
    tm = ROW_TILE
    nb = S // MOBA_BLOCK
    wq, wk, wv, wu = (w_in[:, i * MIX_WIDTH:(i + 1) * MIX_WIDTH] for i in range(4))
    wn = jnp.concatenate([wk, wu], axis=1).astype(BF16)
    wqT = wq.T.astype(BF16)
    wvT = wv.T.astype(BF16)
    const = lambda b, i: (0, 0)
    return pl.pallas_call(
        _even_in_kernel,
        grid=(B, S // tm),
        in_specs=[
            pl.BlockSpec((1, tm, D), lambda b, i: (b, i, 0)),
            pl.BlockSpec((1, D), const),
            pl.BlockSpec((D, 2 * MIX_WIDTH), const),
            pl.BlockSpec((MIX_WIDTH, D), const),
            pl.BlockSpec((MIX_WIDTH, D), const),
        ],
        out_specs=[
            pl.BlockSpec((1, tm, MIX_WIDTH), lambda b, i: (b, i, 0)),
            pl.BlockSpec((1, tm, MIX_WIDTH), lambda b, i: (b, i, 0)),
            pl.BlockSpec((1, tm // MOBA_BLOCK, 1, MIX_WIDTH), lambda b, i: (b, i, 0, 0)),
            pl.BlockSpec((1, MOBA_HEADS * LANES, tm), lambda b, i: (b, 0, i)),
            pl.BlockSpec((1, MIX_WIDTH, tm), lambda b, i: (b, 0, i)),
            pl.BlockSpec((1, tm // ATTN_BLOCK, MIX_WIDTH, ATTN_BLOCK), lambda b, i: (b, i, 0, 0)),
        ],
        out_shape=[
            jax.ShapeDtypeStruct((B, S, MIX_WIDTH), BF16),
            jax.ShapeDtypeStruct((B, S, MIX_WIDTH), F32),
            jax.ShapeDtypeStruct((B, nb, 1, MIX_WIDTH), F32),
            jax.ShapeDtypeStruct((B, MOBA_HEADS * LANES, S), BF16),
            jax.ShapeDtypeStruct((B, MIX_WIDTH, S), F32),
            jax.ShapeDtypeStruct((B, S // ATTN_BLOCK, MIX_WIDTH, ATTN_BLOCK), BF16),
        ],
        compiler_params=_params(("parallel", "parallel")),
        name="even_in",
    )(x, g.reshape(1, D), wn, wqT, wvT)


def _moba_gate_kernel(sl_ref, km_ref, qT_ref, m_ref):
    h = pl.program_id(1)
    i = pl.program_id(2)
    nb = km_ref.shape[2]
    gate = jnp.dot(km_ref[0, 0], qT_ref[0], precision=HIGHEST, preferred_element_type=F32)
    row = lax.broadcasted_iota(jnp.int32, gate.shape, 0)
    past = row < i
    gate = jnp.where(past, gate, NEG_INF)
    chosen = jnp.zeros(gate.shape, F32)
    for _ in range(MOBA_TOPK):
        mx = jnp.max(gate, axis=0, keepdims=True)
        first = jnp.min(jnp.where(gate == mx, row, nb), axis=0, keepdims=True)
        pick = row == first
        chosen = jnp.where(pick, 1.0, chosen)
        gate = jnp.where(pick, -jnp.inf, gate)
    offset = sl_ref[h] * (MOBA_BLOCK * (row - i)).astype(F32)
    m = jnp.where(past, jnp.where(chosen > 0.0, offset, NEG_INF),
                  jnp.where(row == i, 0.0, NEG_INF))
    m_ref[0, 0] = m


def _moba_gate(slopes, kmean, qT):
    B, H, nb, dh = kmean.shape
    S = qT.shape[2]
    tq = MOBA_BLOCK
    return pl.pallas_call(
        _moba_gate_kernel,
        grid=(B, H, S // tq),
        in_specs=[
            pl.BlockSpec(memory_space=pltpu.SMEM),
            pl.BlockSpec((1, 1, nb, dh), lambda b, h, i: (b, h, 0, 0)),
            pl.BlockSpec((1, dh, tq), lambda b, h, i: (b, h, i)),
        ],
        out_specs=pl.BlockSpec((1, 1, nb, tq), lambda b, h, i: (b, h, 0, i)),
        out_shape=jax.ShapeDtypeStruct((B, H, nb, S), F32),
        compiler_params=_params(("parallel", "parallel", "parallel")),
        name="moba_gate",
    )(slopes, kmean, qT)


def _attn_kernel(*refs, moba):
    if moba:
        sl_ref, q_ref, m_ref, k_ref, v_ref, o_ref, bias_ref = refs
    else:
        q_ref, k_ref, v_ref, o_ref = refs
    i = pl.program_id(2)
    tq = q_ref.shape[2]
    tk = ATTN_BLOCK
    dv = v_ref.shape[2]
    key = lax.broadcasted_iota(jnp.int32, (tk, tq), 0)
    qry = lax.broadcasted_iota(jnp.int32, (tk, tq), 1)
    if moba:
        @pl.when(i == 0)
        def _():
            bias_ref[...] = sl_ref[pl.program_id(1)] * (key - qry).astype(F32)
    q = q_ref[0]

    def step(kv, carry, diag):
        m_run, l_run, acc = carry
        start = pl.multiple_of(kv * tk, tk)
        s = _dot(k_ref[0, pl.ds(start, tk), :], q)
        if moba:
            s = s + bias_ref[...] + m_ref[0, 0, pl.ds(kv, 1), :]
        if diag:
            s = jnp.where(key <= qry, s, NEG_INF)
        m_new = jnp.maximum(m_run, jnp.max(s, axis=0, keepdims=True))
        alpha = jnp.exp(m_run - m_new)
        p = jnp.exp(s - m_new)
        l_new = alpha * l_run + jnp.sum(p, axis=0, keepdims=True)
        acc = alpha * acc + _dot(v_ref[0, kv], p.astype(BF16))
        return m_new, l_new, acc

    init = (jnp.full((1, tq), NEG_INF, F32), jnp.zeros((1, tq), F32), jnp.zeros((dv, tq), F32))
    carry = step(i, init, True)
    m_run, l_run, acc = lax.fori_loop(0, i, lambda kv, c: step(kv, c, False), carry)
    o_ref[0] = acc / l_run


def _attention(qT, k, vT, dv, heads, lanes_per_k, slopes=None, mrow=None):
    B, _, S = qT.shape
    tq = ATTN_BLOCK
    nb = S // ATTN_BLOCK
    moba = slopes is not None
    in_specs = [
        pl.BlockSpec((1, LANES, tq), lambda b, h, i: (b, h, i)),
        pl.BlockSpec((1, S, LANES), lambda b, h, i: (b, 0, h // lanes_per_k)),
        pl.BlockSpec((1, nb, dv, ATTN_BLOCK), lambda b, h, i: (b, 0, h, 0)),
    ]
    args = [qT, k, vT]
    scratch = []
    if moba:
        in_specs = ([pl.BlockSpec(memory_space=pltpu.SMEM), in_specs[0],
                     pl.BlockSpec((1, 1, nb, tq), lambda b, h, i: (b, h, 0, i))] + in_specs[1:])
        args = [slopes, qT, mrow, k, vT]
        scratch = [pltpu.VMEM((ATTN_BLOCK, tq), F32)]
    return pl.pallas_call(
        functools.partial(_attn_kernel, moba=moba),
        grid=(B, heads, S // tq),
        in_specs=in_specs,
        out_specs=pl.BlockSpec((1, dv, tq), lambda b, h, i: (b, h, i)),
        out_shape=jax.ShapeDtypeStruct((B, heads * dv, S), F32),
        scratch_shapes=scratch,
        compiler_params=_params(("parallel", "parallel", "arbitrary")),
        name="moba_attn" if moba else "mla_attn",
    )(*args)


def _pool_kernel(x_ref, halo_ref, w_ref, sc_ref, o_ref, xs_ref):
    i = pl.program_id(1)
    tm = x_ref.shape[1]
    x = x_ref[0]
    xs_ref[0:POOL_HALO] = jnp.where(i > 0, halo_ref[0], 0.0)
    xs_ref[POOL_HALO:POOL_HALO + tm] = x
    t = i * tm + lax.broadcasted_iota(jnp.int32, (tm, 1), 0)
    outs = []
    for g, win in enumerate(POOL_WINDOWS):
        lo, hi = g * POOL_GROUP_DIM, (g + 1) * POOL_GROUP_DIM
        xg = x[:, lo:hi]
        acc = xg
        for d in range(1, win):
            acc = acc + xs_ref[POOL_HALO - d:POOL_HALO - d + tm, lo:hi]
        count = jnp.minimum(t + 1, win).astype(F32)
        outs.append(_dot((acc / count - xg).astype(BF16), w_ref[g]))
    o_ref[0] = (jnp.concatenate(outs, axis=1) * sc_ref[...]).astype(o_ref.dtype)


def _pool(ub, pool_w, pool_scale):
    B, S, W = ub.shape
    tm = ROW_TILE
    per = tm // POOL_HALO
    return pl.pallas_call(
        _pool_kernel,
        grid=(B, S // tm),
        in_specs=[
            pl.BlockSpec((1, tm, W), lambda b, i: (b, i, 0)),
            pl.BlockSpec((1, POOL_HALO, W), lambda b, i: (b, jnp.maximum(i * per - 1, 0), 0)),
            pl.BlockSpec(pool_w.shape, lambda b, i: (0, 0, 0)),
            pl.BlockSpec((1, W), lambda b, i: (0, 0)),
        ],
        out_specs=pl.BlockSpec((1, tm, W), lambda b, i: (b, i, 0)),
        out_shape=jax.ShapeDtypeStruct((B, S, W), BF16),
        scratch_shapes=[pltpu.VMEM((POOL_HALO + tm, W), F32)],
        compiler_params=_params(("parallel", "parallel")),
        name="pool",
    )(ub, ub, pool_w.astype(BF16), pool_scale.reshape(1, W))


def _mix_out_kernel(h_ref, aT_ref, b_ref, w_ref, o_ref, *, a_first):
    a = aT_ref[0].T.astype(BF16)
    b = b_ref[0].astype(BF16)
    lo, hi = (a, b) if a_first else (b, a)
    y = _dot(lo, w_ref[:MIX_WIDTH]) + _dot(hi, w_ref[MIX_WIDTH:])
    o_ref[0] = h_ref[0] + y


def _mix_out(h, aT, b, w_out, a_first):
    B, S, D = h.shape
    tm = ROW_TILE
    return pl.pallas_call(
        functools.partial(_mix_out_kernel, a_first=a_first),
        grid=(B, S // tm),
        in_specs=[
            pl.BlockSpec((1, tm, D), lambda b_, i: (b_, i, 0)),
            pl.BlockSpec((1, MIX_WIDTH, tm), lambda b_, i: (b_, 0, i)),
            pl.BlockSpec((1, tm, MIX_WIDTH), lambda b_, i: (b_, i, 0)),
            pl.BlockSpec((2 * MIX_WIDTH, D), lambda b_, i: (0, 0)),
        ],
        out_specs=pl.BlockSpec((1, tm, D), lambda b_, i: (b_, i, 0)),
        out_shape=jax.ShapeDtypeStruct((B, S, D), F32),
        compiler_params=_params(("parallel", "parallel")),
        name="mix_out",
    )(h, aT, b, w_out.astype(BF16))


def _ffn_kernel(*refs, gated):
    if gated:
        h_ref, g_ref, gate_ref, wg_ref, wu_ref, wd_ref, o_ref, xn_ref, acc_ref = refs
    else:
        h_ref, g_ref, wg_ref, wu_ref, wd_ref, o_ref, xn_ref, acc_ref = refs
    e = pl.program_id(1)
    f = pl.program_id(2)

    @pl.when((e == 0) & (f == 0))
    def _():
        xn_ref[...] = _rms(h_ref[...], g_ref[...]).astype(BF16)
        acc_ref[...] = jnp.zeros_like(acc_ref)

    xn = xn_ref[...]
    gt = _dot(xn, wg_ref[0])
    up = _dot(xn, wu_ref[0])
    hid = gt * _sigmoid(gt) * up
    if gated:
        lane = lax.broadcasted_iota(jnp.int32, gate_ref.shape, 1)
        hid = hid * jnp.sum(jnp.where(lane == e, gate_ref[...], 0.0), axis=1, keepdims=True)
    acc_ref[...] += _dot(hid.astype(BF16), wd_ref[0])

    @pl.when((e == pl.num_programs(1) - 1) & (f == pl.num_programs(2) - 1))
    def _():
        o_ref[...] = h_ref[...] + acc_ref[...]


def _ffn(h2d, g, wg, wu, wd, gates=None, tm=ROW_TILE, tf=None):
    T, D = h2d.shape
    E, _, F = wg.shape
    gated = gates is not None
    in_specs = [
        pl.BlockSpec((tm, D), lambda i, e, f: (i, 0)),
        pl.BlockSpec((1, D), lambda i, e, f: (0, 0)),
    ]
    args = [h2d, g.reshape(1, D)]
    if gated:
        in_specs.append(pl.BlockSpec((tm, LANES), lambda i, e, f: (i, 0)))
        args.append(gates)
    in_specs += [
        pl.BlockSpec((1, D, tf), lambda i, e, f: (e, 0, f)),
        pl.BlockSpec((1, D, tf), lambda i, e, f: (e, 0, f)),
        pl.BlockSpec((1, tf, D), lambda i, e, f: (e, f, 0)),
    ]
    args += [wg, wu, wd]
    return pl.pallas_call(
        functools.partial(_ffn_kernel, gated=gated),
        grid=(T // tm, E, F // tf),
        in_specs=in_specs,
        out_specs=pl.BlockSpec((tm, D), lambda i, e, f: (i, 0)),
        out_shape=jax.ShapeDtypeStruct((T, D), F32),
        scratch_shapes=[pltpu.VMEM((tm, D), BF16), pltpu.VMEM((tm, D), F32)],
        compiler_params=_params(("parallel", "arbitrary", "arbitrary")),
        name="moe_ffn" if gated else "ffn",
    )(*args)


def _router_kernel(h_ref, g_ref, w_ref, b_ref, o_ref):
    xn = _rms(h_ref[...], g_ref[...])
    logits = jnp.dot(xn, w_ref[...], precision=HIGHEST, preferred_element_type=F32) + b_ref[...]
    lane = lax.broadcasted_iota(jnp.int32, logits.shape, 1)
    logits = jnp.where(lane < N_EXPERTS, logits, -jnp.inf)
    v0 = jnp.max(logits, axis=1, keepdims=True)
    i0 = jnp.min(jnp.where(logits == v0, lane, LANES), axis=1, keepdims=True)
    rest = jnp.where(lane == i0, -jnp.inf, logits)
    v1 = jnp.max(rest, axis=1, keepdims=True)
    i1 = jnp.min(jnp.where(rest == v1, lane, LANES), axis=1, keepdims=True)
    e1 = jnp.exp(v1 - v0)
    w0 = 1.0 / (1.0 + e1)
    o_ref[...] = jnp.where(lane == i0, w0, jnp.where(lane == i1, e1 * w0, 0.0))


def _router(h2d, g, router_w, router_b):
    T, D = h2d.shape
    tm = ROW_TILE
    w = jnp.zeros((D, LANES), F32).at[:, :N_EXPERTS].set(router_w)
    b = jnp.zeros((1, LANES), F32).at[0, :N_EXPERTS].set(router_b)
    return pl.pallas_call(
        _router_kernel,
        grid=(T // tm,),
        in_specs=[
            pl.BlockSpec((tm, D), lambda i: (i, 0)),
            pl.BlockSpec((1, D), lambda i: (0, 0)),
            pl.BlockSpec((D, LANES), lambda i: (0, 0)),
            pl.BlockSpec((1, LANES), lambda i: (0, 0)),
        ],
        out_specs=pl.BlockSpec((tm, LANES), lambda i: (i, 0)),
        out_shape=jax.ShapeDtypeStruct((T, LANES), F32),
        compiler_params=_params(("parallel",)),
        name="router",
    )(h2d, g.reshape(1, D), w, b)


def _ple_kernel(*refs, final):
    if final:
        h_ref, p_ref, g_ref, wg_ref, wp_ref, fg_ref, o_ref = refs
    else:
        h_ref, p_ref, g_ref, wg_ref, wp_ref, o_ref = refs
    h = h_ref[...]
    gate = _sigmoid(_dot(_rms(h, g_ref[...]).astype(BF16), wg_ref[...]))
    out = h + gate * _dot(p_ref[...].astype(BF16), wp_ref[...])
    if final:
        out = _rms(out, fg_ref[...])
    o_ref[...] = out


def _ple(h2d, p2d, g, w_gate, w_proj, final_g=None):
    T, D = h2d.shape
    tm = ROW_TILE
    final = final_g is not None
    in_specs = [
        pl.BlockSpec((tm, D), lambda i: (i, 0)),
        pl.BlockSpec((tm, PLE_DIM), lambda i: (i, 0)),
        pl.BlockSpec((1, D), lambda i: (0, 0)),
        pl.BlockSpec((D, D), lambda i: (0, 0)),
        pl.BlockSpec((PLE_DIM, D), lambda i: (0, 0)),
    ]
    args = [h2d, p2d, g.reshape(1, D), w_gate.astype(BF16), w_proj.astype(BF16)]
    if final:
        in_specs.append(pl.BlockSpec((1, D), lambda i: (0, 0)))
        args.append(final_g.reshape(1, D))
    return pl.pallas_call(
        functools.partial(_ple_kernel, final=final),
        grid=(T // tm,),
        in_specs=in_specs,
        out_specs=pl.BlockSpec((tm, D), lambda i: (i, 0)),
        out_shape=jax.ShapeDtypeStruct((T, D), F32),
        compiler_params=_params(("parallel",)),
        name="ple_final" if final else "ple",
    )(*args)


ODD_MAIN = 4 * MIX_WIDTH
ODD_COLS = ODD_MAIN + MLA_Q_RANK + MLA_KV_RANK + 2 * LANES
MLA_QK_SCALE = (MLA_NOPE_DIM + MLA_ROPE_DIM) ** -0.5
ROPE_HALF = MLA_ROPE_DIM // 2


def _odd_in_kernel(x_ref, g_ref, wn_ref, qn_ref, wuqT_ref, kvn_ref, wk2_ref, wvT_ref,
                   cosT_ref, sinT_ref, cc_ref, ss_ref,
                   qk_ref, vc_ref, op_ref, misc_ref, mq_ref, mk_ref, mv_ref):
    tm = x_ref.shape[1]
    xn = _rms(x_ref[0], g_ref[...]).astype(BF16)
    u = _dot(xn, wn_ref[...])
    qk_ref[0] = u[:, :2 * MIX_WIDTH]
    vc_ref[0] = u[:, 2 * MIX_WIDTH:3 * MIX_WIDTH].astype(BF16)
    op_ref[0] = u[:, 3 * MIX_WIDTH:ODD_MAIN]
    c0 = ODD_MAIN
    c_q = u[:, c0:c0 + MLA_Q_RANK]
    c0 += MLA_Q_RANK
    c_kv = u[:, c0:c0 + MLA_KV_RANK]
    c0 += MLA_KV_RANK
    misc = u[:, c0:c0 + LANES]
    misc_sw = u[:, c0 + LANES:c0 + 2 * LANES]
    misc_ref[0] = misc
    cqn = _rms(c_q, qn_ref[...]).astype(BF16)
    qT = _dot_nt(wuqT_ref[...], cqn)
    cosT = cosT_ref[...]
    sinT = sinT_ref[...]
    for h in range(MLA_HEADS):
        r = h * LANES
        mq_ref[0, r:r + MLA_NOPE_DIM] = (qT[r:r + MLA_NOPE_DIM] * MLA_QK_SCALE).astype(BF16)
        x1 = qT[r + MLA_NOPE_DIM:r + MLA_NOPE_DIM + ROPE_HALF]
        x2 = qT[r + MLA_NOPE_DIM + ROPE_HALF:r + MLA_NOPE_DIM + MLA_ROPE_DIM]
        mq_ref[0, r + MLA_NOPE_DIM:r + MLA_NOPE_DIM + ROPE_HALF] = (
            (x1 * cosT - x2 * sinT) * MLA_QK_SCALE).astype(BF16)
        mq_ref[0, r + MLA_NOPE_DIM + ROPE_HALF:r + MLA_NOPE_DIM + MLA_ROPE_DIM] = (
            (x1 * sinT + x2 * cosT) * MLA_QK_SCALE).astype(BF16)
        mq_ref[0, r + MLA_NOPE_DIM + MLA_ROPE_DIM:r + LANES] = jnp.zeros(
            (LANES - MLA_NOPE_DIM - MLA_ROPE_DIM, tm), BF16)
    ckvn = _rms(c_kv, kvn_ref[...]).astype(BF16)
    k_rot = (misc * cc_ref[...] + misc_sw * ss_ref[...]).astype(BF16)
    mk_ref[0] = _dot(jnp.concatenate([ckvn, k_rot], axis=1), wk2_ref[...]).astype(BF16)
    vT = _dot_nt(wvT_ref[...], ckvn)
    for j in range(tm // ATTN_BLOCK):
        mv_ref[0, j] = vT[:, j * ATTN_BLOCK:(j + 1) * ATTN_BLOCK].astype(BF16)


def _rope_tables(S):
    inv_freq = ROPE_BASE ** (-jnp.arange(ROPE_HALF, dtype=F32) / ROPE_HALF)
    ang = jnp.arange(S, dtype=F32)[:, None] * inv_freq[None, :]
    cos, sin = jnp.cos(ang), jnp.sin(ang)
    pad = jnp.zeros((S, LANES - MLA_ROPE_DIM), F32)
    cc = jnp.concatenate([cos, cos, pad], axis=1)
    ss = jnp.concatenate([-sin, sin, pad], axis=1)
    return cos.T, sin.T, cc, ss


def _odd_in(x, g, w_in, q_norm, w_uq, kv_norm, w_ukv):
    B, S, D = x.shape
    tm = ROW_TILE
    cuts = np.cumsum([MIX_WIDTH] * 4 + [MLSTM_HEADS, MLSTM_HEADS, MLA_Q_RANK, MLA_KV_RANK]).tolist()
    w_main = w_in[:, :cuts[3]]
    w_i = w_in[:, cuts[3]:cuts[4]]
    w_f = w_in[:, cuts[4]:cuts[5]]
    w_cq = w_in[:, cuts[5]:cuts[6]]
    w_ckv = w_in[:, cuts[6]:cuts[7]]
    w_kr = w_in[:, cuts[7]:]
    w_kr_sw = jnp.concatenate([w_kr[:, ROPE_HALF:], w_kr[:, :ROPE_HALF]], axis=1)
    zpad = lambda n: jnp.zeros((D, n), F32)
    w_misc = jnp.concatenate([w_kr, w_i, w_f, zpad(LANES - MLA_ROPE_DIM - 2 * MLSTM_HEADS)], axis=1)
    w_misc_sw = jnp.concatenate([w_kr_sw, zpad(LANES - MLA_ROPE_DIM)], axis=1)
    wn = jnp.concatenate([w_main, w_cq, w_ckv, w_misc, w_misc_sw], axis=1).astype(BF16)
    qd = MLA_NOPE_DIM + MLA_ROPE_DIM
    w_uq_h = w_uq.reshape(MLA_Q_RANK, MLA_HEADS, qd)
    w_uq_h = jnp.concatenate([w_uq_h, jnp.zeros((MLA_Q_RANK, MLA_HEADS, LANES - qd), F32)], axis=2)
    wuqT = w_uq_h.reshape(MLA_Q_RANK, MLA_HEADS * LANES).T.astype(BF16)
    w_ukv_h = w_ukv.reshape(MLA_KV_RANK, MLA_HEADS, MLA_NOPE_DIM + MLA_V_DIM)
    w_k = jnp.concatenate([w_ukv_h[:, :, :MLA_NOPE_DIM],
                           jnp.zeros((MLA_KV_RANK, MLA_HEADS, LANES - MLA_NOPE_DIM), F32)], axis=2)
    place = jnp.zeros((LANES, MLA_HEADS, LANES), F32)
    eye = jnp.eye(MLA_ROPE_DIM, dtype=F32)
    place = place.at[:MLA_ROPE_DIM, :, MLA_NOPE_DIM:MLA_NOPE_DIM + MLA_ROPE_DIM].set(
        jnp.broadcast_to(eye[:, None, :], (MLA_ROPE_DIM, MLA_HEADS, MLA_ROPE_DIM)))
    wk2 = jnp.concatenate([w_k, place], axis=0).reshape(MLA_KV_RANK + LANES, MLA_HEADS * LANES).astype(BF16)
    wvT = w_ukv_h[:, :, MLA_NOPE_DIM:].reshape(MLA_KV_RANK, MLA_HEADS * MLA_V_DIM).T.astype(BF16)
    cosT, sinT, cc, ss = _rope_tables(S)
    row = lambda b, i: (b, i, 0)
    const = lambda b, i: (0, 0)
    nb = S // ATTN_BLOCK
    return pl.pallas_call(
        _odd_in_kernel,
        grid=(B, S // tm),
        in_specs=[
            pl.BlockSpec((1, tm, D), row),
            pl.BlockSpec((1, D), const),
            pl.BlockSpec((D, ODD_COLS), const),
            pl.BlockSpec((1, MLA_Q_RANK), const),
            pl.BlockSpec((MLA_HEADS * LANES, MLA_Q_RANK), const),
            pl.BlockSpec((1, MLA_KV_RANK), const),
            pl.BlockSpec((MLA_KV_RANK + LANES, MLA_HEADS * LANES), const),
            pl.BlockSpec((MLA_HEADS * MLA_V_DIM, MLA_KV_RANK), const),
            pl.BlockSpec((ROPE_HALF, tm), lambda b, i: (0, i)),
            pl.BlockSpec((ROPE_HALF, tm), lambda b, i: (0, i)),
            pl.BlockSpec((tm, LANES), lambda b, i: (i, 0)),
            pl.BlockSpec((tm, LANES), lambda b, i: (i, 0)),
        ],
        out_specs=[
            pl.BlockSpec((1, tm, 2 * MIX_WIDTH), row),
            pl.BlockSpec((1, tm, MIX_WIDTH), row),
            pl.BlockSpec((1, tm, MIX_WIDTH), row),
            pl.BlockSpec((1, tm, LANES), row),
            pl.BlockSpec((1, MLA_HEADS * LANES, tm), lambda b, i: (b, 0, i)),
            pl.BlockSpec((1, tm, MLA_HEADS * LANES), row),
            pl.BlockSpec((1, tm // ATTN_BLOCK, MLA_HEADS * MLA_V_DIM, ATTN_BLOCK), lambda b, i: (b, i, 0, 0)),
        ],
        out_shape=[
            jax.ShapeDtypeStruct((B, S, 2 * MIX_WIDTH), F32),
            jax.ShapeDtypeStruct((B, S, MIX_WIDTH), BF16),
            jax.ShapeDtypeStruct((B, S, MIX_WIDTH), F32),
            jax.ShapeDtypeStruct((B, S, LANES), F32),
            jax.ShapeDtypeStruct((B, MLA_HEADS * LANES, S), BF16),
            jax.ShapeDtypeStruct((B, S, MLA_HEADS * LANES), BF16),
            jax.ShapeDtypeStruct((B, nb, MLA_HEADS * MLA_V_DIM, ATTN_BLOCK), BF16),
        ],
        compiler_params=_params(("parallel", "parallel")),
        name="odd_in",
    )(x, g.reshape(1, D), wn, q_norm.reshape(1, -1), wuqT, kv_norm.reshape(1, -1), wk2, wvT,
      cosT, sinT, cc, ss)


def _log_sigmoid(x):
    return jnp.minimum(x, 0.0) - jnp.log(1.0 + jnp.exp(-jnp.abs(x)))


def _mlstm_kernel(qk_ref, v_ref, op_ref, misc_ref, cw_ref, gb_ref, hn_ref, o_ref,
                  prev_ref, cn_ref, m_ref):
    c = pl.program_id(1)
    L = qk_ref.shape[1]
    d = MLSTM_HEAD_DIM

    @pl.when(c == 0)
    def _():
        prev_ref[...] = jnp.zeros_like(prev_ref)
        cn_ref[...] = jnp.zeros_like(cn_ref)
        m_ref[...] = jnp.zeros_like(m_ref)

    x = qk_ref[0]
    prev = prev_ref[...]
    row = lax.broadcasted_iota(jnp.int32, (L, 1), 0)
    conv = x * cw_ref[CONV_WIDTH - 1:CONV_WIDTH, :]
    for j in range(1, CONV_WIDTH):
        shifted = pltpu.roll(jnp.where(row >= L - j, prev, x), j, axis=0)
        conv = conv + shifted * cw_ref[CONV_WIDTH - 1 - j:CONV_WIDTH - j, :]
    prev_ref[...] = x
    qk = conv * _sigmoid(conv)

    gates = misc_ref[0] + gb_ref[...]
    lane = lax.broadcasted_iota(jnp.int32, (L, LANES), 1)
    is_f = (lane >= MISC_F) & (lane < MISC_F + MLSTM_HEADS)
    z = jnp.where(is_f, _log_sigmoid(gates), gates)
    tri_r = lax.broadcasted_iota(jnp.int32, (L, L), 0)
    tri_c = lax.broadcasted_iota(jnp.int32, (L, L), 1)
    causal = tri_c <= tri_r
    cum = jnp.dot(causal.astype(F32), z, precision=HIGHEST, preferred_element_type=F32)
    z = jnp.where(is_f, cum, z)
    zT = z.T

    ones_col = (lax.broadcasted_iota(jnp.int32, (L, LANES), 1) == 0).astype(BF16)
    for h in range(MLSTM_HEADS):
        lo, hi = h * d, (h + 1) * d
        q = qk[:, lo:hi].astype(BF16)
        k = qk[:, MIX_WIDTH + lo:MIX_WIDTH + hi] * (d ** -0.5)
        v_aug = jnp.concatenate([v_ref[0, :, lo:hi], ones_col], axis=1)
        i_col = z[:, MISC_I + h:MISC_I + h + 1]
        b_col = z[:, MISC_F + h:MISC_F + h + 1]
        i_row = zT[MISC_I + h:MISC_I + h + 1, :]
        b_row = zT[MISC_F + h:MISC_F + h + 1, :]
        m_prev = m_ref[h:h + 1, 0:1]
        intra = jnp.where(causal, b_col - b_row + i_row, NEG_INF)
        m_inter = b_col + m_prev
        m_t = jnp.maximum(m_inter, jnp.max(intra, axis=1, keepdims=True))
        w_inter = jnp.exp(m_inter - m_t)
        a = jnp.exp(intra - m_t) * _dot_nt(q, k.astype(BF16))
        inter = _dot(q, cn_ref[h].astype(BF16))
        intra_o = _dot(a.astype(BF16), v_aug)
        num = w_inter * inter[:, :d] + intra_o[:, :d]
        den = w_inter * inter[:, d:d + 1] + intra_o[:, d:d + 1]
        hh = num / jnp.maximum(jnp.abs(den), jnp.exp(-m_t))
        hh = _rms(hh, hn_ref[:, lo:hi])
        o_ref[0, :, lo:hi] = (hh * _sigmoid(op_ref[0, :, lo:hi])).astype(o_ref.dtype)
        b_end = b_col[L - 1:L, :]
        g_col = b_end - b_col + i_col
        m_new = jnp.maximum(b_end + m_prev, jnp.max(g_col, axis=0, keepdims=True))
        decay = jnp.exp(b_end + m_prev - m_new)
        kw = k * jnp.exp(g_col - m_new)
        cn_ref[h] = decay * cn_ref[h] + _dot(kw.T.astype(BF16), v_aug)
        m_ref[h:h + 1, :] = jnp.broadcast_to(m_new, (1, LANES))


def _mlstm(qk_raw, vc, o_pre, misc, conv_w, b_i, b_f, head_norm):
    B, S, _ = qk_raw.shape
    L = MLSTM_CHUNK
    gb = jnp.zeros((1, LANES), F32).at[0, MISC_I:MISC_I + MLSTM_HEADS].set(b_i)
    gb = gb.at[0, MISC_F:MISC_F + MLSTM_HEADS].set(b_f)
    row = lambda b, c: (b, c, 0)
    const = lambda b, c: (0, 0)
    return pl.pallas_call(
        _mlstm_kernel,
        grid=(B, S // L),
        in_specs=[
            pl.BlockSpec((1, L, 2 * MIX_WIDTH), row),
            pl.BlockSpec((1, L, MIX_WIDTH), row),
            pl.BlockSpec((1, L, MIX_WIDTH), row),
            pl.BlockSpec((1, L, LANES), row),
            pl.BlockSpec((CONV_WIDTH, 2 * MIX_WIDTH), const),
            pl.BlockSpec((1, LANES), const),
            pl.BlockSpec((1, MIX_WIDTH), const),
        ],
        out_specs=pl.BlockSpec((1, L, MIX_WIDTH), row),
        out_shape=jax.ShapeDtypeStruct((B, S, MIX_WIDTH), BF16),
        scratch_shapes=[
            pltpu.VMEM((L, 2 * MIX_WIDTH), F32),
            pltpu.VMEM((MLSTM_HEADS, MLSTM_HEAD_DIM, 2 * LANES), F32),
            pltpu.VMEM((8, LANES), F32),
        ],
        compiler_params=_params(("parallel", "arbitrary")),
        name="mlstm",
    )(qk_raw, vc, o_pre, misc, conv_w, gb, head_norm.reshape(1, MIX_WIDTH))


def _even_layer(h, norm_mix, w_in, pool_w, pool_scale, w_out, norm_ffn, wg, wu, wd):
    B, S, D = h.shape
    kb, ub, kmean, qaT, qT, vT = _even_in(h, norm_mix, w_in)
    nb = S // MOBA_BLOCK
    kmean = kmean.reshape(B, nb, MOBA_HEADS, MOBA_HEAD_DIM).transpose(0, 2, 1, 3)
    slopes = jnp.asarray(2.0 ** (-8.0 * np.arange(1, MOBA_HEADS + 1) / MOBA_HEADS), dtype=F32)
    mrow = _moba_gate(slopes, kmean, qT)
    aT = _attention(qaT, kb, vT, MOBA_HEAD_DIM, MOBA_HEADS, 2, slopes=slopes, mrow=mrow)
    b_out = _pool(ub, pool_w, pool_scale)
    h = _mix_out(h, aT, b_out, w_out, a_first=True)
    h2d = _ffn(h.reshape(B * S, D), norm_ffn, wg[None].astype(BF16), wu[None].astype(BF16),
               wd[None].astype(BF16), tf=FFN_DIM // 2)
    return h2d


def _odd_layer(h, norm_mix, w_in, conv_w, b_i, b_f, head_norm, q_norm, w_uq, kv_norm, w_ukv,
               w_out, norm_ffn, router_w, router_b, wg, wu, wd):
    B, S, D = h.shape
    qk_raw, vc, o_pre, misc, mqT, mk, mvT = _odd_in(h, norm_mix, w_in, q_norm, w_uq, kv_norm, w_ukv)
    c_out = _mlstm(qk_raw, vc, o_pre, misc, conv_w, b_i, b_f, head_norm)
    dT = _attention(mqT, mk, mvT, MLA_V_DIM, MLA_HEADS, 1)
    h = _mix_out(h, dT, c_out, w_out, a_first=False)
    h2d = h.reshape(B * S, D)
    gates = _router(h2d, norm_ffn, router_w, router_b)
    return _ffn(h2d, norm_ffn, wg.astype(BF16), wu.astype(BF16), wd.astype(BF16), gates=gates,
                tm=2 * ROW_TILE, tf=EXPERT_DIM // 4)


def kernel(x, p, ev_norm_mix, ev_w_in, pool_w, pool_scale, ev_w_out, ev_norm_ffn, ffn_w_gate, ffn_w_up, ffn_w_down, od_norm_mix, od_w_in, conv_w, gate_b_i, gate_b_f, mlstm_norm, mla_q_norm, mla_w_uq, mla_kv_norm, mla_w_ukv, od_w_out, od_norm_ffn, router_w, router_b, moe_w_gate, moe_w_up, moe_w_down, ple_norm, ple_w_gate, ple_w_proj, final_norm):
    B, S, D = x.shape
    depth = p.shape[0]
    assert D == D_MODEL and S % (2 * ROW_TILE) == 0 and S // MOBA_BLOCK >= MOBA_TOPK
    h = x
    for layer in range(depth):
        j = layer // 2
        if layer % 2 == 0:
            h2d = _even_layer(h, ev_norm_mix[j], ev_w_in[j], pool_w[j], pool_scale[j], ev_w_out[j],
                              ev_norm_ffn[j], ffn_w_gate[j], ffn_w_up[j], ffn_w_down[j])
        else:
            h2d = _odd_layer(h, od_norm_mix[j], od_w_in[j], conv_w[j], gate_b_i[j], gate_b_f[j],
                             mlstm_norm[j], mla_q_norm[j], mla_w_uq[j], mla_kv_norm[j], mla_w_ukv[j],
                             od_w_out[j], od_norm_ffn[j], router_w[j], router_b[j],
                             moe_w_gate[j], moe_w_up[j], moe_w_down[j])
        last = layer == depth - 1
        h2d = _ple(h2d, p[layer].reshape(B * S, PLE_DIM), ple_norm[layer], ple_w_gate[layer],
                   ple_w_proj[layer], final_g=final_norm if last else None)
        h = h2d.reshape(B, S, D)
    return h
```

```python
import functools
import math

import numpy as np
import jax
import jax.numpy as jnp
from jax import lax
from jax.experimental import pallas as pl
from jax.experimental.pallas import tpu as pltpu

F32 = jnp.float32
BF16 = jnp.bfloat16
HIGHEST = lax.Precision.HIGHEST

D_MODEL = 1024
PLE_DIM = 256
NORM_EPS = 1e-6
NEG_INF = -1e30

MOBA_HEADS = 8
MOBA_HEAD_DIM = 64
MOBA_BLOCK = 256
MOBA_TOPK = 3
POOL_WINDOWS = (2, 4, 8, 16)
POOL_GROUP_DIM = 128
POOL_HALO = 16
MLSTM_HEADS = 4
MLSTM_HEAD_DIM = 128
MLSTM_CHUNK = 128
CONV_WIDTH = 4
MLA_HEADS = 4
MLA_Q_RANK = 256
MLA_KV_RANK = 128
MLA_NOPE_DIM = 64
MLA_ROPE_DIM = 32
MLA_V_DIM = 128
ROPE_BASE = 10000.0
FFN_DIM = 2816
N_EXPERTS = 8
EXPERT_DIM = 3584
MIX_WIDTH = 512

ATTN_BLOCK = 512
ROW_TILE = 512
LANES = 128
VMEM_LIMIT = 56 * 1024 * 1024

MISC_ROPE = 0
MISC_I = 32
MISC_F = 36


def _params(sem, vmem=VMEM_LIMIT):
    return pltpu.CompilerParams(dimension_semantics=sem, vmem_limit_bytes=vmem)


def _rms(x, g):
    ms = jnp.mean(x * x, axis=-1, keepdims=True)
    return x * lax.rsqrt(ms + NORM_EPS) * g


def _sigmoid(x):
    return 1.0 / (1.0 + jnp.exp(-x))


def _dot(a, b):
    return jnp.dot(a, b, preferred_element_type=F32)


def _dot_nt(a, b, precision=None):
    return lax.dot_general(a, b, (((1,), (1,)), ((), ())), precision=precision,
                           preferred_element_type=F32)


def _even_in_kernel(x_ref, g_ref, wn_ref, wqT_ref, wvT_ref,
                    kb_ref, ub_ref, km_ref, qa_ref, qT_ref, vT_ref):
    tm = x_ref.shape[1]
    xn = _rms(x_ref[0], g_ref[...]).astype(BF16)
    n = _dot(xn, wn_ref[...])
    k = n[:, :MIX_WIDTH]
    kb_ref[0] = k.astype(BF16)
    ub_ref[0] = n[:, MIX_WIDTH:]
    for j in range(tm // MOBA_BLOCK):
        km_ref[0, j] = jnp.mean(k[j * MOBA_BLOCK:(j + 1) * MOBA_BLOCK], axis=0, keepdims=True)
    qT = _dot_nt(wqT_ref[...], xn)
    qT_ref[0] = qT
    scale = MOBA_HEAD_DIM ** -0.5
    zeros = jnp.zeros((MOBA_HEAD_DIM, tm), BF16)
    for h in range(MOBA_HEADS):
        q_h = (qT[h * MOBA_HEAD_DIM:(h + 1) * MOBA_HEAD_DIM] * scale).astype(BF16)
        base = h * LANES
        if h % 2 == 0:
            qa_ref[0, base:base + MOBA_HEAD_DIM] = q_h
            qa_ref[0, base + MOBA_HEAD_DIM:base + LANES] = zeros
        else:
            qa_ref[0, base:base + MOBA_HEAD_DIM] = zeros
            qa_ref[0, base + MOBA_HEAD_DIM:base + LANES] = q_h
    vT = _dot_nt(wvT_ref[...], xn)
    for j in range(tm // ATTN_BLOCK):
        vT_ref[0, j] = vT[:, j * ATTN_BLOCK:(j + 1) * ATTN_BLOCK].astype(BF16)


def _even_in(x, g, w_in):
    B, S, D = x.shape
    tm = ROW_TILE
    nb = S // MOBA_BLOCK
    wq, wk, wv, wu = (w_in[:, i * MIX_WIDTH:(i + 1) * MIX_WIDTH] for i in range(4))
    wn = jnp.concatenate([wk, wu], axis=1).astype(BF16)
    wqT = wq.T.astype(BF16)
    wvT = wv.T.astype(BF16)
    const = lambda b, i: (0, 0)
    return pl.pallas_call(
        _even_in_kernel,
        grid=(B, S // tm),
        in_specs=[
            pl.BlockSpec((1, tm, D), lambda b, i: (b, i, 0)),
            pl.BlockSpec((1, D), const),
            pl.BlockSpec((D, 2 * MIX_WIDTH), const),
            pl.BlockSpec((MIX_WIDTH, D), const),
            pl.BlockSpec((MIX_WIDTH, D), const),
        ],
        out_specs=[
            pl.BlockSpec((1, tm, MIX_WIDTH), lambda b, i: (b, i, 0)),
            pl.BlockSpec((1, tm, MIX_WIDTH), lambda b, i: (b, i, 0)),
            pl.BlockSpec((1, tm // MOBA_BLOCK, 1, MIX_WIDTH), lambda b, i: (b, i, 0, 0)),
            pl.BlockSpec((1, MOBA_HEADS * LANES, tm), lambda b, i: (b, 0, i)),
            pl.BlockSpec((1, MIX_WIDTH, tm), lambda b, i: (b, 0, i)),
            pl.BlockSpec((1, tm // ATTN_BLOCK, MIX_WIDTH, ATTN_BLOCK), lambda b, i: (b, i, 0, 0)),
        ],
        out_shape=[
            jax.ShapeDtypeStruct((B, S, MIX_WIDTH), BF16),
            jax.ShapeDtypeStruct((B, S, MIX_WIDTH), F32),
            jax.ShapeDtypeStruct((B, nb, 1, MIX_WIDTH), F32),
            jax.ShapeDtypeStruct((B, MOBA_HEADS * LANES, S), BF16),
            jax.ShapeDtypeStruct((B, MIX_WIDTH, S), F32),
            jax.ShapeDtypeStruct((B, S // ATTN_BLOCK, MIX_WIDTH, ATTN_BLOCK), BF16),
        ],
        compiler_params=_params(("parallel", "parallel")),
        name="even_in",
    )(x, g.reshape(1, D), wn, wqT, wvT)


def _moba_gate_kernel(km_ref, qT_ref, m_ref):
    i = pl.program_id(1)
    nb = km_ref.shape[2]
    tq = qT_ref.shape[2]
    row = lax.broadcasted_iota(jnp.int32, (nb, tq), 0)
    own = (i * tq + lax.broadcasted_iota(jnp.int32, (nb, tq), 1)) // MOBA_BLOCK
    past = row < own
    for h in range(MOBA_HEADS):
        q_h = qT_ref[0, h * MOBA_HEAD_DIM:(h + 1) * MOBA_HEAD_DIM, :]
        gate = jnp.dot(km_ref[0, h], q_h, precision=HIGHEST, preferred_element_type=F32)
        gate = jnp.where(past, gate, NEG_INF)
        chosen = jnp.zeros(gate.shape, F32)
        for _ in range(MOBA_TOPK):
            mx = jnp.max(gate, axis=0, keepdims=True)
            first = jnp.min(jnp.where(gate == mx, row, nb), axis=0, keepdims=True)
            pick = row == first
            chosen = jnp.where(pick, 1.0, chosen)
            gate = jnp.where(pick, -jnp.inf, gate)
        keep = jnp.where(past, chosen, jnp.where(row == own, 1.0, 0.0))
        m_ref[0, h] = jnp.where(keep > 0.0, 0.0, NEG_INF)


def _moba_gate(kmean, qT):
    B, H, nb, dh = kmean.shape
    S = qT.shape[2]
    tq = ATTN_BLOCK
    return pl.pallas_call(
        _moba_gate_kernel,
        grid=(B, S // tq),
        in_specs=[
            pl.BlockSpec((1, H, nb, dh), lambda b, i: (b, 0, 0, 0)),
            pl.BlockSpec((1, H * dh, tq), lambda b, i: (b, 0, i)),
        ],
        out_specs=pl.BlockSpec((1, H, nb, tq), lambda b, i: (b, 0, 0, i)),
        out_shape=jax.ShapeDtypeStruct((B, H, nb, S), F32),
        compiler_params=_params(("parallel", "parallel")),
        name="moba_gate",
    )(kmean, qT)


HEADS_PER_STEP = 2


def _attn_kernel(*refs, moba):
    if moba:
        sl_ref, q_ref, mrow_ref, k_ref, v_ref, o_ref, m_ref, l_ref, acc_ref, bias_ref = refs
    else:
        q_ref, k_ref, v_ref, o_ref, m_ref, l_ref, acc_ref = refs
    pair = pl.program_id(1)
    i = pl.program_id(2)
    tq = q_ref.shape[2]
    tk = ATTN_BLOCK
    dv = v_ref.shape[2] // HEADS_PER_STEP
    key = lax.broadcasted_iota(jnp.int32, (tk, tq), 0)
    qry = lax.broadcasted_iota(jnp.int32, (tk, tq), 1)
    if moba:
        @pl.when(i == 0)
        def _():
            rel = (key - qry).astype(F32)
            for g in range(HEADS_PER_STEP):
                bias_ref[g] = sl_ref[pair * HEADS_PER_STEP + g] * rel
    m_ref[...] = jnp.full(m_ref.shape, NEG_INF, F32)
    l_ref[...] = jnp.zeros(l_ref.shape, F32)
    acc_ref[...] = jnp.zeros(acc_ref.shape, F32)

    def step(kvt, diag):
        start = pl.multiple_of(kvt * tk, tk)
        k_tile = k_ref[0, pl.ds(start, tk), :]
        v_tile = v_ref[0, kvt]
        for g in range(HEADS_PER_STEP):
            q = q_ref[0, g * LANES:(g + 1) * LANES, :]
            if moba:
                s = _dot(k_tile, q) + bias_ref[g]
                origin = sl_ref[pair * HEADS_PER_STEP + g] * (tk * (kvt - i)).astype(F32)
                rows = []
                for r in range(tk // MOBA_BLOCK):
                    blk = kvt * (tk // MOBA_BLOCK) + r
                    rows.append(s[r * MOBA_BLOCK:(r + 1) * MOBA_BLOCK]
                                + (mrow_ref[0, g, pl.ds(blk, 1), :] + origin))
                s = jnp.concatenate(rows, axis=0)
            else:
                s = _dot(k_tile[:, g * LANES:(g + 1) * LANES], q)
            if diag:
                s = jnp.where(key <= qry, s, NEG_INF)
            m_run = m_ref[g]
            m_new = jnp.maximum(m_run, jnp.max(s, axis=0, keepdims=True))
            alpha = jnp.exp(m_run - m_new)
            p = jnp.exp(s - m_new)
            l_ref[g] = alpha * l_ref[g] + jnp.sum(p, axis=0, keepdims=True)
            acc_ref[g] = alpha * acc_ref[g] + _dot(v_tile[g * dv:(g + 1) * dv], p.astype(BF16))
            m_ref[g] = m_new

    step(i, True)

    def body(kvt, carry):
        step(kvt, False)
        return carry

    lax.fori_loop(0, i, body, 0)
    for g in range(HEADS_PER_STEP):
        o_ref[0, g * dv:(g + 1) * dv, :] = acc_ref[g] / l_ref[g]


def _attention(qT, k, vT, dv, heads, slopes=None, mrow=None):
    B, _, S = qT.shape
    tq = ATTN_BLOCK
    nt = S // ATTN_BLOCK
    moba = slopes is not None
    hp = HEADS_PER_STEP
    k_lanes = LANES if moba else hp * LANES
    q_spec = pl.BlockSpec((1, hp * LANES, tq), lambda b, h, i: (b, h, i))
    k_spec = pl.BlockSpec((1, S, k_lanes), lambda b, h, i: (b, 0, h))
    v_spec = pl.BlockSpec((1, nt, hp * dv, ATTN_BLOCK), lambda b, h, i: (b, 0, h, 0))
    scratch = [pltpu.VMEM((hp, 1, tq), F32), pltpu.VMEM((hp, 1, tq), F32), pltpu.VMEM((hp, dv, tq), F32)]
    if moba:
        nb = S // MOBA_BLOCK
        in_specs = [pl.BlockSpec(memory_space=pltpu.SMEM), q_spec,
                    pl.BlockSpec((1, hp, nb, tq), lambda b, h, i: (b, h, 0, i)), k_spec, v_spec]
        args = [slopes, qT, mrow, k, vT]
        scratch.append(pltpu.VMEM((hp, ATTN_BLOCK, tq), F32))
    else:
        in_specs = [q_spec, k_spec, v_spec]
        args = [qT, k, vT]
    return pl.pallas_call(
        functools.partial(_attn_kernel, moba=moba),
        grid=(B, heads // hp, S // tq),
        in_specs=in_specs,
        out_specs=pl.BlockSpec((1, hp * dv, tq), lambda b, h, i: (b, h, i)),
        out_shape=jax.ShapeDtypeStruct((B, heads * dv, S), F32),
        scratch_shapes=scratch,
        compiler_params=_params(("parallel", "parallel", "arbitrary")),
        name="moba_attn" if moba else "mla_attn",
    )(*args)


def _pool_kernel(x_ref, halo_ref, w_ref, sc_ref, o_ref, xs_ref):
    i = pl.program_id(1)
    tm = x_ref.shape[1]
    x = x_ref[0]
    xs_ref[0:POOL_HALO] = jnp.where(i > 0, halo_ref[0], 0.0)
    xs_ref[POOL_HALO:POOL_HALO + tm] = x
    t = i * tm + lax.broadcasted_iota(jnp.int32, (tm, 1), 0)
    outs = []
    for g, win in enumerate(POOL_WINDOWS):
        lo, hi = g * POOL_GROUP_DIM, (g + 1) * POOL_GROUP_DIM
        xg = x[:, lo:hi]
        acc = xg
        for d in range(1, win):
            acc = acc + xs_ref[POOL_HALO - d:POOL_HALO - d + tm, lo:hi]
        count = jnp.minimum(t + 1, win).astype(F32)
        outs.append(_dot((acc / count - xg).astype(BF16), w_ref[g]))
    o_ref[0] = (jnp.concatenate(outs, axis=1) * sc_ref[...]).astype(o_ref.dtype)


def _pool(ub, pool_w, pool_scale):
    B, S, W = ub.shape
    tm = ROW_TILE
    per = tm // POOL_HALO
    return pl.pallas_call(
        _pool_kernel,
        grid=(B, S // tm),
        in_specs=[
            pl.BlockSpec((1, tm, W), lambda b, i: (b, i, 0)),
            pl.BlockSpec((1, POOL_HALO, W), lambda b, i: (b, jnp.maximum(i * per - 1, 0), 0)),
            pl.BlockSpec(pool_w.shape, lambda b, i: (0, 0, 0)),
            pl.BlockSpec((1, W), lambda b, i: (0, 0)),
        ],
        out_specs=pl.BlockSpec((1, tm, W), lambda b, i: (b, i, 0)),
        out_shape=jax.ShapeDtypeStruct((B, S, W), BF16),
        scratch_shapes=[pltpu.VMEM((POOL_HALO + tm, W), F32)],
        compiler_params=_params(("parallel", "parallel")),
        name="pool",
    )(ub, ub, pool_w.astype(BF16), pool_scale.reshape(1, W))


def _mix_out_kernel(h_ref, aT_ref, b_ref, w_ref, o_ref, *, a_first):
    a = aT_ref[0].T.astype(BF16)
    b = b_ref[0].astype(BF16)
    lo, hi = (a, b) if a_first else (b, a)
    y = _dot(lo, w_ref[:MIX_WIDTH]) + _dot(hi, w_ref[MIX_WIDTH:])
    o_ref[0] = h_ref[0] + y


def _mix_out(h, aT, b, w_out, a_first):
    B, S, D = h.shape
    tm = ROW_TILE
    return pl.pallas_call(
        functools.partial(_mix_out_kernel, a_first=a_first),
        grid=(B, S // tm),
        in_specs=[
            pl.BlockSpec((1, tm, D), lambda b_, i: (b_, i, 0)),
            pl.BlockSpec((1, MIX_WIDTH, tm), lambda b_, i: (b_, 0, i)),
            pl.BlockSpec((1, tm, MIX_WIDTH), lambda b_, i: (b_, i, 0)),
            pl.BlockSpec((2 * MIX_WIDTH, D), lambda b_, i: (0, 0)),
        ],
        out_specs=pl.BlockSpec((1, tm, D), lambda b_, i: (b_, i, 0)),
        out_shape=jax.ShapeDtypeStruct((B, S, D), F32),
        compiler_params=_params(("parallel", "parallel")),
        name="mix_out",
    )(h, aT, b, w_out.astype(BF16))


def _ffn_kernel(*refs, gated):
    if gated:
        h_ref, g_ref, gate_ref, wg_ref, wu_ref, wd_ref, o_ref, xn_ref, acc_ref = refs
    else:
        h_ref, g_ref, wg_ref, wu_ref, wd_ref, o_ref, xn_ref, acc_ref = refs
    e = pl.program_id(1)
    f = pl.program_id(2)

    @pl.when((e == 0) & (f == 0))
    def _():
        xn_ref[...] = _rms(h_ref[...], g_ref[...]).astype(BF16)
        acc_ref[...] = jnp.zeros_like(acc_ref)

    xn = xn_ref[...]
    gt = _dot(xn, wg_ref[0])
    up = _dot(xn, wu_ref[0])
    hid = gt * _sigmoid(gt) * up
    if gated:
        lane = lax.broadcasted_iota(jnp.int32, gate_ref.shape, 1)
        hid = hid * jnp.sum(jnp.where(lane == e, gate_ref[...], 0.0), axis=1, keepdims=True)
    acc_ref[...] += _dot(hid.astype(BF16), wd_ref[0])

    @pl.when((e == pl.num_programs(1) - 1) & (f == pl.num_programs(2) - 1))
    def _():
        o_ref[...] = h_ref[...] + acc_ref[...]


def _ffn(h2d, g, wg, wu, wd, gates=None, tm=ROW_TILE, tf=None):
    T, D = h2d.shape
    E, _, F = wg.shape
    gated = gates is not None
    in_specs = [
        pl.BlockSpec((tm, D), lambda i, e, f: (i, 0)),
        pl.BlockSpec((1, D), lambda i, e, f: (0, 0)),
    ]
    args = [h2d, g.reshape(1, D)]
    if gated:
        in_specs.append(pl.BlockSpec((tm, LANES), lambda i, e, f: (i, 0)))
        args.append(gates)
    in_specs += [
        pl.BlockSpec((1, D, tf), lambda i, e, f: (e, 0, f)),
        pl.BlockSpec((1, D, tf), lambda i, e, f: (e, 0, f)),
        pl.BlockSpec((1, tf, D), lambda i, e, f: (e, f, 0)),
    ]
    args += [wg, wu, wd]
    return pl.pallas_call(
        functools.partial(_ffn_kernel, gated=gated),
        grid=(T // tm, E, F // tf),
        in_specs=in_specs,
        out_specs=pl.BlockSpec((tm, D), lambda i, e, f: (i, 0)),
        out_shape=jax.ShapeDtypeStruct((T, D), F32),
        scratch_shapes=[pltpu.VMEM((tm, D), BF16), pltpu.VMEM((tm, D), F32)],
        compiler_params=_params(("parallel", "arbitrary", "arbitrary")),
        name="moe_ffn" if gated else "ffn",
    )(*args)


def _router_kernel(h_ref, g_ref, w_ref, b_ref, o_ref):
    xn = _rms(h_ref[...], g_ref[...])
    logits = jnp.dot(xn, w_ref[...], precision=HIGHEST, preferred_element_type=F32) + b_ref[...]
    lane = lax.broadcasted_iota(jnp.int32, logits.shape, 1)
    logits = jnp.where(lane < N_EXPERTS, logits, -jnp.inf)
    v0 = jnp.max(logits, axis=1, keepdims=True)
    i0 = jnp.min(jnp.where(logits == v0, lane, LANES), axis=1, keepdims=True)
    rest = jnp.where(lane == i0, -jnp.inf, logits)
    v1 = jnp.max(rest, axis=1, keepdims=True)
    i1 = jnp.min(jnp.where(rest == v1, lane, LANES), axis=1, keepdims=True)
    e1 = jnp.exp(v1 - v0)
    w0 = 1.0 / (1.0 + e1)
    o_ref[...] = jnp.where(lane == i0, w0, jnp.where(lane == i1, e1 * w0, 0.0))


def _router(h2d, g, router_w, router_b):
    T, D = h2d.shape
    tm = ROW_TILE
    w = jnp.zeros((D, LANES), F32).at[:, :N_EXPERTS].set(router_w)
    b = jnp.zeros((1, LANES), F32).at[0, :N_EXPERTS].set(router_b)
    return pl.pallas_call(
        _router_kernel,
        grid=(T // tm,),
        in_specs=[
            pl.BlockSpec((tm, D), lambda i: (i, 0)),
            pl.BlockSpec((1, D), lambda i: (0, 0)),
            pl.BlockSpec((D, LANES), lambda i: (0, 0)),
            pl.BlockSpec((1, LANES), lambda i: (0, 0)),
        ],
        out_specs=pl.BlockSpec((tm, LANES), lambda i: (i, 0)),
        out_shape=jax.ShapeDtypeStruct((T, LANES), F32),
        compiler_params=_params(("parallel",)),
        name="router",
    )(h2d, g.reshape(1, D), w, b)


def _ple_kernel(*refs, final):
    if final:
        h_ref, p_ref, g_ref, wg_ref, wp_ref, fg_ref, o_ref = refs
    else:
        h_ref, p_ref, g_ref, wg_ref, wp_ref, o_ref = refs
    h = h_ref[...]
    gate = _sigmoid(_dot(_rms(h, g_ref[...]).astype(BF16), wg_ref[...]))
    out = h + gate * _dot(p_ref[...].astype(BF16), wp_ref[...])
    if final:
        out = _rms(out, fg_ref[...])
    o_ref[...] = out


def _ple(h2d, p2d, g, w_gate, w_proj, final_g=None):
    T, D = h2d.shape
    tm = ROW_TILE
    final = final_g is not None
    in_specs = [
        pl.BlockSpec((tm, D), lambda i: (i, 0)),
        pl.BlockSpec((tm, PLE_DIM), lambda i: (i, 0)),
        pl.BlockSpec((1, D), lambda i: (0, 0)),
        pl.BlockSpec((D, D), lambda i: (0, 0)),
        pl.BlockSpec((PLE_DIM, D), lambda i: (0, 0)),
    ]
    args = [h2d, p2d, g.reshape(1, D), w_gate.astype(BF16), w_proj.astype(BF16)]
    if final:
        in_specs.append(pl.BlockSpec((1, D), lambda i: (0, 0)))
        args.append(final_g.reshape(1, D))
    return pl.pallas_call(
        functools.partial(_ple_kernel, final=final),
        grid=(T // tm,),
        in_specs=in_specs,
        out_specs=pl.BlockSpec((tm, D), lambda i: (i, 0)),
        out_shape=jax.ShapeDtypeStruct((T, D), F32),
        compiler_params=_params(("parallel",)),
        name="ple_final" if final else "ple",
    )(*args)


ODD_MAIN = 4 * MIX_WIDTH
ODD_COLS = ODD_MAIN + MLA_Q_RANK + MLA_KV_RANK + 2 * LANES
MLA_QK_SCALE = (MLA_NOPE_DIM + MLA_ROPE_DIM) ** -0.5
ROPE_HALF = MLA_ROPE_DIM // 2


def _odd_in_kernel(x_ref, g_ref, wn_ref, qn_ref, wuqT_ref, kvn_ref, wk2_ref, wvT_ref,
                   cosT_ref, sinT_ref, cc_ref, ss_ref,
                   qk_ref, vc_ref, op_ref, misc_ref, mq_ref, mk_ref, mv_ref):
    tm = x_ref.shape[1]
    xn = _rms(x_ref[0], g_ref[...]).astype(BF16)
    u = _dot(xn, wn_ref[...])
    qk_ref[0] = u[:, :2 * MIX_WIDTH]
    vc_ref[0] = u[:, 2 * MIX_WIDTH:3 * MIX_WIDTH].astype(BF16)
    op_ref[0] = u[:, 3 * MIX_WIDTH:ODD_MAIN]
    c0 = ODD_MAIN
    c_q = u[:, c0:c0 + MLA_Q_RANK]
    c0 += MLA_Q_RANK
    c_kv = u[:, c0:c0 + MLA_KV_RANK]
    c0 += MLA_KV_RANK
    misc = u[:, c0:c0 + LANES]
    misc_sw = u[:, c0 + LANES:c0 + 2 * LANES]
    misc_ref[0] = misc
    cqn = _rms(c_q, qn_ref[...]).astype(BF16)
    qT = _dot_nt(wuqT_ref[...], cqn)
    cosT = cosT_ref[...]
    sinT = sinT_ref[...]
    for h in range(MLA_HEADS):
        r = h * LANES
        mq_ref[0, r:r + MLA_NOPE_DIM] = (qT[r:r + MLA_NOPE_DIM] * MLA_QK_SCALE).astype(BF16)
        x1 = qT[r + MLA_NOPE_DIM:r + MLA_NOPE_DIM + ROPE_HALF]
        x2 = qT[r + MLA_NOPE_DIM + ROPE_HALF:r + MLA_NOPE_DIM + MLA_ROPE_DIM]
        mq_ref[0, r + MLA_NOPE_DIM:r + MLA_NOPE_DIM + ROPE_HALF] = (
            (x1 * cosT - x2 * sinT) * MLA_QK_SCALE).astype(BF16)
        mq_ref[0, r + MLA_NOPE_DIM + ROPE_HALF:r + MLA_NOPE_DIM + MLA_ROPE_DIM] = (
            (x1 * sinT + x2 * cosT) * MLA_QK_SCALE).astype(BF16)
        mq_ref[0, r + MLA_NOPE_DIM + MLA_ROPE_DIM:r + LANES] = jnp.zeros(
            (LANES - MLA_NOPE_DIM - MLA_ROPE_DIM, tm), BF16)
    ckvn = _rms(c_kv, kvn_ref[...]).astype(BF16)
    k_rot = (misc * cc_ref[...] + misc_sw * ss_ref[...]).astype(BF16)
    mk_ref[0] = _dot(jnp.concatenate([ckvn, k_rot], axis=1), wk2_ref[...]).astype(BF16)
    vT = _dot_nt(wvT_ref[...], ckvn)
    for j in range(tm // ATTN_BLOCK):
        mv_ref[0, j] = vT[:, j * ATTN_BLOCK:(j + 1) * ATTN_BLOCK].astype(BF16)


def _rope_tables(S):
    inv_freq = ROPE_BASE ** (-jnp.arange(ROPE_HALF, dtype=F32) / ROPE_HALF)
    ang = jnp.arange(S, dtype=F32)[:, None] * inv_freq[None, :]
    cos, sin = jnp.cos(ang), jnp.sin(ang)
    pad = jnp.zeros((S, LANES - MLA_ROPE_DIM), F32)
    cc = jnp.concatenate([cos, cos, pad], axis=1)
    ss = jnp.concatenate([-sin, sin, pad], axis=1)
    return cos.T, sin.T, cc, ss


def _odd_in(x, g, w_in, q_norm, w_uq, kv_norm, w_ukv):
    B, S, D = x.shape
    tm = ROW_TILE
    cuts = np.cumsum([MIX_WIDTH] * 4 + [MLSTM_HEADS, MLSTM_HEADS, MLA_Q_RANK, MLA_KV_RANK]).tolist()
    w_main = w_in[:, :cuts[3]]
    w_i = w_in[:, cuts[3]:cuts[4]]
    w_f = w_in[:, cuts[4]:cuts[5]]
    w_cq = w_in[:, cuts[5]:cuts[6]]
    w_ckv = w_in[:, cuts[6]:cuts[7]]
    w_kr = w_in[:, cuts[7]:]
    w_kr_sw = jnp.concatenate([w_kr[:, ROPE_HALF:], w_kr[:, :ROPE_HALF]], axis=1)
    zpad = lambda n: jnp.zeros((D, n), F32)
    w_misc = jnp.concatenate([w_kr, w_i, w_f, zpad(LANES - MLA_ROPE_DIM - 2 * MLSTM_HEADS)], axis=1)
    w_misc_sw = jnp.concatenate([w_kr_sw, zpad(LANES - MLA_ROPE_DIM)], axis=1)
    wn = jnp.concatenate([w_main, w_cq, w_ckv, w_misc, w_misc_sw], axis=1).astype(BF16)
    qd = MLA_NOPE_DIM + MLA_ROPE_DIM
    w_uq_h = w_uq.reshape(MLA_Q_RANK, MLA_HEADS, qd)
    w_uq_h = jnp.concatenate([w_uq_h, jnp.zeros((MLA_Q_RANK, MLA_HEADS, LANES - qd), F32)], axis=2)
    wuqT = w_uq_h.reshape(MLA_Q_RANK, MLA_HEADS * LANES).T.astype(BF16)
    w_ukv_h = w_ukv.reshape(MLA_KV_RANK, MLA_HEADS, MLA_NOPE_DIM + MLA_V_DIM)
    w_k = jnp.concatenate([w_ukv_h[:, :, :MLA_NOPE_DIM],
                           jnp.zeros((MLA_KV_RANK, MLA_HEADS, LANES - MLA_NOPE_DIM), F32)], axis=2)
    place = jnp.zeros((LANES, MLA_HEADS, LANES), F32)
    eye = jnp.eye(MLA_ROPE_DIM, dtype=F32)
    place = place.at[:MLA_ROPE_DIM, :, MLA_NOPE_DIM:MLA_NOPE_DIM + MLA_ROPE_DIM].set(
        jnp.broadcast_to(eye[:, None, :], (MLA_ROPE_DIM, MLA_HEADS, MLA_ROPE_DIM)))
    wk2 = jnp.concatenate([w_k, place], axis=0).reshape(MLA_KV_RANK + LANES, MLA_HEADS * LANES).astype(BF16)
    wvT = w_ukv_h[:, :, MLA_NOPE_DIM:].reshape(MLA_KV_RANK, MLA_HEADS * MLA_V_DIM).T.astype(BF16)
    cosT, sinT, cc, ss = _rope_tables(S)
    row = lambda b, i: (b, i, 0)
    const = lambda b, i: (0, 0)
    nb = S // ATTN_BLOCK
    return pl.pallas_call(
        _odd_in_kernel,
        grid=(B, S // tm),
        in_specs=[
            pl.BlockSpec((1, tm, D), row),
            pl.BlockSpec((1, D), const),
            pl.BlockSpec((D, ODD_COLS), const),
            pl.BlockSpec((1, MLA_Q_RANK), const),
            pl.BlockSpec((MLA_HEADS * LANES, MLA_Q_RANK), const),
            pl.BlockSpec((1, MLA_KV_RANK), const),
            pl.BlockSpec((MLA_KV_RANK + LANES, MLA_HEADS * LANES), const),
            pl.BlockSpec((MLA_HEADS * MLA_V_DIM, MLA_KV_RANK), const),
            pl.BlockSpec((ROPE_HALF, tm), lambda b, i: (0, i)),
            pl.BlockSpec((ROPE_HALF, tm), lambda b, i: (0, i)),
            pl.BlockSpec((tm, LANES), lambda b, i: (i, 0)),
            pl.BlockSpec((tm, LANES), lambda b, i: (i, 0)),
        ],
        out_specs=[
            pl.BlockSpec((1, tm, 2 * MIX_WIDTH), row),
            pl.BlockSpec((1, tm, MIX_WIDTH), row),
            pl.BlockSpec((1, tm, MIX_WIDTH), row),
            pl.BlockSpec((1, tm, LANES), row),
            pl.BlockSpec((1, MLA_HEADS * LANES, tm), lambda b, i: (b, 0, i)),
            pl.BlockSpec((1, tm, MLA_HEADS * LANES), row),
            pl.BlockSpec((1, tm // ATTN_BLOCK, MLA_HEADS * MLA_V_DIM, ATTN_BLOCK), lambda b, i: (b, i, 0, 0)),
        ],
        out_shape=[
            jax.ShapeDtypeStruct((B, S, 2 * MIX_WIDTH), F32),
            jax.ShapeDtypeStruct((B, S, MIX_WIDTH), BF16),
            jax.ShapeDtypeStruct((B, S, MIX_WIDTH), F32),
            jax.ShapeDtypeStruct((B, S, LANES), F32),
            jax.ShapeDtypeStruct((B, MLA_HEADS * LANES, S), BF16),
            jax.ShapeDtypeStruct((B, S, MLA_HEADS * LANES), BF16),
            jax.ShapeDtypeStruct((B, nb, MLA_HEADS * MLA_V_DIM, ATTN_BLOCK), BF16),
        ],
        compiler_params=_params(("parallel", "parallel")),
        name="odd_in",
    )(x, g.reshape(1, D), wn, q_norm.reshape(1, -1), wuqT, kv_norm.reshape(1, -1), wk2, wvT,
      cosT, sinT, cc, ss)


def _log_sigmoid(x):
    return jnp.minimum(x, 0.0) - jnp.log(1.0 + jnp.exp(-jnp.abs(x)))


def _mlstm_kernel(qk_ref, v_ref, op_ref, misc_ref, cw_ref, gb_ref, hn_ref, o_ref,
                  prev_ref, cn_ref, m_ref):
    c = pl.program_id(1)
    L = qk_ref.shape[1]
    d = MLSTM_HEAD_DIM

    @pl.when(c == 0)
    def _():
        prev_ref[...] = jnp.zeros_like(prev_ref)
        cn_ref[...] = jnp.zeros_like(cn_ref)
        m_ref[...] = jnp.zeros_like(m_ref)

    x = qk_ref[0]
    prev = prev_ref[...]
    row = lax.broadcasted_iota(jnp.int32, (L, 1), 0)
    conv = x * cw_ref[CONV_WIDTH - 1:CONV_WIDTH, :]
    for j in range(1, CONV_WIDTH):
        shifted = pltpu.roll(jnp.where(row >= L - j, prev, x), j, axis=0)
        conv = conv + shifted * cw_ref[CONV_WIDTH - 1 - j:CONV_WIDTH - j, :]
    prev_ref[...] = x
    qk = conv * _sigmoid(conv)

    gates = misc_ref[0] + gb_ref[...]
    lane = lax.broadcasted_iota(jnp.int32, (L, LANES), 1)
    is_f = (lane >= MISC_F) & (lane < MISC_F + MLSTM_HEADS)
    z = jnp.where(is_f, _log_sigmoid(gates), gates)
    tri_r = lax.broadcasted_iota(jnp.int32, (L, L), 0)
    tri_c = lax.broadcasted_iota(jnp.int32, (L, L), 1)
    causal = tri_c <= tri_r
    cum = jnp.dot(causal.astype(F32), z, precision=HIGHEST, preferred_element_type=F32)
    z = jnp.where(is_f, cum, z)
    zT = z.T

    ones_col = (lax.broadcasted_iota(jnp.int32, (L, LANES), 1) == 0).astype(BF16)
    for h in range(MLSTM_HEADS):
        lo, hi = h * d, (h + 1) * d
        q = qk[:, lo:hi].astype(BF16)
        k = qk[:, MIX_WIDTH + lo:MIX_WIDTH + hi] * (d ** -0.5)
        v_aug = jnp.concatenate([v_ref[0, :, lo:hi], ones_col], axis=1)
        i_col = z[:, MISC_I + h:MISC_I + h + 1]
        b_col = z[:, MISC_F + h:MISC_F + h + 1]
        i_row = zT[MISC_I + h:MISC_I + h + 1, :]
        b_row = zT[MISC_F + h:MISC_F + h + 1, :]
        m_prev = m_ref[h:h + 1, 0:1]
        intra = jnp.where(causal, b_col - b_row + i_row, NEG_INF)
        m_inter = b_col + m_prev
        m_t = jnp.maximum(m_inter, jnp.max(intra, axis=1, keepdims=True))
        w_inter = jnp.exp(m_inter - m_t)
        a = jnp.exp(intra - m_t) * _dot_nt(q, k.astype(BF16))
        inter = _dot(q, cn_ref[h].astype(BF16))
        intra_o = _dot(a.astype(BF16), v_aug)
        num = w_inter * inter[:, :d] + intra_o[:, :d]
        den = w_inter * inter[:, d:d + 1] + intra_o[:, d:d + 1]
        hh = num / jnp.maximum(jnp.abs(den), jnp.exp(-m_t))
        hh = _rms(hh, hn_ref[:, lo:hi])
        o_ref[0, :, lo:hi] = (hh * _sigmoid(op_ref[0, :, lo:hi])).astype(o_ref.dtype)
        b_end = b_col[L - 1:L, :]
        g_col = b_end - b_col + i_col
        m_new = jnp.maximum(b_end + m_prev, jnp.max(g_col, axis=0, keepdims=True))
        decay = jnp.exp(b_end + m_prev - m_new)
        kw = k * jnp.exp(g_col - m_new)
        cn_ref[h] = decay * cn_ref[h] + _dot(kw.T.astype(BF16), v_aug)
        m_ref[h:h + 1, :] = jnp.broadcast_to(m_new, (1, LANES))


def _mlstm(qk_raw, vc, o_pre, misc, conv_w, b_i, b_f, head_norm):
    B, S, _ = qk_raw.shape
    L = MLSTM_CHUNK
    gb = jnp.zeros((1, LANES), F32).at[0, MISC_I:MISC_I + MLSTM_HEADS].set(b_i)
    gb = gb.at[0, MISC_F:MISC_F + MLSTM_HEADS].set(b_f)
    row = lambda b, c: (b, c, 0)
    const = lambda b, c: (0, 0)
    return pl.pallas_call(
        _mlstm_kernel,
        grid=(B, S // L),
        in_specs=[
            pl.BlockSpec((1, L, 2 * MIX_WIDTH), row),
            pl.BlockSpec((1, L, MIX_WIDTH), row),
            pl.BlockSpec((1, L, MIX_WIDTH), row),
            pl.BlockSpec((1, L, LANES), row),
            pl.BlockSpec((CONV_WIDTH, 2 * MIX_WIDTH), const),
            pl.BlockSpec((1, LANES), const),
            pl.BlockSpec((1, MIX_WIDTH), const),
        ],
        out_specs=pl.BlockSpec((1, L, MIX_WIDTH), row),
        out_shape=jax.ShapeDtypeStruct((B, S, MIX_WIDTH), BF16),
        scratch_shapes=[
            pltpu.VMEM((L, 2 * MIX_WIDTH), F32),
            pltpu.VMEM((MLSTM_HEADS, MLSTM_HEAD_DIM, 2 * LANES), F32),
            pltpu.VMEM((8, LANES), F32),
        ],
        compiler_params=_params(("parallel", "arbitrary")),
        name="mlstm",
    )(qk_raw, vc, o_pre, misc, conv_w, gb, head_norm.reshape(1, MIX_WIDTH))


def _even_layer(h, norm_mix, w_in, pool_w, pool_scale, w_out, norm_ffn, wg, wu, wd):
    B, S, D = h.shape
    kb, ub, kmean, qaT, qT, vT = _even_in(h, norm_mix, w_in)
    nb = S // MOBA_BLOCK
    kmean = kmean.reshape(B, nb, MOBA_HEADS, MOBA_HEAD_DIM).transpose(0, 2, 1, 3)
    slopes = jnp.asarray(2.0 ** (-8.0 * np.arange(1, MOBA_HEADS + 1) / MOBA_HEADS), dtype=F32)
    mrow = _moba_gate(kmean, qT)
    aT = _attention(qaT, kb, vT, MOBA_HEAD_DIM, MOBA_HEADS, slopes=slopes, mrow=mrow)
    b_out = _pool(ub, pool_w, pool_scale)
    h = _mix_out(h, aT, b_out, w_out, a_first=True)
    h2d = _ffn(h.reshape(B * S, D), norm_ffn, wg[None].astype(BF16), wu[None].astype(BF16),
               wd[None].astype(BF16), tf=FFN_DIM // 2)
    return h2d


def _odd_layer(h, norm_mix, w_in, conv_w, b_i, b_f, head_norm, q_norm, w_uq, kv_norm, w_ukv,
               w_out, norm_ffn, router_w, router_b, wg, wu, wd):
    B, S, D = h.shape
    qk_raw, vc, o_pre, misc, mqT, mk, mvT = _odd_in(h, norm_mix, w_in, q_norm, w_uq, kv_norm, w_ukv)
    c_out = _mlstm(qk_raw, vc, o_pre, misc, conv_w, b_i, b_f, head_norm)
    dT = _attention(mqT, mk, mvT, MLA_V_DIM, MLA_HEADS)
    h = _mix_out(h, dT, c_out, w_out, a_first=False)
    h2d = h.reshape(B * S, D)
    gates = _router(h2d, norm_ffn, router_w, router_b)
    return _ffn(h2d, norm_ffn, wg.astype(BF16), wu.astype(BF16), wd.astype(BF16), gates=gates,
                tm=2 * ROW_TILE, tf=EXPERT_DIM // 4)


def kernel(x, p, ev_norm_mix, ev_w_in, pool_w, pool_scale, ev_w_out, ev_norm_ffn, ffn_w_gate, ffn_w_up, ffn_w_down, od_norm_mix, od_w_in, conv_w, gate_b_i, gate_b_f, mlstm_norm, mla_q_norm, mla_w_uq, mla_kv_norm, mla_w_ukv, od_w_out, od_norm_ffn, router_w, router_b, moe_w_gate, moe_w_up, moe_w_down, ple_norm, ple_w_gate, ple_w_proj, final_norm):
    B, S, D = x.shape
    depth = p.shape[0]
    assert D == D_MODEL and S % (2 * ROW_TILE) == 0 and S // MOBA_BLOCK >= MOBA_TOPK
    h = x
    for layer in range(depth):
        j = layer // 2
        if layer % 2 == 0:
            h2d = _even_layer(h, ev_norm_mix[j], ev_w_in[j], pool_w[j], pool_scale[j], ev_w_out[j],
                              ev_norm_ffn[j], ffn_w_gate[j], ffn_w_up[j], ffn_w_down[j])
        else:
            h2d = _odd_layer(h, od_norm_mix[j], od_w_in[j], conv_w[j], gate_b_i[j], gate_b_f[j],
                             mlstm_norm[j], mla_q_norm[j], mla_w_uq[j], mla_kv_norm[j], mla_w_ukv[j],
                             od_w_out[j], od_norm_ffn[j], router_w[j], router_b[j],
                             moe_w_gate[j], moe_w_up[j], moe_w_down[j])
        last = layer == depth - 1
        h2d = _ple(h2d, p[layer].reshape(B * S, PLE_DIM), ple_norm[layer], ple_w_gate[layer],
                   ple_w_proj[layer], final_g=final_norm if last else None)
        h = h2d.reshape(B, S, D)
    return h
```

```python
import functools
import math

import numpy as np
import jax
import jax.numpy as jnp
from jax import lax
from jax.experimental import pallas as pl
from jax.experimental.pallas import tpu as pltpu

F32 = jnp.float32
BF16 = jnp.bfloat16
HIGHEST = lax.Precision.HIGHEST

D_MODEL = 1024
PLE_DIM = 256
NORM_EPS = 1e-6
NEG_INF = -1e30

MOBA_HEADS = 8
MOBA_HEAD_DIM = 64
MOBA_BLOCK = 256
MOBA_TOPK = 3
POOL_WINDOWS = (2, 4, 8, 16)
POOL_GROUP_DIM = 128
POOL_HALO = 16
MLSTM_HEADS = 4
MLSTM_HEAD_DIM = 128
MLSTM_CHUNK = 128
CONV_WIDTH = 4
MLA_HEADS = 4
MLA_Q_RANK = 256
MLA_KV_RANK = 128
MLA_NOPE_DIM = 64
MLA_ROPE_DIM = 32
MLA_V_DIM = 128
ROPE_BASE = 10000.0
FFN_DIM = 2816
N_EXPERTS = 8
EXPERT_DIM = 3584
MIX_WIDTH = 512

ATTN_BLOCK = 512
ROW_TILE = 512
LANES = 128
VMEM_LIMIT = 56 * 1024 * 1024

MISC_ROPE = 0
MISC_I = 32
MISC_F = 36


def _params(sem, vmem=VMEM_LIMIT):
    return pltpu.CompilerParams(dimension_semantics=sem, vmem_limit_bytes=vmem)


def _rms(x, g):
    ms = jnp.mean(x * x, axis=-1, keepdims=True)
    return x * lax.rsqrt(ms + NORM_EPS) * g


def _sigmoid(x):
    return 1.0 / (1.0 + jnp.exp(-x))


def _dot(a, b):
    return jnp.dot(a, b, preferred_element_type=F32)


def _dot_nt(a, b, precision=None):
    return lax.dot_general(a, b, (((1,), (1,)), ((), ())), precision=precision,
                           preferred_element_type=F32)


def _even_in_kernel(x_ref, g_ref, wn_ref, wqT_ref, wvT_ref,
                    kb_ref, ub_ref, km_ref, qa_ref, qT_ref, vT_ref):
    tm = x_ref.shape[1]
    xn = _rms(x_ref[0], g_ref[...]).astype(BF16)
    n = _dot(xn, wn_ref[...])
    k = n[:, :MIX_WIDTH]
    kb_ref[0] = k.astype(BF16)
    ub_ref[0] = n[:, MIX_WIDTH:]
    for j in range(tm // MOBA_BLOCK):
        km_ref[0, j] = jnp.mean(k[j * MOBA_BLOCK:(j + 1) * MOBA_BLOCK], axis=0, keepdims=True)
    qT = _dot_nt(wqT_ref[...], xn)
    qT_ref[0] = qT
    scale = MOBA_HEAD_DIM ** -0.5
    zeros = jnp.zeros((MOBA_HEAD_DIM, tm), BF16)
    for h in range(MOBA_HEADS):
        q_h = (qT[h * MOBA_HEAD_DIM:(h + 1) * MOBA_HEAD_DIM] * scale).astype(BF16)
        base = h * LANES
        if h % 2 == 0:
            qa_ref[0, base:base + MOBA_HEAD_DIM] = q_h
            qa_ref[0, base + MOBA_HEAD_DIM:base + LANES] = zeros
        else:
            qa_ref[0, base:base + MOBA_HEAD_DIM] = zeros
            qa_ref[0, base + MOBA_HEAD_DIM:base + LANES] = q_h
    vT = _dot_nt(wvT_ref[...], xn)
    for j in range(tm // ATTN_BLOCK):
        vT_ref[0, j] = vT[:, j * ATTN_BLOCK:(j + 1) * ATTN_BLOCK].astype(BF16)


def _even_in(x, g, w_in):
    B, S, D = x.shape
    tm = ROW_TILE
    nb = S // MOBA_BLOCK
    wq, wk, wv, wu = (w_in[:, i * MIX_WIDTH:(i + 1) * MIX_WIDTH] for i in range(4))
    wn = jnp.concatenate([wk, wu], axis=1).astype(BF16)
    wqT = wq.T.astype(BF16)
    wvT = wv.T.astype(BF16)
    const = lambda b, i: (0, 0)
    return pl.pallas_call(
        _even_in_kernel,
        grid=(B, S // tm),
        in_specs=[
            pl.BlockSpec((1, tm, D), lambda b, i: (b, i, 0)),
            pl.BlockSpec((1, D), const),
            pl.BlockSpec((D, 2 * MIX_WIDTH), const),
            pl.BlockSpec((MIX_WIDTH, D), const),
            pl.BlockSpec((MIX_WIDTH, D), const),
        ],
        out_specs=[
            pl.BlockSpec((1, tm, MIX_WIDTH), lambda b, i: (b, i, 0)),
            pl.BlockSpec((1, tm, MIX_WIDTH), lambda b, i: (b, i, 0)),
            pl.BlockSpec((1, tm // MOBA_BLOCK, 1, MIX_WIDTH), lambda b, i: (b, i, 0, 0)),
            pl.BlockSpec((1, MOBA_HEADS * LANES, tm), lambda b, i: (b, 0, i)),
            pl.BlockSpec((1, MIX_WIDTH, tm), lambda b, i: (b, 0, i)),
            pl.BlockSpec((1, tm // ATTN_BLOCK, MIX_WIDTH, ATTN_BLOCK), lambda b, i: (b, i, 0, 0)),
        ],
        out_shape=[
            jax.ShapeDtypeStruct((B, S, MIX_WIDTH), BF16),
            jax.ShapeDtypeStruct((B, S, MIX_WIDTH), F32),
            jax.ShapeDtypeStruct((B, nb, 1, MIX_WIDTH), F32),
            jax.ShapeDtypeStruct((B, MOBA_HEADS * LANES, S), BF16),
            jax.ShapeDtypeStruct((B, MIX_WIDTH, S), F32),
            jax.ShapeDtypeStruct((B, S // ATTN_BLOCK, MIX_WIDTH, ATTN_BLOCK), BF16),
        ],
        compiler_params=_params(("parallel", "parallel")),
        name="even_in",
    )(x, g.reshape(1, D), wn, wqT, wvT)


def _moba_gate_kernel(km_ref, qT_ref, m_ref):
    i = pl.program_id(1)
    nb = km_ref.shape[2]
    tq = qT_ref.shape[2]
    row = lax.broadcasted_iota(jnp.int32, (nb, tq), 0)
    own = (i * tq + lax.broadcasted_iota(jnp.int32, (nb, tq), 1)) // MOBA_BLOCK
    past = row < own
    for h in range(MOBA_HEADS):
        q_h = qT_ref[0, h * MOBA_HEAD_DIM:(h + 1) * MOBA_HEAD_DIM, :]
        gate = jnp.dot(km_ref[0, h], q_h, precision=HIGHEST, preferred_element_type=F32)
        gate = jnp.where(past, gate, NEG_INF)
        chosen = jnp.zeros(gate.shape, F32)
        for _ in range(MOBA_TOPK):
            mx = jnp.max(gate, axis=0, keepdims=True)
            first = jnp.min(jnp.where(gate == mx, row, nb), axis=0, keepdims=True)
            pick = row == first
            chosen = jnp.where(pick, 1.0, chosen)
            gate = jnp.where(pick, -jnp.inf, gate)
        keep = jnp.where(past, chosen, (row == own).astype(F32))
        m_ref[0, h] = jnp.where(keep > 0.0, 0.0, NEG_INF).astype(F32)


def _moba_gate(kmean, qT):
    B, H, nb, dh = kmean.shape
    S = qT.shape[2]
    tq = ATTN_BLOCK
    return pl.pallas_call(
        _moba_gate_kernel,
        grid=(B, S // tq),
        in_specs=[
            pl.BlockSpec((1, H, nb, dh), lambda b, i: (b, 0, 0, 0)),
            pl.BlockSpec((1, H * dh, tq), lambda b, i: (b, 0, i)),
        ],
        out_specs=pl.BlockSpec((1, H, nb, tq), lambda b, i: (b, 0, 0, i)),
        out_shape=jax.ShapeDtypeStruct((B, H, nb, S), F32),
        compiler_params=_params(("parallel", "parallel")),
        name="moba_gate",
    )(kmean, qT)


HEADS_PER_STEP = 2


def _attn_kernel(*refs, moba):
    if moba:
        sl_ref, q_ref, mrow_ref, k_ref, v_ref, o_ref, m_ref, l_ref, acc_ref, bias_ref = refs
    else:
        q_ref, k_ref, v_ref, o_ref, m_ref, l_ref, acc_ref = refs
    pair = pl.program_id(1)
    i = pl.program_id(2)
    tq = q_ref.shape[2]
    tk = ATTN_BLOCK
    dv = v_ref.shape[2] // HEADS_PER_STEP
    key = lax.broadcasted_iota(jnp.int32, (tk, tq), 0)
    qry = lax.broadcasted_iota(jnp.int32, (tk, tq), 1)
    if moba:
        @pl.when(i == 0)
        def _():
            rel = (key - qry).astype(F32)
            for g in range(HEADS_PER_STEP):
                bias_ref[g] = sl_ref[pair * HEADS_PER_STEP + g] * rel
    m_ref[...] = jnp.full(m_ref.shape, NEG_INF, F32)
    l_ref[...] = jnp.zeros(l_ref.shape, F32)
    acc_ref[...] = jnp.zeros(acc_ref.shape, F32)

    def step(kvt, diag):
        start = pl.multiple_of(kvt * tk, tk)
        k_tile = k_ref[0, pl.ds(start, tk), :]
        v_tile = v_ref[0, kvt]
        for g in range(HEADS_PER_STEP):
            q = q_ref[0, g * LANES:(g + 1) * LANES, :]
            if moba:
                s = _dot(k_tile, q) + bias_ref[g]
                origin = sl_ref[pair * HEADS_PER_STEP + g] * (tk * (kvt - i)).astype(F32)
                rows = []
                for r in range(tk // MOBA_BLOCK):
                    blk = kvt * (tk // MOBA_BLOCK) + r
                    rows.append(s[r * MOBA_BLOCK:(r + 1) * MOBA_BLOCK]
                                + (mrow_ref[0, g, pl.ds(blk, 1), :] + origin))
                s = jnp.concatenate(rows, axis=0)
            else:
                s = _dot(k_tile[:, g * LANES:(g + 1) * LANES], q)
            if diag:
                s = jnp.where(key <= qry, s, NEG_INF)
            m_run = m_ref[g]
            m_new = jnp.maximum(m_run, jnp.max(s, axis=0, keepdims=True))
            alpha = jnp.exp(m_run - m_new)
            p = jnp.exp(s - m_new)
            l_ref[g] = alpha * l_ref[g] + jnp.sum(p, axis=0, keepdims=True)
            acc_ref[g] = alpha * acc_ref[g] + _dot(v_tile[g * dv:(g + 1) * dv], p.astype(BF16))
            m_ref[g] = m_new

    step(i, True)

    def body(kvt, carry):
        step(kvt, False)
        return carry

    lax.fori_loop(0, i, body, 0)
    for g in range(HEADS_PER_STEP):
        o_ref[0, g * dv:(g + 1) * dv, :] = acc_ref[g] / l_ref[g]


def _attention(qT, k, vT, dv, heads, slopes=None, mrow=None):
    B, _, S = qT.shape
    tq = ATTN_BLOCK
    nt = S // ATTN_BLOCK
    moba = slopes is not None
    hp = HEADS_PER_STEP
    k_lanes = LANES if moba else hp * LANES
    q_spec = pl.BlockSpec((1, hp * LANES, tq), lambda b, h, i: (b, h, i))
    k_spec = pl.BlockSpec((1, S, k_lanes), lambda b, h, i: (b, 0, h))
    v_spec = pl.BlockSpec((1, nt, hp * dv, ATTN_BLOCK), lambda b, h, i: (b, 0, h, 0))
    scratch = [pltpu.VMEM((hp, 1, tq), F32), pltpu.VMEM((hp, 1, tq), F32), pltpu.VMEM((hp, dv, tq), F32)]
    if moba:
        nb = S // MOBA_BLOCK
        in_specs = [pl.BlockSpec(memory_space=pltpu.SMEM), q_spec,
                    pl.BlockSpec((1, hp, nb, tq), lambda b, h, i: (b, h, 0, i)), k_spec, v_spec]
        args = [slopes, qT, mrow, k, vT]
        scratch.append(pltpu.VMEM((hp, ATTN_BLOCK, tq), F32))
    else:
        in_specs = [q_spec, k_spec, v_spec]
        args = [qT, k, vT]
    return pl.pallas_call(
        functools.partial(_attn_kernel, moba=moba),
        grid=(B, heads // hp, S // tq),
        in_specs=in_specs,
        out_specs=pl.BlockSpec((1, hp * dv, tq), lambda b, h, i: (b, h, i)),
        out_shape=jax.ShapeDtypeStruct((B, heads * dv, S), F32),
        scratch_shapes=scratch,
        compiler_params=_params(("parallel", "parallel", "arbitrary")),
        name="moba_attn" if moba else "mla_attn",
    )(*args)


def _pool_kernel(x_ref, halo_ref, w_ref, sc_ref, o_ref, xs_ref):
    i = pl.program_id(1)
    tm = x_ref.shape[1]
    x = x_ref[0]
    xs_ref[0:POOL_HALO] = jnp.where(i > 0, halo_ref[0], 0.0)
    xs_ref[POOL_HALO:POOL_HALO + tm] = x
    t = i * tm + lax.broadcasted_iota(jnp.int32, (tm, 1), 0)
    outs = []
    for g, win in enumerate(POOL_WINDOWS):
        lo, hi = g * POOL_GROUP_DIM, (g + 1) * POOL_GROUP_DIM
        xg = x[:, lo:hi]
        acc = xg
        for d in range(1, win):
            acc = acc + xs_ref[POOL_HALO - d:POOL_HALO - d + tm, lo:hi]
        count = jnp.minimum(t + 1, win).astype(F32)
        outs.append(_dot((acc / count - xg).astype(BF16), w_ref[g]))
    o_ref[0] = (jnp.concatenate(outs, axis=1) * sc_ref[...]).astype(o_ref.dtype)


def _pool(ub, pool_w, pool_scale):
    B, S, W = ub.shape
    tm = ROW_TILE
    per = tm // POOL_HALO
    return pl.pallas_call(
        _pool_kernel,
        grid=(B, S // tm),
        in_specs=[
            pl.BlockSpec((1, tm, W), lambda b, i: (b, i, 0)),
            pl.BlockSpec((1, POOL_HALO, W), lambda b, i: (b, jnp.maximum(i * per - 1, 0), 0)),
            pl.BlockSpec(pool_w.shape, lambda b, i: (0, 0, 0)),
            pl.BlockSpec((1, W), lambda b, i: (0, 0)),
        ],
        out_specs=pl.BlockSpec((1, tm, W), lambda b, i: (b, i, 0)),
        out_shape=jax.ShapeDtypeStruct((B, S, W), BF16),
        scratch_shapes=[pltpu.VMEM((POOL_HALO + tm, W), F32)],
        compiler_params=_params(("parallel", "parallel")),
        name="pool",
    )(ub, ub, pool_w.astype(BF16), pool_scale.reshape(1, W))


def _mix_out_kernel(h_ref, aT_ref, b_ref, w_ref, o_ref, *, a_first):
    a = aT_ref[0].T.astype(BF16)
    b = b_ref[0].astype(BF16)
    lo, hi = (a, b) if a_first else (b, a)
    y = _dot(lo, w_ref[:MIX_WIDTH]) + _dot(hi, w_ref[MIX_WIDTH:])
    o_ref[0] = h_ref[0] + y


def _mix_out(h, aT, b, w_out, a_first):
    B, S, D = h.shape
    tm = ROW_TILE
    return pl.pallas_call(
        functools.partial(_mix_out_kernel, a_first=a_first),
        grid=(B, S // tm),
        in_specs=[
            pl.BlockSpec((1, tm, D), lambda b_, i: (b_, i, 0)),
            pl.BlockSpec((1, MIX_WIDTH, tm), lambda b_, i: (b_, 0, i)),
            pl.BlockSpec((1, tm, MIX_WIDTH), lambda b_, i: (b_, i, 0)),
            pl.BlockSpec((2 * MIX_WIDTH, D), lambda b_, i: (0, 0)),
        ],
        out_specs=pl.BlockSpec((1, tm, D), lambda b_, i: (b_, i, 0)),
        out_shape=jax.ShapeDtypeStruct((B, S, D), F32),
        compiler_params=_params(("parallel", "parallel")),
        name="mix_out",
    )(h, aT, b, w_out.astype(BF16))


def _swiglu_step(xn, wg, wu, wd):
    gt = _dot(xn, wg)
    up = _dot(xn, wu)
    return _dot((gt * _sigmoid(gt) * up).astype(BF16), wd)


def _ffn_kernel(h_ref, g_ref, wg_ref, wu_ref, wd_ref, o_ref, xn_ref, acc_ref):
    f = pl.program_id(1)

    @pl.when(f == 0)
    def _():
        xn_ref[...] = _rms(h_ref[...], g_ref[...]).astype(BF16)
        acc_ref[...] = jnp.zeros_like(acc_ref)

    acc_ref[...] += _swiglu_step(xn_ref[...], wg_ref[...], wu_ref[...], wd_ref[...])

    @pl.when(f == pl.num_programs(1) - 1)
    def _():
        o_ref[...] = h_ref[...] + acc_ref[...]


def _ffn(h2d, g, wg, wu, wd, tm=ROW_TILE, tf=FFN_DIM // 2):
    T, D = h2d.shape
    F = wg.shape[1]
    return pl.pallas_call(
        _ffn_kernel,
        grid=(T // tm, F // tf),
        in_specs=[
            pl.BlockSpec((tm, D), lambda i, f: (i, 0)),
            pl.BlockSpec((1, D), lambda i, f: (0, 0)),
            pl.BlockSpec((D, tf), lambda i, f: (0, f)),
            pl.BlockSpec((D, tf), lambda i, f: (0, f)),
            pl.BlockSpec((tf, D), lambda i, f: (f, 0)),
        ],
        out_specs=pl.BlockSpec((tm, D), lambda i, f: (i, 0)),
        out_shape=jax.ShapeDtypeStruct((T, D), F32),
        scratch_shapes=[pltpu.VMEM((tm, D), BF16), pltpu.VMEM((tm, D), F32)],
        compiler_params=_params(("parallel", "arbitrary")),
        name="ffn",
    )(h2d, g.reshape(1, D), wg, wu, wd)


ROUTE_E0, ROUTE_E1, ROUTE_W0, ROUTE_W1, ROUTE_R0, ROUTE_R1 = range(6)
MOE_TILE = 512
MOE_TF = EXPERT_DIM // 4


def _lane_pick(tile, lane, idx):
    return jnp.sum(jnp.where(lane == idx, tile, 0.0), axis=1, keepdims=True)


def _router_kernel(h_ref, g_ref, w_ref, b_ref, route_ref, cnt_ref):
    tm = h_ref.shape[0]

    @pl.when(pl.program_id(0) == 0)
    def _():
        cnt_ref[...] = jnp.zeros_like(cnt_ref)

    xn = _rms(h_ref[...], g_ref[...])
    logits = jnp.dot(xn, w_ref[...], precision=HIGHEST, preferred_element_type=F32) + b_ref[...]
    lane = lax.broadcasted_iota(jnp.int32, logits.shape, 1)
    logits = jnp.where(lane < N_EXPERTS, logits, -jnp.inf)
    v0 = jnp.max(logits, axis=1, keepdims=True)
    i0 = jnp.min(jnp.where(logits == v0, lane, LANES), axis=1, keepdims=True)
    rest = jnp.where(lane == i0, -jnp.inf, logits)
    v1 = jnp.max(rest, axis=1, keepdims=True)
    i1 = jnp.min(jnp.where(rest == v1, lane, LANES), axis=1, keepdims=True)
    e1 = jnp.exp(v1 - v0)
    w0 = 1.0 / (1.0 + e1)
    sel = (lane == i0).astype(F32) + (lane == i1).astype(F32)
    earlier = (lax.broadcasted_iota(jnp.int32, (tm, tm), 1)
               < lax.broadcasted_iota(jnp.int32, (tm, tm), 0))
    rank = _dot(earlier.astype(BF16), sel.astype(BF16)) + cnt_ref[...]
    cnt_ref[...] += jnp.sum(sel, axis=0, keepdims=True)
    cols = ((ROUTE_E0, i0.astype(F32)), (ROUTE_E1, i1.astype(F32)), (ROUTE_W0, w0), (ROUTE_W1, e1 * w0),
            (ROUTE_R0, _lane_pick(rank, lane, i0)), (ROUTE_R1, _lane_pick(rank, lane, i1)))
    route = jnp.zeros(logits.shape, F32)
    for c, val in cols:
        route = jnp.where(lane == c, val, route)
    route_ref[...] = route


def _router(h2d, g, router_w, router_b):
    T, D = h2d.shape
    tm = ROW_TILE
    w = jnp.zeros((D, LANES), F32).at[:, :N_EXPERTS].set(router_w)
    b = jnp.zeros((1, LANES), F32).at[0, :N_EXPERTS].set(router_b)
    return pl.pallas_call(
        _router_kernel,
        grid=(T // tm,),
        in_specs=[
            pl.BlockSpec((tm, D), lambda i: (i, 0)),
            pl.BlockSpec((1, D), lambda i: (0, 0)),
            pl.BlockSpec((D, LANES), lambda i: (0, 0)),
            pl.BlockSpec((1, LANES), lambda i: (0, 0)),
        ],
        out_specs=[pl.BlockSpec((tm, LANES), lambda i: (i, 0)),
                   pl.BlockSpec((1, LANES), lambda i: (0, 0))],
        out_shape=[jax.ShapeDtypeStruct((T, LANES), F32), jax.ShapeDtypeStruct((1, LANES), F32)],
        compiler_params=_params(("arbitrary",)),
        name="router",
    )(h2d, g.reshape(1, D), w, b)


def _row_copies(pos_ref, base, r, src_of, dst_of, sem):
    return [pltpu.make_async_copy(src_of(k, pos_ref[base + 2 * r + k]),
                                  dst_of(k, pos_ref[base + 2 * r + k]), sem) for k in range(2)]


def _all_rows(tm, make):
    def issue(r, c):
        for cp in make(r):
            cp.start()
        return c

    def drain(r, c):
        for cp in make(r):
            cp.wait()
        return c

    lax.fori_loop(0, tm, issue, 0, unroll=8)
    lax.fori_loop(0, tm, drain, 0, unroll=8)


def _dispatch_kernel(pos_ref, h_ref, g_ref, init_ref, xs_ref, xn_ref, sem):
    del init_ref
    tm = h_ref.shape[0]
    xn_ref[...] = _rms(h_ref[...], g_ref[...])
    base = pl.program_id(0) * (2 * tm)
    _all_rows(tm, lambda r: _row_copies(
        pos_ref, base, r, lambda k, p: xn_ref.at[pl.ds(r, 1)], lambda k, p: xs_ref.at[pl.ds(p, 1)], sem))


def _dispatch(pos, h2d, g, n_rows):
    T, D = h2d.shape
    tm = ROW_TILE
    return pl.pallas_call(
        _dispatch_kernel,
        grid_spec=pltpu.PrefetchScalarGridSpec(
            num_scalar_prefetch=1,
            grid=(T // tm,),
            in_specs=[pl.BlockSpec((tm, D), lambda i, pos: (i, 0)),
                      pl.BlockSpec((1, D), lambda i, pos: (0, 0)),
                      pl.BlockSpec(memory_space=pl.ANY)],
            out_specs=pl.BlockSpec(memory_space=pl.ANY),
            scratch_shapes=[pltpu.VMEM((tm, D), F32), pltpu.SemaphoreType.DMA(())],
        ),
        out_shape=jax.ShapeDtypeStruct((n_rows, D), F32),
        input_output_aliases={3: 0},
        compiler_params=_params(("arbitrary",)),
        name="moe_dispatch",
    )(pos, h2d, g.reshape(1, D), jnp.zeros((n_rows, D), F32))


def _moe_ffn_kernel(te_ref, nv_ref, x_ref, wg_ref, wu_ref, wd_ref, o_ref, xb_ref, acc_ref):
    j = pl.program_id(0)
    f = pl.program_id(1)
    valid = j < nv_ref[0]

    @pl.when(f == 0)
    def _():
        xb_ref[...] = x_ref[...].astype(BF16)
        acc_ref[...] = jnp.zeros_like(acc_ref)

    @pl.when(valid)
    def _():
        acc_ref[...] += _swiglu_step(xb_ref[...], wg_ref[0], wu_ref[0], wd_ref[0])

    @pl.when(f == pl.num_programs(1) - 1)
    def _():
        o_ref[...] = acc_ref[...]


def _moe_ffn(tile_expert, n_valid, xs, wg, wu, wd):
    N, D = xs.shape
    tm, tf = MOE_TILE, MOE_TF
    F = wg.shape[2]
    return pl.pallas_call(
        _moe_ffn_kernel,
        grid_spec=pltpu.PrefetchScalarGridSpec(
            num_scalar_prefetch=2,
            grid=(N // tm, F // tf),
            in_specs=[pl.BlockSpec((tm, D), lambda j, f, te, nv: (j, 0)),
                      pl.BlockSpec((1, D, tf), lambda j, f, te, nv: (te[j], 0, f)),
                      pl.BlockSpec((1, D, tf), lambda j, f, te, nv: (te[j], 0, f)),
                      pl.BlockSpec((1, tf, D), lambda j, f, te, nv: (te[j], f, 0))],
            out_specs=pl.BlockSpec((tm, D), lambda j, f, te, nv: (j, 0)),
            scratch_shapes=[pltpu.VMEM((tm, D), BF16), pltpu.VMEM((tm, D), F32)],
        ),
        out_shape=jax.ShapeDtypeStruct((N, D), F32),
        compiler_params=_params(("arbitrary", "arbitrary")),
        name="moe_ffn",
    )(tile_expert, n_valid, xs, wg, wu, wd)


def _combine_kernel(pos_ref, h_ref, route_ref, ys_ref, o_ref, y_ref, sem):
    tm = h_ref.shape[0]
    base = pl.program_id(0) * (2 * tm)
    _all_rows(tm, lambda r: _row_copies(
        pos_ref, base, r, lambda k, p: ys_ref.at[pl.ds(p, 1)], lambda k, p: y_ref.at[k, pl.ds(r, 1)], sem))
    route = route_ref[...]
    lane = lax.broadcasted_iota(jnp.int32, route.shape, 1)
    o_ref[...] = (h_ref[...] + _lane_pick(route, lane, ROUTE_W0) * y_ref[0]
                  + _lane_pick(route, lane, ROUTE_W1) * y_ref[1])


def _combine(pos, h2d, route, ys):
    T, D = h2d.shape
    tm = ROW_TILE
    return pl.pallas_call(
        _combine_kernel,
        grid_spec=pltpu.PrefetchScalarGridSpec(
            num_scalar_prefetch=1,
            grid=(T // tm,),
            in_specs=[pl.BlockSpec((tm, D), lambda i, pos: (i, 0)),
                      pl.BlockSpec((tm, LANES), lambda i, pos: (i, 0)),
                      pl.BlockSpec(memory_space=pl.ANY)],
            out_specs=pl.BlockSpec((tm, D), lambda i, pos: (i, 0)),
            scratch_shapes=[pltpu.VMEM((2, tm, D), F32), pltpu.SemaphoreType.DMA(())],
        ),
        out_shape=jax.ShapeDtypeStruct((T, D), F32),
        compiler_params=_params(("arbitrary",)),
        name="moe_combine",
    )(pos, h2d, route, ys)


def _moe(h2d, g, router_w, router_b, wg, wu, wd):
    T, D = h2d.shape
    tm = MOE_TILE
    route, counts = _router(h2d, g, router_w, router_b)
    cnt = counts[0, :N_EXPERTS].astype(jnp.int32)
    padded = (cnt + tm - 1) // tm * tm
    ends = jnp.cumsum(padded)
    start = ends - padded
    e01 = route[:, ROUTE_E0:ROUTE_E1 + 1].astype(jnp.int32)
    r01 = route[:, ROUTE_R0:ROUTE_R1 + 1].astype(jnp.int32)
    pos = (start[e01] + r01).reshape(2 * T)
    n_rows = 2 * T + N_EXPERTS * tm
    tile_row = jnp.arange(n_rows // tm, dtype=jnp.int32) * tm
    tile_expert = jnp.minimum(jnp.sum(tile_row[:, None] >= ends[None, :], axis=1), N_EXPERTS - 1).astype(jnp.int32)
    n_valid = (ends[-1:] // tm).astype(jnp.int32)
    xs = _dispatch(pos, h2d, g, n_rows)
    ys = _moe_ffn(tile_expert, n_valid, xs, wg, wu, wd)
    return _combine(pos, h2d, route, ys)


def _ple_kernel(*refs, final):
    if final:
        h_ref, p_ref, g_ref, wg_ref, wp_ref, fg_ref, o_ref = refs
    else:
        h_ref, p_ref, g_ref, wg_ref, wp_ref, o_ref = refs
    h = h_ref[...]
    gate = _sigmoid(_dot(_rms(h, g_ref[...]).astype(BF16), wg_ref[...]))
    out = h + gate * _dot(p_ref[...].astype(BF16), wp_ref[...])
    if final:
        out = _rms(out, fg_ref[...])
    o_ref[...] = out


def _ple(h2d, p2d, g, w_gate, w_proj, final_g=None):
    T, D = h2d.shape
    tm = ROW_TILE
    final = final_g is not None
    in_specs = [
        pl.BlockSpec((tm, D), lambda i: (i, 0)),
        pl.BlockSpec((tm, PLE_DIM), lambda i: (i, 0)),
        pl.BlockSpec((1, D), lambda i: (0, 0)),
        pl.BlockSpec((D, D), lambda i: (0, 0)),
        pl.BlockSpec((PLE_DIM, D), lambda i: (0, 0)),
    ]
    args = [h2d, p2d, g.reshape(1, D), w_gate.astype(BF16), w_proj.astype(BF16)]
    if final:
        in_specs.append(pl.BlockSpec((1, D), lambda i: (0, 0)))
        args.append(final_g.reshape(1, D))
    return pl.pallas_call(
        functools.partial(_ple_kernel, final=final),
        grid=(T // tm,),
        in_specs=in_specs,
        out_specs=pl.BlockSpec((tm, D), lambda i: (i, 0)),
        out_shape=jax.ShapeDtypeStruct((T, D), F32),
        compiler_params=_params(("parallel",)),
        name="ple_final" if final else "ple",
    )(*args)


ODD_MAIN = 4 * MIX_WIDTH
ODD_COLS = ODD_MAIN + MLA_Q_RANK + MLA_KV_RANK + 2 * LANES
MLA_QK_SCALE = (MLA_NOPE_DIM + MLA_ROPE_DIM) ** -0.5
ROPE_HALF = MLA_ROPE_DIM // 2


def _odd_in_kernel(x_ref, g_ref, wn_ref, qn_ref, wuqT_ref, kvn_ref, wk2_ref, wvT_ref,
                   cosT_ref, sinT_ref, cc_ref, ss_ref,
                   qk_ref, vc_ref, op_ref, misc_ref, mq_ref, mk_ref, mv_ref):
    tm = x_ref.shape[1]
    xn = _rms(x_ref[0], g_ref[...]).astype(BF16)
    u = _dot(xn, wn_ref[...])
    qk_ref[0] = u[:, :2 * MIX_WIDTH]
    vc_ref[0] = u[:, 2 * MIX_WIDTH:3 * MIX_WIDTH].astype(BF16)
    op_ref[0] = u[:, 3 * MIX_WIDTH:ODD_MAIN]
    c0 = ODD_MAIN
    c_q = u[:, c0:c0 + MLA_Q_RANK]
    c0 += MLA_Q_RANK
    c_kv = u[:, c0:c0 + MLA_KV_RANK]
    c0 += MLA_KV_RANK
    misc = u[:, c0:c0 + LANES]
    misc_sw = u[:, c0 + LANES:c0 + 2 * LANES]
    misc_ref[0] = misc
    cqn = _rms(c_q, qn_ref[...]).astype(BF16)
    qT = _dot_nt(wuqT_ref[...], cqn)
    cosT = cosT_ref[...]
    sinT = sinT_ref[...]
    for h in range(MLA_HEADS):
        r = h * LANES
        mq_ref[0, r:r + MLA_NOPE_DIM] = (qT[r:r + MLA_NOPE_DIM] * MLA_QK_SCALE).astype(BF16)
        x1 = qT[r + MLA_NOPE_DIM:r + MLA_NOPE_DIM + ROPE_HALF]
        x2 = qT[r + MLA_NOPE_DIM + ROPE_HALF:r + MLA_NOPE_DIM + MLA_ROPE_DIM]
        mq_ref[0, r + MLA_NOPE_DIM:r + MLA_NOPE_DIM + ROPE_HALF] = (
            (x1 * cosT - x2 * sinT) * MLA_QK_SCALE).astype(BF16)
        mq_ref[0, r + MLA_NOPE_DIM + ROPE_HALF:r + MLA_NOPE_DIM + MLA_ROPE_DIM] = (
            (x1 * sinT + x2 * cosT) * MLA_QK_SCALE).astype(BF16)
        mq_ref[0, r + MLA_NOPE_DIM + MLA_ROPE_DIM:r + LANES] = jnp.zeros(
            (LANES - MLA_NOPE_DIM - MLA_ROPE_DIM, tm), BF16)
    ckvn = _rms(c_kv, kvn_ref[...]).astype(BF16)
    k_rot = (misc * cc_ref[...] + misc_sw * ss_ref[...]).astype(BF16)
    mk_ref[0] = _dot(jnp.concatenate([ckvn, k_rot], axis=1), wk2_ref[...]).astype(BF16)
    vT = _dot_nt(wvT_ref[...], ckvn)
    for j in range(tm // ATTN_BLOCK):
        mv_ref[0, j] = vT[:, j * ATTN_BLOCK:(j + 1) * ATTN_BLOCK].astype(BF16)


def _rope_tables(S):
    inv_freq = ROPE_BASE ** (-jnp.arange(ROPE_HALF, dtype=F32) / ROPE_HALF)
    ang = jnp.arange(S, dtype=F32)[:, None] * inv_freq[None, :]
    cos, sin = jnp.cos(ang), jnp.sin(ang)
    pad = jnp.zeros((S, LANES - MLA_ROPE_DIM), F32)
    cc = jnp.concatenate([cos, cos, pad], axis=1)
    ss = jnp.concatenate([-sin, sin, pad], axis=1)
    return cos.T, sin.T, cc, ss


def _odd_in(x, g, w_in, q_norm, w_uq, kv_norm, w_ukv):
    B, S, D = x.shape
    tm = ROW_TILE
    cuts = np.cumsum([MIX_WIDTH] * 4 + [MLSTM_HEADS, MLSTM_HEADS, MLA_Q_RANK, MLA_KV_RANK]).tolist()
    w_main = w_in[:, :cuts[3]]
    w_i = w_in[:, cuts[3]:cuts[4]]
    w_f = w_in[:, cuts[4]:cuts[5]]
    w_cq = w_in[:, cuts[5]:cuts[6]]
    w_ckv = w_in[:, cuts[6]:cuts[7]]
    w_kr = w_in[:, cuts[7]:]
    w_kr_sw = jnp.concatenate([w_kr[:, ROPE_HALF:], w_kr[:, :ROPE_HALF]], axis=1)
    zpad = lambda n: jnp.zeros((D, n), F32)
    w_misc = jnp.concatenate([w_kr, w_i, w_f, zpad(LANES - MLA_ROPE_DIM - 2 * MLSTM_HEADS)], axis=1)
    w_misc_sw = jnp.concatenate([w_kr_sw, zpad(LANES - MLA_ROPE_DIM)], axis=1)
    wn = jnp.concatenate([w_main, w_cq, w_ckv, w_misc, w_misc_sw], axis=1).astype(BF16)
    qd = MLA_NOPE_DIM + MLA_ROPE_DIM
    w_uq_h = w_uq.reshape(MLA_Q_RANK, MLA_HEADS, qd)
    w_uq_h = jnp.concatenate([w_uq_h, jnp.zeros((MLA_Q_RANK, MLA_HEADS, LANES - qd), F32)], axis=2)
    wuqT = w_uq_h.reshape(MLA_Q_RANK, MLA_HEADS * LANES).T.astype(BF16)
    w_ukv_h = w_ukv.reshape(MLA_KV_RANK, MLA_HEADS, MLA_NOPE_DIM + MLA_V_DIM)
    w_k = jnp.concatenate([w_ukv_h[:, :, :MLA_NOPE_DIM],
                           jnp.zeros((MLA_KV_RANK, MLA_HEADS, LANES - MLA_NOPE_DIM), F32)], axis=2)
    place = jnp.zeros((LANES, MLA_HEADS, LANES), F32)
    eye = jnp.eye(MLA_ROPE_DIM, dtype=F32)
    place = place.at[:MLA_ROPE_DIM, :, MLA_NOPE_DIM:MLA_NOPE_DIM + MLA_ROPE_DIM].set(
        jnp.broadcast_to(eye[:, None, :], (MLA_ROPE_DIM, MLA_HEADS, MLA_ROPE_DIM)))
    wk2 = jnp.concatenate([w_k, place], axis=0).reshape(MLA_KV_RANK + LANES, MLA_HEADS * LANES).astype(BF16)
    wvT = w_ukv_h[:, :, MLA_NOPE_DIM:].reshape(MLA_KV_RANK, MLA_HEADS * MLA_V_DIM).T.astype(BF16)
    cosT, sinT, cc, ss = _rope_tables(S)
    row = lambda b, i: (b, i, 0)
    const = lambda b, i: (0, 0)
    nb = S // ATTN_BLOCK
    return pl.pallas_call(
        _odd_in_kernel,
        grid=(B, S // tm),
        in_specs=[
            pl.BlockSpec((1, tm, D), row),
            pl.BlockSpec((1, D), const),
            pl.BlockSpec((D, ODD_COLS), const),
            pl.BlockSpec((1, MLA_Q_RANK), const),
            pl.BlockSpec((MLA_HEADS * LANES, MLA_Q_RANK), const),
            pl.BlockSpec((1, MLA_KV_RANK), const),
            pl.BlockSpec((MLA_KV_RANK + LANES, MLA_HEADS * LANES), const),
            pl.BlockSpec((MLA_HEADS * MLA_V_DIM, MLA_KV_RANK), const),
            pl.BlockSpec((ROPE_HALF, tm), lambda b, i: (0, i)),
            pl.BlockSpec((ROPE_HALF, tm), lambda b, i: (0, i)),
            pl.BlockSpec((tm, LANES), lambda b, i: (i, 0)),
            pl.BlockSpec((tm, LANES), lambda b, i: (i, 0)),
        ],
        out_specs=[
            pl.BlockSpec((1, tm, 2 * MIX_WIDTH), row),
            pl.BlockSpec((1, tm, MIX_WIDTH), row),
            pl.BlockSpec((1, tm, MIX_WIDTH), row),
            pl.BlockSpec((1, tm, LANES), row),
            pl.BlockSpec((1, MLA_HEADS * LANES, tm), lambda b, i: (b, 0, i)),
            pl.BlockSpec((1, tm, MLA_HEADS * LANES), row),
            pl.BlockSpec((1, tm // ATTN_BLOCK, MLA_HEADS * MLA_V_DIM, ATTN_BLOCK), lambda b, i: (b, i, 0, 0)),
        ],
        out_shape=[
            jax.ShapeDtypeStruct((B, S, 2 * MIX_WIDTH), F32),
            jax.ShapeDtypeStruct((B, S, MIX_WIDTH), BF16),
            jax.ShapeDtypeStruct((B, S, MIX_WIDTH), F32),
            jax.ShapeDtypeStruct((B, S, LANES), F32),
            jax.ShapeDtypeStruct((B, MLA_HEADS * LANES, S), BF16),
            jax.ShapeDtypeStruct((B, S, MLA_HEADS * LANES), BF16),
            jax.ShapeDtypeStruct((B, nb, MLA_HEADS * MLA_V_DIM, ATTN_BLOCK), BF16),
        ],
        compiler_params=_params(("parallel", "parallel")),
        name="odd_in",
    )(x, g.reshape(1, D), wn, q_norm.reshape(1, -1), wuqT, kv_norm.reshape(1, -1), wk2, wvT,
      cosT, sinT, cc, ss)


def _log_sigmoid(x):
    return jnp.minimum(x, 0.0) - jnp.log(1.0 + jnp.exp(-jnp.abs(x)))


def _mlstm_kernel(qk_ref, v_ref, op_ref, misc_ref, cw_ref, gb_ref, hn_ref, o_ref,
                  prev_ref, cn_ref, m_ref):
    c = pl.program_id(1)
    L = qk_ref.shape[1]
    d = MLSTM_HEAD_DIM

    @pl.when(c == 0)
    def _():
        prev_ref[...] = jnp.zeros_like(prev_ref)
        cn_ref[...] = jnp.zeros_like(cn_ref)
        m_ref[...] = jnp.zeros_like(m_ref)

    x = qk_ref[0]
    prev = prev_ref[...]
    row = lax.broadcasted_iota(jnp.int32, (L, 1), 0)
    conv = x * cw_ref[CONV_WIDTH - 1:CONV_WIDTH, :]
    for j in range(1, CONV_WIDTH):
        shifted = pltpu.roll(jnp.where(row >= L - j, prev, x), j, axis=0)
        conv = conv + shifted * cw_ref[CONV_WIDTH - 1 - j:CONV_WIDTH - j, :]
    prev_ref[...] = x
    qk = conv * _sigmoid(conv)

    gates = misc_ref[0] + gb_ref[...]
    lane = lax.broadcasted_iota(jnp.int32, (L, LANES), 1)
    is_f = (lane >= MISC_F) & (lane < MISC_F + MLSTM_HEADS)
    z = jnp.where(is_f, _log_sigmoid(gates), gates)
    tri_r = lax.broadcasted_iota(jnp.int32, (L, L), 0)
    tri_c = lax.broadcasted_iota(jnp.int32, (L, L), 1)
    causal = tri_c <= tri_r
    cum = jnp.dot(causal.astype(F32), z, precision=HIGHEST, preferred_element_type=F32)
    z = jnp.where(is_f, cum, z)
    zT = z.T

    ones_col = (lax.broadcasted_iota(jnp.int32, (L, LANES), 1) == 0).astype(BF16)
    for h in range(MLSTM_HEADS):
        lo, hi = h * d, (h + 1) * d
        q = qk[:, lo:hi].astype(BF16)
        k = qk[:, MIX_WIDTH + lo:MIX_WIDTH + hi] * (d ** -0.5)
        v_aug = jnp.concatenate([v_ref[0, :, lo:hi], ones_col], axis=1)
        i_col = z[:, MISC_I + h:MISC_I + h + 1]
        b_col = z[:, MISC_F + h:MISC_F + h + 1]
        i_row = zT[MISC_I + h:MISC_I + h + 1, :]
        b_row = zT[MISC_F + h:MISC_F + h + 1, :]
        m_prev = m_ref[h:h + 1, 0:1]
        intra = jnp.where(causal, b_col - b_row + i_row, NEG_INF)
        m_inter = b_col + m_prev
        m_t = jnp.maximum(m_inter, jnp.max(intra, axis=1, keepdims=True))
        w_inter = jnp.exp(m_inter - m_t)
        a = jnp.exp(intra - m_t) * _dot_nt(q, k.astype(BF16))
        inter = _dot(q, cn_ref[h].astype(BF16))
        intra_o = _dot(a.astype(BF16), v_aug)
        num = w_inter * inter[:, :d] + intra_o[:, :d]
        den = w_inter * inter[:, d:d + 1] + intra_o[:, d:d + 1]
        hh = num / jnp.maximum(jnp.abs(den), jnp.exp(-m_t))
        hh = _rms(hh, hn_ref[:, lo:hi])
        o_ref[0, :, lo:hi] = (hh * _sigmoid(op_ref[0, :, lo:hi])).astype(o_ref.dtype)
        b_end = b_col[L - 1:L, :]
        g_col = b_end - b_col + i_col
        m_new = jnp.maximum(b_end + m_prev, jnp.max(g_col, axis=0, keepdims=True))
        decay = jnp.exp(b_end + m_prev - m_new)
        kw = k * jnp.exp(g_col - m_new)
        cn_ref[h] = decay * cn_ref[h] + _dot(kw.T.astype(BF16), v_aug)
        m_ref[h:h + 1, :] = jnp.broadcast_to(m_new, (1, LANES))


def _mlstm(qk_raw, vc, o_pre, misc, conv_w, b_i, b_f, head_norm):
    B, S, _ = qk_raw.shape
    L = MLSTM_CHUNK
    gb = jnp.zeros((1, LANES), F32).at[0, MISC_I:MISC_I + MLSTM_HEADS].set(b_i)
    gb = gb.at[0, MISC_F:MISC_F + MLSTM_HEADS].set(b_f)
    row = lambda b, c: (b, c, 0)
    const = lambda b, c: (0, 0)
    return pl.pallas_call(
        _mlstm_kernel,
        grid=(B, S // L),
        in_specs=[
            pl.BlockSpec((1, L, 2 * MIX_WIDTH), row),
            pl.BlockSpec((1, L, MIX_WIDTH), row),
            pl.BlockSpec((1, L, MIX_WIDTH), row),
            pl.BlockSpec((1, L, LANES), row),
            pl.BlockSpec((CONV_WIDTH, 2 * MIX_WIDTH), const),
            pl.BlockSpec((1, LANES), const),
            pl.BlockSpec((1, MIX_WIDTH), const),
        ],
        out_specs=pl.BlockSpec((1, L, MIX_WIDTH), row),
        out_shape=jax.ShapeDtypeStruct((B, S, MIX_WIDTH), BF16),
        scratch_shapes=[
            pltpu.VMEM((L, 2 * MIX_WIDTH), F32),
            pltpu.VMEM((MLSTM_HEADS, MLSTM_HEAD_DIM, 2 * LANES), F32),
            pltpu.VMEM((8, LANES), F32),
        ],
        compiler_params=_params(("parallel", "arbitrary")),
        name="mlstm",
    )(qk_raw, vc, o_pre, misc, conv_w, gb, head_norm.reshape(1, MIX_WIDTH))


def _even_layer(h, norm_mix, w_in, pool_w, pool_scale, w_out, norm_ffn, wg, wu, wd):
    B, S, D = h.shape
    kb, ub, kmean, qaT, qT, vT = _even_in(h, norm_mix, w_in)
    nb = S // MOBA_BLOCK
    kmean = kmean.reshape(B, nb, MOBA_HEADS, MOBA_HEAD_DIM).transpose(0, 2, 1, 3)
    slopes = jnp.asarray(2.0 ** (-8.0 * np.arange(1, MOBA_HEADS + 1) / MOBA_HEADS), dtype=F32)
    mrow = _moba_gate(kmean, qT)
    aT = _attention(qaT, kb, vT, MOBA_HEAD_DIM, MOBA_HEADS, slopes=slopes, mrow=mrow)
    b_out = _pool(ub, pool_w, pool_scale)
    h = _mix_out(h, aT, b_out, w_out, a_first=True)
    return _ffn(h.reshape(B * S, D), norm_ffn, wg.astype(BF16), wu.astype(BF16), wd.astype(BF16))


def _odd_layer(h, norm_mix, w_in, conv_w, b_i, b_f, head_norm, q_norm, w_uq, kv_norm, w_ukv,
               w_out, norm_ffn, router_w, router_b, wg, wu, wd):
    B, S, D = h.shape
    qk_raw, vc, o_pre, misc, mqT, mk, mvT = _odd_in(h, norm_mix, w_in, q_norm, w_uq, kv_norm, w_ukv)
    c_out = _mlstm(qk_raw, vc, o_pre, misc, conv_w, b_i, b_f, head_norm)
    dT = _attention(mqT, mk, mvT, MLA_V_DIM, MLA_HEADS)
    h = _mix_out(h, dT, c_out, w_out, a_first=False)
    return _moe(h.reshape(B * S, D), norm_ffn, router_w, router_b,
                wg.astype(BF16), wu.astype(BF16), wd.astype(BF16))


def kernel(x, p, ev_norm_mix, ev_w_in, pool_w, pool_scale, ev_w_out, ev_norm_ffn, ffn_w_gate, ffn_w_up, ffn_w_down, od_norm_mix, od_w_in, conv_w, gate_b_i, gate_b_f, mlstm_norm, mla_q_norm, mla_w_uq, mla_kv_norm, mla_w_ukv, od_w_out, od_norm_ffn, router_w, router_b, moe_w_gate, moe_w_up, moe_w_down, ple_norm, ple_w_gate, ple_w_proj, final_norm):
    B, S, D = x.shape
    depth = p.shape[0]
    assert D == D_MODEL and S % (2 * ROW_TILE) == 0 and S // MOBA_BLOCK >= MOBA_TOPK
    h = x
    for layer in range(depth):
        j = layer // 2
        if layer % 2 == 0:
            h2d = _even_layer(h, ev_norm_mix[j], ev_w_in[j], pool_w[j], pool_scale[j], ev_w_out[j],
                              ev_norm_ffn[j], ffn_w_gate[j], ffn_w_up[j], ffn_w_down[j])
        else:
            h2d = _odd_layer(h, od_norm_mix[j], od_w_in[j], conv_w[j], gate_b_i[j], gate_b_f[j],
                             mlstm_norm[j], mla_q_norm[j], mla_w_uq[j], mla_kv_norm[j], mla_w_ukv[j],
                             od_w_out[j], od_norm_ffn[j], router_w[j], router_b[j],
                             moe_w_gate[j], moe_w_up[j], moe_w_down[j])
        last = layer == depth - 1
        h2d = _ple(h2d, p[layer].reshape(B * S, PLE_DIM), ple_norm[layer], ple_w_gate[layer],
                   ple_w_proj[layer], final_g=final_norm if last else None)
        h = h2d.reshape(B, S, D)
    return h
```

```python
import functools
import math

import numpy as np
import jax
import jax.numpy as jnp
from jax import lax
from jax.experimental import pallas as pl
from jax.experimental.pallas import tpu as pltpu

F32 = jnp.float32
BF16 = jnp.bfloat16
HIGHEST = lax.Precision.HIGHEST

D_MODEL = 1024
PLE_DIM = 256
NORM_EPS = 1e-6
NEG_INF = -1e30

MOBA_HEADS = 8
MOBA_HEAD_DIM = 64
MOBA_BLOCK = 256
MOBA_TOPK = 3
POOL_WINDOWS = (2, 4, 8, 16)
POOL_GROUP_DIM = 128
POOL_HALO = 16
MLSTM_HEADS = 4
MLSTM_HEAD_DIM = 128
MLSTM_CHUNK = 128
CONV_WIDTH = 4
MLA_HEADS = 4
MLA_Q_RANK = 256
MLA_KV_RANK = 128
MLA_NOPE_DIM = 64
MLA_ROPE_DIM = 32
MLA_V_DIM = 128
ROPE_BASE = 10000.0
FFN_DIM = 2816
N_EXPERTS = 8
EXPERT_DIM = 3584
MIX_WIDTH = 512

ATTN_BLOCK = 512
ROW_TILE = 512
LANES = 128
VMEM_LIMIT = 56 * 1024 * 1024

MISC_ROPE = 0
MISC_I = 32
MISC_F = 36


def _params(sem, vmem=VMEM_LIMIT):
    return pltpu.CompilerParams(dimension_semantics=sem, vmem_limit_bytes=vmem)


def _rms(x, g):
    ms = jnp.mean(x * x, axis=-1, keepdims=True)
    return x * lax.rsqrt(ms + NORM_EPS) * g


def _sigmoid(x):
    return 1.0 / (1.0 + jnp.exp(-x))


def _dot(a, b):
    return jnp.dot(a, b, preferred_element_type=F32)


def _dot_nt(a, b, precision=None):
    return lax.dot_general(a, b, (((1,), (1,)), ((), ())), precision=precision,
                           preferred_element_type=F32)


KAUG_SEL = MOBA_HEAD_DIM
KAUG_POS = KAUG_SEL + 32
MOBA_MAX_BLOCKS = KAUG_POS - KAUG_SEL


def _even_in_kernel(x_ref, g_ref, wn_ref, wqT_ref, wvT_ref, ext_ref,
                    ka_ref, ub_ref, km_ref, qT_ref, vT_ref):
    tm = x_ref.shape[1]
    xn = _rms(x_ref[0], g_ref[...]).astype(BF16)
    n = _dot(xn, wn_ref[...])
    ka = n[:, :MOBA_HEADS * LANES]
    ka_ref[0] = (ka + ext_ref[...].astype(F32)).astype(BF16)
    ub_ref[0] = n[:, MOBA_HEADS * LANES:]
    for j in range(tm // MOBA_BLOCK):
        km_ref[0, j] = jnp.mean(ka[j * MOBA_BLOCK:(j + 1) * MOBA_BLOCK], axis=0, keepdims=True)
    qT_ref[0] = _dot_nt(wqT_ref[...], xn)
    vT = _dot_nt(wvT_ref[...], xn)
    for j in range(tm // ATTN_BLOCK):
        vT_ref[0, j] = vT[:, j * ATTN_BLOCK:(j + 1) * ATTN_BLOCK].astype(BF16)


def _moba_key_extras(S, slopes):
    pos = np.arange(S)
    blk, off = pos // MOBA_BLOCK, pos % MOBA_BLOCK
    ext = np.zeros((S, MOBA_HEADS, LANES), np.float32)
    ext[pos, :, KAUG_SEL + blk] = 1.0
    ext[:, :, KAUG_POS] = slopes[None, :] * (MOBA_BLOCK * blk)[:, None]
    ext[:, :, KAUG_POS + 1] = slopes[None, :] * off[:, None]
    return jnp.asarray(ext.reshape(S, MOBA_HEADS * LANES), dtype=BF16)


def _even_in(x, g, w_in, slopes):
    B, S, D = x.shape
    tm = ROW_TILE
    nb = S // MOBA_BLOCK
    wq, wk, wv, wu = (w_in[:, i * MIX_WIDTH:(i + 1) * MIX_WIDTH] for i in range(4))
    wk_slots = jnp.concatenate(
        [wk.reshape(D, MOBA_HEADS, MOBA_HEAD_DIM),
         jnp.zeros((D, MOBA_HEADS, LANES - MOBA_HEAD_DIM), F32)], axis=2).reshape(D, MOBA_HEADS * LANES)
    wn = jnp.concatenate([wk_slots, wu], axis=1).astype(BF16)
    wqT = wq.T.astype(BF16)
    wvT = wv.T.astype(BF16)
    const = lambda b, i: (0, 0)
    return pl.pallas_call(
        _even_in_kernel,
        grid=(B, S // tm),
        in_specs=[
            pl.BlockSpec((1, tm, D), lambda b, i: (b, i, 0)),
            pl.BlockSpec((1, D), const),
            pl.BlockSpec((D, MOBA_HEADS * LANES + MIX_WIDTH), const),
            pl.BlockSpec((MIX_WIDTH, D), const),
            pl.BlockSpec((MIX_WIDTH, D), const),
            pl.BlockSpec((tm, MOBA_HEADS * LANES), lambda b, i: (i, 0)),
        ],
        out_specs=[
            pl.BlockSpec((1, tm, MOBA_HEADS * LANES), lambda b, i: (b, i, 0)),
            pl.BlockSpec((1, tm, MIX_WIDTH), lambda b, i: (b, i, 0)),
            pl.BlockSpec((1, tm // MOBA_BLOCK, 1, MOBA_HEADS * LANES), lambda b, i: (b, i, 0, 0)),
            pl.BlockSpec((1, MIX_WIDTH, tm), lambda b, i: (b, 0, i)),
            pl.BlockSpec((1, tm // ATTN_BLOCK, MIX_WIDTH, ATTN_BLOCK), lambda b, i: (b, i, 0, 0)),
        ],
        out_shape=[
            jax.ShapeDtypeStruct((B, S, MOBA_HEADS * LANES), BF16),
            jax.ShapeDtypeStruct((B, S, MIX_WIDTH), F32),
            jax.ShapeDtypeStruct((B, nb, 1, MOBA_HEADS * LANES), F32),
            jax.ShapeDtypeStruct((B, MIX_WIDTH, S), F32),
            jax.ShapeDtypeStruct((B, S // ATTN_BLOCK, MIX_WIDTH, ATTN_BLOCK), BF16),
        ],
        compiler_params=_params(("parallel", "parallel")),
        name="even_in",
    )(x, g.reshape(1, D), wn, wqT, wvT, _moba_key_extras(S, slopes))


def _moba_gate_kernel(km_ref, qT_ref, qa_ref):
    i = pl.program_id(1)
    nb = km_ref.shape[2]
    tq = qT_ref.shape[2]
    row = lax.broadcasted_iota(jnp.int32, (nb, tq), 0)
    own = (i * tq + lax.broadcasted_iota(jnp.int32, (nb, tq), 1)) // MOBA_BLOCK
    past = row < own
    tail_row = lax.broadcasted_iota(jnp.int32, (LANES - KAUG_POS, tq), 0)
    tail = (tail_row < 2).astype(BF16)
    pad = jnp.zeros((MOBA_MAX_BLOCKS - nb, tq), BF16) if nb < MOBA_MAX_BLOCKS else None
    for h in range(MOBA_HEADS):
        q_h = qT_ref[0, h * MOBA_HEAD_DIM:(h + 1) * MOBA_HEAD_DIM, :]
        gate = jnp.dot(km_ref[0, h], q_h, precision=HIGHEST, preferred_element_type=F32)
        gate = jnp.where(past, gate, NEG_INF)
        chosen = jnp.zeros(gate.shape, F32)
        for _ in range(MOBA_TOPK):
            mx = jnp.max(gate, axis=0, keepdims=True)
            first = jnp.min(jnp.where(gate == mx, row, nb), axis=0, keepdims=True)
            pick = row == first
            chosen = jnp.where(pick, 1.0, chosen)
            gate = jnp.where(pick, -jnp.inf, gate)
        keep = jnp.where(past, chosen, (row == own).astype(F32))
        sel = jnp.where(keep > 0.0, 0.0, NEG_INF).astype(BF16)
        base = h * LANES
        qa_ref[0, base:base + KAUG_SEL] = (q_h * MOBA_HEAD_DIM ** -0.5).astype(BF16)
        qa_ref[0, base + KAUG_SEL:base + KAUG_SEL + nb] = sel
        if pad is not None:
            qa_ref[0, base + KAUG_SEL + nb:base + KAUG_POS] = pad
        qa_ref[0, base + KAUG_POS:base + LANES] = tail


def _moba_gate(kmean, qT):
    B, H, nb, dh = kmean.shape
    S = qT.shape[2]
    tq = ATTN_BLOCK
    return pl.pallas_call(
        _moba_gate_kernel,
        grid=(B, S // tq),
        in_specs=[
            pl.BlockSpec((1, H, nb, dh), lambda b, i: (b, 0, 0, 0)),
            pl.BlockSpec((1, H * dh, tq), lambda b, i: (b, 0, i)),
        ],
        out_specs=pl.BlockSpec((1, H * LANES, tq), lambda b, i: (b, 0, i)),
        out_shape=jax.ShapeDtypeStruct((B, H * LANES, S), BF16),
        compiler_params=_params(("parallel", "parallel")),
        name="moba_gate",
    )(kmean, qT)


HEADS_PER_STEP = 2


def _attn_kernel(q_ref, k_ref, v_ref, o_ref, sa_ref, sb_ref, m_ref, l_ref, acc_ref):
    i = pl.program_id(2)
    tq = q_ref.shape[2]
    tk = ATTN_BLOCK
    hp = HEADS_PER_STEP
    dv = v_ref.shape[2] // hp
    m_ref[...] = jnp.full(m_ref.shape, NEG_INF, F32)
    l_ref[...] = jnp.zeros(l_ref.shape, F32)
    acc_ref[...] = jnp.zeros(acc_ref.shape, F32)

    def scores(kvt, s_ref, diag):
        start = pl.multiple_of(kvt * tk, tk)
        k_tile = k_ref[0, pl.ds(start, tk), :]
        for g in range(hp):
            s = _dot(k_tile[:, g * LANES:(g + 1) * LANES], q_ref[0, g * LANES:(g + 1) * LANES, :])
            if diag:
                key = lax.broadcasted_iota(jnp.int32, (tk, tq), 0)
                qry = lax.broadcasted_iota(jnp.int32, (tk, tq), 1)
                s = jnp.where(key <= qry, s, NEG_INF)
            s_ref[g] = s

    def consume(kvt, s_ref):
        v_tile = v_ref[0, kvt]
        for g in range(hp):
            s = s_ref[g]
            m_run = m_ref[g]
            m_new = jnp.maximum(m_run, jnp.max(s, axis=0, keepdims=True))
            alpha = jnp.exp(m_run - m_new)
            p = jnp.exp(s - m_new)
            l_ref[g] = alpha * l_ref[g] + jnp.sum(p, axis=0, keepdims=True)
            acc_ref[g] = alpha * acc_ref[g] + _dot(v_tile[g * dv:(g + 1) * dv], p.astype(BF16))
            m_ref[g] = m_new

    tile_at = lambda t: jnp.where(t == 0, i, t - 1)
    scores(i, sa_ref, True)

    def pair(p, carry):
        t = 2 * p
        scores(tile_at(t + 1), sb_ref, False)
        consume(tile_at(t), sa_ref)
        scores(tile_at(t + 2), sa_ref, False)
        consume(tile_at(t + 1), sb_ref)
        return carry

    lax.fori_loop(0, i // 2, pair, 0)
    last = 2 * (i // 2)

    @pl.when(i % 2 == 1)
    def _():
        scores(tile_at(last + 1), sb_ref, False)
        consume(tile_at(last), sa_ref)
        consume(tile_at(last + 1), sb_ref)

    @pl.when(i % 2 == 0)
    def _():
        consume(tile_at(last), sa_ref)

    for g in range(hp):
        o_ref[0, g * dv:(g + 1) * dv, :] = acc_ref[g] / l_ref[g]


def _attention(qT, k, vT, dv, heads, name):
    B, _, S = qT.shape
    tq = ATTN_BLOCK
    nt = S // ATTN_BLOCK
    hp = HEADS_PER_STEP
    return pl.pallas_call(
        _attn_kernel,
        grid=(B, heads // hp, S // tq),
        in_specs=[
            pl.BlockSpec((1, hp * LANES, tq), lambda b, h, i: (b, h, i)),
            pl.BlockSpec((1, S, hp * LANES), lambda b, h, i: (b, 0, h)),
            pl.BlockSpec((1, nt, hp * dv, ATTN_BLOCK), lambda b, h, i: (b, 0, h, 0)),
        ],
        out_specs=pl.BlockSpec((1, hp * dv, tq), lambda b, h, i: (b, h, i)),
        out_shape=jax.ShapeDtypeStruct((B, heads * dv, S), F32),
        scratch_shapes=[pltpu.VMEM((hp, ATTN_BLOCK, tq), F32), pltpu.VMEM((hp, ATTN_BLOCK, tq), F32),
                        pltpu.VMEM((hp, 1, tq), F32), pltpu.VMEM((hp, 1, tq), F32),
                        pltpu.VMEM((hp, dv, tq), F32)],
        compiler_params=_params(("parallel", "parallel", "arbitrary")),
        name=name,
    )(qT, k, vT)


def _pool_kernel(x_ref, halo_ref, w_ref, sc_ref, o_ref, xs_ref):
    i = pl.program_id(1)
    tm = x_ref.shape[1]
    x = x_ref[0]
    xs_ref[0:POOL_HALO] = jnp.where(i > 0, halo_ref[0], 0.0)
    xs_ref[POOL_HALO:POOL_HALO + tm] = x
    t = i * tm + lax.broadcasted_iota(jnp.int32, (tm, 1), 0)
    outs = []
    for g, win in enumerate(POOL_WINDOWS):
        lo, hi = g * POOL_GROUP_DIM, (g + 1) * POOL_GROUP_DIM
        xg = x[:, lo:hi]
        acc = xg
        for d in range(1, win):
            acc = acc + xs_ref[POOL_HALO - d:POOL_HALO - d + tm, lo:hi]
        count = jnp.minimum(t + 1, win).astype(F32)
        outs.append(_dot((acc / count - xg).astype(BF16), w_ref[g]))
    o_ref[0] = (jnp.concatenate(outs, axis=1) * sc_ref[...]).astype(o_ref.dtype)


def _pool(ub, pool_w, pool_scale):
    B, S, W = ub.shape
    tm = ROW_TILE
    per = tm // POOL_HALO
    return pl.pallas_call(
        _pool_kernel,
        grid=(B, S // tm),
        in_specs=[
            pl.BlockSpec((1, tm, W), lambda b, i: (b, i, 0)),
            pl.BlockSpec((1, POOL_HALO, W), lambda b, i: (b, jnp.maximum(i * per - 1, 0), 0)),
            pl.BlockSpec(pool_w.shape, lambda b, i: (0, 0, 0)),
            pl.BlockSpec((1, W), lambda b, i: (0, 0)),
        ],
        out_specs=pl.BlockSpec((1, tm, W), lambda b, i: (b, i, 0)),
        out_shape=jax.ShapeDtypeStruct((B, S, W), BF16),
        scratch_shapes=[pltpu.VMEM((POOL_HALO + tm, W), F32)],
        compiler_params=_params(("parallel", "parallel")),
        name="pool",
    )(ub, ub, pool_w.astype(BF16), pool_scale.reshape(1, W))


def _mix_out_kernel(h_ref, aT_ref, b_ref, w_ref, o_ref, *, a_first):
    a = aT_ref[0].T.astype(BF16)
    b = b_ref[0].astype(BF16)
    lo, hi = (a, b) if a_first else (b, a)
    y = _dot(lo, w_ref[:MIX_WIDTH]) + _dot(hi, w_ref[MIX_WIDTH:])
    o_ref[0] = h_ref[0] + y


def _mix_out(h, aT, b, w_out, a_first):
    B, S, D = h.shape
    tm = ROW_TILE
    return pl.pallas_call(
        functools.partial(_mix_out_kernel, a_first=a_first),
        grid=(B, S // tm),
        in_specs=[
            pl.BlockSpec((1, tm, D), lambda b_, i: (b_, i, 0)),
            pl.BlockSpec((1, MIX_WIDTH, tm), lambda b_, i: (b_, 0, i)),
            pl.BlockSpec((1, tm, MIX_WIDTH), lambda b_, i: (b_, i, 0)),
            pl.BlockSpec((2 * MIX_WIDTH, D), lambda b_, i: (0, 0)),
        ],
        out_specs=pl.BlockSpec((1, tm, D), lambda b_, i: (b_, i, 0)),
        out_shape=jax.ShapeDtypeStruct((B, S, D), F32),
        compiler_params=_params(("parallel", "parallel")),
        name="mix_out",
    )(h, aT, b, w_out.astype(BF16))


def _swiglu_step(xn, wg, wu, wd):
    gt = _dot(xn, wg)
    up = _dot(xn, wu)
    return _dot((gt * _sigmoid(gt) * up).astype(BF16), wd)


def _ffn_kernel(h_ref, g_ref, wg_ref, wu_ref, wd_ref, o_ref, xn_ref, acc_ref):
    f = pl.program_id(1)

    @pl.when(f == 0)
    def _():
        xn_ref[...] = _rms(h_ref[...], g_ref[...]).astype(BF16)
        acc_ref[...] = jnp.zeros_like(acc_ref)

    acc_ref[...] += _swiglu_step(xn_ref[...], wg_ref[...], wu_ref[...], wd_ref[...])

    @pl.when(f == pl.num_programs(1) - 1)
    def _():
        o_ref[...] = h_ref[...] + acc_ref[...]


def _ffn(h2d, g, wg, wu, wd, tm=ROW_TILE, tf=FFN_DIM // 2):
    T, D = h2d.shape
    F = wg.shape[1]
    return pl.pallas_call(
        _ffn_kernel,
        grid=(T // tm, F // tf),
        in_specs=[
            pl.BlockSpec((tm, D), lambda i, f: (i, 0)),
            pl.BlockSpec((1, D), lambda i, f: (0, 0)),
            pl.BlockSpec((D, tf), lambda i, f: (0, f)),
            pl.BlockSpec((D, tf), lambda i, f: (0, f)),
            pl.BlockSpec((tf, D), lambda i, f: (f, 0)),
        ],
        out_specs=pl.BlockSpec((tm, D), lambda i, f: (i, 0)),
        out_shape=jax.ShapeDtypeStruct((T, D), F32),
        scratch_shapes=[pltpu.VMEM((tm, D), BF16), pltpu.VMEM((tm, D), F32)],
        compiler_params=_params(("parallel", "arbitrary")),
        name="ffn",
    )(h2d, g.reshape(1, D), wg, wu, wd)


ROUTE_E0, ROUTE_E1, ROUTE_W0, ROUTE_W1, ROUTE_R0, ROUTE_R1 = range(6)
MOE_TILE = 512
MOE_TF = EXPERT_DIM // 4


def _lane_pick(tile, lane, idx):
    return jnp.sum(jnp.where(lane == idx, tile, 0.0), axis=1, keepdims=True)


def _router_kernel(h_ref, g_ref, w_ref, b_ref, route_ref, cnt_ref):
    tm = h_ref.shape[0]

    @pl.when(pl.program_id(0) == 0)
    def _():
        cnt_ref[...] = jnp.zeros_like(cnt_ref)

    xn = _rms(h_ref[...], g_ref[...])
    logits = jnp.dot(xn, w_ref[...], precision=HIGHEST, preferred_element_type=F32) + b_ref[...]
    lane = lax.broadcasted_iota(jnp.int32, logits.shape, 1)
    logits = jnp.where(lane < N_EXPERTS, logits, -jnp.inf)
    v0 = jnp.max(logits, axis=1, keepdims=True)
    i0 = jnp.min(jnp.where(logits == v0, lane, LANES), axis=1, keepdims=True)
    rest = jnp.where(lane == i0, -jnp.inf, logits)
    v1 = jnp.max(rest, axis=1, keepdims=True)
    i1 = jnp.min(jnp.where(rest == v1, lane, LANES), axis=1, keepdims=True)
    e1 = jnp.exp(v1 - v0)
    w0 = 1.0 / (1.0 + e1)
    sel = (lane == i0).astype(F32) + (lane == i1).astype(F32)
    earlier = (lax.broadcasted_iota(jnp.int32, (tm, tm), 1)
               < lax.broadcasted_iota(jnp.int32, (tm, tm), 0))
    rank = _dot(earlier.astype(BF16), sel.astype(BF16)) + cnt_ref[...]
    cnt_ref[...] += jnp.sum(sel, axis=0, keepdims=True)
    cols = ((ROUTE_E0, i0.astype(F32)), (ROUTE_E1, i1.astype(F32)), (ROUTE_W0, w0), (ROUTE_W1, e1 * w0),
            (ROUTE_R0, _lane_pick(rank, lane, i0)), (ROUTE_R1, _lane_pick(rank, lane, i1)))
    route = jnp.zeros(logits.shape, F32)
    for c, val in cols:
        route = jnp.where(lane == c, val, route)
    route_ref[...] = route


def _router(h2d, g, router_w, router_b):
    T, D = h2d.shape
    tm = ROW_TILE
    w = jnp.zeros((D, LANES), F32).at[:, :N_EXPERTS].set(router_w)
    b = jnp.zeros((1, LANES), F32).at[0, :N_EXPERTS].set(router_b)
    return pl.pallas_call(
        _router_kernel,
        grid=(T // tm,),
        in_specs=[
            pl.BlockSpec((tm, D), lambda i: (i, 0)),
            pl.BlockSpec((1, D), lambda i: (0, 0)),
            pl.BlockSpec((D, LANES), lambda i: (0, 0)),
            pl.BlockSpec((1, LANES), lambda i: (0, 0)),
        ],
        out_specs=[pl.BlockSpec((tm, LANES), lambda i: (i, 0)),
                   pl.BlockSpec((1, LANES), lambda i: (0, 0))],
        out_shape=[jax.ShapeDtypeStruct((T, LANES), F32), jax.ShapeDtypeStruct((1, LANES), F32)],
        compiler_params=_params(("arbitrary",)),
        name="router",
    )(h2d, g.reshape(1, D), w, b)


def _row_copies(pos_ref, base, r, src_of, dst_of, sem):
    return [pltpu.make_async_copy(src_of(k, pos_ref[base + 2 * r + k]),
                                  dst_of(k, pos_ref[base + 2 * r + k]), sem) for k in range(2)]


def _all_rows(tm, make):
    def issue(r, c):
        for cp in make(r):
            cp.start()
        return c

    def drain(r, c):
        for cp in make(r):
            cp.wait()
        return c

    lax.fori_loop(0, tm, issue, 0, unroll=8)
    lax.fori_loop(0, tm, drain, 0, unroll=8)


def _dispatch_kernel(pos_ref, h_ref, g_ref, init_ref, xs_ref, xn_ref, sem):
    del init_ref
    tm = h_ref.shape[0]
    xn_ref[...] = _rms(h_ref[...], g_ref[...])
    base = pl.program_id(0) * (2 * tm)
    _all_rows(tm, lambda r: _row_copies(
        pos_ref, base, r, lambda k, p: xn_ref.at[pl.ds(r, 1)], lambda k, p: xs_ref.at[pl.ds(p, 1)], sem))


def _dispatch(pos, h2d, g, n_rows):
    T, D = h2d.shape
    tm = ROW_TILE
    return pl.pallas_call(
        _dispatch_kernel,
        grid_spec=pltpu.PrefetchScalarGridSpec(
            num_scalar_prefetch=1,
            grid=(T // tm,),
            in_specs=[pl.BlockSpec((tm, D), lambda i, pos: (i, 0)),
                      pl.BlockSpec((1, D), lambda i, pos: (0, 0)),
                      pl.BlockSpec(memory_space=pl.ANY)],
            out_specs=pl.BlockSpec(memory_space=pl.ANY),
            scratch_shapes=[pltpu.VMEM((tm, D), F32), pltpu.SemaphoreType.DMA(())],
        ),
        out_shape=jax.ShapeDtypeStruct((n_rows, D), F32),
        input_output_aliases={3: 0},
        compiler_params=_params(("arbitrary",)),
        name="moe_dispatch",
    )(pos, h2d, g.reshape(1, D), jnp.zeros((n_rows, D), F32))


def _moe_ffn_kernel(te_ref, nv_ref, x_ref, wg_ref, wu_ref, wd_ref, o_ref, xb_ref, acc_ref):
    j = pl.program_id(0)
    f = pl.program_id(1)
    valid = j < nv_ref[0]

    @pl.when(f == 0)
    def _():
        xb_ref[...] = x_ref[...].astype(BF16)
        acc_ref[...] = jnp.zeros_like(acc_ref)

    @pl.when(valid)
    def _():
        acc_ref[...] += _swiglu_step(xb_ref[...], wg_ref[0], wu_ref[0], wd_ref[0])

    @pl.when(f == pl.num_programs(1) - 1)
    def _():
        o_ref[...] = acc_ref[...]


def _moe_ffn(tile_expert, n_valid, xs, wg, wu, wd):
    N, D = xs.shape
    tm, tf = MOE_TILE, MOE_TF
    F = wg.shape[2]
    return pl.pallas_call(
        _moe_ffn_kernel,
        grid_spec=pltpu.PrefetchScalarGridSpec(
            num_scalar_prefetch=2,
            grid=(N // tm, F // tf),
            in_specs=[pl.BlockSpec((tm, D), lambda j, f, te, nv: (j, 0)),
                      pl.BlockSpec((1, D, tf), lambda j, f, te, nv: (te[j], 0, f)),
                      pl.BlockSpec((1, D, tf), lambda j, f, te, nv: (te[j], 0, f)),
                      pl.BlockSpec((1, tf, D), lambda j, f, te, nv: (te[j], f, 0))],
            out_specs=pl.BlockSpec((tm, D), lambda j, f, te, nv: (j, 0)),
            scratch_shapes=[pltpu.VMEM((tm, D), BF16), pltpu.VMEM((tm, D), F32)],
        ),
        out_shape=jax.ShapeDtypeStruct((N, D), F32),
        compiler_params=_params(("arbitrary", "arbitrary")),
        name="moe_ffn",
    )(tile_expert, n_valid, xs, wg, wu, wd)


def _combine_kernel(pos_ref, h_ref, route_ref, ys_ref, o_ref, y_ref, sem):
    tm = h_ref.shape[0]
    base = pl.program_id(0) * (2 * tm)
    _all_rows(tm, lambda r: _row_copies(
        pos_ref, base, r, lambda k, p: ys_ref.at[pl.ds(p, 1)], lambda k, p: y_ref.at[k, pl.ds(r, 1)], sem))
    route = route_ref[...]
    lane = lax.broadcasted_iota(jnp.int32, route.shape, 1)
    o_ref[...] = (h_ref[...] + _lane_pick(route, lane, ROUTE_W0) * y_ref[0]
                  + _lane_pick(route, lane, ROUTE_W1) * y_ref[1])


def _combine(pos, h2d, route, ys):
    T, D = h2d.shape
    tm = ROW_TILE
    return pl.pallas_call(
        _combine_kernel,
        grid_spec=pltpu.PrefetchScalarGridSpec(
            num_scalar_prefetch=1,
            grid=(T // tm,),
            in_specs=[pl.BlockSpec((tm, D), lambda i, pos: (i, 0)),
                      pl.BlockSpec((tm, LANES), lambda i, pos: (i, 0)),
                      pl.BlockSpec(memory_space=pl.ANY)],
            out_specs=pl.BlockSpec((tm, D), lambda i, pos: (i, 0)),
            scratch_shapes=[pltpu.VMEM((2, tm, D), F32), pltpu.SemaphoreType.DMA(())],
        ),
        out_shape=jax.ShapeDtypeStruct((T, D), F32),
        compiler_params=_params(("arbitrary",)),
        name="moe_combine",
    )(pos, h2d, route, ys)


def _moe(h2d, g, router_w, router_b, wg, wu, wd):
    T, D = h2d.shape
    tm = MOE_TILE
    route, counts = _router(h2d, g, router_w, router_b)
    cnt = counts[0, :N_EXPERTS].astype(jnp.int32)
    padded = (cnt + tm - 1) // tm * tm
    ends = jnp.cumsum(padded)
    start = ends - padded
    e01 = route[:, ROUTE_E0:ROUTE_E1 + 1].astype(jnp.int32)
    r01 = route[:, ROUTE_R0:ROUTE_R1 + 1].astype(jnp.int32)
    pos = (start[e01] + r01).reshape(2 * T)
    n_rows = 2 * T + N_EXPERTS * tm
    tile_row = jnp.arange(n_rows // tm, dtype=jnp.int32) * tm
    tile_expert = jnp.minimum(jnp.sum(tile_row[:, None] >= ends[None, :], axis=1), N_EXPERTS - 1).astype(jnp.int32)
    n_valid = (ends[-1:] // tm).astype(jnp.int32)
    xs = _dispatch(pos, h2d, g, n_rows)
    ys = _moe_ffn(tile_expert, n_valid, xs, wg, wu, wd)
    return _combine(pos, h2d, route, ys)


def _ple_kernel(*refs, final):
    if final:
        h_ref, p_ref, g_ref, wg_ref, wp_ref, fg_ref, o_ref = refs
    else:
        h_ref, p_ref, g_ref, wg_ref, wp_ref, o_ref = refs
    h = h_ref[...]
    gate = _sigmoid(_dot(_rms(h, g_ref[...]).astype(BF16), wg_ref[...]))
    out = h + gate * _dot(p_ref[...].astype(BF16), wp_ref[...])
    if final:
        out = _rms(out, fg_ref[...])
    o_ref[...] = out


def _ple(h2d, p2d, g, w_gate, w_proj, final_g=None):
    T, D = h2d.shape
    tm = ROW_TILE
    final = final_g is not None
    in_specs = [
        pl.BlockSpec((tm, D), lambda i: (i, 0)),
        pl.BlockSpec((tm, PLE_DIM), lambda i: (i, 0)),
        pl.BlockSpec((1, D), lambda i: (0, 0)),
        pl.BlockSpec((D, D), lambda i: (0, 0)),
        pl.BlockSpec((PLE_DIM, D), lambda i: (0, 0)),
    ]
    args = [h2d, p2d, g.reshape(1, D), w_gate.astype(BF16), w_proj.astype(BF16)]
    if final:
        in_specs.append(pl.BlockSpec((1, D), lambda i: (0, 0)))
        args.append(final_g.reshape(1, D))
    return pl.pallas_call(
        functools.partial(_ple_kernel, final=final),
        grid=(T // tm,),
        in_specs=in_specs,
        out_specs=pl.BlockSpec((tm, D), lambda i: (i, 0)),
        out_shape=jax.ShapeDtypeStruct((T, D), F32),
        compiler_params=_params(("parallel",)),
        name="ple_final" if final else "ple",
    )(*args)


ODD_MAIN = 4 * MIX_WIDTH
ODD_COLS = ODD_MAIN + MLA_Q_RANK + MLA_KV_RANK + 2 * LANES
MLA_QK_SCALE = (MLA_NOPE_DIM + MLA_ROPE_DIM) ** -0.5
ROPE_HALF = MLA_ROPE_DIM // 2


def _odd_in_kernel(x_ref, g_ref, wn_ref, qn_ref, wuqT_ref, kvn_ref, wk2_ref, wvT_ref,
                   cosT_ref, sinT_ref, cc_ref, ss_ref,
                   qk_ref, vc_ref, op_ref, misc_ref, mq_ref, mk_ref, mv_ref):
    tm = x_ref.shape[1]
    xn = _rms(x_ref[0], g_ref[...]).astype(BF16)
    u = _dot(xn, wn_ref[...])
    qk_ref[0] = u[:, :2 * MIX_WIDTH]
    vc_ref[0] = u[:, 2 * MIX_WIDTH:3 * MIX_WIDTH].astype(BF16)
    op_ref[0] = u[:, 3 * MIX_WIDTH:ODD_MAIN]
    c0 = ODD_MAIN
    c_q = u[:, c0:c0 + MLA_Q_RANK]
    c0 += MLA_Q_RANK
    c_kv = u[:, c0:c0 + MLA_KV_RANK]
    c0 += MLA_KV_RANK
    misc = u[:, c0:c0 + LANES]
    misc_sw = u[:, c0 + LANES:c0 + 2 * LANES]
    misc_ref[0] = misc
    cqn = _rms(c_q, qn_ref[...]).astype(BF16)
    qT = _dot_nt(wuqT_ref[...], cqn)
    cosT = cosT_ref[...]
    sinT = sinT_ref[...]
    for h in range(MLA_HEADS):
        r = h * LANES
        mq_ref[0, r:r + MLA_NOPE_DIM] = (qT[r:r + MLA_NOPE_DIM] * MLA_QK_SCALE).astype(BF16)
        x1 = qT[r + MLA_NOPE_DIM:r + MLA_NOPE_DIM + ROPE_HALF]
        x2 = qT[r + MLA_NOPE_DIM + ROPE_HALF:r + MLA_NOPE_DIM + MLA_ROPE_DIM]
        mq_ref[0, r + MLA_NOPE_DIM:r + MLA_NOPE_DIM + ROPE_HALF] = (
            (x1 * cosT - x2 * sinT) * MLA_QK_SCALE).astype(BF16)
        mq_ref[0, r + MLA_NOPE_DIM + ROPE_HALF:r + MLA_NOPE_DIM + MLA_ROPE_DIM] = (
            (x1 * sinT + x2 * cosT) * MLA_QK_SCALE).astype(BF16)
        mq_ref[0, r + MLA_NOPE_DIM + MLA_ROPE_DIM:r + LANES] = jnp.zeros(
            (LANES - MLA_NOPE_DIM - MLA_ROPE_DIM, tm), BF16)
    ckvn = _rms(c_kv, kvn_ref[...]).astype(BF16)
    k_rot = (misc * cc_ref[...] + misc_sw * ss_ref[...]).astype(BF16)
    mk_ref[0] = _dot(jnp.concatenate([ckvn, k_rot], axis=1), wk2_ref[...]).astype(BF16)
    vT = _dot_nt(wvT_ref[...], ckvn)
    for j in range(tm // ATTN_BLOCK):
        mv_ref[0, j] = vT[:, j * ATTN_BLOCK:(j + 1) * ATTN_BLOCK].astype(BF16)


def _rope_tables(S):
    inv_freq = ROPE_BASE ** (-jnp.arange(ROPE_HALF, dtype=F32) / ROPE_HALF)
    ang = jnp.arange(S, dtype=F32)[:, None] * inv_freq[None, :]
    cos, sin = jnp.cos(ang), jnp.sin(ang)
    pad = jnp.zeros((S, LANES - MLA_ROPE_DIM), F32)
    cc = jnp.concatenate([cos, cos, pad], axis=1)
    ss = jnp.concatenate([-sin, sin, pad], axis=1)
    return cos.T, sin.T, cc, ss


def _odd_in(x, g, w_in, q_norm, w_uq, kv_norm, w_ukv):
    B, S, D = x.shape
    tm = ROW_TILE
    cuts = np.cumsum([MIX_WIDTH] * 4 + [MLSTM_HEADS, MLSTM_HEADS, MLA_Q_RANK, MLA_KV_RANK]).tolist()
    w_main = w_in[:, :cuts[3]]
    w_i = w_in[:, cuts[3]:cuts[4]]
    w_f = w_in[:, cuts[4]:cuts[5]]
    w_cq = w_in[:, cuts[5]:cuts[6]]
    w_ckv = w_in[:, cuts[6]:cuts[7]]
    w_kr = w_in[:, cuts[7]:]
    w_kr_sw = jnp.concatenate([w_kr[:, ROPE_HALF:], w_kr[:, :ROPE_HALF]], axis=1)
    zpad = lambda n: jnp.zeros((D, n), F32)
    w_misc = jnp.concatenate([w_kr, w_i, w_f, zpad(LANES - MLA_ROPE_DIM - 2 * MLSTM_HEADS)], axis=1)
    w_misc_sw = jnp.concatenate([w_kr_sw, zpad(LANES - MLA_ROPE_DIM)], axis=1)
    wn = jnp.concatenate([w_main, w_cq, w_ckv, w_misc, w_misc_sw], axis=1).astype(BF16)
    qd = MLA_NOPE_DIM + MLA_ROPE_DIM
    w_uq_h = w_uq.reshape(MLA_Q_RANK, MLA_HEADS, qd)
    w_uq_h = jnp.concatenate([w_uq_h, jnp.zeros((MLA_Q_RANK, MLA_HEADS, LANES - qd), F32)], axis=2)
    wuqT = w_uq_h.reshape(MLA_Q_RANK, MLA_HEADS * LANES).T.astype(BF16)
    w_ukv_h = w_ukv.reshape(MLA_KV_RANK, MLA_HEADS, MLA_NOPE_DIM + MLA_V_DIM)
    w_k = jnp.concatenate([w_ukv_h[:, :, :MLA_NOPE_DIM],
                           jnp.zeros((MLA_KV_RANK, MLA_HEADS, LANES - MLA_NOPE_DIM), F32)], axis=2)
    place = jnp.zeros((LANES, MLA_HEADS, LANES), F32)
    eye = jnp.eye(MLA_ROPE_DIM, dtype=F32)
    place = place.at[:MLA_ROPE_DIM, :, MLA_NOPE_DIM:MLA_NOPE_DIM + MLA_ROPE_DIM].set(
        jnp.broadcast_to(eye[:, None, :], (MLA_ROPE_DIM, MLA_HEADS, MLA_ROPE_DIM)))
    wk2 = jnp.concatenate([w_k, place], axis=0).reshape(MLA_KV_RANK + LANES, MLA_HEADS * LANES).astype(BF16)
    wvT = w_ukv_h[:, :, MLA_NOPE_DIM:].reshape(MLA_KV_RANK, MLA_HEADS * MLA_V_DIM).T.astype(BF16)
    cosT, sinT, cc, ss = _rope_tables(S)
    row = lambda b, i: (b, i, 0)
    const = lambda b, i: (0, 0)
    nb = S // ATTN_BLOCK
    return pl.pallas_call(
        _odd_in_kernel,
        grid=(B, S // tm),
        in_specs=[
            pl.BlockSpec((1, tm, D), row),
            pl.BlockSpec((1, D), const),
            pl.BlockSpec((D, ODD_COLS), const),
            pl.BlockSpec((1, MLA_Q_RANK), const),
            pl.BlockSpec((MLA_HEADS * LANES, MLA_Q_RANK), const),
            pl.BlockSpec((1, MLA_KV_RANK), const),
            pl.BlockSpec((MLA_KV_RANK + LANES, MLA_HEADS * LANES), const),
            pl.BlockSpec((MLA_HEADS * MLA_V_DIM, MLA_KV_RANK), const),
            pl.BlockSpec((ROPE_HALF, tm), lambda b, i: (0, i)),
            pl.BlockSpec((ROPE_HALF, tm), lambda b, i: (0, i)),
            pl.BlockSpec((tm, LANES), lambda b, i: (i, 0)),
            pl.BlockSpec((tm, LANES), lambda b, i: (i, 0)),
        ],
        out_specs=[
            pl.BlockSpec((1, tm, 2 * MIX_WIDTH), row),
            pl.BlockSpec((1, tm, MIX_WIDTH), row),
            pl.BlockSpec((1, tm, MIX_WIDTH), row),
            pl.BlockSpec((1, tm, LANES), row),
            pl.BlockSpec((1, MLA_HEADS * LANES, tm), lambda b, i: (b, 0, i)),
            pl.BlockSpec((1, tm, MLA_HEADS * LANES), row),
            pl.BlockSpec((1, tm // ATTN_BLOCK, MLA_HEADS * MLA_V_DIM, ATTN_BLOCK), lambda b, i: (b, i, 0, 0)),
        ],
        out_shape=[
            jax.ShapeDtypeStruct((B, S, 2 * MIX_WIDTH), F32),
            jax.ShapeDtypeStruct((B, S, MIX_WIDTH), BF16),
            jax.ShapeDtypeStruct((B, S, MIX_WIDTH), F32),
            jax.ShapeDtypeStruct((B, S, LANES), F32),
            jax.ShapeDtypeStruct((B, MLA_HEADS * LANES, S), BF16),
            jax.ShapeDtypeStruct((B, S, MLA_HEADS * LANES), BF16),
            jax.ShapeDtypeStruct((B, nb, MLA_HEADS * MLA_V_DIM, ATTN_BLOCK), BF16),
        ],
        compiler_params=_params(("parallel", "parallel")),
        name="odd_in",
    )(x, g.reshape(1, D), wn, q_norm.reshape(1, -1), wuqT, kv_norm.reshape(1, -1), wk2, wvT,
      cosT, sinT, cc, ss)


def _log_sigmoid(x):
    return jnp.minimum(x, 0.0) - jnp.log(1.0 + jnp.exp(-jnp.abs(x)))


def _mlstm_kernel(qk_ref, v_ref, op_ref, misc_ref, cw_ref, gb_ref, hn_ref, o_ref,
                  prev_ref, cn_ref, m_ref):
    c = pl.program_id(1)
    L = qk_ref.shape[1]
    d = MLSTM_HEAD_DIM

    @pl.when(c == 0)
    def _():
        prev_ref[...] = jnp.zeros_like(prev_ref)
        cn_ref[...] = jnp.zeros_like(cn_ref)
        m_ref[...] = jnp.zeros_like(m_ref)

    x = qk_ref[0]
    prev = prev_ref[...]
    row = lax.broadcasted_iota(jnp.int32, (L, 1), 0)
    conv = x * cw_ref[CONV_WIDTH - 1:CONV_WIDTH, :]
    for j in range(1, CONV_WIDTH):
        shifted = pltpu.roll(jnp.where(row >= L - j, prev, x), j, axis=0)
        conv = conv + shifted * cw_ref[CONV_WIDTH - 1 - j:CONV_WIDTH - j, :]
    prev_ref[...] = x
    qk = conv * _sigmoid(conv)

    gates = misc_ref[0] + gb_ref[...]
    lane = lax.broadcasted_iota(jnp.int32, (L, LANES), 1)
    is_f = (lane >= MISC_F) & (lane < MISC_F + MLSTM_HEADS)
    z = jnp.where(is_f, _log_sigmoid(gates), gates)
    tri_r = lax.broadcasted_iota(jnp.int32, (L, L), 0)
    tri_c = lax.broadcasted_iota(jnp.int32, (L, L), 1)
    causal = tri_c <= tri_r
    cum = jnp.dot(causal.astype(F32), z, precision=HIGHEST, preferred_element_type=F32)
    z = jnp.where(is_f, cum, z)
    zT = z.T

    ones_col = (lax.broadcasted_iota(jnp.int32, (L, LANES), 1) == 0).astype(BF16)
    for h in range(MLSTM_HEADS):
        lo, hi = h * d, (h + 1) * d
        q = qk[:, lo:hi].astype(BF16)
        k = qk[:, MIX_WIDTH + lo:MIX_WIDTH + hi] * (d ** -0.5)
        v_aug = jnp.concatenate([v_ref[0, :, lo:hi], ones_col], axis=1)
        i_col = z[:, MISC_I + h:MISC_I + h + 1]
        b_col = z[:, MISC_F + h:MISC_F + h + 1]
        i_row = zT[MISC_I + h:MISC_I + h + 1, :]
        b_row = zT[MISC_F + h:MISC_F + h + 1, :]
        m_prev = m_ref[h:h + 1, 0:1]
        intra = jnp.where(causal, b_col - b_row + i_row, NEG_INF)
        m_inter = b_col + m_prev
        m_t = jnp.maximum(m_inter, jnp.max(intra, axis=1, keepdims=True))
        w_inter = jnp.exp(m_inter - m_t)
        a = jnp.exp(intra - m_t) * _dot_nt(q, k.astype(BF16))
        inter = _dot(q, cn_ref[h].astype(BF16))
        intra_o = _dot(a.astype(BF16), v_aug)
        num = w_inter * inter[:, :d] + intra_o[:, :d]
        den = w_inter * inter[:, d:d + 1] + intra_o[:, d:d + 1]
        hh = num / jnp.maximum(jnp.abs(den), jnp.exp(-m_t))
        hh = _rms(hh, hn_ref[:, lo:hi])
        o_ref[0, :, lo:hi] = (hh * _sigmoid(op_ref[0, :, lo:hi])).astype(o_ref.dtype)
        b_end = b_col[L - 1:L, :]
        g_col = b_end - b_col + i_col
        m_new = jnp.maximum(b_end + m_prev, jnp.max(g_col, axis=0, keepdims=True))
        decay = jnp.exp(b_end + m_prev - m_new)
        kw = k * jnp.exp(g_col - m_new)
        cn_ref[h] = decay * cn_ref[h] + _dot(kw.T.astype(BF16), v_aug)
        m_ref[h:h + 1, :] = jnp.broadcast_to(m_new, (1, LANES))


def _mlstm(qk_raw, vc, o_pre, misc, conv_w, b_i, b_f, head_norm):
    B, S, _ = qk_raw.shape
    L = MLSTM_CHUNK
    gb = jnp.zeros((1, LANES), F32).at[0, MISC_I:MISC_I + MLSTM_HEADS].set(b_i)
    gb = gb.at[0, MISC_F:MISC_F + MLSTM_HEADS].set(b_f)
    row = lambda b, c: (b, c, 0)
    const = lambda b, c: (0, 0)
    return pl.pallas_call(
        _mlstm_kernel,
        grid=(B, S // L),
        in_specs=[
            pl.BlockSpec((1, L, 2 * MIX_WIDTH), row),
            pl.BlockSpec((1, L, MIX_WIDTH), row),
            pl.BlockSpec((1, L, MIX_WIDTH), row),
            pl.BlockSpec((1, L, LANES), row),
            pl.BlockSpec((CONV_WIDTH, 2 * MIX_WIDTH), const),
            pl.BlockSpec((1, LANES), const),
            pl.BlockSpec((1, MIX_WIDTH), const),
        ],
        out_specs=pl.BlockSpec((1, L, MIX_WIDTH), row),
        out_shape=jax.ShapeDtypeStruct((B, S, MIX_WIDTH), BF16),
        scratch_shapes=[
            pltpu.VMEM((L, 2 * MIX_WIDTH), F32),
            pltpu.VMEM((MLSTM_HEADS, MLSTM_HEAD_DIM, 2 * LANES), F32),
            pltpu.VMEM((8, LANES), F32),
        ],
        compiler_params=_params(("parallel", "arbitrary")),
        name="mlstm",
    )(qk_raw, vc, o_pre, misc, conv_w, gb, head_norm.reshape(1, MIX_WIDTH))


def _even_layer(h, norm_mix, w_in, pool_w, pool_scale, w_out, norm_ffn, wg, wu, wd):
    B, S, D = h.shape
    slopes = (2.0 ** (-8.0 * np.arange(1, MOBA_HEADS + 1) / MOBA_HEADS)).astype(np.float32)
    ka, ub, kmean, qT, vT = _even_in(h, norm_mix, w_in, slopes)
    nb = S // MOBA_BLOCK
    kmean = kmean.reshape(B, nb, MOBA_HEADS, LANES)[..., :MOBA_HEAD_DIM].transpose(0, 2, 1, 3)
    qaT = _moba_gate(kmean, qT)
    aT = _attention(qaT, ka, vT, MOBA_HEAD_DIM, MOBA_HEADS, "moba_attn")
    b_out = _pool(ub, pool_w, pool_scale)
    h = _mix_out(h, aT, b_out, w_out, a_first=True)
    return _ffn(h.reshape(B * S, D), norm_ffn, wg.astype(BF16), wu.astype(BF16), wd.astype(BF16))


def _odd_layer(h, norm_mix, w_in, conv_w, b_i, b_f, head_norm, q_norm, w_uq, kv_norm, w_ukv,
               w_out, norm_ffn, router_w, router_b, wg, wu, wd):
    B, S, D = h.shape
    qk_raw, vc, o_pre, misc, mqT, mk, mvT = _odd_in(h, norm_mix, w_in, q_norm, w_uq, kv_norm, w_ukv)
    c_out = _mlstm(qk_raw, vc, o_pre, misc, conv_w, b_i, b_f, head_norm)
    dT = _attention(mqT, mk, mvT, MLA_V_DIM, MLA_HEADS, "mla_attn")
    h = _mix_out(h, dT, c_out, w_out, a_first=False)
    return _moe(h.reshape(B * S, D), norm_ffn, router_w, router_b,
                wg.astype(BF16), wu.astype(BF16), wd.astype(BF16))


def kernel(x, p, ev_norm_mix, ev_w_in, pool_w, pool_scale, ev_w_out, ev_norm_ffn, ffn_w_gate, ffn_w_up, ffn_w_down, od_norm_mix, od_w_in, conv_w, gate_b_i, gate_b_f, mlstm_norm, mla_q_norm, mla_w_uq, mla_kv_norm, mla_w_ukv, od_w_out, od_norm_ffn, router_w, router_b, moe_w_gate, moe_w_up, moe_w_down, ple_norm, ple_w_gate, ple_w_proj, final_norm):
    B, S, D = x.shape
    depth = p.shape[0]
    assert D == D_MODEL and S % (2 * ROW_TILE) == 0 and S // MOBA_BLOCK >= MOBA_TOPK
    h = x
    for layer in range(depth):
        j = layer // 2
        if layer % 2 == 0:
            h2d = _even_layer(h, ev_norm_mix[j], ev_w_in[j], pool_w[j], pool_scale[j], ev_w_out[j],
                              ev_norm_ffn[j], ffn_w_gate[j], ffn_w_up[j], ffn_w_down[j])
        else:
            h2d = _odd_layer(h, od_norm_mix[j], od_w_in[j], conv_w[j], gate_b_i[j], gate_b_f[j],
                             mlstm_norm[j], mla_q_norm[j], mla_w_uq[j], mla_kv_norm[j], mla_w_ukv[j],
                             od_w_out[j], od_norm_ffn[j], router_w[j], router_b[j],
                             moe_w_gate[j], moe_w_up[j], moe_w_down[j])
        last = layer == depth - 1
        h2d = _ple(h2d, p[layer].reshape(B * S, PLE_DIM), ple_norm[layer], ple_w_gate[layer],
                   ple_w_proj[layer], final_g=final_norm if last else None)
        h = h2d.reshape(B, S, D)
    return h
```

```python
import functools
import math

import numpy as np
import jax
import jax.numpy as jnp
from jax import lax
from jax.experimental import pallas as pl
from jax.experimental.pallas import tpu as pltpu

F32 = jnp.float32
BF16 = jnp.bfloat16
HIGHEST = lax.Precision.HIGHEST

D_MODEL = 1024
PLE_DIM = 256
NORM_EPS = 1e-6
NEG_INF = -1e30

MOBA_HEADS = 8
MOBA_HEAD_DIM = 64
MOBA_BLOCK = 256
MOBA_TOPK = 3
POOL_WINDOWS = (2, 4, 8, 16)
POOL_GROUP_DIM = 128
POOL_HALO = 16
MLSTM_HEADS = 4
MLSTM_HEAD_DIM = 128
MLSTM_CHUNK = 128
CONV_WIDTH = 4
MLA_HEADS = 4
MLA_Q_RANK = 256
MLA_KV_RANK = 128
MLA_NOPE_DIM = 64
MLA_ROPE_DIM = 32
MLA_V_DIM = 128
ROPE_BASE = 10000.0
FFN_DIM = 2816
N_EXPERTS = 8
EXPERT_DIM = 3584
MIX_WIDTH = 512

ATTN_BLOCK = 512
ROW_TILE = 512
LANES = 128
VMEM_LIMIT = 56 * 1024 * 1024

MISC_ROPE = 0
MISC_I = 32
MISC_F = 36


def _params(sem, vmem=VMEM_LIMIT):
    return pltpu.CompilerParams(dimension_semantics=sem, vmem_limit_bytes=vmem)


def _rms(x, g):
    ms = jnp.mean(x * x, axis=-1, keepdims=True)
    return x * lax.rsqrt(ms + NORM_EPS) * g


def _sigmoid(x):
    return 1.0 / (1.0 + jnp.exp(-x))


def _dot(a, b):
    return jnp.dot(a, b, preferred_element_type=F32)


def _dot_nt(a, b, precision=None):
    return lax.dot_general(a, b, (((1,), (1,)), ((), ())), precision=precision,
                           preferred_element_type=F32)


KAUG_SEL = MOBA_HEAD_DIM
KAUG_POS = KAUG_SEL + 32
MOBA_MAX_BLOCKS = KAUG_POS - KAUG_SEL


def _bf16_terms(x, n):
    out = []
    for _ in range(n):
        bits = np.float32(x).view(np.uint32)
        kept = np.uint32((int(bits) + 0x7FFF + ((int(bits) >> 16) & 1)) & 0xFFFF0000)
        term = float(kept.view(np.float32))
        out.append(term)
        x -= term
    return tuple(out)


LOG2E = math.log2(math.e)
LOG2E_TERMS = _bf16_terms(LOG2E, 3)


def _even_in_kernel(x_ref, g_ref, wn_ref, wqT_ref, wvT_ref, ext_ref,
                    ka_ref, ub_ref, km_ref, qT_ref, vT_ref):
    tm = x_ref.shape[1]
    xn = _rms(x_ref[0], g_ref[...]).astype(BF16)
    n = _dot(xn, wn_ref[...])
    ka = n[:, :MOBA_HEADS * LANES]
    ka_ref[0] = (ka + ext_ref[...].astype(F32)).astype(BF16)
    ub_ref[0] = n[:, MOBA_HEADS * LANES:]
    for j in range(tm // MOBA_BLOCK):
        km_ref[0, j] = jnp.mean(ka[j * MOBA_BLOCK:(j + 1) * MOBA_BLOCK], axis=0, keepdims=True)
    qT_ref[0] = _dot_nt(wqT_ref[...], xn)
    vT = _dot_nt(wvT_ref[...], xn)
    for j in range(tm // ATTN_BLOCK):
        vT_ref[0, j] = vT[:, j * ATTN_BLOCK:(j + 1) * ATTN_BLOCK].astype(BF16)


def _moba_key_extras(S, slopes):
    pos = np.arange(S)
    blk, off = pos // MOBA_BLOCK, pos % MOBA_BLOCK
    ext = np.zeros((S, MOBA_HEADS, LANES), np.float32)
    ext[pos, :, KAUG_SEL + blk] = 1.0
    for term in range(len(LOG2E_TERMS)):
        ext[:, :, KAUG_POS + 2 * term] = slopes[None, :] * (MOBA_BLOCK * blk)[:, None]
        ext[:, :, KAUG_POS + 2 * term + 1] = slopes[None, :] * off[:, None]
    return jnp.asarray(ext.reshape(S, MOBA_HEADS * LANES), dtype=BF16)


def _even_in(x, g, w_in, slopes):
    B, S, D = x.shape
    tm = ROW_TILE
    nb = S // MOBA_BLOCK
    wq, wk, wv, wu = (w_in[:, i * MIX_WIDTH:(i + 1) * MIX_WIDTH] for i in range(4))
    wk_slots = jnp.concatenate(
        [wk.reshape(D, MOBA_HEADS, MOBA_HEAD_DIM),
         jnp.zeros((D, MOBA_HEADS, LANES - MOBA_HEAD_DIM), F32)], axis=2).reshape(D, MOBA_HEADS * LANES)
    wn = jnp.concatenate([wk_slots, wu], axis=1).astype(BF16)
    wqT = wq.T.astype(BF16)
    wvT = wv.T.astype(BF16)
    const = lambda b, i: (0, 0)
    return pl.pallas_call(
        _even_in_kernel,
        grid=(B, S // tm),
        in_specs=[
            pl.BlockSpec((1, tm, D), lambda b, i: (b, i, 0)),
            pl.BlockSpec((1, D), const),
            pl.BlockSpec((D, MOBA_HEADS * LANES + MIX_WIDTH), const),
            pl.BlockSpec((MIX_WIDTH, D), const),
            pl.BlockSpec((MIX_WIDTH, D), const),
            pl.BlockSpec((tm, MOBA_HEADS * LANES), lambda b, i: (i, 0)),
        ],
        out_specs=[
            pl.BlockSpec((1, tm, MOBA_HEADS * LANES), lambda b, i: (b, i, 0)),
            pl.BlockSpec((1, tm, MIX_WIDTH), lambda b, i: (b, i, 0)),
            pl.BlockSpec((1, tm // MOBA_BLOCK, 1, MOBA_HEADS * LANES), lambda b, i: (b, i, 0, 0)),
            pl.BlockSpec((1, MIX_WIDTH, tm), lambda b, i: (b, 0, i)),
            pl.BlockSpec((1, tm // ATTN_BLOCK, MIX_WIDTH, ATTN_BLOCK), lambda b, i: (b, i, 0, 0)),
        ],
        out_shape=[
            jax.ShapeDtypeStruct((B, S, MOBA_HEADS * LANES), BF16),
            jax.ShapeDtypeStruct((B, S, MIX_WIDTH), F32),
            jax.ShapeDtypeStruct((B, nb, 1, MOBA_HEADS * LANES), F32),
            jax.ShapeDtypeStruct((B, MIX_WIDTH, S), F32),
            jax.ShapeDtypeStruct((B, S // ATTN_BLOCK, MIX_WIDTH, ATTN_BLOCK), BF16),
        ],
        compiler_params=_params(("parallel", "parallel")),
        name="even_in",
    )(x, g.reshape(1, D), wn, wqT, wvT, _moba_key_extras(S, slopes))


def _moba_gate_kernel(km_ref, qT_ref, qa_ref):
    i = pl.program_id(1)
    nb = km_ref.shape[2]
    tq = qT_ref.shape[2]
    row = lax.broadcasted_iota(jnp.int32, (nb, tq), 0)
    own = (i * tq + lax.broadcasted_iota(jnp.int32, (nb, tq), 1)) // MOBA_BLOCK
    past = row < own
    tail_row = lax.broadcasted_iota(jnp.int32, (LANES - KAUG_POS, tq), 0)
    tail = jnp.zeros(tail_row.shape, F32)
    for term, value in enumerate(LOG2E_TERMS):
        tail = jnp.where(tail_row // 2 == term, F32(value), tail)
    tail = tail.astype(BF16)
    pad = jnp.zeros((MOBA_MAX_BLOCKS - nb, tq), BF16) if nb < MOBA_MAX_BLOCKS else None
    for h in range(MOBA_HEADS):
        q_h = qT_ref[0, h * MOBA_HEAD_DIM:(h + 1) * MOBA_HEAD_DIM, :]
        gate = jnp.dot(km_ref[0, h], q_h, precision=HIGHEST, preferred_element_type=F32)
        gate = jnp.where(past, gate, NEG_INF)
        chosen = jnp.zeros(gate.shape, F32)
        for _ in range(MOBA_TOPK):
            mx = jnp.max(gate, axis=0, keepdims=True)
            first = jnp.min(jnp.where(gate == mx, row, nb), axis=0, keepdims=True)
            pick = row == first
            chosen = jnp.where(pick, 1.0, chosen)
            gate = jnp.where(pick, -jnp.inf, gate)
        keep = jnp.where(past, chosen, (row == own).astype(F32))
        sel = jnp.where(keep > 0.0, 0.0, NEG_INF).astype(BF16)
        base = h * LANES
        qa_ref[0, base:base + KAUG_SEL] = (q_h * (MOBA_HEAD_DIM ** -0.5 * LOG2E)).astype(BF16)
        qa_ref[0, base + KAUG_SEL:base + KAUG_SEL + nb] = sel
        if pad is not None:
            qa_ref[0, base + KAUG_SEL + nb:base + KAUG_POS] = pad
        qa_ref[0, base + KAUG_POS:base + LANES] = tail


def _moba_gate(kmean, qT):
    B, H, nb, dh = kmean.shape
    S = qT.shape[2]
    tq = ATTN_BLOCK
    return pl.pallas_call(
        _moba_gate_kernel,
        grid=(B, S // tq),
        in_specs=[
            pl.BlockSpec((1, H, nb, dh), lambda b, i: (b, 0, 0, 0)),
            pl.BlockSpec((1, H * dh, tq), lambda b, i: (b, 0, i)),
        ],
        out_specs=pl.BlockSpec((1, H * LANES, tq), lambda b, i: (b, 0, i)),
        out_shape=jax.ShapeDtypeStruct((B, H * LANES, S), BF16),
        compiler_params=_params(("parallel", "parallel")),
        name="moba_gate",
    )(kmean, qT)


HEADS_PER_STEP = 2
SUM_ROWS = 16


def _attn_kernel(q_ref, k_ref, v_ref, o_ref, sa_ref, sb_ref, m_ref, acc_ref):
    i = pl.program_id(2)
    tq = q_ref.shape[2]
    tk = ATTN_BLOCK
    hp = HEADS_PER_STEP
    dv = v_ref.shape[2] // hp
    m_ref[...] = jnp.full(m_ref.shape, NEG_INF, F32)
    acc_ref[...] = jnp.zeros(acc_ref.shape, F32)
    ones_rows = jnp.ones((SUM_ROWS, tk), BF16)

    def scores(kvt, s_ref, diag):
        start = pl.multiple_of(kvt * tk, tk)
        k_tile = k_ref[0, pl.ds(start, tk), :]
        for g in range(hp):
            s = _dot(k_tile[:, g * LANES:(g + 1) * LANES], q_ref[0, g * LANES:(g + 1) * LANES, :])
            if diag:
                key = lax.broadcasted_iota(jnp.int32, (tk, tq), 0)
                qry = lax.broadcasted_iota(jnp.int32, (tk, tq), 1)
                s = jnp.where(key <= qry, s, NEG_INF)
            s_ref[g] = s

    def consume(kvt, s_ref):
        v_tile = v_ref[0, kvt]
        for g in range(hp):
            s = s_ref[g]
            m_run = m_ref[g]
            m_new = jnp.maximum(m_run, jnp.max(s, axis=0, keepdims=True))
            p = jnp.exp2(s - m_new).astype(BF16)
            v_aug = jnp.concatenate([v_tile[g * dv:(g + 1) * dv], ones_rows], axis=0)
            acc_ref[g] = jnp.exp2(m_run - m_new) * acc_ref[g] + _dot(v_aug, p)
            m_ref[g] = m_new

    tile_at = lambda t: jnp.where(t == 0, i, t - 1)
    scores(i, sa_ref, True)

    def pair(p, carry):
        t = 2 * p
        scores(tile_at(t + 1), sb_ref, False)
        consume(tile_at(t), sa_ref)
        scores(tile_at(t + 2), sa_ref, False)
        consume(tile_at(t + 1), sb_ref)
        return carry

    lax.fori_loop(0, i // 2, pair, 0)
    last = 2 * (i // 2)

    @pl.when(i % 2 == 1)
    def _():
        scores(tile_at(last + 1), sb_ref, False)
        consume(tile_at(last), sa_ref)
        consume(tile_at(last + 1), sb_ref)

    @pl.when(i % 2 == 0)
    def _():
        consume(tile_at(last), sa_ref)

    for g in range(hp):
        o_ref[0, g * dv:(g + 1) * dv, :] = acc_ref[g, :dv] / acc_ref[g, dv:dv + 1]


def _attention(qT, k, vT, dv, heads, name):
    B, _, S = qT.shape
    tq = ATTN_BLOCK
    nt = S // ATTN_BLOCK
    hp = HEADS_PER_STEP
    return pl.pallas_call(
        _attn_kernel,
        grid=(B, heads // hp, S // tq),
        in_specs=[
            pl.BlockSpec((1, hp * LANES, tq), lambda b, h, i: (b, h, i)),
            pl.BlockSpec((1, S, hp * LANES), lambda b, h, i: (b, 0, h)),
            pl.BlockSpec((1, nt, hp * dv, ATTN_BLOCK), lambda b, h, i: (b, 0, h, 0)),
        ],
        out_specs=pl.BlockSpec((1, hp * dv, tq), lambda b, h, i: (b, h, i)),
        out_shape=jax.ShapeDtypeStruct((B, heads * dv, S), F32),
        scratch_shapes=[pltpu.VMEM((hp, ATTN_BLOCK, tq), F32), pltpu.VMEM((hp, ATTN_BLOCK, tq), F32),
                        pltpu.VMEM((hp, 1, tq), F32), pltpu.VMEM((hp, dv + SUM_ROWS, tq), F32)],
        compiler_params=_params(("parallel", "parallel", "arbitrary")),
        name=name,
    )(qT, k, vT)


def _pool_kernel(x_ref, halo_ref, w_ref, sc_ref, o_ref, xs_ref):
    i = pl.program_id(1)
    tm = x_ref.shape[1]
    x = x_ref[0]
    xs_ref[0:POOL_HALO] = jnp.where(i > 0, halo_ref[0], 0.0)
    xs_ref[POOL_HALO:POOL_HALO + tm] = x
    t = i * tm + lax.broadcasted_iota(jnp.int32, (tm, 1), 0)
    outs = []
    for g, win in enumerate(POOL_WINDOWS):
        lo, hi = g * POOL_GROUP_DIM, (g + 1) * POOL_GROUP_DIM
        xg = x[:, lo:hi]
        acc = xg
        for d in range(1, win):
            acc = acc + xs_ref[POOL_HALO - d:POOL_HALO - d + tm, lo:hi]
        count = jnp.minimum(t + 1, win).astype(F32)
        outs.append(_dot((acc / count - xg).astype(BF16), w_ref[g]))
    o_ref[0] = (jnp.concatenate(outs, axis=1) * sc_ref[...]).astype(o_ref.dtype)


def _pool(ub, pool_w, pool_scale):
    B, S, W = ub.shape
    tm = ROW_TILE
    per = tm // POOL_HALO
    return pl.pallas_call(
        _pool_kernel,
        grid=(B, S // tm),
        in_specs=[
            pl.BlockSpec((1, tm, W), lambda b, i: (b, i, 0)),
            pl.BlockSpec((1, POOL_HALO, W), lambda b, i: (b, jnp.maximum(i * per - 1, 0), 0)),
            pl.BlockSpec(pool_w.shape, lambda b, i: (0, 0, 0)),
            pl.BlockSpec((1, W), lambda b, i: (0, 0)),
        ],
        out_specs=pl.BlockSpec((1, tm, W), lambda b, i: (b, i, 0)),
        out_shape=jax.ShapeDtypeStruct((B, S, W), BF16),
        scratch_shapes=[pltpu.VMEM((POOL_HALO + tm, W), F32)],
        compiler_params=_params(("parallel", "parallel")),
        name="pool",
    )(ub, ub, pool_w.astype(BF16), pool_scale.reshape(1, W))


def _mix_out_kernel(h_ref, aT_ref, b_ref, w_ref, o_ref, *, a_first):
    a = aT_ref[0].T.astype(BF16)
    b = b_ref[0].astype(BF16)
    lo, hi = (a, b) if a_first else (b, a)
    y = _dot(lo, w_ref[:MIX_WIDTH]) + _dot(hi, w_ref[MIX_WIDTH:])
    o_ref[0] = h_ref[0] + y


def _mix_out(h, aT, b, w_out, a_first):
    B, S, D = h.shape
    tm = ROW_TILE
    return pl.pallas_call(
        functools.partial(_mix_out_kernel, a_first=a_first),
        grid=(B, S // tm),
        in_specs=[
            pl.BlockSpec((1, tm, D), lambda b_, i: (b_, i, 0)),
            pl.BlockSpec((1, MIX_WIDTH, tm), lambda b_, i: (b_, 0, i)),
            pl.BlockSpec((1, tm, MIX_WIDTH), lambda b_, i: (b_, i, 0)),
            pl.BlockSpec((2 * MIX_WIDTH, D), lambda b_, i: (0, 0)),
        ],
        out_specs=pl.BlockSpec((1, tm, D), lambda b_, i: (b_, i, 0)),
        out_shape=jax.ShapeDtypeStruct((B, S, D), F32),
        compiler_params=_params(("parallel", "parallel")),
        name="mix_out",
    )(h, aT, b, w_out.astype(BF16))


def _swiglu_step(xn, wg, wu, wd):
    gt = _dot(xn, wg)
    up = _dot(xn, wu)
    return _dot((gt * _sigmoid(gt) * up).astype(BF16), wd)


def _ffn_kernel(h_ref, g_ref, wg_ref, wu_ref, wd_ref, o_ref, xn_ref, acc_ref):
    f = pl.program_id(1)

    @pl.when(f == 0)
    def _():
        xn_ref[...] = _rms(h_ref[...], g_ref[...]).astype(BF16)
        acc_ref[...] = jnp.zeros_like(acc_ref)

    acc_ref[...] += _swiglu_step(xn_ref[...], wg_ref[...], wu_ref[...], wd_ref[...])

    @pl.when(f == pl.num_programs(1) - 1)
    def _():
        o_ref[...] = h_ref[...] + acc_ref[...]


def _ffn(h2d, g, wg, wu, wd, tm=ROW_TILE, tf=FFN_DIM // 2):
    T, D = h2d.shape
    F = wg.shape[1]
    return pl.pallas_call(
        _ffn_kernel,
        grid=(T // tm, F // tf),
        in_specs=[
            pl.BlockSpec((tm, D), lambda i, f: (i, 0)),
            pl.BlockSpec((1, D), lambda i, f: (0, 0)),
            pl.BlockSpec((D, tf), lambda i, f: (0, f)),
            pl.BlockSpec((D, tf), lambda i, f: (0, f)),
            pl.BlockSpec((tf, D), lambda i, f: (f, 0)),
        ],
        out_specs=pl.BlockSpec((tm, D), lambda i, f: (i, 0)),
        out_shape=jax.ShapeDtypeStruct((T, D), F32),
        scratch_shapes=[pltpu.VMEM((tm, D), BF16), pltpu.VMEM((tm, D), F32)],
        compiler_params=_params(("parallel", "arbitrary")),
        name="ffn",
    )(h2d, g.reshape(1, D), wg, wu, wd)


ROUTE_E0, ROUTE_E1, ROUTE_W0, ROUTE_W1, ROUTE_R0, ROUTE_R1 = range(6)
MOE_TILE = 512
MOE_TF = EXPERT_DIM // 2


def _lane_pick(tile, lane, idx):
    return jnp.sum(jnp.where(lane == idx, tile, 0.0), axis=1, keepdims=True)


def _router_kernel(h_ref, g_ref, w_ref, b_ref, route_ref, cnt_ref):
    tm = h_ref.shape[0]

    @pl.when(pl.program_id(0) == 0)
    def _():
        cnt_ref[...] = jnp.zeros_like(cnt_ref)

    xn = _rms(h_ref[...], g_ref[...])
    logits = jnp.dot(xn, w_ref[...], precision=HIGHEST, preferred_element_type=F32) + b_ref[...]
    lane = lax.broadcasted_iota(jnp.int32, logits.shape, 1)
    logits = jnp.where(lane < N_EXPERTS, logits, -jnp.inf)
    v0 = jnp.max(logits, axis=1, keepdims=True)
    i0 = jnp.min(jnp.where(logits == v0, lane, LANES), axis=1, keepdims=True)
    rest = jnp.where(lane == i0, -jnp.inf, logits)
    v1 = jnp.max(rest, axis=1, keepdims=True)
    i1 = jnp.min(jnp.where(rest == v1, lane, LANES), axis=1, keepdims=True)
    e1 = jnp.exp(v1 - v0)
    w0 = 1.0 / (1.0 + e1)
    sel = (lane == i0).astype(F32) + (lane == i1).astype(F32)
    earlier = (lax.broadcasted_iota(jnp.int32, (tm, tm), 1)
               < lax.broadcasted_iota(jnp.int32, (tm, tm), 0))
    rank = _dot(earlier.astype(BF16), sel.astype(BF16)) + cnt_ref[...]
    cnt_ref[...] += jnp.sum(sel, axis=0, keepdims=True)
    cols = ((ROUTE_E0, i0.astype(F32)), (ROUTE_E1, i1.astype(F32)), (ROUTE_W0, w0), (ROUTE_W1, e1 * w0),
            (ROUTE_R0, _lane_pick(rank, lane, i0)), (ROUTE_R1, _lane_pick(rank, lane, i1)))
    route = jnp.zeros(logits.shape, F32)
    for c, val in cols:
        route = jnp.where(lane == c, val, route)
    route_ref[...] = route


def _router(h2d, g, router_w, router_b):
    T, D = h2d.shape
    tm = ROW_TILE
    w = jnp.zeros((D, LANES), F32).at[:, :N_EXPERTS].set(router_w)
    b = jnp.zeros((1, LANES), F32).at[0, :N_EXPERTS].set(router_b)
    return pl.pallas_call(
        _router_kernel,
        grid=(T // tm,),
        in_specs=[
            pl.BlockSpec((tm, D), lambda i: (i, 0)),
            pl.BlockSpec((1, D), lambda i: (0, 0)),
            pl.BlockSpec((D, LANES), lambda i: (0, 0)),
            pl.BlockSpec((1, LANES), lambda i: (0, 0)),
        ],
        out_specs=[pl.BlockSpec((tm, LANES), lambda i: (i, 0)),
                   pl.BlockSpec((1, LANES), lambda i: (0, 0))],
        out_shape=[jax.ShapeDtypeStruct((T, LANES), F32), jax.ShapeDtypeStruct((1, LANES), F32)],
        compiler_params=_params(("arbitrary",)),
        name="router",
    )(h2d, g.reshape(1, D), w, b)


def _row_copies(pos_ref, base, r, src_of, dst_of, sem):
    return [pltpu.make_async_copy(src_of(k, pos_ref[base + 2 * r + k]),
                                  dst_of(k, pos_ref[base + 2 * r + k]), sem) for k in range(2)]


def _all_rows(tm, make):
    def issue(r, c):
        for cp in make(r):
            cp.start()
        return c

    def drain(r, c):
        for cp in make(r):
            cp.wait()
        return c

    lax.fori_loop(0, tm, issue, 0, unroll=8)
    lax.fori_loop(0, tm, drain, 0, unroll=8)


def _dispatch_kernel(pos_ref, h_ref, g_ref, init_ref, xs_ref, xn_ref, sem):
    del init_ref
    tm = h_ref.shape[0]
    xn_ref[...] = _rms(h_ref[...], g_ref[...])
    base = pl.program_id(0) * (2 * tm)
    _all_rows(tm, lambda r: _row_copies(
        pos_ref, base, r, lambda k, p: xn_ref.at[pl.ds(r, 1)], lambda k, p: xs_ref.at[pl.ds(p, 1)], sem))


def _dispatch(pos, h2d, g, n_rows):
    T, D = h2d.shape
    tm = ROW_TILE
    return pl.pallas_call(
        _dispatch_kernel,
        grid_spec=pltpu.PrefetchScalarGridSpec(
            num_scalar_prefetch=1,
            grid=(T // tm,),
            in_specs=[pl.BlockSpec((tm, D), lambda i, pos: (i, 0)),
                      pl.BlockSpec((1, D), lambda i, pos: (0, 0)),
                      pl.BlockSpec(memory_space=pl.ANY)],
            out_specs=pl.BlockSpec(memory_space=pl.ANY),
            scratch_shapes=[pltpu.VMEM((tm, D), F32), pltpu.SemaphoreType.DMA(())],
        ),
        out_shape=jax.ShapeDtypeStruct((n_rows, D), F32),
        input_output_aliases={3: 0},
        compiler_params=_params(("arbitrary",)),
        name="moe_dispatch",
    )(pos, h2d, g.reshape(1, D), jnp.zeros((n_rows, D), F32))


def _moe_ffn_kernel(te_ref, nv_ref, x_ref, wg_ref, wu_ref, wd_ref, o_ref, xb_ref, acc_ref):
    j = pl.program_id(0)
    f = pl.program_id(1)
    valid = j < nv_ref[0]

    @pl.when(f == 0)
    def _():
        xb_ref[...] = x_ref[...].astype(BF16)
        acc_ref[...] = jnp.zeros_like(acc_ref)

    @pl.when(valid)
    def _():
        acc_ref[...] += _swiglu_step(xb_ref[...], wg_ref[0], wu_ref[0], wd_ref[0])

    @pl.when(f == pl.num_programs(1) - 1)
    def _():
        o_ref[...] = acc_ref[...]


def _moe_ffn(tile_expert, n_valid, xs, wg, wu, wd):
    N, D = xs.shape
    tm, tf = MOE_TILE, MOE_TF
    F = wg.shape[2]
    return pl.pallas_call(
        _moe_ffn_kernel,
        grid_spec=pltpu.PrefetchScalarGridSpec(
            num_scalar_prefetch=2,
            grid=(N // tm, F // tf),
            in_specs=[pl.BlockSpec((tm, D), lambda j, f, te, nv: (j, 0)),
                      pl.BlockSpec((1, D, tf), lambda j, f, te, nv: (te[j], 0, f)),
                      pl.BlockSpec((1, D, tf), lambda j, f, te, nv: (te[j], 0, f)),
                      pl.BlockSpec((1, tf, D), lambda j, f, te, nv: (te[j], f, 0))],
            out_specs=pl.BlockSpec((tm, D), lambda j, f, te, nv: (j, 0)),
            scratch_shapes=[pltpu.VMEM((tm, D), BF16), pltpu.VMEM((tm, D), F32)],
        ),
        out_shape=jax.ShapeDtypeStruct((N, D), F32),
        compiler_params=_params(("arbitrary", "arbitrary")),
        name="moe_ffn",
    )(tile_expert, n_valid, xs, wg, wu, wd)


def _combine_kernel(pos_ref, h_ref, route_ref, ys_ref, o_ref, y_ref, sem):
    tm = h_ref.shape[0]
    base = pl.program_id(0) * (2 * tm)
    _all_rows(tm, lambda r: _row_copies(
        pos_ref, base, r, lambda k, p: ys_ref.at[pl.ds(p, 1)], lambda k, p: y_ref.at[k, pl.ds(r, 1)], sem))
    route = route_ref[...]
    lane = lax.broadcasted_iota(jnp.int32, route.shape, 1)
    o_ref[...] = (h_ref[...] + _lane_pick(route, lane, ROUTE_W0) * y_ref[0]
                  + _lane_pick(route, lane, ROUTE_W1) * y_ref[1])


def _combine(pos, h2d, route, ys):
    T, D = h2d.shape
    tm = ROW_TILE
    return pl.pallas_call(
        _combine_kernel,
        grid_spec=pltpu.PrefetchScalarGridSpec(
            num_scalar_prefetch=1,
            grid=(T // tm,),
            in_specs=[pl.BlockSpec((tm, D), lambda i, pos: (i, 0)),
                      pl.BlockSpec((tm, LANES), lambda i, pos: (i, 0)),
                      pl.BlockSpec(memory_space=pl.ANY)],
            out_specs=pl.BlockSpec((tm, D), lambda i, pos: (i, 0)),
            scratch_shapes=[pltpu.VMEM((2, tm, D), F32), pltpu.SemaphoreType.DMA(())],
        ),
        out_shape=jax.ShapeDtypeStruct((T, D), F32),
        compiler_params=_params(("arbitrary",)),
        name="moe_combine",
    )(pos, h2d, route, ys)


def _moe(h2d, g, router_w, router_b, wg, wu, wd):
    T, D = h2d.shape
    tm = MOE_TILE
    route, counts = _router(h2d, g, router_w, router_b)
    cnt = counts[0, :N_EXPERTS].astype(jnp.int32)
    padded = (cnt + tm - 1) // tm * tm
    ends = jnp.cumsum(padded)
    start = ends - padded
    e01 = route[:, ROUTE_E0:ROUTE_E1 + 1].astype(jnp.int32)
    r01 = route[:, ROUTE_R0:ROUTE_R1 + 1].astype(jnp.int32)
    pos = (start[e01] + r01).reshape(2 * T)
    n_rows = 2 * T + N_EXPERTS * tm
    tile_row = jnp.arange(n_rows // tm, dtype=jnp.int32) * tm
    tile_expert = jnp.minimum(jnp.sum(tile_row[:, None] >= ends[None, :], axis=1), N_EXPERTS - 1).astype(jnp.int32)
    n_valid = (ends[-1:] // tm).astype(jnp.int32)
    xs = _dispatch(pos, h2d, g, n_rows)
    ys = _moe_ffn(tile_expert, n_valid, xs, wg, wu, wd)
    return _combine(pos, h2d, route, ys)


def _ple_kernel(*refs, final):
    if final:
        h_ref, p_ref, g_ref, wg_ref, wp_ref, fg_ref, o_ref = refs
    else:
        h_ref, p_ref, g_ref, wg_ref, wp_ref, o_ref = refs
    h = h_ref[...]
    gate = _sigmoid(_dot(_rms(h, g_ref[...]).astype(BF16), wg_ref[...]))
    out = h + gate * _dot(p_ref[...].astype(BF16), wp_ref[...])
    if final:
        out = _rms(out, fg_ref[...])
    o_ref[...] = out


def _ple(h2d, p2d, g, w_gate, w_proj, final_g=None):
    T, D = h2d.shape
    tm = ROW_TILE
    final = final_g is not None
    in_specs = [
        pl.BlockSpec((tm, D), lambda i: (i, 0)),
        pl.BlockSpec((tm, PLE_DIM), lambda i: (i, 0)),
        pl.BlockSpec((1, D), lambda i: (0, 0)),
        pl.BlockSpec((D, D), lambda i: (0, 0)),
        pl.BlockSpec((PLE_DIM, D), lambda i: (0, 0)),
    ]
    args = [h2d, p2d, g.reshape(1, D), w_gate.astype(BF16), w_proj.astype(BF16)]
    if final:
        in_specs.append(pl.BlockSpec((1, D), lambda i: (0, 0)))
        args.append(final_g.reshape(1, D))
    return pl.pallas_call(
        functools.partial(_ple_kernel, final=final),
        grid=(T // tm,),
        in_specs=in_specs,
        out_specs=pl.BlockSpec((tm, D), lambda i: (i, 0)),
        out_shape=jax.ShapeDtypeStruct((T, D), F32),
        compiler_params=_params(("parallel",)),
        name="ple_final" if final else "ple",
    )(*args)


ODD_MAIN = 4 * MIX_WIDTH
ODD_COLS = ODD_MAIN + MLA_Q_RANK + MLA_KV_RANK + 2 * LANES
MLA_QK_SCALE = (MLA_NOPE_DIM + MLA_ROPE_DIM) ** -0.5 * LOG2E
ROPE_HALF = MLA_ROPE_DIM // 2


def _odd_in_kernel(x_ref, g_ref, wn_ref, qn_ref, wuqT_ref, kvn_ref, wk2_ref, wvT_ref,
                   cosT_ref, sinT_ref, cc_ref, ss_ref,
                   qk_ref, vc_ref, op_ref, misc_ref, mq_ref, mk_ref, mv_ref):
    tm = x_ref.shape[1]
    xn = _rms(x_ref[0], g_ref[...]).astype(BF16)
    u = _dot(xn, wn_ref[...])
    qk_ref[0] = u[:, :2 * MIX_WIDTH]
    vc_ref[0] = u[:, 2 * MIX_WIDTH:3 * MIX_WIDTH].astype(BF16)
    op_ref[0] = u[:, 3 * MIX_WIDTH:ODD_MAIN]
    c0 = ODD_MAIN
    c_q = u[:, c0:c0 + MLA_Q_RANK]
    c0 += MLA_Q_RANK
    c_kv = u[:, c0:c0 + MLA_KV_RANK]
    c0 += MLA_KV_RANK
    misc = u[:, c0:c0 + LANES]
    misc_sw = u[:, c0 + LANES:c0 + 2 * LANES]
    misc_ref[0] = misc
    cqn = _rms(c_q, qn_ref[...]).astype(BF16)
    qT = _dot_nt(wuqT_ref[...], cqn)
    cosT = cosT_ref[...]
    sinT = sinT_ref[...]
    for h in range(MLA_HEADS):
        r = h * LANES
        mq_ref[0, r:r + MLA_NOPE_DIM] = (qT[r:r + MLA_NOPE_DIM] * MLA_QK_SCALE).astype(BF16)
        x1 = qT[r + MLA_NOPE_DIM:r + MLA_NOPE_DIM + ROPE_HALF]
        x2 = qT[r + MLA_NOPE_DIM + ROPE_HALF:r + MLA_NOPE_DIM + MLA_ROPE_DIM]
        mq_ref[0, r + MLA_NOPE_DIM:r + MLA_NOPE_DIM + ROPE_HALF] = (
            (x1 * cosT - x2 * sinT) * MLA_QK_SCALE).astype(BF16)
        mq_ref[0, r + MLA_NOPE_DIM + ROPE_HALF:r + MLA_NOPE_DIM + MLA_ROPE_DIM] = (
            (x1 * sinT + x2 * cosT) * MLA_QK_SCALE).astype(BF16)
        mq_ref[0, r + MLA_NOPE_DIM + MLA_ROPE_DIM:r + LANES] = jnp.zeros(
            (LANES - MLA_NOPE_DIM - MLA_ROPE_DIM, tm), BF16)
    ckvn = _rms(c_kv, kvn_ref[...]).astype(BF16)
    k_rot = (misc * cc_ref[...] + misc_sw * ss_ref[...]).astype(BF16)
    mk_ref[0] = _dot(jnp.concatenate([ckvn, k_rot], axis=1), wk2_ref[...]).astype(BF16)
    vT = _dot_nt(wvT_ref[...], ckvn)
    for j in range(tm // ATTN_BLOCK):
        mv_ref[0, j] = vT[:, j * ATTN_BLOCK:(j + 1) * ATTN_BLOCK].astype(BF16)


def _rope_tables(S):
    inv_freq = ROPE_BASE ** (-jnp.arange(ROPE_HALF, dtype=F32) / ROPE_HALF)
    ang = jnp.arange(S, dtype=F32)[:, None] * inv_freq[None, :]
    cos, sin = jnp.cos(ang), jnp.sin(ang)
    pad = jnp.zeros((S, LANES - MLA_ROPE_DIM), F32)
    cc = jnp.concatenate([cos, cos, pad], axis=1)
    ss = jnp.concatenate([-sin, sin, pad], axis=1)
    return cos.T, sin.T, cc, ss


def _odd_in(x, g, w_in, q_norm, w_uq, kv_norm, w_ukv):
    B, S, D = x.shape
    tm = ROW_TILE
    cuts = np.cumsum([MIX_WIDTH] * 4 + [MLSTM_HEADS, MLSTM_HEADS, MLA_Q_RANK, MLA_KV_RANK]).tolist()
    w_main = w_in[:, :cuts[3]]
    w_i = w_in[:, cuts[3]:cuts[4]]
    w_f = w_in[:, cuts[4]:cuts[5]]
    w_cq = w_in[:, cuts[5]:cuts[6]]
    w_ckv = w_in[:, cuts[6]:cuts[7]]
    w_kr = w_in[:, cuts[7]:]
    w_kr_sw = jnp.concatenate([w_kr[:, ROPE_HALF:], w_kr[:, :ROPE_HALF]], axis=1)
    zpad = lambda n: jnp.zeros((D, n), F32)
    w_misc = jnp.concatenate([w_kr, w_i, w_f, zpad(LANES - MLA_ROPE_DIM - 2 * MLSTM_HEADS)], axis=1)
    w_misc_sw = jnp.concatenate([w_kr_sw, zpad(LANES - MLA_ROPE_DIM)], axis=1)
    wn = jnp.concatenate([w_main, w_cq, w_ckv, w_misc, w_misc_sw], axis=1).astype(BF16)
    qd = MLA_NOPE_DIM + MLA_ROPE_DIM
    w_uq_h = w_uq.reshape(MLA_Q_RANK, MLA_HEADS, qd)
    w_uq_h = jnp.concatenate([w_uq_h, jnp.zeros((MLA_Q_RANK, MLA_HEADS, LANES - qd), F32)], axis=2)
    wuqT = w_uq_h.reshape(MLA_Q_RANK, MLA_HEADS * LANES).T.astype(BF16)
    w_ukv_h = w_ukv.reshape(MLA_KV_RANK, MLA_HEADS, MLA_NOPE_DIM + MLA_V_DIM)
    w_k = jnp.concatenate([w_ukv_h[:, :, :MLA_NOPE_DIM],
                           jnp.zeros((MLA_KV_RANK, MLA_HEADS, LANES - MLA_NOPE_DIM), F32)], axis=2)
    place = jnp.zeros((LANES, MLA_HEADS, LANES), F32)
    eye = jnp.eye(MLA_ROPE_DIM, dtype=F32)
    place = place.at[:MLA_ROPE_DIM, :, MLA_NOPE_DIM:MLA_NOPE_DIM + MLA_ROPE_DIM].set(
        jnp.broadcast_to(eye[:, None, :], (MLA_ROPE_DIM, MLA_HEADS, MLA_ROPE_DIM)))
    wk2 = jnp.concatenate([w_k, place], axis=0).reshape(MLA_KV_RANK + LANES, MLA_HEADS * LANES).astype(BF16)
    wvT = w_ukv_h[:, :, MLA_NOPE_DIM:].reshape(MLA_KV_RANK, MLA_HEADS * MLA_V_DIM).T.astype(BF16)
    cosT, sinT, cc, ss = _rope_tables(S)
    row = lambda b, i: (b, i, 0)
    const = lambda b, i: (0, 0)
    nb = S // ATTN_BLOCK
    return pl.pallas_call(
        _odd_in_kernel,
        grid=(B, S // tm),
        in_specs=[
            pl.BlockSpec((1, tm, D), row),
            pl.BlockSpec((1, D), const),
            pl.BlockSpec((D, ODD_COLS), const),
            pl.BlockSpec((1, MLA_Q_RANK), const),
            pl.BlockSpec((MLA_HEADS * LANES, MLA_Q_RANK), const),
            pl.BlockSpec((1, MLA_KV_RANK), const),
            pl.BlockSpec((MLA_KV_RANK + LANES, MLA_HEADS * LANES), const),
            pl.BlockSpec((MLA_HEADS * MLA_V_DIM, MLA_KV_RANK), const),
            pl.BlockSpec((ROPE_HALF, tm), lambda b, i: (0, i)),
            pl.BlockSpec((ROPE_HALF, tm), lambda b, i: (0, i)),
            pl.BlockSpec((tm, LANES), lambda b, i: (i, 0)),
            pl.BlockSpec((tm, LANES), lambda b, i: (i, 0)),
        ],
        out_specs=[
            pl.BlockSpec((1, tm, 2 * MIX_WIDTH), row),
            pl.BlockSpec((1, tm, MIX_WIDTH), row),
            pl.BlockSpec((1, tm, MIX_WIDTH), row),
            pl.BlockSpec((1, tm, LANES), row),
            pl.BlockSpec((1, MLA_HEADS * LANES, tm), lambda b, i: (b, 0, i)),
            pl.BlockSpec((1, tm, MLA_HEADS * LANES), row),
            pl.BlockSpec((1, tm // ATTN_BLOCK, MLA_HEADS * MLA_V_DIM, ATTN_BLOCK), lambda b, i: (b, i, 0, 0)),
        ],
        out_shape=[
            jax.ShapeDtypeStruct((B, S, 2 * MIX_WIDTH), F32),
            jax.ShapeDtypeStruct((B, S, MIX_WIDTH), BF16),
            jax.ShapeDtypeStruct((B, S, MIX_WIDTH), F32),
            jax.ShapeDtypeStruct((B, S, LANES), F32),
            jax.ShapeDtypeStruct((B, MLA_HEADS * LANES, S), BF16),
            jax.ShapeDtypeStruct((B, S, MLA_HEADS * LANES), BF16),
            jax.ShapeDtypeStruct((B, nb, MLA_HEADS * MLA_V_DIM, ATTN_BLOCK), BF16),
        ],
        compiler_params=_params(("parallel", "parallel")),
        name="odd_in",
    )(x, g.reshape(1, D), wn, q_norm.reshape(1, -1), wuqT, kv_norm.reshape(1, -1), wk2, wvT,
      cosT, sinT, cc, ss)


def _log_sigmoid(x):
    return jnp.minimum(x, 0.0) - jnp.log(1.0 + jnp.exp(-jnp.abs(x)))


def _mlstm_kernel(qk_ref, v_ref, op_ref, misc_ref, cw_ref, gb_ref, hn_ref, o_ref,
                  prev_ref, cn_ref, m_ref):
    c = pl.program_id(1)
    L = qk_ref.shape[1]
    d = MLSTM_HEAD_DIM

    @pl.when(c == 0)
    def _():
        prev_ref[...] = jnp.zeros_like(prev_ref)
        cn_ref[...] = jnp.zeros_like(cn_ref)
        m_ref[...] = jnp.zeros_like(m_ref)

    x = qk_ref[0]
    prev = prev_ref[...]
    row = lax.broadcasted_iota(jnp.int32, (L, 1), 0)
    conv = x * cw_ref[CONV_WIDTH - 1:CONV_WIDTH, :]
    for j in range(1, CONV_WIDTH):
        shifted = pltpu.roll(jnp.where(row >= L - j, prev, x), j, axis=0)
        conv = conv + shifted * cw_ref[CONV_WIDTH - 1 - j:CONV_WIDTH - j, :]
    prev_ref[...] = x
    qk = conv * _sigmoid(conv)

    gates = misc_ref[0] + gb_ref[...]
    lane = lax.broadcasted_iota(jnp.int32, (L, LANES), 1)
    is_f = (lane >= MISC_F) & (lane < MISC_F + MLSTM_HEADS)
    z = jnp.where(is_f, _log_sigmoid(gates), gates)
    tri_r = lax.broadcasted_iota(jnp.int32, (L, L), 0)
    tri_c = lax.broadcasted_iota(jnp.int32, (L, L), 1)
    causal = tri_c <= tri_r
    cum = jnp.dot(causal.astype(F32), z, precision=HIGHEST, preferred_element_type=F32)
    z = jnp.where(is_f, cum, z)
    zT = z.T

    ones_col = (lax.broadcasted_iota(jnp.int32, (L, LANES), 1) == 0).astype(BF16)
    for h in range(MLSTM_HEADS):
        lo, hi = h * d, (h + 1) * d
        q = qk[:, lo:hi].astype(BF16)
        k = qk[:, MIX_WIDTH + lo:MIX_WIDTH + hi] * (d ** -0.5)
        v_aug = jnp.concatenate([v_ref[0, :, lo:hi], ones_col], axis=1)
        i_col = z[:, MISC_I + h:MISC_I + h + 1]
        b_col = z[:, MISC_F + h:MISC_F + h + 1]
        i_row = zT[MISC_I + h:MISC_I + h + 1, :]
        b_row = zT[MISC_F + h:MISC_F + h + 1, :]
        m_prev = m_ref[h:h + 1, 0:1]
        intra = jnp.where(causal, b_col - b_row + i_row, NEG_INF)
        m_inter = b_col + m_prev
        m_t = jnp.maximum(m_inter, jnp.max(intra, axis=1, keepdims=True))
        w_inter = jnp.exp(m_inter - m_t)
        a = jnp.exp(intra - m_t) * _dot_nt(q, k.astype(BF16))
        inter = _dot(q, cn_ref[h].astype(BF16))
        intra_o = _dot(a.astype(BF16), v_aug)
        num = w_inter * inter[:, :d] + intra_o[:, :d]
        den = w_inter * inter[:, d:d + 1] + intra_o[:, d:d + 1]
        hh = num / jnp.maximum(jnp.abs(den), jnp.exp(-m_t))
        hh = _rms(hh, hn_ref[:, lo:hi])
        o_ref[0, :, lo:hi] = (hh * _sigmoid(op_ref[0, :, lo:hi])).astype(o_ref.dtype)
        b_end = b_col[L - 1:L, :]
        g_col = b_end - b_col + i_col
        m_new = jnp.maximum(b_end + m_prev, jnp.max(g_col, axis=0, keepdims=True))
        decay = jnp.exp(b_end + m_prev - m_new)
        kw = k * jnp.exp(g_col - m_new)
        cn_ref[h] = decay * cn_ref[h] + _dot(kw.T.astype(BF16), v_aug)
        m_ref[h:h + 1, :] = jnp.broadcast_to(m_new, (1, LANES))


def _mlstm(qk_raw, vc, o_pre, misc, conv_w, b_i, b_f, head_norm):
    B, S, _ = qk_raw.shape
    L = MLSTM_CHUNK
    gb = jnp.zeros((1, LANES), F32).at[0, MISC_I:MISC_I + MLSTM_HEADS].set(b_i)
    gb = gb.at[0, MISC_F:MISC_F + MLSTM_HEADS].set(b_f)
    row = lambda b, c: (b, c, 0)
    const = lambda b, c: (0, 0)
    return pl.pallas_call(
        _mlstm_kernel,
        grid=(B, S // L),
        in_specs=[
            pl.BlockSpec((1, L, 2 * MIX_WIDTH), row),
            pl.BlockSpec((1, L, MIX_WIDTH), row),
            pl.BlockSpec((1, L, MIX_WIDTH), row),
            pl.BlockSpec((1, L, LANES), row),
            pl.BlockSpec((CONV_WIDTH, 2 * MIX_WIDTH), const),
            pl.BlockSpec((1, LANES), const),
            pl.BlockSpec((1, MIX_WIDTH), const),
        ],
        out_specs=pl.BlockSpec((1, L, MIX_WIDTH), row),
        out_shape=jax.ShapeDtypeStruct((B, S, MIX_WIDTH), BF16),
        scratch_shapes=[
            pltpu.VMEM((L, 2 * MIX_WIDTH), F32),
            pltpu.VMEM((MLSTM_HEADS, MLSTM_HEAD_DIM, 2 * LANES), F32),
            pltpu.VMEM((8, LANES), F32),
        ],
        compiler_params=_params(("parallel", "arbitrary")),
        name="mlstm",
    )(qk_raw, vc, o_pre, misc, conv_w, gb, head_norm.reshape(1, MIX_WIDTH))


def _even_layer(h, norm_mix, w_in, pool_w, pool_scale, w_out, norm_ffn, wg, wu, wd):
    B, S, D = h.shape
    slopes = (2.0 ** (-8.0 * np.arange(1, MOBA_HEADS + 1) / MOBA_HEADS)).astype(np.float32)
    ka, ub, kmean, qT, vT = _even_in(h, norm_mix, w_in, slopes)
    nb = S // MOBA_BLOCK
    kmean = kmean.reshape(B, nb, MOBA_HEADS, LANES)[..., :MOBA_HEAD_DIM].transpose(0, 2, 1, 3)
    qaT = _moba_gate(kmean, qT)
    aT = _attention(qaT, ka, vT, MOBA_HEAD_DIM, MOBA_HEADS, "moba_attn")
    b_out = _pool(ub, pool_w, pool_scale)
    h = _mix_out(h, aT, b_out, w_out, a_first=True)
    return _ffn(h.reshape(B * S, D), norm_ffn, wg.astype(BF16), wu.astype(BF16), wd.astype(BF16))


def _odd_layer(h, norm_mix, w_in, conv_w, b_i, b_f, head_norm, q_norm, w_uq, kv_norm, w_ukv,
               w_out, norm_ffn, router_w, router_b, wg, wu, wd):
    B, S, D = h.shape
    qk_raw, vc, o_pre, misc, mqT, mk, mvT = _odd_in(h, norm_mix, w_in, q_norm, w_uq, kv_norm, w_ukv)
    c_out = _mlstm(qk_raw, vc, o_pre, misc, conv_w, b_i, b_f, head_norm)
    dT = _attention(mqT, mk, mvT, MLA_V_DIM, MLA_HEADS, "mla_attn")
    h = _mix_out(h, dT, c_out, w_out, a_first=False)
    return _moe(h.reshape(B * S, D), norm_ffn, router_w, router_b,
                wg.astype(BF16), wu.astype(BF16), wd.astype(BF16))


def kernel(x, p, ev_norm_mix, ev_w_in, pool_w, pool_scale, ev_w_out, ev_norm_ffn, ffn_w_gate, ffn_w_up, ffn_w_down, od_norm_mix, od_w_in, conv_w, gate_b_i, gate_b_f, mlstm_norm, mla_q_norm, mla_w_uq, mla_kv_norm, mla_w_ukv, od_w_out, od_norm_ffn, router_w, router_b, moe_w_gate, moe_w_up, moe_w_down, ple_norm, ple_w_gate, ple_w_proj, final_norm):
    B, S, D = x.shape
    depth = p.shape[0]
    assert D == D_MODEL and S % (2 * ROW_TILE) == 0 and S // MOBA_BLOCK >= MOBA_TOPK
    h = x
    for layer in range(depth):
        j = layer // 2
        if layer % 2 == 0:
            h2d = _even_layer(h, ev_norm_mix[j], ev_w_in[j], pool_w[j], pool_scale[j], ev_w_out[j],
                              ev_norm_ffn[j], ffn_w_gate[j], ffn_w_up[j], ffn_w_down[j])
        else:
            h2d = _odd_layer(h, od_norm_mix[j], od_w_in[j], conv_w[j], gate_b_i[j], gate_b_f[j],
                             mlstm_norm[j], mla_q_norm[j], mla_w_uq[j], mla_kv_norm[j], mla_w_ukv[j],
                             od_w_out[j], od_norm_ffn[j], router_w[j], router_b[j],
                             moe_w_gate[j], moe_w_up[j], moe_w_down[j])
        last = layer == depth - 1
        h2d = _ple(h2d, p[layer].reshape(B * S, PLE_DIM), ple_norm[layer], ple_w_gate[layer],
                   ple_w_proj[layer], final_g=final_norm if last else None)
        h = h2d.reshape(B, S, D)
    return h
```

```python
import functools
import math

import numpy as np
import jax
import jax.numpy as jnp
from jax import lax
from jax.experimental import pallas as pl
from jax.experimental.pallas import tpu as pltpu

F32 = jnp.float32
BF16 = jnp.bfloat16
HIGHEST = lax.Precision.HIGHEST

D_MODEL = 1024
PLE_DIM = 256
NORM_EPS = 1e-6
NEG_INF = -1e30

MOBA_HEADS = 8
MOBA_HEAD_DIM = 64
MOBA_BLOCK = 256
MOBA_TOPK = 3
POOL_WINDOWS = (2, 4, 8, 16)
POOL_GROUP_DIM = 128
POOL_HALO = 16
MLSTM_HEADS = 4
MLSTM_HEAD_DIM = 128
MLSTM_CHUNK = 128
MLSTM_BATCH = 2
CONV_WIDTH = 4
MLA_HEADS = 4
MLA_Q_RANK = 256
MLA_KV_RANK = 128
MLA_NOPE_DIM = 64
MLA_ROPE_DIM = 32
MLA_V_DIM = 128
ROPE_BASE = 10000.0
FFN_DIM = 2816
N_EXPERTS = 8
EXPERT_DIM = 3584
MIX_WIDTH = 512

ATTN_BLOCK = 512
ROW_TILE = 512
LANES = 128
VMEM_LIMIT = 56 * 1024 * 1024

MISC_ROPE = 0
MISC_I = 32
MISC_F = 36


def _params(sem, vmem=VMEM_LIMIT):
    return pltpu.CompilerParams(dimension_semantics=sem, vmem_limit_bytes=vmem)


def _rms(x, g):
    ms = jnp.mean(x * x, axis=-1, keepdims=True)
    return x * lax.rsqrt(ms + NORM_EPS) * g


def _sigmoid(x):
    return 1.0 / (1.0 + jnp.exp(-x))


def _dot(a, b):
    return jnp.dot(a, b, preferred_element_type=F32)


def _dot_nt(a, b, precision=None):
    return lax.dot_general(a, b, (((1,), (1,)), ((), ())), precision=precision,
                           preferred_element_type=F32)


KAUG_SEL = MOBA_HEAD_DIM
KAUG_POS = KAUG_SEL + 32
MOBA_MAX_BLOCKS = KAUG_POS - KAUG_SEL


def _bf16_terms(x, n):
    out = []
    for _ in range(n):
        bits = np.float32(x).view(np.uint32)
        kept = np.uint32((int(bits) + 0x7FFF + ((int(bits) >> 16) & 1)) & 0xFFFF0000)
        term = float(kept.view(np.float32))
        out.append(term)
        x -= term
    return tuple(out)


LOG2E = math.log2(math.e)
LOG2E_TERMS = _bf16_terms(LOG2E, 3)


def _even_in_kernel(x_ref, g_ref, wn_ref, wqT_ref, wvT_ref, ext_ref,
                    ka_ref, ub_ref, km_ref, qT_ref, vT_ref):
    tm = x_ref.shape[1]
    xn = _rms(x_ref[0], g_ref[...]).astype(BF16)
    n = _dot(xn, wn_ref[...])
    ka = n[:, :MOBA_HEADS * LANES]
    ka_ref[0] = (ka + ext_ref[...].astype(F32)).astype(BF16)
    ub_ref[0] = n[:, MOBA_HEADS * LANES:]
    for j in range(tm // MOBA_BLOCK):
        km_ref[0, j] = jnp.mean(ka[j * MOBA_BLOCK:(j + 1) * MOBA_BLOCK], axis=0, keepdims=True)
    qT_ref[0] = _dot_nt(wqT_ref[...], xn)
    vT = _dot_nt(wvT_ref[...], xn)
    for j in range(tm // ATTN_BLOCK):
        vT_ref[0, j] = vT[:, j * ATTN_BLOCK:(j + 1) * ATTN_BLOCK].astype(BF16)


def _moba_key_extras(S, slopes):
    pos = np.arange(S)
    blk, off = pos // MOBA_BLOCK, pos % MOBA_BLOCK
    ext = np.zeros((S, MOBA_HEADS, LANES), np.float32)
    ext[pos, :, KAUG_SEL + blk] = 1.0
    for term in range(len(LOG2E_TERMS)):
        ext[:, :, KAUG_POS + 2 * term] = slopes[None, :] * (MOBA_BLOCK * blk)[:, None]
        ext[:, :, KAUG_POS + 2 * term + 1] = slopes[None, :] * off[:, None]
    return jnp.asarray(ext.reshape(S, MOBA_HEADS * LANES), dtype=BF16)


def _even_in(x, g, w_in, slopes):
    B, S, D = x.shape
    tm = ROW_TILE
    nb = S // MOBA_BLOCK
    wq, wk, wv, wu = (w_in[:, i * MIX_WIDTH:(i + 1) * MIX_WIDTH] for i in range(4))
    wk_slots = jnp.concatenate(
        [wk.reshape(D, MOBA_HEADS, MOBA_HEAD_DIM),
         jnp.zeros((D, MOBA_HEADS, LANES - MOBA_HEAD_DIM), F32)], axis=2).reshape(D, MOBA_HEADS * LANES)
    wn = jnp.concatenate([wk_slots, wu], axis=1).astype(BF16)
    wqT = wq.T.astype(BF16)
    wvT = wv.T.astype(BF16)
    const = lambda b, i: (0, 0)
    return pl.pallas_call(
        _even_in_kernel,
        grid=(B, S // tm),
        in_specs=[
            pl.BlockSpec((1, tm, D), lambda b, i: (b, i, 0)),
            pl.BlockSpec((1, D), const),
            pl.BlockSpec((D, MOBA_HEADS * LANES + MIX_WIDTH), const),
            pl.BlockSpec((MIX_WIDTH, D), const),
            pl.BlockSpec((MIX_WIDTH, D), const),
            pl.BlockSpec((tm, MOBA_HEADS * LANES), lambda b, i: (i, 0)),
        ],
        out_specs=[
            pl.BlockSpec((1, tm, MOBA_HEADS * LANES), lambda b, i: (b, i, 0)),
            pl.BlockSpec((1, tm, MIX_WIDTH), lambda b, i: (b, i, 0)),
            pl.BlockSpec((1, tm // MOBA_BLOCK, 1, MOBA_HEADS * LANES), lambda b, i: (b, i, 0, 0)),
            pl.BlockSpec((1, MIX_WIDTH, tm), lambda b, i: (b, 0, i)),
            pl.BlockSpec((1, tm // ATTN_BLOCK, MIX_WIDTH, ATTN_BLOCK), lambda b, i: (b, i, 0, 0)),
        ],
        out_shape=[
            jax.ShapeDtypeStruct((B, S, MOBA_HEADS * LANES), BF16),
            jax.ShapeDtypeStruct((B, S, MIX_WIDTH), F32),
            jax.ShapeDtypeStruct((B, nb, 1, MOBA_HEADS * LANES), F32),
            jax.ShapeDtypeStruct((B, MIX_WIDTH, S), F32),
            jax.ShapeDtypeStruct((B, S // ATTN_BLOCK, MIX_WIDTH, ATTN_BLOCK), BF16),
        ],
        compiler_params=_params(("parallel", "parallel")),
        name="even_in",
    )(x, g.reshape(1, D), wn, wqT, wvT, _moba_key_extras(S, slopes))


def _moba_gate_kernel(km_ref, qT_ref, qa_ref):
    i = pl.program_id(1)
    nb = km_ref.shape[2]
    tq = qT_ref.shape[2]
    row = lax.broadcasted_iota(jnp.int32, (nb, tq), 0)
    own = (i * tq + lax.broadcasted_iota(jnp.int32, (nb, tq), 1)) // MOBA_BLOCK
    past = row < own
    tail_row = lax.broadcasted_iota(jnp.int32, (LANES - KAUG_POS, tq), 0)
    tail = jnp.zeros(tail_row.shape, F32)
    for term, value in enumerate(LOG2E_TERMS):
        tail = jnp.where(tail_row // 2 == term, F32(value), tail)
    tail = tail.astype(BF16)
    pad = jnp.zeros((MOBA_MAX_BLOCKS - nb, tq), BF16) if nb < MOBA_MAX_BLOCKS else None
    for h in range(MOBA_HEADS):
        q_h = qT_ref[0, h * MOBA_HEAD_DIM:(h + 1) * MOBA_HEAD_DIM, :]
        gate = jnp.dot(km_ref[0, h], q_h, precision=HIGHEST, preferred_element_type=F32)
        gate = jnp.where(past, gate, NEG_INF)
        chosen = jnp.zeros(gate.shape, F32)
        for _ in range(MOBA_TOPK):
            mx = jnp.max(gate, axis=0, keepdims=True)
            first = jnp.min(jnp.where(gate == mx, row, nb), axis=0, keepdims=True)
            pick = row == first
            chosen = jnp.where(pick, 1.0, chosen)
            gate = jnp.where(pick, -jnp.inf, gate)
        keep = jnp.where(past, chosen, (row == own).astype(F32))
        sel = jnp.where(keep > 0.0, 0.0, NEG_INF).astype(BF16)
        base = h * LANES
        qa_ref[0, base:base + KAUG_SEL] = (q_h * (MOBA_HEAD_DIM ** -0.5 * LOG2E)).astype(BF16)
        qa_ref[0, base + KAUG_SEL:base + KAUG_SEL + nb] = sel
        if pad is not None:
            qa_ref[0, base + KAUG_SEL + nb:base + KAUG_POS] = pad
        qa_ref[0, base + KAUG_POS:base + LANES] = tail


def _moba_gate(kmean, qT):
    B, H, nb, dh = kmean.shape
    S = qT.shape[2]
    tq = ATTN_BLOCK
    return pl.pallas_call(
        _moba_gate_kernel,
        grid=(B, S // tq),
        in_specs=[
            pl.BlockSpec((1, H, nb, dh), lambda b, i: (b, 0, 0, 0)),
            pl.BlockSpec((1, H * dh, tq), lambda b, i: (b, 0, i)),
        ],
        out_specs=pl.BlockSpec((1, H * LANES, tq), lambda b, i: (b, 0, i)),
        out_shape=jax.ShapeDtypeStruct((B, H * LANES, S), BF16),
        compiler_params=_params(("parallel", "parallel")),
        name="moba_gate",
    )(kmean, qT)


HEADS_PER_STEP = 2
SUM_ROWS = 16


def _attn_kernel(q_ref, k_ref, v_ref, o_ref, sa_ref, sb_ref, m_ref, acc_ref):
    i = pl.program_id(2)
    tq = q_ref.shape[2]
    tk = ATTN_BLOCK
    hp = HEADS_PER_STEP
    dv = v_ref.shape[2] // hp
    m_ref[...] = jnp.full(m_ref.shape, NEG_INF, F32)
    acc_ref[...] = jnp.zeros(acc_ref.shape, F32)
    ones_rows = jnp.ones((SUM_ROWS, tk), BF16)

    def scores(kvt, s_ref, diag):
        start = pl.multiple_of(kvt * tk, tk)
        k_tile = k_ref[0, pl.ds(start, tk), :]
        for g in range(hp):
            s = _dot(k_tile[:, g * LANES:(g + 1) * LANES], q_ref[0, g * LANES:(g + 1) * LANES, :])
            if diag:
                key = lax.broadcasted_iota(jnp.int32, (tk, tq), 0)
                qry = lax.broadcasted_iota(jnp.int32, (tk, tq), 1)
                s = jnp.where(key <= qry, s, NEG_INF)
            s_ref[g] = s

    def consume(kvt, s_ref):
        v_tile = v_ref[0, kvt]
        for g in range(hp):
            s = s_ref[g]
            m_run = m_ref[g]
            m_new = jnp.maximum(m_run, jnp.max(s, axis=0, keepdims=True))
            p = jnp.exp2(s - m_new).astype(BF16)
            v_aug = jnp.concatenate([v_tile[g * dv:(g + 1) * dv], ones_rows], axis=0)
            acc_ref[g] = jnp.exp2(m_run - m_new) * acc_ref[g] + _dot(v_aug, p)
            m_ref[g] = m_new

    tile_at = lambda t: jnp.where(t == 0, i, t - 1)
    scores(i, sa_ref, True)

    def pair(p, carry):
        t = 2 * p
        scores(tile_at(t + 1), sb_ref, False)
        consume(tile_at(t), sa_ref)
        scores(tile_at(t + 2), sa_ref, False)
        consume(tile_at(t + 1), sb_ref)
        return carry

    lax.fori_loop(0, i // 2, pair, 0)
    last = 2 * (i // 2)

    @pl.when(i % 2 == 1)
    def _():
        scores(tile_at(last + 1), sb_ref, False)
        consume(tile_at(last), sa_ref)
        consume(tile_at(last + 1), sb_ref)

    @pl.when(i % 2 == 0)
    def _():
        consume(tile_at(last), sa_ref)

    for g in range(hp):
        o_ref[0, g * dv:(g + 1) * dv, :] = acc_ref[g, :dv] / acc_ref[g, dv:dv + 1]


def _attention(qT, k, vT, dv, heads, name):
    B, _, S = qT.shape
    tq = ATTN_BLOCK
    nt = S // ATTN_BLOCK
    hp = HEADS_PER_STEP
    return pl.pallas_call(
        _attn_kernel,
        grid=(B, heads // hp, S // tq),
        in_specs=[
            pl.BlockSpec((1, hp * LANES, tq), lambda b, h, i: (b, h, i)),
            pl.BlockSpec((1, S, hp * LANES), lambda b, h, i: (b, 0, h)),
            pl.BlockSpec((1, nt, hp * dv, ATTN_BLOCK), lambda b, h, i: (b, 0, h, 0)),
        ],
        out_specs=pl.BlockSpec((1, hp * dv, tq), lambda b, h, i: (b, h, i)),
        out_shape=jax.ShapeDtypeStruct((B, heads * dv, S), F32),
        scratch_shapes=[pltpu.VMEM((hp, ATTN_BLOCK, tq), F32), pltpu.VMEM((hp, ATTN_BLOCK, tq), F32),
                        pltpu.VMEM((hp, 1, tq), F32), pltpu.VMEM((hp, dv + SUM_ROWS, tq), F32)],
        compiler_params=_params(("parallel", "parallel", "arbitrary")),
        name=name,
    )(qT, k, vT)


def _pool_kernel(x_ref, halo_ref, w_ref, sc_ref, o_ref, xs_ref):
    i = pl.program_id(1)
    tm = x_ref.shape[1]
    x = x_ref[0]
    xs_ref[0:POOL_HALO] = jnp.where(i > 0, halo_ref[0], 0.0)
    xs_ref[POOL_HALO:POOL_HALO + tm] = x
    t = i * tm + lax.broadcasted_iota(jnp.int32, (tm, 1), 0)
    outs = []
    for g, win in enumerate(POOL_WINDOWS):
        lo, hi = g * POOL_GROUP_DIM, (g + 1) * POOL_GROUP_DIM
        xg = x[:, lo:hi]
        acc = xg
        for d in range(1, win):
            acc = acc + xs_ref[POOL_HALO - d:POOL_HALO - d + tm, lo:hi]
        count = jnp.minimum(t + 1, win).astype(F32)
        outs.append(_dot((acc / count - xg).astype(BF16), w_ref[g]))
    o_ref[0] = (jnp.concatenate(outs, axis=1) * sc_ref[...]).astype(o_ref.dtype)


def _pool(ub, pool_w, pool_scale):
    B, S, W = ub.shape
    tm = ROW_TILE
    per = tm // POOL_HALO
    return pl.pallas_call(
        _pool_kernel,
        grid=(B, S // tm),
        in_specs=[
            pl.BlockSpec((1, tm, W), lambda b, i: (b, i, 0)),
            pl.BlockSpec((1, POOL_HALO, W), lambda b, i: (b, jnp.maximum(i * per - 1, 0), 0)),
            pl.BlockSpec(pool_w.shape, lambda b, i: (0, 0, 0)),
            pl.BlockSpec((1, W), lambda b, i: (0, 0)),
        ],
        out_specs=pl.BlockSpec((1, tm, W), lambda b, i: (b, i, 0)),
        out_shape=jax.ShapeDtypeStruct((B, S, W), BF16),
        scratch_shapes=[pltpu.VMEM((POOL_HALO + tm, W), F32)],
        compiler_params=_params(("parallel", "parallel")),
        name="pool",
    )(ub, ub, pool_w.astype(BF16), pool_scale.reshape(1, W))


def _mix_out_kernel(h_ref, aT_ref, b_ref, w_ref, o_ref, *, a_first):
    a = aT_ref[0].T.astype(BF16)
    b = b_ref[0].astype(BF16)
    lo, hi = (a, b) if a_first else (b, a)
    y = _dot(lo, w_ref[:MIX_WIDTH]) + _dot(hi, w_ref[MIX_WIDTH:])
    o_ref[0] = h_ref[0] + y


def _mix_out(h, aT, b, w_out, a_first):
    B, S, D = h.shape
    tm = ROW_TILE
    return pl.pallas_call(
        functools.partial(_mix_out_kernel, a_first=a_first),
        grid=(B, S // tm),
        in_specs=[
            pl.BlockSpec((1, tm, D), lambda b_, i: (b_, i, 0)),
            pl.BlockSpec((1, MIX_WIDTH, tm), lambda b_, i: (b_, 0, i)),
            pl.BlockSpec((1, tm, MIX_WIDTH), lambda b_, i: (b_, i, 0)),
            pl.BlockSpec((2 * MIX_WIDTH, D), lambda b_, i: (0, 0)),
        ],
        out_specs=pl.BlockSpec((1, tm, D), lambda b_, i: (b_, i, 0)),
        out_shape=jax.ShapeDtypeStruct((B, S, D), F32),
        compiler_params=_params(("parallel", "parallel")),
        name="mix_out",
    )(h, aT, b, w_out.astype(BF16))


def _swiglu_step(xn, wg, wu, wd):
    gt = _dot(xn, wg)
    up = _dot(xn, wu)
    return _dot((gt * _sigmoid(gt) * up).astype(BF16), wd)


def _ffn_kernel(h_ref, g_ref, wg_ref, wu_ref, wd_ref, o_ref, xn_ref, acc_ref):
    f = pl.program_id(1)

    @pl.when(f == 0)
    def _():
        xn_ref[...] = _rms(h_ref[...], g_ref[...]).astype(BF16)
        acc_ref[...] = jnp.zeros_like(acc_ref)

    acc_ref[...] += _swiglu_step(xn_ref[...], wg_ref[...], wu_ref[...], wd_ref[...])

    @pl.when(f == pl.num_programs(1) - 1)
    def _():
        o_ref[...] = h_ref[...] + acc_ref[...]


def _ffn(h2d, g, wg, wu, wd, tm=ROW_TILE, tf=FFN_DIM // 2):
    T, D = h2d.shape
    F = wg.shape[1]
    return pl.pallas_call(
        _ffn_kernel,
        grid=(T // tm, F // tf),
        in_specs=[
            pl.BlockSpec((tm, D), lambda i, f: (i, 0)),
            pl.BlockSpec((1, D), lambda i, f: (0, 0)),
            pl.BlockSpec((D, tf), lambda i, f: (0, f)),
            pl.BlockSpec((D, tf), lambda i, f: (0, f)),
            pl.BlockSpec((tf, D), lambda i, f: (f, 0)),
        ],
        out_specs=pl.BlockSpec((tm, D), lambda i, f: (i, 0)),
        out_shape=jax.ShapeDtypeStruct((T, D), F32),
        scratch_shapes=[pltpu.VMEM((tm, D), BF16), pltpu.VMEM((tm, D), F32)],
        compiler_params=_params(("parallel", "arbitrary")),
        name="ffn",
    )(h2d, g.reshape(1, D), wg, wu, wd)


ROUTE_E0, ROUTE_E1, ROUTE_W0, ROUTE_W1, ROUTE_R0, ROUTE_R1 = range(6)
MOE_TILE = 512
MOE_TF = EXPERT_DIM // 2


def _lane_pick(tile, lane, idx):
    return jnp.sum(jnp.where(lane == idx, tile, 0.0), axis=1, keepdims=True)


def _router_kernel(h_ref, g_ref, w_ref, b_ref, route_ref, cnt_ref):
    tm = h_ref.shape[0]

    @pl.when(pl.program_id(0) == 0)
    def _():
        cnt_ref[...] = jnp.zeros_like(cnt_ref)

    xn = _rms(h_ref[...], g_ref[...])
    logits = jnp.dot(xn, w_ref[...], precision=HIGHEST, preferred_element_type=F32) + b_ref[...]
    lane = lax.broadcasted_iota(jnp.int32, logits.shape, 1)
    logits = jnp.where(lane < N_EXPERTS, logits, -jnp.inf)
    v0 = jnp.max(logits, axis=1, keepdims=True)
    i0 = jnp.min(jnp.where(logits == v0, lane, LANES), axis=1, keepdims=True)
    rest = jnp.where(lane == i0, -jnp.inf, logits)
    v1 = jnp.max(rest, axis=1, keepdims=True)
    i1 = jnp.min(jnp.where(rest == v1, lane, LANES), axis=1, keepdims=True)
    e1 = jnp.exp(v1 - v0)
    w0 = 1.0 / (1.0 + e1)
    sel = (lane == i0).astype(F32) + (lane == i1).astype(F32)
    earlier = (lax.broadcasted_iota(jnp.int32, (tm, tm), 1)
               < lax.broadcasted_iota(jnp.int32, (tm, tm), 0))
    rank = _dot(earlier.astype(BF16), sel.astype(BF16)) + cnt_ref[...]
    cnt_ref[...] += jnp.sum(sel, axis=0, keepdims=True)
    cols = ((ROUTE_E0, i0.astype(F32)), (ROUTE_E1, i1.astype(F32)), (ROUTE_W0, w0), (ROUTE_W1, e1 * w0),
            (ROUTE_R0, _lane_pick(rank, lane, i0)), (ROUTE_R1, _lane_pick(rank, lane, i1)))
    route = jnp.zeros(logits.shape, F32)
    for c, val in cols:
        route = jnp.where(lane == c, val, route)
    route_ref[...] = route


def _router(h2d, g, router_w, router_b):
    T, D = h2d.shape
    tm = ROW_TILE
    w = jnp.zeros((D, LANES), F32).at[:, :N_EXPERTS].set(router_w)
    b = jnp.zeros((1, LANES), F32).at[0, :N_EXPERTS].set(router_b)
    return pl.pallas_call(
        _router_kernel,
        grid=(T // tm,),
        in_specs=[
            pl.BlockSpec((tm, D), lambda i: (i, 0)),
            pl.BlockSpec((1, D), lambda i: (0, 0)),
            pl.BlockSpec((D, LANES), lambda i: (0, 0)),
            pl.BlockSpec((1, LANES), lambda i: (0, 0)),
        ],
        out_specs=[pl.BlockSpec((tm, LANES), lambda i: (i, 0)),
                   pl.BlockSpec((1, LANES), lambda i: (0, 0))],
        out_shape=[jax.ShapeDtypeStruct((T, LANES), F32), jax.ShapeDtypeStruct((1, LANES), F32)],
        compiler_params=_params(("arbitrary",)),
        name="router",
    )(h2d, g.reshape(1, D), w, b)


def _row_copies(pos_ref, base, r, src_of, dst_of, sem):
    return [pltpu.make_async_copy(src_of(k, pos_ref[base + 2 * r + k]),
                                  dst_of(k, pos_ref[base + 2 * r + k]), sem) for k in range(2)]


def _all_rows(tm, make):
    def issue(r, c):
        for k, cp in enumerate(make(r)):
            cp.start(priority=k)
        return c

    def drain(r, c):
        for cp in make(r):
            cp.wait()
        return c

    lax.fori_loop(0, tm, issue, 0, unroll=8)
    lax.fori_loop(0, tm, drain, 0, unroll=8)


def _dispatch_kernel(pos_ref, h_ref, g_ref, init_ref, xs_ref, xn_ref, sem):
    del init_ref
    tm = h_ref.shape[0]
    xn_ref[...] = _rms(h_ref[...], g_ref[...])
    base = pl.program_id(0) * (2 * tm)
    _all_rows(tm, lambda r: _row_copies(
        pos_ref, base, r, lambda k, p: xn_ref.at[pl.ds(r, 1)], lambda k, p: xs_ref.at[pl.ds(p, 1)], sem))


def _dispatch(pos, h2d, g, n_rows):
    T, D = h2d.shape
    tm = ROW_TILE
    return pl.pallas_call(
        _dispatch_kernel,
        grid_spec=pltpu.PrefetchScalarGridSpec(
            num_scalar_prefetch=1,
            grid=(T // tm,),
            in_specs=[pl.BlockSpec((tm, D), lambda i, pos: (i, 0)),
                      pl.BlockSpec((1, D), lambda i, pos: (0, 0)),
                      pl.BlockSpec(memory_space=pl.ANY)],
            out_specs=pl.BlockSpec(memory_space=pl.ANY),
            scratch_shapes=[pltpu.VMEM((tm, D), F32), pltpu.SemaphoreType.DMA(())],
        ),
        out_shape=jax.ShapeDtypeStruct((n_rows, D), F32),
        input_output_aliases={3: 0},
        compiler_params=_params(("arbitrary",)),
        name="moe_dispatch",
    )(pos, h2d, g.reshape(1, D), jnp.zeros((n_rows, D), F32))


def _moe_ffn_kernel(te_ref, nv_ref, x_ref, wg_ref, wu_ref, wd_ref, o_ref, xb_ref, acc_ref):
    j = pl.program_id(0)
    f = pl.program_id(1)
    valid = j < nv_ref[0]

    @pl.when(f == 0)
    def _():
        xb_ref[...] = x_ref[...].astype(BF16)
        acc_ref[...] = jnp.zeros_like(acc_ref)

    @pl.when(valid)
    def _():
        acc_ref[...] += _swiglu_step(xb_ref[...], wg_ref[0], wu_ref[0], wd_ref[0])

    @pl.when(f == pl.num_programs(1) - 1)
    def _():
        o_ref[...] = acc_ref[...]


def _moe_ffn(tile_expert, n_valid, xs, wg, wu, wd):
    N, D = xs.shape
    tm, tf = MOE_TILE, MOE_TF
    F = wg.shape[2]
    return pl.pallas_call(
        _moe_ffn_kernel,
        grid_spec=pltpu.PrefetchScalarGridSpec(
            num_scalar_prefetch=2,
            grid=(N // tm, F // tf),
            in_specs=[pl.BlockSpec((tm, D), lambda j, f, te, nv: (j, 0)),
                      pl.BlockSpec((1, D, tf), lambda j, f, te, nv: (te[j], 0, f)),
                      pl.BlockSpec((1, D, tf), lambda j, f, te, nv: (te[j], 0, f)),
                      pl.BlockSpec((1, tf, D), lambda j, f, te, nv: (te[j], f, 0))],
            out_specs=pl.BlockSpec((tm, D), lambda j, f, te, nv: (j, 0)),
            scratch_shapes=[pltpu.VMEM((tm, D), BF16), pltpu.VMEM((tm, D), F32)],
        ),
        out_shape=jax.ShapeDtypeStruct((N, D), F32),
        compiler_params=_params(("arbitrary", "arbitrary")),
        name="moe_ffn",
    )(tile_expert, n_valid, xs, wg, wu, wd)


def _combine_kernel(pos_ref, h_ref, route_ref, ys_ref, o_ref, y_ref, sem):
    tm = h_ref.shape[0]
    base = pl.program_id(0) * (2 * tm)
    _all_rows(tm, lambda r: _row_copies(
        pos_ref, base, r, lambda k, p: ys_ref.at[pl.ds(p, 1)], lambda k, p: y_ref.at[k, pl.ds(r, 1)], sem))
    route = route_ref[...]
    lane = lax.broadcasted_iota(jnp.int32, route.shape, 1)
    o_ref[...] = (h_ref[...] + _lane_pick(route, lane, ROUTE_W0) * y_ref[0]
                  + _lane_pick(route, lane, ROUTE_W1) * y_ref[1])


def _combine(pos, h2d, route, ys):
    T, D = h2d.shape
    tm = ROW_TILE
    return pl.pallas_call(
        _combine_kernel,
        grid_spec=pltpu.PrefetchScalarGridSpec(
            num_scalar_prefetch=1,
            grid=(T // tm,),
            in_specs=[pl.BlockSpec((tm, D), lambda i, pos: (i, 0)),
                      pl.BlockSpec((tm, LANES), lambda i, pos: (i, 0)),
                      pl.BlockSpec(memory_space=pl.ANY)],
            out_specs=pl.BlockSpec((tm, D), lambda i, pos: (i, 0)),
            scratch_shapes=[pltpu.VMEM((2, tm, D), F32), pltpu.SemaphoreType.DMA(())],
        ),
        out_shape=jax.ShapeDtypeStruct((T, D), F32),
        compiler_params=_params(("arbitrary",)),
        name="moe_combine",
    )(pos, h2d, route, ys)


def _moe(h2d, g, router_w, router_b, wg, wu, wd):
    T, D = h2d.shape
    tm = MOE_TILE
    route, counts = _router(h2d, g, router_w, router_b)
    cnt = counts[0, :N_EXPERTS].astype(jnp.int32)
    padded = (cnt + tm - 1) // tm * tm
    ends = jnp.cumsum(padded)
    start = ends - padded
    e01 = route[:, ROUTE_E0:ROUTE_E1 + 1].astype(jnp.int32)
    r01 = route[:, ROUTE_R0:ROUTE_R1 + 1].astype(jnp.int32)
    pos = (start[e01] + r01).reshape(2 * T)
    n_rows = 2 * T + N_EXPERTS * tm
    tile_row = jnp.arange(n_rows // tm, dtype=jnp.int32) * tm
    tile_expert = jnp.minimum(jnp.sum(tile_row[:, None] >= ends[None, :], axis=1), N_EXPERTS - 1).astype(jnp.int32)
    n_valid = (ends[-1:] // tm).astype(jnp.int32)
    xs = _dispatch(pos, h2d, g, n_rows)
    ys = _moe_ffn(tile_expert, n_valid, xs, wg, wu, wd)
    return _combine(pos, h2d, route, ys)


def _ple_kernel(*refs, final):
    if final:
        h_ref, p_ref, g_ref, wg_ref, wp_ref, fg_ref, o_ref = refs
    else:
        h_ref, p_ref, g_ref, wg_ref, wp_ref, o_ref = refs
    h = h_ref[...]
    gate = _sigmoid(_dot(_rms(h, g_ref[...]).astype(BF16), wg_ref[...]))
    out = h + gate * _dot(p_ref[...].astype(BF16), wp_ref[...])
    if final:
        out = _rms(out, fg_ref[...])
    o_ref[...] = out


def _ple(h2d, p2d, g, w_gate, w_proj, final_g=None):
    T, D = h2d.shape
    tm = ROW_TILE
    final = final_g is not None
    in_specs = [
        pl.BlockSpec((tm, D), lambda i: (i, 0)),
        pl.BlockSpec((tm, PLE_DIM), lambda i: (i, 0)),
        pl.BlockSpec((1, D), lambda i: (0, 0)),
        pl.BlockSpec((D, D), lambda i: (0, 0)),
        pl.BlockSpec((PLE_DIM, D), lambda i: (0, 0)),
    ]
    args = [h2d, p2d, g.reshape(1, D), w_gate.astype(BF16), w_proj.astype(BF16)]
    if final:
        in_specs.append(pl.BlockSpec((1, D), lambda i: (0, 0)))
        args.append(final_g.reshape(1, D))
    return pl.pallas_call(
        functools.partial(_ple_kernel, final=final),
        grid=(T // tm,),
        in_specs=in_specs,
        out_specs=pl.BlockSpec((tm, D), lambda i: (i, 0)),
        out_shape=jax.ShapeDtypeStruct((T, D), F32),
        compiler_params=_params(("parallel",)),
        name="ple_final" if final else "ple",
    )(*args)


ODD_MAIN = 4 * MIX_WIDTH
ODD_COLS = ODD_MAIN + MLA_Q_RANK + MLA_KV_RANK + 2 * LANES
MLA_QK_SCALE = (MLA_NOPE_DIM + MLA_ROPE_DIM) ** -0.5 * LOG2E
ROPE_HALF = MLA_ROPE_DIM // 2


def _odd_in_kernel(x_ref, g_ref, wn_ref, qn_ref, wuqT_ref, kvn_ref, wk2_ref, wvT_ref,
                   cosT_ref, sinT_ref, cc_ref, ss_ref,
                   qk_ref, vc_ref, op_ref, misc_ref, mq_ref, mk_ref, mv_ref):
    tm = x_ref.shape[1]
    xn = _rms(x_ref[0], g_ref[...]).astype(BF16)
    u = _dot(xn, wn_ref[...])
    qk_ref[0] = u[:, :2 * MIX_WIDTH]
    vc_ref[0] = u[:, 2 * MIX_WIDTH:3 * MIX_WIDTH].astype(BF16)
    op_ref[0] = u[:, 3 * MIX_WIDTH:ODD_MAIN]
    c0 = ODD_MAIN
    c_q = u[:, c0:c0 + MLA_Q_RANK]
    c0 += MLA_Q_RANK
    c_kv = u[:, c0:c0 + MLA_KV_RANK]
    c0 += MLA_KV_RANK
    misc = u[:, c0:c0 + LANES]
    misc_sw = u[:, c0 + LANES:c0 + 2 * LANES]
    misc_ref[0] = misc
    cqn = _rms(c_q, qn_ref[...]).astype(BF16)
    qT = _dot_nt(wuqT_ref[...], cqn)
    cosT = cosT_ref[...]
    sinT = sinT_ref[...]
    for h in range(MLA_HEADS):
        r = h * LANES
        mq_ref[0, r:r + MLA_NOPE_DIM] = (qT[r:r + MLA_NOPE_DIM] * MLA_QK_SCALE).astype(BF16)
        x1 = qT[r + MLA_NOPE_DIM:r + MLA_NOPE_DIM + ROPE_HALF]
        x2 = qT[r + MLA_NOPE_DIM + ROPE_HALF:r + MLA_NOPE_DIM + MLA_ROPE_DIM]
        mq_ref[0, r + MLA_NOPE_DIM:r + MLA_NOPE_DIM + ROPE_HALF] = (
            (x1 * cosT - x2 * sinT) * MLA_QK_SCALE).astype(BF16)
        mq_ref[0, r + MLA_NOPE_DIM + ROPE_HALF:r + MLA_NOPE_DIM + MLA_ROPE_DIM] = (
            (x1 * sinT + x2 * cosT) * MLA_QK_SCALE).astype(BF16)
        mq_ref[0, r + MLA_NOPE_DIM + MLA_ROPE_DIM:r + LANES] = jnp.zeros(
            (LANES - MLA_NOPE_DIM - MLA_ROPE_DIM, tm), BF16)
    ckvn = _rms(c_kv, kvn_ref[...]).astype(BF16)
    k_rot = (misc * cc_ref[...] + misc_sw * ss_ref[...]).astype(BF16)
    mk_ref[0] = _dot(jnp.concatenate([ckvn, k_rot], axis=1), wk2_ref[...]).astype(BF16)
    vT = _dot_nt(wvT_ref[...], ckvn)
    for j in range(tm // ATTN_BLOCK):
        mv_ref[0, j] = vT[:, j * ATTN_BLOCK:(j + 1) * ATTN_BLOCK].astype(BF16)


def _rope_tables(S):
    inv_freq = ROPE_BASE ** (-jnp.arange(ROPE_HALF, dtype=F32) / ROPE_HALF)
    ang = jnp.arange(S, dtype=F32)[:, None] * inv_freq[None, :]
    cos, sin = jnp.cos(ang), jnp.sin(ang)
    pad = jnp.zeros((S, LANES - MLA_ROPE_DIM), F32)
    cc = jnp.concatenate([cos, cos, pad], axis=1)
    ss = jnp.concatenate([-sin, sin, pad], axis=1)
    return cos.T, sin.T, cc, ss


def _odd_in(x, g, w_in, q_norm, w_uq, kv_norm, w_ukv):
    B, S, D = x.shape
    tm = ROW_TILE
    cuts = np.cumsum([MIX_WIDTH] * 4 + [MLSTM_HEADS, MLSTM_HEADS, MLA_Q_RANK, MLA_KV_RANK]).tolist()
    w_main = w_in[:, :cuts[3]]
    w_i = w_in[:, cuts[3]:cuts[4]]
    w_f = w_in[:, cuts[4]:cuts[5]]
    w_cq = w_in[:, cuts[5]:cuts[6]]
    w_ckv = w_in[:, cuts[6]:cuts[7]]
    w_kr = w_in[:, cuts[7]:]
    w_kr_sw = jnp.concatenate([w_kr[:, ROPE_HALF:], w_kr[:, :ROPE_HALF]], axis=1)
    zpad = lambda n: jnp.zeros((D, n), F32)
    w_misc = jnp.concatenate([w_kr, w_i, w_f, zpad(LANES - MLA_ROPE_DIM - 2 * MLSTM_HEADS)], axis=1)
    w_misc_sw = jnp.concatenate([w_kr_sw, zpad(LANES - MLA_ROPE_DIM)], axis=1)
    wn = jnp.concatenate([w_main, w_cq, w_ckv, w_misc, w_misc_sw], axis=1).astype(BF16)
    qd = MLA_NOPE_DIM + MLA_ROPE_DIM
    w_uq_h = w_uq.reshape(MLA_Q_RANK, MLA_HEADS, qd)
    w_uq_h = jnp.concatenate([w_uq_h, jnp.zeros((MLA_Q_RANK, MLA_HEADS, LANES - qd), F32)], axis=2)
    wuqT = w_uq_h.reshape(MLA_Q_RANK, MLA_HEADS * LANES).T.astype(BF16)
    w_ukv_h = w_ukv.reshape(MLA_KV_RANK, MLA_HEADS, MLA_NOPE_DIM + MLA_V_DIM)
    w_k = jnp.concatenate([w_ukv_h[:, :, :MLA_NOPE_DIM],
                           jnp.zeros((MLA_KV_RANK, MLA_HEADS, LANES - MLA_NOPE_DIM), F32)], axis=2)
    place = jnp.zeros((LANES, MLA_HEADS, LANES), F32)
    eye = jnp.eye(MLA_ROPE_DIM, dtype=F32)
    place = place.at[:MLA_ROPE_DIM, :, MLA_NOPE_DIM:MLA_NOPE_DIM + MLA_ROPE_DIM].set(
        jnp.broadcast_to(eye[:, None, :], (MLA_ROPE_DIM, MLA_HEADS, MLA_ROPE_DIM)))
    wk2 = jnp.concatenate([w_k, place], axis=0).reshape(MLA_KV_RANK + LANES, MLA_HEADS * LANES).astype(BF16)
    wvT = w_ukv_h[:, :, MLA_NOPE_DIM:].reshape(MLA_KV_RANK, MLA_HEADS * MLA_V_DIM).T.astype(BF16)
    cosT, sinT, cc, ss = _rope_tables(S)
    row = lambda b, i: (b, i, 0)
    const = lambda b, i: (0, 0)
    nb = S // ATTN_BLOCK
    return pl.pallas_call(
        _odd_in_kernel,
        grid=(B, S // tm),
        in_specs=[
            pl.BlockSpec((1, tm, D), row),
            pl.BlockSpec((1, D), const),
            pl.BlockSpec((D, ODD_COLS), const),
            pl.BlockSpec((1, MLA_Q_RANK), const),
            pl.BlockSpec((MLA_HEADS * LANES, MLA_Q_RANK), const),
            pl.BlockSpec((1, MLA_KV_RANK), const),
            pl.BlockSpec((MLA_KV_RANK + LANES, MLA_HEADS * LANES), const),
            pl.BlockSpec((MLA_HEADS * MLA_V_DIM, MLA_KV_RANK), const),
            pl.BlockSpec((ROPE_HALF, tm), lambda b, i: (0, i)),
            pl.BlockSpec((ROPE_HALF, tm), lambda b, i: (0, i)),
            pl.BlockSpec((tm, LANES), lambda b, i: (i, 0)),
            pl.BlockSpec((tm, LANES), lambda b, i: (i, 0)),
        ],
        out_specs=[
            pl.BlockSpec((1, tm, 2 * MIX_WIDTH), row),
            pl.BlockSpec((1, tm, MIX_WIDTH), row),
            pl.BlockSpec((1, tm, MIX_WIDTH), row),
            pl.BlockSpec((1, tm, LANES), row),
            pl.BlockSpec((1, MLA_HEADS * LANES, tm), lambda b, i: (b, 0, i)),
            pl.BlockSpec((1, tm, MLA_HEADS * LANES), row),
            pl.BlockSpec((1, tm // ATTN_BLOCK, MLA_HEADS * MLA_V_DIM, ATTN_BLOCK), lambda b, i: (b, i, 0, 0)),
        ],
        out_shape=[
            jax.ShapeDtypeStruct((B, S, 2 * MIX_WIDTH), F32),
            jax.ShapeDtypeStruct((B, S, MIX_WIDTH), BF16),
            jax.ShapeDtypeStruct((B, S, MIX_WIDTH), F32),
            jax.ShapeDtypeStruct((B, S, LANES), F32),
            jax.ShapeDtypeStruct((B, MLA_HEADS * LANES, S), BF16),
            jax.ShapeDtypeStruct((B, S, MLA_HEADS * LANES), BF16),
            jax.ShapeDtypeStruct((B, nb, MLA_HEADS * MLA_V_DIM, ATTN_BLOCK), BF16),
        ],
        compiler_params=_params(("parallel", "parallel")),
        name="odd_in",
    )(x, g.reshape(1, D), wn, q_norm.reshape(1, -1), wuqT, kv_norm.reshape(1, -1), wk2, wvT,
      cosT, sinT, cc, ss)


def _log_sigmoid(x):
    return jnp.minimum(x, 0.0) - jnp.log(1.0 + jnp.exp(-jnp.abs(x)))


def _mlstm_kernel(qk_ref, v_ref, op_ref, misc_ref, cw_ref, gb_ref, hn_ref, o_ref,
                  prev_ref, cn_ref, m_ref):
    c = pl.program_id(1)
    L = qk_ref.shape[1]
    row = lax.broadcasted_iota(jnp.int32, (L, 1), 0)
    lane = lax.broadcasted_iota(jnp.int32, (L, LANES), 1)
    is_f = (lane >= MISC_F) & (lane < MISC_F + MLSTM_HEADS)
    causal = lax.broadcasted_iota(jnp.int32, (L, L), 1) <= lax.broadcasted_iota(jnp.int32, (L, L), 0)
    ones_col = (lane == 0).astype(BF16)

    @pl.when(c == 0)
    def _():
        prev_ref[...] = jnp.zeros_like(prev_ref)
        cn_ref[...] = jnp.zeros_like(cn_ref)
        m_ref[...] = jnp.zeros_like(m_ref)

    seqs = range(qk_ref.shape[0])
    heads = [(bb, h) for bb in seqs for h in range(MLSTM_HEADS)]
    d = MLSTM_HEAD_DIM

    qk = []
    for bb in seqs:
        x = qk_ref[bb]
        prev = prev_ref[bb]
        conv = x * cw_ref[CONV_WIDTH - 1:CONV_WIDTH, :]
        for j in range(1, CONV_WIDTH):
            shifted = pltpu.roll(jnp.where(row >= L - j, prev, x), j, axis=0)
            conv = conv + shifted * cw_ref[CONV_WIDTH - 1 - j:CONV_WIDTH - j, :]
        prev_ref[bb] = x
        qk.append(conv * _sigmoid(conv))

    zs, zTs = [], []
    for bb in seqs:
        gates = misc_ref[bb] + gb_ref[...]
        z = jnp.where(is_f, _log_sigmoid(gates), gates)
        cum = jnp.dot(causal.astype(F32), z, precision=HIGHEST, preferred_element_type=F32)
        z = jnp.where(is_f, cum, z)
        zs.append(z)
        zTs.append(z.T)

    q, k, v_aug, i_col, b_col, m_prev, m_t, w_inter, scores, inter = ({} for _ in range(10))
    for key in heads:
        bb, h = key
        lo, hi = h * d, (h + 1) * d
        q[key] = qk[bb][:, lo:hi].astype(BF16)
        k[key] = qk[bb][:, MIX_WIDTH + lo:MIX_WIDTH + hi] * (d ** -0.5)
        v_aug[key] = jnp.concatenate([v_ref[bb, :, lo:hi], ones_col], axis=1)
        scores[key] = _dot_nt(q[key], k[key].astype(BF16))
        inter[key] = _dot(q[key], cn_ref[bb, h].astype(BF16))
    intra = {}
    for key in heads:
        bb, h = key
        i_col[key] = zs[bb][:, MISC_I + h:MISC_I + h + 1]
        b_col[key] = zs[bb][:, MISC_F + h:MISC_F + h + 1]
        i_row = zTs[bb][MISC_I + h:MISC_I + h + 1, :]
        b_row = zTs[bb][MISC_F + h:MISC_F + h + 1, :]
        m_prev[key] = m_ref[bb, h:h + 1, 0:1]
        intra[key] = jnp.where(causal, b_col[key] - b_row + i_row, NEG_INF)
    for key in heads:
        m_inter = b_col[key] + m_prev[key]
        m_t[key] = jnp.maximum(m_inter, jnp.max(intra[key], axis=1, keepdims=True))
        w_inter[key] = jnp.exp(m_inter - m_t[key])
    intra_o = {}
    for key in heads:
        a = jnp.exp(intra[key] - m_t[key]) * scores[key]
        intra_o[key] = _dot(a.astype(BF16), v_aug[key])
    for key in heads:
        bb, h = key
        lo, hi = h * d, (h + 1) * d
        num = w_inter[key] * inter[key][:, :d] + intra_o[key][:, :d]
        den = w_inter[key] * inter[key][:, d:d + 1] + intra_o[key][:, d:d + 1]
        hh = num / jnp.maximum(jnp.abs(den), jnp.exp(-m_t[key]))
        hh = _rms(hh, hn_ref[:, lo:hi])
        o_ref[bb, :, lo:hi] = (hh * _sigmoid(op_ref[bb, :, lo:hi])).astype(o_ref.dtype)
    for key in heads:
        bb, h = key
        b_end = b_col[key][L - 1:L, :]
        g_col = b_end - b_col[key] + i_col[key]
        m_new = jnp.maximum(b_end + m_prev[key], jnp.max(g_col, axis=0, keepdims=True))
        decay = jnp.exp(b_end + m_prev[key] - m_new)
        kw = k[key] * jnp.exp(g_col - m_new)
        cn_ref[bb, h] = decay * cn_ref[bb, h] + _dot(kw.T.astype(BF16), v_aug[key])
        m_ref[bb, h:h + 1, :] = jnp.broadcast_to(m_new, (1, LANES))


def _mlstm(qk_raw, vc, o_pre, misc, conv_w, b_i, b_f, head_norm):
    B, S, _ = qk_raw.shape
    L = MLSTM_CHUNK
    nb = MLSTM_BATCH
    gb = jnp.zeros((1, LANES), F32).at[0, MISC_I:MISC_I + MLSTM_HEADS].set(b_i)
    gb = gb.at[0, MISC_F:MISC_F + MLSTM_HEADS].set(b_f)
    row = lambda b, c: (b, c, 0)
    const = lambda b, c: (0, 0)
    return pl.pallas_call(
        _mlstm_kernel,
        grid=(B // nb, S // L),
        in_specs=[
            pl.BlockSpec((nb, L, 2 * MIX_WIDTH), row),
            pl.BlockSpec((nb, L, MIX_WIDTH), row),
            pl.BlockSpec((nb, L, MIX_WIDTH), row),
            pl.BlockSpec((nb, L, LANES), row),
            pl.BlockSpec((CONV_WIDTH, 2 * MIX_WIDTH), const),
            pl.BlockSpec((1, LANES), const),
            pl.BlockSpec((1, MIX_WIDTH), const),
        ],
        out_specs=pl.BlockSpec((nb, L, MIX_WIDTH), row),
        out_shape=jax.ShapeDtypeStruct((B, S, MIX_WIDTH), BF16),
        scratch_shapes=[
            pltpu.VMEM((nb, L, 2 * MIX_WIDTH), F32),
            pltpu.VMEM((nb, MLSTM_HEADS, MLSTM_HEAD_DIM, 2 * LANES), F32),
            pltpu.VMEM((nb, 8, LANES), F32),
        ],
        compiler_params=_params(("parallel", "arbitrary")),
        name="mlstm",
    )(qk_raw, vc, o_pre, misc, conv_w, gb, head_norm.reshape(1, MIX_WIDTH))


def _even_layer(h, norm_mix, w_in, pool_w, pool_scale, w_out, norm_ffn, wg, wu, wd):
    B, S, D = h.shape
    slopes = (2.0 ** (-8.0 * np.arange(1, MOBA_HEADS + 1) / MOBA_HEADS)).astype(np.float32)
    ka, ub, kmean, qT, vT = _even_in(h, norm_mix, w_in, slopes)
    nb = S // MOBA_BLOCK
    kmean = kmean.reshape(B, nb, MOBA_HEADS, LANES)[..., :MOBA_HEAD_DIM].transpose(0, 2, 1, 3)
    qaT = _moba_gate(kmean, qT)
    aT = _attention(qaT, ka, vT, MOBA_HEAD_DIM, MOBA_HEADS, "moba_attn")
    b_out = _pool(ub, pool_w, pool_scale)
    h = _mix_out(h, aT, b_out, w_out, a_first=True)
    return _ffn(h.reshape(B * S, D), norm_ffn, wg.astype(BF16), wu.astype(BF16), wd.astype(BF16))


def _odd_layer(h, norm_mix, w_in, conv_w, b_i, b_f, head_norm, q_norm, w_uq, kv_norm, w_ukv,
               w_out, norm_ffn, router_w, router_b, wg, wu, wd):
    B, S, D = h.shape
    qk_raw, vc, o_pre, misc, mqT, mk, mvT = _odd_in(h, norm_mix, w_in, q_norm, w_uq, kv_norm, w_ukv)
    c_out = _mlstm(qk_raw, vc, o_pre, misc, conv_w, b_i, b_f, head_norm)
    dT = _attention(mqT, mk, mvT, MLA_V_DIM, MLA_HEADS, "mla_attn")
    h = _mix_out(h, dT, c_out, w_out, a_first=False)
    return _moe(h.reshape(B * S, D), norm_ffn, router_w, router_b,
                wg.astype(BF16), wu.astype(BF16), wd.astype(BF16))


def kernel(x, p, ev_norm_mix, ev_w_in, pool_w, pool_scale, ev_w_out, ev_norm_ffn, ffn_w_gate, ffn_w_up, ffn_w_down, od_norm_mix, od_w_in, conv_w, gate_b_i, gate_b_f, mlstm_norm, mla_q_norm, mla_w_uq, mla_kv_norm, mla_w_ukv, od_w_out, od_norm_ffn, router_w, router_b, moe_w_gate, moe_w_up, moe_w_down, ple_norm, ple_w_gate, ple_w_proj, final_norm):
    B, S, D = x.shape
    depth = p.shape[0]
    assert D == D_MODEL and S % (2 * ROW_TILE) == 0 and B % MLSTM_BATCH == 0
    assert MOBA_TOPK <= S // MOBA_BLOCK <= MOBA_MAX_BLOCKS
    h = x
    for layer in range(depth):
        j = layer // 2
        if layer % 2 == 0:
            h2d = _even_layer(h, ev_norm_mix[j], ev_w_in[j], pool_w[j], pool_scale[j], ev_w_out[j],
                              ev_norm_ffn[j], ffn_w_gate[j], ffn_w_up[j], ffn_w_down[j])
        else:
            h2d = _odd_layer(h, od_norm_mix[j], od_w_in[j], conv_w[j], gate_b_i[j], gate_b_f[j],
                             mlstm_norm[j], mla_q_norm[j], mla_w_uq[j], mla_kv_norm[j], mla_w_ukv[j],
                             od_w_out[j], od_norm_ffn[j], router_w[j], router_b[j],
                             moe_w_gate[j], moe_w_up[j], moe_w_down[j])
        last = layer == depth - 1
        h2d = _ple(h2d, p[layer].reshape(B * S, PLE_DIM), ple_norm[layer], ple_w_gate[layer],
                   ple_w_proj[layer], final_g=final_norm if last else None)
        h = h2d.reshape(B, S, D)
    return h
```

```python
import functools
import math

import numpy as np
import jax
import jax.numpy as jnp
from jax import lax
from jax.experimental import pallas as pl
from jax.experimental.pallas import tpu as pltpu

F32 = jnp.float32
BF16 = jnp.bfloat16
HIGHEST = lax.Precision.HIGHEST

D_MODEL = 1024
PLE_DIM = 256
NORM_EPS = 1e-6
NEG_INF = -1e30

MOBA_HEADS = 8
MOBA_HEAD_DIM = 64
MOBA_BLOCK = 256
MOBA_TOPK = 3
POOL_WINDOWS = (2, 4, 8, 16)
POOL_GROUP_DIM = 128
POOL_HALO = 16
MLSTM_HEADS = 4
MLSTM_HEAD_DIM = 128
MLSTM_CHUNK = 128
MLSTM_BATCH = 2
CONV_WIDTH = 4
MLA_HEADS = 4
MLA_Q_RANK = 256
MLA_KV_RANK = 128
MLA_NOPE_DIM = 64
MLA_ROPE_DIM = 32
MLA_V_DIM = 128
ROPE_BASE = 10000.0
FFN_DIM = 2816
N_EXPERTS = 8
EXPERT_DIM = 3584
MIX_WIDTH = 512

ATTN_BLOCK = 512
ROW_TILE = 512
LANES = 128
VMEM_LIMIT = 56 * 1024 * 1024

MISC_ROPE = 0
MISC_I = 32
MISC_F = 36


def _params(sem, vmem=VMEM_LIMIT):
    return pltpu.CompilerParams(dimension_semantics=sem, vmem_limit_bytes=vmem)


def _rms(x, g):
    ms = jnp.mean(x * x, axis=-1, keepdims=True)
    return x * lax.rsqrt(ms + NORM_EPS) * g


def _sigmoid(x):
    return 1.0 / (1.0 + jnp.exp(-x))


def _dot(a, b):
    return jnp.dot(a, b, preferred_element_type=F32)


def _dot_nt(a, b, precision=None):
    return lax.dot_general(a, b, (((1,), (1,)), ((), ())), precision=precision,
                           preferred_element_type=F32)


KAUG_SEL = MOBA_HEAD_DIM
KAUG_POS = KAUG_SEL + 32
MOBA_MAX_BLOCKS = KAUG_POS - KAUG_SEL


def _bf16_terms(x, n):
    out = []
    for _ in range(n):
        bits = np.float32(x).view(np.uint32)
        kept = np.uint32((int(bits) + 0x7FFF + ((int(bits) >> 16) & 1)) & 0xFFFF0000)
        term = float(kept.view(np.float32))
        out.append(term)
        x -= term
    return tuple(out)


LOG2E = math.log2(math.e)
LOG2E_TERMS = _bf16_terms(LOG2E, 3)


def _even_in_kernel(x_ref, g_ref, wn_ref, wqT_ref, wvT_ref, ext_ref,
                    ka_ref, ub_ref, km_ref, qT_ref, vT_ref):
    tm = x_ref.shape[1]
    xn = _rms(x_ref[0], g_ref[...]).astype(BF16)
    n = _dot(xn, wn_ref[...])
    ka = n[:, :MOBA_HEADS * LANES]
    ka_ref[0] = (ka + ext_ref[...].astype(F32)).astype(BF16)
    ub_ref[0] = n[:, MOBA_HEADS * LANES:]
    for j in range(tm // MOBA_BLOCK):
        km_ref[0, j] = jnp.mean(ka[j * MOBA_BLOCK:(j + 1) * MOBA_BLOCK], axis=0, keepdims=True)
    qT_ref[0] = _dot_nt(wqT_ref[...], xn)
    vT = _dot_nt(wvT_ref[...], xn)
    for j in range(tm // ATTN_BLOCK):
        vT_ref[0, j] = vT[:, j * ATTN_BLOCK:(j + 1) * ATTN_BLOCK].astype(BF16)


def _moba_key_extras(S, slopes):
    pos = np.arange(S)
    blk, off = pos // MOBA_BLOCK, pos % MOBA_BLOCK
    ext = np.zeros((S, MOBA_HEADS, LANES), np.float32)
    ext[pos, :, KAUG_SEL + blk] = 1.0
    for term in range(len(LOG2E_TERMS)):
        ext[:, :, KAUG_POS + 2 * term] = slopes[None, :] * (MOBA_BLOCK * blk)[:, None]
        ext[:, :, KAUG_POS + 2 * term + 1] = slopes[None, :] * off[:, None]
    return jnp.asarray(ext.reshape(S, MOBA_HEADS * LANES), dtype=BF16)


def _even_in(x, g, w_in, slopes):
    B, S, D = x.shape
    tm = ROW_TILE
    nb = S // MOBA_BLOCK
    wq, wk, wv, wu = (w_in[:, i * MIX_WIDTH:(i + 1) * MIX_WIDTH] for i in range(4))
    wk_slots = jnp.concatenate(
        [wk.reshape(D, MOBA_HEADS, MOBA_HEAD_DIM),
         jnp.zeros((D, MOBA_HEADS, LANES - MOBA_HEAD_DIM), F32)], axis=2).reshape(D, MOBA_HEADS * LANES)
    wn = jnp.concatenate([wk_slots, wu], axis=1).astype(BF16)
    wqT = wq.T.astype(BF16)
    wvT = wv.T.astype(BF16)
    const = lambda b, i: (0, 0)
    return pl.pallas_call(
        _even_in_kernel,
        grid=(B, S // tm),
        in_specs=[
            pl.BlockSpec((1, tm, D), lambda b, i: (b, i, 0)),
            pl.BlockSpec((1, D), const),
            pl.BlockSpec((D, MOBA_HEADS * LANES + MIX_WIDTH), const),
            pl.BlockSpec((MIX_WIDTH, D), const),
            pl.BlockSpec((MIX_WIDTH, D), const),
            pl.BlockSpec((tm, MOBA_HEADS * LANES), lambda b, i: (i, 0)),
        ],
        out_specs=[
            pl.BlockSpec((1, tm, MOBA_HEADS * LANES), lambda b, i: (b, i, 0)),
            pl.BlockSpec((1, tm, MIX_WIDTH), lambda b, i: (b, i, 0)),
            pl.BlockSpec((1, tm // MOBA_BLOCK, 1, MOBA_HEADS * LANES), lambda b, i: (b, i, 0, 0)),
            pl.BlockSpec((1, MIX_WIDTH, tm), lambda b, i: (b, 0, i)),
            pl.BlockSpec((1, tm // ATTN_BLOCK, MIX_WIDTH, ATTN_BLOCK), lambda b, i: (b, i, 0, 0)),
        ],
        out_shape=[
            jax.ShapeDtypeStruct((B, S, MOBA_HEADS * LANES), BF16),
            jax.ShapeDtypeStruct((B, S, MIX_WIDTH), F32),
            jax.ShapeDtypeStruct((B, nb, 1, MOBA_HEADS * LANES), F32),
            jax.ShapeDtypeStruct((B, MIX_WIDTH, S), F32),
            jax.ShapeDtypeStruct((B, S // ATTN_BLOCK, MIX_WIDTH, ATTN_BLOCK), BF16),
        ],
        compiler_params=_params(("parallel", "parallel")),
        name="even_in",
    )(x, g.reshape(1, D), wn, wqT, wvT, _moba_key_extras(S, slopes))


def _moba_gate_kernel(km_ref, qT_ref, qa_ref):
    i = pl.program_id(1)
    nb = km_ref.shape[2]
    tq = qT_ref.shape[2]
    row = lax.broadcasted_iota(jnp.int32, (nb, tq), 0)
    own = (i * tq + lax.broadcasted_iota(jnp.int32, (nb, tq), 1)) // MOBA_BLOCK
    past = row < own
    tail_row = lax.broadcasted_iota(jnp.int32, (LANES - KAUG_POS, tq), 0)
    tail = jnp.zeros(tail_row.shape, F32)
    for term, value in enumerate(LOG2E_TERMS):
        tail = jnp.where(tail_row // 2 == term, F32(value), tail)
    tail = tail.astype(BF16)
    pad = jnp.zeros((MOBA_MAX_BLOCKS - nb, tq), BF16) if nb < MOBA_MAX_BLOCKS else None
    for h in range(MOBA_HEADS):
        q_h = qT_ref[0, h * MOBA_HEAD_DIM:(h + 1) * MOBA_HEAD_DIM, :]
        gate = jnp.dot(km_ref[0, h], q_h, precision=HIGHEST, preferred_element_type=F32)
        gate = jnp.where(past, gate, NEG_INF)
        chosen = jnp.zeros(gate.shape, F32)
        for _ in range(MOBA_TOPK):
            mx = jnp.max(gate, axis=0, keepdims=True)
            first = jnp.min(jnp.where(gate == mx, row, nb), axis=0, keepdims=True)
            pick = row == first
            chosen = jnp.where(pick, 1.0, chosen)
            gate = jnp.where(pick, -jnp.inf, gate)
        keep = jnp.where(past, chosen, (row == own).astype(F32))
        sel = jnp.where(keep > 0.0, 0.0, NEG_INF).astype(BF16)
        base = h * LANES
        qa_ref[0, base:base + KAUG_SEL] = (q_h * (MOBA_HEAD_DIM ** -0.5 * LOG2E)).astype(BF16)
        qa_ref[0, base + KAUG_SEL:base + KAUG_SEL + nb] = sel
        if pad is not None:
            qa_ref[0, base + KAUG_SEL + nb:base + KAUG_POS] = pad
        qa_ref[0, base + KAUG_POS:base + LANES] = tail


def _moba_gate(kmean, qT):
    B, H, nb, dh = kmean.shape
    S = qT.shape[2]
    tq = ATTN_BLOCK
    return pl.pallas_call(
        _moba_gate_kernel,
        grid=(B, S // tq),
        in_specs=[
            pl.BlockSpec((1, H, nb, dh), lambda b, i: (b, 0, 0, 0)),
            pl.BlockSpec((1, H * dh, tq), lambda b, i: (b, 0, i)),
        ],
        out_specs=pl.BlockSpec((1, H * LANES, tq), lambda b, i: (b, 0, i)),
        out_shape=jax.ShapeDtypeStruct((B, H * LANES, S), BF16),
        compiler_params=_params(("parallel", "parallel")),
        name="moba_gate",
    )(kmean, qT)


HEADS_PER_STEP = 4
SUM_ROWS = 16


def _attn_kernel(q_ref, k_ref, v_ref, o_ref, sa_ref, sb_ref, m_ref, acc_ref):
    i = pl.program_id(2)
    tq = q_ref.shape[2]
    tk = ATTN_BLOCK
    hp = HEADS_PER_STEP
    dv = v_ref.shape[2] // hp
    m_ref[...] = jnp.full(m_ref.shape, NEG_INF, F32)
    acc_ref[...] = jnp.zeros(acc_ref.shape, F32)
    ones_rows = jnp.ones((SUM_ROWS, tk), BF16)

    def scores(kvt, s_ref, diag):
        start = pl.multiple_of(kvt * tk, tk)
        k_tile = k_ref[0, pl.ds(start, tk), :]
        for g in range(hp):
            s = _dot(k_tile[:, g * LANES:(g + 1) * LANES], q_ref[0, g * LANES:(g + 1) * LANES, :])
            if diag:
                key = lax.broadcasted_iota(jnp.int32, (tk, tq), 0)
                qry = lax.broadcasted_iota(jnp.int32, (tk, tq), 1)
                s = jnp.where(key <= qry, s, NEG_INF)
            s_ref[g] = s

    def consume(kvt, s_ref):
        v_tile = v_ref[0, kvt]
        for g in range(hp):
            s = s_ref[g]
            m_run = m_ref[g]
            m_new = jnp.maximum(m_run, jnp.max(s, axis=0, keepdims=True))
            p = jnp.exp2(s - m_new).astype(BF16)
            v_aug = jnp.concatenate([v_tile[g * dv:(g + 1) * dv], ones_rows], axis=0)
            acc_ref[g] = jnp.exp2(m_run - m_new) * acc_ref[g] + _dot(v_aug, p)
            m_ref[g] = m_new

    tile_at = lambda t: jnp.where(t == 0, i, t - 1)
    scores(i, sa_ref, True)

    def pair(p, carry):
        t = 2 * p
        scores(tile_at(t + 1), sb_ref, False)
        consume(tile_at(t), sa_ref)
        scores(tile_at(t + 2), sa_ref, False)
        consume(tile_at(t + 1), sb_ref)
        return carry

    lax.fori_loop(0, i // 2, pair, 0)
    last = 2 * (i // 2)

    @pl.when(i % 2 == 1)
    def _():
        scores(tile_at(last + 1), sb_ref, False)
        consume(tile_at(last), sa_ref)
        consume(tile_at(last + 1), sb_ref)

    @pl.when(i % 2 == 0)
    def _():
        consume(tile_at(last), sa_ref)

    for g in range(hp):
        o_ref[0, g * dv:(g + 1) * dv, :] = acc_ref[g, :dv] / acc_ref[g, dv:dv + 1]


def _attention(qT, k, vT, dv, heads, name):
    B, _, S = qT.shape
    tq = ATTN_BLOCK
    nt = S // ATTN_BLOCK
    hp = HEADS_PER_STEP
    return pl.pallas_call(
        _attn_kernel,
        grid=(B, heads // hp, S // tq),
        in_specs=[
            pl.BlockSpec((1, hp * LANES, tq), lambda b, h, i: (b, h, i)),
            pl.BlockSpec((1, S, hp * LANES), lambda b, h, i: (b, 0, h)),
            pl.BlockSpec((1, nt, hp * dv, ATTN_BLOCK), lambda b, h, i: (b, 0, h, 0)),
        ],
        out_specs=pl.BlockSpec((1, hp * dv, tq), lambda b, h, i: (b, h, i)),
        out_shape=jax.ShapeDtypeStruct((B, heads * dv, S), F32),
        scratch_shapes=[pltpu.VMEM((hp, ATTN_BLOCK, tq), F32), pltpu.VMEM((hp, ATTN_BLOCK, tq), F32),
                        pltpu.VMEM((hp, 1, tq), F32), pltpu.VMEM((hp, dv + SUM_ROWS, tq), F32)],
        compiler_params=_params(("parallel", "parallel", "arbitrary")),
        name=name,
    )(qT, k, vT)


def _pool_kernel(x_ref, halo_ref, w_ref, sc_ref, o_ref, xs_ref):
    i = pl.program_id(1)
    tm = x_ref.shape[1]
    x = x_ref[0]
    xs_ref[0:POOL_HALO] = jnp.where(i > 0, halo_ref[0], 0.0)
    xs_ref[POOL_HALO:POOL_HALO + tm] = x
    t = i * tm + lax.broadcasted_iota(jnp.int32, (tm, 1), 0)
    outs = []
    for g, win in enumerate(POOL_WINDOWS):
        lo, hi = g * POOL_GROUP_DIM, (g + 1) * POOL_GROUP_DIM
        xg = x[:, lo:hi]
        acc = xg
        for d in range(1, win):
            acc = acc + xs_ref[POOL_HALO - d:POOL_HALO - d + tm, lo:hi]
        count = jnp.minimum(t + 1, win).astype(F32)
        outs.append(_dot((acc / count - xg).astype(BF16), w_ref[g]))
    o_ref[0] = (jnp.concatenate(outs, axis=1) * sc_ref[...]).astype(o_ref.dtype)


def _pool(ub, pool_w, pool_scale):
    B, S, W = ub.shape
    tm = ROW_TILE
    per = tm // POOL_HALO
    return pl.pallas_call(
        _pool_kernel,
        grid=(B, S // tm),
        in_specs=[
            pl.BlockSpec((1, tm, W), lambda b, i: (b, i, 0)),
            pl.BlockSpec((1, POOL_HALO, W), lambda b, i: (b, jnp.maximum(i * per - 1, 0), 0)),
            pl.BlockSpec(pool_w.shape, lambda b, i: (0, 0, 0)),
            pl.BlockSpec((1, W), lambda b, i: (0, 0)),
        ],
        out_specs=pl.BlockSpec((1, tm, W), lambda b, i: (b, i, 0)),
        out_shape=jax.ShapeDtypeStruct((B, S, W), BF16),
        scratch_shapes=[pltpu.VMEM((POOL_HALO + tm, W), F32)],
        compiler_params=_params(("parallel", "parallel")),
        name="pool",
    )(ub, ub, pool_w.astype(BF16), pool_scale.reshape(1, W))


def _mix_out_kernel(h_ref, aT_ref, b_ref, w_ref, o_ref, *, a_first):
    a = aT_ref[0].T.astype(BF16)
    b = b_ref[0].astype(BF16)
    lo, hi = (a, b) if a_first else (b, a)
    y = _dot(lo, w_ref[:MIX_WIDTH]) + _dot(hi, w_ref[MIX_WIDTH:])
    o_ref[0] = h_ref[0] + y


def _mix_out(h, aT, b, w_out, a_first):
    B, S, D = h.shape
    tm = ROW_TILE
    return pl.pallas_call(
        functools.partial(_mix_out_kernel, a_first=a_first),
        grid=(B, S // tm),
        in_specs=[
            pl.BlockSpec((1, tm, D), lambda b_, i: (b_, i, 0)),
            pl.BlockSpec((1, MIX_WIDTH, tm), lambda b_, i: (b_, 0, i)),
            pl.BlockSpec((1, tm, MIX_WIDTH), lambda b_, i: (b_, i, 0)),
            pl.BlockSpec((2 * MIX_WIDTH, D), lambda b_, i: (0, 0)),
        ],
        out_specs=pl.BlockSpec((1, tm, D), lambda b_, i: (b_, i, 0)),
        out_shape=jax.ShapeDtypeStruct((B, S, D), F32),
        compiler_params=_params(("parallel", "parallel")),
        name="mix_out",
    )(h, aT, b, w_out.astype(BF16))


def _swiglu_step(xn, wg, wu, wd):
    gt = _dot(xn, wg)
    up = _dot(xn, wu)
    return _dot((gt * _sigmoid(gt) * up).astype(BF16), wd)


def _ffn_kernel(h_ref, g_ref, wg_ref, wu_ref, wd_ref, o_ref, xn_ref, acc_ref):
    f = pl.program_id(1)

    @pl.when(f == 0)
    def _():
        xn_ref[...] = _rms(h_ref[...], g_ref[...]).astype(BF16)
        acc_ref[...] = jnp.zeros_like(acc_ref)

    acc_ref[...] += _swiglu_step(xn_ref[...], wg_ref[...], wu_ref[...], wd_ref[...])

    @pl.when(f == pl.num_programs(1) - 1)
    def _():
        o_ref[...] = h_ref[...] + acc_ref[...]


def _ffn(h2d, g, wg, wu, wd, tm=ROW_TILE, tf=FFN_DIM // 2):
    T, D = h2d.shape
    F = wg.shape[1]
    return pl.pallas_call(
        _ffn_kernel,
        grid=(T // tm, F // tf),
        in_specs=[
            pl.BlockSpec((tm, D), lambda i, f: (i, 0)),
            pl.BlockSpec((1, D), lambda i, f: (0, 0)),
            pl.BlockSpec((D, tf), lambda i, f: (0, f)),
            pl.BlockSpec((D, tf), lambda i, f: (0, f)),
            pl.BlockSpec((tf, D), lambda i, f: (f, 0)),
        ],
        out_specs=pl.BlockSpec((tm, D), lambda i, f: (i, 0)),
        out_shape=jax.ShapeDtypeStruct((T, D), F32),
        scratch_shapes=[pltpu.VMEM((tm, D), BF16), pltpu.VMEM((tm, D), F32)],
        compiler_params=_params(("parallel", "arbitrary")),
        name="ffn",
    )(h2d, g.reshape(1, D), wg, wu, wd)


ROUTE_E0, ROUTE_E1, ROUTE_W0, ROUTE_W1, ROUTE_R0, ROUTE_R1 = range(6)
MOE_TILE = 512
MOE_TF = EXPERT_DIM // 2


def _lane_pick(tile, lane, idx):
    return jnp.sum(jnp.where(lane == idx, tile, 0.0), axis=1, keepdims=True)


def _router_kernel(h_ref, g_ref, w_ref, b_ref, route_ref, cnt_ref):
    tm = h_ref.shape[0]

    @pl.when(pl.program_id(0) == 0)
    def _():
        cnt_ref[...] = jnp.zeros_like(cnt_ref)

    xn = _rms(h_ref[...], g_ref[...])
    logits = jnp.dot(xn, w_ref[...], precision=HIGHEST, preferred_element_type=F32) + b_ref[...]
    lane = lax.broadcasted_iota(jnp.int32, logits.shape, 1)
    logits = jnp.where(lane < N_EXPERTS, logits, -jnp.inf)
    v0 = jnp.max(logits, axis=1, keepdims=True)
    i0 = jnp.min(jnp.where(logits == v0, lane, LANES), axis=1, keepdims=True)
    rest = jnp.where(lane == i0, -jnp.inf, logits)
    v1 = jnp.max(rest, axis=1, keepdims=True)
    i1 = jnp.min(jnp.where(rest == v1, lane, LANES), axis=1, keepdims=True)
    e1 = jnp.exp(v1 - v0)
    w0 = 1.0 / (1.0 + e1)
    sel = (lane == i0).astype(F32) + (lane == i1).astype(F32)
    earlier = (lax.broadcasted_iota(jnp.int32, (tm, tm), 1)
               < lax.broadcasted_iota(jnp.int32, (tm, tm), 0))
    rank = _dot(earlier.astype(BF16), sel.astype(BF16)) + cnt_ref[...]
    cnt_ref[...] += jnp.sum(sel, axis=0, keepdims=True)
    cols = ((ROUTE_E0, i0.astype(F32)), (ROUTE_E1, i1.astype(F32)), (ROUTE_W0, w0), (ROUTE_W1, e1 * w0),
            (ROUTE_R0, _lane_pick(rank, lane, i0)), (ROUTE_R1, _lane_pick(rank, lane, i1)))
    route = jnp.zeros(logits.shape, F32)
    for c, val in cols:
        route = jnp.where(lane == c, val, route)
    route_ref[...] = route


def _router(h2d, g, router_w, router_b):
    T, D = h2d.shape
    tm = ROW_TILE
    w = jnp.zeros((D, LANES), F32).at[:, :N_EXPERTS].set(router_w)
    b = jnp.zeros((1, LANES), F32).at[0, :N_EXPERTS].set(router_b)
    return pl.pallas_call(
        _router_kernel,
        grid=(T // tm,),
        in_specs=[
            pl.BlockSpec((tm, D), lambda i: (i, 0)),
            pl.BlockSpec((1, D), lambda i: (0, 0)),
            pl.BlockSpec((D, LANES), lambda i: (0, 0)),
            pl.BlockSpec((1, LANES), lambda i: (0, 0)),
        ],
        out_specs=[pl.BlockSpec((tm, LANES), lambda i: (i, 0)),
                   pl.BlockSpec((1, LANES), lambda i: (0, 0))],
        out_shape=[jax.ShapeDtypeStruct((T, LANES), F32), jax.ShapeDtypeStruct((1, LANES), F32)],
        compiler_params=_params(("arbitrary",)),
        name="router",
    )(h2d, g.reshape(1, D), w, b)


def _row_copies(pos_ref, base, r, src_of, dst_of, sem):
    return [pltpu.make_async_copy(src_of(k, pos_ref[base + 2 * r + k]),
                                  dst_of(k, pos_ref[base + 2 * r + k]), sem) for k in range(2)]


def _all_rows(tm, make):
    def issue(r, c):
        for k, cp in enumerate(make(r)):
            cp.start(priority=k)
        return c

    def drain(r, c):
        for cp in make(r):
            cp.wait()
        return c

    lax.fori_loop(0, tm, issue, 0, unroll=8)
    lax.fori_loop(0, tm, drain, 0, unroll=8)


def _dispatch_kernel(pos_ref, h_ref, g_ref, init_ref, xs_ref, xn_ref, sem):
    del init_ref
    tm = h_ref.shape[0]
    xn_ref[...] = _rms(h_ref[...], g_ref[...])
    base = pl.program_id(0) * (2 * tm)
    _all_rows(tm, lambda r: _row_copies(
        pos_ref, base, r, lambda k, p: xn_ref.at[pl.ds(r, 1)], lambda k, p: xs_ref.at[pl.ds(p, 1)], sem))


def _dispatch(pos, h2d, g, n_rows):
    T, D = h2d.shape
    tm = ROW_TILE
    return pl.pallas_call(
        _dispatch_kernel,
        grid_spec=pltpu.PrefetchScalarGridSpec(
            num_scalar_prefetch=1,
            grid=(T // tm,),
            in_specs=[pl.BlockSpec((tm, D), lambda i, pos: (i, 0)),
                      pl.BlockSpec((1, D), lambda i, pos: (0, 0)),
                      pl.BlockSpec(memory_space=pl.ANY)],
            out_specs=pl.BlockSpec(memory_space=pl.ANY),
            scratch_shapes=[pltpu.VMEM((tm, D), F32), pltpu.SemaphoreType.DMA(())],
        ),
        out_shape=jax.ShapeDtypeStruct((n_rows, D), F32),
        input_output_aliases={3: 0},
        compiler_params=_params(("arbitrary",)),
        name="moe_dispatch",
    )(pos, h2d, g.reshape(1, D), jnp.zeros((n_rows, D), F32))


def _moe_ffn_kernel(te_ref, nv_ref, x_ref, wg_ref, wu_ref, wd_ref, o_ref, xb_ref, acc_ref):
    j = pl.program_id(0)
    f = pl.program_id(1)
    valid = j < nv_ref[0]

    @pl.when(f == 0)
    def _():
        xb_ref[...] = x_ref[...].astype(BF16)
        acc_ref[...] = jnp.zeros_like(acc_ref)

    @pl.when(valid)
    def _():
        acc_ref[...] += _swiglu_step(xb_ref[...], wg_ref[0], wu_ref[0], wd_ref[0])

    @pl.when(f == pl.num_programs(1) - 1)
    def _():
        o_ref[...] = acc_ref[...]


def _moe_ffn(tile_expert, n_valid, xs, wg, wu, wd):
    N, D = xs.shape
    tm, tf = MOE_TILE, MOE_TF
    F = wg.shape[2]
    return pl.pallas_call(
        _moe_ffn_kernel,
        grid_spec=pltpu.PrefetchScalarGridSpec(
            num_scalar_prefetch=2,
            grid=(N // tm, F // tf),
            in_specs=[pl.BlockSpec((tm, D), lambda j, f, te, nv: (j, 0)),
                      pl.BlockSpec((1, D, tf), lambda j, f, te, nv: (te[j], 0, f)),
                      pl.BlockSpec((1, D, tf), lambda j, f, te, nv: (te[j], 0, f)),
                      pl.BlockSpec((1, tf, D), lambda j, f, te, nv: (te[j], f, 0))],
            out_specs=pl.BlockSpec((tm, D), lambda j, f, te, nv: (j, 0)),
            scratch_shapes=[pltpu.VMEM((tm, D), BF16), pltpu.VMEM((tm, D), F32)],
        ),
        out_shape=jax.ShapeDtypeStruct((N, D), F32),
        compiler_params=_params(("arbitrary", "arbitrary")),
        name="moe_ffn",
    )(tile_expert, n_valid, xs, wg, wu, wd)


def _combine_kernel(pos_ref, h_ref, route_ref, ys_ref, o_ref, y_ref, sem):
    tm = h_ref.shape[0]
    base = pl.program_id(0) * (2 * tm)
    _all_rows(tm, lambda r: _row_copies(
        pos_ref, base, r, lambda k, p: ys_ref.at[pl.ds(p, 1)], lambda k, p: y_ref.at[k, pl.ds(r, 1)], sem))
    route = route_ref[...]
    lane = lax.broadcasted_iota(jnp.int32, route.shape, 1)
    o_ref[...] = (h_ref[...] + _lane_pick(route, lane, ROUTE_W0) * y_ref[0]
                  + _lane_pick(route, lane, ROUTE_W1) * y_ref[1])


def _combine(pos, h2d, route, ys):
    T, D = h2d.shape
    tm = ROW_TILE
    return pl.pallas_call(
        _combine_kernel,
        grid_spec=pltpu.PrefetchScalarGridSpec(
            num_scalar_prefetch=1,
            grid=(T // tm,),
            in_specs=[pl.BlockSpec((tm, D), lambda i, pos: (i, 0)),
                      pl.BlockSpec((tm, LANES), lambda i, pos: (i, 0)),
                      pl.BlockSpec(memory_space=pl.ANY)],
            out_specs=pl.BlockSpec((tm, D), lambda i, pos: (i, 0)),
            scratch_shapes=[pltpu.VMEM((2, tm, D), F32), pltpu.SemaphoreType.DMA(())],
        ),
        out_shape=jax.ShapeDtypeStruct((T, D), F32),
        compiler_params=_params(("arbitrary",)),
        name="moe_combine",
    )(pos, h2d, route, ys)


def _moe(h2d, g, router_w, router_b, wg, wu, wd):
    T, D = h2d.shape
    tm = MOE_TILE
    route, counts = _router(h2d, g, router_w, router_b)
    cnt = counts[0, :N_EXPERTS].astype(jnp.int32)
    padded = (cnt + tm - 1) // tm * tm
    ends = jnp.cumsum(padded)
    start = ends - padded
    e01 = route[:, ROUTE_E0:ROUTE_E1 + 1].astype(jnp.int32)
    r01 = route[:, ROUTE_R0:ROUTE_R1 + 1].astype(jnp.int32)
    pos = (start[e01] + r01).reshape(2 * T)
    n_rows = 2 * T + N_EXPERTS * tm
    tile_row = jnp.arange(n_rows // tm, dtype=jnp.int32) * tm
    tile_expert = jnp.minimum(jnp.sum(tile_row[:, None] >= ends[None, :], axis=1), N_EXPERTS - 1).astype(jnp.int32)
    n_valid = (ends[-1:] // tm).astype(jnp.int32)
    xs = _dispatch(pos, h2d, g, n_rows)
    ys = _moe_ffn(tile_expert, n_valid, xs, wg, wu, wd)
    return _combine(pos, h2d, route, ys)


def _ple_kernel(*refs, final):
    if final:
        h_ref, p_ref, g_ref, wg_ref, wp_ref, fg_ref, o_ref = refs
    else:
        h_ref, p_ref, g_ref, wg_ref, wp_ref, o_ref = refs
    h = h_ref[...]
    gate = _sigmoid(_dot(_rms(h, g_ref[...]).astype(BF16), wg_ref[...]))
    out = h + gate * _dot(p_ref[...].astype(BF16), wp_ref[...])
    if final:
        out = _rms(out, fg_ref[...])
    o_ref[...] = out


def _ple(h2d, p2d, g, w_gate, w_proj, final_g=None):
    T, D = h2d.shape
    tm = ROW_TILE
    final = final_g is not None
    in_specs = [
        pl.BlockSpec((tm, D), lambda i: (i, 0)),
        pl.BlockSpec((tm, PLE_DIM), lambda i: (i, 0)),
        pl.BlockSpec((1, D), lambda i: (0, 0)),
        pl.BlockSpec((D, D), lambda i: (0, 0)),
        pl.BlockSpec((PLE_DIM, D), lambda i: (0, 0)),
    ]
    args = [h2d, p2d, g.reshape(1, D), w_gate.astype(BF16), w_proj.astype(BF16)]
    if final:
        in_specs.append(pl.BlockSpec((1, D), lambda i: (0, 0)))
        args.append(final_g.reshape(1, D))
    return pl.pallas_call(
        functools.partial(_ple_kernel, final=final),
        grid=(T // tm,),
        in_specs=in_specs,
        out_specs=pl.BlockSpec((tm, D), lambda i: (i, 0)),
        out_shape=jax.ShapeDtypeStruct((T, D), F32),
        compiler_params=_params(("parallel",)),
        name="ple_final" if final else "ple",
    )(*args)


ODD_MAIN = 4 * MIX_WIDTH
ODD_COLS = ODD_MAIN + MLA_Q_RANK + MLA_KV_RANK + 2 * LANES
MLA_QK_SCALE = (MLA_NOPE_DIM + MLA_ROPE_DIM) ** -0.5 * LOG2E
ROPE_HALF = MLA_ROPE_DIM // 2


def _odd_in_kernel(x_ref, g_ref, wn_ref, qn_ref, wuqT_ref, kvn_ref, wk2_ref, wvT_ref,
                   cosT_ref, sinT_ref, cc_ref, ss_ref,
                   qk_ref, vc_ref, op_ref, misc_ref, mq_ref, mk_ref, mv_ref):
    tm = x_ref.shape[1]
    xn = _rms(x_ref[0], g_ref[...]).astype(BF16)
    u = _dot(xn, wn_ref[...])
    qk_ref[0] = u[:, :2 * MIX_WIDTH]
    vc_ref[0] = u[:, 2 * MIX_WIDTH:3 * MIX_WIDTH].astype(BF16)
    op_ref[0] = u[:, 3 * MIX_WIDTH:ODD_MAIN]
    c0 = ODD_MAIN
    c_q = u[:, c0:c0 + MLA_Q_RANK]
    c0 += MLA_Q_RANK
    c_kv = u[:, c0:c0 + MLA_KV_RANK]
    c0 += MLA_KV_RANK
    misc = u[:, c0:c0 + LANES]
    misc_sw = u[:, c0 + LANES:c0 + 2 * LANES]
    misc_ref[0] = misc
    cqn = _rms(c_q, qn_ref[...]).astype(BF16)
    qT = _dot_nt(wuqT_ref[...], cqn)
    cosT = cosT_ref[...]
    sinT = sinT_ref[...]
    for h in range(MLA_HEADS):
        r = h * LANES
        mq_ref[0, r:r + MLA_NOPE_DIM] = (qT[r:r + MLA_NOPE_DIM] * MLA_QK_SCALE).astype(BF16)
        x1 = qT[r + MLA_NOPE_DIM:r + MLA_NOPE_DIM + ROPE_HALF]
        x2 = qT[r + MLA_NOPE_DIM + ROPE_HALF:r + MLA_NOPE_DIM + MLA_ROPE_DIM]
        mq_ref[0, r + MLA_NOPE_DIM:r + MLA_NOPE_DIM + ROPE_HALF] = (
            (x1 * cosT - x2 * sinT) * MLA_QK_SCALE).astype(BF16)
        mq_ref[0, r + MLA_NOPE_DIM + ROPE_HALF:r + MLA_NOPE_DIM + MLA_ROPE_DIM] = (
            (x1 * sinT + x2 * cosT) * MLA_QK_SCALE).astype(BF16)
        mq_ref[0, r + MLA_NOPE_DIM + MLA_ROPE_DIM:r + LANES] = jnp.zeros(
            (LANES - MLA_NOPE_DIM - MLA_ROPE_DIM, tm), BF16)
    ckvn = _rms(c_kv, kvn_ref[...]).astype(BF16)
    k_rot = (misc * cc_ref[...] + misc_sw * ss_ref[...]).astype(BF16)
    mk_ref[0] = _dot(jnp.concatenate([ckvn, k_rot], axis=1), wk2_ref[...]).astype(BF16)
    vT = _dot_nt(wvT_ref[...], ckvn)
    for j in range(tm // ATTN_BLOCK):
        mv_ref[0, j] = vT[:, j * ATTN_BLOCK:(j + 1) * ATTN_BLOCK].astype(BF16)


def _rope_tables(S):
    inv_freq = ROPE_BASE ** (-jnp.arange(ROPE_HALF, dtype=F32) / ROPE_HALF)
    ang = jnp.arange(S, dtype=F32)[:, None] * inv_freq[None, :]
    cos, sin = jnp.cos(ang), jnp.sin(ang)
    pad = jnp.zeros((S, LANES - MLA_ROPE_DIM), F32)
    cc = jnp.concatenate([cos, cos, pad], axis=1)
    ss = jnp.concatenate([-sin, sin, pad], axis=1)
    return cos.T, sin.T, cc, ss


def _odd_in(x, g, w_in, q_norm, w_uq, kv_norm, w_ukv):
    B, S, D = x.shape
    tm = ROW_TILE
    cuts = np.cumsum([MIX_WIDTH] * 4 + [MLSTM_HEADS, MLSTM_HEADS, MLA_Q_RANK, MLA_KV_RANK]).tolist()
    w_main = w_in[:, :cuts[3]]
    w_i = w_in[:, cuts[3]:cuts[4]]
    w_f = w_in[:, cuts[4]:cuts[5]]
    w_cq = w_in[:, cuts[5]:cuts[6]]
    w_ckv = w_in[:, cuts[6]:cuts[7]]
    w_kr = w_in[:, cuts[7]:]
    w_kr_sw = jnp.concatenate([w_kr[:, ROPE_HALF:], w_kr[:, :ROPE_HALF]], axis=1)
    zpad = lambda n: jnp.zeros((D, n), F32)
    w_misc = jnp.concatenate([w_kr, w_i, w_f, zpad(LANES - MLA_ROPE_DIM - 2 * MLSTM_HEADS)], axis=1)
    w_misc_sw = jnp.concatenate([w_kr_sw, zpad(LANES - MLA_ROPE_DIM)], axis=1)
    wn = jnp.concatenate([w_main, w_cq, w_ckv, w_misc, w_misc_sw], axis=1).astype(BF16)
    qd = MLA_NOPE_DIM + MLA_ROPE_DIM
    w_uq_h = w_uq.reshape(MLA_Q_RANK, MLA_HEADS, qd)
    w_uq_h = jnp.concatenate([w_uq_h, jnp.zeros((MLA_Q_RANK, MLA_HEADS, LANES - qd), F32)], axis=2)
    wuqT = w_uq_h.reshape(MLA_Q_RANK, MLA_HEADS * LANES).T.astype(BF16)
    w_ukv_h = w_ukv.reshape(MLA_KV_RANK, MLA_HEADS, MLA_NOPE_DIM + MLA_V_DIM)
    w_k = jnp.concatenate([w_ukv_h[:, :, :MLA_NOPE_DIM],
                           jnp.zeros((MLA_KV_RANK, MLA_HEADS, LANES - MLA_NOPE_DIM), F32)], axis=2)
    place = jnp.zeros((LANES, MLA_HEADS, LANES), F32)
    eye = jnp.eye(MLA_ROPE_DIM, dtype=F32)
    place = place.at[:MLA_ROPE_DIM, :, MLA_NOPE_DIM:MLA_NOPE_DIM + MLA_ROPE_DIM].set(
        jnp.broadcast_to(eye[:, None, :], (MLA_ROPE_DIM, MLA_HEADS, MLA_ROPE_DIM)))
    wk2 = jnp.concatenate([w_k, place], axis=0).reshape(MLA_KV_RANK + LANES, MLA_HEADS * LANES).astype(BF16)
    wvT = w_ukv_h[:, :, MLA_NOPE_DIM:].reshape(MLA_KV_RANK, MLA_HEADS * MLA_V_DIM).T.astype(BF16)
    cosT, sinT, cc, ss = _rope_tables(S)
    row = lambda b, i: (b, i, 0)
    const = lambda b, i: (0, 0)
    nb = S // ATTN_BLOCK
    return pl.pallas_call(
        _odd_in_kernel,
        grid=(B, S // tm),
        in_specs=[
            pl.BlockSpec((1, tm, D), row),
            pl.BlockSpec((1, D), const),
            pl.BlockSpec((D, ODD_COLS), const),
            pl.BlockSpec((1, MLA_Q_RANK), const),
            pl.BlockSpec((MLA_HEADS * LANES, MLA_Q_RANK), const),
            pl.BlockSpec((1, MLA_KV_RANK), const),
            pl.BlockSpec((MLA_KV_RANK + LANES, MLA_HEADS * LANES), const),
            pl.BlockSpec((MLA_HEADS * MLA_V_DIM, MLA_KV_RANK), const),
            pl.BlockSpec((ROPE_HALF, tm), lambda b, i: (0, i)),
            pl.BlockSpec((ROPE_HALF, tm), lambda b, i: (0, i)),
            pl.BlockSpec((tm, LANES), lambda b, i: (i, 0)),
            pl.BlockSpec((tm, LANES), lambda b, i: (i, 0)),
        ],
        out_specs=[
            pl.BlockSpec((1, tm, 2 * MIX_WIDTH), row),
            pl.BlockSpec((1, tm, MIX_WIDTH), row),
            pl.BlockSpec((1, tm, MIX_WIDTH), row),
            pl.BlockSpec((1, tm, LANES), row),
            pl.BlockSpec((1, MLA_HEADS * LANES, tm), lambda b, i: (b, 0, i)),
            pl.BlockSpec((1, tm, MLA_HEADS * LANES), row),
            pl.BlockSpec((1, tm // ATTN_BLOCK, MLA_HEADS * MLA_V_DIM, ATTN_BLOCK), lambda b, i: (b, i, 0, 0)),
        ],
        out_shape=[
            jax.ShapeDtypeStruct((B, S, 2 * MIX_WIDTH), F32),
            jax.ShapeDtypeStruct((B, S, MIX_WIDTH), BF16),
            jax.ShapeDtypeStruct((B, S, MIX_WIDTH), F32),
            jax.ShapeDtypeStruct((B, S, LANES), F32),
            jax.ShapeDtypeStruct((B, MLA_HEADS * LANES, S), BF16),
            jax.ShapeDtypeStruct((B, S, MLA_HEADS * LANES), BF16),
            jax.ShapeDtypeStruct((B, nb, MLA_HEADS * MLA_V_DIM, ATTN_BLOCK), BF16),
        ],
        compiler_params=_params(("parallel", "parallel")),
        name="odd_in",
    )(x, g.reshape(1, D), wn, q_norm.reshape(1, -1), wuqT, kv_norm.reshape(1, -1), wk2, wvT,
      cosT, sinT, cc, ss)


def _log_sigmoid(x):
    return jnp.minimum(x, 0.0) - jnp.log(1.0 + jnp.exp(-jnp.abs(x)))


def _mlstm_kernel(qk_ref, v_ref, op_ref, misc_ref, cw_ref, gb_ref, hn_ref, o_ref,
                  prev_ref, cn_ref, m_ref):
    c = pl.program_id(1)
    L = qk_ref.shape[1]
    row = lax.broadcasted_iota(jnp.int32, (L, 1), 0)
    lane = lax.broadcasted_iota(jnp.int32, (L, LANES), 1)
    is_f = (lane >= MISC_F) & (lane < MISC_F + MLSTM_HEADS)
    causal = lax.broadcasted_iota(jnp.int32, (L, L), 1) <= lax.broadcasted_iota(jnp.int32, (L, L), 0)
    ones_col = (lane == 0).astype(BF16)

    @pl.when(c == 0)
    def _():
        prev_ref[...] = jnp.zeros_like(prev_ref)
        cn_ref[...] = jnp.zeros_like(cn_ref)
        m_ref[...] = jnp.zeros_like(m_ref)

    seqs = range(qk_ref.shape[0])
    heads = [(bb, h) for bb in seqs for h in range(MLSTM_HEADS)]
    d = MLSTM_HEAD_DIM

    qk = []
    for bb in seqs:
        x = qk_ref[bb]
        prev = prev_ref[bb]
        conv = x * cw_ref[CONV_WIDTH - 1:CONV_WIDTH, :]
        for j in range(1, CONV_WIDTH):
            shifted = pltpu.roll(jnp.where(row >= L - j, prev, x), j, axis=0)
            conv = conv + shifted * cw_ref[CONV_WIDTH - 1 - j:CONV_WIDTH - j, :]
        prev_ref[bb] = x
        qk.append(conv * _sigmoid(conv))

    zs, zTs = [], []
    for bb in seqs:
        gates = misc_ref[bb] + gb_ref[...]
        z = jnp.where(is_f, _log_sigmoid(gates), gates)
        cum = jnp.dot(causal.astype(F32), z, precision=HIGHEST, preferred_element_type=F32)
        z = jnp.where(is_f, cum, z)
        zs.append(z)
        zTs.append(z.T)

    q, k, v_aug, i_col, b_col, m_prev, m_t, w_inter, scores, inter = ({} for _ in range(10))
    for key in heads:
        bb, h = key
        lo, hi = h * d, (h + 1) * d
        q[key] = qk[bb][:, lo:hi].astype(BF16)
        k[key] = qk[bb][:, MIX_WIDTH + lo:MIX_WIDTH + hi] * (d ** -0.5)
        v_aug[key] = jnp.concatenate([v_ref[bb, :, lo:hi], ones_col], axis=1)
        scores[key] = _dot_nt(q[key], k[key].astype(BF16))
        inter[key] = _dot(q[key], cn_ref[bb, h].astype(BF16))
    intra = {}
    for key in heads:
        bb, h = key
        i_col[key] = zs[bb][:, MISC_I + h:MISC_I + h + 1]
        b_col[key] = zs[bb][:, MISC_F + h:MISC_F + h + 1]
        i_row = zTs[bb][MISC_I + h:MISC_I + h + 1, :]
        b_row = zTs[bb][MISC_F + h:MISC_F + h + 1, :]
        m_prev[key] = m_ref[bb, h:h + 1, 0:1]
        intra[key] = jnp.where(causal, b_col[key] - b_row + i_row, NEG_INF)
    for key in heads:
        m_inter = b_col[key] + m_prev[key]
        m_t[key] = jnp.maximum(m_inter, jnp.max(intra[key], axis=1, keepdims=True))
        w_inter[key] = jnp.exp(m_inter - m_t[key])
    intra_o = {}
    for key in heads:
        a = jnp.exp(intra[key] - m_t[key]) * scores[key]
        intra_o[key] = _dot(a.astype(BF16), v_aug[key])
    for key in heads:
        bb, h = key
        lo, hi = h * d, (h + 1) * d
        num = w_inter[key] * inter[key][:, :d] + intra_o[key][:, :d]
        den = w_inter[key] * inter[key][:, d:d + 1] + intra_o[key][:, d:d + 1]
        hh = num / jnp.maximum(jnp.abs(den), jnp.exp(-m_t[key]))
        hh = _rms(hh, hn_ref[:, lo:hi])
        o_ref[bb, :, lo:hi] = (hh * _sigmoid(op_ref[bb, :, lo:hi])).astype(o_ref.dtype)
    for key in heads:
        bb, h = key
        b_end = b_col[key][L - 1:L, :]
        g_col = b_end - b_col[key] + i_col[key]
        m_new = jnp.maximum(b_end + m_prev[key], jnp.max(g_col, axis=0, keepdims=True))
        decay = jnp.exp(b_end + m_prev[key] - m_new)
        kw = k[key] * jnp.exp(g_col - m_new)
        cn_ref[bb, h] = decay * cn_ref[bb, h] + _dot(kw.T.astype(BF16), v_aug[key])
        m_ref[bb, h:h + 1, :] = jnp.broadcast_to(m_new, (1, LANES))


def _mlstm(qk_raw, vc, o_pre, misc, conv_w, b_i, b_f, head_norm):
    B, S, _ = qk_raw.shape
    L = MLSTM_CHUNK
    nb = MLSTM_BATCH
    gb = jnp.zeros((1, LANES), F32).at[0, MISC_I:MISC_I + MLSTM_HEADS].set(b_i)
    gb = gb.at[0, MISC_F:MISC_F + MLSTM_HEADS].set(b_f)
    row = lambda b, c: (b, c, 0)
    const = lambda b, c: (0, 0)
    return pl.pallas_call(
        _mlstm_kernel,
        grid=(B // nb, S // L),
        in_specs=[
            pl.BlockSpec((nb, L, 2 * MIX_WIDTH), row),
            pl.BlockSpec((nb, L, MIX_WIDTH), row),
            pl.BlockSpec((nb, L, MIX_WIDTH), row),
            pl.BlockSpec((nb, L, LANES), row),
            pl.BlockSpec((CONV_WIDTH, 2 * MIX_WIDTH), const),
            pl.BlockSpec((1, LANES), const),
            pl.BlockSpec((1, MIX_WIDTH), const),
        ],
        out_specs=pl.BlockSpec((nb, L, MIX_WIDTH), row),
        out_shape=jax.ShapeDtypeStruct((B, S, MIX_WIDTH), BF16),
        scratch_shapes=[
            pltpu.VMEM((nb, L, 2 * MIX_WIDTH), F32),
            pltpu.VMEM((nb, MLSTM_HEADS, MLSTM_HEAD_DIM, 2 * LANES), F32),
            pltpu.VMEM((nb, 8, LANES), F32),
        ],
        compiler_params=_params(("parallel", "arbitrary")),
        name="mlstm",
    )(qk_raw, vc, o_pre, misc, conv_w, gb, head_norm.reshape(1, MIX_WIDTH))


def _even_layer(h, norm_mix, w_in, pool_w, pool_scale, w_out, norm_ffn, wg, wu, wd):
    B, S, D = h.shape
    slopes = (2.0 ** (-8.0 * np.arange(1, MOBA_HEADS + 1) / MOBA_HEADS)).astype(np.float32)
    ka, ub, kmean, qT, vT = _even_in(h, norm_mix, w_in, slopes)
    nb = S // MOBA_BLOCK
    kmean = kmean.reshape(B, nb, MOBA_HEADS, LANES)[..., :MOBA_HEAD_DIM].transpose(0, 2, 1, 3)
    qaT = _moba_gate(kmean, qT)
    aT = _attention(qaT, ka, vT, MOBA_HEAD_DIM, MOBA_HEADS, "moba_attn")
    b_out = _pool(ub, pool_w, pool_scale)
    h = _mix_out(h, aT, b_out, w_out, a_first=True)
    return _ffn(h.reshape(B * S, D), norm_ffn, wg.astype(BF16), wu.astype(BF16), wd.astype(BF16))


def _odd_layer(h, norm_mix, w_in, conv_w, b_i, b_f, head_norm, q_norm, w_uq, kv_norm, w_ukv,
               w_out, norm_ffn, router_w, router_b, wg, wu, wd):
    B, S, D = h.shape
    qk_raw, vc, o_pre, misc, mqT, mk, mvT = _odd_in(h, norm_mix, w_in, q_norm, w_uq, kv_norm, w_ukv)
    c_out = _mlstm(qk_raw, vc, o_pre, misc, conv_w, b_i, b_f, head_norm)
    dT = _attention(mqT, mk, mvT, MLA_V_DIM, MLA_HEADS, "mla_attn")
    h = _mix_out(h, dT, c_out, w_out, a_first=False)
    return _moe(h.reshape(B * S, D), norm_ffn, router_w, router_b,
                wg.astype(BF16), wu.astype(BF16), wd.astype(BF16))


def kernel(x, p, ev_norm_mix, ev_w_in, pool_w, pool_scale, ev_w_out, ev_norm_ffn, ffn_w_gate, ffn_w_up, ffn_w_down, od_norm_mix, od_w_in, conv_w, gate_b_i, gate_b_f, mlstm_norm, mla_q_norm, mla_w_uq, mla_kv_norm, mla_w_ukv, od_w_out, od_norm_ffn, router_w, router_b, moe_w_gate, moe_w_up, moe_w_down, ple_norm, ple_w_gate, ple_w_proj, final_norm):
    B, S, D = x.shape
    depth = p.shape[0]
    assert D == D_MODEL and S % (2 * ROW_TILE) == 0 and B % MLSTM_BATCH == 0
    assert MOBA_TOPK <= S // MOBA_BLOCK <= MOBA_MAX_BLOCKS
    h = x
    for layer in range(depth):
        j = layer // 2
        if layer % 2 == 0:
            h2d = _even_layer(h, ev_norm_mix[j], ev_w_in[j], pool_w[j], pool_scale[j], ev_w_out[j],
                              ev_norm_ffn[j], ffn_w_gate[j], ffn_w_up[j], ffn_w_down[j])
        else:
            h2d = _odd_layer(h, od_norm_mix[j], od_w_in[j], conv_w[j], gate_b_i[j], gate_b_f[j],
                             mlstm_norm[j], mla_q_norm[j], mla_w_uq[j], mla_kv_norm[j], mla_w_ukv[j],
                             od_w_out[j], od_norm_ffn[j], router_w[j], router_b[j],
                             moe_w_gate[j], moe_w_up[j], moe_w_down[j])
        last = layer == depth - 1
        h2d = _ple(h2d, p[layer].reshape(B * S, PLE_DIM), ple_norm[layer], ple_w_gate[layer],
                   ple_w_proj[layer], final_g=final_norm if last else None)
        h = h2d.reshape(B, S, D)
    return h
```

```python
import functools
import math

import numpy as np
import jax
import jax.numpy as jnp
from jax import lax
from jax.experimental import pallas as pl
from jax.experimental.pallas import tpu as pltpu

F32 = jnp.float32
BF16 = jnp.bfloat16
HIGHEST = lax.Precision.HIGHEST

D_MODEL = 1024
PLE_DIM = 256
NORM_EPS = 1e-6
NEG_INF = -1e30

MOBA_HEADS = 8
MOBA_HEAD_DIM = 64
MOBA_BLOCK = 256
MOBA_TOPK = 3
POOL_WINDOWS = (2, 4, 8, 16)
POOL_GROUP_DIM = 128
POOL_HALO = 16
MLSTM_HEADS = 4
MLSTM_HEAD_DIM = 128
MLSTM_CHUNK = 128
MLSTM_BATCH = 2
assert MLSTM_CHUNK == 128
CONV_WIDTH = 4
MLA_HEADS = 4
MLA_Q_RANK = 256
MLA_KV_RANK = 128
MLA_NOPE_DIM = 64
MLA_ROPE_DIM = 32
MLA_V_DIM = 128
ROPE_BASE = 10000.0
FFN_DIM = 2816
N_EXPERTS = 8
EXPERT_DIM = 3584
MIX_WIDTH = 512

ATTN_BLOCK = 512
ROW_TILE = 512
LANES = 128
VMEM_LIMIT = 56 * 1024 * 1024

MISC_ROPE = 0
MISC_I = 32
MISC_F = 36


def _params(sem, vmem=VMEM_LIMIT):
    return pltpu.CompilerParams(dimension_semantics=sem, vmem_limit_bytes=vmem)


def _rms(x, g):
    ms = jnp.mean(x * x, axis=-1, keepdims=True)
    return x * lax.rsqrt(ms + NORM_EPS) * g


def _sigmoid(x):
    return 1.0 / (1.0 + jnp.exp(-x))


def _dot(a, b):
    return jnp.dot(a, b, preferred_element_type=F32)


def _dot_nt(a, b, precision=None):
    return lax.dot_general(a, b, (((1,), (1,)), ((), ())), precision=precision,
                           preferred_element_type=F32)


KAUG_SEL = MOBA_HEAD_DIM
KAUG_POS = KAUG_SEL + 32
MOBA_MAX_BLOCKS = KAUG_POS - KAUG_SEL


def _bf16_terms(x, n):
    out = []
    for _ in range(n):
        bits = np.float32(x).view(np.uint32)
        kept = np.uint32((int(bits) + 0x7FFF + ((int(bits) >> 16) & 1)) & 0xFFFF0000)
        term = float(kept.view(np.float32))
        out.append(term)
        x -= term
    return tuple(out)


LOG2E = math.log2(math.e)
LOG2E_TERMS = _bf16_terms(LOG2E, 3)


def _even_in_kernel(x_ref, g_ref, wn_ref, wqT_ref, wvT_ref, ext_ref,
                    ka_ref, ub_ref, km_ref, qT_ref, vT_ref):
    tm = x_ref.shape[1]
    xn = _rms(x_ref[0], g_ref[...]).astype(BF16)
    n = _dot(xn, wn_ref[...])
    ka = n[:, :MOBA_HEADS * LANES]
    ka_ref[0] = (ka + ext_ref[...].astype(F32)).astype(BF16)
    ub_ref[0] = n[:, MOBA_HEADS * LANES:]
    for j in range(tm // MOBA_BLOCK):
        km_ref[0, j] = jnp.mean(ka[j * MOBA_BLOCK:(j + 1) * MOBA_BLOCK], axis=0, keepdims=True)
    qT_ref[0] = _dot_nt(wqT_ref[...], xn)
    vT = _dot_nt(wvT_ref[...], xn)
    for j in range(tm // ATTN_BLOCK):
        vT_ref[0, j] = vT[:, j * ATTN_BLOCK:(j + 1) * ATTN_BLOCK].astype(BF16)


def _moba_key_extras(S, slopes):
    pos = np.arange(S)
    blk, off = pos // MOBA_BLOCK, pos % MOBA_BLOCK
    ext = np.zeros((S, MOBA_HEADS, LANES), np.float32)
    ext[pos, :, KAUG_SEL + blk] = 1.0
    for term in range(len(LOG2E_TERMS)):
        ext[:, :, KAUG_POS + 2 * term] = slopes[None, :] * (MOBA_BLOCK * blk)[:, None]
        ext[:, :, KAUG_POS + 2 * term + 1] = slopes[None, :] * off[:, None]
    return jnp.asarray(ext.reshape(S, MOBA_HEADS * LANES), dtype=BF16)


def _even_in(x, g, w_in, slopes):
    B, S, D = x.shape
    tm = ROW_TILE
    nb = S // MOBA_BLOCK
    wq, wk, wv, wu = (w_in[:, i * MIX_WIDTH:(i + 1) * MIX_WIDTH] for i in range(4))
    wk_slots = jnp.concatenate(
        [wk.reshape(D, MOBA_HEADS, MOBA_HEAD_DIM),
         jnp.zeros((D, MOBA_HEADS, LANES - MOBA_HEAD_DIM), F32)], axis=2).reshape(D, MOBA_HEADS * LANES)
    wn = jnp.concatenate([wk_slots, wu], axis=1).astype(BF16)
    wqT = wq.T.astype(BF16)
    wvT = wv.T.astype(BF16)
    const = lambda b, i: (0, 0)
    return pl.pallas_call(
        _even_in_kernel,
        grid=(B, S // tm),
        in_specs=[
            pl.BlockSpec((1, tm, D), lambda b, i: (b, i, 0)),
            pl.BlockSpec((1, D), const),
            pl.BlockSpec((D, MOBA_HEADS * LANES + MIX_WIDTH), const),
            pl.BlockSpec((MIX_WIDTH, D), const),
            pl.BlockSpec((MIX_WIDTH, D), const),
            pl.BlockSpec((tm, MOBA_HEADS * LANES), lambda b, i: (i, 0)),
        ],
        out_specs=[
            pl.BlockSpec((1, tm, MOBA_HEADS * LANES), lambda b, i: (b, i, 0)),
            pl.BlockSpec((1, tm, MIX_WIDTH), lambda b, i: (b, i, 0)),
            pl.BlockSpec((1, tm // MOBA_BLOCK, 1, MOBA_HEADS * LANES), lambda b, i: (b, i, 0, 0)),
            pl.BlockSpec((1, MIX_WIDTH, tm), lambda b, i: (b, 0, i)),
            pl.BlockSpec((1, tm // ATTN_BLOCK, MIX_WIDTH, ATTN_BLOCK), lambda b, i: (b, i, 0, 0)),
        ],
        out_shape=[
            jax.ShapeDtypeStruct((B, S, MOBA_HEADS * LANES), BF16),
            jax.ShapeDtypeStruct((B, S, MIX_WIDTH), F32),
            jax.ShapeDtypeStruct((B, nb, 1, MOBA_HEADS * LANES), F32),
            jax.ShapeDtypeStruct((B, MIX_WIDTH, S), F32),
            jax.ShapeDtypeStruct((B, S // ATTN_BLOCK, MIX_WIDTH, ATTN_BLOCK), BF16),
        ],
        compiler_params=_params(("parallel", "parallel")),
        name="even_in",
    )(x, g.reshape(1, D), wn, wqT, wvT, _moba_key_extras(S, slopes))


def _moba_gate_kernel(km_ref, qT_ref, qa_ref):
    i = pl.program_id(1)
    nb = km_ref.shape[2]
    tq = qT_ref.shape[2]
    row = lax.broadcasted_iota(jnp.int32, (nb, tq), 0)
    own = (i * tq + lax.broadcasted_iota(jnp.int32, (nb, tq), 1)) // MOBA_BLOCK
    past = row < own
    tail_row = lax.broadcasted_iota(jnp.int32, (LANES - KAUG_POS, tq), 0)
    tail = jnp.zeros(tail_row.shape, F32)
    for term, value in enumerate(LOG2E_TERMS):
        tail = jnp.where(tail_row // 2 == term, F32(value), tail)
    tail = tail.astype(BF16)
    pad = jnp.zeros((MOBA_MAX_BLOCKS - nb, tq), BF16) if nb < MOBA_MAX_BLOCKS else None
    for h in range(MOBA_HEADS):
        q_h = qT_ref[0, h * MOBA_HEAD_DIM:(h + 1) * MOBA_HEAD_DIM, :]
        gate = jnp.dot(km_ref[0, h], q_h, precision=HIGHEST, preferred_element_type=F32)
        gate = jnp.where(past, gate, NEG_INF)
        chosen = jnp.zeros(gate.shape, F32)
        for _ in range(MOBA_TOPK):
            mx = jnp.max(gate, axis=0, keepdims=True)
            first = jnp.min(jnp.where(gate == mx, row, nb), axis=0, keepdims=True)
            pick = row == first
            chosen = jnp.where(pick, 1.0, chosen)
            gate = jnp.where(pick, -jnp.inf, gate)
        keep = jnp.where(past, chosen, (row == own).astype(F32))
        sel = jnp.where(keep > 0.0, 0.0, NEG_INF).astype(BF16)
        base = h * LANES
        qa_ref[0, base:base + KAUG_SEL] = (q_h * (MOBA_HEAD_DIM ** -0.5 * LOG2E)).astype(BF16)
        qa_ref[0, base + KAUG_SEL:base + KAUG_SEL + nb] = sel
        if pad is not None:
            qa_ref[0, base + KAUG_SEL + nb:base + KAUG_POS] = pad
        qa_ref[0, base + KAUG_POS:base + LANES] = tail


def _moba_gate(kmean, qT):
    B, H, nb, dh = kmean.shape
    S = qT.shape[2]
    tq = ATTN_BLOCK
    return pl.pallas_call(
        _moba_gate_kernel,
        grid=(B, S // tq),
        in_specs=[
            pl.BlockSpec((1, H, nb, dh), lambda b, i: (b, 0, 0, 0)),
            pl.BlockSpec((1, H * dh, tq), lambda b, i: (b, 0, i)),
        ],
        out_specs=pl.BlockSpec((1, H * LANES, tq), lambda b, i: (b, 0, i)),
        out_shape=jax.ShapeDtypeStruct((B, H * LANES, S), BF16),
        compiler_params=_params(("parallel", "parallel")),
        name="moba_gate",
    )(kmean, qT)


HEADS_PER_STEP = 4
SUM_ROWS = 16


def _attn_kernel(q_ref, k_ref, v_ref, o_ref, sa_ref, sb_ref, m_ref, acc_ref):
    i = pl.program_id(2)
    tq = q_ref.shape[2]
    tk = ATTN_BLOCK
    hp = HEADS_PER_STEP
    dv = v_ref.shape[2] // hp
    m_ref[...] = jnp.full(m_ref.shape, NEG_INF, F32)
    acc_ref[...] = jnp.zeros(acc_ref.shape, F32)
    ones_rows = jnp.ones((SUM_ROWS, tk), BF16)

    def scores(kvt, s_ref, diag):
        start = pl.multiple_of(kvt * tk, tk)
        k_tile = k_ref[0, pl.ds(start, tk), :]
        for g in range(hp):
            s = _dot(k_tile[:, g * LANES:(g + 1) * LANES], q_ref[0, g * LANES:(g + 1) * LANES, :])
            if diag:
                key = lax.broadcasted_iota(jnp.int32, (tk, tq), 0)
                qry = lax.broadcasted_iota(jnp.int32, (tk, tq), 1)
                s = jnp.where(key <= qry, s, NEG_INF)
            s_ref[g] = s

    def consume(kvt, s_ref):
        v_tile = v_ref[0, kvt]
        for g in range(hp):
            s = s_ref[g]
            m_run = m_ref[g]
            m_new = jnp.maximum(m_run, jnp.max(s, axis=0, keepdims=True))
            p = jnp.exp2(s - m_new).astype(BF16)
            v_aug = jnp.concatenate([v_tile[g * dv:(g + 1) * dv], ones_rows], axis=0)
            acc_ref[g] = jnp.exp2(m_run - m_new) * acc_ref[g] + _dot(v_aug, p)
            m_ref[g] = m_new

    tile_at = lambda t: jnp.where(t == 0, i, t - 1)
    scores(i, sa_ref, True)

    def pair(p, carry):
        t = 2 * p
        scores(tile_at(t + 1), sb_ref, False)
        consume(tile_at(t), sa_ref)
        scores(tile_at(t + 2), sa_ref, False)
        consume(tile_at(t + 1), sb_ref)
        return carry

    lax.fori_loop(0, i // 2, pair, 0)
    last = 2 * (i // 2)

    @pl.when(i % 2 == 1)
    def _():
        scores(tile_at(last + 1), sb_ref, False)
        consume(tile_at(last), sa_ref)
        consume(tile_at(last + 1), sb_ref)

    @pl.when(i % 2 == 0)
    def _():
        consume(tile_at(last), sa_ref)

    for g in range(hp):
        o_ref[0, g * dv:(g + 1) * dv, :] = acc_ref[g, :dv] / acc_ref[g, dv:dv + 1]


def _attention(qT, k, vT, dv, heads, name):
    B, _, S = qT.shape
    tq = ATTN_BLOCK
    nt = S // ATTN_BLOCK
    hp = HEADS_PER_STEP
    return pl.pallas_call(
        _attn_kernel,
        grid=(B, heads // hp, S // tq),
        in_specs=[
            pl.BlockSpec((1, hp * LANES, tq), lambda b, h, i: (b, h, i)),
            pl.BlockSpec((1, S, hp * LANES), lambda b, h, i: (b, 0, h)),
            pl.BlockSpec((1, nt, hp * dv, ATTN_BLOCK), lambda b, h, i: (b, 0, h, 0)),
        ],
        out_specs=pl.BlockSpec((1, hp * dv, tq), lambda b, h, i: (b, h, i)),
        out_shape=jax.ShapeDtypeStruct((B, heads * dv, S), F32),
        scratch_shapes=[pltpu.VMEM((hp, ATTN_BLOCK, tq), F32), pltpu.VMEM((hp, ATTN_BLOCK, tq), F32),
                        pltpu.VMEM((hp, 1, tq), F32), pltpu.VMEM((hp, dv + SUM_ROWS, tq), F32)],
        compiler_params=_params(("parallel", "parallel", "arbitrary")),
        name=name,
    )(qT, k, vT)


def _pool_kernel(x_ref, halo_ref, w_ref, sc_ref, o_ref, xs_ref):
    i = pl.program_id(1)
    tm = x_ref.shape[1]
    x = x_ref[0]
    xs_ref[0:POOL_HALO] = jnp.where(i > 0, halo_ref[0], 0.0)
    xs_ref[POOL_HALO:POOL_HALO + tm] = x
    t = i * tm + lax.broadcasted_iota(jnp.int32, (tm, 1), 0)
    outs = []
    for g, win in enumerate(POOL_WINDOWS):
        lo, hi = g * POOL_GROUP_DIM, (g + 1) * POOL_GROUP_DIM
        xg = x[:, lo:hi]
        acc = xg
        for d in range(1, win):
            acc = acc + xs_ref[POOL_HALO - d:POOL_HALO - d + tm, lo:hi]
        count = jnp.minimum(t + 1, win).astype(F32)
        outs.append(_dot((acc / count - xg).astype(BF16), w_ref[g]))
    o_ref[0] = (jnp.concatenate(outs, axis=1) * sc_ref[...]).astype(o_ref.dtype)


def _pool(ub, pool_w, pool_scale):
    B, S, W = ub.shape
    tm = ROW_TILE
    per = tm // POOL_HALO
    return pl.pallas_call(
        _pool_kernel,
        grid=(B, S // tm),
        in_specs=[
            pl.BlockSpec((1, tm, W), lambda b, i: (b, i, 0)),
            pl.BlockSpec((1, POOL_HALO, W), lambda b, i: (b, jnp.maximum(i * per - 1, 0), 0)),
            pl.BlockSpec(pool_w.shape, lambda b, i: (0, 0, 0)),
            pl.BlockSpec((1, W), lambda b, i: (0, 0)),
        ],
        out_specs=pl.BlockSpec((1, tm, W), lambda b, i: (b, i, 0)),
        out_shape=jax.ShapeDtypeStruct((B, S, W), BF16),
        scratch_shapes=[pltpu.VMEM((POOL_HALO + tm, W), F32)],
        compiler_params=_params(("parallel", "parallel")),
        name="pool",
    )(ub, ub, pool_w.astype(BF16), pool_scale.reshape(1, W))


def _mix_out_kernel(h_ref, aT_ref, b_ref, w_ref, o_ref, *, a_first):
    a = aT_ref[0].T.astype(BF16)
    b = b_ref[0].astype(BF16)
    lo, hi = (a, b) if a_first else (b, a)
    y = _dot(lo, w_ref[:MIX_WIDTH]) + _dot(hi, w_ref[MIX_WIDTH:])
    o_ref[0] = h_ref[0] + y


def _mix_out(h, aT, b, w_out, a_first):
    B, S, D = h.shape
    tm = ROW_TILE
    return pl.pallas_call(
        functools.partial(_mix_out_kernel, a_first=a_first),
        grid=(B, S // tm),
        in_specs=[
            pl.BlockSpec((1, tm, D), lambda b_, i: (b_, i, 0)),
            pl.BlockSpec((1, MIX_WIDTH, tm), lambda b_, i: (b_, 0, i)),
            pl.BlockSpec((1, tm, MIX_WIDTH), lambda b_, i: (b_, i, 0)),
            pl.BlockSpec((2 * MIX_WIDTH, D), lambda b_, i: (0, 0)),
        ],
        out_specs=pl.BlockSpec((1, tm, D), lambda b_, i: (b_, i, 0)),
        out_shape=jax.ShapeDtypeStruct((B, S, D), F32),
        compiler_params=_params(("parallel", "parallel")),
        name="mix_out",
    )(h, aT, b, w_out.astype(BF16))


def _swiglu_step(xn, wg, wu, wd):
    gt = _dot(xn, wg)
    up = _dot(xn, wu)
    return _dot((gt * _sigmoid(gt) * up).astype(BF16), wd)


def _ffn_kernel(h_ref, g_ref, wg_ref, wu_ref, wd_ref, o_ref, xn_ref, acc_ref):
    f = pl.program_id(1)

    @pl.when(f == 0)
    def _():
        xn_ref[...] = _rms(h_ref[...], g_ref[...]).astype(BF16)
        acc_ref[...] = jnp.zeros_like(acc_ref)

    acc_ref[...] += _swiglu_step(xn_ref[...], wg_ref[...], wu_ref[...], wd_ref[...])

    @pl.when(f == pl.num_programs(1) - 1)
    def _():
        o_ref[...] = h_ref[...] + acc_ref[...]


def _ffn(h2d, g, wg, wu, wd, tm=ROW_TILE, tf=FFN_DIM // 2):
    T, D = h2d.shape
    F = wg.shape[1]
    return pl.pallas_call(
        _ffn_kernel,
        grid=(T // tm, F // tf),
        in_specs=[
            pl.BlockSpec((tm, D), lambda i, f: (i, 0)),
            pl.BlockSpec((1, D), lambda i, f: (0, 0)),
            pl.BlockSpec((D, tf), lambda i, f: (0, f)),
            pl.BlockSpec((D, tf), lambda i, f: (0, f)),
            pl.BlockSpec((tf, D), lambda i, f: (f, 0)),
        ],
        out_specs=pl.BlockSpec((tm, D), lambda i, f: (i, 0)),
        out_shape=jax.ShapeDtypeStruct((T, D), F32),
        scratch_shapes=[pltpu.VMEM((tm, D), BF16), pltpu.VMEM((tm, D), F32)],
        compiler_params=_params(("parallel", "arbitrary")),
        name="ffn",
    )(h2d, g.reshape(1, D), wg, wu, wd)


ROUTE_E0, ROUTE_E1, ROUTE_W0, ROUTE_W1, ROUTE_R0, ROUTE_R1 = range(6)
MOE_TILE = 512
MOE_TF = EXPERT_DIM // 2


def _lane_pick(tile, lane, idx):
    return jnp.sum(jnp.where(lane == idx, tile, 0.0), axis=1, keepdims=True)


def _router_kernel(h_ref, g_ref, w_ref, b_ref, route_ref, cnt_ref):
    tm = h_ref.shape[0]

    @pl.when(pl.program_id(0) == 0)
    def _():
        cnt_ref[...] = jnp.zeros_like(cnt_ref)

    xn = _rms(h_ref[...], g_ref[...])
    logits = jnp.dot(xn, w_ref[...], precision=HIGHEST, preferred_element_type=F32) + b_ref[...]
    lane = lax.broadcasted_iota(jnp.int32, logits.shape, 1)
    logits = jnp.where(lane < N_EXPERTS, logits, -jnp.inf)
    v0 = jnp.max(logits, axis=1, keepdims=True)
    i0 = jnp.min(jnp.where(logits == v0, lane, LANES), axis=1, keepdims=True)
    rest = jnp.where(lane == i0, -jnp.inf, logits)
    v1 = jnp.max(rest, axis=1, keepdims=True)
    i1 = jnp.min(jnp.where(rest == v1, lane, LANES), axis=1, keepdims=True)
    e1 = jnp.exp(v1 - v0)
    w0 = 1.0 / (1.0 + e1)
    sel = (lane == i0).astype(F32) + (lane == i1).astype(F32)
    earlier = (lax.broadcasted_iota(jnp.int32, (tm, tm), 1)
               < lax.broadcasted_iota(jnp.int32, (tm, tm), 0))
    rank = _dot(earlier.astype(BF16), sel.astype(BF16)) + cnt_ref[...]
    cnt_ref[...] += jnp.sum(sel, axis=0, keepdims=True)
    cols = ((ROUTE_E0, i0.astype(F32)), (ROUTE_E1, i1.astype(F32)), (ROUTE_W0, w0), (ROUTE_W1, e1 * w0),
            (ROUTE_R0, _lane_pick(rank, lane, i0)), (ROUTE_R1, _lane_pick(rank, lane, i1)))
    route = jnp.zeros(logits.shape, F32)
    for c, val in cols:
        route = jnp.where(lane == c, val, route)
    route_ref[...] = route


def _router(h2d, g, router_w, router_b):
    T, D = h2d.shape
    tm = ROW_TILE
    w = jnp.zeros((D, LANES), F32).at[:, :N_EXPERTS].set(router_w)
    b = jnp.zeros((1, LANES), F32).at[0, :N_EXPERTS].set(router_b)
    return pl.pallas_call(
        _router_kernel,
        grid=(T // tm,),
        in_specs=[
            pl.BlockSpec((tm, D), lambda i: (i, 0)),
            pl.BlockSpec((1, D), lambda i: (0, 0)),
            pl.BlockSpec((D, LANES), lambda i: (0, 0)),
            pl.BlockSpec((1, LANES), lambda i: (0, 0)),
        ],
        out_specs=[pl.BlockSpec((tm, LANES), lambda i: (i, 0)),
                   pl.BlockSpec((1, LANES), lambda i: (0, 0))],
        out_shape=[jax.ShapeDtypeStruct((T, LANES), F32), jax.ShapeDtypeStruct((1, LANES), F32)],
        compiler_params=_params(("arbitrary",)),
        name="router",
    )(h2d, g.reshape(1, D), w, b)


def _row_copies(pos_ref, base, r, src_of, dst_of, sem):
    return [pltpu.make_async_copy(src_of(k, pos_ref[base + 2 * r + k]),
                                  dst_of(k, pos_ref[base + 2 * r + k]), sem) for k in range(2)]


def _all_rows(tm, make):
    def issue(r, c):
        for k, cp in enumerate(make(r)):
            cp.start(priority=k)
        return c

    def drain(r, c):
        for cp in make(r):
            cp.wait()
        return c

    lax.fori_loop(0, tm, issue, 0, unroll=8)
    lax.fori_loop(0, tm, drain, 0, unroll=8)


def _dispatch_kernel(pos_ref, h_ref, g_ref, init_ref, xs_ref, xn_ref, sem):
    del init_ref
    tm = h_ref.shape[0]
    xn_ref[...] = _rms(h_ref[...], g_ref[...])
    base = pl.program_id(0) * (2 * tm)
    _all_rows(tm, lambda r: _row_copies(
        pos_ref, base, r, lambda k, p: xn_ref.at[pl.ds(r, 1)], lambda k, p: xs_ref.at[pl.ds(p, 1)], sem))


def _dispatch(pos, h2d, g, n_rows):
    T, D = h2d.shape
    tm = ROW_TILE
    return pl.pallas_call(
        _dispatch_kernel,
        grid_spec=pltpu.PrefetchScalarGridSpec(
            num_scalar_prefetch=1,
            grid=(T // tm,),
            in_specs=[pl.BlockSpec((tm, D), lambda i, pos: (i, 0)),
                      pl.BlockSpec((1, D), lambda i, pos: (0, 0)),
                      pl.BlockSpec(memory_space=pl.ANY)],
            out_specs=pl.BlockSpec(memory_space=pl.ANY),
            scratch_shapes=[pltpu.VMEM((tm, D), F32), pltpu.SemaphoreType.DMA(())],
        ),
        out_shape=jax.ShapeDtypeStruct((n_rows, D), F32),
        input_output_aliases={3: 0},
        compiler_params=_params(("arbitrary",)),
        name="moe_dispatch",
    )(pos, h2d, g.reshape(1, D), jnp.zeros((n_rows, D), F32))


def _moe_ffn_kernel(te_ref, nv_ref, x_ref, wg_ref, wu_ref, wd_ref, o_ref, xb_ref, acc_ref):
    j = pl.program_id(0)
    f = pl.program_id(1)
    valid = j < nv_ref[0]

    @pl.when(f == 0)
    def _():
        xb_ref[...] = x_ref[...].astype(BF16)
        acc_ref[...] = jnp.zeros_like(acc_ref)

    @pl.when(valid)
    def _():
        acc_ref[...] += _swiglu_step(xb_ref[...], wg_ref[0], wu_ref[0], wd_ref[0])

    @pl.when(f == pl.num_programs(1) - 1)
    def _():
        o_ref[...] = acc_ref[...]


def _moe_ffn(tile_expert, n_valid, xs, wg, wu, wd):
    N, D = xs.shape
    tm, tf = MOE_TILE, MOE_TF
    F = wg.shape[2]
    return pl.pallas_call(
        _moe_ffn_kernel,
        grid_spec=pltpu.PrefetchScalarGridSpec(
            num_scalar_prefetch=2,
            grid=(N // tm, F // tf),
            in_specs=[pl.BlockSpec((tm, D), lambda j, f, te, nv: (j, 0)),
                      pl.BlockSpec((1, D, tf), lambda j, f, te, nv: (te[j], 0, f)),
                      pl.BlockSpec((1, D, tf), lambda j, f, te, nv: (te[j], 0, f)),
                      pl.BlockSpec((1, tf, D), lambda j, f, te, nv: (te[j], f, 0))],
            out_specs=pl.BlockSpec((tm, D), lambda j, f, te, nv: (j, 0)),
            scratch_shapes=[pltpu.VMEM((tm, D), BF16), pltpu.VMEM((tm, D), F32)],
        ),
        out_shape=jax.ShapeDtypeStruct((N, D), F32),
        compiler_params=_params(("arbitrary", "arbitrary")),
        name="moe_ffn",
    )(tile_expert, n_valid, xs, wg, wu, wd)


def _combine_kernel(pos_ref, h_ref, route_ref, ys_ref, o_ref, y_ref, sem):
    tm = h_ref.shape[0]
    base = pl.program_id(0) * (2 * tm)
    _all_rows(tm, lambda r: _row_copies(
        pos_ref, base, r, lambda k, p: ys_ref.at[pl.ds(p, 1)], lambda k, p: y_ref.at[k, pl.ds(r, 1)], sem))
    route = route_ref[...]
    lane = lax.broadcasted_iota(jnp.int32, route.shape, 1)
    o_ref[...] = (h_ref[...] + _lane_pick(route, lane, ROUTE_W0) * y_ref[0]
                  + _lane_pick(route, lane, ROUTE_W1) * y_ref[1])


def _combine(pos, h2d, route, ys):
    T, D = h2d.shape
    tm = ROW_TILE
    return pl.pallas_call(
        _combine_kernel,
        grid_spec=pltpu.PrefetchScalarGridSpec(
            num_scalar_prefetch=1,
            grid=(T // tm,),
            in_specs=[pl.BlockSpec((tm, D), lambda i, pos: (i, 0)),
                      pl.BlockSpec((tm, LANES), lambda i, pos: (i, 0)),
                      pl.BlockSpec(memory_space=pl.ANY)],
            out_specs=pl.BlockSpec((tm, D), lambda i, pos: (i, 0)),
            scratch_shapes=[pltpu.VMEM((2, tm, D), F32), pltpu.SemaphoreType.DMA(())],
        ),
        out_shape=jax.ShapeDtypeStruct((T, D), F32),
        compiler_params=_params(("arbitrary",)),
        name="moe_combine",
    )(pos, h2d, route, ys)


def _moe(h2d, g, router_w, router_b, wg, wu, wd):
    T, D = h2d.shape
    tm = MOE_TILE
    route, counts = _router(h2d, g, router_w, router_b)
    cnt = counts[0, :N_EXPERTS].astype(jnp.int32)
    padded = (cnt + tm - 1) // tm * tm
    ends = jnp.cumsum(padded)
    start = ends - padded
    e01 = route[:, ROUTE_E0:ROUTE_E1 + 1].astype(jnp.int32)
    r01 = route[:, ROUTE_R0:ROUTE_R1 + 1].astype(jnp.int32)
    pos = (start[e01] + r01).reshape(2 * T)
    n_rows = 2 * T + N_EXPERTS * tm
    tile_row = jnp.arange(n_rows // tm, dtype=jnp.int32) * tm
    tile_expert = jnp.minimum(jnp.sum(tile_row[:, None] >= ends[None, :], axis=1), N_EXPERTS - 1).astype(jnp.int32)
    n_valid = (ends[-1:] // tm).astype(jnp.int32)
    xs = _dispatch(pos, h2d, g, n_rows)
    ys = _moe_ffn(tile_expert, n_valid, xs, wg, wu, wd)
    return _combine(pos, h2d, route, ys)


def _ple_kernel(*refs, final):
    if final:
        h_ref, p_ref, g_ref, wg_ref, wp_ref, fg_ref, o_ref = refs
    else:
        h_ref, p_ref, g_ref, wg_ref, wp_ref, o_ref = refs
    h = h_ref[...]
    gate = _sigmoid(_dot(_rms(h, g_ref[...]).astype(BF16), wg_ref[...]))
    out = h + gate * _dot(p_ref[...].astype(BF16), wp_ref[...])
    if final:
        out = _rms(out, fg_ref[...])
    o_ref[...] = out


def _ple(h2d, p2d, g, w_gate, w_proj, final_g=None):
    T, D = h2d.shape
    tm = ROW_TILE
    final = final_g is not None
    in_specs = [
        pl.BlockSpec((tm, D), lambda i: (i, 0)),
        pl.BlockSpec((tm, PLE_DIM), lambda i: (i, 0)),
        pl.BlockSpec((1, D), lambda i: (0, 0)),
        pl.BlockSpec((D, D), lambda i: (0, 0)),
        pl.BlockSpec((PLE_DIM, D), lambda i: (0, 0)),
    ]
    args = [h2d, p2d, g.reshape(1, D), w_gate.astype(BF16), w_proj.astype(BF16)]
    if final:
        in_specs.append(pl.BlockSpec((1, D), lambda i: (0, 0)))
        args.append(final_g.reshape(1, D))
    return pl.pallas_call(
        functools.partial(_ple_kernel, final=final),
        grid=(T // tm,),
        in_specs=in_specs,
        out_specs=pl.BlockSpec((tm, D), lambda i: (i, 0)),
        out_shape=jax.ShapeDtypeStruct((T, D), F32),
        compiler_params=_params(("parallel",)),
        name="ple_final" if final else "ple",
    )(*args)


ODD_MAIN = 4 * MIX_WIDTH
ODD_COLS = ODD_MAIN + MLA_Q_RANK + MLA_KV_RANK + 2 * LANES
MLA_QK_SCALE = (MLA_NOPE_DIM + MLA_ROPE_DIM) ** -0.5 * LOG2E
ROPE_HALF = MLA_ROPE_DIM // 2


def _odd_in_kernel(x_ref, g_ref, wn_ref, qn_ref, wuqT_ref, kvn_ref, wk2_ref, wvT_ref,
                   cosT_ref, sinT_ref, cc_ref, ss_ref,
                   qk_ref, vc_ref, op_ref, misc_ref, mq_ref, mk_ref, mv_ref):
    tm = x_ref.shape[1]
    xn = _rms(x_ref[0], g_ref[...]).astype(BF16)
    u = _dot(xn, wn_ref[...])
    qk_ref[0] = u[:, :2 * MIX_WIDTH]
    vc_ref[0] = u[:, 2 * MIX_WIDTH:3 * MIX_WIDTH].astype(BF16)
    op_ref[0] = u[:, 3 * MIX_WIDTH:ODD_MAIN]
    c0 = ODD_MAIN
    c_q = u[:, c0:c0 + MLA_Q_RANK]
    c0 += MLA_Q_RANK
    c_kv = u[:, c0:c0 + MLA_KV_RANK]
    c0 += MLA_KV_RANK
    misc = u[:, c0:c0 + LANES]
    misc_sw = u[:, c0 + LANES:c0 + 2 * LANES]
    misc_ref[0] = misc
    cqn = _rms(c_q, qn_ref[...]).astype(BF16)
    qT = _dot_nt(wuqT_ref[...], cqn)
    cosT = cosT_ref[...]
    sinT = sinT_ref[...]
    for h in range(MLA_HEADS):
        r = h * LANES
        mq_ref[0, r:r + MLA_NOPE_DIM] = (qT[r:r + MLA_NOPE_DIM] * MLA_QK_SCALE).astype(BF16)
        x1 = qT[r + MLA_NOPE_DIM:r + MLA_NOPE_DIM + ROPE_HALF]
        x2 = qT[r + MLA_NOPE_DIM + ROPE_HALF:r + MLA_NOPE_DIM + MLA_ROPE_DIM]
        mq_ref[0, r + MLA_NOPE_DIM:r + MLA_NOPE_DIM + ROPE_HALF] = (
            (x1 * cosT - x2 * sinT) * MLA_QK_SCALE).astype(BF16)
        mq_ref[0, r + MLA_NOPE_DIM + ROPE_HALF:r + MLA_NOPE_DIM + MLA_ROPE_DIM] = (
            (x1 * sinT + x2 * cosT) * MLA_QK_SCALE).astype(BF16)
        mq_ref[0, r + MLA_NOPE_DIM + MLA_ROPE_DIM:r + LANES] = jnp.zeros(
            (LANES - MLA_NOPE_DIM - MLA_ROPE_DIM, tm), BF16)
    ckvn = _rms(c_kv, kvn_ref[...]).astype(BF16)
    k_rot = (misc * cc_ref[...] + misc_sw * ss_ref[...]).astype(BF16)
    mk_ref[0] = _dot(jnp.concatenate([ckvn, k_rot], axis=1), wk2_ref[...]).astype(BF16)
    vT = _dot_nt(wvT_ref[...], ckvn)
    for j in range(tm // ATTN_BLOCK):
        mv_ref[0, j] = vT[:, j * ATTN_BLOCK:(j + 1) * ATTN_BLOCK].astype(BF16)


def _rope_tables(S):
    inv_freq = ROPE_BASE ** (-jnp.arange(ROPE_HALF, dtype=F32) / ROPE_HALF)
    ang = jnp.arange(S, dtype=F32)[:, None] * inv_freq[None, :]
    cos, sin = jnp.cos(ang), jnp.sin(ang)
    pad = jnp.zeros((S, LANES - MLA_ROPE_DIM), F32)
    cc = jnp.concatenate([cos, cos, pad], axis=1)
    ss = jnp.concatenate([-sin, sin, pad], axis=1)
    return cos.T, sin.T, cc, ss


def _odd_in(x, g, w_in, q_norm, w_uq, kv_norm, w_ukv):
    B, S, D = x.shape
    tm = ROW_TILE
    cuts = np.cumsum([MIX_WIDTH] * 4 + [MLSTM_HEADS, MLSTM_HEADS, MLA_Q_RANK, MLA_KV_RANK]).tolist()
    w_main = w_in[:, :cuts[3]]
    w_i = w_in[:, cuts[3]:cuts[4]]
    w_f = w_in[:, cuts[4]:cuts[5]]
    w_cq = w_in[:, cuts[5]:cuts[6]]
    w_ckv = w_in[:, cuts[6]:cuts[7]]
    w_kr = w_in[:, cuts[7]:]
    w_kr_sw = jnp.concatenate([w_kr[:, ROPE_HALF:], w_kr[:, :ROPE_HALF]], axis=1)
    zpad = lambda n: jnp.zeros((D, n), F32)
    w_misc = jnp.concatenate([w_kr, w_i, w_f, zpad(LANES - MLA_ROPE_DIM - 2 * MLSTM_HEADS)], axis=1)
    w_misc_sw = jnp.concatenate([w_kr_sw, zpad(LANES - MLA_ROPE_DIM)], axis=1)
    wn = jnp.concatenate([w_main, w_cq, w_ckv, w_misc, w_misc_sw], axis=1).astype(BF16)
    qd = MLA_NOPE_DIM + MLA_ROPE_DIM
    w_uq_h = w_uq.reshape(MLA_Q_RANK, MLA_HEADS, qd)
    w_uq_h = jnp.concatenate([w_uq_h, jnp.zeros((MLA_Q_RANK, MLA_HEADS, LANES - qd), F32)], axis=2)
    wuqT = w_uq_h.reshape(MLA_Q_RANK, MLA_HEADS * LANES).T.astype(BF16)
    w_ukv_h = w_ukv.reshape(MLA_KV_RANK, MLA_HEADS, MLA_NOPE_DIM + MLA_V_DIM)
    w_k = jnp.concatenate([w_ukv_h[:, :, :MLA_NOPE_DIM],
                           jnp.zeros((MLA_KV_RANK, MLA_HEADS, LANES - MLA_NOPE_DIM), F32)], axis=2)
    place = jnp.zeros((LANES, MLA_HEADS, LANES), F32)
    eye = jnp.eye(MLA_ROPE_DIM, dtype=F32)
    place = place.at[:MLA_ROPE_DIM, :, MLA_NOPE_DIM:MLA_NOPE_DIM + MLA_ROPE_DIM].set(
        jnp.broadcast_to(eye[:, None, :], (MLA_ROPE_DIM, MLA_HEADS, MLA_ROPE_DIM)))
    wk2 = jnp.concatenate([w_k, place], axis=0).reshape(MLA_KV_RANK + LANES, MLA_HEADS * LANES).astype(BF16)
    wvT = w_ukv_h[:, :, MLA_NOPE_DIM:].reshape(MLA_KV_RANK, MLA_HEADS * MLA_V_DIM).T.astype(BF16)
    cosT, sinT, cc, ss = _rope_tables(S)
    row = lambda b, i: (b, i, 0)
    const = lambda b, i: (0, 0)
    nb = S // ATTN_BLOCK
    return pl.pallas_call(
        _odd_in_kernel,
        grid=(B, S // tm),
        in_specs=[
            pl.BlockSpec((1, tm, D), row),
            pl.BlockSpec((1, D), const),
            pl.BlockSpec((D, ODD_COLS), const),
            pl.BlockSpec((1, MLA_Q_RANK), const),
            pl.BlockSpec((MLA_HEADS * LANES, MLA_Q_RANK), const),
            pl.BlockSpec((1, MLA_KV_RANK), const),
            pl.BlockSpec((MLA_KV_RANK + LANES, MLA_HEADS * LANES), const),
            pl.BlockSpec((MLA_HEADS * MLA_V_DIM, MLA_KV_RANK), const),
            pl.BlockSpec((ROPE_HALF, tm), lambda b, i: (0, i)),
            pl.BlockSpec((ROPE_HALF, tm), lambda b, i: (0, i)),
            pl.BlockSpec((tm, LANES), lambda b, i: (i, 0)),
            pl.BlockSpec((tm, LANES), lambda b, i: (i, 0)),
        ],
        out_specs=[
            pl.BlockSpec((1, tm, 2 * MIX_WIDTH), row),
            pl.BlockSpec((1, tm, MIX_WIDTH), row),
            pl.BlockSpec((1, tm, MIX_WIDTH), row),
            pl.BlockSpec((1, tm, LANES), row),
            pl.BlockSpec((1, MLA_HEADS * LANES, tm), lambda b, i: (b, 0, i)),
            pl.BlockSpec((1, tm, MLA_HEADS * LANES), row),
            pl.BlockSpec((1, tm // ATTN_BLOCK, MLA_HEADS * MLA_V_DIM, ATTN_BLOCK), lambda b, i: (b, i, 0, 0)),
        ],
        out_shape=[
            jax.ShapeDtypeStruct((B, S, 2 * MIX_WIDTH), F32),
            jax.ShapeDtypeStruct((B, S, MIX_WIDTH), BF16),
            jax.ShapeDtypeStruct((B, S, MIX_WIDTH), F32),
            jax.ShapeDtypeStruct((B, S, LANES), F32),
            jax.ShapeDtypeStruct((B, MLA_HEADS * LANES, S), BF16),
            jax.ShapeDtypeStruct((B, S, MLA_HEADS * LANES), BF16),
            jax.ShapeDtypeStruct((B, nb, MLA_HEADS * MLA_V_DIM, ATTN_BLOCK), BF16),
        ],
        compiler_params=_params(("parallel", "parallel")),
        name="odd_in",
    )(x, g.reshape(1, D), wn, q_norm.reshape(1, -1), wuqT, kv_norm.reshape(1, -1), wk2, wvT,
      cosT, sinT, cc, ss)


def _log_sigmoid(x):
    return jnp.minimum(x, 0.0) - jnp.log(1.0 + jnp.exp(-jnp.abs(x)))


def _mlstm_kernel(qk_ref, v_ref, op_ref, misc_ref, cw_ref, gb_ref, hn_ref, o_ref,
                  prev_ref, cn_ref, m_ref):
    c = pl.program_id(1)
    L = qk_ref.shape[1]
    row = lax.broadcasted_iota(jnp.int32, (L, 1), 0)
    lane = lax.broadcasted_iota(jnp.int32, (L, LANES), 1)
    is_f = (lane >= MISC_F) & (lane < MISC_F + MLSTM_HEADS)
    causal = lax.broadcasted_iota(jnp.int32, (L, L), 1) <= lax.broadcasted_iota(jnp.int32, (L, L), 0)

    @pl.when(c == 0)
    def _():
        prev_ref[...] = jnp.zeros_like(prev_ref)
        cn_ref[...] = jnp.zeros_like(cn_ref)
        m_ref[...] = jnp.zeros_like(m_ref)

    seqs = range(qk_ref.shape[0])
    heads = [(bb, h) for bb in seqs for h in range(MLSTM_HEADS)]
    d = MLSTM_HEAD_DIM

    qk = []
    for bb in seqs:
        x = qk_ref[bb]
        prev = prev_ref[bb]
        conv = x * cw_ref[CONV_WIDTH - 1:CONV_WIDTH, :]
        for j in range(1, CONV_WIDTH):
            shifted = pltpu.roll(jnp.where(row >= L - j, prev, x), j, axis=0)
            conv = conv + shifted * cw_ref[CONV_WIDTH - 1 - j:CONV_WIDTH - j, :]
        prev_ref[bb] = x
        qk.append(conv * _sigmoid(conv))

    sel_r = lax.broadcasted_iota(jnp.int32, (LANES, 2 * MIX_WIDTH), 0)
    sel_c = lax.broadcasted_iota(jnp.int32, (LANES, 2 * MIX_WIDTH), 1)
    spread = (sel_r == MISC_I + sel_c // LANES).astype(F32)
    pick = (lax.broadcasted_iota(jnp.int32, (8, LANES), 1)
            == MISC_I + lax.broadcasted_iota(jnp.int32, (8, LANES), 0)).astype(F32)
    mean_mat = jnp.full((d, d), 1.0 / d, F32)
    cols, rows = [], []
    for bb in seqs:
        gates = misc_ref[bb] + gb_ref[...]
        z = jnp.where(is_f, _log_sigmoid(gates), gates)
        cum = jnp.dot(causal.astype(F32), z, precision=HIGHEST, preferred_element_type=F32)
        z = jnp.where(is_f, cum, z)
        cols.append(jnp.dot(z, spread, precision=HIGHEST, preferred_element_type=F32))
        rows.append(_dot_nt(pick, z, precision=HIGHEST))

    q, k, v_aug, i_b, b_b, m_prev, m_t, w_inter, scores, inter = ({} for _ in range(10))
    ones_blk = jnp.ones((L, LANES), BF16)
    for key in heads:
        bb, h = key
        lo, hi = h * d, (h + 1) * d
        q[key] = qk[bb][:, lo:hi].astype(BF16)
        k[key] = qk[bb][:, MIX_WIDTH + lo:MIX_WIDTH + hi] * (d ** -0.5)
        v_aug[key] = jnp.concatenate([v_ref[bb, :, lo:hi], ones_blk], axis=1)
        scores[key] = _dot_nt(q[key], k[key].astype(BF16))
        inter[key] = _dot(q[key], cn_ref[bb, h].astype(BF16))
    intra = {}
    for key in heads:
        bb, h = key
        i_b[key] = cols[bb][:, h * LANES:(h + 1) * LANES]
        b_b[key] = cols[bb][:, (MLSTM_HEADS + h) * LANES:(MLSTM_HEADS + h + 1) * LANES]
        i_row = rows[bb][h:h + 1, :]
        b_row = rows[bb][MLSTM_HEADS + h:MLSTM_HEADS + h + 1, :]
        m_prev[key] = m_ref[bb, h:h + 1, :]
        intra[key] = jnp.where(causal, b_b[key] - b_row + i_row, NEG_INF)
    for key in heads:
        m_inter = b_b[key] + m_prev[key]
        m_t[key] = jnp.maximum(m_inter, jnp.max(intra[key], axis=1, keepdims=True))
        w_inter[key] = jnp.exp(m_inter - m_t[key])
    intra_o = {}
    for key in heads:
        a = jnp.exp(intra[key] - m_t[key]) * scores[key]
        intra_o[key] = _dot(a.astype(BF16), v_aug[key])
    for key in heads:
        bb, h = key
        lo, hi = h * d, (h + 1) * d
        num = w_inter[key] * inter[key][:, :d] + intra_o[key][:, :d]
        den = w_inter[key] * inter[key][:, d:] + intra_o[key][:, d:]
        hh = num / jnp.maximum(jnp.abs(den), jnp.exp(-m_t[key]))
        ms = jnp.dot(hh * hh, mean_mat, precision=HIGHEST, preferred_element_type=F32)
        hh = hh * lax.rsqrt(ms + NORM_EPS) * hn_ref[:, lo:hi]
        o_ref[bb, :, lo:hi] = (hh * _sigmoid(op_ref[bb, :, lo:hi])).astype(o_ref.dtype)
    for key in heads:
        bb, h = key
        b_end = b_b[key][L - 1:L, :]
        g = b_end - b_b[key] + i_b[key]
        m_new = jnp.maximum(b_end + m_prev[key], jnp.max(g, axis=0, keepdims=True))
        decay = jnp.exp(b_end + m_prev[key] - m_new)
        kw = k[key] * jnp.exp(g - m_new)
        cn_ref[bb, h] = (jnp.concatenate([decay, decay], axis=1) * cn_ref[bb, h]
                         + _dot(kw.T.astype(BF16), v_aug[key]))
        m_ref[bb, h:h + 1, :] = m_new


def _mlstm(qk_raw, vc, o_pre, misc, conv_w, b_i, b_f, head_norm):
    B, S, _ = qk_raw.shape
    L = MLSTM_CHUNK
    nb = MLSTM_BATCH
    gb = jnp.zeros((1, LANES), F32).at[0, MISC_I:MISC_I + MLSTM_HEADS].set(b_i)
    gb = gb.at[0, MISC_F:MISC_F + MLSTM_HEADS].set(b_f)
    row = lambda b, c: (b, c, 0)
    const = lambda b, c: (0, 0)
    return pl.pallas_call(
        _mlstm_kernel,
        grid=(B // nb, S // L),
        in_specs=[
            pl.BlockSpec((nb, L, 2 * MIX_WIDTH), row),
            pl.BlockSpec((nb, L, MIX_WIDTH), row),
            pl.BlockSpec((nb, L, MIX_WIDTH), row),
            pl.BlockSpec((nb, L, LANES), row),
            pl.BlockSpec((CONV_WIDTH, 2 * MIX_WIDTH), const),
            pl.BlockSpec((1, LANES), const),
            pl.BlockSpec((1, MIX_WIDTH), const),
        ],
        out_specs=pl.BlockSpec((nb, L, MIX_WIDTH), row),
        out_shape=jax.ShapeDtypeStruct((B, S, MIX_WIDTH), BF16),
        scratch_shapes=[
            pltpu.VMEM((nb, L, 2 * MIX_WIDTH), F32),
            pltpu.VMEM((nb, MLSTM_HEADS, MLSTM_HEAD_DIM, 2 * LANES), F32),
            pltpu.VMEM((nb, 8, LANES), F32),
        ],
        compiler_params=_params(("parallel", "arbitrary")),
        name="mlstm",
    )(qk_raw, vc, o_pre, misc, conv_w, gb, head_norm.reshape(1, MIX_WIDTH))


def _even_layer(h, norm_mix, w_in, pool_w, pool_scale, w_out, norm_ffn, wg, wu, wd):
    B, S, D = h.shape
    slopes = (2.0 ** (-8.0 * np.arange(1, MOBA_HEADS + 1) / MOBA_HEADS)).astype(np.float32)
    ka, ub, kmean, qT, vT = _even_in(h, norm_mix, w_in, slopes)
    nb = S // MOBA_BLOCK
    kmean = kmean.reshape(B, nb, MOBA_HEADS, LANES)[..., :MOBA_HEAD_DIM].transpose(0, 2, 1, 3)
    qaT = _moba_gate(kmean, qT)
    aT = _attention(qaT, ka, vT, MOBA_HEAD_DIM, MOBA_HEADS, "moba_attn")
    b_out = _pool(ub, pool_w, pool_scale)
    h = _mix_out(h, aT, b_out, w_out, a_first=True)
    return _ffn(h.reshape(B * S, D), norm_ffn, wg.astype(BF16), wu.astype(BF16), wd.astype(BF16))


def _odd_layer(h, norm_mix, w_in, conv_w, b_i, b_f, head_norm, q_norm, w_uq, kv_norm, w_ukv,
               w_out, norm_ffn, router_w, router_b, wg, wu, wd):
    B, S, D = h.shape
    qk_raw, vc, o_pre, misc, mqT, mk, mvT = _odd_in(h, norm_mix, w_in, q_norm, w_uq, kv_norm, w_ukv)
    c_out = _mlstm(qk_raw, vc, o_pre, misc, conv_w, b_i, b_f, head_norm)
    dT = _attention(mqT, mk, mvT, MLA_V_DIM, MLA_HEADS, "mla_attn")
    h = _mix_out(h, dT, c_out, w_out, a_first=False)
    return _moe(h.reshape(B * S, D), norm_ffn, router_w, router_b,
                wg.astype(BF16), wu.astype(BF16), wd.astype(BF16))


def kernel(x, p, ev_norm_mix, ev_w_in, pool_w, pool_scale, ev_w_out, ev_norm_ffn, ffn_w_gate, ffn_w_up, ffn_w_down, od_norm_mix, od_w_in, conv_w, gate_b_i, gate_b_f, mlstm_norm, mla_q_norm, mla_w_uq, mla_kv_norm, mla_w_ukv, od_w_out, od_norm_ffn, router_w, router_b, moe_w_gate, moe_w_up, moe_w_down, ple_norm, ple_w_gate, ple_w_proj, final_norm):
    B, S, D = x.shape
    depth = p.shape[0]
    assert D == D_MODEL and S % (2 * ROW_TILE) == 0 and B % MLSTM_BATCH == 0
    assert MOBA_TOPK <= S // MOBA_BLOCK <= MOBA_MAX_BLOCKS
    h = x
    for layer in range(depth):
        j = layer // 2
        if layer % 2 == 0:
            h2d = _even_layer(h, ev_norm_mix[j], ev_w_in[j], pool_w[j], pool_scale[j], ev_w_out[j],
                              ev_norm_ffn[j], ffn_w_gate[j], ffn_w_up[j], ffn_w_down[j])
        else:
            h2d = _odd_layer(h, od_norm_mix[j], od_w_in[j], conv_w[j], gate_b_i[j], gate_b_f[j],
                             mlstm_norm[j], mla_q_norm[j], mla_w_uq[j], mla_kv_norm[j], mla_w_ukv[j],
                             od_w_out[j], od_norm_ffn[j], router_w[j], router_b[j],
                             moe_w_gate[j], moe_w_up[j], moe_w_down[j])
        last = layer == depth - 1
        h2d = _ple(h2d, p[layer].reshape(B * S, PLE_DIM), ple_norm[layer], ple_w_gate[layer],
                   ple_w_proj[layer], final_g=final_norm if last else None)
        h = h2d.reshape(B, S, D)
    return h
```

```python
import functools
import math

import numpy as np
import jax
import jax.numpy as jnp
from jax import lax
from jax.experimental import pallas as pl
from jax.experimental.pallas import tpu as pltpu

F32 = jnp.float32
BF16 = jnp.bfloat16
HIGHEST = lax.Precision.HIGHEST

D_MODEL = 1024
PLE_DIM = 256
NORM_EPS = 1e-6
NEG_INF = -1e30

MOBA_HEADS = 8
MOBA_HEAD_DIM = 64
MOBA_BLOCK = 256
MOBA_TOPK = 3
POOL_WINDOWS = (2, 4, 8, 16)
POOL_GROUP_DIM = 128
POOL_HALO = 16
MLSTM_HEADS = 4
MLSTM_HEAD_DIM = 128
MLSTM_CHUNK = 128
MLSTM_BATCH = 2
assert MLSTM_CHUNK == 128
CONV_WIDTH = 4
MLA_HEADS = 4
MLA_Q_RANK = 256
MLA_KV_RANK = 128
MLA_NOPE_DIM = 64
MLA_ROPE_DIM = 32
MLA_V_DIM = 128
ROPE_BASE = 10000.0
FFN_DIM = 2816
N_EXPERTS = 8
EXPERT_DIM = 3584
MIX_WIDTH = 512

ATTN_BLOCK = 512
ROW_TILE = 512
LANES = 128
VMEM_LIMIT = 56 * 1024 * 1024

MISC_ROPE = 0
MISC_I = 32
MISC_F = 36


def _params(sem, vmem=VMEM_LIMIT):
    return pltpu.CompilerParams(dimension_semantics=sem, vmem_limit_bytes=vmem)


def _rms(x, g):
    ms = jnp.mean(x * x, axis=-1, keepdims=True)
    return x * lax.rsqrt(ms + NORM_EPS) * g


def _sigmoid(x):
    return 1.0 / (1.0 + jnp.exp(-x))


def _dot(a, b):
    return jnp.dot(a, b, preferred_element_type=F32)


def _dot_nt(a, b, precision=None):
    return lax.dot_general(a, b, (((1,), (1,)), ((), ())), precision=precision,
                           preferred_element_type=F32)


KAUG_SEL = MOBA_HEAD_DIM
KAUG_POS = KAUG_SEL + 32
MOBA_MAX_BLOCKS = KAUG_POS - KAUG_SEL


def _bf16_terms(x, n):
    out = []
    for _ in range(n):
        bits = np.float32(x).view(np.uint32)
        kept = np.uint32((int(bits) + 0x7FFF + ((int(bits) >> 16) & 1)) & 0xFFFF0000)
        term = float(kept.view(np.float32))
        out.append(term)
        x -= term
    return tuple(out)


LOG2E = math.log2(math.e)
LOG2E_TERMS = _bf16_terms(LOG2E, 3)


def _even_in_kernel(x_ref, g_ref, wn_ref, wqT_ref, wvT_ref, ext_ref,
                    ka_ref, ub_ref, km_ref, kn_ref, qT_ref, vT_ref):
    tm = x_ref.shape[1]
    xn = _rms(x_ref[0], g_ref[...]).astype(BF16)
    n = _dot(xn, wn_ref[...])
    ka = n[:, :MOBA_HEADS * LANES]
    ka_ref[0] = (ka + ext_ref[...].astype(F32)).astype(BF16)
    ub_ref[0] = n[:, MOBA_HEADS * LANES:]
    for j in range(tm // MOBA_BLOCK):
        km_ref[0, j] = jnp.mean(ka[j * MOBA_BLOCK:(j + 1) * MOBA_BLOCK], axis=0, keepdims=True)
    slot = lax.broadcasted_iota(jnp.int32, (MOBA_HEADS * LANES, LANES), 0) // LANES
    head = lax.broadcasted_iota(jnp.int32, (MOBA_HEADS * LANES, LANES), 1)
    kn_ref[0] = jnp.dot(ka * ka, (slot == head).astype(F32), precision=HIGHEST,
                        preferred_element_type=F32)
    qT_ref[0] = _dot_nt(wqT_ref[...], xn)
    vT = _dot_nt(wvT_ref[...], xn)
    for j in range(tm // ATTN_BLOCK):
        vT_ref[0, j] = vT[:, j * ATTN_BLOCK:(j + 1) * ATTN_BLOCK].astype(BF16)


def _moba_key_extras(S, slopes):
    pos = np.arange(S)
    blk, off = pos // MOBA_BLOCK, pos % MOBA_BLOCK
    ext = np.zeros((S, MOBA_HEADS, LANES), np.float32)
    ext[pos, :, KAUG_SEL + blk] = 1.0
    for term in range(len(LOG2E_TERMS)):
        ext[:, :, KAUG_POS + 2 * term] = slopes[None, :] * (MOBA_BLOCK * blk)[:, None]
        ext[:, :, KAUG_POS + 2 * term + 1] = slopes[None, :] * off[:, None]
    return jnp.asarray(ext.reshape(S, MOBA_HEADS * LANES), dtype=BF16)


def _even_in(x, g, w_in, slopes):
    B, S, D = x.shape
    tm = ROW_TILE
    nb = S // MOBA_BLOCK
    wq, wk, wv, wu = (w_in[:, i * MIX_WIDTH:(i + 1) * MIX_WIDTH] for i in range(4))
    wk_slots = jnp.concatenate(
        [wk.reshape(D, MOBA_HEADS, MOBA_HEAD_DIM),
         jnp.zeros((D, MOBA_HEADS, LANES - MOBA_HEAD_DIM), F32)], axis=2).reshape(D, MOBA_HEADS * LANES)
    wn = jnp.concatenate([wk_slots, wu], axis=1).astype(BF16)
    wqT = wq.T.astype(BF16)
    wvT = wv.T.astype(BF16)
    const = lambda b, i: (0, 0)
    return pl.pallas_call(
        _even_in_kernel,
        grid=(B, S // tm),
        in_specs=[
            pl.BlockSpec((1, tm, D), lambda b, i: (b, i, 0)),
            pl.BlockSpec((1, D), const),
            pl.BlockSpec((D, MOBA_HEADS * LANES + MIX_WIDTH), const),
            pl.BlockSpec((MIX_WIDTH, D), const),
            pl.BlockSpec((MIX_WIDTH, D), const),
            pl.BlockSpec((tm, MOBA_HEADS * LANES), lambda b, i: (i, 0)),
        ],
        out_specs=[
            pl.BlockSpec((1, tm, MOBA_HEADS * LANES), lambda b, i: (b, i, 0)),
            pl.BlockSpec((1, tm, MIX_WIDTH), lambda b, i: (b, i, 0)),
            pl.BlockSpec((1, tm // MOBA_BLOCK, 1, MOBA_HEADS * LANES), lambda b, i: (b, i, 0, 0)),
            pl.BlockSpec((1, tm, LANES), lambda b, i: (b, i, 0)),
            pl.BlockSpec((1, MIX_WIDTH, tm), lambda b, i: (b, 0, i)),
            pl.BlockSpec((1, tm // ATTN_BLOCK, MIX_WIDTH, ATTN_BLOCK), lambda b, i: (b, i, 0, 0)),
        ],
        out_shape=[
            jax.ShapeDtypeStruct((B, S, MOBA_HEADS * LANES), BF16),
            jax.ShapeDtypeStruct((B, S, MIX_WIDTH), F32),
            jax.ShapeDtypeStruct((B, nb, 1, MOBA_HEADS * LANES), F32),
            jax.ShapeDtypeStruct((B, S, LANES), F32),
            jax.ShapeDtypeStruct((B, MIX_WIDTH, S), F32),
            jax.ShapeDtypeStruct((B, S // ATTN_BLOCK, MIX_WIDTH, ATTN_BLOCK), BF16),
        ],
        compiler_params=_params(("parallel", "parallel")),
        name="even_in",
    )(x, g.reshape(1, D), wn, wqT, wvT, _moba_key_extras(S, slopes))


def _moba_gate_kernel(km_ref, qT_ref, qa_ref, qn_ref):
    i = pl.program_id(1)
    nb = km_ref.shape[2]
    tq = qT_ref.shape[2]
    row = lax.broadcasted_iota(jnp.int32, (nb, tq), 0)
    own = (i * tq + lax.broadcasted_iota(jnp.int32, (nb, tq), 1)) // MOBA_BLOCK
    past = row < own
    tail_row = lax.broadcasted_iota(jnp.int32, (LANES - KAUG_POS, tq), 0)
    tail = jnp.zeros(tail_row.shape, F32)
    for term, value in enumerate(LOG2E_TERMS):
        tail = jnp.where(tail_row // 2 == term, F32(value), tail)
    tail = tail.astype(BF16)
    pad = jnp.zeros((MOBA_MAX_BLOCKS - nb, tq), BF16) if nb < MOBA_MAX_BLOCKS else None
    for h in range(MOBA_HEADS):
        q_h = qT_ref[0, h * MOBA_HEAD_DIM:(h + 1) * MOBA_HEAD_DIM, :]
        qn_ref[0, h:h + 1, :] = jnp.sum(q_h * q_h, axis=0, keepdims=True)
        gate = jnp.dot(km_ref[0, h], q_h, precision=HIGHEST, preferred_element_type=F32)
        gate = jnp.where(past, gate, NEG_INF)
        chosen = jnp.zeros(gate.shape, F32)
        for _ in range(MOBA_TOPK):
            mx = jnp.max(gate, axis=0, keepdims=True)
            first = jnp.min(jnp.where(gate == mx, row, nb), axis=0, keepdims=True)
            pick = row == first
            chosen = jnp.where(pick, 1.0, chosen)
            gate = jnp.where(pick, -jnp.inf, gate)
        keep = jnp.where(past, chosen, (row == own).astype(F32))
        sel = jnp.where(keep > 0.0, 0.0, NEG_INF).astype(BF16)
        base = h * LANES
        qa_ref[0, base:base + KAUG_SEL] = (q_h * (MOBA_HEAD_DIM ** -0.5 * LOG2E)).astype(BF16)
        qa_ref[0, base + KAUG_SEL:base + KAUG_SEL + nb] = sel
        if pad is not None:
            qa_ref[0, base + KAUG_SEL + nb:base + KAUG_POS] = pad
        qa_ref[0, base + KAUG_POS:base + LANES] = tail


def _moba_gate(kmean, qT):
    B, H, nb, dh = kmean.shape
    S = qT.shape[2]
    tq = ATTN_BLOCK
    return pl.pallas_call(
        _moba_gate_kernel,
        grid=(B, S // tq),
        in_specs=[
            pl.BlockSpec((1, H, nb, dh), lambda b, i: (b, 0, 0, 0)),
            pl.BlockSpec((1, H * dh, tq), lambda b, i: (b, 0, i)),
        ],
        out_specs=[pl.BlockSpec((1, H * LANES, tq), lambda b, i: (b, 0, i)),
                   pl.BlockSpec((1, H, tq), lambda b, i: (b, 0, i))],
        out_shape=[jax.ShapeDtypeStruct((B, H * LANES, S), BF16), jax.ShapeDtypeStruct((B, H, S), F32)],
        compiler_params=_params(("parallel", "parallel")),
        name="moba_gate",
    )(kmean, qT)


HEADS_PER_STEP = 4
SUM_ROWS = 16


def _attn_kernel(lo_ref, q_ref, k_ref, v_ref, o_ref, sa_ref, sb_ref, m_ref, acc_ref):
    i = pl.program_id(2)
    lo = lo_ref[(pl.program_id(0) * pl.num_programs(1) + pl.program_id(1)) * pl.num_programs(2) + i]
    n_past = i - lo
    tq = q_ref.shape[2]
    tk = ATTN_BLOCK
    hp = HEADS_PER_STEP
    dv = v_ref.shape[2] // hp
    m_ref[...] = jnp.full(m_ref.shape, NEG_INF, F32)
    acc_ref[...] = jnp.zeros(acc_ref.shape, F32)
    ones_rows = jnp.ones((SUM_ROWS, tk), BF16)

    def scores(kvt, s_ref, diag):
        start = pl.multiple_of(kvt * tk, tk)
        k_tile = k_ref[0, pl.ds(start, tk), :]
        for g in range(hp):
            s = _dot(k_tile[:, g * LANES:(g + 1) * LANES], q_ref[0, g * LANES:(g + 1) * LANES, :])
            if diag:
                key = lax.broadcasted_iota(jnp.int32, (tk, tq), 0)
                qry = lax.broadcasted_iota(jnp.int32, (tk, tq), 1)
                s = jnp.where(key <= qry, s, NEG_INF)
            s_ref[g] = s

    def consume(kvt, s_ref):
        v_tile = v_ref[0, kvt]
        for g in range(hp):
            s = s_ref[g]
            m_run = m_ref[g]
            m_new = jnp.maximum(m_run, jnp.max(s, axis=0, keepdims=True))
            p = jnp.exp2(s - m_new).astype(BF16)
            v_aug = jnp.concatenate([v_tile[g * dv:(g + 1) * dv], ones_rows], axis=0)
            acc_ref[g] = jnp.exp2(m_run - m_new) * acc_ref[g] + _dot(v_aug, p)
            m_ref[g] = m_new

    tile_at = lambda t: jnp.where(t == 0, i, lo + t - 1)
    scores(i, sa_ref, True)

    def pair(p, carry):
        t = 2 * p
        scores(tile_at(t + 1), sb_ref, False)
        consume(tile_at(t), sa_ref)
        scores(tile_at(t + 2), sa_ref, False)
        consume(tile_at(t + 1), sb_ref)
        return carry

    lax.fori_loop(0, n_past // 2, pair, 0)
    last = 2 * (n_past // 2)

    @pl.when(n_past % 2 == 1)
    def _():
        scores(tile_at(last + 1), sb_ref, False)
        consume(tile_at(last), sa_ref)
        consume(tile_at(last + 1), sb_ref)

    @pl.when(n_past % 2 == 0)
    def _():
        consume(tile_at(last), sa_ref)

    for g in range(hp):
        o_ref[0, g * dv:(g + 1) * dv, :] = acc_ref[g, :dv] / acc_ref[g, dv:dv + 1]


def _attention(qT, k, vT, dv, heads, name, first_tile=None):
    B, _, S = qT.shape
    tq = ATTN_BLOCK
    nt = S // ATTN_BLOCK
    hp = HEADS_PER_STEP
    if first_tile is None:
        first_tile = jnp.zeros((B, heads // hp, nt), jnp.int32)
    return pl.pallas_call(
        _attn_kernel,
        grid_spec=pltpu.PrefetchScalarGridSpec(
            num_scalar_prefetch=1,
            grid=(B, heads // hp, S // tq),
            in_specs=[
                pl.BlockSpec((1, hp * LANES, tq), lambda b, h, i, lo: (b, h, i)),
                pl.BlockSpec((1, S, hp * LANES), lambda b, h, i, lo: (b, 0, h)),
                pl.BlockSpec((1, nt, hp * dv, ATTN_BLOCK), lambda b, h, i, lo: (b, 0, h, 0)),
            ],
            out_specs=pl.BlockSpec((1, hp * dv, tq), lambda b, h, i, lo: (b, h, i)),
            scratch_shapes=[pltpu.VMEM((hp, ATTN_BLOCK, tq), F32), pltpu.VMEM((hp, ATTN_BLOCK, tq), F32),
                            pltpu.VMEM((hp, 1, tq), F32), pltpu.VMEM((hp, dv + SUM_ROWS, tq), F32)],
        ),
        out_shape=jax.ShapeDtypeStruct((B, heads * dv, S), F32),
        compiler_params=_params(("parallel", "parallel", "arbitrary")),
        name=name,
    )(first_tile.reshape(-1), qT, k, vT)


def _pool_kernel(x_ref, halo_ref, w_ref, sc_ref, o_ref, xs_ref):
    i = pl.program_id(1)
    tm = x_ref.shape[1]
    x = x_ref[0]
    xs_ref[0:POOL_HALO] = jnp.where(i > 0, halo_ref[0], 0.0)
    xs_ref[POOL_HALO:POOL_HALO + tm] = x
    t = i * tm + lax.broadcasted_iota(jnp.int32, (tm, 1), 0)
    outs = []
    for g, win in enumerate(POOL_WINDOWS):
        lo, hi = g * POOL_GROUP_DIM, (g + 1) * POOL_GROUP_DIM
        xg = x[:, lo:hi]
        acc = xg
        for d in range(1, win):
            acc = acc + xs_ref[POOL_HALO - d:POOL_HALO - d + tm, lo:hi]
        count = jnp.minimum(t + 1, win).astype(F32)
        outs.append(_dot((acc / count - xg).astype(BF16), w_ref[g]))
    o_ref[0] = (jnp.concatenate(outs, axis=1) * sc_ref[...]).astype(o_ref.dtype)


def _pool(ub, pool_w, pool_scale):
    B, S, W = ub.shape
    tm = ROW_TILE
    per = tm // POOL_HALO
    return pl.pallas_call(
        _pool_kernel,
        grid=(B, S // tm),
        in_specs=[
            pl.BlockSpec((1, tm, W), lambda b, i: (b, i, 0)),
            pl.BlockSpec((1, POOL_HALO, W), lambda b, i: (b, jnp.maximum(i * per - 1, 0), 0)),
            pl.BlockSpec(pool_w.shape, lambda b, i: (0, 0, 0)),
            pl.BlockSpec((1, W), lambda b, i: (0, 0)),
        ],
        out_specs=pl.BlockSpec((1, tm, W), lambda b, i: (b, i, 0)),
        out_shape=jax.ShapeDtypeStruct((B, S, W), BF16),
        scratch_shapes=[pltpu.VMEM((POOL_HALO + tm, W), F32)],
        compiler_params=_params(("parallel", "parallel")),
        name="pool",
    )(ub, ub, pool_w.astype(BF16), pool_scale.reshape(1, W))


def _mix_out_kernel(h_ref, aT_ref, b_ref, w_ref, o_ref, *, a_first):
    a = aT_ref[0].T.astype(BF16)
    b = b_ref[0].astype(BF16)
    lo, hi = (a, b) if a_first else (b, a)
    y = _dot(lo, w_ref[:MIX_WIDTH]) + _dot(hi, w_ref[MIX_WIDTH:])
    o_ref[0] = h_ref[0] + y


def _mix_out(h, aT, b, w_out, a_first):
    B, S, D = h.shape
    tm = ROW_TILE
    return pl.pallas_call(
        functools.partial(_mix_out_kernel, a_first=a_first),
        grid=(B, S // tm),
        in_specs=[
            pl.BlockSpec((1, tm, D), lambda b_, i: (b_, i, 0)),
            pl.BlockSpec((1, MIX_WIDTH, tm), lambda b_, i: (b_, 0, i)),
            pl.BlockSpec((1, tm, MIX_WIDTH), lambda b_, i: (b_, i, 0)),
            pl.BlockSpec((2 * MIX_WIDTH, D), lambda b_, i: (0, 0)),
        ],
        out_specs=pl.BlockSpec((1, tm, D), lambda b_, i: (b_, i, 0)),
        out_shape=jax.ShapeDtypeStruct((B, S, D), F32),
        compiler_params=_params(("parallel", "parallel")),
        name="mix_out",
    )(h, aT, b, w_out.astype(BF16))


def _swiglu_step(xn, wg, wu, wd):
    gt = _dot(xn, wg)
    up = _dot(xn, wu)
    return _dot((gt * _sigmoid(gt) * up).astype(BF16), wd)


def _ffn_kernel(h_ref, g_ref, wg_ref, wu_ref, wd_ref, o_ref, xn_ref, acc_ref):
    f = pl.program_id(1)

    @pl.when(f == 0)
    def _():
        xn_ref[...] = _rms(h_ref[...], g_ref[...]).astype(BF16)
        acc_ref[...] = jnp.zeros_like(acc_ref)

    acc_ref[...] += _swiglu_step(xn_ref[...], wg_ref[...], wu_ref[...], wd_ref[...])

    @pl.when(f == pl.num_programs(1) - 1)
    def _():
        o_ref[...] = h_ref[...] + acc_ref[...]


def _ffn(h2d, g, wg, wu, wd, tm=ROW_TILE, tf=FFN_DIM // 2):
    T, D = h2d.shape
    F = wg.shape[1]
    return pl.pallas_call(
        _ffn_kernel,
        grid=(T // tm, F // tf),
        in_specs=[
            pl.BlockSpec((tm, D), lambda i, f: (i, 0)),
            pl.BlockSpec((1, D), lambda i, f: (0, 0)),
            pl.BlockSpec((D, tf), lambda i, f: (0, f)),
            pl.BlockSpec((D, tf), lambda i, f: (0, f)),
            pl.BlockSpec((tf, D), lambda i, f: (f, 0)),
        ],
        out_specs=pl.BlockSpec((tm, D), lambda i, f: (i, 0)),
        out_shape=jax.ShapeDtypeStruct((T, D), F32),
        scratch_shapes=[pltpu.VMEM((tm, D), BF16), pltpu.VMEM((tm, D), F32)],
        compiler_params=_params(("parallel", "arbitrary")),
        name="ffn",
    )(h2d, g.reshape(1, D), wg, wu, wd)


ROUTE_E0, ROUTE_E1, ROUTE_W0, ROUTE_W1, ROUTE_R0, ROUTE_R1 = range(6)
MOE_TILE = 512
MOE_TF = EXPERT_DIM // 2


def _lane_pick(tile, lane, idx):
    return jnp.sum(jnp.where(lane == idx, tile, 0.0), axis=1, keepdims=True)


def _router_kernel(h_ref, g_ref, w_ref, b_ref, route_ref, cnt_ref):
    tm = h_ref.shape[0]

    @pl.when(pl.program_id(0) == 0)
    def _():
        cnt_ref[...] = jnp.zeros_like(cnt_ref)

    xn = _rms(h_ref[...], g_ref[...])
    logits = jnp.dot(xn, w_ref[...], precision=HIGHEST, preferred_element_type=F32) + b_ref[...]
    lane = lax.broadcasted_iota(jnp.int32, logits.shape, 1)
    logits = jnp.where(lane < N_EXPERTS, logits, -jnp.inf)
    v0 = jnp.max(logits, axis=1, keepdims=True)
    i0 = jnp.min(jnp.where(logits == v0, lane, LANES), axis=1, keepdims=True)
    rest = jnp.where(lane == i0, -jnp.inf, logits)
    v1 = jnp.max(rest, axis=1, keepdims=True)
    i1 = jnp.min(jnp.where(rest == v1, lane, LANES), axis=1, keepdims=True)
    e1 = jnp.exp(v1 - v0)
    w0 = 1.0 / (1.0 + e1)
    sel = (lane == i0).astype(F32) + (lane == i1).astype(F32)
    earlier = (lax.broadcasted_iota(jnp.int32, (tm, tm), 1)
               < lax.broadcasted_iota(jnp.int32, (tm, tm), 0))
    rank = _dot(earlier.astype(BF16), sel.astype(BF16)) + cnt_ref[...]
    cnt_ref[...] += jnp.sum(sel, axis=0, keepdims=True)
    cols = ((ROUTE_E0, i0.astype(F32)), (ROUTE_E1, i1.astype(F32)), (ROUTE_W0, w0), (ROUTE_W1, e1 * w0),
            (ROUTE_R0, _lane_pick(rank, lane, i0)), (ROUTE_R1, _lane_pick(rank, lane, i1)))
    route = jnp.zeros(logits.shape, F32)
    for c, val in cols:
        route = jnp.where(lane == c, val, route)
    route_ref[...] = route


def _router(h2d, g, router_w, router_b):
    T, D = h2d.shape
    tm = ROW_TILE
    w = jnp.zeros((D, LANES), F32).at[:, :N_EXPERTS].set(router_w)
    b = jnp.zeros((1, LANES), F32).at[0, :N_EXPERTS].set(router_b)
    return pl.pallas_call(
        _router_kernel,
        grid=(T // tm,),
        in_specs=[
            pl.BlockSpec((tm, D), lambda i: (i, 0)),
            pl.BlockSpec((1, D), lambda i: (0, 0)),
            pl.BlockSpec((D, LANES), lambda i: (0, 0)),
            pl.BlockSpec((1, LANES), lambda i: (0, 0)),
        ],
        out_specs=[pl.BlockSpec((tm, LANES), lambda i: (i, 0)),
                   pl.BlockSpec((1, LANES), lambda i: (0, 0))],
        out_shape=[jax.ShapeDtypeStruct((T, LANES), F32), jax.ShapeDtypeStruct((1, LANES), F32)],
        compiler_params=_params(("arbitrary",)),
        name="router",
    )(h2d, g.reshape(1, D), w, b)


def _row_copies(pos_ref, base, r, src_of, dst_of, sem):
    return [pltpu.make_async_copy(src_of(k, pos_ref[base + 2 * r + k]),
                                  dst_of(k, pos_ref[base + 2 * r + k]), sem) for k in range(2)]


def _all_rows(tm, make):
    def issue(r, c):
        for k, cp in enumerate(make(r)):
            cp.start(priority=k)
        return c

    def drain(r, c):
        for cp in make(r):
            cp.wait()
        return c

    lax.fori_loop(0, tm, issue, 0, unroll=8)
    lax.fori_loop(0, tm, drain, 0, unroll=8)


def _dispatch_kernel(pos_ref, h_ref, g_ref, init_ref, xs_ref, xn_ref, sem):
    del init_ref
    tm = h_ref.shape[0]
    xn_ref[...] = _rms(h_ref[...], g_ref[...])
    base = pl.program_id(0) * (2 * tm)
    _all_rows(tm, lambda r: _row_copies(
        pos_ref, base, r, lambda k, p: xn_ref.at[pl.ds(r, 1)], lambda k, p: xs_ref.at[pl.ds(p, 1)], sem))


def _dispatch(pos, h2d, g, n_rows):
    T, D = h2d.shape
    tm = ROW_TILE
    return pl.pallas_call(
        _dispatch_kernel,
        grid_spec=pltpu.PrefetchScalarGridSpec(
            num_scalar_prefetch=1,
            grid=(T // tm,),
            in_specs=[pl.BlockSpec((tm, D), lambda i, pos: (i, 0)),
                      pl.BlockSpec((1, D), lambda i, pos: (0, 0)),
                      pl.BlockSpec(memory_space=pl.ANY)],
            out_specs=pl.BlockSpec(memory_space=pl.ANY),
            scratch_shapes=[pltpu.VMEM((tm, D), F32), pltpu.SemaphoreType.DMA(())],
        ),
        out_shape=jax.ShapeDtypeStruct((n_rows, D), F32),
        input_output_aliases={3: 0},
        compiler_params=_params(("arbitrary",)),
        name="moe_dispatch",
    )(pos, h2d, g.reshape(1, D), jnp.zeros((n_rows, D), F32))


def _moe_ffn_kernel(te_ref, nv_ref, x_ref, wg_ref, wu_ref, wd_ref, o_ref, xb_ref, acc_ref):
    j = pl.program_id(0)
    f = pl.program_id(1)
    valid = j < nv_ref[0]

    @pl.when(f == 0)
    def _():
        xb_ref[...] = x_ref[...].astype(BF16)
        acc_ref[...] = jnp.zeros_like(acc_ref)

    @pl.when(valid)
    def _():
        acc_ref[...] += _swiglu_step(xb_ref[...], wg_ref[0], wu_ref[0], wd_ref[0])

    @pl.when(f == pl.num_programs(1) - 1)
    def _():
        o_ref[...] = acc_ref[...]


def _moe_ffn(tile_expert, n_valid, xs, wg, wu, wd):
    N, D = xs.shape
    tm, tf = MOE_TILE, MOE_TF
    F = wg.shape[2]
    return pl.pallas_call(
        _moe_ffn_kernel,
        grid_spec=pltpu.PrefetchScalarGridSpec(
            num_scalar_prefetch=2,
            grid=(N // tm, F // tf),
            in_specs=[pl.BlockSpec((tm, D), lambda j, f, te, nv: (j, 0)),
                      pl.BlockSpec((1, D, tf), lambda j, f, te, nv: (te[j], 0, f)),
                      pl.BlockSpec((1, D, tf), lambda j, f, te, nv: (te[j], 0, f)),
                      pl.BlockSpec((1, tf, D), lambda j, f, te, nv: (te[j], f, 0))],
            out_specs=pl.BlockSpec((tm, D), lambda j, f, te, nv: (j, 0)),
            scratch_shapes=[pltpu.VMEM((tm, D), BF16), pltpu.VMEM((tm, D), F32)],
        ),
        out_shape=jax.ShapeDtypeStruct((N, D), F32),
        compiler_params=_params(("arbitrary", "arbitrary")),
        name="moe_ffn",
    )(tile_expert, n_valid, xs, wg, wu, wd)


def _combine_kernel(pos_ref, h_ref, route_ref, ys_ref, o_ref, y_ref, sem):
    tm = h_ref.shape[0]
    base = pl.program_id(0) * (2 * tm)
    _all_rows(tm, lambda r: _row_copies(
        pos_ref, base, r, lambda k, p: ys_ref.at[pl.ds(p, 1)], lambda k, p: y_ref.at[k, pl.ds(r, 1)], sem))
    route = route_ref[...]
    lane = lax.broadcasted_iota(jnp.int32, route.shape, 1)
    o_ref[...] = (h_ref[...] + _lane_pick(route, lane, ROUTE_W0) * y_ref[0]
                  + _lane_pick(route, lane, ROUTE_W1) * y_ref[1])


def _combine(pos, h2d, route, ys):
    T, D = h2d.shape
    tm = ROW_TILE
    return pl.pallas_call(
        _combine_kernel,
        grid_spec=pltpu.PrefetchScalarGridSpec(
            num_scalar_prefetch=1,
            grid=(T // tm,),
            in_specs=[pl.BlockSpec((tm, D), lambda i, pos: (i, 0)),
                      pl.BlockSpec((tm, LANES), lambda i, pos: (i, 0)),
                      pl.BlockSpec(memory_space=pl.ANY)],
            out_specs=pl.BlockSpec((tm, D), lambda i, pos: (i, 0)),
            scratch_shapes=[pltpu.VMEM((2, tm, D), F32), pltpu.SemaphoreType.DMA(())],
        ),
        out_shape=jax.ShapeDtypeStruct((T, D), F32),
        compiler_params=_params(("arbitrary",)),
        name="moe_combine",
    )(pos, h2d, route, ys)


def _moe(h2d, g, router_w, router_b, wg, wu, wd):
    T, D = h2d.shape
    tm = MOE_TILE
    route, counts = _router(h2d, g, router_w, router_b)
    cnt = counts[0, :N_EXPERTS].astype(jnp.int32)
    padded = (cnt + tm - 1) // tm * tm
    ends = jnp.cumsum(padded)
    start = ends - padded
    e01 = route[:, ROUTE_E0:ROUTE_E1 + 1].astype(jnp.int32)
    r01 = route[:, ROUTE_R0:ROUTE_R1 + 1].astype(jnp.int32)
    pos = (start[e01] + r01).reshape(2 * T)
    n_rows = 2 * T + N_EXPERTS * tm
    tile_row = jnp.arange(n_rows // tm, dtype=jnp.int32) * tm
    tile_expert = jnp.minimum(jnp.sum(tile_row[:, None] >= ends[None, :], axis=1), N_EXPERTS - 1).astype(jnp.int32)
    n_valid = (ends[-1:] // tm).astype(jnp.int32)
    xs = _dispatch(pos, h2d, g, n_rows)
    ys = _moe_ffn(tile_expert, n_valid, xs, wg, wu, wd)
    return _combine(pos, h2d, route, ys)


def _ple_kernel(*refs, final):
    if final:
        h_ref, p_ref, g_ref, wg_ref, wp_ref, fg_ref, o_ref = refs
    else:
        h_ref, p_ref, g_ref, wg_ref, wp_ref, o_ref = refs
    h = h_ref[...]
    gate = _sigmoid(_dot(_rms(h, g_ref[...]).astype(BF16), wg_ref[...]))
    out = h + gate * _dot(p_ref[...].astype(BF16), wp_ref[...])
    if final:
        out = _rms(out, fg_ref[...])
    o_ref[...] = out


def _ple(h2d, p2d, g, w_gate, w_proj, final_g=None):
    T, D = h2d.shape
    tm = ROW_TILE
    final = final_g is not None
    in_specs = [
        pl.BlockSpec((tm, D), lambda i: (i, 0)),
        pl.BlockSpec((tm, PLE_DIM), lambda i: (i, 0)),
        pl.BlockSpec((1, D), lambda i: (0, 0)),
        pl.BlockSpec((D, D), lambda i: (0, 0)),
        pl.BlockSpec((PLE_DIM, D), lambda i: (0, 0)),
    ]
    args = [h2d, p2d, g.reshape(1, D), w_gate.astype(BF16), w_proj.astype(BF16)]
    if final:
        in_specs.append(pl.BlockSpec((1, D), lambda i: (0, 0)))
        args.append(final_g.reshape(1, D))
    return pl.pallas_call(
        functools.partial(_ple_kernel, final=final),
        grid=(T // tm,),
        in_specs=in_specs,
        out_specs=pl.BlockSpec((tm, D), lambda i: (i, 0)),
        out_shape=jax.ShapeDtypeStruct((T, D), F32),
        compiler_params=_params(("parallel",)),
        name="ple_final" if final else "ple",
    )(*args)


ODD_MAIN = 4 * MIX_WIDTH
ODD_COLS = ODD_MAIN + MLA_Q_RANK + MLA_KV_RANK + 2 * LANES
MLA_QK_SCALE = (MLA_NOPE_DIM + MLA_ROPE_DIM) ** -0.5 * LOG2E
ROPE_HALF = MLA_ROPE_DIM // 2


def _odd_in_kernel(x_ref, g_ref, wn_ref, qn_ref, wuqT_ref, kvn_ref, wk2_ref, wvT_ref,
                   cosT_ref, sinT_ref, cc_ref, ss_ref,
                   qk_ref, vc_ref, op_ref, misc_ref, mq_ref, mk_ref, mv_ref):
    tm = x_ref.shape[1]
    xn = _rms(x_ref[0], g_ref[...]).astype(BF16)
    u = _dot(xn, wn_ref[...])
    qk_ref[0] = u[:, :2 * MIX_WIDTH]
    vc_ref[0] = u[:, 2 * MIX_WIDTH:3 * MIX_WIDTH].astype(BF16)
    op_ref[0] = u[:, 3 * MIX_WIDTH:ODD_MAIN]
    c0 = ODD_MAIN
    c_q = u[:, c0:c0 + MLA_Q_RANK]
    c0 += MLA_Q_RANK
    c_kv = u[:, c0:c0 + MLA_KV_RANK]
    c0 += MLA_KV_RANK
    misc = u[:, c0:c0 + LANES]
    misc_sw = u[:, c0 + LANES:c0 + 2 * LANES]
    misc_ref[0] = misc
    cqn = _rms(c_q, qn_ref[...]).astype(BF16)
    qT = _dot_nt(wuqT_ref[...], cqn)
    cosT = cosT_ref[...]
    sinT = sinT_ref[...]
    for h in range(MLA_HEADS):
        r = h * LANES
        mq_ref[0, r:r + MLA_NOPE_DIM] = (qT[r:r + MLA_NOPE_DIM] * MLA_QK_SCALE).astype(BF16)
        x1 = qT[r + MLA_NOPE_DIM:r + MLA_NOPE_DIM + ROPE_HALF]
        x2 = qT[r + MLA_NOPE_DIM + ROPE_HALF:r + MLA_NOPE_DIM + MLA_ROPE_DIM]
        mq_ref[0, r + MLA_NOPE_DIM:r + MLA_NOPE_DIM + ROPE_HALF] = (
            (x1 * cosT - x2 * sinT) * MLA_QK_SCALE).astype(BF16)
        mq_ref[0, r + MLA_NOPE_DIM + ROPE_HALF:r + MLA_NOPE_DIM + MLA_ROPE_DIM] = (
            (x1 * sinT + x2 * cosT) * MLA_QK_SCALE).astype(BF16)
        mq_ref[0, r + MLA_NOPE_DIM + MLA_ROPE_DIM:r + LANES] = jnp.zeros(
            (LANES - MLA_NOPE_DIM - MLA_ROPE_DIM, tm), BF16)
    ckvn = _rms(c_kv, kvn_ref[...]).astype(BF16)
    k_rot = (misc * cc_ref[...] + misc_sw * ss_ref[...]).astype(BF16)
    mk_ref[0] = _dot(jnp.concatenate([ckvn, k_rot], axis=1), wk2_ref[...]).astype(BF16)
    vT = _dot_nt(wvT_ref[...], ckvn)
    for j in range(tm // ATTN_BLOCK):
        mv_ref[0, j] = vT[:, j * ATTN_BLOCK:(j + 1) * ATTN_BLOCK].astype(BF16)


def _rope_tables(S):
    inv_freq = ROPE_BASE ** (-jnp.arange(ROPE_HALF, dtype=F32) / ROPE_HALF)
    ang = jnp.arange(S, dtype=F32)[:, None] * inv_freq[None, :]
    cos, sin = jnp.cos(ang), jnp.sin(ang)
    pad = jnp.zeros((S, LANES - MLA_ROPE_DIM), F32)
    cc = jnp.concatenate([cos, cos, pad], axis=1)
    ss = jnp.concatenate([-sin, sin, pad], axis=1)
    return cos.T, sin.T, cc, ss


def _odd_in(x, g, w_in, q_norm, w_uq, kv_norm, w_ukv):
    B, S, D = x.shape
    tm = ROW_TILE
    cuts = np.cumsum([MIX_WIDTH] * 4 + [MLSTM_HEADS, MLSTM_HEADS, MLA_Q_RANK, MLA_KV_RANK]).tolist()
    w_main = w_in[:, :cuts[3]]
    w_i = w_in[:, cuts[3]:cuts[4]]
    w_f = w_in[:, cuts[4]:cuts[5]]
    w_cq = w_in[:, cuts[5]:cuts[6]]
    w_ckv = w_in[:, cuts[6]:cuts[7]]
    w_kr = w_in[:, cuts[7]:]
    w_kr_sw = jnp.concatenate([w_kr[:, ROPE_HALF:], w_kr[:, :ROPE_HALF]], axis=1)
    zpad = lambda n: jnp.zeros((D, n), F32)
    w_misc = jnp.concatenate([w_kr, w_i, w_f, zpad(LANES - MLA_ROPE_DIM - 2 * MLSTM_HEADS)], axis=1)
    w_misc_sw = jnp.concatenate([w_kr_sw, zpad(LANES - MLA_ROPE_DIM)], axis=1)
    wn = jnp.concatenate([w_main, w_cq, w_ckv, w_misc, w_misc_sw], axis=1).astype(BF16)
    qd = MLA_NOPE_DIM + MLA_ROPE_DIM
    w_uq_h = w_uq.reshape(MLA_Q_RANK, MLA_HEADS, qd)
    w_uq_h = jnp.concatenate([w_uq_h, jnp.zeros((MLA_Q_RANK, MLA_HEADS, LANES - qd), F32)], axis=2)
    wuqT = w_uq_h.reshape(MLA_Q_RANK, MLA_HEADS * LANES).T.astype(BF16)
    w_ukv_h = w_ukv.reshape(MLA_KV_RANK, MLA_HEADS, MLA_NOPE_DIM + MLA_V_DIM)
    w_k = jnp.concatenate([w_ukv_h[:, :, :MLA_NOPE_DIM],
                           jnp.zeros((MLA_KV_RANK, MLA_HEADS, LANES - MLA_NOPE_DIM), F32)], axis=2)
    place = jnp.zeros((LANES, MLA_HEADS, LANES), F32)
    eye = jnp.eye(MLA_ROPE_DIM, dtype=F32)
    place = place.at[:MLA_ROPE_DIM, :, MLA_NOPE_DIM:MLA_NOPE_DIM + MLA_ROPE_DIM].set(
        jnp.broadcast_to(eye[:, None, :], (MLA_ROPE_DIM, MLA_HEADS, MLA_ROPE_DIM)))
    wk2 = jnp.concatenate([w_k, place], axis=0).reshape(MLA_KV_RANK + LANES, MLA_HEADS * LANES).astype(BF16)
    wvT = w_ukv_h[:, :, MLA_NOPE_DIM:].reshape(MLA_KV_RANK, MLA_HEADS * MLA_V_DIM).T.astype(BF16)
    cosT, sinT, cc, ss = _rope_tables(S)
    row = lambda b, i: (b, i, 0)
    const = lambda b, i: (0, 0)
    nb = S // ATTN_BLOCK
    return pl.pallas_call(
        _odd_in_kernel,
        grid=(B, S // tm),
        in_specs=[
            pl.BlockSpec((1, tm, D), row),
            pl.BlockSpec((1, D), const),
            pl.BlockSpec((D, ODD_COLS), const),
            pl.BlockSpec((1, MLA_Q_RANK), const),
            pl.BlockSpec((MLA_HEADS * LANES, MLA_Q_RANK), const),
            pl.BlockSpec((1, MLA_KV_RANK), const),
            pl.BlockSpec((MLA_KV_RANK + LANES, MLA_HEADS * LANES), const),
            pl.BlockSpec((MLA_HEADS * MLA_V_DIM, MLA_KV_RANK), const),
            pl.BlockSpec((ROPE_HALF, tm), lambda b, i: (0, i)),
            pl.BlockSpec((ROPE_HALF, tm), lambda b, i: (0, i)),
            pl.BlockSpec((tm, LANES), lambda b, i: (i, 0)),
            pl.BlockSpec((tm, LANES), lambda b, i: (i, 0)),
        ],
        out_specs=[
            pl.BlockSpec((1, tm, 2 * MIX_WIDTH), row),
            pl.BlockSpec((1, tm, MIX_WIDTH), row),
            pl.BlockSpec((1, tm, MIX_WIDTH), row),
            pl.BlockSpec((1, tm, LANES), row),
            pl.BlockSpec((1, MLA_HEADS * LANES, tm), lambda b, i: (b, 0, i)),
            pl.BlockSpec((1, tm, MLA_HEADS * LANES), row),
            pl.BlockSpec((1, tm // ATTN_BLOCK, MLA_HEADS * MLA_V_DIM, ATTN_BLOCK), lambda b, i: (b, i, 0, 0)),
        ],
        out_shape=[
            jax.ShapeDtypeStruct((B, S, 2 * MIX_WIDTH), F32),
            jax.ShapeDtypeStruct((B, S, MIX_WIDTH), BF16),
            jax.ShapeDtypeStruct((B, S, MIX_WIDTH), F32),
            jax.ShapeDtypeStruct((B, S, LANES), F32),
            jax.ShapeDtypeStruct((B, MLA_HEADS * LANES, S), BF16),
            jax.ShapeDtypeStruct((B, S, MLA_HEADS * LANES), BF16),
            jax.ShapeDtypeStruct((B, nb, MLA_HEADS * MLA_V_DIM, ATTN_BLOCK), BF16),
        ],
        compiler_params=_params(("parallel", "parallel")),
        name="odd_in",
    )(x, g.reshape(1, D), wn, q_norm.reshape(1, -1), wuqT, kv_norm.reshape(1, -1), wk2, wvT,
      cosT, sinT, cc, ss)


def _log_sigmoid(x):
    return jnp.minimum(x, 0.0) - jnp.log(1.0 + jnp.exp(-jnp.abs(x)))


def _mlstm_kernel(qk_ref, v_ref, op_ref, misc_ref, cw_ref, gb_ref, hn_ref, o_ref,
                  prev_ref, cn_ref, m_ref):
    c = pl.program_id(1)
    L = qk_ref.shape[1]
    row = lax.broadcasted_iota(jnp.int32, (L, 1), 0)
    lane = lax.broadcasted_iota(jnp.int32, (L, LANES), 1)
    is_f = (lane >= MISC_F) & (lane < MISC_F + MLSTM_HEADS)
    causal = lax.broadcasted_iota(jnp.int32, (L, L), 1) <= lax.broadcasted_iota(jnp.int32, (L, L), 0)

    @pl.when(c == 0)
    def _():
        prev_ref[...] = jnp.zeros_like(prev_ref)
        cn_ref[...] = jnp.zeros_like(cn_ref)
        m_ref[...] = jnp.zeros_like(m_ref)

    seqs = range(qk_ref.shape[0])
    heads = [(bb, h) for bb in seqs for h in range(MLSTM_HEADS)]
    d = MLSTM_HEAD_DIM

    qk = []
    for bb in seqs:
        x = qk_ref[bb]
        prev = prev_ref[bb]
        conv = x * cw_ref[CONV_WIDTH - 1:CONV_WIDTH, :]
        for j in range(1, CONV_WIDTH):
            shifted = pltpu.roll(jnp.where(row >= L - j, prev, x), j, axis=0)
            conv = conv + shifted * cw_ref[CONV_WIDTH - 1 - j:CONV_WIDTH - j, :]
        prev_ref[bb] = x
        qk.append(conv * _sigmoid(conv))

    sel_r = lax.broadcasted_iota(jnp.int32, (LANES, 2 * MIX_WIDTH), 0)
    sel_c = lax.broadcasted_iota(jnp.int32, (LANES, 2 * MIX_WIDTH), 1)
    spread = (sel_r == MISC_I + sel_c // LANES).astype(F32)
    pick = (lax.broadcasted_iota(jnp.int32, (8, LANES), 1)
            == MISC_I + lax.broadcasted_iota(jnp.int32, (8, LANES), 0)).astype(F32)
    mean_mat = jnp.full((d, d), 1.0 / d, F32)
    cols, rows = [], []
    for bb in seqs:
        gates = misc_ref[bb] + gb_ref[...]
        z = jnp.where(is_f, _log_sigmoid(gates), gates)
        cum = jnp.dot(causal.astype(F32), z, precision=HIGHEST, preferred_element_type=F32)
        z = jnp.where(is_f, cum, z)
        cols.append(jnp.dot(z, spread, precision=HIGHEST, preferred_element_type=F32))
        rows.append(_dot_nt(pick, z, precision=HIGHEST))

    q, k, v_aug, i_b, b_b, m_prev, m_t, w_inter, scores, inter = ({} for _ in range(10))
    ones_blk = jnp.ones((L, LANES), BF16)
    for key in heads:
        bb, h = key
        lo, hi = h * d, (h + 1) * d
        q[key] = qk[bb][:, lo:hi].astype(BF16)
        k[key] = qk[bb][:, MIX_WIDTH + lo:MIX_WIDTH + hi] * (d ** -0.5)
        v_aug[key] = jnp.concatenate([v_ref[bb, :, lo:hi], ones_blk], axis=1)
        scores[key] = _dot_nt(q[key], k[key].astype(BF16))
        inter[key] = _dot(q[key], cn_ref[bb, h].astype(BF16))
    intra = {}
    for key in heads:
        bb, h = key
        i_b[key] = cols[bb][:, h * LANES:(h + 1) * LANES]
        b_b[key] = cols[bb][:, (MLSTM_HEADS + h) * LANES:(MLSTM_HEADS + h + 1) * LANES]
        i_row = rows[bb][h:h + 1, :]
        b_row = rows[bb][MLSTM_HEADS + h:MLSTM_HEADS + h + 1, :]
        m_prev[key] = m_ref[bb, h:h + 1, :]
        intra[key] = jnp.where(causal, b_b[key] - b_row + i_row, NEG_INF)
    for key in heads:
        m_inter = b_b[key] + m_prev[key]
        m_t[key] = jnp.maximum(m_inter, jnp.max(intra[key], axis=1, keepdims=True))
        w_inter[key] = jnp.exp(m_inter - m_t[key])
    intra_o = {}
    for key in heads:
        a = jnp.exp(intra[key] - m_t[key]) * scores[key]
        intra_o[key] = _dot(a.astype(BF16), v_aug[key])
    for key in heads:
        bb, h = key
        lo, hi = h * d, (h + 1) * d
        num = w_inter[key] * inter[key][:, :d] + intra_o[key][:, :d]
        den = w_inter[key] * inter[key][:, d:] + intra_o[key][:, d:]
        hh = num / jnp.maximum(jnp.abs(den), jnp.exp(-m_t[key]))
        ms = jnp.dot(hh * hh, mean_mat, precision=HIGHEST, preferred_element_type=F32)
        hh = hh * lax.rsqrt(ms + NORM_EPS) * hn_ref[:, lo:hi]
        o_ref[bb, :, lo:hi] = (hh * _sigmoid(op_ref[bb, :, lo:hi])).astype(o_ref.dtype)
    for key in heads:
        bb, h = key
        b_end = b_b[key][L - 1:L, :]
        g = b_end - b_b[key] + i_b[key]
        m_new = jnp.maximum(b_end + m_prev[key], jnp.max(g, axis=0, keepdims=True))
        decay = jnp.exp(b_end + m_prev[key] - m_new)
        kw = k[key] * jnp.exp(g - m_new)
        cn_ref[bb, h] = (jnp.concatenate([decay, decay], axis=1) * cn_ref[bb, h]
                         + _dot(kw.T.astype(BF16), v_aug[key]))
        m_ref[bb, h:h + 1, :] = m_new


def _mlstm(qk_raw, vc, o_pre, misc, conv_w, b_i, b_f, head_norm):
    B, S, _ = qk_raw.shape
    L = MLSTM_CHUNK
    nb = MLSTM_BATCH
    gb = jnp.zeros((1, LANES), F32).at[0, MISC_I:MISC_I + MLSTM_HEADS].set(b_i)
    gb = gb.at[0, MISC_F:MISC_F + MLSTM_HEADS].set(b_f)
    row = lambda b, c: (b, c, 0)
    const = lambda b, c: (0, 0)
    return pl.pallas_call(
        _mlstm_kernel,
        grid=(B // nb, S // L),
        in_specs=[
            pl.BlockSpec((nb, L, 2 * MIX_WIDTH), row),
            pl.BlockSpec((nb, L, MIX_WIDTH), row),
            pl.BlockSpec((nb, L, MIX_WIDTH), row),
            pl.BlockSpec((nb, L, LANES), row),
            pl.BlockSpec((CONV_WIDTH, 2 * MIX_WIDTH), const),
            pl.BlockSpec((1, LANES), const),
            pl.BlockSpec((1, MIX_WIDTH), const),
        ],
        out_specs=pl.BlockSpec((nb, L, MIX_WIDTH), row),
        out_shape=jax.ShapeDtypeStruct((B, S, MIX_WIDTH), BF16),
        scratch_shapes=[
            pltpu.VMEM((nb, L, 2 * MIX_WIDTH), F32),
            pltpu.VMEM((nb, MLSTM_HEADS, MLSTM_HEAD_DIM, 2 * LANES), F32),
            pltpu.VMEM((nb, 8, LANES), F32),
        ],
        compiler_params=_params(("parallel", "arbitrary")),
        name="mlstm",
    )(qk_raw, vc, o_pre, misc, conv_w, gb, head_norm.reshape(1, MIX_WIDTH))


ZERO_WEIGHT_LOG2 = 160.0
NORM_SLACK = 1.02


def _alibi_first_tile(qn2, kn2, slopes):
    B, H, S = qn2.shape
    nt = S // ATTN_BLOCK
    k_max = jnp.sqrt(jnp.max(kn2[:, :, :H], axis=1))
    q_max = jnp.sqrt(jnp.max(qn2.reshape(B, H, nt, ATTN_BLOCK), axis=3))
    c = MOBA_HEAD_DIM ** -0.5 * LOG2E
    reach = ((ZERO_WEIGHT_LOG2 + 2.0 * NORM_SLACK * c * q_max * k_max[:, :, None])
             / (jnp.asarray(slopes)[None, :, None] * LOG2E))
    tiles = jnp.minimum(jnp.ceil((reach - 1.0) / ATTN_BLOCK), nt)
    tiles = jnp.max(tiles.reshape(B, H // HEADS_PER_STEP, HEADS_PER_STEP, nt), axis=2)
    first = jnp.arange(nt, dtype=F32)[None, None, :] - tiles
    return jnp.maximum(first, 0.0).astype(jnp.int32)


def _even_layer(h, norm_mix, w_in, pool_w, pool_scale, w_out, norm_ffn, wg, wu, wd):
    B, S, D = h.shape
    slopes = (2.0 ** (-8.0 * np.arange(1, MOBA_HEADS + 1) / MOBA_HEADS)).astype(np.float32)
    ka, ub, kmean, kn2, qT, vT = _even_in(h, norm_mix, w_in, slopes)
    nb = S // MOBA_BLOCK
    kmean = kmean.reshape(B, nb, MOBA_HEADS, LANES)[..., :MOBA_HEAD_DIM].transpose(0, 2, 1, 3)
    qaT, qn2 = _moba_gate(kmean, qT)
    aT = _attention(qaT, ka, vT, MOBA_HEAD_DIM, MOBA_HEADS, "moba_attn",
                    first_tile=_alibi_first_tile(qn2, kn2, slopes))
    b_out = _pool(ub, pool_w, pool_scale)
    h = _mix_out(h, aT, b_out, w_out, a_first=True)
    return _ffn(h.reshape(B * S, D), norm_ffn, wg.astype(BF16), wu.astype(BF16), wd.astype(BF16))


def _odd_layer(h, norm_mix, w_in, conv_w, b_i, b_f, head_norm, q_norm, w_uq, kv_norm, w_ukv,
               w_out, norm_ffn, router_w, router_b, wg, wu, wd):
    B, S, D = h.shape
    qk_raw, vc, o_pre, misc, mqT, mk, mvT = _odd_in(h, norm_mix, w_in, q_norm, w_uq, kv_norm, w_ukv)
    c_out = _mlstm(qk_raw, vc, o_pre, misc, conv_w, b_i, b_f, head_norm)
    dT = _attention(mqT, mk, mvT, MLA_V_DIM, MLA_HEADS, "mla_attn")
    h = _mix_out(h, dT, c_out, w_out, a_first=False)
    return _moe(h.reshape(B * S, D), norm_ffn, router_w, router_b,
                wg.astype(BF16), wu.astype(BF16), wd.astype(BF16))


def kernel(x, p, ev_norm_mix, ev_w_in, pool_w, pool_scale, ev_w_out, ev_norm_ffn, ffn_w_gate, ffn_w_up, ffn_w_down, od_norm_mix, od_w_in, conv_w, gate_b_i, gate_b_f, mlstm_norm, mla_q_norm, mla_w_uq, mla_kv_norm, mla_w_ukv, od_w_out, od_norm_ffn, router_w, router_b, moe_w_gate, moe_w_up, moe_w_down, ple_norm, ple_w_gate, ple_w_proj, final_norm):
    B, S, D = x.shape
    depth = p.shape[0]
    assert D == D_MODEL and S % (2 * ROW_TILE) == 0 and B % MLSTM_BATCH == 0
    assert MOBA_TOPK <= S // MOBA_BLOCK <= MOBA_MAX_BLOCKS
    h = x
    for layer in range(depth):
        j = layer // 2
        if layer % 2 == 0:
            h2d = _even_layer(h, ev_norm_mix[j], ev_w_in[j], pool_w[j], pool_scale[j], ev_w_out[j],
                              ev_norm_ffn[j], ffn_w_gate[j], ffn_w_up[j], ffn_w_down[j])
        else:
            h2d = _odd_layer(h, od_norm_mix[j], od_w_in[j], conv_w[j], gate_b_i[j], gate_b_f[j],
                             mlstm_norm[j], mla_q_norm[j], mla_w_uq[j], mla_kv_norm[j], mla_w_ukv[j],
                             od_w_out[j], od_norm_ffn[j], router_w[j], router_b[j],
                             moe_w_gate[j], moe_w_up[j], moe_w_down[j])
        last = layer == depth - 1
        h2d = _ple(h2d, p[layer].reshape(B * S, PLE_DIM), ple_norm[layer], ple_w_gate[layer],
                   ple_w_proj[layer], final_g=final_norm if last else None)
        h = h2d.reshape(B, S, D)
    return h
```

```python
import functools
import math

import numpy as np
import jax
import jax.numpy as jnp
from jax import lax
from jax.experimental import pallas as pl
from jax.experimental.pallas import tpu as pltpu

F32 = jnp.float32
BF16 = jnp.bfloat16
HIGHEST = lax.Precision.HIGHEST

D_MODEL = 1024
PLE_DIM = 256
NORM_EPS = 1e-6
NEG_INF = -1e30

MOBA_HEADS = 8
MOBA_HEAD_DIM = 64
MOBA_BLOCK = 256
MOBA_TOPK = 3
POOL_WINDOWS = (2, 4, 8, 16)
POOL_GROUP_DIM = 128
POOL_HALO = 16
MLSTM_HEADS = 4
MLSTM_HEAD_DIM = 128
MLSTM_CHUNK = 128
MLSTM_BATCH = 2
assert MLSTM_CHUNK == 128
CONV_WIDTH = 4
MLA_HEADS = 4
MLA_Q_RANK = 256
MLA_KV_RANK = 128
MLA_NOPE_DIM = 64
MLA_ROPE_DIM = 32
MLA_V_DIM = 128
ROPE_BASE = 10000.0
FFN_DIM = 2816
N_EXPERTS = 8
EXPERT_DIM = 3584
MIX_WIDTH = 512

ATTN_BLOCK = 512
ROW_TILE = 512
LANES = 128
VMEM_LIMIT = 56 * 1024 * 1024

MISC_ROPE = 0
MISC_I = 32
MISC_F = 36


def _params(sem, vmem=VMEM_LIMIT):
    return pltpu.CompilerParams(dimension_semantics=sem, vmem_limit_bytes=vmem)


def _rms(x, g):
    ms = jnp.mean(x * x, axis=-1, keepdims=True)
    return x * lax.rsqrt(ms + NORM_EPS) * g


def _sigmoid(x):
    return 1.0 / (1.0 + jnp.exp(-x))


def _dot(a, b):
    return jnp.dot(a, b, preferred_element_type=F32)


def _dot_nt(a, b, precision=None):
    return lax.dot_general(a, b, (((1,), (1,)), ((), ())), precision=precision,
                           preferred_element_type=F32)


KAUG_SEL = MOBA_HEAD_DIM
KAUG_POS = KAUG_SEL + 32
MOBA_MAX_BLOCKS = KAUG_POS - KAUG_SEL


def _bf16_terms(x, n):
    out = []
    for _ in range(n):
        bits = np.float32(x).view(np.uint32)
        kept = np.uint32((int(bits) + 0x7FFF + ((int(bits) >> 16) & 1)) & 0xFFFF0000)
        term = float(kept.view(np.float32))
        out.append(term)
        x -= term
    return tuple(out)


LOG2E = math.log2(math.e)
LOG2E_TERMS = _bf16_terms(LOG2E, 3)


def _even_in_kernel(x_ref, g_ref, wn_ref, wqT_ref, wvT_ref, ext_ref,
                    ka_ref, ub_ref, km_ref, kn_ref, qT_ref, vT_ref):
    tm = x_ref.shape[1]
    xn = _rms(x_ref[0], g_ref[...]).astype(BF16)
    n = _dot(xn, wn_ref[...])
    ka = n[:, :MOBA_HEADS * LANES]
    ka_ref[0] = (ka + ext_ref[...].astype(F32)).astype(BF16)
    ub_ref[0] = n[:, MOBA_HEADS * LANES:]
    for j in range(tm // MOBA_BLOCK):
        km_ref[0, j] = jnp.mean(ka[j * MOBA_BLOCK:(j + 1) * MOBA_BLOCK], axis=0, keepdims=True)
    slot = lax.broadcasted_iota(jnp.int32, (MOBA_HEADS * LANES, LANES), 0) // LANES
    head = lax.broadcasted_iota(jnp.int32, (MOBA_HEADS * LANES, LANES), 1)
    kn_ref[0] = _dot((ka * ka).astype(BF16), (slot == head).astype(BF16))
    qT_ref[0] = _dot_nt(wqT_ref[...], xn)
    vT = _dot_nt(wvT_ref[...], xn)
    for j in range(tm // ATTN_BLOCK):
        vT_ref[0, j] = vT[:, j * ATTN_BLOCK:(j + 1) * ATTN_BLOCK].astype(BF16)


def _moba_key_extras(S, slopes):
    pos = np.arange(S)
    blk, off = pos // MOBA_BLOCK, pos % MOBA_BLOCK
    ext = np.zeros((S, MOBA_HEADS, LANES), np.float32)
    ext[pos, :, KAUG_SEL + blk] = 1.0
    for term in range(len(LOG2E_TERMS)):
        ext[:, :, KAUG_POS + 2 * term] = slopes[None, :] * (MOBA_BLOCK * blk)[:, None]
        ext[:, :, KAUG_POS + 2 * term + 1] = slopes[None, :] * off[:, None]
    return jnp.asarray(ext.reshape(S, MOBA_HEADS * LANES), dtype=BF16)


def _even_in(x, g, w_in, slopes):
    B, S, D = x.shape
    tm = ROW_TILE
    nb = S // MOBA_BLOCK
    wq, wk, wv, wu = (w_in[:, i * MIX_WIDTH:(i + 1) * MIX_WIDTH] for i in range(4))
    wk_slots = jnp.concatenate(
        [wk.reshape(D, MOBA_HEADS, MOBA_HEAD_DIM),
         jnp.zeros((D, MOBA_HEADS, LANES - MOBA_HEAD_DIM), F32)], axis=2).reshape(D, MOBA_HEADS * LANES)
    wn = jnp.concatenate([wk_slots, wu], axis=1).astype(BF16)
    wqT = wq.T.astype(BF16)
    wvT = wv.T.astype(BF16)
    const = lambda b, i: (0, 0)
    return pl.pallas_call(
        _even_in_kernel,
        grid=(B, S // tm),
        in_specs=[
            pl.BlockSpec((1, tm, D), lambda b, i: (b, i, 0)),
            pl.BlockSpec((1, D), const),
            pl.BlockSpec((D, MOBA_HEADS * LANES + MIX_WIDTH), const),
            pl.BlockSpec((MIX_WIDTH, D), const),
            pl.BlockSpec((MIX_WIDTH, D), const),
            pl.BlockSpec((tm, MOBA_HEADS * LANES), lambda b, i: (i, 0)),
        ],
        out_specs=[
            pl.BlockSpec((1, tm, MOBA_HEADS * LANES), lambda b, i: (b, i, 0)),
            pl.BlockSpec((1, tm, MIX_WIDTH), lambda b, i: (b, i, 0)),
            pl.BlockSpec((1, tm // MOBA_BLOCK, 1, MOBA_HEADS * LANES), lambda b, i: (b, i, 0, 0)),
            pl.BlockSpec((1, tm, LANES), lambda b, i: (b, i, 0)),
            pl.BlockSpec((1, MIX_WIDTH, tm), lambda b, i: (b, 0, i)),
            pl.BlockSpec((1, tm // ATTN_BLOCK, MIX_WIDTH, ATTN_BLOCK), lambda b, i: (b, i, 0, 0)),
        ],
        out_shape=[
            jax.ShapeDtypeStruct((B, S, MOBA_HEADS * LANES), BF16),
            jax.ShapeDtypeStruct((B, S, MIX_WIDTH), F32),
            jax.ShapeDtypeStruct((B, nb, 1, MOBA_HEADS * LANES), F32),
            jax.ShapeDtypeStruct((B, S, LANES), F32),
            jax.ShapeDtypeStruct((B, MIX_WIDTH, S), F32),
            jax.ShapeDtypeStruct((B, S // ATTN_BLOCK, MIX_WIDTH, ATTN_BLOCK), BF16),
        ],
        compiler_params=_params(("parallel", "parallel")),
        name="even_in",
    )(x, g.reshape(1, D), wn, wqT, wvT, _moba_key_extras(S, slopes))


def _moba_gate_kernel(km_ref, qT_ref, qa_ref, qn_ref):
    i = pl.program_id(1)
    nb = km_ref.shape[2]
    tq = qT_ref.shape[2]
    row = lax.broadcasted_iota(jnp.int32, (nb, tq), 0)
    own = (i * tq + lax.broadcasted_iota(jnp.int32, (nb, tq), 1)) // MOBA_BLOCK
    past = row < own
    tail_row = lax.broadcasted_iota(jnp.int32, (LANES - KAUG_POS, tq), 0)
    tail = jnp.zeros(tail_row.shape, F32)
    for term, value in enumerate(LOG2E_TERMS):
        tail = jnp.where(tail_row // 2 == term, F32(value), tail)
    tail = tail.astype(BF16)
    pad = jnp.zeros((MOBA_MAX_BLOCKS - nb, tq), BF16) if nb < MOBA_MAX_BLOCKS else None
    for h in range(MOBA_HEADS):
        q_h = qT_ref[0, h * MOBA_HEAD_DIM:(h + 1) * MOBA_HEAD_DIM, :]
        qn_ref[0, h:h + 1, :] = jnp.sum(q_h * q_h, axis=0, keepdims=True)
        gate = jnp.dot(km_ref[0, h], q_h, precision=HIGHEST, preferred_element_type=F32)
        gate = jnp.where(past, gate, NEG_INF)
        chosen = jnp.zeros(gate.shape, F32)
        for _ in range(MOBA_TOPK):
            mx = jnp.max(gate, axis=0, keepdims=True)
            first = jnp.min(jnp.where(gate == mx, row, nb), axis=0, keepdims=True)
            pick = row == first
            chosen = jnp.where(pick, 1.0, chosen)
            gate = jnp.where(pick, -jnp.inf, gate)
        keep = jnp.where(past, chosen, (row == own).astype(F32))
        sel = jnp.where(keep > 0.0, 0.0, NEG_INF).astype(BF16)
        base = h * LANES
        qa_ref[0, base:base + KAUG_SEL] = (q_h * (MOBA_HEAD_DIM ** -0.5 * LOG2E)).astype(BF16)
        qa_ref[0, base + KAUG_SEL:base + KAUG_SEL + nb] = sel
        if pad is not None:
            qa_ref[0, base + KAUG_SEL + nb:base + KAUG_POS] = pad
        qa_ref[0, base + KAUG_POS:base + LANES] = tail


def _moba_gate(kmean, qT):
    B, H, nb, dh = kmean.shape
    S = qT.shape[2]
    tq = ATTN_BLOCK
    return pl.pallas_call(
        _moba_gate_kernel,
        grid=(B, S // tq),
        in_specs=[
            pl.BlockSpec((1, H, nb, dh), lambda b, i: (b, 0, 0, 0)),
            pl.BlockSpec((1, H * dh, tq), lambda b, i: (b, 0, i)),
        ],
        out_specs=[pl.BlockSpec((1, H * LANES, tq), lambda b, i: (b, 0, i)),
                   pl.BlockSpec((1, H, tq), lambda b, i: (b, 0, i))],
        out_shape=[jax.ShapeDtypeStruct((B, H * LANES, S), BF16), jax.ShapeDtypeStruct((B, H, S), F32)],
        compiler_params=_params(("parallel", "parallel")),
        name="moba_gate",
    )(kmean, qT)


HEADS_PER_STEP = 4
SUM_ROWS = 16


def _attn_kernel(lo_ref, q_ref, k_ref, v_ref, o_ref, sa_ref, sb_ref, m_ref, acc_ref):
    i = pl.program_id(2)
    lo = lo_ref[(pl.program_id(0) * pl.num_programs(1) + pl.program_id(1)) * pl.num_programs(2) + i]
    n_past = i - lo
    tq = q_ref.shape[2]
    tk = ATTN_BLOCK
    hp = HEADS_PER_STEP
    dv = v_ref.shape[2] // hp
    m_ref[...] = jnp.full(m_ref.shape, NEG_INF, F32)
    acc_ref[...] = jnp.zeros(acc_ref.shape, F32)
    ones_rows = jnp.ones((SUM_ROWS, tk), BF16)

    def scores(kvt, s_ref, diag):
        start = pl.multiple_of(kvt * tk, tk)
        k_tile = k_ref[0, pl.ds(start, tk), :]
        for g in range(hp):
            s = _dot(k_tile[:, g * LANES:(g + 1) * LANES], q_ref[0, g * LANES:(g + 1) * LANES, :])
            if diag:
                key = lax.broadcasted_iota(jnp.int32, (tk, tq), 0)
                qry = lax.broadcasted_iota(jnp.int32, (tk, tq), 1)
                s = jnp.where(key <= qry, s, NEG_INF)
            s_ref[g] = s

    def consume(kvt, s_ref):
        v_tile = v_ref[0, kvt]
        for g in range(hp):
            s = s_ref[g]
            m_run = m_ref[g]
            m_new = jnp.maximum(m_run, jnp.max(s, axis=0, keepdims=True))
            p = jnp.exp2(s - m_new).astype(BF16)
            v_aug = jnp.concatenate([v_tile[g * dv:(g + 1) * dv], ones_rows], axis=0)
            acc_ref[g] = jnp.exp2(m_run - m_new) * acc_ref[g] + _dot(v_aug, p)
            m_ref[g] = m_new

    tile_at = lambda t: jnp.where(t == 0, i, lo + t - 1)
    scores(i, sa_ref, True)

    def pair(p, carry):
        t = 2 * p
        scores(tile_at(t + 1), sb_ref, False)
        consume(tile_at(t), sa_ref)
        scores(tile_at(t + 2), sa_ref, False)
        consume(tile_at(t + 1), sb_ref)
        return carry

    lax.fori_loop(0, n_past // 2, pair, 0)
    last = 2 * (n_past // 2)

    @pl.when(n_past % 2 == 1)
    def _():
        scores(tile_at(last + 1), sb_ref, False)
        consume(tile_at(last), sa_ref)
        consume(tile_at(last + 1), sb_ref)

    @pl.when(n_past % 2 == 0)
    def _():
        consume(tile_at(last), sa_ref)

    for g in range(hp):
        o_ref[0, g * dv:(g + 1) * dv, :] = acc_ref[g, :dv] / acc_ref[g, dv:dv + 1]


def _attention(qT, k, vT, dv, heads, name, first_tile=None):
    B, _, S = qT.shape
    tq = ATTN_BLOCK
    nt = S // ATTN_BLOCK
    hp = HEADS_PER_STEP
    if first_tile is None:
        first_tile = jnp.zeros((B, heads // hp, nt), jnp.int32)
    return pl.pallas_call(
        _attn_kernel,
        grid_spec=pltpu.PrefetchScalarGridSpec(
            num_scalar_prefetch=1,
            grid=(B, heads // hp, S // tq),
            in_specs=[
                pl.BlockSpec((1, hp * LANES, tq), lambda b, h, i, lo: (b, h, i)),
                pl.BlockSpec((1, S, hp * LANES), lambda b, h, i, lo: (b, 0, h)),
                pl.BlockSpec((1, nt, hp * dv, ATTN_BLOCK), lambda b, h, i, lo: (b, 0, h, 0)),
            ],
            out_specs=pl.BlockSpec((1, hp * dv, tq), lambda b, h, i, lo: (b, h, i)),
            scratch_shapes=[pltpu.VMEM((hp, ATTN_BLOCK, tq), F32), pltpu.VMEM((hp, ATTN_BLOCK, tq), F32),
                            pltpu.VMEM((hp, 1, tq), F32), pltpu.VMEM((hp, dv + SUM_ROWS, tq), F32)],
        ),
        out_shape=jax.ShapeDtypeStruct((B, heads * dv, S), F32),
        compiler_params=_params(("parallel", "parallel", "arbitrary")),
        name=name,
    )(first_tile.reshape(-1), qT, k, vT)


def _pool_kernel(x_ref, halo_ref, w_ref, sc_ref, o_ref, xs_ref):
    i = pl.program_id(1)
    tm = x_ref.shape[1]
    x = x_ref[0]
    xs_ref[0:POOL_HALO] = jnp.where(i > 0, halo_ref[0], 0.0)
    xs_ref[POOL_HALO:POOL_HALO + tm] = x
    t = i * tm + lax.broadcasted_iota(jnp.int32, (tm, 1), 0)
    outs = []
    for g, win in enumerate(POOL_WINDOWS):
        lo, hi = g * POOL_GROUP_DIM, (g + 1) * POOL_GROUP_DIM
        xg = x[:, lo:hi]
        acc = xg
        for d in range(1, win):
            acc = acc + xs_ref[POOL_HALO - d:POOL_HALO - d + tm, lo:hi]
        count = jnp.minimum(t + 1, win).astype(F32)
        outs.append(_dot((acc / count - xg).astype(BF16), w_ref[g]))
    o_ref[0] = (jnp.concatenate(outs, axis=1) * sc_ref[...]).astype(o_ref.dtype)


def _pool(ub, pool_w, pool_scale):
    B, S, W = ub.shape
    tm = ROW_TILE
    per = tm // POOL_HALO
    return pl.pallas_call(
        _pool_kernel,
        grid=(B, S // tm),
        in_specs=[
            pl.BlockSpec((1, tm, W), lambda b, i: (b, i, 0)),
            pl.BlockSpec((1, POOL_HALO, W), lambda b, i: (b, jnp.maximum(i * per - 1, 0), 0)),
            pl.BlockSpec(pool_w.shape, lambda b, i: (0, 0, 0)),
            pl.BlockSpec((1, W), lambda b, i: (0, 0)),
        ],
        out_specs=pl.BlockSpec((1, tm, W), lambda b, i: (b, i, 0)),
        out_shape=jax.ShapeDtypeStruct((B, S, W), BF16),
        scratch_shapes=[pltpu.VMEM((POOL_HALO + tm, W), F32)],
        compiler_params=_params(("parallel", "parallel")),
        name="pool",
    )(ub, ub, pool_w.astype(BF16), pool_scale.reshape(1, W))


def _mix_out_kernel(h_ref, aT_ref, b_ref, w_ref, o_ref, *, a_first):
    a = aT_ref[0].T.astype(BF16)
    b = b_ref[0].astype(BF16)
    lo, hi = (a, b) if a_first else (b, a)
    y = _dot(lo, w_ref[:MIX_WIDTH]) + _dot(hi, w_ref[MIX_WIDTH:])
    o_ref[0] = h_ref[0] + y


def _mix_out(h, aT, b, w_out, a_first):
    B, S, D = h.shape
    tm = ROW_TILE
    return pl.pallas_call(
        functools.partial(_mix_out_kernel, a_first=a_first),
        grid=(B, S // tm),
        in_specs=[
            pl.BlockSpec((1, tm, D), lambda b_, i: (b_, i, 0)),
            pl.BlockSpec((1, MIX_WIDTH, tm), lambda b_, i: (b_, 0, i)),
            pl.BlockSpec((1, tm, MIX_WIDTH), lambda b_, i: (b_, i, 0)),
            pl.BlockSpec((2 * MIX_WIDTH, D), lambda b_, i: (0, 0)),
        ],
        out_specs=pl.BlockSpec((1, tm, D), lambda b_, i: (b_, i, 0)),
        out_shape=jax.ShapeDtypeStruct((B, S, D), F32),
        compiler_params=_params(("parallel", "parallel")),
        name="mix_out",
    )(h, aT, b, w_out.astype(BF16))


def _swiglu_step(xn, wg, wu, wd):
    gt = _dot(xn, wg)
    up = _dot(xn, wu)
    return _dot((gt * _sigmoid(gt) * up).astype(BF16), wd)


def _ffn_kernel(h_ref, g_ref, wg_ref, wu_ref, wd_ref, o_ref):
    h = h_ref[...]
    xn = _rms(h, g_ref[...]).astype(BF16)
    o_ref[...] = h + _swiglu_step(xn, wg_ref[...], wu_ref[...], wd_ref[...])


def _ffn(h2d, g, wg, wu, wd, tm=ROW_TILE):
    T, D = h2d.shape
    F = wg.shape[1]
    resident = lambda shape: pl.BlockSpec(shape, lambda i: (0, 0), pipeline_mode=pl.Buffered(1))
    return pl.pallas_call(
        _ffn_kernel,
        grid=(T // tm,),
        in_specs=[
            pl.BlockSpec((tm, D), lambda i: (i, 0)),
            pl.BlockSpec((1, D), lambda i: (0, 0)),
            resident((D, F)),
            resident((D, F)),
            resident((F, D)),
        ],
        out_specs=pl.BlockSpec((tm, D), lambda i: (i, 0)),
        out_shape=jax.ShapeDtypeStruct((T, D), F32),
        compiler_params=_params(("parallel",)),
        name="ffn",
    )(h2d, g.reshape(1, D), wg, wu, wd)


ROUTE_E0, ROUTE_E1, ROUTE_W0, ROUTE_W1, ROUTE_R0, ROUTE_R1 = range(6)
MOE_TILE = 512
MOE_TF = EXPERT_DIM // 2


def _lane_pick(tile, lane, idx):
    return jnp.sum(jnp.where(lane == idx, tile, 0.0), axis=1, keepdims=True)


def _router_kernel(h_ref, g_ref, w_ref, b_ref, route_ref, cnt_ref):
    tm = h_ref.shape[0]

    @pl.when(pl.program_id(0) == 0)
    def _():
        cnt_ref[...] = jnp.zeros_like(cnt_ref)

    xn = _rms(h_ref[...], g_ref[...])
    logits = jnp.dot(xn, w_ref[...], precision=HIGHEST, preferred_element_type=F32) + b_ref[...]
    lane = lax.broadcasted_iota(jnp.int32, logits.shape, 1)
    logits = jnp.where(lane < N_EXPERTS, logits, -jnp.inf)
    v0 = jnp.max(logits, axis=1, keepdims=True)
    i0 = jnp.min(jnp.where(logits == v0, lane, LANES), axis=1, keepdims=True)
    rest = jnp.where(lane == i0, -jnp.inf, logits)
    v1 = jnp.max(rest, axis=1, keepdims=True)
    i1 = jnp.min(jnp.where(rest == v1, lane, LANES), axis=1, keepdims=True)
    e1 = jnp.exp(v1 - v0)
    w0 = 1.0 / (1.0 + e1)
    sel = (lane == i0).astype(F32) + (lane == i1).astype(F32)
    earlier = (lax.broadcasted_iota(jnp.int32, (tm, tm), 1)
               < lax.broadcasted_iota(jnp.int32, (tm, tm), 0))
    rank = _dot(earlier.astype(BF16), sel.astype(BF16)) + cnt_ref[...]
    cnt_ref[...] += jnp.sum(sel, axis=0, keepdims=True)
    cols = ((ROUTE_E0, i0.astype(F32)), (ROUTE_E1, i1.astype(F32)), (ROUTE_W0, w0), (ROUTE_W1, e1 * w0),
            (ROUTE_R0, _lane_pick(rank, lane, i0)), (ROUTE_R1, _lane_pick(rank, lane, i1)))
    route = jnp.zeros(logits.shape, F32)
    for c, val in cols:
        route = jnp.where(lane == c, val, route)
    route_ref[...] = route


def _router(h2d, g, router_w, router_b):
    T, D = h2d.shape
    tm = ROW_TILE
    w = jnp.zeros((D, LANES), F32).at[:, :N_EXPERTS].set(router_w)
    b = jnp.zeros((1, LANES), F32).at[0, :N_EXPERTS].set(router_b)
    return pl.pallas_call(
        _router_kernel,
        grid=(T // tm,),
        in_specs=[
            pl.BlockSpec((tm, D), lambda i: (i, 0)),
            pl.BlockSpec((1, D), lambda i: (0, 0)),
            pl.BlockSpec((D, LANES), lambda i: (0, 0)),
            pl.BlockSpec((1, LANES), lambda i: (0, 0)),
        ],
        out_specs=[pl.BlockSpec((tm, LANES), lambda i: (i, 0)),
                   pl.BlockSpec((1, LANES), lambda i: (0, 0))],
        out_shape=[jax.ShapeDtypeStruct((T, LANES), F32), jax.ShapeDtypeStruct((1, LANES), F32)],
        compiler_params=_params(("arbitrary",)),
        name="router",
    )(h2d, g.reshape(1, D), w, b)


def _row_copies(pos_ref, base, r, src_of, dst_of, sem):
    return [pltpu.make_async_copy(src_of(k, pos_ref[base + 2 * r + k]),
                                  dst_of(k, pos_ref[base + 2 * r + k]), sem) for k in range(2)]


def _all_rows(tm, make):
    def issue(r, c):
        for k, cp in enumerate(make(r)):
            cp.start(priority=k)
        return c

    def drain(r, c):
        for cp in make(r):
            cp.wait()
        return c

    lax.fori_loop(0, tm, issue, 0, unroll=8)
    lax.fori_loop(0, tm, drain, 0, unroll=8)


def _dispatch_kernel(pos_ref, h_ref, g_ref, init_ref, xs_ref, xn_ref, sem):
    del init_ref
    tm = h_ref.shape[0]
    xn_ref[...] = _rms(h_ref[...], g_ref[...])
    base = pl.program_id(0) * (2 * tm)
    _all_rows(tm, lambda r: _row_copies(
        pos_ref, base, r, lambda k, p: xn_ref.at[pl.ds(r, 1)], lambda k, p: xs_ref.at[pl.ds(p, 1)], sem))


def _dispatch(pos, h2d, g, n_rows):
    T, D = h2d.shape
    tm = ROW_TILE
    return pl.pallas_call(
        _dispatch_kernel,
        grid_spec=pltpu.PrefetchScalarGridSpec(
            num_scalar_prefetch=1,
            grid=(T // tm,),
            in_specs=[pl.BlockSpec((tm, D), lambda i, pos: (i, 0)),
                      pl.BlockSpec((1, D), lambda i, pos: (0, 0)),
                      pl.BlockSpec(memory_space=pl.ANY)],
            out_specs=pl.BlockSpec(memory_space=pl.ANY),
            scratch_shapes=[pltpu.VMEM((tm, D), F32), pltpu.SemaphoreType.DMA(())],
        ),
        out_shape=jax.ShapeDtypeStruct((n_rows, D), F32),
        input_output_aliases={3: 0},
        compiler_params=_params(("arbitrary",)),
        name="moe_dispatch",
    )(pos, h2d, g.reshape(1, D), jnp.zeros((n_rows, D), F32))


def _moe_ffn_kernel(te_ref, nv_ref, x_ref, wg_ref, wu_ref, wd_ref, o_ref, xb_ref, acc_ref):
    j = pl.program_id(0)
    f = pl.program_id(1)
    valid = j < nv_ref[0]

    @pl.when(f == 0)
    def _():
        xb_ref[...] = x_ref[...].astype(BF16)
        acc_ref[...] = jnp.zeros_like(acc_ref)

    @pl.when(valid)
    def _():
        acc_ref[...] += _swiglu_step(xb_ref[...], wg_ref[0], wu_ref[0], wd_ref[0])

    @pl.when(f == pl.num_programs(1) - 1)
    def _():
        o_ref[...] = acc_ref[...]


def _moe_ffn(tile_expert, n_valid, xs, wg, wu, wd):
    N, D = xs.shape
    tm, tf = MOE_TILE, MOE_TF
    F = wg.shape[2]
    return pl.pallas_call(
        _moe_ffn_kernel,
        grid_spec=pltpu.PrefetchScalarGridSpec(
            num_scalar_prefetch=2,
            grid=(N // tm, F // tf),
            in_specs=[pl.BlockSpec((tm, D), lambda j, f, te, nv: (j, 0)),
                      pl.BlockSpec((1, D, tf), lambda j, f, te, nv: (te[j], 0, f)),
                      pl.BlockSpec((1, D, tf), lambda j, f, te, nv: (te[j], 0, f)),
                      pl.BlockSpec((1, tf, D), lambda j, f, te, nv: (te[j], f, 0))],
            out_specs=pl.BlockSpec((tm, D), lambda j, f, te, nv: (j, 0)),
            scratch_shapes=[pltpu.VMEM((tm, D), BF16), pltpu.VMEM((tm, D), F32)],
        ),
        out_shape=jax.ShapeDtypeStruct((N, D), F32),
        compiler_params=_params(("arbitrary", "arbitrary")),
        name="moe_ffn",
    )(tile_expert, n_valid, xs, wg, wu, wd)


def _combine_kernel(pos_ref, h_ref, route_ref, ys_ref, o_ref, y_ref, sem):
    tm = h_ref.shape[0]
    base = pl.program_id(0) * (2 * tm)
    _all_rows(tm, lambda r: _row_copies(
        pos_ref, base, r, lambda k, p: ys_ref.at[pl.ds(p, 1)], lambda k, p: y_ref.at[k, pl.ds(r, 1)], sem))
    route = route_ref[...]
    lane = lax.broadcasted_iota(jnp.int32, route.shape, 1)
    o_ref[...] = (h_ref[...] + _lane_pick(route, lane, ROUTE_W0) * y_ref[0]
                  + _lane_pick(route, lane, ROUTE_W1) * y_ref[1])


def _combine(pos, h2d, route, ys):
    T, D = h2d.shape
    tm = ROW_TILE
    return pl.pallas_call(
        _combine_kernel,
        grid_spec=pltpu.PrefetchScalarGridSpec(
            num_scalar_prefetch=1,
            grid=(T // tm,),
            in_specs=[pl.BlockSpec((tm, D), lambda i, pos: (i, 0)),
                      pl.BlockSpec((tm, LANES), lambda i, pos: (i, 0)),
                      pl.BlockSpec(memory_space=pl.ANY)],
            out_specs=pl.BlockSpec((tm, D), lambda i, pos: (i, 0)),
            scratch_shapes=[pltpu.VMEM((2, tm, D), F32), pltpu.SemaphoreType.DMA(())],
        ),
        out_shape=jax.ShapeDtypeStruct((T, D), F32),
        compiler_params=_params(("arbitrary",)),
        name="moe_combine",
    )(pos, h2d, route, ys)


def _moe(h2d, g, router_w, router_b, wg, wu, wd):
    T, D = h2d.shape
    tm = MOE_TILE
    route, counts = _router(h2d, g, router_w, router_b)
    cnt = counts[0, :N_EXPERTS].astype(jnp.int32)
    padded = (cnt + tm - 1) // tm * tm
    ends = jnp.cumsum(padded)
    start = ends - padded
    e01 = route[:, ROUTE_E0:ROUTE_E1 + 1].astype(jnp.int32)
    r01 = route[:, ROUTE_R0:ROUTE_R1 + 1].astype(jnp.int32)
    pos = (start[e01] + r01).reshape(2 * T)
    n_rows = 2 * T + N_EXPERTS * tm
    tile_row = jnp.arange(n_rows // tm, dtype=jnp.int32) * tm
    tile_expert = jnp.minimum(jnp.sum(tile_row[:, None] >= ends[None, :], axis=1), N_EXPERTS - 1).astype(jnp.int32)
    n_valid = (ends[-1:] // tm).astype(jnp.int32)
    xs = _dispatch(pos, h2d, g, n_rows)
    ys = _moe_ffn(tile_expert, n_valid, xs, wg, wu, wd)
    return _combine(pos, h2d, route, ys)


def _ple_kernel(*refs, final):
    if final:
        h_ref, p_ref, g_ref, wg_ref, wp_ref, fg_ref, o_ref = refs
    else:
        h_ref, p_ref, g_ref, wg_ref, wp_ref, o_ref = refs
    h = h_ref[...]
    gate = _sigmoid(_dot(_rms(h, g_ref[...]).astype(BF16), wg_ref[...]))
    out = h + gate * _dot(p_ref[...].astype(BF16), wp_ref[...])
    if final:
        out = _rms(out, fg_ref[...])
    o_ref[...] = out


def _ple(h2d, p2d, g, w_gate, w_proj, final_g=None):
    T, D = h2d.shape
    tm = ROW_TILE
    final = final_g is not None
    in_specs = [
        pl.BlockSpec((tm, D), lambda i: (i, 0)),
        pl.BlockSpec((tm, PLE_DIM), lambda i: (i, 0)),
        pl.BlockSpec((1, D), lambda i: (0, 0)),
        pl.BlockSpec((D, D), lambda i: (0, 0)),
        pl.BlockSpec((PLE_DIM, D), lambda i: (0, 0)),
    ]
    args = [h2d, p2d, g.reshape(1, D), w_gate.astype(BF16), w_proj.astype(BF16)]
    if final:
        in_specs.append(pl.BlockSpec((1, D), lambda i: (0, 0)))
        args.append(final_g.reshape(1, D))
    return pl.pallas_call(
        functools.partial(_ple_kernel, final=final),
        grid=(T // tm,),
        in_specs=in_specs,
        out_specs=pl.BlockSpec((tm, D), lambda i: (i, 0)),
        out_shape=jax.ShapeDtypeStruct((T, D), F32),
        compiler_params=_params(("parallel",)),
        name="ple_final" if final else "ple",
    )(*args)


ODD_MAIN = 4 * MIX_WIDTH
ODD_COLS = ODD_MAIN + MLA_Q_RANK + MLA_KV_RANK + 2 * LANES
MLA_QK_SCALE = (MLA_NOPE_DIM + MLA_ROPE_DIM) ** -0.5 * LOG2E
ROPE_HALF = MLA_ROPE_DIM // 2


def _odd_in_kernel(x_ref, g_ref, wn_ref, qn_ref, wuqT_ref, kvn_ref, wk2_ref, wvT_ref,
                   cosT_ref, sinT_ref, cc_ref, ss_ref,
                   qk_ref, vc_ref, op_ref, misc_ref, mq_ref, mk_ref, mv_ref):
    tm = x_ref.shape[1]
    xn = _rms(x_ref[0], g_ref[...]).astype(BF16)
    u = _dot(xn, wn_ref[...])
    qk_ref[0] = u[:, :2 * MIX_WIDTH]
    vc_ref[0] = u[:, 2 * MIX_WIDTH:3 * MIX_WIDTH].astype(BF16)
    op_ref[0] = u[:, 3 * MIX_WIDTH:ODD_MAIN]
    c0 = ODD_MAIN
    c_q = u[:, c0:c0 + MLA_Q_RANK]
    c0 += MLA_Q_RANK
    c_kv = u[:, c0:c0 + MLA_KV_RANK]
    c0 += MLA_KV_RANK
    misc = u[:, c0:c0 + LANES]
    misc_sw = u[:, c0 + LANES:c0 + 2 * LANES]
    misc_ref[0] = misc
    cqn = _rms(c_q, qn_ref[...]).astype(BF16)
    qT = _dot_nt(wuqT_ref[...], cqn)
    cosT = cosT_ref[...]
    sinT = sinT_ref[...]
    for h in range(MLA_HEADS):
        r = h * LANES
        mq_ref[0, r:r + MLA_NOPE_DIM] = (qT[r:r + MLA_NOPE_DIM] * MLA_QK_SCALE).astype(BF16)
        x1 = qT[r + MLA_NOPE_DIM:r + MLA_NOPE_DIM + ROPE_HALF]
        x2 = qT[r + MLA_NOPE_DIM + ROPE_HALF:r + MLA_NOPE_DIM + MLA_ROPE_DIM]
        mq_ref[0, r + MLA_NOPE_DIM:r + MLA_NOPE_DIM + ROPE_HALF] = (
            (x1 * cosT - x2 * sinT) * MLA_QK_SCALE).astype(BF16)
        mq_ref[0, r + MLA_NOPE_DIM + ROPE_HALF:r + MLA_NOPE_DIM + MLA_ROPE_DIM] = (
            (x1 * sinT + x2 * cosT) * MLA_QK_SCALE).astype(BF16)
        mq_ref[0, r + MLA_NOPE_DIM + MLA_ROPE_DIM:r + LANES] = jnp.zeros(
            (LANES - MLA_NOPE_DIM - MLA_ROPE_DIM, tm), BF16)
    ckvn = _rms(c_kv, kvn_ref[...]).astype(BF16)
    k_rot = (misc * cc_ref[...] + misc_sw * ss_ref[...]).astype(BF16)
    mk_ref[0] = _dot(jnp.concatenate([ckvn, k_rot], axis=1), wk2_ref[...]).astype(BF16)
    vT = _dot_nt(wvT_ref[...], ckvn)
    for j in range(tm // ATTN_BLOCK):
        mv_ref[0, j] = vT[:, j * ATTN_BLOCK:(j + 1) * ATTN_BLOCK].astype(BF16)


def _rope_tables(S):
    inv_freq = ROPE_BASE ** (-jnp.arange(ROPE_HALF, dtype=F32) / ROPE_HALF)
    ang = jnp.arange(S, dtype=F32)[:, None] * inv_freq[None, :]
    cos, sin = jnp.cos(ang), jnp.sin(ang)
    pad = jnp.zeros((S, LANES - MLA_ROPE_DIM), F32)
    cc = jnp.concatenate([cos, cos, pad], axis=1)
    ss = jnp.concatenate([-sin, sin, pad], axis=1)
    return cos.T, sin.T, cc, ss


def _odd_in(x, g, w_in, q_norm, w_uq, kv_norm, w_ukv):
    B, S, D = x.shape
    tm = ROW_TILE
    cuts = np.cumsum([MIX_WIDTH] * 4 + [MLSTM_HEADS, MLSTM_HEADS, MLA_Q_RANK, MLA_KV_RANK]).tolist()
    w_main = w_in[:, :cuts[3]]
    w_i = w_in[:, cuts[3]:cuts[4]]
    w_f = w_in[:, cuts[4]:cuts[5]]
    w_cq = w_in[:, cuts[5]:cuts[6]]
    w_ckv = w_in[:, cuts[6]:cuts[7]]
    w_kr = w_in[:, cuts[7]:]
    w_kr_sw = jnp.concatenate([w_kr[:, ROPE_HALF:], w_kr[:, :ROPE_HALF]], axis=1)
    zpad = lambda n: jnp.zeros((D, n), F32)
    w_misc = jnp.concatenate([w_kr, w_i, w_f, zpad(LANES - MLA_ROPE_DIM - 2 * MLSTM_HEADS)], axis=1)
    w_misc_sw = jnp.concatenate([w_kr_sw, zpad(LANES - MLA_ROPE_DIM)], axis=1)
    wn = jnp.concatenate([w_main, w_cq, w_ckv, w_misc, w_misc_sw], axis=1).astype(BF16)
    qd = MLA_NOPE_DIM + MLA_ROPE_DIM
    w_uq_h = w_uq.reshape(MLA_Q_RANK, MLA_HEADS, qd)
    w_uq_h = jnp.concatenate([w_uq_h, jnp.zeros((MLA_Q_RANK, MLA_HEADS, LANES - qd), F32)], axis=2)
    wuqT = w_uq_h.reshape(MLA_Q_RANK, MLA_HEADS * LANES).T.astype(BF16)
    w_ukv_h = w_ukv.reshape(MLA_KV_RANK, MLA_HEADS, MLA_NOPE_DIM + MLA_V_DIM)
    w_k = jnp.concatenate([w_ukv_h[:, :, :MLA_NOPE_DIM],
                           jnp.zeros((MLA_KV_RANK, MLA_HEADS, LANES - MLA_NOPE_DIM), F32)], axis=2)
    place = jnp.zeros((LANES, MLA_HEADS, LANES), F32)
    eye = jnp.eye(MLA_ROPE_DIM, dtype=F32)
    place = place.at[:MLA_ROPE_DIM, :, MLA_NOPE_DIM:MLA_NOPE_DIM + MLA_ROPE_DIM].set(
        jnp.broadcast_to(eye[:, None, :], (MLA_ROPE_DIM, MLA_HEADS, MLA_ROPE_DIM)))
    wk2 = jnp.concatenate([w_k, place], axis=0).reshape(MLA_KV_RANK + LANES, MLA_HEADS * LANES).astype(BF16)
    wvT = w_ukv_h[:, :, MLA_NOPE_DIM:].reshape(MLA_KV_RANK, MLA_HEADS * MLA_V_DIM).T.astype(BF16)
    cosT, sinT, cc, ss = _rope_tables(S)
    row = lambda b, i: (b, i, 0)
    const = lambda b, i: (0, 0)
    nb = S // ATTN_BLOCK
    return pl.pallas_call(
        _odd_in_kernel,
        grid=(B, S // tm),
        in_specs=[
            pl.BlockSpec((1, tm, D), row),
            pl.BlockSpec((1, D), const),
            pl.BlockSpec((D, ODD_COLS), const),
            pl.BlockSpec((1, MLA_Q_RANK), const),
            pl.BlockSpec((MLA_HEADS * LANES, MLA_Q_RANK), const),
            pl.BlockSpec((1, MLA_KV_RANK), const),
            pl.BlockSpec((MLA_KV_RANK + LANES, MLA_HEADS * LANES), const),
            pl.BlockSpec((MLA_HEADS * MLA_V_DIM, MLA_KV_RANK), const),
            pl.BlockSpec((ROPE_HALF, tm), lambda b, i: (0, i)),
            pl.BlockSpec((ROPE_HALF, tm), lambda b, i: (0, i)),
            pl.BlockSpec((tm, LANES), lambda b, i: (i, 0)),
            pl.BlockSpec((tm, LANES), lambda b, i: (i, 0)),
        ],
        out_specs=[
            pl.BlockSpec((1, tm, 2 * MIX_WIDTH), row),
            pl.BlockSpec((1, tm, MIX_WIDTH), row),
            pl.BlockSpec((1, tm, MIX_WIDTH), row),
            pl.BlockSpec((1, tm, LANES), row),
            pl.BlockSpec((1, MLA_HEADS * LANES, tm), lambda b, i: (b, 0, i)),
            pl.BlockSpec((1, tm, MLA_HEADS * LANES), row),
            pl.BlockSpec((1, tm // ATTN_BLOCK, MLA_HEADS * MLA_V_DIM, ATTN_BLOCK), lambda b, i: (b, i, 0, 0)),
        ],
        out_shape=[
            jax.ShapeDtypeStruct((B, S, 2 * MIX_WIDTH), F32),
            jax.ShapeDtypeStruct((B, S, MIX_WIDTH), BF16),
            jax.ShapeDtypeStruct((B, S, MIX_WIDTH), F32),
            jax.ShapeDtypeStruct((B, S, LANES), F32),
            jax.ShapeDtypeStruct((B, MLA_HEADS * LANES, S), BF16),
            jax.ShapeDtypeStruct((B, S, MLA_HEADS * LANES), BF16),
            jax.ShapeDtypeStruct((B, nb, MLA_HEADS * MLA_V_DIM, ATTN_BLOCK), BF16),
        ],
        compiler_params=_params(("parallel", "parallel")),
        name="odd_in",
    )(x, g.reshape(1, D), wn, q_norm.reshape(1, -1), wuqT, kv_norm.reshape(1, -1), wk2, wvT,
      cosT, sinT, cc, ss)


def _log_sigmoid(x):
    return jnp.minimum(x, 0.0) - jnp.log(1.0 + jnp.exp(-jnp.abs(x)))


def _mlstm_kernel(qk_ref, v_ref, op_ref, misc_ref, cw_ref, gb_ref, hn_ref, o_ref,
                  prev_ref, cn_ref, m_ref):
    c = pl.program_id(1)
    L = qk_ref.shape[1]
    row = lax.broadcasted_iota(jnp.int32, (L, 1), 0)
    lane = lax.broadcasted_iota(jnp.int32, (L, LANES), 1)
    is_f = (lane >= MISC_F) & (lane < MISC_F + MLSTM_HEADS)
    causal = lax.broadcasted_iota(jnp.int32, (L, L), 1) <= lax.broadcasted_iota(jnp.int32, (L, L), 0)

    @pl.when(c == 0)
    def _():
        prev_ref[...] = jnp.zeros_like(prev_ref)
        cn_ref[...] = jnp.zeros_like(cn_ref)
        m_ref[...] = jnp.zeros_like(m_ref)

    seqs = range(qk_ref.shape[0])
    heads = [(bb, h) for bb in seqs for h in range(MLSTM_HEADS)]
    d = MLSTM_HEAD_DIM

    qk = []
    for bb in seqs:
        x = qk_ref[bb]
        prev = prev_ref[bb]
        conv = x * cw_ref[CONV_WIDTH - 1:CONV_WIDTH, :]
        for j in range(1, CONV_WIDTH):
            shifted = pltpu.roll(jnp.where(row >= L - j, prev, x), j, axis=0)
            conv = conv + shifted * cw_ref[CONV_WIDTH - 1 - j:CONV_WIDTH - j, :]
        prev_ref[bb] = x
        qk.append(conv * _sigmoid(conv))

    sel_r = lax.broadcasted_iota(jnp.int32, (LANES, 2 * MIX_WIDTH), 0)
    sel_c = lax.broadcasted_iota(jnp.int32, (LANES, 2 * MIX_WIDTH), 1)
    spread = (sel_r == MISC_I + sel_c // LANES).astype(F32)
    pick = (lax.broadcasted_iota(jnp.int32, (8, LANES), 1)
            == MISC_I + lax.broadcasted_iota(jnp.int32, (8, LANES), 0)).astype(F32)
    mean_mat = jnp.full((d, d), 1.0 / d, F32)
    cols, rows = [], []
    for bb in seqs:
        gates = misc_ref[bb] + gb_ref[...]
        z = jnp.where(is_f, _log_sigmoid(gates), gates)
        cum = jnp.dot(causal.astype(F32), z, precision=HIGHEST, preferred_element_type=F32)
        z = jnp.where(is_f, cum, z)
        cols.append(jnp.dot(z, spread, precision=HIGHEST, preferred_element_type=F32))
        rows.append(_dot_nt(pick, z, precision=HIGHEST))

    q, k, v_aug, i_b, b_b, m_prev, m_t, w_inter, scores, inter = ({} for _ in range(10))
    ones_blk = jnp.ones((L, LANES), BF16)
    for key in heads:
        bb, h = key
        lo, hi = h * d, (h + 1) * d
        q[key] = qk[bb][:, lo:hi].astype(BF16)
        k[key] = qk[bb][:, MIX_WIDTH + lo:MIX_WIDTH + hi] * (d ** -0.5)
        v_aug[key] = jnp.concatenate([v_ref[bb, :, lo:hi], ones_blk], axis=1)
        scores[key] = _dot_nt(q[key], k[key].astype(BF16))
        inter[key] = _dot(q[key], cn_ref[bb, h].astype(BF16))
    intra = {}
    for key in heads:
        bb, h = key
        i_b[key] = cols[bb][:, h * LANES:(h + 1) * LANES]
        b_b[key] = cols[bb][:, (MLSTM_HEADS + h) * LANES:(MLSTM_HEADS + h + 1) * LANES]
        i_row = rows[bb][h:h + 1, :]
        b_row = rows[bb][MLSTM_HEADS + h:MLSTM_HEADS + h + 1, :]
        m_prev[key] = m_ref[bb, h:h + 1, :]
        intra[key] = jnp.where(causal, b_b[key] - b_row + i_row, NEG_INF)
    for key in heads:
        m_inter = b_b[key] + m_prev[key]
        m_t[key] = jnp.maximum(m_inter, jnp.max(intra[key], axis=1, keepdims=True))
        w_inter[key] = jnp.exp(m_inter - m_t[key])
    intra_o = {}
    for key in heads:
        a = jnp.exp(intra[key] - m_t[key]) * scores[key]
        intra_o[key] = _dot(a.astype(BF16), v_aug[key])
    for key in heads:
        bb, h = key
        lo, hi = h * d, (h + 1) * d
        num = w_inter[key] * inter[key][:, :d] + intra_o[key][:, :d]
        den = w_inter[key] * inter[key][:, d:] + intra_o[key][:, d:]
        hh = num / jnp.maximum(jnp.abs(den), jnp.exp(-m_t[key]))
        ms = jnp.dot(hh * hh, mean_mat, precision=HIGHEST, preferred_element_type=F32)
        hh = hh * lax.rsqrt(ms + NORM_EPS) * hn_ref[:, lo:hi]
        o_ref[bb, :, lo:hi] = (hh * _sigmoid(op_ref[bb, :, lo:hi])).astype(o_ref.dtype)
    for key in heads:
        bb, h = key
        b_end = b_b[key][L - 1:L, :]
        g = b_end - b_b[key] + i_b[key]
        m_new = jnp.maximum(b_end + m_prev[key], jnp.max(g, axis=0, keepdims=True))
        decay = jnp.exp(b_end + m_prev[key] - m_new)
        kw = k[key] * jnp.exp(g - m_new)
        cn_ref[bb, h] = (jnp.concatenate([decay, decay], axis=1) * cn_ref[bb, h]
                         + _dot(kw.T.astype(BF16), v_aug[key]))
        m_ref[bb, h:h + 1, :] = m_new


def _mlstm(qk_raw, vc, o_pre, misc, conv_w, b_i, b_f, head_norm):
    B, S, _ = qk_raw.shape
    L = MLSTM_CHUNK
    nb = MLSTM_BATCH
    gb = jnp.zeros((1, LANES), F32).at[0, MISC_I:MISC_I + MLSTM_HEADS].set(b_i)
    gb = gb.at[0, MISC_F:MISC_F + MLSTM_HEADS].set(b_f)
    row = lambda b, c: (b, c, 0)
    const = lambda b, c: (0, 0)
    return pl.pallas_call(
        _mlstm_kernel,
        grid=(B // nb, S // L),
        in_specs=[
            pl.BlockSpec((nb, L, 2 * MIX_WIDTH), row),
            pl.BlockSpec((nb, L, MIX_WIDTH), row),
            pl.BlockSpec((nb, L, MIX_WIDTH), row),
            pl.BlockSpec((nb, L, LANES), row),
            pl.BlockSpec((CONV_WIDTH, 2 * MIX_WIDTH), const),
            pl.BlockSpec((1, LANES), const),
            pl.BlockSpec((1, MIX_WIDTH), const),
        ],
        out_specs=pl.BlockSpec((nb, L, MIX_WIDTH), row),
        out_shape=jax.ShapeDtypeStruct((B, S, MIX_WIDTH), BF16),
        scratch_shapes=[
            pltpu.VMEM((nb, L, 2 * MIX_WIDTH), F32),
            pltpu.VMEM((nb, MLSTM_HEADS, MLSTM_HEAD_DIM, 2 * LANES), F32),
            pltpu.VMEM((nb, 8, LANES), F32),
        ],
        compiler_params=_params(("parallel", "arbitrary")),
        name="mlstm",
    )(qk_raw, vc, o_pre, misc, conv_w, gb, head_norm.reshape(1, MIX_WIDTH))


ZERO_WEIGHT_LOG2 = 160.0
NORM_SLACK = 1.02


def _alibi_first_tile(qn2, kn2, slopes):
    B, H, S = qn2.shape
    nt = S // ATTN_BLOCK
    k_max = jnp.sqrt(jnp.max(kn2[:, :, :H], axis=1))
    q_max = jnp.sqrt(jnp.max(qn2.reshape(B, H, nt, ATTN_BLOCK), axis=3))
    c = MOBA_HEAD_DIM ** -0.5 * LOG2E
    reach = ((ZERO_WEIGHT_LOG2 + 2.0 * NORM_SLACK * c * q_max * k_max[:, :, None])
             / (jnp.asarray(slopes)[None, :, None] * LOG2E))
    tiles = jnp.minimum(jnp.ceil((reach - 1.0) / ATTN_BLOCK), nt)
    tiles = jnp.max(tiles.reshape(B, H // HEADS_PER_STEP, HEADS_PER_STEP, nt), axis=2)
    first = jnp.arange(nt, dtype=F32)[None, None, :] - tiles
    return jnp.maximum(first, 0.0).astype(jnp.int32)


def _even_layer(h, norm_mix, w_in, pool_w, pool_scale, w_out, norm_ffn, wg, wu, wd):
    B, S, D = h.shape
    slopes = (2.0 ** (-8.0 * np.arange(1, MOBA_HEADS + 1) / MOBA_HEADS)).astype(np.float32)
    ka, ub, kmean, kn2, qT, vT = _even_in(h, norm_mix, w_in, slopes)
    nb = S // MOBA_BLOCK
    kmean = kmean.reshape(B, nb, MOBA_HEADS, LANES)[..., :MOBA_HEAD_DIM].transpose(0, 2, 1, 3)
    qaT, qn2 = _moba_gate(kmean, qT)
    aT = _attention(qaT, ka, vT, MOBA_HEAD_DIM, MOBA_HEADS, "moba_attn",
                    first_tile=_alibi_first_tile(qn2, kn2, slopes))
    b_out = _pool(ub, pool_w, pool_scale)
    h = _mix_out(h, aT, b_out, w_out, a_first=True)
    return _ffn(h.reshape(B * S, D), norm_ffn, wg.astype(BF16), wu.astype(BF16), wd.astype(BF16))


def _odd_layer(h, norm_mix, w_in, conv_w, b_i, b_f, head_norm, q_norm, w_uq, kv_norm, w_ukv,
               w_out, norm_ffn, router_w, router_b, wg, wu, wd):
    B, S, D = h.shape
    qk_raw, vc, o_pre, misc, mqT, mk, mvT = _odd_in(h, norm_mix, w_in, q_norm, w_uq, kv_norm, w_ukv)
    c_out = _mlstm(qk_raw, vc, o_pre, misc, conv_w, b_i, b_f, head_norm)
    dT = _attention(mqT, mk, mvT, MLA_V_DIM, MLA_HEADS, "mla_attn")
    h = _mix_out(h, dT, c_out, w_out, a_first=False)
    return _moe(h.reshape(B * S, D), norm_ffn, router_w, router_b,
                wg.astype(BF16), wu.astype(BF16), wd.astype(BF16))


def kernel(x, p, ev_norm_mix, ev_w_in, pool_w, pool_scale, ev_w_out, ev_norm_ffn, ffn_w_gate, ffn_w_up, ffn_w_down, od_norm_mix, od_w_in, conv_w, gate_b_i, gate_b_f, mlstm_norm, mla_q_norm, mla_w_uq, mla_kv_norm, mla_w_ukv, od_w_out, od_norm_ffn, router_w, router_b, moe_w_gate, moe_w_up, moe_w_down, ple_norm, ple_w_gate, ple_w_proj, final_norm):
    B, S, D = x.shape
    depth = p.shape[0]
    assert D == D_MODEL and S % (2 * ROW_TILE) == 0 and B % MLSTM_BATCH == 0
    assert MOBA_TOPK <= S // MOBA_BLOCK <= MOBA_MAX_BLOCKS
    h = x
    for layer in range(depth):
        j = layer // 2
        if layer % 2 == 0:
            h2d = _even_layer(h, ev_norm_mix[j], ev_w_in[j], pool_w[j], pool_scale[j], ev_w_out[j],
                              ev_norm_ffn[j], ffn_w_gate[j], ffn_w_up[j], ffn_w_down[j])
        else:
            h2d = _odd_layer(h, od_norm_mix[j], od_w_in[j], conv_w[j], gate_b_i[j], gate_b_f[j],
                             mlstm_norm[j], mla_q_norm[j], mla_w_uq[j], mla_kv_norm[j], mla_w_ukv[j],
                             od_w_out[j], od_norm_ffn[j], router_w[j], router_b[j],
                             moe_w_gate[j], moe_w_up[j], moe_w_down[j])
        last = layer == depth - 1
        h2d = _ple(h2d, p[layer].reshape(B * S, PLE_DIM), ple_norm[layer], ple_w_gate[layer],
                   ple_w_proj[layer], final_g=final_norm if last else None)
        h = h2d.reshape(B, S, D)
    return h
```

```python
import functools
import math

import numpy as np
import jax
import jax.numpy as jnp
from jax import lax
from jax.experimental import pallas as pl
from jax.experimental.pallas import tpu as pltpu

F32 = jnp.float32
BF16 = jnp.bfloat16
HIGHEST = lax.Precision.HIGHEST

D_MODEL = 1024
PLE_DIM = 256
NORM_EPS = 1e-6
NEG_INF = -1e30

MOBA_HEADS = 8
MOBA_HEAD_DIM = 64
MOBA_BLOCK = 256
MOBA_TOPK = 3
POOL_WINDOWS = (2, 4, 8, 16)
POOL_GROUP_DIM = 128
POOL_HALO = 16
MLSTM_HEADS = 4
MLSTM_HEAD_DIM = 128
MLSTM_CHUNK = 128
MLSTM_BATCH = 2
assert MLSTM_CHUNK == 128
CONV_WIDTH = 4
MLA_HEADS = 4
MLA_Q_RANK = 256
MLA_KV_RANK = 128
MLA_NOPE_DIM = 64
MLA_ROPE_DIM = 32
MLA_V_DIM = 128
ROPE_BASE = 10000.0
FFN_DIM = 2816
N_EXPERTS = 8
EXPERT_DIM = 3584
MIX_WIDTH = 512

ATTN_BLOCK = 512
ROW_TILE = 512
LANES = 128
VMEM_LIMIT = 56 * 1024 * 1024

MISC_ROPE = 0
MISC_I = 32
MISC_F = 36


def _params(sem, vmem=VMEM_LIMIT):
    return pltpu.CompilerParams(dimension_semantics=sem, vmem_limit_bytes=vmem)


def _rms(x, g):
    ms = jnp.mean(x * x, axis=-1, keepdims=True)
    return x * lax.rsqrt(ms + NORM_EPS) * g


def _sigmoid(x):
    return 1.0 / (1.0 + jnp.exp(-x))


def _dot(a, b):
    return jnp.dot(a, b, preferred_element_type=F32)


def _dot_nt(a, b, precision=None):
    return lax.dot_general(a, b, (((1,), (1,)), ((), ())), precision=precision,
                           preferred_element_type=F32)


KAUG_SEL = MOBA_HEAD_DIM
KAUG_POS = KAUG_SEL + 32
MOBA_MAX_BLOCKS = KAUG_POS - KAUG_SEL


def _bf16_terms(x, n):
    out = []
    for _ in range(n):
        bits = np.float32(x).view(np.uint32)
        kept = np.uint32((int(bits) + 0x7FFF + ((int(bits) >> 16) & 1)) & 0xFFFF0000)
        term = float(kept.view(np.float32))
        out.append(term)
        x -= term
    return tuple(out)


LOG2E = math.log2(math.e)
LOG2E_TERMS = _bf16_terms(LOG2E, 3)


def _even_in_kernel(x_ref, g_ref, wn_ref, wqT_ref, wvT_ref, ext_ref,
                    ka_ref, ub_ref, km_ref, kn_ref, qT_ref, vT_ref):
    tm = x_ref.shape[1]
    xn = _rms(x_ref[0], g_ref[...]).astype(BF16)
    n = _dot(xn, wn_ref[...])
    ka = n[:, :MOBA_HEADS * LANES]
    ka_ref[0] = (ka + ext_ref[...].astype(F32)).astype(BF16)
    ub_ref[0] = n[:, MOBA_HEADS * LANES:]
    for j in range(tm // MOBA_BLOCK):
        km_ref[0, j] = jnp.mean(ka[j * MOBA_BLOCK:(j + 1) * MOBA_BLOCK], axis=0, keepdims=True)
    slot = lax.broadcasted_iota(jnp.int32, (MOBA_HEADS * LANES, LANES), 0) // LANES
    head = lax.broadcasted_iota(jnp.int32, (MOBA_HEADS * LANES, LANES), 1)
    kn_ref[0] = _dot((ka * ka).astype(BF16), (slot == head).astype(BF16))
    qT_ref[0] = _dot_nt(wqT_ref[...], xn)
    vT = _dot_nt(wvT_ref[...], xn)
    for j in range(tm // ATTN_BLOCK):
        vT_ref[0, j] = vT[:, j * ATTN_BLOCK:(j + 1) * ATTN_BLOCK].astype(BF16)


def _moba_key_extras(S, slopes):
    pos = np.arange(S)
    blk, off = pos // MOBA_BLOCK, pos % MOBA_BLOCK
    ext = np.zeros((S, MOBA_HEADS, LANES), np.float32)
    ext[pos, :, KAUG_SEL + blk] = 1.0
    for term in range(len(LOG2E_TERMS)):
        ext[:, :, KAUG_POS + 2 * term] = slopes[None, :] * (MOBA_BLOCK * blk)[:, None]
        ext[:, :, KAUG_POS + 2 * term + 1] = slopes[None, :] * off[:, None]
    return jnp.asarray(ext.reshape(S, MOBA_HEADS * LANES), dtype=BF16)


def _even_in(x, g, w_in, slopes):
    B, S, D = x.shape
    tm = ROW_TILE
    nb = S // MOBA_BLOCK
    wq, wk, wv, wu = (w_in[:, i * MIX_WIDTH:(i + 1) * MIX_WIDTH] for i in range(4))
    wk_slots = jnp.concatenate(
        [wk.reshape(D, MOBA_HEADS, MOBA_HEAD_DIM),
         jnp.zeros((D, MOBA_HEADS, LANES - MOBA_HEAD_DIM), F32)], axis=2).reshape(D, MOBA_HEADS * LANES)
    wn = jnp.concatenate([wk_slots, wu], axis=1).astype(BF16)
    wqT = wq.T.astype(BF16)
    wvT = wv.T.astype(BF16)
    const = lambda b, i: (0, 0)
    return pl.pallas_call(
        _even_in_kernel,
        grid=(B, S // tm),
        in_specs=[
            pl.BlockSpec((1, tm, D), lambda b, i: (b, i, 0)),
            pl.BlockSpec((1, D), const),
            pl.BlockSpec((D, MOBA_HEADS * LANES + MIX_WIDTH), const),
            pl.BlockSpec((MIX_WIDTH, D), const),
            pl.BlockSpec((MIX_WIDTH, D), const),
            pl.BlockSpec((tm, MOBA_HEADS * LANES), lambda b, i: (i, 0)),
        ],
        out_specs=[
            pl.BlockSpec((1, tm, MOBA_HEADS * LANES), lambda b, i: (b, i, 0)),
            pl.BlockSpec((1, tm, MIX_WIDTH), lambda b, i: (b, i, 0)),
            pl.BlockSpec((1, tm // MOBA_BLOCK, 1, MOBA_HEADS * LANES), lambda b, i: (b, i, 0, 0)),
            pl.BlockSpec((1, tm, LANES), lambda b, i: (b, i, 0)),
            pl.BlockSpec((1, MIX_WIDTH, tm), lambda b, i: (b, 0, i)),
            pl.BlockSpec((1, tm // ATTN_BLOCK, MIX_WIDTH, ATTN_BLOCK), lambda b, i: (b, i, 0, 0)),
        ],
        out_shape=[
            jax.ShapeDtypeStruct((B, S, MOBA_HEADS * LANES), BF16),
            jax.ShapeDtypeStruct((B, S, MIX_WIDTH), F32),
            jax.ShapeDtypeStruct((B, nb, 1, MOBA_HEADS * LANES), F32),
            jax.ShapeDtypeStruct((B, S, LANES), F32),
            jax.ShapeDtypeStruct((B, MIX_WIDTH, S), F32),
            jax.ShapeDtypeStruct((B, S // ATTN_BLOCK, MIX_WIDTH, ATTN_BLOCK), BF16),
        ],
        compiler_params=_params(("parallel", "parallel")),
        name="even_in",
    )(x, g.reshape(1, D), wn, wqT, wvT, _moba_key_extras(S, slopes))


def _moba_gate_kernel(km_ref, qT_ref, qa_ref, qn_ref):
    i = pl.program_id(1)
    nb = km_ref.shape[2]
    tq = qT_ref.shape[2]
    row = lax.broadcasted_iota(jnp.int32, (nb, tq), 0)
    own = (i * tq + lax.broadcasted_iota(jnp.int32, (nb, tq), 1)) // MOBA_BLOCK
    past = row < own
    tail_row = lax.broadcasted_iota(jnp.int32, (LANES - KAUG_POS, tq), 0)
    tail = jnp.zeros(tail_row.shape, F32)
    for term, value in enumerate(LOG2E_TERMS):
        tail = jnp.where(tail_row // 2 == term, F32(value), tail)
    tail = tail.astype(BF16)
    pad = jnp.zeros((MOBA_MAX_BLOCKS - nb, tq), BF16) if nb < MOBA_MAX_BLOCKS else None
    for h in range(MOBA_HEADS):
        q_h = qT_ref[0, h * MOBA_HEAD_DIM:(h + 1) * MOBA_HEAD_DIM, :]
        qn_ref[0, h:h + 1, :] = jnp.sum(q_h * q_h, axis=0, keepdims=True)
        gate = jnp.dot(km_ref[0, h], q_h, precision=HIGHEST, preferred_element_type=F32)
        gate = jnp.where(past, gate, NEG_INF)
        chosen = jnp.zeros(gate.shape, F32)
        for _ in range(MOBA_TOPK):
            mx = jnp.max(gate, axis=0, keepdims=True)
            first = jnp.min(jnp.where(gate == mx, row, nb), axis=0, keepdims=True)
            pick = row == first
            chosen = jnp.where(pick, 1.0, chosen)
            gate = jnp.where(pick, -jnp.inf, gate)
        keep = jnp.where(past, chosen, (row == own).astype(F32))
        sel = jnp.where(keep > 0.0, 0.0, NEG_INF).astype(BF16)
        base = h * LANES
        qa_ref[0, base:base + KAUG_SEL] = (q_h * (MOBA_HEAD_DIM ** -0.5 * LOG2E)).astype(BF16)
        qa_ref[0, base + KAUG_SEL:base + KAUG_SEL + nb] = sel
        if pad is not None:
            qa_ref[0, base + KAUG_SEL + nb:base + KAUG_POS] = pad
        qa_ref[0, base + KAUG_POS:base + LANES] = tail


def _moba_gate(kmean, qT):
    B, H, nb, dh = kmean.shape
    S = qT.shape[2]
    tq = ATTN_BLOCK
    return pl.pallas_call(
        _moba_gate_kernel,
        grid=(B, S // tq),
        in_specs=[
            pl.BlockSpec((1, H, nb, dh), lambda b, i: (b, 0, 0, 0)),
            pl.BlockSpec((1, H * dh, tq), lambda b, i: (b, 0, i)),
        ],
        out_specs=[pl.BlockSpec((1, H * LANES, tq), lambda b, i: (b, 0, i)),
                   pl.BlockSpec((1, H, tq), lambda b, i: (b, 0, i))],
        out_shape=[jax.ShapeDtypeStruct((B, H * LANES, S), BF16), jax.ShapeDtypeStruct((B, H, S), F32)],
        compiler_params=_params(("parallel", "parallel")),
        name="moba_gate",
    )(kmean, qT)


HEADS_PER_STEP = 4
SUM_ROWS = 16


def _attn_kernel(lo_ref, q_ref, k_ref, v_ref, o_ref, sa_ref, sb_ref, m_ref, acc_ref):
    i = pl.program_id(2)
    lo = lo_ref[(pl.program_id(0) * pl.num_programs(1) + pl.program_id(1)) * pl.num_programs(2) + i]
    n_past = i - lo
    tq = q_ref.shape[2]
    tk = ATTN_BLOCK
    hp = HEADS_PER_STEP
    dv = v_ref.shape[2] // hp
    m_ref[...] = jnp.full(m_ref.shape, NEG_INF, F32)
    acc_ref[...] = jnp.zeros(acc_ref.shape, F32)
    ones_rows = jnp.ones((SUM_ROWS, tk), BF16)

    def scores(kvt, s_ref, diag):
        start = pl.multiple_of(kvt * tk, tk)
        k_tile = k_ref[0, pl.ds(start, tk), :]
        for g in range(hp):
            s = _dot(k_tile[:, g * LANES:(g + 1) * LANES], q_ref[0, g * LANES:(g + 1) * LANES, :])
            if diag:
                key = lax.broadcasted_iota(jnp.int32, (tk, tq), 0)
                qry = lax.broadcasted_iota(jnp.int32, (tk, tq), 1)
                s = jnp.where(key <= qry, s, NEG_INF)
            s_ref[g] = s

    def consume(kvt, s_ref):
        v_tile = v_ref[0, kvt]
        for g in range(hp):
            s = s_ref[g]
            m_run = m_ref[g]
            m_new = jnp.maximum(m_run, jnp.max(s, axis=0, keepdims=True))
            p = jnp.exp2(s - m_new).astype(BF16)
            v_aug = jnp.concatenate([v_tile[g * dv:(g + 1) * dv], ones_rows], axis=0)
            acc_ref[g] = jnp.exp2(m_run - m_new) * acc_ref[g] + _dot(v_aug, p)
            m_ref[g] = m_new

    tile_at = lambda t: jnp.where(t == 0, i, lo + t - 1)
    scores(i, sa_ref, True)

    def pair(p, carry):
        t = 2 * p
        scores(tile_at(t + 1), sb_ref, False)
        consume(tile_at(t), sa_ref)
        scores(tile_at(t + 2), sa_ref, False)
        consume(tile_at(t + 1), sb_ref)
        return carry

    lax.fori_loop(0, n_past // 2, pair, 0)
    last = 2 * (n_past // 2)

    @pl.when(n_past % 2 == 1)
    def _():
        scores(tile_at(last + 1), sb_ref, False)
        consume(tile_at(last), sa_ref)
        consume(tile_at(last + 1), sb_ref)

    @pl.when(n_past % 2 == 0)
    def _():
        consume(tile_at(last), sa_ref)

    for g in range(hp):
        o_ref[0, g * dv:(g + 1) * dv, :] = (acc_ref[g, :dv] / acc_ref[g, dv:dv + 1]).astype(o_ref.dtype)


def _attention(qT, k, vT, dv, heads, name, first_tile=None):
    B, _, S = qT.shape
    tq = ATTN_BLOCK
    nt = S // ATTN_BLOCK
    hp = HEADS_PER_STEP
    if first_tile is None:
        first_tile = jnp.zeros((B, heads // hp, nt), jnp.int32)
    return pl.pallas_call(
        _attn_kernel,
        grid_spec=pltpu.PrefetchScalarGridSpec(
            num_scalar_prefetch=1,
            grid=(B, heads // hp, S // tq),
            in_specs=[
                pl.BlockSpec((1, hp * LANES, tq), lambda b, h, i, lo: (b, h, i)),
                pl.BlockSpec((1, S, hp * LANES), lambda b, h, i, lo: (b, 0, h)),
                pl.BlockSpec((1, nt, hp * dv, ATTN_BLOCK), lambda b, h, i, lo: (b, 0, h, 0)),
            ],
            out_specs=pl.BlockSpec((1, hp * dv, tq), lambda b, h, i, lo: (b, h, i)),
            scratch_shapes=[pltpu.VMEM((hp, ATTN_BLOCK, tq), F32), pltpu.VMEM((hp, ATTN_BLOCK, tq), F32),
                            pltpu.VMEM((hp, 1, tq), F32), pltpu.VMEM((hp, dv + SUM_ROWS, tq), F32)],
        ),
        out_shape=jax.ShapeDtypeStruct((B, heads * dv, S), BF16),
        compiler_params=_params(("parallel", "parallel", "arbitrary")),
        name=name,
    )(first_tile.reshape(-1), qT, k, vT)


def _pool_kernel(x_ref, halo_ref, w_ref, sc_ref, o_ref, xs_ref):
    i = pl.program_id(1)
    tm = x_ref.shape[1]
    x = x_ref[0]
    xs_ref[0:POOL_HALO] = jnp.where(i > 0, halo_ref[0], 0.0)
    xs_ref[POOL_HALO:POOL_HALO + tm] = x
    t = i * tm + lax.broadcasted_iota(jnp.int32, (tm, 1), 0)
    outs = []
    for g, win in enumerate(POOL_WINDOWS):
        lo, hi = g * POOL_GROUP_DIM, (g + 1) * POOL_GROUP_DIM
        xg = x[:, lo:hi]
        acc = xg
        for d in range(1, win):
            acc = acc + xs_ref[POOL_HALO - d:POOL_HALO - d + tm, lo:hi]
        count = jnp.minimum(t + 1, win).astype(F32)
        outs.append(_dot((acc / count - xg).astype(BF16), w_ref[g]))
    o_ref[0] = (jnp.concatenate(outs, axis=1) * sc_ref[...]).astype(o_ref.dtype)


def _pool(ub, pool_w, pool_scale):
    B, S, W = ub.shape
    tm = ROW_TILE
    per = tm // POOL_HALO
    return pl.pallas_call(
        _pool_kernel,
        grid=(B, S // tm),
        in_specs=[
            pl.BlockSpec((1, tm, W), lambda b, i: (b, i, 0)),
            pl.BlockSpec((1, POOL_HALO, W), lambda b, i: (b, jnp.maximum(i * per - 1, 0), 0)),
            pl.BlockSpec(pool_w.shape, lambda b, i: (0, 0, 0)),
            pl.BlockSpec((1, W), lambda b, i: (0, 0)),
        ],
        out_specs=pl.BlockSpec((1, tm, W), lambda b, i: (b, i, 0)),
        out_shape=jax.ShapeDtypeStruct((B, S, W), BF16),
        scratch_shapes=[pltpu.VMEM((POOL_HALO + tm, W), F32)],
        compiler_params=_params(("parallel", "parallel")),
        name="pool",
    )(ub, ub, pool_w.astype(BF16), pool_scale.reshape(1, W))


def _mix_out_kernel(h_ref, aT_ref, b_ref, w_ref, o_ref, *, a_first):
    a = aT_ref[0].astype(F32).T.astype(BF16)
    b = b_ref[0].astype(BF16)
    lo, hi = (a, b) if a_first else (b, a)
    y = _dot(lo, w_ref[:MIX_WIDTH]) + _dot(hi, w_ref[MIX_WIDTH:])
    o_ref[0] = h_ref[0] + y


def _mix_out(h, aT, b, w_out, a_first):
    B, S, D = h.shape
    tm = ROW_TILE
    return pl.pallas_call(
        functools.partial(_mix_out_kernel, a_first=a_first),
        grid=(B, S // tm),
        in_specs=[
            pl.BlockSpec((1, tm, D), lambda b_, i: (b_, i, 0)),
            pl.BlockSpec((1, MIX_WIDTH, tm), lambda b_, i: (b_, 0, i)),
            pl.BlockSpec((1, tm, MIX_WIDTH), lambda b_, i: (b_, i, 0)),
            pl.BlockSpec((2 * MIX_WIDTH, D), lambda b_, i: (0, 0)),
        ],
        out_specs=pl.BlockSpec((1, tm, D), lambda b_, i: (b_, i, 0)),
        out_shape=jax.ShapeDtypeStruct((B, S, D), F32),
        compiler_params=_params(("parallel", "parallel")),
        name="mix_out",
    )(h, aT, b, w_out.astype(BF16))


def _swiglu_step(xn, wg, wu, wd):
    gt = _dot(xn, wg)
    up = _dot(xn, wu)
    return _dot((gt * _sigmoid(gt) * up).astype(BF16), wd)


def _ffn_kernel(h_ref, g_ref, wg_ref, wu_ref, wd_ref, o_ref):
    h = h_ref[...]
    xn = _rms(h, g_ref[...]).astype(BF16)
    o_ref[...] = h + _swiglu_step(xn, wg_ref[...], wu_ref[...], wd_ref[...])


def _ffn(h2d, g, wg, wu, wd, tm=ROW_TILE):
    T, D = h2d.shape
    F = wg.shape[1]
    resident = lambda shape: pl.BlockSpec(shape, lambda i: (0, 0), pipeline_mode=pl.Buffered(1))
    return pl.pallas_call(
        _ffn_kernel,
        grid=(T // tm,),
        in_specs=[
            pl.BlockSpec((tm, D), lambda i: (i, 0)),
            pl.BlockSpec((1, D), lambda i: (0, 0)),
            resident((D, F)),
            resident((D, F)),
            resident((F, D)),
        ],
        out_specs=pl.BlockSpec((tm, D), lambda i: (i, 0)),
        out_shape=jax.ShapeDtypeStruct((T, D), F32),
        compiler_params=_params(("parallel",)),
        name="ffn",
    )(h2d, g.reshape(1, D), wg, wu, wd)


ROUTE_E0, ROUTE_E1, ROUTE_W0, ROUTE_W1, ROUTE_R0, ROUTE_R1 = range(6)
MOE_TILE = 512
MOE_TF = EXPERT_DIM // 2


def _lane_pick(tile, lane, idx):
    return jnp.sum(jnp.where(lane == idx, tile, 0.0), axis=1, keepdims=True)


ROUTER_ROWS = 16


def _router_kernel(h_ref, g_ref, wT_ref, b_ref, route_ref, cnt_ref):
    tm = h_ref.shape[0]

    @pl.when(pl.program_id(0) == 0)
    def _():
        cnt_ref[...] = jnp.zeros_like(cnt_ref)

    xn = _rms(h_ref[...], g_ref[...])
    logits = _dot_nt(wT_ref[...], xn, precision=HIGHEST) + b_ref[...]
    row = lax.broadcasted_iota(jnp.int32, logits.shape, 0)
    logits = jnp.where(row < N_EXPERTS, logits, -jnp.inf)
    v0 = jnp.max(logits, axis=0, keepdims=True)
    i0 = jnp.min(jnp.where(logits == v0, row, ROUTER_ROWS), axis=0, keepdims=True)
    rest = jnp.where(row == i0, -jnp.inf, logits)
    v1 = jnp.max(rest, axis=0, keepdims=True)
    i1 = jnp.min(jnp.where(rest == v1, row, ROUTER_ROWS), axis=0, keepdims=True)
    e1 = jnp.exp(v1 - v0)
    w0 = 1.0 / (1.0 + e1)
    sel = (row == i0).astype(F32) + (row == i1).astype(F32)
    earlier = (lax.broadcasted_iota(jnp.int32, (tm, tm), 0)
               < lax.broadcasted_iota(jnp.int32, (tm, tm), 1))
    counts = cnt_ref[:, 0:1]
    rank = _dot(sel.astype(BF16), earlier.astype(BF16)) + counts
    cnt_ref[...] = jnp.broadcast_to(counts + jnp.sum(sel, axis=1, keepdims=True), cnt_ref.shape)
    pick = lambda idx: jnp.sum(jnp.where(row == idx, rank, 0.0), axis=0, keepdims=True)
    rows = [None] * 6
    rows[ROUTE_E0], rows[ROUTE_E1] = i0.astype(F32), i1.astype(F32)
    rows[ROUTE_W0], rows[ROUTE_W1] = w0, e1 * w0
    rows[ROUTE_R0], rows[ROUTE_R1] = pick(i0), pick(i1)
    routeT = jnp.concatenate(rows + [jnp.zeros((LANES - len(rows), tm), F32)], axis=0)
    route_ref[...] = routeT.T


def _router(h2d, g, router_w, router_b):
    T, D = h2d.shape
    tm = ROW_TILE
    wT = jnp.zeros((ROUTER_ROWS, D), F32).at[:N_EXPERTS].set(router_w.T)
    b = jnp.zeros((ROUTER_ROWS, 1), F32).at[:N_EXPERTS, 0].set(router_b)
    return pl.pallas_call(
        _router_kernel,
        grid=(T // tm,),
        in_specs=[
            pl.BlockSpec((tm, D), lambda i: (i, 0)),
            pl.BlockSpec((1, D), lambda i: (0, 0)),
            pl.BlockSpec((ROUTER_ROWS, D), lambda i: (0, 0)),
            pl.BlockSpec((ROUTER_ROWS, 1), lambda i: (0, 0)),
        ],
        out_specs=[pl.BlockSpec((tm, LANES), lambda i: (i, 0)),
                   pl.BlockSpec((ROUTER_ROWS, LANES), lambda i: (0, 0))],
        out_shape=[jax.ShapeDtypeStruct((T, LANES), F32), jax.ShapeDtypeStruct((ROUTER_ROWS, LANES), F32)],
        compiler_params=_params(("arbitrary",)),
        name="router",
    )(h2d, g.reshape(1, D), wT, b)


def _row_copies(pos_ref, base, r, src_of, dst_of, sem):
    return [pltpu.make_async_copy(src_of(k, pos_ref[base + 2 * r + k]),
                                  dst_of(k, pos_ref[base + 2 * r + k]), sem) for k in range(2)]


def _all_rows(tm, make):
    def issue(r, c):
        for k, cp in enumerate(make(r)):
            cp.start(priority=k)
        return c

    def drain(r, c):
        for cp in make(r):
            cp.wait()
        return c

    lax.fori_loop(0, tm, issue, 0, unroll=8)
    lax.fori_loop(0, tm, drain, 0, unroll=8)


def _dispatch_kernel(pos_ref, h_ref, g_ref, init_ref, xs_ref, xn_ref, sem):
    del init_ref
    tm = h_ref.shape[0]
    xn_ref[...] = _rms(h_ref[...], g_ref[...])
    base = pl.program_id(0) * (2 * tm)
    _all_rows(tm, lambda r: _row_copies(
        pos_ref, base, r, lambda k, p: xn_ref.at[pl.ds(r, 1)], lambda k, p: xs_ref.at[pl.ds(p, 1)], sem))


def _dispatch(pos, h2d, g, n_rows):
    T, D = h2d.shape
    tm = ROW_TILE
    return pl.pallas_call(
        _dispatch_kernel,
        grid_spec=pltpu.PrefetchScalarGridSpec(
            num_scalar_prefetch=1,
            grid=(T // tm,),
            in_specs=[pl.BlockSpec((tm, D), lambda i, pos: (i, 0)),
                      pl.BlockSpec((1, D), lambda i, pos: (0, 0)),
                      pl.BlockSpec(memory_space=pl.ANY)],
            out_specs=pl.BlockSpec(memory_space=pl.ANY),
            scratch_shapes=[pltpu.VMEM((tm, D), F32), pltpu.SemaphoreType.DMA(())],
        ),
        out_shape=jax.ShapeDtypeStruct((n_rows, D), F32),
        input_output_aliases={3: 0},
        compiler_params=_params(("arbitrary",)),
        name="moe_dispatch",
    )(pos, h2d, g.reshape(1, D), jnp.zeros((n_rows, D), F32))


def _moe_ffn_kernel(te_ref, nv_ref, x_ref, wg_ref, wu_ref, wd_ref, o_ref, xb_ref, acc_ref):
    j = pl.program_id(0)
    f = pl.program_id(1)
    valid = j < nv_ref[0]

    @pl.when(f == 0)
    def _():
        xb_ref[...] = x_ref[...].astype(BF16)
        acc_ref[...] = jnp.zeros_like(acc_ref)

    @pl.when(valid)
    def _():
        acc_ref[...] += _swiglu_step(xb_ref[...], wg_ref[0], wu_ref[0], wd_ref[0])

    @pl.when(f == pl.num_programs(1) - 1)
    def _():
        o_ref[...] = acc_ref[...]


def _moe_ffn(tile_expert, n_valid, xs, wg, wu, wd):
    N, D = xs.shape
    tm, tf = MOE_TILE, MOE_TF
    F = wg.shape[2]
    return pl.pallas_call(
        _moe_ffn_kernel,
        grid_spec=pltpu.PrefetchScalarGridSpec(
            num_scalar_prefetch=2,
            grid=(N // tm, F // tf),
            in_specs=[pl.BlockSpec((tm, D), lambda j, f, te, nv: (j, 0)),
                      pl.BlockSpec((1, D, tf), lambda j, f, te, nv: (te[j], 0, f)),
                      pl.BlockSpec((1, D, tf), lambda j, f, te, nv: (te[j], 0, f)),
                      pl.BlockSpec((1, tf, D), lambda j, f, te, nv: (te[j], f, 0))],
            out_specs=pl.BlockSpec((tm, D), lambda j, f, te, nv: (j, 0)),
            scratch_shapes=[pltpu.VMEM((tm, D), BF16), pltpu.VMEM((tm, D), F32)],
        ),
        out_shape=jax.ShapeDtypeStruct((N, D), F32),
        compiler_params=_params(("arbitrary", "arbitrary")),
        name="moe_ffn",
    )(tile_expert, n_valid, xs, wg, wu, wd)


def _combine_kernel(pos_ref, h_ref, route_ref, ys_ref, o_ref, y_ref, sem):
    tm = h_ref.shape[0]
    base = pl.program_id(0) * (2 * tm)
    _all_rows(tm, lambda r: _row_copies(
        pos_ref, base, r, lambda k, p: ys_ref.at[pl.ds(p, 1)], lambda k, p: y_ref.at[k, pl.ds(r, 1)], sem))
    route = route_ref[...]
    lane = lax.broadcasted_iota(jnp.int32, route.shape, 1)
    o_ref[...] = (h_ref[...] + _lane_pick(route, lane, ROUTE_W0) * y_ref[0]
                  + _lane_pick(route, lane, ROUTE_W1) * y_ref[1])


def _combine(pos, h2d, route, ys):
    T, D = h2d.shape
    tm = ROW_TILE
    return pl.pallas_call(
        _combine_kernel,
        grid_spec=pltpu.PrefetchScalarGridSpec(
            num_scalar_prefetch=1,
            grid=(T // tm,),
            in_specs=[pl.BlockSpec((tm, D), lambda i, pos: (i, 0)),
                      pl.BlockSpec((tm, LANES), lambda i, pos: (i, 0)),
                      pl.BlockSpec(memory_space=pl.ANY)],
            out_specs=pl.BlockSpec((tm, D), lambda i, pos: (i, 0)),
            scratch_shapes=[pltpu.VMEM((2, tm, D), F32), pltpu.SemaphoreType.DMA(())],
        ),
        out_shape=jax.ShapeDtypeStruct((T, D), F32),
        compiler_params=_params(("arbitrary",)),
        name="moe_combine",
    )(pos, h2d, route, ys)


def _moe(h2d, g, router_w, router_b, wg, wu, wd):
    T, D = h2d.shape
    tm = MOE_TILE
    route, counts = _router(h2d, g, router_w, router_b)
    cnt = counts[:N_EXPERTS, 0].astype(jnp.int32)
    padded = (cnt + tm - 1) // tm * tm
    ends = jnp.cumsum(padded)
    start = ends - padded
    e01 = route[:, ROUTE_E0:ROUTE_E1 + 1].astype(jnp.int32)
    r01 = route[:, ROUTE_R0:ROUTE_R1 + 1].astype(jnp.int32)
    pos = (start[e01] + r01).reshape(2 * T)
    n_rows = 2 * T + N_EXPERTS * tm
    tile_row = jnp.arange(n_rows // tm, dtype=jnp.int32) * tm
    tile_expert = jnp.minimum(jnp.sum(tile_row[:, None] >= ends[None, :], axis=1), N_EXPERTS - 1).astype(jnp.int32)
    n_valid = (ends[-1:] // tm).astype(jnp.int32)
    xs = _dispatch(pos, h2d, g, n_rows)
    ys = _moe_ffn(tile_expert, n_valid, xs, wg, wu, wd)
    return _combine(pos, h2d, route, ys)


def _ple_kernel(*refs, final):
    if final:
        h_ref, p_ref, g_ref, wg_ref, wp_ref, fg_ref, o_ref = refs
    else:
        h_ref, p_ref, g_ref, wg_ref, wp_ref, o_ref = refs
    h = h_ref[...]
    gate = _sigmoid(_dot(_rms(h, g_ref[...]).astype(BF16), wg_ref[...]))
    out = h + gate * _dot(p_ref[...].astype(BF16), wp_ref[...])
    if final:
        out = _rms(out, fg_ref[...])
    o_ref[...] = out


def _ple(h2d, p2d, g, w_gate, w_proj, final_g=None):
    T, D = h2d.shape
    tm = ROW_TILE
    final = final_g is not None
    in_specs = [
        pl.BlockSpec((tm, D), lambda i: (i, 0)),
        pl.BlockSpec((tm, PLE_DIM), lambda i: (i, 0)),
        pl.BlockSpec((1, D), lambda i: (0, 0)),
        pl.BlockSpec((D, D), lambda i: (0, 0)),
        pl.BlockSpec((PLE_DIM, D), lambda i: (0, 0)),
    ]
    args = [h2d, p2d, g.reshape(1, D), w_gate.astype(BF16), w_proj.astype(BF16)]
    if final:
        in_specs.append(pl.BlockSpec((1, D), lambda i: (0, 0)))
        args.append(final_g.reshape(1, D))
    return pl.pallas_call(
        functools.partial(_ple_kernel, final=final),
        grid=(T // tm,),
        in_specs=in_specs,
        out_specs=pl.BlockSpec((tm, D), lambda i: (i, 0)),
        out_shape=jax.ShapeDtypeStruct((T, D), F32),
        compiler_params=_params(("parallel",)),
        name="ple_final" if final else "ple",
    )(*args)


ODD_MAIN = 4 * MIX_WIDTH
ODD_COLS = ODD_MAIN + MLA_Q_RANK + MLA_KV_RANK + 2 * LANES
MLA_QK_SCALE = (MLA_NOPE_DIM + MLA_ROPE_DIM) ** -0.5 * LOG2E
ROPE_HALF = MLA_ROPE_DIM // 2


def _odd_in_kernel(x_ref, g_ref, wn_ref, qn_ref, wuqT_ref, kvn_ref, wk2_ref, wvT_ref,
                   cosT_ref, sinT_ref, cc_ref, ss_ref,
                   qk_ref, vc_ref, op_ref, misc_ref, mq_ref, mk_ref, mv_ref):
    tm = x_ref.shape[1]
    xn = _rms(x_ref[0], g_ref[...]).astype(BF16)
    u = _dot(xn, wn_ref[...])
    qk_ref[0] = u[:, :2 * MIX_WIDTH]
    vc_ref[0] = u[:, 2 * MIX_WIDTH:3 * MIX_WIDTH].astype(BF16)
    op_ref[0] = u[:, 3 * MIX_WIDTH:ODD_MAIN]
    c0 = ODD_MAIN
    c_q = u[:, c0:c0 + MLA_Q_RANK]
    c0 += MLA_Q_RANK
    c_kv = u[:, c0:c0 + MLA_KV_RANK]
    c0 += MLA_KV_RANK
    misc = u[:, c0:c0 + LANES]
    misc_sw = u[:, c0 + LANES:c0 + 2 * LANES]
    misc_ref[0] = misc
    cqn = _rms(c_q, qn_ref[...]).astype(BF16)
    qT = _dot_nt(wuqT_ref[...], cqn)
    cosT = cosT_ref[...]
    sinT = sinT_ref[...]
    for h in range(MLA_HEADS):
        r = h * LANES
        mq_ref[0, r:r + MLA_NOPE_DIM] = (qT[r:r + MLA_NOPE_DIM] * MLA_QK_SCALE).astype(BF16)
        x1 = qT[r + MLA_NOPE_DIM:r + MLA_NOPE_DIM + ROPE_HALF]
        x2 = qT[r + MLA_NOPE_DIM + ROPE_HALF:r + MLA_NOPE_DIM + MLA_ROPE_DIM]
        mq_ref[0, r + MLA_NOPE_DIM:r + MLA_NOPE_DIM + ROPE_HALF] = (
            (x1 * cosT - x2 * sinT) * MLA_QK_SCALE).astype(BF16)
        mq_ref[0, r + MLA_NOPE_DIM + ROPE_HALF:r + MLA_NOPE_DIM + MLA_ROPE_DIM] = (
            (x1 * sinT + x2 * cosT) * MLA_QK_SCALE).astype(BF16)
        mq_ref[0, r + MLA_NOPE_DIM + MLA_ROPE_DIM:r + LANES] = jnp.zeros(
            (LANES - MLA_NOPE_DIM - MLA_ROPE_DIM, tm), BF16)
    ckvn = _rms(c_kv, kvn_ref[...]).astype(BF16)
    k_rot = (misc * cc_ref[...] + misc_sw * ss_ref[...]).astype(BF16)
    mk_ref[0] = _dot(jnp.concatenate([ckvn, k_rot], axis=1), wk2_ref[...]).astype(BF16)
    vT = _dot_nt(wvT_ref[...], ckvn)
    for j in range(tm // ATTN_BLOCK):
        mv_ref[0, j] = vT[:, j * ATTN_BLOCK:(j + 1) * ATTN_BLOCK].astype(BF16)


def _rope_tables(S):
    inv_freq = ROPE_BASE ** (-jnp.arange(ROPE_HALF, dtype=F32) / ROPE_HALF)
    ang = jnp.arange(S, dtype=F32)[:, None] * inv_freq[None, :]
    cos, sin = jnp.cos(ang), jnp.sin(ang)
    pad = jnp.zeros((S, LANES - MLA_ROPE_DIM), F32)
    cc = jnp.concatenate([cos, cos, pad], axis=1)
    ss = jnp.concatenate([-sin, sin, pad], axis=1)
    return cos.T, sin.T, cc, ss


def _odd_in(x, g, w_in, q_norm, w_uq, kv_norm, w_ukv):
    B, S, D = x.shape
    tm = ROW_TILE
    cuts = np.cumsum([MIX_WIDTH] * 4 + [MLSTM_HEADS, MLSTM_HEADS, MLA_Q_RANK, MLA_KV_RANK]).tolist()
    w_main = w_in[:, :cuts[3]]
    w_i = w_in[:, cuts[3]:cuts[4]]
    w_f = w_in[:, cuts[4]:cuts[5]]
    w_cq = w_in[:, cuts[5]:cuts[6]]
    w_ckv = w_in[:, cuts[6]:cuts[7]]
    w_kr = w_in[:, cuts[7]:]
    w_kr_sw = jnp.concatenate([w_kr[:, ROPE_HALF:], w_kr[:, :ROPE_HALF]], axis=1)
    zpad = lambda n: jnp.zeros((D, n), F32)
    w_misc = jnp.concatenate([w_kr, w_i, w_f, zpad(LANES - MLA_ROPE_DIM - 2 * MLSTM_HEADS)], axis=1)
    w_misc_sw = jnp.concatenate([w_kr_sw, zpad(LANES - MLA_ROPE_DIM)], axis=1)
    wn = jnp.concatenate([w_main, w_cq, w_ckv, w_misc, w_misc_sw], axis=1).astype(BF16)
    qd = MLA_NOPE_DIM + MLA_ROPE_DIM
    w_uq_h = w_uq.reshape(MLA_Q_RANK, MLA_HEADS, qd)
    w_uq_h = jnp.concatenate([w_uq_h, jnp.zeros((MLA_Q_RANK, MLA_HEADS, LANES - qd), F32)], axis=2)
    wuqT = w_uq_h.reshape(MLA_Q_RANK, MLA_HEADS * LANES).T.astype(BF16)
    w_ukv_h = w_ukv.reshape(MLA_KV_RANK, MLA_HEADS, MLA_NOPE_DIM + MLA_V_DIM)
    w_k = jnp.concatenate([w_ukv_h[:, :, :MLA_NOPE_DIM],
                           jnp.zeros((MLA_KV_RANK, MLA_HEADS, LANES - MLA_NOPE_DIM), F32)], axis=2)
    place = jnp.zeros((LANES, MLA_HEADS, LANES), F32)
    eye = jnp.eye(MLA_ROPE_DIM, dtype=F32)
    place = place.at[:MLA_ROPE_DIM, :, MLA_NOPE_DIM:MLA_NOPE_DIM + MLA_ROPE_DIM].set(
        jnp.broadcast_to(eye[:, None, :], (MLA_ROPE_DIM, MLA_HEADS, MLA_ROPE_DIM)))
    wk2 = jnp.concatenate([w_k, place], axis=0).reshape(MLA_KV_RANK + LANES, MLA_HEADS * LANES).astype(BF16)
    wvT = w_ukv_h[:, :, MLA_NOPE_DIM:].reshape(MLA_KV_RANK, MLA_HEADS * MLA_V_DIM).T.astype(BF16)
    cosT, sinT, cc, ss = _rope_tables(S)
    row = lambda b, i: (b, i, 0)
    const = lambda b, i: (0, 0)
    nb = S // ATTN_BLOCK
    return pl.pallas_call(
        _odd_in_kernel,
        grid=(B, S // tm),
        in_specs=[
            pl.BlockSpec((1, tm, D), row),
            pl.BlockSpec((1, D), const),
            pl.BlockSpec((D, ODD_COLS), const),
            pl.BlockSpec((1, MLA_Q_RANK), const),
            pl.BlockSpec((MLA_HEADS * LANES, MLA_Q_RANK), const),
            pl.BlockSpec((1, MLA_KV_RANK), const),
            pl.BlockSpec((MLA_KV_RANK + LANES, MLA_HEADS * LANES), const),
            pl.BlockSpec((MLA_HEADS * MLA_V_DIM, MLA_KV_RANK), const),
            pl.BlockSpec((ROPE_HALF, tm), lambda b, i: (0, i)),
            pl.BlockSpec((ROPE_HALF, tm), lambda b, i: (0, i)),
            pl.BlockSpec((tm, LANES), lambda b, i: (i, 0)),
            pl.BlockSpec((tm, LANES), lambda b, i: (i, 0)),
        ],
        out_specs=[
            pl.BlockSpec((1, tm, 2 * MIX_WIDTH), row),
            pl.BlockSpec((1, tm, MIX_WIDTH), row),
            pl.BlockSpec((1, tm, MIX_WIDTH), row),
            pl.BlockSpec((1, tm, LANES), row),
            pl.BlockSpec((1, MLA_HEADS * LANES, tm), lambda b, i: (b, 0, i)),
            pl.BlockSpec((1, tm, MLA_HEADS * LANES), row),
            pl.BlockSpec((1, tm // ATTN_BLOCK, MLA_HEADS * MLA_V_DIM, ATTN_BLOCK), lambda b, i: (b, i, 0, 0)),
        ],
        out_shape=[
            jax.ShapeDtypeStruct((B, S, 2 * MIX_WIDTH), F32),
            jax.ShapeDtypeStruct((B, S, MIX_WIDTH), BF16),
            jax.ShapeDtypeStruct((B, S, MIX_WIDTH), F32),
            jax.ShapeDtypeStruct((B, S, LANES), F32),
            jax.ShapeDtypeStruct((B, MLA_HEADS * LANES, S), BF16),
            jax.ShapeDtypeStruct((B, S, MLA_HEADS * LANES), BF16),
            jax.ShapeDtypeStruct((B, nb, MLA_HEADS * MLA_V_DIM, ATTN_BLOCK), BF16),
        ],
        compiler_params=_params(("parallel", "parallel")),
        name="odd_in",
    )(x, g.reshape(1, D), wn, q_norm.reshape(1, -1), wuqT, kv_norm.reshape(1, -1), wk2, wvT,
      cosT, sinT, cc, ss)


def _log_sigmoid(x):
    return jnp.minimum(x, 0.0) - jnp.log(1.0 + jnp.exp(-jnp.abs(x)))


def _mlstm_kernel(qk_ref, v_ref, op_ref, misc_ref, cw_ref, gb_ref, hn_ref, o_ref,
                  prev_ref, cn_ref, m_ref):
    c = pl.program_id(1)
    L = qk_ref.shape[1]
    row = lax.broadcasted_iota(jnp.int32, (L, 1), 0)
    lane = lax.broadcasted_iota(jnp.int32, (L, LANES), 1)
    is_f = (lane >= MISC_F) & (lane < MISC_F + MLSTM_HEADS)
    causal = lax.broadcasted_iota(jnp.int32, (L, L), 1) <= lax.broadcasted_iota(jnp.int32, (L, L), 0)

    @pl.when(c == 0)
    def _():
        prev_ref[...] = jnp.zeros_like(prev_ref)
        cn_ref[...] = jnp.zeros_like(cn_ref)
        m_ref[...] = jnp.zeros_like(m_ref)

    seqs = range(qk_ref.shape[0])
    heads = [(bb, h) for bb in seqs for h in range(MLSTM_HEADS)]
    d = MLSTM_HEAD_DIM

    qk = []
    for bb in seqs:
        x = qk_ref[bb]
        prev = prev_ref[bb]
        conv = x * cw_ref[CONV_WIDTH - 1:CONV_WIDTH, :]
        for j in range(1, CONV_WIDTH):
            shifted = pltpu.roll(jnp.where(row >= L - j, prev, x), j, axis=0)
            conv = conv + shifted * cw_ref[CONV_WIDTH - 1 - j:CONV_WIDTH - j, :]
        prev_ref[bb] = x
        qk.append(conv * _sigmoid(conv))

    sel_r = lax.broadcasted_iota(jnp.int32, (LANES, 2 * MIX_WIDTH), 0)
    sel_c = lax.broadcasted_iota(jnp.int32, (LANES, 2 * MIX_WIDTH), 1)
    spread = (sel_r == MISC_I + sel_c // LANES).astype(F32)
    pick = (lax.broadcasted_iota(jnp.int32, (8, LANES), 1)
            == MISC_I + lax.broadcasted_iota(jnp.int32, (8, LANES), 0)).astype(F32)
    mean_mat = jnp.full((d, d), 1.0 / d, F32)
    cols, rows = [], []
    for bb in seqs:
        gates = misc_ref[bb] + gb_ref[...]
        z = jnp.where(is_f, _log_sigmoid(gates), gates)
        cum = jnp.dot(causal.astype(F32), z, precision=HIGHEST, preferred_element_type=F32)
        z = jnp.where(is_f, cum, z)
        cols.append(jnp.dot(z, spread, precision=HIGHEST, preferred_element_type=F32))
        rows.append(_dot_nt(pick, z, precision=HIGHEST))

    q, k, v_aug, i_b, b_b, m_prev, m_t, w_inter, scores, inter = ({} for _ in range(10))
    ones_blk = jnp.ones((L, LANES), BF16)
    for key in heads:
        bb, h = key
        lo, hi = h * d, (h + 1) * d
        q[key] = qk[bb][:, lo:hi].astype(BF16)
        k[key] = qk[bb][:, MIX_WIDTH + lo:MIX_WIDTH + hi] * (d ** -0.5)
        v_aug[key] = jnp.concatenate([v_ref[bb, :, lo:hi], ones_blk], axis=1)
        scores[key] = _dot_nt(q[key], k[key].astype(BF16))
        inter[key] = _dot(q[key], cn_ref[bb, h].astype(BF16))
    intra = {}
    for key in heads:
        bb, h = key
        i_b[key] = cols[bb][:, h * LANES:(h + 1) * LANES]
        b_b[key] = cols[bb][:, (MLSTM_HEADS + h) * LANES:(MLSTM_HEADS + h + 1) * LANES]
        i_row = rows[bb][h:h + 1, :]
        b_row = rows[bb][MLSTM_HEADS + h:MLSTM_HEADS + h + 1, :]
        m_prev[key] = m_ref[bb, h:h + 1, :]
        intra[key] = jnp.where(causal, b_b[key] - b_row + i_row, NEG_INF)
    for key in heads:
        m_inter = b_b[key] + m_prev[key]
        m_t[key] = jnp.maximum(m_inter, jnp.max(intra[key], axis=1, keepdims=True))
        w_inter[key] = jnp.exp(m_inter - m_t[key])
    intra_o = {}
    for key in heads:
        a = jnp.exp(intra[key] - m_t[key]) * scores[key]
        intra_o[key] = _dot(a.astype(BF16), v_aug[key])
    for key in heads:
        bb, h = key
        lo, hi = h * d, (h + 1) * d
        num = w_inter[key] * inter[key][:, :d] + intra_o[key][:, :d]
        den = w_inter[key] * inter[key][:, d:] + intra_o[key][:, d:]
        hh = num / jnp.maximum(jnp.abs(den), jnp.exp(-m_t[key]))
        ms = jnp.dot(hh * hh, mean_mat, precision=HIGHEST, preferred_element_type=F32)
        hh = hh * lax.rsqrt(ms + NORM_EPS) * hn_ref[:, lo:hi]
        o_ref[bb, :, lo:hi] = (hh * _sigmoid(op_ref[bb, :, lo:hi])).astype(o_ref.dtype)
    for key in heads:
        bb, h = key
        b_end = b_b[key][L - 1:L, :]
        g = b_end - b_b[key] + i_b[key]
        m_new = jnp.maximum(b_end + m_prev[key], jnp.max(g, axis=0, keepdims=True))
        decay = jnp.exp(b_end + m_prev[key] - m_new)
        kw = k[key] * jnp.exp(g - m_new)
        cn_ref[bb, h] = (jnp.concatenate([decay, decay], axis=1) * cn_ref[bb, h]
                         + _dot(kw.T.astype(BF16), v_aug[key]))
        m_ref[bb, h:h + 1, :] = m_new


def _mlstm(qk_raw, vc, o_pre, misc, conv_w, b_i, b_f, head_norm):
    B, S, _ = qk_raw.shape
    L = MLSTM_CHUNK
    nb = MLSTM_BATCH
    gb = jnp.zeros((1, LANES), F32).at[0, MISC_I:MISC_I + MLSTM_HEADS].set(b_i)
    gb = gb.at[0, MISC_F:MISC_F + MLSTM_HEADS].set(b_f)
    row = lambda b, c: (b, c, 0)
    const = lambda b, c: (0, 0)
    return pl.pallas_call(
        _mlstm_kernel,
        grid=(B // nb, S // L),
        in_specs=[
            pl.BlockSpec((nb, L, 2 * MIX_WIDTH), row),
            pl.BlockSpec((nb, L, MIX_WIDTH), row),
            pl.BlockSpec((nb, L, MIX_WIDTH), row),
            pl.BlockSpec((nb, L, LANES), row),
            pl.BlockSpec((CONV_WIDTH, 2 * MIX_WIDTH), const),
            pl.BlockSpec((1, LANES), const),
            pl.BlockSpec((1, MIX_WIDTH), const),
        ],
        out_specs=pl.BlockSpec((nb, L, MIX_WIDTH), row),
        out_shape=jax.ShapeDtypeStruct((B, S, MIX_WIDTH), BF16),
        scratch_shapes=[
            pltpu.VMEM((nb, L, 2 * MIX_WIDTH), F32),
            pltpu.VMEM((nb, MLSTM_HEADS, MLSTM_HEAD_DIM, 2 * LANES), F32),
            pltpu.VMEM((nb, 8, LANES), F32),
        ],
        compiler_params=_params(("parallel", "arbitrary")),
        name="mlstm",
    )(qk_raw, vc, o_pre, misc, conv_w, gb, head_norm.reshape(1, MIX_WIDTH))


ZERO_WEIGHT_LOG2 = 160.0
NORM_SLACK = 1.02


def _alibi_first_tile(qn2, kn2, slopes):
    B, H, S = qn2.shape
    nt = S // ATTN_BLOCK
    k_max = jnp.sqrt(jnp.max(kn2[:, :, :H], axis=1))
    q_max = jnp.sqrt(jnp.max(qn2.reshape(B, H, nt, ATTN_BLOCK), axis=3))
    c = MOBA_HEAD_DIM ** -0.5 * LOG2E
    reach = ((ZERO_WEIGHT_LOG2 + 2.0 * NORM_SLACK * c * q_max * k_max[:, :, None])
             / (jnp.asarray(slopes)[None, :, None] * LOG2E))
    tiles = jnp.minimum(jnp.ceil((reach - 1.0) / ATTN_BLOCK), nt)
    tiles = jnp.max(tiles.reshape(B, H // HEADS_PER_STEP, HEADS_PER_STEP, nt), axis=2)
    first = jnp.arange(nt, dtype=F32)[None, None, :] - tiles
    return jnp.maximum(first, 0.0).astype(jnp.int32)


def _even_layer(h, norm_mix, w_in, pool_w, pool_scale, w_out, norm_ffn, wg, wu, wd):
    B, S, D = h.shape
    slopes = (2.0 ** (-8.0 * np.arange(1, MOBA_HEADS + 1) / MOBA_HEADS)).astype(np.float32)
    ka, ub, kmean, kn2, qT, vT = _even_in(h, norm_mix, w_in, slopes)
    nb = S // MOBA_BLOCK
    kmean = kmean.reshape(B, nb, MOBA_HEADS, LANES)[..., :MOBA_HEAD_DIM].transpose(0, 2, 1, 3)
    qaT, qn2 = _moba_gate(kmean, qT)
    aT = _attention(qaT, ka, vT, MOBA_HEAD_DIM, MOBA_HEADS, "moba_attn",
                    first_tile=_alibi_first_tile(qn2, kn2, slopes))
    b_out = _pool(ub, pool_w, pool_scale)
    h = _mix_out(h, aT, b_out, w_out, a_first=True)
    return _ffn(h.reshape(B * S, D), norm_ffn, wg.astype(BF16), wu.astype(BF16), wd.astype(BF16))


def _odd_layer(h, norm_mix, w_in, conv_w, b_i, b_f, head_norm, q_norm, w_uq, kv_norm, w_ukv,
               w_out, norm_ffn, router_w, router_b, wg, wu, wd):
    B, S, D = h.shape
    qk_raw, vc, o_pre, misc, mqT, mk, mvT = _odd_in(h, norm_mix, w_in, q_norm, w_uq, kv_norm, w_ukv)
    c_out = _mlstm(qk_raw, vc, o_pre, misc, conv_w, b_i, b_f, head_norm)
    dT = _attention(mqT, mk, mvT, MLA_V_DIM, MLA_HEADS, "mla_attn")
    h = _mix_out(h, dT, c_out, w_out, a_first=False)
    return _moe(h.reshape(B * S, D), norm_ffn, router_w, router_b,
                wg.astype(BF16), wu.astype(BF16), wd.astype(BF16))


def kernel(x, p, ev_norm_mix, ev_w_in, pool_w, pool_scale, ev_w_out, ev_norm_ffn, ffn_w_gate, ffn_w_up, ffn_w_down, od_norm_mix, od_w_in, conv_w, gate_b_i, gate_b_f, mlstm_norm, mla_q_norm, mla_w_uq, mla_kv_norm, mla_w_ukv, od_w_out, od_norm_ffn, router_w, router_b, moe_w_gate, moe_w_up, moe_w_down, ple_norm, ple_w_gate, ple_w_proj, final_norm):
    B, S, D = x.shape
    depth = p.shape[0]
    assert D == D_MODEL and S % (2 * ROW_TILE) == 0 and B % MLSTM_BATCH == 0
    assert MOBA_TOPK <= S // MOBA_BLOCK <= MOBA_MAX_BLOCKS
    h = x
    for layer in range(depth):
        j = layer // 2
        if layer % 2 == 0:
            h2d = _even_layer(h, ev_norm_mix[j], ev_w_in[j], pool_w[j], pool_scale[j], ev_w_out[j],
                              ev_norm_ffn[j], ffn_w_gate[j], ffn_w_up[j], ffn_w_down[j])
        else:
            h2d = _odd_layer(h, od_norm_mix[j], od_w_in[j], conv_w[j], gate_b_i[j], gate_b_f[j],
                             mlstm_norm[j], mla_q_norm[j], mla_w_uq[j], mla_kv_norm[j], mla_w_ukv[j],
                             od_w_out[j], od_norm_ffn[j], router_w[j], router_b[j],
                             moe_w_gate[j], moe_w_up[j], moe_w_down[j])
        last = layer == depth - 1
        h2d = _ple(h2d, p[layer].reshape(B * S, PLE_DIM), ple_norm[layer], ple_w_gate[layer],
                   ple_w_proj[layer], final_g=final_norm if last else None)
        h = h2d.reshape(B, S, D)
    return h
```

```python
import functools
import math

import numpy as np
import jax
import jax.numpy as jnp
from jax import lax
from jax.experimental import pallas as pl
from jax.experimental.pallas import tpu as pltpu

F32 = jnp.float32
BF16 = jnp.bfloat16
HIGHEST = lax.Precision.HIGHEST

D_MODEL = 1024
PLE_DIM = 256
NORM_EPS = 1e-6
NEG_INF = -1e30

MOBA_HEADS = 8
MOBA_HEAD_DIM = 64
MOBA_BLOCK = 256
MOBA_TOPK = 3
POOL_WINDOWS = (2, 4, 8, 16)
POOL_GROUP_DIM = 128
POOL_HALO = 16
MLSTM_HEADS = 4
MLSTM_HEAD_DIM = 128
MLSTM_CHUNK = 128
MLSTM_BATCH = 2
assert MLSTM_CHUNK == 128
CONV_WIDTH = 4
MLA_HEADS = 4
MLA_Q_RANK = 256
MLA_KV_RANK = 128
MLA_NOPE_DIM = 64
MLA_ROPE_DIM = 32
MLA_V_DIM = 128
ROPE_BASE = 10000.0
FFN_DIM = 2816
N_EXPERTS = 8
EXPERT_DIM = 3584
MIX_WIDTH = 512

ATTN_BLOCK = 512
ROW_TILE = 512
LANES = 128
VMEM_LIMIT = 56 * 1024 * 1024

MISC_ROPE = 0
MISC_I = 32
MISC_F = 36


def _params(sem, vmem=VMEM_LIMIT):
    return pltpu.CompilerParams(dimension_semantics=sem, vmem_limit_bytes=vmem)


def _rms(x, g):
    ms = jnp.mean(x * x, axis=-1, keepdims=True)
    return x * lax.rsqrt(ms + NORM_EPS) * g


def _sigmoid(x):
    return 1.0 / (1.0 + jnp.exp(-x))


def _dot(a, b):
    return jnp.dot(a, b, preferred_element_type=F32)


def _dot_nt(a, b, precision=None):
    return lax.dot_general(a, b, (((1,), (1,)), ((), ())), precision=precision,
                           preferred_element_type=F32)


KAUG_SEL = MOBA_HEAD_DIM
KAUG_POS = KAUG_SEL + 32
MOBA_MAX_BLOCKS = KAUG_POS - KAUG_SEL


def _bf16_terms(x, n):
    out = []
    for _ in range(n):
        bits = np.float32(x).view(np.uint32)
        kept = np.uint32((int(bits) + 0x7FFF + ((int(bits) >> 16) & 1)) & 0xFFFF0000)
        term = float(kept.view(np.float32))
        out.append(term)
        x -= term
    return tuple(out)


LOG2E = math.log2(math.e)
LOG2E_TERMS = _bf16_terms(LOG2E, 3)


def _even_in_kernel(x_ref, g_ref, wn_ref, wqT_ref, wvT_ref, ext_ref,
                    ka_ref, ub_ref, km_ref, kn_ref, qT_ref, vT_ref):
    tm = x_ref.shape[1]
    xn = _rms(x_ref[0], g_ref[...]).astype(BF16)
    n = _dot(xn, wn_ref[...])
    ka = n[:, :MOBA_HEADS * LANES]
    ka_ref[0] = (ka + ext_ref[...].astype(F32)).astype(BF16)
    ub_ref[0] = n[:, MOBA_HEADS * LANES:]
    for j in range(tm // MOBA_BLOCK):
        km_ref[0, j] = jnp.mean(ka[j * MOBA_BLOCK:(j + 1) * MOBA_BLOCK], axis=0, keepdims=True)
    slot = lax.broadcasted_iota(jnp.int32, (MOBA_HEADS * LANES, LANES), 0) // LANES
    head = lax.broadcasted_iota(jnp.int32, (MOBA_HEADS * LANES, LANES), 1)
    kn_ref[0] = _dot((ka * ka).astype(BF16), (slot == head).astype(BF16))
    qT_ref[0] = _dot_nt(wqT_ref[...], xn)
    vT = _dot_nt(wvT_ref[...], xn)
    for j in range(tm // ATTN_BLOCK):
        vT_ref[0, j] = vT[:, j * ATTN_BLOCK:(j + 1) * ATTN_BLOCK].astype(BF16)


def _moba_key_extras(S, slopes):
    pos = np.arange(S)
    blk, off = pos // MOBA_BLOCK, pos % MOBA_BLOCK
    ext = np.zeros((S, MOBA_HEADS, LANES), np.float32)
    ext[pos, :, KAUG_SEL + blk] = 1.0
    for term in range(len(LOG2E_TERMS)):
        ext[:, :, KAUG_POS + 2 * term] = slopes[None, :] * (MOBA_BLOCK * blk)[:, None]
        ext[:, :, KAUG_POS + 2 * term + 1] = slopes[None, :] * off[:, None]
    return jnp.asarray(ext.reshape(S, MOBA_HEADS * LANES), dtype=BF16)


def _even_in(x, g, w_in, slopes):
    B, S, D = x.shape
    tm = ROW_TILE
    nb = S // MOBA_BLOCK
    wq, wk, wv, wu = (w_in[:, i * MIX_WIDTH:(i + 1) * MIX_WIDTH] for i in range(4))
    wk_slots = jnp.concatenate(
        [wk.reshape(D, MOBA_HEADS, MOBA_HEAD_DIM),
         jnp.zeros((D, MOBA_HEADS, LANES - MOBA_HEAD_DIM), F32)], axis=2).reshape(D, MOBA_HEADS * LANES)
    wn = jnp.concatenate([wk_slots, wu], axis=1).astype(BF16)
    wqT = wq.T.astype(BF16)
    wvT = wv.T.astype(BF16)
    const = lambda b, i: (0, 0)
    return pl.pallas_call(
        _even_in_kernel,
        grid=(B, S // tm),
        in_specs=[
            pl.BlockSpec((1, tm, D), lambda b, i: (b, i, 0)),
            pl.BlockSpec((1, D), const),
            pl.BlockSpec((D, MOBA_HEADS * LANES + MIX_WIDTH), const),
            pl.BlockSpec((MIX_WIDTH, D), const),
            pl.BlockSpec((MIX_WIDTH, D), const),
            pl.BlockSpec((tm, MOBA_HEADS * LANES), lambda b, i: (i, 0)),
        ],
        out_specs=[
            pl.BlockSpec((1, tm, MOBA_HEADS * LANES), lambda b, i: (b, i, 0)),
            pl.BlockSpec((1, tm, MIX_WIDTH), lambda b, i: (b, i, 0)),
            pl.BlockSpec((1, tm // MOBA_BLOCK, 1, MOBA_HEADS * LANES), lambda b, i: (b, i, 0, 0)),
            pl.BlockSpec((1, tm, LANES), lambda b, i: (b, i, 0)),
            pl.BlockSpec((1, MIX_WIDTH, tm), lambda b, i: (b, 0, i)),
            pl.BlockSpec((1, tm // ATTN_BLOCK, MIX_WIDTH, ATTN_BLOCK), lambda b, i: (b, i, 0, 0)),
        ],
        out_shape=[
            jax.ShapeDtypeStruct((B, S, MOBA_HEADS * LANES), BF16),
            jax.ShapeDtypeStruct((B, S, MIX_WIDTH), F32),
            jax.ShapeDtypeStruct((B, nb, 1, MOBA_HEADS * LANES), F32),
            jax.ShapeDtypeStruct((B, S, LANES), F32),
            jax.ShapeDtypeStruct((B, MIX_WIDTH, S), F32),
            jax.ShapeDtypeStruct((B, S // ATTN_BLOCK, MIX_WIDTH, ATTN_BLOCK), BF16),
        ],
        compiler_params=_params(("parallel", "parallel")),
        name="even_in",
    )(x, g.reshape(1, D), wn, wqT, wvT, _moba_key_extras(S, slopes))


def _moba_gate_kernel(km_ref, qT_ref, qa_ref, qn_ref):
    i = pl.program_id(1)
    nb = km_ref.shape[2]
    tq = qT_ref.shape[2]
    row = lax.broadcasted_iota(jnp.int32, (nb, tq), 0)
    own = (i * tq + lax.broadcasted_iota(jnp.int32, (nb, tq), 1)) // MOBA_BLOCK
    past = row < own
    tail_row = lax.broadcasted_iota(jnp.int32, (LANES - KAUG_POS, tq), 0)
    tail = jnp.zeros(tail_row.shape, F32)
    for term, value in enumerate(LOG2E_TERMS):
        tail = jnp.where(tail_row // 2 == term, F32(value), tail)
    tail = tail.astype(BF16)
    pad = jnp.zeros((MOBA_MAX_BLOCKS - nb, tq), BF16) if nb < MOBA_MAX_BLOCKS else None
    for h in range(MOBA_HEADS):
        q_h = qT_ref[0, h * MOBA_HEAD_DIM:(h + 1) * MOBA_HEAD_DIM, :]
        qn_ref[0, h:h + 1, :] = jnp.sum(q_h * q_h, axis=0, keepdims=True)
        gate = jnp.dot(km_ref[0, h], q_h, precision=HIGHEST, preferred_element_type=F32)
        gate = jnp.where(past, gate, NEG_INF)
        chosen = jnp.zeros(gate.shape, F32)
        for _ in range(MOBA_TOPK):
            mx = jnp.max(gate, axis=0, keepdims=True)
            first = jnp.min(jnp.where(gate == mx, row, nb), axis=0, keepdims=True)
            pick = row == first
            chosen = jnp.where(pick, 1.0, chosen)
            gate = jnp.where(pick, -jnp.inf, gate)
        keep = jnp.where(past, chosen, (row == own).astype(F32))
        sel = jnp.where(keep > 0.0, 0.0, NEG_INF).astype(BF16)
        base = h * LANES
        qa_ref[0, base:base + KAUG_SEL] = (q_h * (MOBA_HEAD_DIM ** -0.5 * LOG2E)).astype(BF16)
        qa_ref[0, base + KAUG_SEL:base + KAUG_SEL + nb] = sel
        if pad is not None:
            qa_ref[0, base + KAUG_SEL + nb:base + KAUG_POS] = pad
        qa_ref[0, base + KAUG_POS:base + LANES] = tail


def _moba_gate(kmean, qT):
    B, H, nb, dh = kmean.shape
    S = qT.shape[2]
    tq = ATTN_BLOCK
    return pl.pallas_call(
        _moba_gate_kernel,
        grid=(B, S // tq),
        in_specs=[
            pl.BlockSpec((1, H, nb, dh), lambda b, i: (b, 0, 0, 0)),
            pl.BlockSpec((1, H * dh, tq), lambda b, i: (b, 0, i)),
        ],
        out_specs=[pl.BlockSpec((1, H * LANES, tq), lambda b, i: (b, 0, i)),
                   pl.BlockSpec((1, H, tq), lambda b, i: (b, 0, i))],
        out_shape=[jax.ShapeDtypeStruct((B, H * LANES, S), BF16), jax.ShapeDtypeStruct((B, H, S), F32)],
        compiler_params=_params(("parallel", "parallel")),
        name="moba_gate",
    )(kmean, qT)


HEADS_PER_STEP = 4
SUM_ROWS = 16


def _attn_kernel(lo_ref, q_ref, k_ref, v_ref, o_ref, sa_ref, sb_ref, m_ref, acc_ref):
    i = pl.program_id(2)
    lo = lo_ref[(pl.program_id(0) * pl.num_programs(1) + pl.program_id(1)) * pl.num_programs(2) + i]
    n_past = i - lo
    tq = q_ref.shape[2]
    tk = ATTN_BLOCK
    hp = HEADS_PER_STEP
    dv = v_ref.shape[2] // hp
    m_ref[...] = jnp.full(m_ref.shape, NEG_INF, F32)
    acc_ref[...] = jnp.zeros(acc_ref.shape, F32)
    ones_rows = jnp.ones((SUM_ROWS, tk), BF16)

    def scores(kvt, s_ref, diag):
        start = pl.multiple_of(kvt * tk, tk)
        k_tile = k_ref[0, pl.ds(start, tk), :]
        for g in range(hp):
            s = _dot(k_tile[:, g * LANES:(g + 1) * LANES], q_ref[0, g * LANES:(g + 1) * LANES, :])
            if diag:
                key = lax.broadcasted_iota(jnp.int32, (tk, tq), 0)
                qry = lax.broadcasted_iota(jnp.int32, (tk, tq), 1)
                s = jnp.where(key <= qry, s, NEG_INF)
            s_ref[g] = s

    def consume(kvt, s_ref):
        v_tile = v_ref[0, kvt]
        for g in range(hp):
            s = s_ref[g]
            m_run = m_ref[g]
            m_new = jnp.maximum(m_run, jnp.max(s, axis=0, keepdims=True))
            p = jnp.exp2(s - m_new).astype(BF16)
            v_aug = jnp.concatenate([v_tile[g * dv:(g + 1) * dv], ones_rows], axis=0)
            acc_ref[g] = jnp.exp2(m_run - m_new) * acc_ref[g] + _dot(v_aug, p)
            m_ref[g] = m_new

    tile_at = lambda t: jnp.where(t == 0, i, lo + t - 1)
    scores(i, sa_ref, True)

    def pair(p, carry):
        t = 2 * p
        scores(tile_at(t + 1), sb_ref, False)
        consume(tile_at(t), sa_ref)
        scores(tile_at(t + 2), sa_ref, False)
        consume(tile_at(t + 1), sb_ref)
        return carry

    lax.fori_loop(0, n_past // 2, pair, 0)
    last = 2 * (n_past // 2)

    @pl.when(n_past % 2 == 1)
    def _():
        scores(tile_at(last + 1), sb_ref, False)
        consume(tile_at(last), sa_ref)
        consume(tile_at(last + 1), sb_ref)

    @pl.when(n_past % 2 == 0)
    def _():
        consume(tile_at(last), sa_ref)

    for g in range(hp):
        o_ref[0, g * dv:(g + 1) * dv, :] = (acc_ref[g, :dv] / acc_ref[g, dv:dv + 1]).astype(o_ref.dtype)


def _attention(qT, k, vT, dv, heads, name, first_tile=None):
    B, _, S = qT.shape
    tq = ATTN_BLOCK
    nt = S // ATTN_BLOCK
    hp = HEADS_PER_STEP
    if first_tile is None:
        first_tile = jnp.zeros((B, heads // hp, nt), jnp.int32)
    return pl.pallas_call(
        _attn_kernel,
        grid_spec=pltpu.PrefetchScalarGridSpec(
            num_scalar_prefetch=1,
            grid=(B, heads // hp, S // tq),
            in_specs=[
                pl.BlockSpec((1, hp * LANES, tq), lambda b, h, i, lo: (b, h, i)),
                pl.BlockSpec((1, S, hp * LANES), lambda b, h, i, lo: (b, 0, h)),
                pl.BlockSpec((1, nt, hp * dv, ATTN_BLOCK), lambda b, h, i, lo: (b, 0, h, 0)),
            ],
            out_specs=pl.BlockSpec((1, hp * dv, tq), lambda b, h, i, lo: (b, h, i)),
            scratch_shapes=[pltpu.VMEM((hp, ATTN_BLOCK, tq), F32), pltpu.VMEM((hp, ATTN_BLOCK, tq), F32),
                            pltpu.VMEM((hp, 1, tq), F32), pltpu.VMEM((hp, dv + SUM_ROWS, tq), F32)],
        ),
        out_shape=jax.ShapeDtypeStruct((B, heads * dv, S), BF16),
        compiler_params=_params(("parallel", "parallel", "arbitrary")),
        name=name,
    )(first_tile.reshape(-1), qT, k, vT)


def _pool_kernel(x_ref, halo_ref, w_ref, sc_ref, o_ref, xs_ref):
    i = pl.program_id(1)
    tm = x_ref.shape[1]
    x = x_ref[0]
    xs_ref[0:POOL_HALO] = jnp.where(i > 0, halo_ref[0], 0.0)
    xs_ref[POOL_HALO:POOL_HALO + tm] = x
    t = i * tm + lax.broadcasted_iota(jnp.int32, (tm, 1), 0)
    outs = []
    for g, win in enumerate(POOL_WINDOWS):
        lo, hi = g * POOL_GROUP_DIM, (g + 1) * POOL_GROUP_DIM
        xg = x[:, lo:hi]
        acc = xg
        for d in range(1, win):
            acc = acc + xs_ref[POOL_HALO - d:POOL_HALO - d + tm, lo:hi]
        count = jnp.minimum(t + 1, win).astype(F32)
        outs.append(_dot((acc / count - xg).astype(BF16), w_ref[g]))
    o_ref[0] = (jnp.concatenate(outs, axis=1) * sc_ref[...]).astype(o_ref.dtype)


def _pool(ub, pool_w, pool_scale):
    B, S, W = ub.shape
    tm = ROW_TILE
    per = tm // POOL_HALO
    return pl.pallas_call(
        _pool_kernel,
        grid=(B, S // tm),
        in_specs=[
            pl.BlockSpec((1, tm, W), lambda b, i: (b, i, 0)),
            pl.BlockSpec((1, POOL_HALO, W), lambda b, i: (b, jnp.maximum(i * per - 1, 0), 0)),
            pl.BlockSpec(pool_w.shape, lambda b, i: (0, 0, 0)),
            pl.BlockSpec((1, W), lambda b, i: (0, 0)),
        ],
        out_specs=pl.BlockSpec((1, tm, W), lambda b, i: (b, i, 0)),
        out_shape=jax.ShapeDtypeStruct((B, S, W), BF16),
        scratch_shapes=[pltpu.VMEM((POOL_HALO + tm, W), F32)],
        compiler_params=_params(("parallel", "parallel")),
        name="pool",
    )(ub, ub, pool_w.astype(BF16), pool_scale.reshape(1, W))


def _mix_out_kernel(h_ref, aT_ref, b_ref, w_ref, o_ref, *, a_first):
    a = aT_ref[0].astype(F32).T.astype(BF16)
    b = b_ref[0].astype(BF16)
    lo, hi = (a, b) if a_first else (b, a)
    y = _dot(lo, w_ref[:MIX_WIDTH]) + _dot(hi, w_ref[MIX_WIDTH:])
    o_ref[0] = h_ref[0] + y


def _mix_out(h, aT, b, w_out, a_first):
    B, S, D = h.shape
    tm = ROW_TILE
    return pl.pallas_call(
        functools.partial(_mix_out_kernel, a_first=a_first),
        grid=(B, S // tm),
        in_specs=[
            pl.BlockSpec((1, tm, D), lambda b_, i: (b_, i, 0)),
            pl.BlockSpec((1, MIX_WIDTH, tm), lambda b_, i: (b_, 0, i)),
            pl.BlockSpec((1, tm, MIX_WIDTH), lambda b_, i: (b_, i, 0)),
            pl.BlockSpec((2 * MIX_WIDTH, D), lambda b_, i: (0, 0)),
        ],
        out_specs=pl.BlockSpec((1, tm, D), lambda b_, i: (b_, i, 0)),
        out_shape=jax.ShapeDtypeStruct((B, S, D), F32),
        compiler_params=_params(("parallel", "parallel")),
        name="mix_out",
    )(h, aT, b, w_out.astype(BF16))


def _swiglu_step(xn, wg, wu, wd):
    gt = _dot(xn, wg)
    up = _dot(xn, wu)
    return _dot((gt * _sigmoid(gt) * up).astype(BF16), wd)


def _ffn_kernel(h_ref, g_ref, wg_ref, wu_ref, wd_ref, o_ref):
    h = h_ref[...]
    xn = _rms(h, g_ref[...]).astype(BF16)
    o_ref[...] = h + _swiglu_step(xn, wg_ref[...], wu_ref[...], wd_ref[...])


def _ffn(h2d, g, wg, wu, wd, tm=ROW_TILE):
    T, D = h2d.shape
    F = wg.shape[1]
    resident = lambda shape: pl.BlockSpec(shape, lambda i: (0, 0), pipeline_mode=pl.Buffered(1))
    return pl.pallas_call(
        _ffn_kernel,
        grid=(T // tm,),
        in_specs=[
            pl.BlockSpec((tm, D), lambda i: (i, 0)),
            pl.BlockSpec((1, D), lambda i: (0, 0)),
            resident((D, F)),
            resident((D, F)),
            resident((F, D)),
        ],
        out_specs=pl.BlockSpec((tm, D), lambda i: (i, 0)),
        out_shape=jax.ShapeDtypeStruct((T, D), F32),
        compiler_params=_params(("parallel",)),
        name="ffn",
    )(h2d, g.reshape(1, D), wg, wu, wd)


ROUTE_E0, ROUTE_E1, ROUTE_W0, ROUTE_W1, ROUTE_R0, ROUTE_R1 = range(6)
MOE_TILE = 512
MOE_TF = EXPERT_DIM // 2


def _lane_pick(tile, lane, idx):
    return jnp.sum(jnp.where(lane == idx, tile, 0.0), axis=1, keepdims=True)


ROUTER_ROWS = 16


def _router_kernel(h_ref, g_ref, wT_ref, b_ref, route_ref, cnt_ref):
    tm = h_ref.shape[0]

    @pl.when(pl.program_id(0) == 0)
    def _():
        cnt_ref[...] = jnp.zeros_like(cnt_ref)

    xn = _rms(h_ref[...], g_ref[...])
    logits = _dot_nt(wT_ref[...], xn, precision=HIGHEST) + b_ref[...]
    row = lax.broadcasted_iota(jnp.int32, logits.shape, 0)
    logits = jnp.where(row < N_EXPERTS, logits, -jnp.inf)
    v0 = jnp.max(logits, axis=0, keepdims=True)
    i0 = jnp.min(jnp.where(logits == v0, row, ROUTER_ROWS), axis=0, keepdims=True)
    rest = jnp.where(row == i0, -jnp.inf, logits)
    v1 = jnp.max(rest, axis=0, keepdims=True)
    i1 = jnp.min(jnp.where(rest == v1, row, ROUTER_ROWS), axis=0, keepdims=True)
    e1 = jnp.exp(v1 - v0)
    w0 = 1.0 / (1.0 + e1)
    sel = (row == i0).astype(F32) + (row == i1).astype(F32)
    earlier = (lax.broadcasted_iota(jnp.int32, (tm, tm), 0)
               < lax.broadcasted_iota(jnp.int32, (tm, tm), 1))
    counts = cnt_ref[:, 0:1]
    rank = _dot(sel.astype(BF16), earlier.astype(BF16)) + counts
    cnt_ref[...] = jnp.broadcast_to(counts + jnp.sum(sel, axis=1, keepdims=True), cnt_ref.shape)
    pick = lambda idx: jnp.sum(jnp.where(row == idx, rank, 0.0), axis=0, keepdims=True)
    rows = [None] * 6
    rows[ROUTE_E0], rows[ROUTE_E1] = i0.astype(F32), i1.astype(F32)
    rows[ROUTE_W0], rows[ROUTE_W1] = w0, e1 * w0
    rows[ROUTE_R0], rows[ROUTE_R1] = pick(i0), pick(i1)
    routeT = jnp.concatenate(rows + [jnp.zeros((LANES - len(rows), tm), F32)], axis=0)
    route_ref[...] = routeT.T


def _router(h2d, g, router_w, router_b):
    T, D = h2d.shape
    tm = ROW_TILE
    wT = jnp.zeros((ROUTER_ROWS, D), F32).at[:N_EXPERTS].set(router_w.T)
    b = jnp.zeros((ROUTER_ROWS, 1), F32).at[:N_EXPERTS, 0].set(router_b)
    return pl.pallas_call(
        _router_kernel,
        grid=(T // tm,),
        in_specs=[
            pl.BlockSpec((tm, D), lambda i: (i, 0)),
            pl.BlockSpec((1, D), lambda i: (0, 0)),
            pl.BlockSpec((ROUTER_ROWS, D), lambda i: (0, 0)),
            pl.BlockSpec((ROUTER_ROWS, 1), lambda i: (0, 0)),
        ],
        out_specs=[pl.BlockSpec((tm, LANES), lambda i: (i, 0)),
                   pl.BlockSpec((ROUTER_ROWS, LANES), lambda i: (0, 0))],
        out_shape=[jax.ShapeDtypeStruct((T, LANES), F32), jax.ShapeDtypeStruct((ROUTER_ROWS, LANES), F32)],
        compiler_params=_params(("arbitrary",)),
        name="router",
    )(h2d, g.reshape(1, D), wT, b)


SUBLANES = 8
ROW_LANES = D_MODEL // SUBLANES
assert ROW_LANES == LANES


def _row_tile(ref, r):
    return ref.at[pl.ds(pl.multiple_of(r * SUBLANES, SUBLANES), SUBLANES), :]


def _to_row_tiles(ref, x):
    tm = x.shape[0]
    for s in range(SUBLANES):
        ref[pl.ds(s, tm, stride=SUBLANES), :] = x[:, s * LANES:(s + 1) * LANES]


def _from_row_tiles(ref):
    tm = ref.shape[0] // SUBLANES
    return jnp.concatenate([ref[pl.ds(s, tm, stride=SUBLANES), :] for s in range(SUBLANES)], axis=1)


def _row_copies(pos_ref, base, r, src_of, dst_of, sem):
    return [pltpu.make_async_copy(src_of(k, pos_ref[base + 2 * r + k]),
                                  dst_of(k, pos_ref[base + 2 * r + k]), sem) for k in range(2)]


def _all_rows(tm, make):
    def issue(r, c):
        for k, cp in enumerate(make(r)):
            cp.start(priority=k)
        return c

    def drain(r, c):
        for cp in make(r):
            cp.wait()
        return c

    lax.fori_loop(0, tm, issue, 0, unroll=8)
    lax.fori_loop(0, tm, drain, 0, unroll=8)


def _dispatch_kernel(pos_ref, h_ref, g_ref, init_ref, xs_ref, xn_ref, sem):
    del init_ref
    tm = h_ref.shape[0]
    _to_row_tiles(xn_ref, _rms(h_ref[...], g_ref[...]))
    base = pl.program_id(0) * (2 * tm)
    _all_rows(tm, lambda r: _row_copies(
        pos_ref, base, r, lambda k, p: _row_tile(xn_ref, r), lambda k, p: _row_tile(xs_ref, p), sem))


def _dispatch(pos, h2d, g, n_rows):
    T, D = h2d.shape
    tm = ROW_TILE
    return pl.pallas_call(
        _dispatch_kernel,
        grid_spec=pltpu.PrefetchScalarGridSpec(
            num_scalar_prefetch=1,
            grid=(T // tm,),
            in_specs=[pl.BlockSpec((tm, D), lambda i, pos: (i, 0)),
                      pl.BlockSpec((1, D), lambda i, pos: (0, 0)),
                      pl.BlockSpec(memory_space=pl.ANY)],
            out_specs=pl.BlockSpec(memory_space=pl.ANY),
            scratch_shapes=[pltpu.VMEM((tm * SUBLANES, ROW_LANES), F32), pltpu.SemaphoreType.DMA(())],
        ),
        out_shape=jax.ShapeDtypeStruct((n_rows * SUBLANES, ROW_LANES), F32),
        input_output_aliases={3: 0},
        compiler_params=_params(("arbitrary",)),
        name="moe_dispatch",
    )(pos, h2d, g.reshape(1, D), jnp.zeros((n_rows * SUBLANES, ROW_LANES), F32))


def _moe_ffn_kernel(te_ref, nv_ref, x_ref, wg_ref, wu_ref, wd_ref, o_ref, xb_ref, acc_ref):
    j = pl.program_id(0)
    f = pl.program_id(1)
    valid = j < nv_ref[0]

    @pl.when(f == 0)
    def _():
        xb_ref[...] = _from_row_tiles(x_ref).astype(BF16)
        acc_ref[...] = jnp.zeros_like(acc_ref)

    @pl.when(valid)
    def _():
        acc_ref[...] += _swiglu_step(xb_ref[...], wg_ref[0], wu_ref[0], wd_ref[0])

    @pl.when(f == pl.num_programs(1) - 1)
    def _():
        _to_row_tiles(o_ref, acc_ref[...])


def _moe_ffn(tile_expert, n_valid, xs, wg, wu, wd):
    N, D = xs.shape[0] // SUBLANES, D_MODEL
    tm, tf = MOE_TILE, MOE_TF
    F = wg.shape[2]
    return pl.pallas_call(
        _moe_ffn_kernel,
        grid_spec=pltpu.PrefetchScalarGridSpec(
            num_scalar_prefetch=2,
            grid=(N // tm, F // tf),
            in_specs=[pl.BlockSpec((tm * SUBLANES, ROW_LANES), lambda j, f, te, nv: (j, 0)),
                      pl.BlockSpec((1, D, tf), lambda j, f, te, nv: (te[j], 0, f)),
                      pl.BlockSpec((1, D, tf), lambda j, f, te, nv: (te[j], 0, f)),
                      pl.BlockSpec((1, tf, D), lambda j, f, te, nv: (te[j], f, 0))],
            out_specs=pl.BlockSpec((tm * SUBLANES, ROW_LANES), lambda j, f, te, nv: (j, 0)),
            scratch_shapes=[pltpu.VMEM((tm, D), BF16), pltpu.VMEM((tm, D), F32)],
        ),
        out_shape=jax.ShapeDtypeStruct(xs.shape, F32),
        compiler_params=_params(("arbitrary", "arbitrary")),
        name="moe_ffn",
    )(tile_expert, n_valid, xs, wg, wu, wd)


def _combine_kernel(pos_ref, h_ref, route_ref, ys_ref, o_ref, y_ref, sem):
    tm = h_ref.shape[0]
    base = pl.program_id(0) * (2 * tm)
    _all_rows(tm, lambda r: _row_copies(
        pos_ref, base, r, lambda k, p: _row_tile(ys_ref, p), lambda k, p: _row_tile(y_ref.at[k], r), sem))
    route = route_ref[...]
    lane = lax.broadcasted_iota(jnp.int32, route.shape, 1)
    o_ref[...] = (h_ref[...] + _lane_pick(route, lane, ROUTE_W0) * _from_row_tiles(y_ref.at[0])
                  + _lane_pick(route, lane, ROUTE_W1) * _from_row_tiles(y_ref.at[1]))


def _combine(pos, h2d, route, ys):
    T, D = h2d.shape
    tm = ROW_TILE
    return pl.pallas_call(
        _combine_kernel,
        grid_spec=pltpu.PrefetchScalarGridSpec(
            num_scalar_prefetch=1,
            grid=(T // tm,),
            in_specs=[pl.BlockSpec((tm, D), lambda i, pos: (i, 0)),
                      pl.BlockSpec((tm, LANES), lambda i, pos: (i, 0)),
                      pl.BlockSpec(memory_space=pl.ANY)],
            out_specs=pl.BlockSpec((tm, D), lambda i, pos: (i, 0)),
            scratch_shapes=[pltpu.VMEM((2, tm * SUBLANES, ROW_LANES), F32), pltpu.SemaphoreType.DMA(())],
        ),
        out_shape=jax.ShapeDtypeStruct((T, D), F32),
        compiler_params=_params(("arbitrary",)),
        name="moe_combine",
    )(pos, h2d, route, ys)


def _moe(h2d, g, router_w, router_b, wg, wu, wd):
    T, D = h2d.shape
    tm = MOE_TILE
    route, counts = _router(h2d, g, router_w, router_b)
    cnt = counts[:N_EXPERTS, 0].astype(jnp.int32)
    padded = (cnt + tm - 1) // tm * tm
    ends = jnp.cumsum(padded)
    start = ends - padded
    e01 = route[:, ROUTE_E0:ROUTE_E1 + 1].astype(jnp.int32)
    r01 = route[:, ROUTE_R0:ROUTE_R1 + 1].astype(jnp.int32)
    pos = (start[e01] + r01).reshape(2 * T)
    n_rows = 2 * T + N_EXPERTS * tm
    tile_row = jnp.arange(n_rows // tm, dtype=jnp.int32) * tm
    tile_expert = jnp.minimum(jnp.sum(tile_row[:, None] >= ends[None, :], axis=1), N_EXPERTS - 1).astype(jnp.int32)
    n_valid = (ends[-1:] // tm).astype(jnp.int32)
    xs = _dispatch(pos, h2d, g, n_rows)
    ys = _moe_ffn(tile_expert, n_valid, xs, wg, wu, wd)
    return _combine(pos, h2d, route, ys)


def _ple_kernel(*refs, final):
    if final:
        h_ref, p_ref, g_ref, wg_ref, wp_ref, fg_ref, o_ref = refs
    else:
        h_ref, p_ref, g_ref, wg_ref, wp_ref, o_ref = refs
    h = h_ref[...]
    gate = _sigmoid(_dot(_rms(h, g_ref[...]).astype(BF16), wg_ref[...]))
    out = h + gate * _dot(p_ref[...].astype(BF16), wp_ref[...])
    if final:
        out = _rms(out, fg_ref[...])
    o_ref[...] = out


def _ple(h2d, p2d, g, w_gate, w_proj, final_g=None):
    T, D = h2d.shape
    tm = ROW_TILE
    final = final_g is not None
    in_specs = [
        pl.BlockSpec((tm, D), lambda i: (i, 0)),
        pl.BlockSpec((tm, PLE_DIM), lambda i: (i, 0)),
        pl.BlockSpec((1, D), lambda i: (0, 0)),
        pl.BlockSpec((D, D), lambda i: (0, 0)),
        pl.BlockSpec((PLE_DIM, D), lambda i: (0, 0)),
    ]
    args = [h2d, p2d, g.reshape(1, D), w_gate.astype(BF16), w_proj.astype(BF16)]
    if final:
        in_specs.append(pl.BlockSpec((1, D), lambda i: (0, 0)))
        args.append(final_g.reshape(1, D))
    return pl.pallas_call(
        functools.partial(_ple_kernel, final=final),
        grid=(T // tm,),
        in_specs=in_specs,
        out_specs=pl.BlockSpec((tm, D), lambda i: (i, 0)),
        out_shape=jax.ShapeDtypeStruct((T, D), F32),
        compiler_params=_params(("parallel",)),
        name="ple_final" if final else "ple",
    )(*args)


ODD_MAIN = 4 * MIX_WIDTH
ODD_COLS = ODD_MAIN + MLA_Q_RANK + MLA_KV_RANK + 2 * LANES
MLA_QK_SCALE = (MLA_NOPE_DIM + MLA_ROPE_DIM) ** -0.5 * LOG2E
ROPE_HALF = MLA_ROPE_DIM // 2


def _odd_in_kernel(x_ref, g_ref, wn_ref, qn_ref, wuqT_ref, kvn_ref, wk2_ref, wvT_ref,
                   cosT_ref, sinT_ref, cc_ref, ss_ref,
                   qk_ref, vc_ref, op_ref, misc_ref, mq_ref, mk_ref, mv_ref):
    tm = x_ref.shape[1]
    xn = _rms(x_ref[0], g_ref[...]).astype(BF16)
    u = _dot(xn, wn_ref[...])
    qk_ref[0] = u[:, :2 * MIX_WIDTH]
    vc_ref[0] = u[:, 2 * MIX_WIDTH:3 * MIX_WIDTH].astype(BF16)
    op_ref[0] = u[:, 3 * MIX_WIDTH:ODD_MAIN]
    c0 = ODD_MAIN
    c_q = u[:, c0:c0 + MLA_Q_RANK]
    c0 += MLA_Q_RANK
    c_kv = u[:, c0:c0 + MLA_KV_RANK]
    c0 += MLA_KV_RANK
    misc = u[:, c0:c0 + LANES]
    misc_sw = u[:, c0 + LANES:c0 + 2 * LANES]
    misc_ref[0] = misc
    cqn = _rms(c_q, qn_ref[...]).astype(BF16)
    qT = _dot_nt(wuqT_ref[...], cqn)
    cosT = cosT_ref[...]
    sinT = sinT_ref[...]
    for h in range(MLA_HEADS):
        r = h * LANES
        mq_ref[0, r:r + MLA_NOPE_DIM] = (qT[r:r + MLA_NOPE_DIM] * MLA_QK_SCALE).astype(BF16)
        x1 = qT[r + MLA_NOPE_DIM:r + MLA_NOPE_DIM + ROPE_HALF]
        x2 = qT[r + MLA_NOPE_DIM + ROPE_HALF:r + MLA_NOPE_DIM + MLA_ROPE_DIM]
        mq_ref[0, r + MLA_NOPE_DIM:r + MLA_NOPE_DIM + ROPE_HALF] = (
            (x1 * cosT - x2 * sinT) * MLA_QK_SCALE).astype(BF16)
        mq_ref[0, r + MLA_NOPE_DIM + ROPE_HALF:r + MLA_NOPE_DIM + MLA_ROPE_DIM] = (
            (x1 * sinT + x2 * cosT) * MLA_QK_SCALE).astype(BF16)
        mq_ref[0, r + MLA_NOPE_DIM + MLA_ROPE_DIM:r + LANES] = jnp.zeros(
            (LANES - MLA_NOPE_DIM - MLA_ROPE_DIM, tm), BF16)
    ckvn = _rms(c_kv, kvn_ref[...]).astype(BF16)
    k_rot = (misc * cc_ref[...] + misc_sw * ss_ref[...]).astype(BF16)
    mk_ref[0] = _dot(jnp.concatenate([ckvn, k_rot], axis=1), wk2_ref[...]).astype(BF16)
    vT = _dot_nt(wvT_ref[...], ckvn)
    for j in range(tm // ATTN_BLOCK):
        mv_ref[0, j] = vT[:, j * ATTN_BLOCK:(j + 1) * ATTN_BLOCK].astype(BF16)


def _rope_tables(S):
    inv_freq = ROPE_BASE ** (-jnp.arange(ROPE_HALF, dtype=F32) / ROPE_HALF)
    ang = jnp.arange(S, dtype=F32)[:, None] * inv_freq[None, :]
    cos, sin = jnp.cos(ang), jnp.sin(ang)
    pad = jnp.zeros((S, LANES - MLA_ROPE_DIM), F32)
    cc = jnp.concatenate([cos, cos, pad], axis=1)
    ss = jnp.concatenate([-sin, sin, pad], axis=1)
    return cos.T, sin.T, cc, ss


def _odd_in(x, g, w_in, q_norm, w_uq, kv_norm, w_ukv):
    B, S, D = x.shape
    tm = ROW_TILE
    cuts = np.cumsum([MIX_WIDTH] * 4 + [MLSTM_HEADS, MLSTM_HEADS, MLA_Q_RANK, MLA_KV_RANK]).tolist()
    w_main = w_in[:, :cuts[3]]
    w_i = w_in[:, cuts[3]:cuts[4]]
    w_f = w_in[:, cuts[4]:cuts[5]]
    w_cq = w_in[:, cuts[5]:cuts[6]]
    w_ckv = w_in[:, cuts[6]:cuts[7]]
    w_kr = w_in[:, cuts[7]:]
    w_kr_sw = jnp.concatenate([w_kr[:, ROPE_HALF:], w_kr[:, :ROPE_HALF]], axis=1)
    zpad = lambda n: jnp.zeros((D, n), F32)
    w_misc = jnp.concatenate([w_kr, w_i, w_f, zpad(LANES - MLA_ROPE_DIM - 2 * MLSTM_HEADS)], axis=1)
    w_misc_sw = jnp.concatenate([w_kr_sw, zpad(LANES - MLA_ROPE_DIM)], axis=1)
    wn = jnp.concatenate([w_main, w_cq, w_ckv, w_misc, w_misc_sw], axis=1).astype(BF16)
    qd = MLA_NOPE_DIM + MLA_ROPE_DIM
    w_uq_h = w_uq.reshape(MLA_Q_RANK, MLA_HEADS, qd)
    w_uq_h = jnp.concatenate([w_uq_h, jnp.zeros((MLA_Q_RANK, MLA_HEADS, LANES - qd), F32)], axis=2)
    wuqT = w_uq_h.reshape(MLA_Q_RANK, MLA_HEADS * LANES).T.astype(BF16)
    w_ukv_h = w_ukv.reshape(MLA_KV_RANK, MLA_HEADS, MLA_NOPE_DIM + MLA_V_DIM)
    w_k = jnp.concatenate([w_ukv_h[:, :, :MLA_NOPE_DIM],
                           jnp.zeros((MLA_KV_RANK, MLA_HEADS, LANES - MLA_NOPE_DIM), F32)], axis=2)
    place = jnp.zeros((LANES, MLA_HEADS, LANES), F32)
    eye = jnp.eye(MLA_ROPE_DIM, dtype=F32)
    place = place.at[:MLA_ROPE_DIM, :, MLA_NOPE_DIM:MLA_NOPE_DIM + MLA_ROPE_DIM].set(
        jnp.broadcast_to(eye[:, None, :], (MLA_ROPE_DIM, MLA_HEADS, MLA_ROPE_DIM)))
    wk2 = jnp.concatenate([w_k, place], axis=0).reshape(MLA_KV_RANK + LANES, MLA_HEADS * LANES).astype(BF16)
    wvT = w_ukv_h[:, :, MLA_NOPE_DIM:].reshape(MLA_KV_RANK, MLA_HEADS * MLA_V_DIM).T.astype(BF16)
    cosT, sinT, cc, ss = _rope_tables(S)
    row = lambda b, i: (b, i, 0)
    const = lambda b, i: (0, 0)
    nb = S // ATTN_BLOCK
    return pl.pallas_call(
        _odd_in_kernel,
        grid=(B, S // tm),
        in_specs=[
            pl.BlockSpec((1, tm, D), row),
            pl.BlockSpec((1, D), const),
            pl.BlockSpec((D, ODD_COLS), const),
            pl.BlockSpec((1, MLA_Q_RANK), const),
            pl.BlockSpec((MLA_HEADS * LANES, MLA_Q_RANK), const),
            pl.BlockSpec((1, MLA_KV_RANK), const),
            pl.BlockSpec((MLA_KV_RANK + LANES, MLA_HEADS * LANES), const),
            pl.BlockSpec((MLA_HEADS * MLA_V_DIM, MLA_KV_RANK), const),
            pl.BlockSpec((ROPE_HALF, tm), lambda b, i: (0, i)),
            pl.BlockSpec((ROPE_HALF, tm), lambda b, i: (0, i)),
            pl.BlockSpec((tm, LANES), lambda b, i: (i, 0)),
            pl.BlockSpec((tm, LANES), lambda b, i: (i, 0)),
        ],
        out_specs=[
            pl.BlockSpec((1, tm, 2 * MIX_WIDTH), row),
            pl.BlockSpec((1, tm, MIX_WIDTH), row),
            pl.BlockSpec((1, tm, MIX_WIDTH), row),
            pl.BlockSpec((1, tm, LANES), row),
            pl.BlockSpec((1, MLA_HEADS * LANES, tm), lambda b, i: (b, 0, i)),
            pl.BlockSpec((1, tm, MLA_HEADS * LANES), row),
            pl.BlockSpec((1, tm // ATTN_BLOCK, MLA_HEADS * MLA_V_DIM, ATTN_BLOCK), lambda b, i: (b, i, 0, 0)),
        ],
        out_shape=[
            jax.ShapeDtypeStruct((B, S, 2 * MIX_WIDTH), F32),
            jax.ShapeDtypeStruct((B, S, MIX_WIDTH), BF16),
            jax.ShapeDtypeStruct((B, S, MIX_WIDTH), F32),
            jax.ShapeDtypeStruct((B, S, LANES), F32),
            jax.ShapeDtypeStruct((B, MLA_HEADS * LANES, S), BF16),
            jax.ShapeDtypeStruct((B, S, MLA_HEADS * LANES), BF16),
            jax.ShapeDtypeStruct((B, nb, MLA_HEADS * MLA_V_DIM, ATTN_BLOCK), BF16),
        ],
        compiler_params=_params(("parallel", "parallel")),
        name="odd_in",
    )(x, g.reshape(1, D), wn, q_norm.reshape(1, -1), wuqT, kv_norm.reshape(1, -1), wk2, wvT,
      cosT, sinT, cc, ss)


def _log_sigmoid(x):
    return jnp.minimum(x, 0.0) - jnp.log(1.0 + jnp.exp(-jnp.abs(x)))


def _mlstm_kernel(qk_ref, v_ref, op_ref, misc_ref, cw_ref, gb_ref, hn_ref, o_ref,
                  prev_ref, cn_ref, m_ref):
    c = pl.program_id(1)
    L = qk_ref.shape[1]
    row = lax.broadcasted_iota(jnp.int32, (L, 1), 0)
    lane = lax.broadcasted_iota(jnp.int32, (L, LANES), 1)
    is_f = (lane >= MISC_F) & (lane < MISC_F + MLSTM_HEADS)
    causal = lax.broadcasted_iota(jnp.int32, (L, L), 1) <= lax.broadcasted_iota(jnp.int32, (L, L), 0)

    @pl.when(c == 0)
    def _():
        prev_ref[...] = jnp.zeros_like(prev_ref)
        cn_ref[...] = jnp.zeros_like(cn_ref)
        m_ref[...] = jnp.zeros_like(m_ref)

    seqs = range(qk_ref.shape[0])
    heads = [(bb, h) for bb in seqs for h in range(MLSTM_HEADS)]
    d = MLSTM_HEAD_DIM

    qk = []
    for bb in seqs:
        x = qk_ref[bb]
        prev = prev_ref[bb]
        conv = x * cw_ref[CONV_WIDTH - 1:CONV_WIDTH, :]
        for j in range(1, CONV_WIDTH):
            shifted = pltpu.roll(jnp.where(row >= L - j, prev, x), j, axis=0)
            conv = conv + shifted * cw_ref[CONV_WIDTH - 1 - j:CONV_WIDTH - j, :]
        prev_ref[bb] = x
        qk.append(conv * _sigmoid(conv))

    sel_r = lax.broadcasted_iota(jnp.int32, (LANES, 2 * MIX_WIDTH), 0)
    sel_c = lax.broadcasted_iota(jnp.int32, (LANES, 2 * MIX_WIDTH), 1)
    spread = (sel_r == MISC_I + sel_c // LANES).astype(F32)
    pick = (lax.broadcasted_iota(jnp.int32, (8, LANES), 1)
            == MISC_I + lax.broadcasted_iota(jnp.int32, (8, LANES), 0)).astype(F32)
    mean_mat = jnp.full((d, d), 1.0 / d, F32)
    cols, rows = [], []
    for bb in seqs:
        gates = misc_ref[bb] + gb_ref[...]
        z = jnp.where(is_f, _log_sigmoid(gates), gates)
        cum = jnp.dot(causal.astype(F32), z, precision=HIGHEST, preferred_element_type=F32)
        z = jnp.where(is_f, cum, z)
        cols.append(jnp.dot(z, spread, precision=HIGHEST, preferred_element_type=F32))
        rows.append(_dot_nt(pick, z, precision=HIGHEST))

    q, k, v_aug, i_b, b_b, m_prev, m_t, w_inter, scores, inter = ({} for _ in range(10))
    ones_blk = jnp.ones((L, LANES), BF16)
    for key in heads:
        bb, h = key
        lo, hi = h * d, (h + 1) * d
        q[key] = qk[bb][:, lo:hi].astype(BF16)
        k[key] = qk[bb][:, MIX_WIDTH + lo:MIX_WIDTH + hi] * (d ** -0.5)
        v_aug[key] = jnp.concatenate([v_ref[bb, :, lo:hi], ones_blk], axis=1)
        scores[key] = _dot_nt(q[key], k[key].astype(BF16))
        inter[key] = _dot(q[key], cn_ref[bb, h].astype(BF16))
    intra = {}
    for key in heads:
        bb, h = key
        i_b[key] = cols[bb][:, h * LANES:(h + 1) * LANES]
        b_b[key] = cols[bb][:, (MLSTM_HEADS + h) * LANES:(MLSTM_HEADS + h + 1) * LANES]
        i_row = rows[bb][h:h + 1, :]
        b_row = rows[bb][MLSTM_HEADS + h:MLSTM_HEADS + h + 1, :]
        m_prev[key] = m_ref[bb, h:h + 1, :]
        intra[key] = jnp.where(causal, b_b[key] - b_row + i_row, NEG_INF)
    for key in heads:
        m_inter = b_b[key] + m_prev[key]
        m_t[key] = jnp.maximum(m_inter, jnp.max(intra[key], axis=1, keepdims=True))
        w_inter[key] = jnp.exp(m_inter - m_t[key])
    intra_o = {}
    for key in heads:
        a = jnp.exp(intra[key] - m_t[key]) * scores[key]
        intra_o[key] = _dot(a.astype(BF16), v_aug[key])
    for key in heads:
        bb, h = key
        lo, hi = h * d, (h + 1) * d
        num = w_inter[key] * inter[key][:, :d] + intra_o[key][:, :d]
        den = w_inter[key] * inter[key][:, d:] + intra_o[key][:, d:]
        hh = num / jnp.maximum(jnp.abs(den), jnp.exp(-m_t[key]))
        ms = jnp.dot(hh * hh, mean_mat, precision=HIGHEST, preferred_element_type=F32)
        hh = hh * lax.rsqrt(ms + NORM_EPS) * hn_ref[:, lo:hi]
        o_ref[bb, :, lo:hi] = (hh * _sigmoid(op_ref[bb, :, lo:hi])).astype(o_ref.dtype)
    for key in heads:
        bb, h = key
        b_end = b_b[key][L - 1:L, :]
        g = b_end - b_b[key] + i_b[key]
        m_new = jnp.maximum(b_end + m_prev[key], jnp.max(g, axis=0, keepdims=True))
        decay = jnp.exp(b_end + m_prev[key] - m_new)
        kw = k[key] * jnp.exp(g - m_new)
        cn_ref[bb, h] = (jnp.concatenate([decay, decay], axis=1) * cn_ref[bb, h]
                         + _dot(kw.T.astype(BF16), v_aug[key]))
        m_ref[bb, h:h + 1, :] = m_new


def _mlstm(qk_raw, vc, o_pre, misc, conv_w, b_i, b_f, head_norm):
    B, S, _ = qk_raw.shape
    L = MLSTM_CHUNK
    nb = MLSTM_BATCH
    gb = jnp.zeros((1, LANES), F32).at[0, MISC_I:MISC_I + MLSTM_HEADS].set(b_i)
    gb = gb.at[0, MISC_F:MISC_F + MLSTM_HEADS].set(b_f)
    row = lambda b, c: (b, c, 0)
    const = lambda b, c: (0, 0)
    return pl.pallas_call(
        _mlstm_kernel,
        grid=(B // nb, S // L),
        in_specs=[
            pl.BlockSpec((nb, L, 2 * MIX_WIDTH), row),
            pl.BlockSpec((nb, L, MIX_WIDTH), row),
            pl.BlockSpec((nb, L, MIX_WIDTH), row),
            pl.BlockSpec((nb, L, LANES), row),
            pl.BlockSpec((CONV_WIDTH, 2 * MIX_WIDTH), const),
            pl.BlockSpec((1, LANES), const),
            pl.BlockSpec((1, MIX_WIDTH), const),
        ],
        out_specs=pl.BlockSpec((nb, L, MIX_WIDTH), row),
        out_shape=jax.ShapeDtypeStruct((B, S, MIX_WIDTH), BF16),
        scratch_shapes=[
            pltpu.VMEM((nb, L, 2 * MIX_WIDTH), F32),
            pltpu.VMEM((nb, MLSTM_HEADS, MLSTM_HEAD_DIM, 2 * LANES), F32),
            pltpu.VMEM((nb, 8, LANES), F32),
        ],
        compiler_params=_params(("parallel", "arbitrary")),
        name="mlstm",
    )(qk_raw, vc, o_pre, misc, conv_w, gb, head_norm.reshape(1, MIX_WIDTH))


ZERO_WEIGHT_LOG2 = 160.0
NORM_SLACK = 1.02


def _alibi_first_tile(qn2, kn2, slopes):
    B, H, S = qn2.shape
    nt = S // ATTN_BLOCK
    k_max = jnp.sqrt(jnp.max(kn2[:, :, :H], axis=1))
    q_max = jnp.sqrt(jnp.max(qn2.reshape(B, H, nt, ATTN_BLOCK), axis=3))
    c = MOBA_HEAD_DIM ** -0.5 * LOG2E
    reach = ((ZERO_WEIGHT_LOG2 + 2.0 * NORM_SLACK * c * q_max * k_max[:, :, None])
             / (jnp.asarray(slopes)[None, :, None] * LOG2E))
    tiles = jnp.minimum(jnp.ceil((reach - 1.0) / ATTN_BLOCK), nt)
    tiles = jnp.max(tiles.reshape(B, H // HEADS_PER_STEP, HEADS_PER_STEP, nt), axis=2)
    first = jnp.arange(nt, dtype=F32)[None, None, :] - tiles
    return jnp.maximum(first, 0.0).astype(jnp.int32)


def _even_layer(h, norm_mix, w_in, pool_w, pool_scale, w_out, norm_ffn, wg, wu, wd):
    B, S, D = h.shape
    slopes = (2.0 ** (-8.0 * np.arange(1, MOBA_HEADS + 1) / MOBA_HEADS)).astype(np.float32)
    ka, ub, kmean, kn2, qT, vT = _even_in(h, norm_mix, w_in, slopes)
    nb = S // MOBA_BLOCK
    kmean = kmean.reshape(B, nb, MOBA_HEADS, LANES)[..., :MOBA_HEAD_DIM].transpose(0, 2, 1, 3)
    qaT, qn2 = _moba_gate(kmean, qT)
    aT = _attention(qaT, ka, vT, MOBA_HEAD_DIM, MOBA_HEADS, "moba_attn",
                    first_tile=_alibi_first_tile(qn2, kn2, slopes))
    b_out = _pool(ub, pool_w, pool_scale)
    h = _mix_out(h, aT, b_out, w_out, a_first=True)
    return _ffn(h.reshape(B * S, D), norm_ffn, wg.astype(BF16), wu.astype(BF16), wd.astype(BF16))


def _odd_layer(h, norm_mix, w_in, conv_w, b_i, b_f, head_norm, q_norm, w_uq, kv_norm, w_ukv,
               w_out, norm_ffn, router_w, router_b, wg, wu, wd):
    B, S, D = h.shape
    qk_raw, vc, o_pre, misc, mqT, mk, mvT = _odd_in(h, norm_mix, w_in, q_norm, w_uq, kv_norm, w_ukv)
    c_out = _mlstm(qk_raw, vc, o_pre, misc, conv_w, b_i, b_f, head_norm)
    dT = _attention(mqT, mk, mvT, MLA_V_DIM, MLA_HEADS, "mla_attn")
    h = _mix_out(h, dT, c_out, w_out, a_first=False)
    return _moe(h.reshape(B * S, D), norm_ffn, router_w, router_b,
                wg.astype(BF16), wu.astype(BF16), wd.astype(BF16))


def kernel(x, p, ev_norm_mix, ev_w_in, pool_w, pool_scale, ev_w_out, ev_norm_ffn, ffn_w_gate, ffn_w_up, ffn_w_down, od_norm_mix, od_w_in, conv_w, gate_b_i, gate_b_f, mlstm_norm, mla_q_norm, mla_w_uq, mla_kv_norm, mla_w_ukv, od_w_out, od_norm_ffn, router_w, router_b, moe_w_gate, moe_w_up, moe_w_down, ple_norm, ple_w_gate, ple_w_proj, final_norm):
    B, S, D = x.shape
    depth = p.shape[0]
    assert D == D_MODEL and S % (2 * ROW_TILE) == 0 and B % MLSTM_BATCH == 0
    assert MOBA_TOPK <= S // MOBA_BLOCK <= MOBA_MAX_BLOCKS
    h = x
    for layer in range(depth):
        j = layer // 2
        if layer % 2 == 0:
            h2d = _even_layer(h, ev_norm_mix[j], ev_w_in[j], pool_w[j], pool_scale[j], ev_w_out[j],
                              ev_norm_ffn[j], ffn_w_gate[j], ffn_w_up[j], ffn_w_down[j])
        else:
            h2d = _odd_layer(h, od_norm_mix[j], od_w_in[j], conv_w[j], gate_b_i[j], gate_b_f[j],
                             mlstm_norm[j], mla_q_norm[j], mla_w_uq[j], mla_kv_norm[j], mla_w_ukv[j],
                             od_w_out[j], od_norm_ffn[j], router_w[j], router_b[j],
                             moe_w_gate[j], moe_w_up[j], moe_w_down[j])
        last = layer == depth - 1
        h2d = _ple(h2d, p[layer].reshape(B * S, PLE_DIM), ple_norm[layer], ple_w_gate[layer],
                   ple_w_proj[layer], final_g=final_norm if last else None)
        h = h2d.reshape(B, S, D)
    return h
```

```python
import functools
import math

import numpy as np
import jax
import jax.numpy as jnp
from jax import lax
from jax.experimental import pallas as pl
from jax.experimental.pallas import tpu as pltpu

F32 = jnp.float32
BF16 = jnp.bfloat16
HIGHEST = lax.Precision.HIGHEST

D_MODEL = 1024
PLE_DIM = 256
NORM_EPS = 1e-6
NEG_INF = -1e30

MOBA_HEADS = 8
MOBA_HEAD_DIM = 64
MOBA_BLOCK = 256
MOBA_TOPK = 3
POOL_WINDOWS = (2, 4, 8, 16)
POOL_GROUP_DIM = 128
POOL_HALO = 16
MLSTM_HEADS = 4
MLSTM_HEAD_DIM = 128
MLSTM_CHUNK = 128
MLSTM_BATCH = 2
assert MLSTM_CHUNK == 128
CONV_WIDTH = 4
MLA_HEADS = 4
MLA_Q_RANK = 256
MLA_KV_RANK = 128
MLA_NOPE_DIM = 64
MLA_ROPE_DIM = 32
MLA_V_DIM = 128
ROPE_BASE = 10000.0
FFN_DIM = 2816
N_EXPERTS = 8
EXPERT_DIM = 3584
MIX_WIDTH = 512

ATTN_BLOCK = 512
ROW_TILE = 512
LANES = 128
VMEM_LIMIT = 56 * 1024 * 1024

MISC_ROPE = 0
MISC_I = 32
MISC_F = 36


def _params(sem, vmem=VMEM_LIMIT):
    return pltpu.CompilerParams(dimension_semantics=sem, vmem_limit_bytes=vmem)


def _rms(x, g):
    ms = jnp.mean(x * x, axis=-1, keepdims=True)
    return x * lax.rsqrt(ms + NORM_EPS) * g


def _sigmoid(x):
    return 1.0 / (1.0 + jnp.exp(-x))


def _dot(a, b):
    return jnp.dot(a, b, preferred_element_type=F32)


def _dot_nt(a, b, precision=None):
    return lax.dot_general(a, b, (((1,), (1,)), ((), ())), precision=precision,
                           preferred_element_type=F32)


KAUG_SEL = MOBA_HEAD_DIM
KAUG_POS = KAUG_SEL + 32
MOBA_MAX_BLOCKS = KAUG_POS - KAUG_SEL


def _bf16_terms(x, n):
    out = []
    for _ in range(n):
        bits = np.float32(x).view(np.uint32)
        kept = np.uint32((int(bits) + 0x7FFF + ((int(bits) >> 16) & 1)) & 0xFFFF0000)
        term = float(kept.view(np.float32))
        out.append(term)
        x -= term
    return tuple(out)


LOG2E = math.log2(math.e)
LOG2E_TERMS = _bf16_terms(LOG2E, 3)


def _even_in_kernel(x_ref, g_ref, wn_ref, wqT_ref, wvT_ref, ext_ref,
                    ka_ref, ub_ref, km_ref, kn_ref, qT_ref, vT_ref):
    tm = x_ref.shape[1]
    xn = _rms(x_ref[0], g_ref[...]).astype(BF16)
    n = _dot(xn, wn_ref[...])
    ka = n[:, :MOBA_HEADS * LANES]
    ka_ref[0] = (ka + ext_ref[...].astype(F32)).astype(BF16)
    ub_ref[0] = n[:, MOBA_HEADS * LANES:]
    for j in range(tm // MOBA_BLOCK):
        km_ref[0, j] = jnp.mean(ka[j * MOBA_BLOCK:(j + 1) * MOBA_BLOCK], axis=0, keepdims=True)
    slot = lax.broadcasted_iota(jnp.int32, (MOBA_HEADS * LANES, LANES), 0) // LANES
    head = lax.broadcasted_iota(jnp.int32, (MOBA_HEADS * LANES, LANES), 1)
    kn_ref[0] = _dot((ka * ka).astype(BF16), (slot == head).astype(BF16))
    qT_ref[0] = _dot_nt(wqT_ref[...], xn)
    vT = _dot_nt(wvT_ref[...], xn)
    for j in range(tm // ATTN_BLOCK):
        vT_ref[0, j] = vT[:, j * ATTN_BLOCK:(j + 1) * ATTN_BLOCK].astype(BF16)


def _moba_key_extras(S, slopes):
    pos = np.arange(S)
    blk, off = pos // MOBA_BLOCK, pos % MOBA_BLOCK
    ext = np.zeros((S, MOBA_HEADS, LANES), np.float32)
    ext[pos, :, KAUG_SEL + blk] = 1.0
    for term in range(len(LOG2E_TERMS)):
        ext[:, :, KAUG_POS + 2 * term] = slopes[None, :] * (MOBA_BLOCK * blk)[:, None]
        ext[:, :, KAUG_POS + 2 * term + 1] = slopes[None, :] * off[:, None]
    return jnp.asarray(ext.reshape(S, MOBA_HEADS * LANES), dtype=BF16)


def _even_in(x, g, w_in, slopes):
    B, S, D = x.shape
    tm = ROW_TILE
    nb = S // MOBA_BLOCK
    wq, wk, wv, wu = (w_in[:, i * MIX_WIDTH:(i + 1) * MIX_WIDTH] for i in range(4))
    wk_slots = jnp.concatenate(
        [wk.reshape(D, MOBA_HEADS, MOBA_HEAD_DIM),
         jnp.zeros((D, MOBA_HEADS, LANES - MOBA_HEAD_DIM), F32)], axis=2).reshape(D, MOBA_HEADS * LANES)
    wn = jnp.concatenate([wk_slots, wu], axis=1).astype(BF16)
    wqT = wq.T.astype(BF16)
    wvT = wv.T.astype(BF16)
    const = lambda b, i: (0, 0)
    return pl.pallas_call(
        _even_in_kernel,
        grid=(B, S // tm),
        in_specs=[
            pl.BlockSpec((1, tm, D), lambda b, i: (b, i, 0)),
            pl.BlockSpec((1, D), const),
            pl.BlockSpec((D, MOBA_HEADS * LANES + MIX_WIDTH), const),
            pl.BlockSpec((MIX_WIDTH, D), const),
            pl.BlockSpec((MIX_WIDTH, D), const),
            pl.BlockSpec((tm, MOBA_HEADS * LANES), lambda b, i: (i, 0)),
        ],
        out_specs=[
            pl.BlockSpec((1, tm, MOBA_HEADS * LANES), lambda b, i: (b, i, 0)),
            pl.BlockSpec((1, tm, MIX_WIDTH), lambda b, i: (b, i, 0)),
            pl.BlockSpec((1, tm // MOBA_BLOCK, 1, MOBA_HEADS * LANES), lambda b, i: (b, i, 0, 0)),
            pl.BlockSpec((1, tm, LANES), lambda b, i: (b, i, 0)),
            pl.BlockSpec((1, MIX_WIDTH, tm), lambda b, i: (b, 0, i)),
            pl.BlockSpec((1, tm // ATTN_BLOCK, MIX_WIDTH, ATTN_BLOCK), lambda b, i: (b, i, 0, 0)),
        ],
        out_shape=[
            jax.ShapeDtypeStruct((B, S, MOBA_HEADS * LANES), BF16),
            jax.ShapeDtypeStruct((B, S, MIX_WIDTH), F32),
            jax.ShapeDtypeStruct((B, nb, 1, MOBA_HEADS * LANES), F32),
            jax.ShapeDtypeStruct((B, S, LANES), F32),
            jax.ShapeDtypeStruct((B, MIX_WIDTH, S), F32),
            jax.ShapeDtypeStruct((B, S // ATTN_BLOCK, MIX_WIDTH, ATTN_BLOCK), BF16),
        ],
        compiler_params=_params(("parallel", "parallel")),
        name="even_in",
    )(x, g.reshape(1, D), wn, wqT, wvT, _moba_key_extras(S, slopes))


def _moba_gate_kernel(km_ref, qT_ref, qa_ref, qn_ref):
    i = pl.program_id(1)
    nb = km_ref.shape[2]
    tq = qT_ref.shape[2]
    row = lax.broadcasted_iota(jnp.int32, (nb, tq), 0)
    own = (i * tq + lax.broadcasted_iota(jnp.int32, (nb, tq), 1)) // MOBA_BLOCK
    past = row < own
    tail_row = lax.broadcasted_iota(jnp.int32, (LANES - KAUG_POS, tq), 0)
    tail = jnp.zeros(tail_row.shape, F32)
    for term, value in enumerate(LOG2E_TERMS):
        tail = jnp.where(tail_row // 2 == term, F32(value), tail)
    tail = tail.astype(BF16)
    pad = jnp.zeros((MOBA_MAX_BLOCKS - nb, tq), BF16) if nb < MOBA_MAX_BLOCKS else None
    for h in range(MOBA_HEADS):
        q_h = qT_ref[0, h * MOBA_HEAD_DIM:(h + 1) * MOBA_HEAD_DIM, :]
        qn_ref[0, h:h + 1, :] = jnp.sum(q_h * q_h, axis=0, keepdims=True)
        gate = jnp.dot(km_ref[0, h], q_h, precision=HIGHEST, preferred_element_type=F32)
        gate = jnp.where(past, gate, NEG_INF)
        chosen = jnp.zeros(gate.shape, F32)
        for _ in range(MOBA_TOPK):
            mx = jnp.max(gate, axis=0, keepdims=True)
            first = jnp.min(jnp.where(gate == mx, row, nb), axis=0, keepdims=True)
            pick = row == first
            chosen = jnp.where(pick, 1.0, chosen)
            gate = jnp.where(pick, -jnp.inf, gate)
        keep = jnp.where(past, chosen, (row == own).astype(F32))
        sel = jnp.where(keep > 0.0, 0.0, NEG_INF).astype(BF16)
        base = h * LANES
        qa_ref[0, base:base + KAUG_SEL] = (q_h * (MOBA_HEAD_DIM ** -0.5 * LOG2E)).astype(BF16)
        qa_ref[0, base + KAUG_SEL:base + KAUG_SEL + nb] = sel
        if pad is not None:
            qa_ref[0, base + KAUG_SEL + nb:base + KAUG_POS] = pad
        qa_ref[0, base + KAUG_POS:base + LANES] = tail


def _moba_gate(kmean, qT):
    B, H, nb, dh = kmean.shape
    S = qT.shape[2]
    tq = ATTN_BLOCK
    return pl.pallas_call(
        _moba_gate_kernel,
        grid=(B, S // tq),
        in_specs=[
            pl.BlockSpec((1, H, nb, dh), lambda b, i: (b, 0, 0, 0)),
            pl.BlockSpec((1, H * dh, tq), lambda b, i: (b, 0, i)),
        ],
        out_specs=[pl.BlockSpec((1, H * LANES, tq), lambda b, i: (b, 0, i)),
                   pl.BlockSpec((1, H, tq), lambda b, i: (b, 0, i))],
        out_shape=[jax.ShapeDtypeStruct((B, H * LANES, S), BF16), jax.ShapeDtypeStruct((B, H, S), F32)],
        compiler_params=_params(("parallel", "parallel")),
        name="moba_gate",
    )(kmean, qT)


HEADS_PER_STEP = 4
SUM_ROWS = 16


def _attn_kernel(lo_ref, q_ref, k_ref, v_ref, o_ref, sa_ref, sb_ref, m_ref, acc_ref):
    i = pl.program_id(2)
    lo = lo_ref[(pl.program_id(0) * pl.num_programs(1) + pl.program_id(1)) * pl.num_programs(2) + i]
    n_past = i - lo
    tq = q_ref.shape[2]
    tk = ATTN_BLOCK
    hp = HEADS_PER_STEP
    dv = v_ref.shape[2] // hp
    m_ref[...] = jnp.full(m_ref.shape, NEG_INF, F32)
    acc_ref[...] = jnp.zeros(acc_ref.shape, F32)
    ones_rows = jnp.ones((SUM_ROWS, tk), BF16)

    def scores(kvt, s_ref, diag):
        start = pl.multiple_of(kvt * tk, tk)
        k_tile = k_ref[0, pl.ds(start, tk), :]
        for g in range(hp):
            s = _dot(k_tile[:, g * LANES:(g + 1) * LANES], q_ref[0, g * LANES:(g + 1) * LANES, :])
            if diag:
                key = lax.broadcasted_iota(jnp.int32, (tk, tq), 0)
                qry = lax.broadcasted_iota(jnp.int32, (tk, tq), 1)
                s = jnp.where(key <= qry, s, NEG_INF)
            s_ref[g] = s

    def consume(kvt, s_ref):
        v_tile = v_ref[0, kvt]
        for g in range(hp):
            s = s_ref[g]
            m_run = m_ref[g]
            m_new = jnp.maximum(m_run, jnp.max(s, axis=0, keepdims=True))
            p = jnp.exp2(s - m_new).astype(BF16)
            v_aug = jnp.concatenate([v_tile[g * dv:(g + 1) * dv], ones_rows], axis=0)
            acc_ref[g] = jnp.exp2(m_run - m_new) * acc_ref[g] + _dot(v_aug, p)
            m_ref[g] = m_new

    tile_at = lambda t: jnp.where(t == 0, i, lo + t - 1)
    scores(i, sa_ref, True)

    def pair(p, carry):
        t = 2 * p
        scores(tile_at(t + 1), sb_ref, False)
        consume(tile_at(t), sa_ref)
        scores(tile_at(t + 2), sa_ref, False)
        consume(tile_at(t + 1), sb_ref)
        return carry

    lax.fori_loop(0, n_past // 2, pair, 0)
    last = 2 * (n_past // 2)

    @pl.when(n_past % 2 == 1)
    def _():
        scores(tile_at(last + 1), sb_ref, False)
        consume(tile_at(last), sa_ref)
        consume(tile_at(last + 1), sb_ref)

    @pl.when(n_past % 2 == 0)
    def _():
        consume(tile_at(last), sa_ref)

    for g in range(hp):
        o_ref[0, g * dv:(g + 1) * dv, :] = (acc_ref[g, :dv] / acc_ref[g, dv:dv + 1]).astype(o_ref.dtype)


def _attention(qT, k, vT, dv, heads, name, first_tile=None):
    B, _, S = qT.shape
    tq = ATTN_BLOCK
    nt = S // ATTN_BLOCK
    hp = HEADS_PER_STEP
    if first_tile is None:
        first_tile = jnp.zeros((B, heads // hp, nt), jnp.int32)
    return pl.pallas_call(
        _attn_kernel,
        grid_spec=pltpu.PrefetchScalarGridSpec(
            num_scalar_prefetch=1,
            grid=(B, heads // hp, S // tq),
            in_specs=[
                pl.BlockSpec((1, hp * LANES, tq), lambda b, h, i, lo: (b, h, i)),
                pl.BlockSpec((1, S, hp * LANES), lambda b, h, i, lo: (b, 0, h)),
                pl.BlockSpec((1, nt, hp * dv, ATTN_BLOCK), lambda b, h, i, lo: (b, 0, h, 0)),
            ],
            out_specs=pl.BlockSpec((1, hp * dv, tq), lambda b, h, i, lo: (b, h, i)),
            scratch_shapes=[pltpu.VMEM((hp, ATTN_BLOCK, tq), F32), pltpu.VMEM((hp, ATTN_BLOCK, tq), F32),
                            pltpu.VMEM((hp, 1, tq), F32), pltpu.VMEM((hp, dv + SUM_ROWS, tq), F32)],
        ),
        out_shape=jax.ShapeDtypeStruct((B, heads * dv, S), BF16),
        compiler_params=_params(("parallel", "parallel", "arbitrary")),
        name=name,
    )(first_tile.reshape(-1), qT, k, vT)


def _pool_kernel(x_ref, halo_ref, w_ref, sc_ref, o_ref, xs_ref):
    i = pl.program_id(1)
    tm = x_ref.shape[1]
    x = x_ref[0]
    xs_ref[0:POOL_HALO] = jnp.where(i > 0, halo_ref[0], 0.0)
    xs_ref[POOL_HALO:POOL_HALO + tm] = x
    t = i * tm + lax.broadcasted_iota(jnp.int32, (tm, 1), 0)
    outs = []
    for g, win in enumerate(POOL_WINDOWS):
        lo, hi = g * POOL_GROUP_DIM, (g + 1) * POOL_GROUP_DIM
        xg = x[:, lo:hi]
        acc = xg
        for d in range(1, win):
            acc = acc + xs_ref[POOL_HALO - d:POOL_HALO - d + tm, lo:hi]
        count = jnp.minimum(t + 1, win).astype(F32)
        outs.append(_dot((acc / count - xg).astype(BF16), w_ref[g]))
    o_ref[0] = (jnp.concatenate(outs, axis=1) * sc_ref[...]).astype(o_ref.dtype)


def _pool(ub, pool_w, pool_scale):
    B, S, W = ub.shape
    tm = ROW_TILE
    per = tm // POOL_HALO
    return pl.pallas_call(
        _pool_kernel,
        grid=(B, S // tm),
        in_specs=[
            pl.BlockSpec((1, tm, W), lambda b, i: (b, i, 0)),
            pl.BlockSpec((1, POOL_HALO, W), lambda b, i: (b, jnp.maximum(i * per - 1, 0), 0)),
            pl.BlockSpec(pool_w.shape, lambda b, i: (0, 0, 0)),
            pl.BlockSpec((1, W), lambda b, i: (0, 0)),
        ],
        out_specs=pl.BlockSpec((1, tm, W), lambda b, i: (b, i, 0)),
        out_shape=jax.ShapeDtypeStruct((B, S, W), BF16),
        scratch_shapes=[pltpu.VMEM((POOL_HALO + tm, W), F32)],
        compiler_params=_params(("parallel", "parallel")),
        name="pool",
    )(ub, ub, pool_w.astype(BF16), pool_scale.reshape(1, W))


def _mix_out_kernel(h_ref, aT_ref, b_ref, w_ref, o_ref, *, a_first):
    a = aT_ref[0].astype(F32).T.astype(BF16)
    b = b_ref[0].astype(BF16)
    lo, hi = (a, b) if a_first else (b, a)
    y = _dot(lo, w_ref[:MIX_WIDTH]) + _dot(hi, w_ref[MIX_WIDTH:])
    o_ref[0] = h_ref[0] + y


def _mix_out(h, aT, b, w_out, a_first):
    B, S, D = h.shape
    tm = ROW_TILE
    return pl.pallas_call(
        functools.partial(_mix_out_kernel, a_first=a_first),
        grid=(B, S // tm),
        in_specs=[
            pl.BlockSpec((1, tm, D), lambda b_, i: (b_, i, 0)),
            pl.BlockSpec((1, MIX_WIDTH, tm), lambda b_, i: (b_, 0, i)),
            pl.BlockSpec((1, tm, MIX_WIDTH), lambda b_, i: (b_, i, 0)),
            pl.BlockSpec((2 * MIX_WIDTH, D), lambda b_, i: (0, 0)),
        ],
        out_specs=pl.BlockSpec((1, tm, D), lambda b_, i: (b_, i, 0)),
        out_shape=jax.ShapeDtypeStruct((B, S, D), F32),
        compiler_params=_params(("parallel", "parallel")),
        name="mix_out",
    )(h, aT, b, w_out.astype(BF16))


def _swiglu_step(xn, wg, wu, wd):
    gt = _dot(xn, wg)
    up = _dot(xn, wu)
    return _dot((gt * _sigmoid(gt) * up).astype(BF16), wd)


def _ffn_kernel(h_ref, g_ref, wg_ref, wu_ref, wd_ref, o_ref):
    h = h_ref[...]
    xn = _rms(h, g_ref[...]).astype(BF16)
    o_ref[...] = h + _swiglu_step(xn, wg_ref[...], wu_ref[...], wd_ref[...])


def _ffn(h2d, g, wg, wu, wd, tm=ROW_TILE):
    T, D = h2d.shape
    F = wg.shape[1]
    resident = lambda shape: pl.BlockSpec(shape, lambda i: (0, 0), pipeline_mode=pl.Buffered(1))
    return pl.pallas_call(
        _ffn_kernel,
        grid=(T // tm,),
        in_specs=[
            pl.BlockSpec((tm, D), lambda i: (i, 0)),
            pl.BlockSpec((1, D), lambda i: (0, 0)),
            resident((D, F)),
            resident((D, F)),
            resident((F, D)),
        ],
        out_specs=pl.BlockSpec((tm, D), lambda i: (i, 0)),
        out_shape=jax.ShapeDtypeStruct((T, D), F32),
        compiler_params=_params(("parallel",)),
        name="ffn",
    )(h2d, g.reshape(1, D), wg, wu, wd)


ROUTE_E0, ROUTE_E1, ROUTE_W0, ROUTE_W1, ROUTE_R0, ROUTE_R1 = range(6)
MOE_TILE = 512
MOE_TF = EXPERT_DIM // 2


def _lane_pick(tile, lane, idx):
    return jnp.sum(jnp.where(lane == idx, tile, 0.0), axis=1, keepdims=True)


ROUTER_ROWS = 16


def _router_kernel(h_ref, g_ref, wT_ref, b_ref, route_ref, cnt_ref):
    tm = h_ref.shape[0]

    @pl.when(pl.program_id(0) == 0)
    def _():
        cnt_ref[...] = jnp.zeros_like(cnt_ref)

    xn = _rms(h_ref[...], g_ref[...])
    logits = _dot_nt(wT_ref[...], xn, precision=HIGHEST) + b_ref[...]
    row = lax.broadcasted_iota(jnp.int32, logits.shape, 0)
    logits = jnp.where(row < N_EXPERTS, logits, -jnp.inf)
    v0 = jnp.max(logits, axis=0, keepdims=True)
    i0 = jnp.min(jnp.where(logits == v0, row, ROUTER_ROWS), axis=0, keepdims=True)
    rest = jnp.where(row == i0, -jnp.inf, logits)
    v1 = jnp.max(rest, axis=0, keepdims=True)
    i1 = jnp.min(jnp.where(rest == v1, row, ROUTER_ROWS), axis=0, keepdims=True)
    e1 = jnp.exp(v1 - v0)
    w0 = 1.0 / (1.0 + e1)
    sel = (row == i0).astype(F32) + (row == i1).astype(F32)
    earlier = (lax.broadcasted_iota(jnp.int32, (tm, tm), 0)
               < lax.broadcasted_iota(jnp.int32, (tm, tm), 1))
    counts = cnt_ref[:, 0:1]
    rank = _dot(sel.astype(BF16), earlier.astype(BF16)) + counts
    cnt_ref[...] = jnp.broadcast_to(counts + jnp.sum(sel, axis=1, keepdims=True), cnt_ref.shape)
    pick = lambda idx: jnp.sum(jnp.where(row == idx, rank, 0.0), axis=0, keepdims=True)
    rows = [None] * 6
    rows[ROUTE_E0], rows[ROUTE_E1] = i0.astype(F32), i1.astype(F32)
    rows[ROUTE_W0], rows[ROUTE_W1] = w0, e1 * w0
    rows[ROUTE_R0], rows[ROUTE_R1] = pick(i0), pick(i1)
    routeT = jnp.concatenate(rows + [jnp.zeros((LANES - len(rows), tm), F32)], axis=0)
    route_ref[...] = routeT.T


def _router(h2d, g, router_w, router_b):
    T, D = h2d.shape
    tm = ROW_TILE
    wT = jnp.zeros((ROUTER_ROWS, D), F32).at[:N_EXPERTS].set(router_w.T)
    b = jnp.zeros((ROUTER_ROWS, 1), F32).at[:N_EXPERTS, 0].set(router_b)
    return pl.pallas_call(
        _router_kernel,
        grid=(T // tm,),
        in_specs=[
            pl.BlockSpec((tm, D), lambda i: (i, 0)),
            pl.BlockSpec((1, D), lambda i: (0, 0)),
            pl.BlockSpec((ROUTER_ROWS, D), lambda i: (0, 0)),
            pl.BlockSpec((ROUTER_ROWS, 1), lambda i: (0, 0)),
        ],
        out_specs=[pl.BlockSpec((tm, LANES), lambda i: (i, 0)),
                   pl.BlockSpec((ROUTER_ROWS, LANES), lambda i: (0, 0))],
        out_shape=[jax.ShapeDtypeStruct((T, LANES), F32), jax.ShapeDtypeStruct((ROUTER_ROWS, LANES), F32)],
        compiler_params=_params(("arbitrary",)),
        name="router",
    )(h2d, g.reshape(1, D), wT, b)


SUBLANES = 8
ROW_LANES = D_MODEL // SUBLANES
assert ROW_LANES == LANES


def _row_tile(ref, r):
    return ref.at[pl.ds(pl.multiple_of(r * SUBLANES, SUBLANES), SUBLANES), :]


def _to_row_tiles(ref, x):
    tm = x.shape[0]
    for s in range(SUBLANES):
        ref[pl.ds(s, tm, stride=SUBLANES), :] = x[:, s * LANES:(s + 1) * LANES]


def _from_row_tiles(ref):
    tm = ref.shape[0] // SUBLANES
    return jnp.concatenate([ref[pl.ds(s, tm, stride=SUBLANES), :] for s in range(SUBLANES)], axis=1)


def _row_copies(pos_ref, base, r, src_of, dst_of, sem):
    return [pltpu.make_async_copy(src_of(k, pos_ref[base + 2 * r + k]),
                                  dst_of(k, pos_ref[base + 2 * r + k]), sem) for k in range(2)]


def _start_rows(tm, make):
    def issue(r, c):
        for k, cp in enumerate(make(r)):
            cp.start(priority=k)
        return c

    lax.fori_loop(0, tm, issue, 0, unroll=8)


def _wait_rows(tm, make):
    def drain(r, c):
        for cp in make(r):
            cp.wait()
        return c

    lax.fori_loop(0, tm, drain, 0, unroll=8)


def _dispatch_kernel(pos_ref, h_ref, g_ref, init_ref, xs_ref, xn_ref, sem):
    del init_ref
    tm = h_ref.shape[0]
    i = pl.program_id(0)
    slot = i % 2

    def scatters(tile, buf):
        return lambda r: _row_copies(
            pos_ref, tile * (2 * tm), r, lambda k, p: _row_tile(xn_ref.at[buf], r),
            lambda k, p: _row_tile(xs_ref, p), sem.at[buf])

    _to_row_tiles(xn_ref.at[slot], _rms(h_ref[...], g_ref[...]))
    _start_rows(tm, scatters(i, slot))

    @pl.when(i > 0)
    def _():
        _wait_rows(tm, scatters(i - 1, 1 - slot))

    @pl.when(i == pl.num_programs(0) - 1)
    def _():
        _wait_rows(tm, scatters(i, slot))


def _dispatch(pos, h2d, g, n_rows):
    T, D = h2d.shape
    tm = ROW_TILE
    return pl.pallas_call(
        _dispatch_kernel,
        grid_spec=pltpu.PrefetchScalarGridSpec(
            num_scalar_prefetch=1,
            grid=(T // tm,),
            in_specs=[pl.BlockSpec((tm, D), lambda i, pos: (i, 0)),
                      pl.BlockSpec((1, D), lambda i, pos: (0, 0)),
                      pl.BlockSpec(memory_space=pl.ANY)],
            out_specs=pl.BlockSpec(memory_space=pl.ANY),
            scratch_shapes=[pltpu.VMEM((2, tm * SUBLANES, ROW_LANES), F32), pltpu.SemaphoreType.DMA((2,))],
        ),
        out_shape=jax.ShapeDtypeStruct((n_rows * SUBLANES, ROW_LANES), F32),
        input_output_aliases={3: 0},
        compiler_params=_params(("arbitrary",)),
        name="moe_dispatch",
    )(pos, h2d, g.reshape(1, D), jnp.zeros((n_rows * SUBLANES, ROW_LANES), F32))


def _moe_ffn_kernel(te_ref, nv_ref, x_ref, wg_ref, wu_ref, wd_ref, o_ref, xb_ref, acc_ref):
    j = pl.program_id(0)
    f = pl.program_id(1)
    valid = j < nv_ref[0]

    @pl.when(f == 0)
    def _():
        xb_ref[...] = _from_row_tiles(x_ref).astype(BF16)
        acc_ref[...] = jnp.zeros_like(acc_ref)

    @pl.when(valid)
    def _():
        acc_ref[...] += _swiglu_step(xb_ref[...], wg_ref[0], wu_ref[0], wd_ref[0])

    @pl.when(f == pl.num_programs(1) - 1)
    def _():
        _to_row_tiles(o_ref, acc_ref[...])


def _moe_ffn(tile_expert, n_valid, xs, wg, wu, wd):
    N, D = xs.shape[0] // SUBLANES, D_MODEL
    tm, tf = MOE_TILE, MOE_TF
    F = wg.shape[2]
    return pl.pallas_call(
        _moe_ffn_kernel,
        grid_spec=pltpu.PrefetchScalarGridSpec(
            num_scalar_prefetch=2,
            grid=(N // tm, F // tf),
            in_specs=[pl.BlockSpec((tm * SUBLANES, ROW_LANES), lambda j, f, te, nv: (j, 0)),
                      pl.BlockSpec((1, D, tf), lambda j, f, te, nv: (te[j], 0, f)),
                      pl.BlockSpec((1, D, tf), lambda j, f, te, nv: (te[j], 0, f)),
                      pl.BlockSpec((1, tf, D), lambda j, f, te, nv: (te[j], f, 0))],
            out_specs=pl.BlockSpec((tm * SUBLANES, ROW_LANES), lambda j, f, te, nv: (j, 0)),
            scratch_shapes=[pltpu.VMEM((tm, D), BF16), pltpu.VMEM((tm, D), F32)],
        ),
        out_shape=jax.ShapeDtypeStruct(xs.shape, F32),
        compiler_params=_params(("arbitrary", "arbitrary")),
        name="moe_ffn",
    )(tile_expert, n_valid, xs, wg, wu, wd)


def _combine_kernel(pos_ref, h_ref, route_ref, ys_ref, o_ref, y_ref, sem):
    tm = h_ref.shape[0]
    i = pl.program_id(0)
    slot = i % 2

    def gathers(tile, buf):
        return lambda r: _row_copies(
            pos_ref, tile * (2 * tm), r, lambda k, p: _row_tile(ys_ref, p),
            lambda k, p: _row_tile(y_ref.at[buf, k], r), sem.at[buf])

    @pl.when(i == 0)
    def _():
        _start_rows(tm, gathers(0, 0))

    @pl.when(i + 1 < pl.num_programs(0))
    def _():
        _start_rows(tm, gathers(i + 1, 1 - slot))

    _wait_rows(tm, gathers(i, slot))
    route = route_ref[...]
    lane = lax.broadcasted_iota(jnp.int32, route.shape, 1)
    o_ref[...] = (h_ref[...] + _lane_pick(route, lane, ROUTE_W0) * _from_row_tiles(y_ref.at[slot, 0])
                  + _lane_pick(route, lane, ROUTE_W1) * _from_row_tiles(y_ref.at[slot, 1]))


def _combine(pos, h2d, route, ys):
    T, D = h2d.shape
    tm = ROW_TILE
    return pl.pallas_call(
        _combine_kernel,
        grid_spec=pltpu.PrefetchScalarGridSpec(
            num_scalar_prefetch=1,
            grid=(T // tm,),
            in_specs=[pl.BlockSpec((tm, D), lambda i, pos: (i, 0)),
                      pl.BlockSpec((tm, LANES), lambda i, pos: (i, 0)),
                      pl.BlockSpec(memory_space=pl.ANY)],
            out_specs=pl.BlockSpec((tm, D), lambda i, pos: (i, 0)),
            scratch_shapes=[pltpu.VMEM((2, 2, tm * SUBLANES, ROW_LANES), F32),
                            pltpu.SemaphoreType.DMA((2,))],
        ),
        out_shape=jax.ShapeDtypeStruct((T, D), F32),
        compiler_params=_params(("arbitrary",)),
        name="moe_combine",
    )(pos, h2d, route, ys)


def _moe(h2d, g, router_w, router_b, wg, wu, wd):
    T, D = h2d.shape
    tm = MOE_TILE
    route, counts = _router(h2d, g, router_w, router_b)
    cnt = counts[:N_EXPERTS, 0].astype(jnp.int32)
    padded = (cnt + tm - 1) // tm * tm
    ends = jnp.cumsum(padded)
    start = ends - padded
    e01 = route[:, ROUTE_E0:ROUTE_E1 + 1].astype(jnp.int32)
    r01 = route[:, ROUTE_R0:ROUTE_R1 + 1].astype(jnp.int32)
    pos = (start[e01] + r01).reshape(2 * T)
    n_rows = 2 * T + N_EXPERTS * tm
    tile_row = jnp.arange(n_rows // tm, dtype=jnp.int32) * tm
    tile_expert = jnp.minimum(jnp.sum(tile_row[:, None] >= ends[None, :], axis=1), N_EXPERTS - 1).astype(jnp.int32)
    n_valid = (ends[-1:] // tm).astype(jnp.int32)
    xs = _dispatch(pos, h2d, g, n_rows)
    ys = _moe_ffn(tile_expert, n_valid, xs, wg, wu, wd)
    return _combine(pos, h2d, route, ys)


def _ple_kernel(*refs, final):
    if final:
        h_ref, p_ref, g_ref, wg_ref, wp_ref, fg_ref, o_ref = refs
    else:
        h_ref, p_ref, g_ref, wg_ref, wp_ref, o_ref = refs
    h = h_ref[...]
    gate = _sigmoid(_dot(_rms(h, g_ref[...]).astype(BF16), wg_ref[...]))
    out = h + gate * _dot(p_ref[...].astype(BF16), wp_ref[...])
    if final:
        out = _rms(out, fg_ref[...])
    o_ref[...] = out


def _ple(h2d, p2d, g, w_gate, w_proj, final_g=None):
    T, D = h2d.shape
    tm = ROW_TILE
    final = final_g is not None
    in_specs = [
        pl.BlockSpec((tm, D), lambda i: (i, 0)),
        pl.BlockSpec((tm, PLE_DIM), lambda i: (i, 0)),
        pl.BlockSpec((1, D), lambda i: (0, 0)),
        pl.BlockSpec((D, D), lambda i: (0, 0)),
        pl.BlockSpec((PLE_DIM, D), lambda i: (0, 0)),
    ]
    args = [h2d, p2d, g.reshape(1, D), w_gate.astype(BF16), w_proj.astype(BF16)]
    if final:
        in_specs.append(pl.BlockSpec((1, D), lambda i: (0, 0)))
        args.append(final_g.reshape(1, D))
    return pl.pallas_call(
        functools.partial(_ple_kernel, final=final),
        grid=(T // tm,),
        in_specs=in_specs,
        out_specs=pl.BlockSpec((tm, D), lambda i: (i, 0)),
        out_shape=jax.ShapeDtypeStruct((T, D), F32),
        compiler_params=_params(("parallel",)),
        name="ple_final" if final else "ple",
    )(*args)


ODD_MAIN = 4 * MIX_WIDTH
ODD_COLS = ODD_MAIN + MLA_Q_RANK + MLA_KV_RANK + 2 * LANES
MLA_QK_SCALE = (MLA_NOPE_DIM + MLA_ROPE_DIM) ** -0.5 * LOG2E
ROPE_HALF = MLA_ROPE_DIM // 2


def _odd_in_kernel(x_ref, g_ref, wn_ref, qn_ref, wuqT_ref, kvn_ref, wk2_ref, wvT_ref,
                   cosT_ref, sinT_ref, cc_ref, ss_ref,
                   qk_ref, vc_ref, op_ref, misc_ref, mq_ref, mk_ref, mv_ref):
    tm = x_ref.shape[1]
    xn = _rms(x_ref[0], g_ref[...]).astype(BF16)
    u = _dot(xn, wn_ref[...])
    qk_ref[0] = u[:, :2 * MIX_WIDTH]
    vc_ref[0] = u[:, 2 * MIX_WIDTH:3 * MIX_WIDTH].astype(BF16)
    op_ref[0] = u[:, 3 * MIX_WIDTH:ODD_MAIN]
    c0 = ODD_MAIN
    c_q = u[:, c0:c0 + MLA_Q_RANK]
    c0 += MLA_Q_RANK
    c_kv = u[:, c0:c0 + MLA_KV_RANK]
    c0 += MLA_KV_RANK
    misc = u[:, c0:c0 + LANES]
    misc_sw = u[:, c0 + LANES:c0 + 2 * LANES]
    misc_ref[0] = misc
    cqn = _rms(c_q, qn_ref[...]).astype(BF16)
    qT = _dot_nt(wuqT_ref[...], cqn)
    cosT = cosT_ref[...]
    sinT = sinT_ref[...]
    for h in range(MLA_HEADS):
        r = h * LANES
        mq_ref[0, r:r + MLA_NOPE_DIM] = (qT[r:r + MLA_NOPE_DIM] * MLA_QK_SCALE).astype(BF16)
        x1 = qT[r + MLA_NOPE_DIM:r + MLA_NOPE_DIM + ROPE_HALF]
        x2 = qT[r + MLA_NOPE_DIM + ROPE_HALF:r + MLA_NOPE_DIM + MLA_ROPE_DIM]
        mq_ref[0, r + MLA_NOPE_DIM:r + MLA_NOPE_DIM + ROPE_HALF] = (
            (x1 * cosT - x2 * sinT) * MLA_QK_SCALE).astype(BF16)
        mq_ref[0, r + MLA_NOPE_DIM + ROPE_HALF:r + MLA_NOPE_DIM + MLA_ROPE_DIM] = (
            (x1 * sinT + x2 * cosT) * MLA_QK_SCALE).astype(BF16)
        mq_ref[0, r + MLA_NOPE_DIM + MLA_ROPE_DIM:r + LANES] = jnp.zeros(
            (LANES - MLA_NOPE_DIM - MLA_ROPE_DIM, tm), BF16)
    ckvn = _rms(c_kv, kvn_ref[...]).astype(BF16)
    k_rot = (misc * cc_ref[...] + misc_sw * ss_ref[...]).astype(BF16)
    mk_ref[0] = _dot(jnp.concatenate([ckvn, k_rot], axis=1), wk2_ref[...]).astype(BF16)
    vT = _dot_nt(wvT_ref[...], ckvn)
    for j in range(tm // ATTN_BLOCK):
        mv_ref[0, j] = vT[:, j * ATTN_BLOCK:(j + 1) * ATTN_BLOCK].astype(BF16)


def _rope_tables(S):
    inv_freq = ROPE_BASE ** (-jnp.arange(ROPE_HALF, dtype=F32) / ROPE_HALF)
    ang = jnp.arange(S, dtype=F32)[:, None] * inv_freq[None, :]
    cos, sin = jnp.cos(ang), jnp.sin(ang)
    pad = jnp.zeros((S, LANES - MLA_ROPE_DIM), F32)
    cc = jnp.concatenate([cos, cos, pad], axis=1)
    ss = jnp.concatenate([-sin, sin, pad], axis=1)
    return cos.T, sin.T, cc, ss


def _odd_in(x, g, w_in, q_norm, w_uq, kv_norm, w_ukv):
    B, S, D = x.shape
    tm = ROW_TILE
    cuts = np.cumsum([MIX_WIDTH] * 4 + [MLSTM_HEADS, MLSTM_HEADS, MLA_Q_RANK, MLA_KV_RANK]).tolist()
    w_main = w_in[:, :cuts[3]]
    w_i = w_in[:, cuts[3]:cuts[4]]
    w_f = w_in[:, cuts[4]:cuts[5]]
    w_cq = w_in[:, cuts[5]:cuts[6]]
    w_ckv = w_in[:, cuts[6]:cuts[7]]
    w_kr = w_in[:, cuts[7]:]
    w_kr_sw = jnp.concatenate([w_kr[:, ROPE_HALF:], w_kr[:, :ROPE_HALF]], axis=1)
    zpad = lambda n: jnp.zeros((D, n), F32)
    w_misc = jnp.concatenate([w_kr, w_i, w_f, zpad(LANES - MLA_ROPE_DIM - 2 * MLSTM_HEADS)], axis=1)
    w_misc_sw = jnp.concatenate([w_kr_sw, zpad(LANES - MLA_ROPE_DIM)], axis=1)
    wn = jnp.concatenate([w_main, w_cq, w_ckv, w_misc, w_misc_sw], axis=1).astype(BF16)
    qd = MLA_NOPE_DIM + MLA_ROPE_DIM
    w_uq_h = w_uq.reshape(MLA_Q_RANK, MLA_HEADS, qd)
    w_uq_h = jnp.concatenate([w_uq_h, jnp.zeros((MLA_Q_RANK, MLA_HEADS, LANES - qd), F32)], axis=2)
    wuqT = w_uq_h.reshape(MLA_Q_RANK, MLA_HEADS * LANES).T.astype(BF16)
    w_ukv_h = w_ukv.reshape(MLA_KV_RANK, MLA_HEADS, MLA_NOPE_DIM + MLA_V_DIM)
    w_k = jnp.concatenate([w_ukv_h[:, :, :MLA_NOPE_DIM],
                           jnp.zeros((MLA_KV_RANK, MLA_HEADS, LANES - MLA_NOPE_DIM), F32)], axis=2)
    place = jnp.zeros((LANES, MLA_HEADS, LANES), F32)
    eye = jnp.eye(MLA_ROPE_DIM, dtype=F32)
    place = place.at[:MLA_ROPE_DIM, :, MLA_NOPE_DIM:MLA_NOPE_DIM + MLA_ROPE_DIM].set(
        jnp.broadcast_to(eye[:, None, :], (MLA_ROPE_DIM, MLA_HEADS, MLA_ROPE_DIM)))
    wk2 = jnp.concatenate([w_k, place], axis=0).reshape(MLA_KV_RANK + LANES, MLA_HEADS * LANES).astype(BF16)
    wvT = w_ukv_h[:, :, MLA_NOPE_DIM:].reshape(MLA_KV_RANK, MLA_HEADS * MLA_V_DIM).T.astype(BF16)
    cosT, sinT, cc, ss = _rope_tables(S)
    row = lambda b, i: (b, i, 0)
    const = lambda b, i: (0, 0)
    nb = S // ATTN_BLOCK
    return pl.pallas_call(
        _odd_in_kernel,
        grid=(B, S // tm),
        in_specs=[
            pl.BlockSpec((1, tm, D), row),
            pl.BlockSpec((1, D), const),
            pl.BlockSpec((D, ODD_COLS), const),
            pl.BlockSpec((1, MLA_Q_RANK), const),
            pl.BlockSpec((MLA_HEADS * LANES, MLA_Q_RANK), const),
            pl.BlockSpec((1, MLA_KV_RANK), const),
            pl.BlockSpec((MLA_KV_RANK + LANES, MLA_HEADS * LANES), const),
            pl.BlockSpec((MLA_HEADS * MLA_V_DIM, MLA_KV_RANK), const),
            pl.BlockSpec((ROPE_HALF, tm), lambda b, i: (0, i)),
            pl.BlockSpec((ROPE_HALF, tm), lambda b, i: (0, i)),
            pl.BlockSpec((tm, LANES), lambda b, i: (i, 0)),
            pl.BlockSpec((tm, LANES), lambda b, i: (i, 0)),
        ],
        out_specs=[
            pl.BlockSpec((1, tm, 2 * MIX_WIDTH), row),
            pl.BlockSpec((1, tm, MIX_WIDTH), row),
            pl.BlockSpec((1, tm, MIX_WIDTH), row),
            pl.BlockSpec((1, tm, LANES), row),
            pl.BlockSpec((1, MLA_HEADS * LANES, tm), lambda b, i: (b, 0, i)),
            pl.BlockSpec((1, tm, MLA_HEADS * LANES), row),
            pl.BlockSpec((1, tm // ATTN_BLOCK, MLA_HEADS * MLA_V_DIM, ATTN_BLOCK), lambda b, i: (b, i, 0, 0)),
        ],
        out_shape=[
            jax.ShapeDtypeStruct((B, S, 2 * MIX_WIDTH), F32),
            jax.ShapeDtypeStruct((B, S, MIX_WIDTH), BF16),
            jax.ShapeDtypeStruct((B, S, MIX_WIDTH), F32),
            jax.ShapeDtypeStruct((B, S, LANES), F32),
            jax.ShapeDtypeStruct((B, MLA_HEADS * LANES, S), BF16),
            jax.ShapeDtypeStruct((B, S, MLA_HEADS * LANES), BF16),
            jax.ShapeDtypeStruct((B, nb, MLA_HEADS * MLA_V_DIM, ATTN_BLOCK), BF16),
        ],
        compiler_params=_params(("parallel", "parallel")),
        name="odd_in",
    )(x, g.reshape(1, D), wn, q_norm.reshape(1, -1), wuqT, kv_norm.reshape(1, -1), wk2, wvT,
      cosT, sinT, cc, ss)


def _log_sigmoid(x):
    return jnp.minimum(x, 0.0) - jnp.log(1.0 + jnp.exp(-jnp.abs(x)))


def _mlstm_kernel(qk_ref, v_ref, op_ref, misc_ref, cw_ref, gb_ref, hn_ref, o_ref,
                  prev_ref, cn_ref, m_ref):
    c = pl.program_id(1)
    L = qk_ref.shape[1]
    row = lax.broadcasted_iota(jnp.int32, (L, 1), 0)
    lane = lax.broadcasted_iota(jnp.int32, (L, LANES), 1)
    is_f = (lane >= MISC_F) & (lane < MISC_F + MLSTM_HEADS)
    causal = lax.broadcasted_iota(jnp.int32, (L, L), 1) <= lax.broadcasted_iota(jnp.int32, (L, L), 0)

    @pl.when(c == 0)
    def _():
        prev_ref[...] = jnp.zeros_like(prev_ref)
        cn_ref[...] = jnp.zeros_like(cn_ref)
        m_ref[...] = jnp.zeros_like(m_ref)

    seqs = range(qk_ref.shape[0])
    heads = [(bb, h) for bb in seqs for h in range(MLSTM_HEADS)]
    d = MLSTM_HEAD_DIM

    qk = []
    for bb in seqs:
        x = qk_ref[bb]
        prev = prev_ref[bb]
        conv = x * cw_ref[CONV_WIDTH - 1:CONV_WIDTH, :]
        for j in range(1, CONV_WIDTH):
            shifted = pltpu.roll(jnp.where(row >= L - j, prev, x), j, axis=0)
            conv = conv + shifted * cw_ref[CONV_WIDTH - 1 - j:CONV_WIDTH - j, :]
        prev_ref[bb] = x
        qk.append(conv * _sigmoid(conv))

    sel_r = lax.broadcasted_iota(jnp.int32, (LANES, 2 * MIX_WIDTH), 0)
    sel_c = lax.broadcasted_iota(jnp.int32, (LANES, 2 * MIX_WIDTH), 1)
    spread = (sel_r == MISC_I + sel_c // LANES).astype(F32)
    pick = (lax.broadcasted_iota(jnp.int32, (8, LANES), 1)
            == MISC_I + lax.broadcasted_iota(jnp.int32, (8, LANES), 0)).astype(F32)
    mean_mat = jnp.full((d, d), 1.0 / d, F32)
    cols, rows = [], []
    for bb in seqs:
        gates = misc_ref[bb] + gb_ref[...]
        z = jnp.where(is_f, _log_sigmoid(gates), gates)
        cum = jnp.dot(causal.astype(F32), z, precision=HIGHEST, preferred_element_type=F32)
        z = jnp.where(is_f, cum, z)
        cols.append(jnp.dot(z, spread, precision=HIGHEST, preferred_element_type=F32))
        rows.append(_dot_nt(pick, z, precision=HIGHEST))

    q, k, v_aug, i_b, b_b, m_prev, m_t, w_inter, scores, inter = ({} for _ in range(10))
    ones_blk = jnp.ones((L, LANES), BF16)
    for key in heads:
        bb, h = key
        lo, hi = h * d, (h + 1) * d
        q[key] = qk[bb][:, lo:hi].astype(BF16)
        k[key] = qk[bb][:, MIX_WIDTH + lo:MIX_WIDTH + hi] * (d ** -0.5)
        v_aug[key] = jnp.concatenate([v_ref[bb, :, lo:hi], ones_blk], axis=1)
        scores[key] = _dot_nt(q[key], k[key].astype(BF16))
        inter[key] = _dot(q[key], cn_ref[bb, h].astype(BF16))
    intra = {}
    for key in heads:
        bb, h = key
        i_b[key] = cols[bb][:, h * LANES:(h + 1) * LANES]
        b_b[key] = cols[bb][:, (MLSTM_HEADS + h) * LANES:(MLSTM_HEADS + h + 1) * LANES]
        i_row = rows[bb][h:h + 1, :]
        b_row = rows[bb][MLSTM_HEADS + h:MLSTM_HEADS + h + 1, :]
        m_prev[key] = m_ref[bb, h:h + 1, :]
        intra[key] = jnp.where(causal, b_b[key] - b_row + i_row, NEG_INF)
    for key in heads:
        m_inter = b_b[key] + m_prev[key]
        m_t[key] = jnp.maximum(m_inter, jnp.max(intra[key], axis=1, keepdims=True))
        w_inter[key] = jnp.exp(m_inter - m_t[key])
    intra_o = {}
    for key in heads:
        a = jnp.exp(intra[key] - m_t[key]) * scores[key]
        intra_o[key] = _dot(a.astype(BF16), v_aug[key])
    for key in heads:
        bb, h = key
        lo, hi = h * d, (h + 1) * d
        num = w_inter[key] * inter[key][:, :d] + intra_o[key][:, :d]
        den = w_inter[key] * inter[key][:, d:] + intra_o[key][:, d:]
        hh = num / jnp.maximum(jnp.abs(den), jnp.exp(-m_t[key]))
        ms = jnp.dot(hh * hh, mean_mat, precision=HIGHEST, preferred_element_type=F32)
        hh = hh * lax.rsqrt(ms + NORM_EPS) * hn_ref[:, lo:hi]
        o_ref[bb, :, lo:hi] = (hh * _sigmoid(op_ref[bb, :, lo:hi])).astype(o_ref.dtype)
    for key in heads:
        bb, h = key
        b_end = b_b[key][L - 1:L, :]
        g = b_end - b_b[key] + i_b[key]
        m_new = jnp.maximum(b_end + m_prev[key], jnp.max(g, axis=0, keepdims=True))
        decay = jnp.exp(b_end + m_prev[key] - m_new)
        kw = k[key] * jnp.exp(g - m_new)
        cn_ref[bb, h] = (jnp.concatenate([decay, decay], axis=1) * cn_ref[bb, h]
                         + _dot(kw.T.astype(BF16), v_aug[key]))
        m_ref[bb, h:h + 1, :] = m_new


def _mlstm(qk_raw, vc, o_pre, misc, conv_w, b_i, b_f, head_norm):
    B, S, _ = qk_raw.shape
    L = MLSTM_CHUNK
    nb = MLSTM_BATCH
    gb = jnp.zeros((1, LANES), F32).at[0, MISC_I:MISC_I + MLSTM_HEADS].set(b_i)
    gb = gb.at[0, MISC_F:MISC_F + MLSTM_HEADS].set(b_f)
    row = lambda b, c: (b, c, 0)
    const = lambda b, c: (0, 0)
    return pl.pallas_call(
        _mlstm_kernel,
        grid=(B // nb, S // L),
        in_specs=[
            pl.BlockSpec((nb, L, 2 * MIX_WIDTH), row),
            pl.BlockSpec((nb, L, MIX_WIDTH), row),
            pl.BlockSpec((nb, L, MIX_WIDTH), row),
            pl.BlockSpec((nb, L, LANES), row),
            pl.BlockSpec((CONV_WIDTH, 2 * MIX_WIDTH), const),
            pl.BlockSpec((1, LANES), const),
            pl.BlockSpec((1, MIX_WIDTH), const),
        ],
        out_specs=pl.BlockSpec((nb, L, MIX_WIDTH), row),
        out_shape=jax.ShapeDtypeStruct((B, S, MIX_WIDTH), BF16),
        scratch_shapes=[
            pltpu.VMEM((nb, L, 2 * MIX_WIDTH), F32),
            pltpu.VMEM((nb, MLSTM_HEADS, MLSTM_HEAD_DIM, 2 * LANES), F32),
            pltpu.VMEM((nb, 8, LANES), F32),
        ],
        compiler_params=_params(("parallel", "arbitrary")),
        name="mlstm",
    )(qk_raw, vc, o_pre, misc, conv_w, gb, head_norm.reshape(1, MIX_WIDTH))


ZERO_WEIGHT_LOG2 = 160.0
NORM_SLACK = 1.02


def _alibi_first_tile(qn2, kn2, slopes):
    B, H, S = qn2.shape
    nt = S // ATTN_BLOCK
    k_max = jnp.sqrt(jnp.max(kn2[:, :, :H], axis=1))
    q_max = jnp.sqrt(jnp.max(qn2.reshape(B, H, nt, ATTN_BLOCK), axis=3))
    c = MOBA_HEAD_DIM ** -0.5 * LOG2E
    reach = ((ZERO_WEIGHT_LOG2 + 2.0 * NORM_SLACK * c * q_max * k_max[:, :, None])
             / (jnp.asarray(slopes)[None, :, None] * LOG2E))
    tiles = jnp.minimum(jnp.ceil((reach - 1.0) / ATTN_BLOCK), nt)
    tiles = jnp.max(tiles.reshape(B, H // HEADS_PER_STEP, HEADS_PER_STEP, nt), axis=2)
    first = jnp.arange(nt, dtype=F32)[None, None, :] - tiles
    return jnp.maximum(first, 0.0).astype(jnp.int32)


def _even_layer(h, norm_mix, w_in, pool_w, pool_scale, w_out, norm_ffn, wg, wu, wd):
    B, S, D = h.shape
    slopes = (2.0 ** (-8.0 * np.arange(1, MOBA_HEADS + 1) / MOBA_HEADS)).astype(np.float32)
    ka, ub, kmean, kn2, qT, vT = _even_in(h, norm_mix, w_in, slopes)
    nb = S // MOBA_BLOCK
    kmean = kmean.reshape(B, nb, MOBA_HEADS, LANES)[..., :MOBA_HEAD_DIM].transpose(0, 2, 1, 3)
    qaT, qn2 = _moba_gate(kmean, qT)
    aT = _attention(qaT, ka, vT, MOBA_HEAD_DIM, MOBA_HEADS, "moba_attn",
                    first_tile=_alibi_first_tile(qn2, kn2, slopes))
    b_out = _pool(ub, pool_w, pool_scale)
    h = _mix_out(h, aT, b_out, w_out, a_first=True)
    return _ffn(h.reshape(B * S, D), norm_ffn, wg.astype(BF16), wu.astype(BF16), wd.astype(BF16))


def _odd_layer(h, norm_mix, w_in, conv_w, b_i, b_f, head_norm, q_norm, w_uq, kv_norm, w_ukv,
               w_out, norm_ffn, router_w, router_b, wg, wu, wd):
    B, S, D = h.shape
    qk_raw, vc, o_pre, misc, mqT, mk, mvT = _odd_in(h, norm_mix, w_in, q_norm, w_uq, kv_norm, w_ukv)
    c_out = _mlstm(qk_raw, vc, o_pre, misc, conv_w, b_i, b_f, head_norm)
    dT = _attention(mqT, mk, mvT, MLA_V_DIM, MLA_HEADS, "mla_attn")
    h = _mix_out(h, dT, c_out, w_out, a_first=False)
    return _moe(h.reshape(B * S, D), norm_ffn, router_w, router_b,
                wg.astype(BF16), wu.astype(BF16), wd.astype(BF16))


def kernel(x, p, ev_norm_mix, ev_w_in, pool_w, pool_scale, ev_w_out, ev_norm_ffn, ffn_w_gate, ffn_w_up, ffn_w_down, od_norm_mix, od_w_in, conv_w, gate_b_i, gate_b_f, mlstm_norm, mla_q_norm, mla_w_uq, mla_kv_norm, mla_w_ukv, od_w_out, od_norm_ffn, router_w, router_b, moe_w_gate, moe_w_up, moe_w_down, ple_norm, ple_w_gate, ple_w_proj, final_norm):
    B, S, D = x.shape
    depth = p.shape[0]
    assert D == D_MODEL and S % (2 * ROW_TILE) == 0 and B % MLSTM_BATCH == 0
    assert MOBA_TOPK <= S // MOBA_BLOCK <= MOBA_MAX_BLOCKS
    h = x
    for layer in range(depth):
        j = layer // 2
        if layer % 2 == 0:
            h2d = _even_layer(h, ev_norm_mix[j], ev_w_in[j], pool_w[j], pool_scale[j], ev_w_out[j],
                              ev_norm_ffn[j], ffn_w_gate[j], ffn_w_up[j], ffn_w_down[j])
        else:
            h2d = _odd_layer(h, od_norm_mix[j], od_w_in[j], conv_w[j], gate_b_i[j], gate_b_f[j],
                             mlstm_norm[j], mla_q_norm[j], mla_w_uq[j], mla_kv_norm[j], mla_w_ukv[j],
                             od_w_out[j], od_norm_ffn[j], router_w[j], router_b[j],
                             moe_w_gate[j], moe_w_up[j], moe_w_down[j])
        last = layer == depth - 1
        h2d = _ple(h2d, p[layer].reshape(B * S, PLE_DIM), ple_norm[layer], ple_w_gate[layer],
                   ple_w_proj[layer], final_g=final_norm if last else None)
        h = h2d.reshape(B, S, D)
    return h
```

```python
import functools
import math

import numpy as np
import jax
import jax.numpy as jnp
from jax import lax
from jax.experimental import pallas as pl
from jax.experimental.pallas import tpu as pltpu

F32 = jnp.float32
BF16 = jnp.bfloat16
HIGHEST = lax.Precision.HIGHEST

D_MODEL = 1024
PLE_DIM = 256
NORM_EPS = 1e-6
NEG_INF = -1e30

MOBA_HEADS = 8
MOBA_HEAD_DIM = 64
MOBA_BLOCK = 256
MOBA_TOPK = 3
POOL_WINDOWS = (2, 4, 8, 16)
POOL_GROUP_DIM = 128
POOL_HALO = 16
MLSTM_HEADS = 4
MLSTM_HEAD_DIM = 128
MLSTM_CHUNK = 128
MLSTM_BATCH = 2
assert MLSTM_CHUNK == 128
CONV_WIDTH = 4
MLA_HEADS = 4
MLA_Q_RANK = 256
MLA_KV_RANK = 128
MLA_NOPE_DIM = 64
MLA_ROPE_DIM = 32
MLA_V_DIM = 128
ROPE_BASE = 10000.0
FFN_DIM = 2816
N_EXPERTS = 8
EXPERT_DIM = 3584
MIX_WIDTH = 512

ATTN_BLOCK = 512
ROW_TILE = 512
PROJ_TILE = 1024
LANES = 128
VMEM_LIMIT = 56 * 1024 * 1024

MISC_ROPE = 0
MISC_I = 32
MISC_F = 36


def _params(sem, vmem=VMEM_LIMIT):
    return pltpu.CompilerParams(dimension_semantics=sem, vmem_limit_bytes=vmem)


def _rms(x, g):
    ms = jnp.mean(x * x, axis=-1, keepdims=True)
    return x * lax.rsqrt(ms + NORM_EPS) * g


def _sigmoid(x):
    return 1.0 / (1.0 + jnp.exp(-x))


def _dot(a, b):
    return jnp.dot(a, b, preferred_element_type=F32)


def _dot_nt(a, b, precision=None):
    return lax.dot_general(a, b, (((1,), (1,)), ((), ())), precision=precision,
                           preferred_element_type=F32)


KAUG_SEL = MOBA_HEAD_DIM
KAUG_POS = KAUG_SEL + 32
MOBA_MAX_BLOCKS = KAUG_POS - KAUG_SEL


def _bf16_terms(x, n):
    out = []
    for _ in range(n):
        bits = np.float32(x).view(np.uint32)
        kept = np.uint32((int(bits) + 0x7FFF + ((int(bits) >> 16) & 1)) & 0xFFFF0000)
        term = float(kept.view(np.float32))
        out.append(term)
        x -= term
    return tuple(out)


LOG2E = math.log2(math.e)
LOG2E_TERMS = _bf16_terms(LOG2E, 3)


def _even_in_kernel(x_ref, g_ref, wn_ref, wqT_ref, wvT_ref, ext_ref,
                    ka_ref, ub_ref, km_ref, kn_ref, qT_ref, vT_ref):
    tm = x_ref.shape[1]
    xn = _rms(x_ref[0], g_ref[...]).astype(BF16)
    n = _dot(xn, wn_ref[...])
    ka = n[:, :MOBA_HEADS * LANES]
    ka_ref[0] = (ka + ext_ref[...].astype(F32)).astype(BF16)
    ub_ref[0] = n[:, MOBA_HEADS * LANES:]
    for j in range(tm // MOBA_BLOCK):
        km_ref[0, j] = jnp.mean(ka[j * MOBA_BLOCK:(j + 1) * MOBA_BLOCK], axis=0, keepdims=True)
    slot = lax.broadcasted_iota(jnp.int32, (MOBA_HEADS * LANES, LANES), 0) // LANES
    head = lax.broadcasted_iota(jnp.int32, (MOBA_HEADS * LANES, LANES), 1)
    kn_ref[0] = _dot((ka * ka).astype(BF16), (slot == head).astype(BF16))
    qT_ref[0] = _dot_nt(wqT_ref[...], xn)
    vT = _dot_nt(wvT_ref[...], xn)
    for j in range(tm // ATTN_BLOCK):
        vT_ref[0, j] = vT[:, j * ATTN_BLOCK:(j + 1) * ATTN_BLOCK].astype(BF16)


def _moba_key_extras(S, slopes):
    pos = np.arange(S)
    blk, off = pos // MOBA_BLOCK, pos % MOBA_BLOCK
    ext = np.zeros((S, MOBA_HEADS, LANES), np.float32)
    ext[pos, :, KAUG_SEL + blk] = 1.0
    for term in range(len(LOG2E_TERMS)):
        ext[:, :, KAUG_POS + 2 * term] = slopes[None, :] * (MOBA_BLOCK * blk)[:, None]
        ext[:, :, KAUG_POS + 2 * term + 1] = slopes[None, :] * off[:, None]
    return jnp.asarray(ext.reshape(S, MOBA_HEADS * LANES), dtype=BF16)


def _even_in(x, g, w_in, slopes):
    B, S, D = x.shape
    tm = PROJ_TILE
    nb = S // MOBA_BLOCK
    wq, wk, wv, wu = (w_in[:, i * MIX_WIDTH:(i + 1) * MIX_WIDTH] for i in range(4))
    wk_slots = jnp.concatenate(
        [wk.reshape(D, MOBA_HEADS, MOBA_HEAD_DIM),
         jnp.zeros((D, MOBA_HEADS, LANES - MOBA_HEAD_DIM), F32)], axis=2).reshape(D, MOBA_HEADS * LANES)
    wn = jnp.concatenate([wk_slots, wu], axis=1).astype(BF16)
    wqT = wq.T.astype(BF16)
    wvT = wv.T.astype(BF16)
    const = lambda b, i: (0, 0)
    return pl.pallas_call(
        _even_in_kernel,
        grid=(B, S // tm),
        in_specs=[
            pl.BlockSpec((1, tm, D), lambda b, i: (b, i, 0)),
            pl.BlockSpec((1, D), const),
            pl.BlockSpec((D, MOBA_HEADS * LANES + MIX_WIDTH), const),
            pl.BlockSpec((MIX_WIDTH, D), const),
            pl.BlockSpec((MIX_WIDTH, D), const),
            pl.BlockSpec((tm, MOBA_HEADS * LANES), lambda b, i: (i, 0)),
        ],
        out_specs=[
            pl.BlockSpec((1, tm, MOBA_HEADS * LANES), lambda b, i: (b, i, 0)),
            pl.BlockSpec((1, tm, MIX_WIDTH), lambda b, i: (b, i, 0)),
            pl.BlockSpec((1, tm // MOBA_BLOCK, 1, MOBA_HEADS * LANES), lambda b, i: (b, i, 0, 0)),
            pl.BlockSpec((1, tm, LANES), lambda b, i: (b, i, 0)),
            pl.BlockSpec((1, MIX_WIDTH, tm), lambda b, i: (b, 0, i)),
            pl.BlockSpec((1, tm // ATTN_BLOCK, MIX_WIDTH, ATTN_BLOCK), lambda b, i: (b, i, 0, 0)),
        ],
        out_shape=[
            jax.ShapeDtypeStruct((B, S, MOBA_HEADS * LANES), BF16),
            jax.ShapeDtypeStruct((B, S, MIX_WIDTH), F32),
            jax.ShapeDtypeStruct((B, nb, 1, MOBA_HEADS * LANES), F32),
            jax.ShapeDtypeStruct((B, S, LANES), F32),
            jax.ShapeDtypeStruct((B, MIX_WIDTH, S), F32),
            jax.ShapeDtypeStruct((B, S // ATTN_BLOCK, MIX_WIDTH, ATTN_BLOCK), BF16),
        ],
        compiler_params=_params(("parallel", "parallel")),
        name="even_in",
    )(x, g.reshape(1, D), wn, wqT, wvT, _moba_key_extras(S, slopes))


def _moba_gate_kernel(km_ref, qT_ref, qa_ref, qn_ref):
    i = pl.program_id(1)
    nb = km_ref.shape[2]
    tq = qT_ref.shape[2]
    row = lax.broadcasted_iota(jnp.int32, (nb, tq), 0)
    own = (i * tq + lax.broadcasted_iota(jnp.int32, (nb, tq), 1)) // MOBA_BLOCK
    past = row < own
    tail_row = lax.broadcasted_iota(jnp.int32, (LANES - KAUG_POS, tq), 0)
    tail = jnp.zeros(tail_row.shape, F32)
    for term, value in enumerate(LOG2E_TERMS):
        tail = jnp.where(tail_row // 2 == term, F32(value), tail)
    tail = tail.astype(BF16)
    pad = jnp.zeros((MOBA_MAX_BLOCKS - nb, tq), BF16) if nb < MOBA_MAX_BLOCKS else None
    for h in range(MOBA_HEADS):
        q_h = qT_ref[0, h * MOBA_HEAD_DIM:(h + 1) * MOBA_HEAD_DIM, :]
        qn_ref[0, h:h + 1, :] = jnp.sum(q_h * q_h, axis=0, keepdims=True)
        gate = jnp.dot(km_ref[0, h], q_h, precision=HIGHEST, preferred_element_type=F32)
        gate = jnp.where(past, gate, NEG_INF)
        chosen = jnp.zeros(gate.shape, F32)
        for _ in range(MOBA_TOPK):
            mx = jnp.max(gate, axis=0, keepdims=True)
            first = jnp.min(jnp.where(gate == mx, row, nb), axis=0, keepdims=True)
            pick = row == first
            chosen = jnp.where(pick, 1.0, chosen)
            gate = jnp.where(pick, -jnp.inf, gate)
        keep = jnp.where(past, chosen, (row == own).astype(F32))
        sel = jnp.where(keep > 0.0, 0.0, NEG_INF).astype(BF16)
        base = h * LANES
        qa_ref[0, base:base + KAUG_SEL] = (q_h * (MOBA_HEAD_DIM ** -0.5 * LOG2E)).astype(BF16)
        qa_ref[0, base + KAUG_SEL:base + KAUG_SEL + nb] = sel
        if pad is not None:
            qa_ref[0, base + KAUG_SEL + nb:base + KAUG_POS] = pad
        qa_ref[0, base + KAUG_POS:base + LANES] = tail


def _moba_gate(kmean, qT):
    B, H, nb, dh = kmean.shape
    S = qT.shape[2]
    tq = ATTN_BLOCK
    return pl.pallas_call(
        _moba_gate_kernel,
        grid=(B, S // tq),
        in_specs=[
            pl.BlockSpec((1, H, nb, dh), lambda b, i: (b, 0, 0, 0)),
            pl.BlockSpec((1, H * dh, tq), lambda b, i: (b, 0, i)),
        ],
        out_specs=[pl.BlockSpec((1, H * LANES, tq), lambda b, i: (b, 0, i)),
                   pl.BlockSpec((1, H, tq), lambda b, i: (b, 0, i))],
        out_shape=[jax.ShapeDtypeStruct((B, H * LANES, S), BF16), jax.ShapeDtypeStruct((B, H, S), F32)],
        compiler_params=_params(("parallel", "parallel")),
        name="moba_gate",
    )(kmean, qT)


HEADS_PER_STEP = 4
SUM_ROWS = 16


def _attn_kernel(lo_ref, q_ref, k_ref, v_ref, o_ref, sa_ref, sb_ref, m_ref, acc_ref):
    i = pl.program_id(2)
    lo = lo_ref[(pl.program_id(0) * pl.num_programs(1) + pl.program_id(1)) * pl.num_programs(2) + i]
    n_past = i - lo
    tq = q_ref.shape[2]
    tk = ATTN_BLOCK
    hp = HEADS_PER_STEP
    dv = v_ref.shape[2] // hp
    m_ref[...] = jnp.full(m_ref.shape, NEG_INF, F32)
    acc_ref[...] = jnp.zeros(acc_ref.shape, F32)
    ones_rows = jnp.ones((SUM_ROWS, tk), BF16)

    def scores(kvt, s_ref, diag):
        start = pl.multiple_of(kvt * tk, tk)
        k_tile = k_ref[0, pl.ds(start, tk), :]
        for g in range(hp):
            s = _dot(k_tile[:, g * LANES:(g + 1) * LANES], q_ref[0, g * LANES:(g + 1) * LANES, :])
            if diag:
                key = lax.broadcasted_iota(jnp.int32, (tk, tq), 0)
                qry = lax.broadcasted_iota(jnp.int32, (tk, tq), 1)
                s = jnp.where(key <= qry, s, NEG_INF)
            s_ref[g] = s

    def consume(kvt, s_ref):
        v_tile = v_ref[0, kvt]
        for g in range(hp):
            s = s_ref[g]
            m_run = m_ref[g]
            m_new = jnp.maximum(m_run, jnp.max(s, axis=0, keepdims=True))
            p = jnp.exp2(s - m_new).astype(BF16)
            v_aug = jnp.concatenate([v_tile[g * dv:(g + 1) * dv], ones_rows], axis=0)
            acc_ref[g] = jnp.exp2(m_run - m_new) * acc_ref[g] + _dot(v_aug, p)
            m_ref[g] = m_new

    tile_at = lambda t: jnp.where(t == 0, i, lo + t - 1)
    scores(i, sa_ref, True)

    def pair(p, carry):
        t = 2 * p
        scores(tile_at(t + 1), sb_ref, False)
        consume(tile_at(t), sa_ref)
        scores(tile_at(t + 2), sa_ref, False)
        consume(tile_at(t + 1), sb_ref)
        return carry

    lax.fori_loop(0, n_past // 2, pair, 0)
    last = 2 * (n_past // 2)

    @pl.when(n_past % 2 == 1)
    def _():
        scores(tile_at(last + 1), sb_ref, False)
        consume(tile_at(last), sa_ref)
        consume(tile_at(last + 1), sb_ref)

    @pl.when(n_past % 2 == 0)
    def _():
        consume(tile_at(last), sa_ref)

    for g in range(hp):
        o_ref[0, g * dv:(g + 1) * dv, :] = (acc_ref[g, :dv] / acc_ref[g, dv:dv + 1]).astype(o_ref.dtype)


def _attention(qT, k, vT, dv, heads, name, first_tile=None):
    B, _, S = qT.shape
    tq = ATTN_BLOCK
    nt = S // ATTN_BLOCK
    hp = HEADS_PER_STEP
    if first_tile is None:
        first_tile = jnp.zeros((B, heads // hp, nt), jnp.int32)
    return pl.pallas_call(
        _attn_kernel,
        grid_spec=pltpu.PrefetchScalarGridSpec(
            num_scalar_prefetch=1,
            grid=(B, heads // hp, S // tq),
            in_specs=[
                pl.BlockSpec((1, hp * LANES, tq), lambda b, h, i, lo: (b, h, i)),
                pl.BlockSpec((1, S, hp * LANES), lambda b, h, i, lo: (b, 0, h)),
                pl.BlockSpec((1, nt, hp * dv, ATTN_BLOCK), lambda b, h, i, lo: (b, 0, h, 0)),
            ],
            out_specs=pl.BlockSpec((1, hp * dv, tq), lambda b, h, i, lo: (b, h, i)),
            scratch_shapes=[pltpu.VMEM((hp, ATTN_BLOCK, tq), F32), pltpu.VMEM((hp, ATTN_BLOCK, tq), F32),
                            pltpu.VMEM((hp, 1, tq), F32), pltpu.VMEM((hp, dv + SUM_ROWS, tq), F32)],
        ),
        out_shape=jax.ShapeDtypeStruct((B, heads * dv, S), BF16),
        compiler_params=_params(("parallel", "parallel", "arbitrary")),
        name=name,
    )(first_tile.reshape(-1), qT, k, vT)


def _pool_kernel(x_ref, halo_ref, w_ref, sc_ref, o_ref, xs_ref):
    i = pl.program_id(1)
    tm = x_ref.shape[1]
    x = x_ref[0]
    xs_ref[0:POOL_HALO] = jnp.where(i > 0, halo_ref[0], 0.0)
    xs_ref[POOL_HALO:POOL_HALO + tm] = x
    t = i * tm + lax.broadcasted_iota(jnp.int32, (tm, 1), 0)
    outs = []
    for g, win in enumerate(POOL_WINDOWS):
        lo, hi = g * POOL_GROUP_DIM, (g + 1) * POOL_GROUP_DIM
        xg = x[:, lo:hi]
        acc = xg
        for d in range(1, win):
            acc = acc + xs_ref[POOL_HALO - d:POOL_HALO - d + tm, lo:hi]
        count = jnp.minimum(t + 1, win).astype(F32)
        outs.append(_dot((acc / count - xg).astype(BF16), w_ref[g]))
    o_ref[0] = (jnp.concatenate(outs, axis=1) * sc_ref[...]).astype(o_ref.dtype)


def _pool(ub, pool_w, pool_scale):
    B, S, W = ub.shape
    tm = PROJ_TILE
    per = tm // POOL_HALO
    return pl.pallas_call(
        _pool_kernel,
        grid=(B, S // tm),
        in_specs=[
            pl.BlockSpec((1, tm, W), lambda b, i: (b, i, 0)),
            pl.BlockSpec((1, POOL_HALO, W), lambda b, i: (b, jnp.maximum(i * per - 1, 0), 0)),
            pl.BlockSpec(pool_w.shape, lambda b, i: (0, 0, 0)),
            pl.BlockSpec((1, W), lambda b, i: (0, 0)),
        ],
        out_specs=pl.BlockSpec((1, tm, W), lambda b, i: (b, i, 0)),
        out_shape=jax.ShapeDtypeStruct((B, S, W), BF16),
        scratch_shapes=[pltpu.VMEM((POOL_HALO + tm, W), F32)],
        compiler_params=_params(("parallel", "parallel")),
        name="pool",
    )(ub, ub, pool_w.astype(BF16), pool_scale.reshape(1, W))


def _mix_out_kernel(h_ref, aT_ref, b_ref, w_ref, o_ref, *, a_first):
    a = aT_ref[0].astype(F32).T.astype(BF16)
    b = b_ref[0].astype(BF16)
    lo, hi = (a, b) if a_first else (b, a)
    y = _dot(lo, w_ref[:MIX_WIDTH]) + _dot(hi, w_ref[MIX_WIDTH:])
    o_ref[0] = h_ref[0] + y


def _mix_out(h, aT, b, w_out, a_first):
    B, S, D = h.shape
    tm = PROJ_TILE
    return pl.pallas_call(
        functools.partial(_mix_out_kernel, a_first=a_first),
        grid=(B, S // tm),
        in_specs=[
            pl.BlockSpec((1, tm, D), lambda b_, i: (b_, i, 0)),
            pl.BlockSpec((1, MIX_WIDTH, tm), lambda b_, i: (b_, 0, i)),
            pl.BlockSpec((1, tm, MIX_WIDTH), lambda b_, i: (b_, i, 0)),
            pl.BlockSpec((2 * MIX_WIDTH, D), lambda b_, i: (0, 0)),
        ],
        out_specs=pl.BlockSpec((1, tm, D), lambda b_, i: (b_, i, 0)),
        out_shape=jax.ShapeDtypeStruct((B, S, D), F32),
        compiler_params=_params(("parallel", "parallel")),
        name="mix_out",
    )(h, aT, b, w_out.astype(BF16))


def _swiglu_step(xn, wg, wu, wd):
    gt = _dot(xn, wg)
    up = _dot(xn, wu)
    return _dot((gt * _sigmoid(gt) * up).astype(BF16), wd)


def _ffn_kernel(h_ref, g_ref, wg_ref, wu_ref, wd_ref, o_ref):
    h = h_ref[...]
    xn = _rms(h, g_ref[...]).astype(BF16)
    o_ref[...] = h + _swiglu_step(xn, wg_ref[...], wu_ref[...], wd_ref[...])


def _ffn(h2d, g, wg, wu, wd, tm=ROW_TILE):
    T, D = h2d.shape
    F = wg.shape[1]
    resident = lambda shape: pl.BlockSpec(shape, lambda i: (0, 0), pipeline_mode=pl.Buffered(1))
    return pl.pallas_call(
        _ffn_kernel,
        grid=(T // tm,),
        in_specs=[
            pl.BlockSpec((tm, D), lambda i: (i, 0)),
            pl.BlockSpec((1, D), lambda i: (0, 0)),
            resident((D, F)),
            resident((D, F)),
            resident((F, D)),
        ],
        out_specs=pl.BlockSpec((tm, D), lambda i: (i, 0)),
        out_shape=jax.ShapeDtypeStruct((T, D), F32),
        compiler_params=_params(("parallel",)),
        name="ffn",
    )(h2d, g.reshape(1, D), wg, wu, wd)


ROUTE_E0, ROUTE_E1, ROUTE_W0, ROUTE_W1, ROUTE_R0, ROUTE_R1 = range(6)
MOE_TILE = 512
MOE_TF = EXPERT_DIM // 2


def _lane_pick(tile, lane, idx):
    return jnp.sum(jnp.where(lane == idx, tile, 0.0), axis=1, keepdims=True)


ROUTER_ROWS = 16


def _router_kernel(h_ref, g_ref, wT_ref, b_ref, route_ref, cnt_ref):
    tm = h_ref.shape[0]

    @pl.when(pl.program_id(0) == 0)
    def _():
        cnt_ref[...] = jnp.zeros_like(cnt_ref)

    xn = _rms(h_ref[...], g_ref[...])
    logits = _dot_nt(wT_ref[...], xn, precision=HIGHEST) + b_ref[...]
    row = lax.broadcasted_iota(jnp.int32, logits.shape, 0)
    logits = jnp.where(row < N_EXPERTS, logits, -jnp.inf)
    v0 = jnp.max(logits, axis=0, keepdims=True)
    i0 = jnp.min(jnp.where(logits == v0, row, ROUTER_ROWS), axis=0, keepdims=True)
    rest = jnp.where(row == i0, -jnp.inf, logits)
    v1 = jnp.max(rest, axis=0, keepdims=True)
    i1 = jnp.min(jnp.where(rest == v1, row, ROUTER_ROWS), axis=0, keepdims=True)
    e1 = jnp.exp(v1 - v0)
    w0 = 1.0 / (1.0 + e1)
    sel = (row == i0).astype(F32) + (row == i1).astype(F32)
    earlier = (lax.broadcasted_iota(jnp.int32, (tm, tm), 0)
               < lax.broadcasted_iota(jnp.int32, (tm, tm), 1))
    counts = cnt_ref[:, 0:1]
    rank = _dot(sel.astype(BF16), earlier.astype(BF16)) + counts
    cnt_ref[...] = jnp.broadcast_to(counts + jnp.sum(sel, axis=1, keepdims=True), cnt_ref.shape)
    pick = lambda idx: jnp.sum(jnp.where(row == idx, rank, 0.0), axis=0, keepdims=True)
    rows = [None] * 6
    rows[ROUTE_E0], rows[ROUTE_E1] = i0.astype(F32), i1.astype(F32)
    rows[ROUTE_W0], rows[ROUTE_W1] = w0, e1 * w0
    rows[ROUTE_R0], rows[ROUTE_R1] = pick(i0), pick(i1)
    routeT = jnp.concatenate(rows + [jnp.zeros((LANES - len(rows), tm), F32)], axis=0)
    route_ref[...] = routeT.T


def _router(h2d, g, router_w, router_b):
    T, D = h2d.shape
    tm = ROW_TILE
    wT = jnp.zeros((ROUTER_ROWS, D), F32).at[:N_EXPERTS].set(router_w.T)
    b = jnp.zeros((ROUTER_ROWS, 1), F32).at[:N_EXPERTS, 0].set(router_b)
    return pl.pallas_call(
        _router_kernel,
        grid=(T // tm,),
        in_specs=[
            pl.BlockSpec((tm, D), lambda i: (i, 0)),
            pl.BlockSpec((1, D), lambda i: (0, 0)),
            pl.BlockSpec((ROUTER_ROWS, D), lambda i: (0, 0)),
            pl.BlockSpec((ROUTER_ROWS, 1), lambda i: (0, 0)),
        ],
        out_specs=[pl.BlockSpec((tm, LANES), lambda i: (i, 0)),
                   pl.BlockSpec((ROUTER_ROWS, LANES), lambda i: (0, 0))],
        out_shape=[jax.ShapeDtypeStruct((T, LANES), F32), jax.ShapeDtypeStruct((ROUTER_ROWS, LANES), F32)],
        compiler_params=_params(("arbitrary",)),
        name="router",
    )(h2d, g.reshape(1, D), wT, b)


SUBLANES = 8
ROW_LANES = D_MODEL // SUBLANES
assert ROW_LANES == LANES


def _row_tile(ref, r):
    return ref.at[pl.ds(pl.multiple_of(r * SUBLANES, SUBLANES), SUBLANES), :]


def _to_row_tiles(ref, x):
    tm = x.shape[0]
    for s in range(SUBLANES):
        ref[pl.ds(s, tm, stride=SUBLANES), :] = x[:, s * LANES:(s + 1) * LANES]


def _from_row_tiles(ref):
    tm = ref.shape[0] // SUBLANES
    return jnp.concatenate([ref[pl.ds(s, tm, stride=SUBLANES), :] for s in range(SUBLANES)], axis=1)


def _row_copies(pos_ref, base, r, src_of, dst_of, sem):
    return [pltpu.make_async_copy(src_of(k, pos_ref[base + 2 * r + k]),
                                  dst_of(k, pos_ref[base + 2 * r + k]), sem) for k in range(2)]


def _start_rows(tm, make):
    def issue(r, c):
        for k, cp in enumerate(make(r)):
            cp.start(priority=k)
        return c

    lax.fori_loop(0, tm, issue, 0, unroll=8)


def _wait_rows(tm, make):
    def drain(r, c):
        for cp in make(r):
            cp.wait()
        return c

    lax.fori_loop(0, tm, drain, 0, unroll=8)


def _dispatch_kernel(pos_ref, pad_ref, h_ref, g_ref, xs_ref, xn_ref, zero_ref, sem, zsem):
    tm = h_ref.shape[0]
    i = pl.program_id(0)
    slot = i % 2

    @pl.when(i == 0)
    def _():
        zero_ref[...] = jnp.zeros_like(zero_ref)
        for e in range(N_EXPERTS):
            first, count = pad_ref[e], pad_ref[N_EXPERTS + e]
            fill = lambda r, first=first: pltpu.make_async_copy(zero_ref, _row_tile(xs_ref, first + r), zsem)

            def start(r, c, fill=fill):
                fill(r).start()
                return c

            def wait(r, c, fill=fill):
                fill(r).wait()
                return c

            lax.fori_loop(0, count, start, 0)
            lax.fori_loop(0, count, wait, 0)

    def scatters(tile, buf):
        return lambda r: _row_copies(
            pos_ref, tile * (2 * tm), r, lambda k, p: _row_tile(xn_ref.at[buf], r),
            lambda k, p: _row_tile(xs_ref, p), sem.at[buf])

    _to_row_tiles(xn_ref.at[slot], _rms(h_ref[...], g_ref[...]))
    _start_rows(tm, scatters(i, slot))

    @pl.when(i > 0)
    def _():
        _wait_rows(tm, scatters(i - 1, 1 - slot))

    @pl.when(i == pl.num_programs(0) - 1)
    def _():
        _wait_rows(tm, scatters(i, slot))


def _dispatch(pos, pad, h2d, g, n_rows):
    T, D = h2d.shape
    tm = ROW_TILE
    return pl.pallas_call(
        _dispatch_kernel,
        grid_spec=pltpu.PrefetchScalarGridSpec(
            num_scalar_prefetch=2,
            grid=(T // tm,),
            in_specs=[pl.BlockSpec((tm, D), lambda i, pos, pad: (i, 0)),
                      pl.BlockSpec((1, D), lambda i, pos, pad: (0, 0))],
            out_specs=pl.BlockSpec(memory_space=pl.ANY),
            scratch_shapes=[pltpu.VMEM((2, tm * SUBLANES, ROW_LANES), F32),
                            pltpu.VMEM((SUBLANES, ROW_LANES), F32),
                            pltpu.SemaphoreType.DMA((2,)), pltpu.SemaphoreType.DMA(())],
        ),
        out_shape=jax.ShapeDtypeStruct((n_rows * SUBLANES, ROW_LANES), F32),
        compiler_params=_params(("arbitrary",)),
        name="moe_dispatch",
    )(pos, pad, h2d, g.reshape(1, D))


def _moe_ffn_kernel(te_ref, nv_ref, x_ref, wg_ref, wu_ref, wd_ref, o_ref, xb_ref, acc_ref):
    j = pl.program_id(0)
    f = pl.program_id(1)
    valid = j < nv_ref[0]

    @pl.when(f == 0)
    def _():
        acc_ref[...] = jnp.zeros_like(acc_ref)

    @pl.when(valid & (f == 0))
    def _():
        xb_ref[...] = _from_row_tiles(x_ref).astype(BF16)

    @pl.when(valid)
    def _():
        acc_ref[...] += _swiglu_step(xb_ref[...], wg_ref[0], wu_ref[0], wd_ref[0])

    @pl.when(f == pl.num_programs(1) - 1)
    def _():
        _to_row_tiles(o_ref, acc_ref[...])


def _moe_ffn(tile_expert, n_valid, xs, wg, wu, wd):
    N, D = xs.shape[0] // SUBLANES, D_MODEL
    tm, tf = MOE_TILE, MOE_TF
    F = wg.shape[2]
    return pl.pallas_call(
        _moe_ffn_kernel,
        grid_spec=pltpu.PrefetchScalarGridSpec(
            num_scalar_prefetch=2,
            grid=(N // tm, F // tf),
            in_specs=[pl.BlockSpec((tm * SUBLANES, ROW_LANES), lambda j, f, te, nv: (j, 0)),
                      pl.BlockSpec((1, D, tf), lambda j, f, te, nv: (te[j], 0, f)),
                      pl.BlockSpec((1, D, tf), lambda j, f, te, nv: (te[j], 0, f)),
                      pl.BlockSpec((1, tf, D), lambda j, f, te, nv: (te[j], f, 0))],
            out_specs=pl.BlockSpec((tm * SUBLANES, ROW_LANES), lambda j, f, te, nv: (j, 0)),
            scratch_shapes=[pltpu.VMEM((tm, D), BF16), pltpu.VMEM((tm, D), F32)],
        ),
        out_shape=jax.ShapeDtypeStruct(xs.shape, F32),
        compiler_params=_params(("arbitrary", "arbitrary")),
        name="moe_ffn",
    )(tile_expert, n_valid, xs, wg, wu, wd)


def _combine_kernel(pos_ref, h_ref, route_ref, ys_ref, o_ref, y_ref, sem):
    tm = h_ref.shape[0]
    i = pl.program_id(0)
    slot = i % 2

    def gathers(tile, buf):
        return lambda r: _row_copies(
            pos_ref, tile * (2 * tm), r, lambda k, p: _row_tile(ys_ref, p),
            lambda k, p: _row_tile(y_ref.at[buf, k], r), sem.at[buf])

    @pl.when(i == 0)
    def _():
        _start_rows(tm, gathers(0, 0))

    @pl.when(i + 1 < pl.num_programs(0))
    def _():
        _start_rows(tm, gathers(i + 1, 1 - slot))

    _wait_rows(tm, gathers(i, slot))
    route = route_ref[...]
    lane = lax.broadcasted_iota(jnp.int32, route.shape, 1)
    o_ref[...] = (h_ref[...] + _lane_pick(route, lane, ROUTE_W0) * _from_row_tiles(y_ref.at[slot, 0])
                  + _lane_pick(route, lane, ROUTE_W1) * _from_row_tiles(y_ref.at[slot, 1]))


def _combine(pos, h2d, route, ys):
    T, D = h2d.shape
    tm = ROW_TILE
    return pl.pallas_call(
        _combine_kernel,
        grid_spec=pltpu.PrefetchScalarGridSpec(
            num_scalar_prefetch=1,
            grid=(T // tm,),
            in_specs=[pl.BlockSpec((tm, D), lambda i, pos: (i, 0)),
                      pl.BlockSpec((tm, LANES), lambda i, pos: (i, 0)),
                      pl.BlockSpec(memory_space=pl.ANY)],
            out_specs=pl.BlockSpec((tm, D), lambda i, pos: (i, 0)),
            scratch_shapes=[pltpu.VMEM((2, 2, tm * SUBLANES, ROW_LANES), F32),
                            pltpu.SemaphoreType.DMA((2,))],
        ),
        out_shape=jax.ShapeDtypeStruct((T, D), F32),
        compiler_params=_params(("arbitrary",)),
        name="moe_combine",
    )(pos, h2d, route, ys)


def _moe(h2d, g, router_w, router_b, wg, wu, wd):
    T, D = h2d.shape
    tm = MOE_TILE
    route, counts = _router(h2d, g, router_w, router_b)
    cnt = counts[:N_EXPERTS, 0].astype(jnp.int32)
    padded = (cnt + tm - 1) // tm * tm
    ends = jnp.cumsum(padded)
    start = ends - padded
    e01 = route[:, ROUTE_E0:ROUTE_E1 + 1].astype(jnp.int32)
    r01 = route[:, ROUTE_R0:ROUTE_R1 + 1].astype(jnp.int32)
    pos = (start[e01] + r01).reshape(2 * T)
    n_rows = 2 * T + N_EXPERTS * tm
    tile_row = jnp.arange(n_rows // tm, dtype=jnp.int32) * tm
    tile_expert = jnp.minimum(jnp.sum(tile_row[:, None] >= ends[None, :], axis=1), N_EXPERTS - 1).astype(jnp.int32)
    n_valid = (ends[-1:] // tm).astype(jnp.int32)
    pad = jnp.concatenate([start + cnt, padded - cnt]).astype(jnp.int32)
    xs = _dispatch(pos, pad, h2d, g, n_rows)
    ys = _moe_ffn(tile_expert, n_valid, xs, wg, wu, wd)
    return _combine(pos, h2d, route, ys)


def _ple_kernel(*refs, final):
    if final:
        h_ref, p_ref, g_ref, wg_ref, wp_ref, fg_ref, o_ref = refs
    else:
        h_ref, p_ref, g_ref, wg_ref, wp_ref, o_ref = refs
    h = h_ref[...]
    gate = _sigmoid(_dot(_rms(h, g_ref[...]).astype(BF16), wg_ref[...]))
    out = h + gate * _dot(p_ref[...].astype(BF16), wp_ref[...])
    if final:
        out = _rms(out, fg_ref[...])
    o_ref[...] = out


def _ple(h2d, p2d, g, w_gate, w_proj, final_g=None):
    T, D = h2d.shape
    tm = PROJ_TILE
    final = final_g is not None
    in_specs = [
        pl.BlockSpec((tm, D), lambda i: (i, 0)),
        pl.BlockSpec((tm, PLE_DIM), lambda i: (i, 0)),
        pl.BlockSpec((1, D), lambda i: (0, 0)),
        pl.BlockSpec((D, D), lambda i: (0, 0)),
        pl.BlockSpec((PLE_DIM, D), lambda i: (0, 0)),
    ]
    args = [h2d, p2d, g.reshape(1, D), w_gate.astype(BF16), w_proj.astype(BF16)]
    if final:
        in_specs.append(pl.BlockSpec((1, D), lambda i: (0, 0)))
        args.append(final_g.reshape(1, D))
    return pl.pallas_call(
        functools.partial(_ple_kernel, final=final),
        grid=(T // tm,),
        in_specs=in_specs,
        out_specs=pl.BlockSpec((tm, D), lambda i: (i, 0)),
        out_shape=jax.ShapeDtypeStruct((T, D), F32),
        compiler_params=_params(("parallel",)),
        name="ple_final" if final else "ple",
    )(*args)


ODD_MAIN = 4 * MIX_WIDTH
ODD_COLS = ODD_MAIN + MLA_Q_RANK + MLA_KV_RANK + 2 * LANES
MLA_QK_SCALE = (MLA_NOPE_DIM + MLA_ROPE_DIM) ** -0.5 * LOG2E
ROPE_HALF = MLA_ROPE_DIM // 2


def _odd_in_kernel(x_ref, g_ref, wn_ref, qn_ref, wuqT_ref, kvn_ref, wk2_ref, wvT_ref,
                   cosT_ref, sinT_ref, cc_ref, ss_ref,
                   qk_ref, vc_ref, op_ref, misc_ref, mq_ref, mk_ref, mv_ref):
    tm = x_ref.shape[1]
    xn = _rms(x_ref[0], g_ref[...]).astype(BF16)
    u = _dot(xn, wn_ref[...])
    qk_ref[0] = u[:, :2 * MIX_WIDTH]
    vc_ref[0] = u[:, 2 * MIX_WIDTH:3 * MIX_WIDTH].astype(BF16)
    op_ref[0] = u[:, 3 * MIX_WIDTH:ODD_MAIN]
    c0 = ODD_MAIN
    c_q = u[:, c0:c0 + MLA_Q_RANK]
    c0 += MLA_Q_RANK
    c_kv = u[:, c0:c0 + MLA_KV_RANK]
    c0 += MLA_KV_RANK
    misc = u[:, c0:c0 + LANES]
    misc_sw = u[:, c0 + LANES:c0 + 2 * LANES]
    misc_ref[0] = misc
    cqn = _rms(c_q, qn_ref[...]).astype(BF16)
    qT = _dot_nt(wuqT_ref[...], cqn)
    cosT = cosT_ref[...]
    sinT = sinT_ref[...]
    for h in range(MLA_HEADS):
        r = h * LANES
        mq_ref[0, r:r + MLA_NOPE_DIM] = (qT[r:r + MLA_NOPE_DIM] * MLA_QK_SCALE).astype(BF16)
        x1 = qT[r + MLA_NOPE_DIM:r + MLA_NOPE_DIM + ROPE_HALF]
        x2 = qT[r + MLA_NOPE_DIM + ROPE_HALF:r + MLA_NOPE_DIM + MLA_ROPE_DIM]
        mq_ref[0, r + MLA_NOPE_DIM:r + MLA_NOPE_DIM + ROPE_HALF] = (
            (x1 * cosT - x2 * sinT) * MLA_QK_SCALE).astype(BF16)
        mq_ref[0, r + MLA_NOPE_DIM + ROPE_HALF:r + MLA_NOPE_DIM + MLA_ROPE_DIM] = (
            (x1 * sinT + x2 * cosT) * MLA_QK_SCALE).astype(BF16)
        mq_ref[0, r + MLA_NOPE_DIM + MLA_ROPE_DIM:r + LANES] = jnp.zeros(
            (LANES - MLA_NOPE_DIM - MLA_ROPE_DIM, tm), BF16)
    ckvn = _rms(c_kv, kvn_ref[...]).astype(BF16)
    k_rot = (misc * cc_ref[...] + misc_sw * ss_ref[...]).astype(BF16)
    mk_ref[0] = _dot(jnp.concatenate([ckvn, k_rot], axis=1), wk2_ref[...]).astype(BF16)
    vT = _dot_nt(wvT_ref[...], ckvn)
    for j in range(tm // ATTN_BLOCK):
        mv_ref[0, j] = vT[:, j * ATTN_BLOCK:(j + 1) * ATTN_BLOCK].astype(BF16)


def _rope_tables(S):
    inv_freq = ROPE_BASE ** (-jnp.arange(ROPE_HALF, dtype=F32) / ROPE_HALF)
    ang = jnp.arange(S, dtype=F32)[:, None] * inv_freq[None, :]
    cos, sin = jnp.cos(ang), jnp.sin(ang)
    pad = jnp.zeros((S, LANES - MLA_ROPE_DIM), F32)
    cc = jnp.concatenate([cos, cos, pad], axis=1)
    ss = jnp.concatenate([-sin, sin, pad], axis=1)
    return cos.T, sin.T, cc, ss


def _odd_in(x, g, w_in, q_norm, w_uq, kv_norm, w_ukv):
    B, S, D = x.shape
    tm = PROJ_TILE
    cuts = np.cumsum([MIX_WIDTH] * 4 + [MLSTM_HEADS, MLSTM_HEADS, MLA_Q_RANK, MLA_KV_RANK]).tolist()
    w_main = w_in[:, :cuts[3]]
    w_i = w_in[:, cuts[3]:cuts[4]]
    w_f = w_in[:, cuts[4]:cuts[5]]
    w_cq = w_in[:, cuts[5]:cuts[6]]
    w_ckv = w_in[:, cuts[6]:cuts[7]]
    w_kr = w_in[:, cuts[7]:]
    w_kr_sw = jnp.concatenate([w_kr[:, ROPE_HALF:], w_kr[:, :ROPE_HALF]], axis=1)
    zpad = lambda n: jnp.zeros((D, n), F32)
    w_misc = jnp.concatenate([w_kr, w_i, w_f, zpad(LANES - MLA_ROPE_DIM - 2 * MLSTM_HEADS)], axis=1)
    w_misc_sw = jnp.concatenate([w_kr_sw, zpad(LANES - MLA_ROPE_DIM)], axis=1)
    wn = jnp.concatenate([w_main, w_cq, w_ckv, w_misc, w_misc_sw], axis=1).astype(BF16)
    qd = MLA_NOPE_DIM + MLA_ROPE_DIM
    w_uq_h = w_uq.reshape(MLA_Q_RANK, MLA_HEADS, qd)
    w_uq_h = jnp.concatenate([w_uq_h, jnp.zeros((MLA_Q_RANK, MLA_HEADS, LANES - qd), F32)], axis=2)
    wuqT = w_uq_h.reshape(MLA_Q_RANK, MLA_HEADS * LANES).T.astype(BF16)
    w_ukv_h = w_ukv.reshape(MLA_KV_RANK, MLA_HEADS, MLA_NOPE_DIM + MLA_V_DIM)
    w_k = jnp.concatenate([w_ukv_h[:, :, :MLA_NOPE_DIM],
                           jnp.zeros((MLA_KV_RANK, MLA_HEADS, LANES - MLA_NOPE_DIM), F32)], axis=2)
    place = jnp.zeros((LANES, MLA_HEADS, LANES), F32)
    eye = jnp.eye(MLA_ROPE_DIM, dtype=F32)
    place = place.at[:MLA_ROPE_DIM, :, MLA_NOPE_DIM:MLA_NOPE_DIM + MLA_ROPE_DIM].set(
        jnp.broadcast_to(eye[:, None, :], (MLA_ROPE_DIM, MLA_HEADS, MLA_ROPE_DIM)))
    wk2 = jnp.concatenate([w_k, place], axis=0).reshape(MLA_KV_RANK + LANES, MLA_HEADS * LANES).astype(BF16)
    wvT = w_ukv_h[:, :, MLA_NOPE_DIM:].reshape(MLA_KV_RANK, MLA_HEADS * MLA_V_DIM).T.astype(BF16)
    cosT, sinT, cc, ss = _rope_tables(S)
    row = lambda b, i: (b, i, 0)
    const = lambda b, i: (0, 0)
    nb = S // ATTN_BLOCK
    return pl.pallas_call(
        _odd_in_kernel,
        grid=(B, S // tm),
        in_specs=[
            pl.BlockSpec((1, tm, D), row),
            pl.BlockSpec((1, D), const),
            pl.BlockSpec((D, ODD_COLS), const),
            pl.BlockSpec((1, MLA_Q_RANK), const),
            pl.BlockSpec((MLA_HEADS * LANES, MLA_Q_RANK), const),
            pl.BlockSpec((1, MLA_KV_RANK), const),
            pl.BlockSpec((MLA_KV_RANK + LANES, MLA_HEADS * LANES), const),
            pl.BlockSpec((MLA_HEADS * MLA_V_DIM, MLA_KV_RANK), const),
            pl.BlockSpec((ROPE_HALF, tm), lambda b, i: (0, i)),
            pl.BlockSpec((ROPE_HALF, tm), lambda b, i: (0, i)),
            pl.BlockSpec((tm, LANES), lambda b, i: (i, 0)),
            pl.BlockSpec((tm, LANES), lambda b, i: (i, 0)),
        ],
        out_specs=[
            pl.BlockSpec((1, tm, 2 * MIX_WIDTH), row),
            pl.BlockSpec((1, tm, MIX_WIDTH), row),
            pl.BlockSpec((1, tm, MIX_WIDTH), row),
            pl.BlockSpec((1, tm, LANES), row),
            pl.BlockSpec((1, MLA_HEADS * LANES, tm), lambda b, i: (b, 0, i)),
            pl.BlockSpec((1, tm, MLA_HEADS * LANES), row),
            pl.BlockSpec((1, tm // ATTN_BLOCK, MLA_HEADS * MLA_V_DIM, ATTN_BLOCK), lambda b, i: (b, i, 0, 0)),
        ],
        out_shape=[
            jax.ShapeDtypeStruct((B, S, 2 * MIX_WIDTH), F32),
            jax.ShapeDtypeStruct((B, S, MIX_WIDTH), BF16),
            jax.ShapeDtypeStruct((B, S, MIX_WIDTH), F32),
            jax.ShapeDtypeStruct((B, S, LANES), F32),
            jax.ShapeDtypeStruct((B, MLA_HEADS * LANES, S), BF16),
            jax.ShapeDtypeStruct((B, S, MLA_HEADS * LANES), BF16),
            jax.ShapeDtypeStruct((B, nb, MLA_HEADS * MLA_V_DIM, ATTN_BLOCK), BF16),
        ],
        compiler_params=_params(("parallel", "parallel")),
        name="odd_in",
    )(x, g.reshape(1, D), wn, q_norm.reshape(1, -1), wuqT, kv_norm.reshape(1, -1), wk2, wvT,
      cosT, sinT, cc, ss)


def _log_sigmoid(x):
    return jnp.minimum(x, 0.0) - jnp.log(1.0 + jnp.exp(-jnp.abs(x)))


def _mlstm_kernel(qk_ref, v_ref, op_ref, misc_ref, cw_ref, gb_ref, hn_ref, o_ref,
                  prev_ref, cn_ref, m_ref):
    c = pl.program_id(1)
    L = qk_ref.shape[1]
    row = lax.broadcasted_iota(jnp.int32, (L, 1), 0)
    lane = lax.broadcasted_iota(jnp.int32, (L, LANES), 1)
    is_f = (lane >= MISC_F) & (lane < MISC_F + MLSTM_HEADS)
    causal = lax.broadcasted_iota(jnp.int32, (L, L), 1) <= lax.broadcasted_iota(jnp.int32, (L, L), 0)

    @pl.when(c == 0)
    def _():
        prev_ref[...] = jnp.zeros_like(prev_ref)
        cn_ref[...] = jnp.zeros_like(cn_ref)
        m_ref[...] = jnp.zeros_like(m_ref)

    seqs = range(qk_ref.shape[0])
    heads = [(bb, h) for bb in seqs for h in range(MLSTM_HEADS)]
    d = MLSTM_HEAD_DIM

    qk = []
    for bb in seqs:
        x = qk_ref[bb]
        prev = prev_ref[bb]
        conv = x * cw_ref[CONV_WIDTH - 1:CONV_WIDTH, :]
        for j in range(1, CONV_WIDTH):
            shifted = pltpu.roll(jnp.where(row >= L - j, prev, x), j, axis=0)
            conv = conv + shifted * cw_ref[CONV_WIDTH - 1 - j:CONV_WIDTH - j, :]
        prev_ref[bb] = x
        qk.append(conv * _sigmoid(conv))

    sel_r = lax.broadcasted_iota(jnp.int32, (LANES, 2 * MIX_WIDTH), 0)
    sel_c = lax.broadcasted_iota(jnp.int32, (LANES, 2 * MIX_WIDTH), 1)
    spread = (sel_r == MISC_I + sel_c // LANES).astype(F32)
    pick = (lax.broadcasted_iota(jnp.int32, (8, LANES), 1)
            == MISC_I + lax.broadcasted_iota(jnp.int32, (8, LANES), 0)).astype(F32)
    mean_mat = jnp.full((d, d), 1.0 / d, F32)
    cols, rows = [], []
    for bb in seqs:
        gates = misc_ref[bb] + gb_ref[...]
        z = jnp.where(is_f, _log_sigmoid(gates), gates)
        cum = jnp.dot(causal.astype(F32), z, precision=HIGHEST, preferred_element_type=F32)
        z = jnp.where(is_f, cum, z)
        cols.append(jnp.dot(z, spread, precision=HIGHEST, preferred_element_type=F32))
        rows.append(_dot_nt(pick, z, precision=HIGHEST))

    q, k, v_aug, i_b, b_b, m_prev, m_t, w_inter, scores, inter = ({} for _ in range(10))
    ones_blk = jnp.ones((L, LANES), BF16)
    for key in heads:
        bb, h = key
        lo, hi = h * d, (h + 1) * d
        q[key] = qk[bb][:, lo:hi].astype(BF16)
        k[key] = qk[bb][:, MIX_WIDTH + lo:MIX_WIDTH + hi] * (d ** -0.5)
        v_aug[key] = jnp.concatenate([v_ref[bb, :, lo:hi], ones_blk], axis=1)
        scores[key] = _dot_nt(q[key], k[key].astype(BF16))
        inter[key] = _dot(q[key], cn_ref[bb, h].astype(BF16))
    intra = {}
    for key in heads:
        bb, h = key
        i_b[key] = cols[bb][:, h * LANES:(h + 1) * LANES]
        b_b[key] = cols[bb][:, (MLSTM_HEADS + h) * LANES:(MLSTM_HEADS + h + 1) * LANES]
        i_row = rows[bb][h:h + 1, :]
        b_row = rows[bb][MLSTM_HEADS + h:MLSTM_HEADS + h + 1, :]
        m_prev[key] = m_ref[bb, h:h + 1, :]
        intra[key] = jnp.where(causal, b_b[key] - b_row + i_row, NEG_INF)
    for key in heads:
        m_inter = b_b[key] + m_prev[key]
        m_t[key] = jnp.maximum(m_inter, jnp.max(intra[key], axis=1, keepdims=True))
        w_inter[key] = jnp.exp(m_inter - m_t[key])
    intra_o = {}
    for key in heads:
        a = jnp.exp(intra[key] - m_t[key]) * scores[key]
        intra_o[key] = _dot(a.astype(BF16), v_aug[key])
    for key in heads:
        bb, h = key
        lo, hi = h * d, (h + 1) * d
        num = w_inter[key] * inter[key][:, :d] + intra_o[key][:, :d]
        den = w_inter[key] * inter[key][:, d:] + intra_o[key][:, d:]
        hh = num / jnp.maximum(jnp.abs(den), jnp.exp(-m_t[key]))
        ms = jnp.dot(hh * hh, mean_mat, precision=HIGHEST, preferred_element_type=F32)
        hh = hh * lax.rsqrt(ms + NORM_EPS) * hn_ref[:, lo:hi]
        o_ref[bb, :, lo:hi] = (hh * _sigmoid(op_ref[bb, :, lo:hi])).astype(o_ref.dtype)
    for key in heads:
        bb, h = key
        b_end = b_b[key][L - 1:L, :]
        g = b_end - b_b[key] + i_b[key]
        m_new = jnp.maximum(b_end + m_prev[key], jnp.max(g, axis=0, keepdims=True))
        decay = jnp.exp(b_end + m_prev[key] - m_new)
        kw = k[key] * jnp.exp(g - m_new)
        cn_ref[bb, h] = (jnp.concatenate([decay, decay], axis=1) * cn_ref[bb, h]
                         + _dot(kw.T.astype(BF16), v_aug[key]))
        m_ref[bb, h:h + 1, :] = m_new


def _mlstm(qk_raw, vc, o_pre, misc, conv_w, b_i, b_f, head_norm):
    B, S, _ = qk_raw.shape
    L = MLSTM_CHUNK
    nb = MLSTM_BATCH
    gb = jnp.zeros((1, LANES), F32).at[0, MISC_I:MISC_I + MLSTM_HEADS].set(b_i)
    gb = gb.at[0, MISC_F:MISC_F + MLSTM_HEADS].set(b_f)
    row = lambda b, c: (b, c, 0)
    const = lambda b, c: (0, 0)
    return pl.pallas_call(
        _mlstm_kernel,
        grid=(B // nb, S // L),
        in_specs=[
            pl.BlockSpec((nb, L, 2 * MIX_WIDTH), row),
            pl.BlockSpec((nb, L, MIX_WIDTH), row),
            pl.BlockSpec((nb, L, MIX_WIDTH), row),
            pl.BlockSpec((nb, L, LANES), row),
            pl.BlockSpec((CONV_WIDTH, 2 * MIX_WIDTH), const),
            pl.BlockSpec((1, LANES), const),
            pl.BlockSpec((1, MIX_WIDTH), const),
        ],
        out_specs=pl.BlockSpec((nb, L, MIX_WIDTH), row),
        out_shape=jax.ShapeDtypeStruct((B, S, MIX_WIDTH), BF16),
        scratch_shapes=[
            pltpu.VMEM((nb, L, 2 * MIX_WIDTH), F32),
            pltpu.VMEM((nb, MLSTM_HEADS, MLSTM_HEAD_DIM, 2 * LANES), F32),
            pltpu.VMEM((nb, 8, LANES), F32),
        ],
        compiler_params=_params(("parallel", "arbitrary")),
        name="mlstm",
    )(qk_raw, vc, o_pre, misc, conv_w, gb, head_norm.reshape(1, MIX_WIDTH))


ZERO_WEIGHT_LOG2 = 160.0
NORM_SLACK = 1.02


def _alibi_first_tile(qn2, kn2, slopes):
    B, H, S = qn2.shape
    nt = S // ATTN_BLOCK
    k_max = jnp.sqrt(jnp.max(kn2[:, :, :H], axis=1))
    q_max = jnp.sqrt(jnp.max(qn2.reshape(B, H, nt, ATTN_BLOCK), axis=3))
    c = MOBA_HEAD_DIM ** -0.5 * LOG2E
    reach = ((ZERO_WEIGHT_LOG2 + 2.0 * NORM_SLACK * c * q_max * k_max[:, :, None])
             / (jnp.asarray(slopes)[None, :, None] * LOG2E))
    tiles = jnp.minimum(jnp.ceil((reach - 1.0) / ATTN_BLOCK), nt)
    tiles = jnp.max(tiles.reshape(B, H // HEADS_PER_STEP, HEADS_PER_STEP, nt), axis=2)
    first = jnp.arange(nt, dtype=F32)[None, None, :] - tiles
    return jnp.maximum(first, 0.0).astype(jnp.int32)


def _even_layer(h, norm_mix, w_in, pool_w, pool_scale, w_out, norm_ffn, wg, wu, wd):
    B, S, D = h.shape
    slopes = (2.0 ** (-8.0 * np.arange(1, MOBA_HEADS + 1) / MOBA_HEADS)).astype(np.float32)
    ka, ub, kmean, kn2, qT, vT = _even_in(h, norm_mix, w_in, slopes)
    nb = S // MOBA_BLOCK
    kmean = kmean.reshape(B, nb, MOBA_HEADS, LANES)[..., :MOBA_HEAD_DIM].transpose(0, 2, 1, 3)
    qaT, qn2 = _moba_gate(kmean, qT)
    aT = _attention(qaT, ka, vT, MOBA_HEAD_DIM, MOBA_HEADS, "moba_attn",
                    first_tile=_alibi_first_tile(qn2, kn2, slopes))
    b_out = _pool(ub, pool_w, pool_scale)
    h = _mix_out(h, aT, b_out, w_out, a_first=True)
    return _ffn(h.reshape(B * S, D), norm_ffn, wg.astype(BF16), wu.astype(BF16), wd.astype(BF16))


def _odd_layer(h, norm_mix, w_in, conv_w, b_i, b_f, head_norm, q_norm, w_uq, kv_norm, w_ukv,
               w_out, norm_ffn, router_w, router_b, wg, wu, wd):
    B, S, D = h.shape
    qk_raw, vc, o_pre, misc, mqT, mk, mvT = _odd_in(h, norm_mix, w_in, q_norm, w_uq, kv_norm, w_ukv)
    c_out = _mlstm(qk_raw, vc, o_pre, misc, conv_w, b_i, b_f, head_norm)
    dT = _attention(mqT, mk, mvT, MLA_V_DIM, MLA_HEADS, "mla_attn")
    h = _mix_out(h, dT, c_out, w_out, a_first=False)
    return _moe(h.reshape(B * S, D), norm_ffn, router_w, router_b,
                wg.astype(BF16), wu.astype(BF16), wd.astype(BF16))


def kernel(x, p, ev_norm_mix, ev_w_in, pool_w, pool_scale, ev_w_out, ev_norm_ffn, ffn_w_gate, ffn_w_up, ffn_w_down, od_norm_mix, od_w_in, conv_w, gate_b_i, gate_b_f, mlstm_norm, mla_q_norm, mla_w_uq, mla_kv_norm, mla_w_ukv, od_w_out, od_norm_ffn, router_w, router_b, moe_w_gate, moe_w_up, moe_w_down, ple_norm, ple_w_gate, ple_w_proj, final_norm):
    B, S, D = x.shape
    depth = p.shape[0]
    assert D == D_MODEL and S % PROJ_TILE == 0 and B % MLSTM_BATCH == 0
    assert MOBA_TOPK <= S // MOBA_BLOCK <= MOBA_MAX_BLOCKS
    h = x
    for layer in range(depth):
        j = layer // 2
        if layer % 2 == 0:
            h2d = _even_layer(h, ev_norm_mix[j], ev_w_in[j], pool_w[j], pool_scale[j], ev_w_out[j],
                              ev_norm_ffn[j], ffn_w_gate[j], ffn_w_up[j], ffn_w_down[j])
        else:
            h2d = _odd_layer(h, od_norm_mix[j], od_w_in[j], conv_w[j], gate_b_i[j], gate_b_f[j],
                             mlstm_norm[j], mla_q_norm[j], mla_w_uq[j], mla_kv_norm[j], mla_w_ukv[j],
                             od_w_out[j], od_norm_ffn[j], router_w[j], router_b[j],
                             moe_w_gate[j], moe_w_up[j], moe_w_down[j])
        last = layer == depth - 1
        h2d = _ple(h2d, p[layer].reshape(B * S, PLE_DIM), ple_norm[layer], ple_w_gate[layer],
                   ple_w_proj[layer], final_g=final_norm if last else None)
        h = h2d.reshape(B, S, D)
    return h
```

```python
import functools
import math

import numpy as np
import jax
import jax.numpy as jnp
from jax import lax
from jax.experimental import pallas as pl
from jax.experimental.pallas import tpu as pltpu

F32 = jnp.float32
BF16 = jnp.bfloat16
HIGHEST = lax.Precision.HIGHEST

D_MODEL = 1024
PLE_DIM = 256
NORM_EPS = 1e-6
NEG_INF = -1e30

MOBA_HEADS = 8
MOBA_HEAD_DIM = 64
MOBA_BLOCK = 256
MOBA_TOPK = 3
POOL_WINDOWS = (2, 4, 8, 16)
POOL_GROUP_DIM = 128
POOL_HALO = 16
MLSTM_HEADS = 4
MLSTM_HEAD_DIM = 128
MLSTM_CHUNK = 128
MLSTM_BATCH = 2
assert MLSTM_CHUNK == 128
CONV_WIDTH = 4
MLA_HEADS = 4
MLA_Q_RANK = 256
MLA_KV_RANK = 128
MLA_NOPE_DIM = 64
MLA_ROPE_DIM = 32
MLA_V_DIM = 128
ROPE_BASE = 10000.0
FFN_DIM = 2816
N_EXPERTS = 8
EXPERT_DIM = 3584
MIX_WIDTH = 512

ATTN_BLOCK = 512
ROW_TILE = 512
PROJ_TILE = 1024
LANES = 128
VMEM_LIMIT = 56 * 1024 * 1024

MISC_ROPE = 0
MISC_I = 32
MISC_F = 36


def _params(sem, vmem=VMEM_LIMIT):
    return pltpu.CompilerParams(dimension_semantics=sem, vmem_limit_bytes=vmem)


def _rms(x, g):
    ms = jnp.mean(x * x, axis=-1, keepdims=True)
    return x * lax.rsqrt(ms + NORM_EPS) * g


def _sigmoid(x):
    return 1.0 / (1.0 + jnp.exp(-x))


def _dot(a, b):
    return jnp.dot(a, b, preferred_element_type=F32)


def _dot_nt(a, b, precision=None):
    return lax.dot_general(a, b, (((1,), (1,)), ((), ())), precision=precision,
                           preferred_element_type=F32)


KAUG_SEL = MOBA_HEAD_DIM
KAUG_POS = KAUG_SEL + 32
MOBA_MAX_BLOCKS = KAUG_POS - KAUG_SEL


def _bf16_terms(x, n):
    out = []
    for _ in range(n):
        bits = np.float32(x).view(np.uint32)
        kept = np.uint32((int(bits) + 0x7FFF + ((int(bits) >> 16) & 1)) & 0xFFFF0000)
        term = float(kept.view(np.float32))
        out.append(term)
        x -= term
    return tuple(out)


LOG2E = math.log2(math.e)
LOG2E_TERMS = _bf16_terms(LOG2E, 3)


def _even_in_kernel(x_ref, g_ref, wn_ref, wqT_ref, wvT_ref, ext_ref,
                    ka_ref, ub_ref, km_ref, kn_ref, qT_ref, vT_ref):
    tm = x_ref.shape[1]
    xn = _rms(x_ref[0], g_ref[...]).astype(BF16)
    n = _dot(xn, wn_ref[...])
    ka = n[:, :MOBA_HEADS * LANES]
    ka_ref[0] = (ka + ext_ref[...].astype(F32)).astype(BF16)
    ub_ref[0] = n[:, MOBA_HEADS * LANES:]
    for j in range(tm // MOBA_BLOCK):
        km_ref[0, j] = jnp.mean(ka[j * MOBA_BLOCK:(j + 1) * MOBA_BLOCK], axis=0, keepdims=True)
    slot = lax.broadcasted_iota(jnp.int32, (MOBA_HEADS * LANES, LANES), 0) // LANES
    head = lax.broadcasted_iota(jnp.int32, (MOBA_HEADS * LANES, LANES), 1)
    kn_ref[0] = _dot((ka * ka).astype(BF16), (slot == head).astype(BF16))
    qT_ref[0] = _dot_nt(wqT_ref[...], xn)
    vT = _dot_nt(wvT_ref[...], xn)
    for j in range(tm // ATTN_BLOCK):
        vT_ref[0, j] = vT[:, j * ATTN_BLOCK:(j + 1) * ATTN_BLOCK].astype(BF16)


def _moba_key_extras(S, slopes):
    pos = np.arange(S)
    blk, off = pos // MOBA_BLOCK, pos % MOBA_BLOCK
    ext = np.zeros((S, MOBA_HEADS, LANES), np.float32)
    ext[pos, :, KAUG_SEL + blk] = 1.0
    for term in range(len(LOG2E_TERMS)):
        ext[:, :, KAUG_POS + 2 * term] = slopes[None, :] * (MOBA_BLOCK * blk)[:, None]
        ext[:, :, KAUG_POS + 2 * term + 1] = slopes[None, :] * off[:, None]
    return jnp.asarray(ext.reshape(S, MOBA_HEADS * LANES), dtype=BF16)


def _even_in(x, g, w_in, slopes):
    B, S, D = x.shape
    tm = PROJ_TILE
    nb = S // MOBA_BLOCK
    wq, wk, wv, wu = (w_in[:, i * MIX_WIDTH:(i + 1) * MIX_WIDTH] for i in range(4))
    wk_slots = jnp.concatenate(
        [wk.reshape(D, MOBA_HEADS, MOBA_HEAD_DIM),
         jnp.zeros((D, MOBA_HEADS, LANES - MOBA_HEAD_DIM), F32)], axis=2).reshape(D, MOBA_HEADS * LANES)
    wn = jnp.concatenate([wk_slots, wu], axis=1).astype(BF16)
    wqT = wq.T.astype(BF16)
    wvT = wv.T.astype(BF16)
    const = lambda b, i: (0, 0)
    return pl.pallas_call(
        _even_in_kernel,
        grid=(B, S // tm),
        in_specs=[
            pl.BlockSpec((1, tm, D), lambda b, i: (b, i, 0)),
            pl.BlockSpec((1, D), const),
            pl.BlockSpec((D, MOBA_HEADS * LANES + MIX_WIDTH), const),
            pl.BlockSpec((MIX_WIDTH, D), const),
            pl.BlockSpec((MIX_WIDTH, D), const),
            pl.BlockSpec((tm, MOBA_HEADS * LANES), lambda b, i: (i, 0)),
        ],
        out_specs=[
            pl.BlockSpec((1, tm, MOBA_HEADS * LANES), lambda b, i: (b, i, 0)),
            pl.BlockSpec((1, tm, MIX_WIDTH), lambda b, i: (b, i, 0)),
            pl.BlockSpec((1, tm // MOBA_BLOCK, 1, MOBA_HEADS * LANES), lambda b, i: (b, i, 0, 0)),
            pl.BlockSpec((1, tm, LANES), lambda b, i: (b, i, 0)),
            pl.BlockSpec((1, MIX_WIDTH, tm), lambda b, i: (b, 0, i)),
            pl.BlockSpec((1, tm // ATTN_BLOCK, MIX_WIDTH, ATTN_BLOCK), lambda b, i: (b, i, 0, 0)),
        ],
        out_shape=[
            jax.ShapeDtypeStruct((B, S, MOBA_HEADS * LANES), BF16),
            jax.ShapeDtypeStruct((B, S, MIX_WIDTH), F32),
            jax.ShapeDtypeStruct((B, nb, 1, MOBA_HEADS * LANES), F32),
            jax.ShapeDtypeStruct((B, S, LANES), F32),
            jax.ShapeDtypeStruct((B, MIX_WIDTH, S), F32),
            jax.ShapeDtypeStruct((B, S // ATTN_BLOCK, MIX_WIDTH, ATTN_BLOCK), BF16),
        ],
        compiler_params=_params(("parallel", "parallel")),
        name="even_in",
    )(x, g.reshape(1, D), wn, wqT, wvT, _moba_key_extras(S, slopes))


def _moba_gate_kernel(km_ref, qT_ref, qa_ref, qn_ref):
    i = pl.program_id(1)
    nb = km_ref.shape[2]
    tq = qT_ref.shape[2]
    row = lax.broadcasted_iota(jnp.int32, (nb, tq), 0)
    own = (i * tq + lax.broadcasted_iota(jnp.int32, (nb, tq), 1)) // MOBA_BLOCK
    past = row < own
    tail_row = lax.broadcasted_iota(jnp.int32, (LANES - KAUG_POS, tq), 0)
    tail = jnp.zeros(tail_row.shape, F32)
    for term, value in enumerate(LOG2E_TERMS):
        tail = jnp.where(tail_row // 2 == term, F32(value), tail)
    tail = tail.astype(BF16)
    pad = jnp.zeros((MOBA_MAX_BLOCKS - nb, tq), BF16) if nb < MOBA_MAX_BLOCKS else None
    for h in range(MOBA_HEADS):
        q_h = qT_ref[0, h * MOBA_HEAD_DIM:(h + 1) * MOBA_HEAD_DIM, :]
        qn_ref[0, h:h + 1, :] = jnp.sum(q_h * q_h, axis=0, keepdims=True)
        gate = jnp.dot(km_ref[0, h], q_h, precision=HIGHEST, preferred_element_type=F32)
        gate = jnp.where(past, gate, NEG_INF)
        chosen = jnp.zeros(gate.shape, F32)
        for _ in range(MOBA_TOPK):
            mx = jnp.max(gate, axis=0, keepdims=True)
            first = jnp.min(jnp.where(gate == mx, row, nb), axis=0, keepdims=True)
            pick = row == first
            chosen = jnp.where(pick, 1.0, chosen)
            gate = jnp.where(pick, -jnp.inf, gate)
        keep = jnp.where(past, chosen, (row == own).astype(F32))
        sel = jnp.where(keep > 0.0, 0.0, NEG_INF).astype(BF16)
        base = h * LANES
        qa_ref[0, base:base + KAUG_SEL] = (q_h * (MOBA_HEAD_DIM ** -0.5 * LOG2E)).astype(BF16)
        qa_ref[0, base + KAUG_SEL:base + KAUG_SEL + nb] = sel
        if pad is not None:
            qa_ref[0, base + KAUG_SEL + nb:base + KAUG_POS] = pad
        qa_ref[0, base + KAUG_POS:base + LANES] = tail


def _moba_gate(kmean, qT):
    B, H, nb, dh = kmean.shape
    S = qT.shape[2]
    tq = ATTN_BLOCK
    return pl.pallas_call(
        _moba_gate_kernel,
        grid=(B, S // tq),
        in_specs=[
            pl.BlockSpec((1, H, nb, dh), lambda b, i: (b, 0, 0, 0)),
            pl.BlockSpec((1, H * dh, tq), lambda b, i: (b, 0, i)),
        ],
        out_specs=[pl.BlockSpec((1, H * LANES, tq), lambda b, i: (b, 0, i)),
                   pl.BlockSpec((1, H, tq), lambda b, i: (b, 0, i))],
        out_shape=[jax.ShapeDtypeStruct((B, H * LANES, S), BF16), jax.ShapeDtypeStruct((B, H, S), F32)],
        compiler_params=_params(("parallel", "parallel")),
        name="moba_gate",
    )(kmean, qT)


HEADS_PER_STEP = 4
SUM_ROWS = 16


def _attn_kernel(lo_ref, q_ref, k_ref, v_ref, o_ref, sa_ref, sb_ref, m_ref, acc_ref):
    i = pl.program_id(2)
    lo = lo_ref[(pl.program_id(0) * pl.num_programs(1) + pl.program_id(1)) * pl.num_programs(2) + i]
    n_past = i - lo
    tq = q_ref.shape[2]
    tk = ATTN_BLOCK
    hp = HEADS_PER_STEP
    dv = v_ref.shape[2] // hp
    m_ref[...] = jnp.full(m_ref.shape, NEG_INF, F32)
    acc_ref[...] = jnp.zeros(acc_ref.shape, F32)
    ones_rows = jnp.ones((SUM_ROWS, tk), BF16)

    def scores(kvt, s_ref, diag):
        start = pl.multiple_of(kvt * tk, tk)
        k_tile = k_ref[0, pl.ds(start, tk), :]
        for g in range(hp):
            s = _dot(k_tile[:, g * LANES:(g + 1) * LANES], q_ref[0, g * LANES:(g + 1) * LANES, :])
            if diag:
                key = lax.broadcasted_iota(jnp.int32, (tk, tq), 0)
                qry = lax.broadcasted_iota(jnp.int32, (tk, tq), 1)
                s = jnp.where(key <= qry, s, NEG_INF)
            s_ref[g] = s

    def consume(kvt, s_ref):
        v_tile = v_ref[0, kvt]
        for g in range(hp):
            s = s_ref[g]
            m_run = m_ref[g]
            m_new = jnp.maximum(m_run, jnp.max(s, axis=0, keepdims=True))
            p = jnp.exp2(s - m_new).astype(BF16)
            v_aug = jnp.concatenate([v_tile[g * dv:(g + 1) * dv], ones_rows], axis=0)
            acc_ref[g] = jnp.exp2(m_run - m_new) * acc_ref[g] + _dot(v_aug, p)
            m_ref[g] = m_new

    tile_at = lambda t: jnp.where(t == 0, i, lo + t - 1)
    scores(i, sa_ref, True)

    def pair(p, carry):
        t = 2 * p
        scores(tile_at(t + 1), sb_ref, False)
        consume(tile_at(t), sa_ref)
        scores(tile_at(t + 2), sa_ref, False)
        consume(tile_at(t + 1), sb_ref)
        return carry

    lax.fori_loop(0, n_past // 2, pair, 0)
    last = 2 * (n_past // 2)

    @pl.when(n_past % 2 == 1)
    def _():
        scores(tile_at(last + 1), sb_ref, False)
        consume(tile_at(last), sa_ref)
        consume(tile_at(last + 1), sb_ref)

    @pl.when(n_past % 2 == 0)
    def _():
        consume(tile_at(last), sa_ref)

    for g in range(hp):
        o_ref[0, g * dv:(g + 1) * dv, :] = (acc_ref[g, :dv] / acc_ref[g, dv:dv + 1]).astype(o_ref.dtype)


def _attention(qT, k, vT, dv, heads, name, first_tile=None):
    B, _, S = qT.shape
    tq = ATTN_BLOCK
    nt = S // ATTN_BLOCK
    hp = HEADS_PER_STEP
    if first_tile is None:
        first_tile = jnp.zeros((B, heads // hp, nt), jnp.int32)
    return pl.pallas_call(
        _attn_kernel,
        grid_spec=pltpu.PrefetchScalarGridSpec(
            num_scalar_prefetch=1,
            grid=(B, heads // hp, S // tq),
            in_specs=[
                pl.BlockSpec((1, hp * LANES, tq), lambda b, h, i, lo: (b, h, i)),
                pl.BlockSpec((1, S, hp * LANES), lambda b, h, i, lo: (b, 0, h)),
                pl.BlockSpec((1, nt, hp * dv, ATTN_BLOCK), lambda b, h, i, lo: (b, 0, h, 0)),
            ],
            out_specs=pl.BlockSpec((1, hp * dv, tq), lambda b, h, i, lo: (b, h, i)),
            scratch_shapes=[pltpu.VMEM((hp, ATTN_BLOCK, tq), F32), pltpu.VMEM((hp, ATTN_BLOCK, tq), F32),
                            pltpu.VMEM((hp, 1, tq), F32), pltpu.VMEM((hp, dv + SUM_ROWS, tq), F32)],
        ),
        out_shape=jax.ShapeDtypeStruct((B, heads * dv, S), BF16),
        compiler_params=_params(("parallel", "parallel", "arbitrary")),
        name=name,
    )(first_tile.reshape(-1), qT, k, vT)


def _pool_kernel(x_ref, halo_ref, w_ref, sc_ref, o_ref, xs_ref):
    i = pl.program_id(1)
    tm = x_ref.shape[1]
    x = x_ref[0]
    xs_ref[0:POOL_HALO] = jnp.where(i > 0, halo_ref[0], 0.0)
    xs_ref[POOL_HALO:POOL_HALO + tm] = x
    t = i * tm + lax.broadcasted_iota(jnp.int32, (tm, 1), 0)
    outs = []
    for g, win in enumerate(POOL_WINDOWS):
        lo, hi = g * POOL_GROUP_DIM, (g + 1) * POOL_GROUP_DIM
        xg = x[:, lo:hi]
        acc = xg
        for d in range(1, win):
            acc = acc + xs_ref[POOL_HALO - d:POOL_HALO - d + tm, lo:hi]
        count = jnp.minimum(t + 1, win).astype(F32)
        outs.append(_dot((acc / count - xg).astype(BF16), w_ref[g]))
    o_ref[0] = (jnp.concatenate(outs, axis=1) * sc_ref[...]).astype(o_ref.dtype)


def _pool(ub, pool_w, pool_scale):
    B, S, W = ub.shape
    tm = PROJ_TILE
    per = tm // POOL_HALO
    return pl.pallas_call(
        _pool_kernel,
        grid=(B, S // tm),
        in_specs=[
            pl.BlockSpec((1, tm, W), lambda b, i: (b, i, 0)),
            pl.BlockSpec((1, POOL_HALO, W), lambda b, i: (b, jnp.maximum(i * per - 1, 0), 0)),
            pl.BlockSpec(pool_w.shape, lambda b, i: (0, 0, 0)),
            pl.BlockSpec((1, W), lambda b, i: (0, 0)),
        ],
        out_specs=pl.BlockSpec((1, tm, W), lambda b, i: (b, i, 0)),
        out_shape=jax.ShapeDtypeStruct((B, S, W), BF16),
        scratch_shapes=[pltpu.VMEM((POOL_HALO + tm, W), F32)],
        compiler_params=_params(("parallel", "parallel")),
        name="pool",
    )(ub, ub, pool_w.astype(BF16), pool_scale.reshape(1, W))


def _mix_out_kernel(h_ref, aT_ref, b_ref, w_ref, o_ref, *, a_first):
    a = aT_ref[0].astype(F32).T.astype(BF16)
    b = b_ref[0].astype(BF16)
    lo, hi = (a, b) if a_first else (b, a)
    y = _dot(lo, w_ref[:MIX_WIDTH]) + _dot(hi, w_ref[MIX_WIDTH:])
    o_ref[0] = h_ref[0] + y


def _mix_out(h, aT, b, w_out, a_first):
    B, S, D = h.shape
    tm = PROJ_TILE
    return pl.pallas_call(
        functools.partial(_mix_out_kernel, a_first=a_first),
        grid=(B, S // tm),
        in_specs=[
            pl.BlockSpec((1, tm, D), lambda b_, i: (b_, i, 0)),
            pl.BlockSpec((1, MIX_WIDTH, tm), lambda b_, i: (b_, 0, i)),
            pl.BlockSpec((1, tm, MIX_WIDTH), lambda b_, i: (b_, i, 0)),
            pl.BlockSpec((2 * MIX_WIDTH, D), lambda b_, i: (0, 0)),
        ],
        out_specs=pl.BlockSpec((1, tm, D), lambda b_, i: (b_, i, 0)),
        out_shape=jax.ShapeDtypeStruct((B, S, D), F32),
        compiler_params=_params(("parallel", "parallel")),
        name="mix_out",
    )(h, aT, b, w_out.astype(BF16))


def _swiglu_step(xn, wg, wu, wd):
    gt = _dot(xn, wg)
    up = _dot(xn, wu)
    return _dot((gt * _sigmoid(gt) * up).astype(BF16), wd)


def _ffn_kernel(h_ref, g_ref, wg_ref, wu_ref, wd_ref, o_ref):
    h = h_ref[...]
    xn = _rms(h, g_ref[...]).astype(BF16)
    o_ref[...] = h + _swiglu_step(xn, wg_ref[...], wu_ref[...], wd_ref[...])


def _ffn(h2d, g, wg, wu, wd, tm=ROW_TILE):
    T, D = h2d.shape
    F = wg.shape[1]
    resident = lambda shape: pl.BlockSpec(shape, lambda i: (0, 0), pipeline_mode=pl.Buffered(1))
    return pl.pallas_call(
        _ffn_kernel,
        grid=(T // tm,),
        in_specs=[
            pl.BlockSpec((tm, D), lambda i: (i, 0)),
            pl.BlockSpec((1, D), lambda i: (0, 0)),
            resident((D, F)),
            resident((D, F)),
            resident((F, D)),
        ],
        out_specs=pl.BlockSpec((tm, D), lambda i: (i, 0)),
        out_shape=jax.ShapeDtypeStruct((T, D), F32),
        compiler_params=_params(("parallel",)),
        name="ffn",
    )(h2d, g.reshape(1, D), wg, wu, wd)


ROUTE_E0, ROUTE_E1, ROUTE_W0, ROUTE_W1, ROUTE_R0, ROUTE_R1 = range(6)
MOE_TILE = 512
MOE_TF = EXPERT_DIM // 2


def _lane_pick(tile, lane, idx):
    return jnp.sum(jnp.where(lane == idx, tile, 0.0), axis=1, keepdims=True)


ROUTER_ROWS = 16


def _router_kernel(h_ref, g_ref, wT_ref, b_ref, route_ref, cnt_ref):
    tm = h_ref.shape[0]

    @pl.when(pl.program_id(0) == 0)
    def _():
        cnt_ref[...] = jnp.zeros_like(cnt_ref)

    xn = _rms(h_ref[...], g_ref[...])
    logits = _dot_nt(wT_ref[...], xn, precision=HIGHEST) + b_ref[...]
    row = lax.broadcasted_iota(jnp.int32, logits.shape, 0)
    logits = jnp.where(row < N_EXPERTS, logits, -jnp.inf)
    v0 = jnp.max(logits, axis=0, keepdims=True)
    i0 = jnp.min(jnp.where(logits == v0, row, ROUTER_ROWS), axis=0, keepdims=True)
    rest = jnp.where(row == i0, -jnp.inf, logits)
    v1 = jnp.max(rest, axis=0, keepdims=True)
    i1 = jnp.min(jnp.where(rest == v1, row, ROUTER_ROWS), axis=0, keepdims=True)
    e1 = jnp.exp(v1 - v0)
    w0 = 1.0 / (1.0 + e1)
    sel = (row == i0).astype(F32) + (row == i1).astype(F32)
    earlier = (lax.broadcasted_iota(jnp.int32, (tm, tm), 0)
               < lax.broadcasted_iota(jnp.int32, (tm, tm), 1))
    counts = cnt_ref[:, 0:1]
    rank = _dot(sel.astype(BF16), earlier.astype(BF16)) + counts
    cnt_ref[...] = jnp.broadcast_to(counts + jnp.sum(sel, axis=1, keepdims=True), cnt_ref.shape)
    pick = lambda idx: jnp.sum(jnp.where(row == idx, rank, 0.0), axis=0, keepdims=True)
    rows = [None] * 6
    rows[ROUTE_E0], rows[ROUTE_E1] = i0.astype(F32), i1.astype(F32)
    rows[ROUTE_W0], rows[ROUTE_W1] = w0, e1 * w0
    rows[ROUTE_R0], rows[ROUTE_R1] = pick(i0), pick(i1)
    routeT = jnp.concatenate(rows + [jnp.zeros((LANES - len(rows), tm), F32)], axis=0)
    route_ref[...] = routeT.T


def _router(h2d, g, router_w, router_b):
    T, D = h2d.shape
    tm = ROW_TILE
    wT = jnp.zeros((ROUTER_ROWS, D), F32).at[:N_EXPERTS].set(router_w.T)
    b = jnp.zeros((ROUTER_ROWS, 1), F32).at[:N_EXPERTS, 0].set(router_b)
    return pl.pallas_call(
        _router_kernel,
        grid=(T // tm,),
        in_specs=[
            pl.BlockSpec((tm, D), lambda i: (i, 0)),
            pl.BlockSpec((1, D), lambda i: (0, 0)),
            pl.BlockSpec((ROUTER_ROWS, D), lambda i: (0, 0)),
            pl.BlockSpec((ROUTER_ROWS, 1), lambda i: (0, 0)),
        ],
        out_specs=[pl.BlockSpec((tm, LANES), lambda i: (i, 0)),
                   pl.BlockSpec((ROUTER_ROWS, LANES), lambda i: (0, 0))],
        out_shape=[jax.ShapeDtypeStruct((T, LANES), F32), jax.ShapeDtypeStruct((ROUTER_ROWS, LANES), F32)],
        compiler_params=_params(("arbitrary",)),
        name="router",
    )(h2d, g.reshape(1, D), wT, b)


SUBLANES = 8
ROW_LANES = D_MODEL // SUBLANES
assert ROW_LANES == LANES


def _row_tile(ref, r):
    return ref.at[pl.ds(pl.multiple_of(r * SUBLANES, SUBLANES), SUBLANES), :]


def _to_row_tiles(ref, x):
    tm = x.shape[0]
    for s in range(SUBLANES):
        ref[pl.ds(s, tm, stride=SUBLANES), :] = x[:, s * LANES:(s + 1) * LANES]


def _from_row_tiles(ref):
    tm = ref.shape[0] // SUBLANES
    return jnp.concatenate([ref[pl.ds(s, tm, stride=SUBLANES), :] for s in range(SUBLANES)], axis=1)


def _row_copies(pos_ref, base, r, src_of, dst_of, sem):
    return [pltpu.make_async_copy(src_of(k, pos_ref[base + 2 * r + k]),
                                  dst_of(k, pos_ref[base + 2 * r + k]), sem) for k in range(2)]


def _start_rows(tm, make):
    def issue(r, c):
        for k, cp in enumerate(make(r)):
            cp.start(priority=k)
        return c

    lax.fori_loop(0, tm, issue, 0, unroll=8)


def _wait_rows(tm, make):
    def drain(r, c):
        for cp in make(r):
            cp.wait()
        return c

    lax.fori_loop(0, tm, drain, 0, unroll=8)


def _dispatch_kernel(pos_ref, pad_ref, h_ref, g_ref, xs_ref, xn_ref, zero_ref, sem, zsem):
    tm = h_ref.shape[0]
    i = pl.program_id(0)
    slot = i % 2

    @pl.when(i == 0)
    def _():
        zero_ref[...] = jnp.zeros_like(zero_ref)
        for e in range(N_EXPERTS):
            first, count = pad_ref[e], pad_ref[N_EXPERTS + e]
            fill = lambda r, first=first: pltpu.make_async_copy(zero_ref, _row_tile(xs_ref, first + r), zsem)

            def start(r, c, fill=fill):
                fill(r).start()
                return c

            def wait(r, c, fill=fill):
                fill(r).wait()
                return c

            lax.fori_loop(0, count, start, 0)
            lax.fori_loop(0, count, wait, 0)

    def scatters(tile, buf):
        return lambda r: _row_copies(
            pos_ref, tile * (2 * tm), r, lambda k, p: _row_tile(xn_ref.at[buf], r),
            lambda k, p: _row_tile(xs_ref, p), sem.at[buf])

    _to_row_tiles(xn_ref.at[slot], _rms(h_ref[...], g_ref[...]))
    _start_rows(tm, scatters(i, slot))

    @pl.when(i > 0)
    def _():
        _wait_rows(tm, scatters(i - 1, 1 - slot))

    @pl.when(i == pl.num_programs(0) - 1)
    def _():
        _wait_rows(tm, scatters(i, slot))


def _dispatch(pos, pad, h2d, g, n_rows):
    T, D = h2d.shape
    tm = ROW_TILE
    return pl.pallas_call(
        _dispatch_kernel,
        grid_spec=pltpu.PrefetchScalarGridSpec(
            num_scalar_prefetch=2,
            grid=(T // tm,),
            in_specs=[pl.BlockSpec((tm, D), lambda i, pos, pad: (i, 0)),
                      pl.BlockSpec((1, D), lambda i, pos, pad: (0, 0))],
            out_specs=pl.BlockSpec(memory_space=pl.ANY),
            scratch_shapes=[pltpu.VMEM((2, tm * SUBLANES, ROW_LANES), F32),
                            pltpu.VMEM((SUBLANES, ROW_LANES), F32),
                            pltpu.SemaphoreType.DMA((2,)), pltpu.SemaphoreType.DMA(())],
        ),
        out_shape=jax.ShapeDtypeStruct((n_rows * SUBLANES, ROW_LANES), F32),
        compiler_params=_params(("arbitrary",)),
        name="moe_dispatch",
    )(pos, pad, h2d, g.reshape(1, D))


def _moe_ffn_kernel(te_ref, nv_ref, x_ref, wg_ref, wu_ref, wd_ref, o_ref, xb_ref, acc_ref):
    j = pl.program_id(0)
    f = pl.program_id(1)
    valid = j < nv_ref[0]

    @pl.when(f == 0)
    def _():
        acc_ref[...] = jnp.zeros_like(acc_ref)

    @pl.when(valid & (f == 0))
    def _():
        xb_ref[...] = _from_row_tiles(x_ref).astype(BF16)

    @pl.when(valid)
    def _():
        acc_ref[...] += _swiglu_step(xb_ref[...], wg_ref[0], wu_ref[0], wd_ref[0])

    @pl.when(f == pl.num_programs(1) - 1)
    def _():
        _to_row_tiles(o_ref, acc_ref[...])


def _moe_ffn(tile_expert, n_valid, xs, wg, wu, wd):
    N, D = xs.shape[0] // SUBLANES, D_MODEL
    tm, tf = MOE_TILE, MOE_TF
    F = wg.shape[2]
    return pl.pallas_call(
        _moe_ffn_kernel,
        grid_spec=pltpu.PrefetchScalarGridSpec(
            num_scalar_prefetch=2,
            grid=(N // tm, F // tf),
            in_specs=[pl.BlockSpec((tm * SUBLANES, ROW_LANES), lambda j, f, te, nv: (j, 0)),
                      pl.BlockSpec((1, D, tf), lambda j, f, te, nv: (te[j], 0, f)),
                      pl.BlockSpec((1, D, tf), lambda j, f, te, nv: (te[j], 0, f)),
                      pl.BlockSpec((1, tf, D), lambda j, f, te, nv: (te[j], f, 0))],
            out_specs=pl.BlockSpec((tm * SUBLANES, ROW_LANES), lambda j, f, te, nv: (j, 0)),
            scratch_shapes=[pltpu.VMEM((tm, D), BF16), pltpu.VMEM((tm, D), F32)],
        ),
        out_shape=jax.ShapeDtypeStruct(xs.shape, F32),
        compiler_params=_params(("arbitrary", "arbitrary")),
        name="moe_ffn",
    )(tile_expert, n_valid, xs, wg, wu, wd)


def _combine_kernel(*refs, final):
    if final:
        pos_ref, h_ref, route_ref, p_ref, g_ref, wg_ref, wp_ref, fg_ref, ys_ref, o_ref, y_ref, sem = refs
    else:
        pos_ref, h_ref, route_ref, p_ref, g_ref, wg_ref, wp_ref, ys_ref, o_ref, y_ref, sem = refs
    tm = h_ref.shape[0]
    i = pl.program_id(0)
    slot = i % 2

    def gathers(tile, buf):
        return lambda r: _row_copies(
            pos_ref, tile * (2 * tm), r, lambda k, p: _row_tile(ys_ref, p),
            lambda k, p: _row_tile(y_ref.at[buf, k], r), sem.at[buf])

    @pl.when(i == 0)
    def _():
        _start_rows(tm, gathers(0, 0))

    @pl.when(i + 1 < pl.num_programs(0))
    def _():
        _start_rows(tm, gathers(i + 1, 1 - slot))

    _wait_rows(tm, gathers(i, slot))
    route = route_ref[...]
    lane = lax.broadcasted_iota(jnp.int32, route.shape, 1)
    h = (h_ref[...] + _lane_pick(route, lane, ROUTE_W0) * _from_row_tiles(y_ref.at[slot, 0])
         + _lane_pick(route, lane, ROUTE_W1) * _from_row_tiles(y_ref.at[slot, 1]))
    o_ref[...] = _ple_math(h, p_ref[...], g_ref[...], wg_ref[...], wp_ref[...],
                           fg_ref[...] if final else None)


def _combine(pos, h2d, route, ys, p2d, ple_g, ple_w_gate, ple_w_proj, final_g=None):
    T, D = h2d.shape
    tm = ROW_TILE
    const = lambda i, pos: (0, 0)
    final = final_g is not None
    ple_specs = [pl.BlockSpec((tm, PLE_DIM), lambda i, pos: (i, 0)), pl.BlockSpec((1, D), const),
                 pl.BlockSpec((D, D), const), pl.BlockSpec((PLE_DIM, D), const)]
    ple_args = [p2d, ple_g.reshape(1, D), ple_w_gate.astype(BF16), ple_w_proj.astype(BF16)]
    if final:
        ple_specs.append(pl.BlockSpec((1, D), const))
        ple_args.append(final_g.reshape(1, D))
    return pl.pallas_call(
        functools.partial(_combine_kernel, final=final),
        grid_spec=pltpu.PrefetchScalarGridSpec(
            num_scalar_prefetch=1,
            grid=(T // tm,),
            in_specs=[pl.BlockSpec((tm, D), lambda i, pos: (i, 0)),
                      pl.BlockSpec((tm, LANES), lambda i, pos: (i, 0)),
                      *ple_specs,
                      pl.BlockSpec(memory_space=pl.ANY)],
            out_specs=pl.BlockSpec((tm, D), lambda i, pos: (i, 0)),
            scratch_shapes=[pltpu.VMEM((2, 2, tm * SUBLANES, ROW_LANES), F32),
                            pltpu.SemaphoreType.DMA((2,))],
        ),
        out_shape=jax.ShapeDtypeStruct((T, D), F32),
        compiler_params=_params(("arbitrary",)),
        name="moe_combine",
    )(pos, h2d, route, *ple_args, ys)


def _moe(h2d, g, router_w, router_b, wg, wu, wd, ple):
    T, D = h2d.shape
    tm = MOE_TILE
    route, counts = _router(h2d, g, router_w, router_b)
    cnt = counts[:N_EXPERTS, 0].astype(jnp.int32)
    padded = (cnt + tm - 1) // tm * tm
    ends = jnp.cumsum(padded)
    start = ends - padded
    e01 = route[:, ROUTE_E0:ROUTE_E1 + 1].astype(jnp.int32)
    r01 = route[:, ROUTE_R0:ROUTE_R1 + 1].astype(jnp.int32)
    pos = (start[e01] + r01).reshape(2 * T)
    n_rows = 2 * T + N_EXPERTS * tm
    tile_row = jnp.arange(n_rows // tm, dtype=jnp.int32) * tm
    tile_expert = jnp.minimum(jnp.sum(tile_row[:, None] >= ends[None, :], axis=1), N_EXPERTS - 1).astype(jnp.int32)
    n_valid = (ends[-1:] // tm).astype(jnp.int32)
    pad = jnp.concatenate([start + cnt, padded - cnt]).astype(jnp.int32)
    xs = _dispatch(pos, pad, h2d, g, n_rows)
    ys = _moe_ffn(tile_expert, n_valid, xs, wg, wu, wd)
    return _combine(pos, h2d, route, ys, *ple)


def _ple_math(h, p, g, w_gate, w_proj, final_g=None):
    gate = _sigmoid(_dot(_rms(h, g).astype(BF16), w_gate))
    out = h + gate * _dot(p.astype(BF16), w_proj)
    return out if final_g is None else _rms(out, final_g)


def _ple_kernel(h_ref, p_ref, g_ref, wg_ref, wp_ref, o_ref):
    o_ref[...] = _ple_math(h_ref[...], p_ref[...], g_ref[...], wg_ref[...], wp_ref[...])


def _ple(h2d, p2d, g, w_gate, w_proj):
    T, D = h2d.shape
    tm = PROJ_TILE
    return pl.pallas_call(
        _ple_kernel,
        grid=(T // tm,),
        in_specs=[
            pl.BlockSpec((tm, D), lambda i: (i, 0)),
            pl.BlockSpec((tm, PLE_DIM), lambda i: (i, 0)),
            pl.BlockSpec((1, D), lambda i: (0, 0)),
            pl.BlockSpec((D, D), lambda i: (0, 0)),
            pl.BlockSpec((PLE_DIM, D), lambda i: (0, 0)),
        ],
        out_specs=pl.BlockSpec((tm, D), lambda i: (i, 0)),
        out_shape=jax.ShapeDtypeStruct((T, D), F32),
        compiler_params=_params(("parallel",)),
        name="ple",
    )(h2d, p2d, g.reshape(1, D), w_gate.astype(BF16), w_proj.astype(BF16))


ODD_MAIN = 4 * MIX_WIDTH
ODD_COLS = ODD_MAIN + MLA_Q_RANK + MLA_KV_RANK + 2 * LANES
MLA_QK_SCALE = (MLA_NOPE_DIM + MLA_ROPE_DIM) ** -0.5 * LOG2E
ROPE_HALF = MLA_ROPE_DIM // 2


def _odd_in_kernel(x_ref, g_ref, wn_ref, qn_ref, wuqT_ref, kvn_ref, wk2_ref, wvT_ref,
                   cosT_ref, sinT_ref, cc_ref, ss_ref,
                   qk_ref, vc_ref, op_ref, misc_ref, mq_ref, mk_ref, mv_ref):
    tm = x_ref.shape[1]
    xn = _rms(x_ref[0], g_ref[...]).astype(BF16)
    u = _dot(xn, wn_ref[...])
    qk_ref[0] = u[:, :2 * MIX_WIDTH]
    vc_ref[0] = u[:, 2 * MIX_WIDTH:3 * MIX_WIDTH].astype(BF16)
    op_ref[0] = u[:, 3 * MIX_WIDTH:ODD_MAIN]
    c0 = ODD_MAIN
    c_q = u[:, c0:c0 + MLA_Q_RANK]
    c0 += MLA_Q_RANK
    c_kv = u[:, c0:c0 + MLA_KV_RANK]
    c0 += MLA_KV_RANK
    misc = u[:, c0:c0 + LANES]
    misc_sw = u[:, c0 + LANES:c0 + 2 * LANES]
    misc_ref[0] = misc
    cqn = _rms(c_q, qn_ref[...]).astype(BF16)
    qT = _dot_nt(wuqT_ref[...], cqn)
    cosT = cosT_ref[...]
    sinT = sinT_ref[...]
    for h in range(MLA_HEADS):
        r = h * LANES
        mq_ref[0, r:r + MLA_NOPE_DIM] = (qT[r:r + MLA_NOPE_DIM] * MLA_QK_SCALE).astype(BF16)
        x1 = qT[r + MLA_NOPE_DIM:r + MLA_NOPE_DIM + ROPE_HALF]
        x2 = qT[r + MLA_NOPE_DIM + ROPE_HALF:r + MLA_NOPE_DIM + MLA_ROPE_DIM]
        mq_ref[0, r + MLA_NOPE_DIM:r + MLA_NOPE_DIM + ROPE_HALF] = (
            (x1 * cosT - x2 * sinT) * MLA_QK_SCALE).astype(BF16)
        mq_ref[0, r + MLA_NOPE_DIM + ROPE_HALF:r + MLA_NOPE_DIM + MLA_ROPE_DIM] = (
            (x1 * sinT + x2 * cosT) * MLA_QK_SCALE).astype(BF16)
        mq_ref[0, r + MLA_NOPE_DIM + MLA_ROPE_DIM:r + LANES] = jnp.zeros(
            (LANES - MLA_NOPE_DIM - MLA_ROPE_DIM, tm), BF16)
    ckvn = _rms(c_kv, kvn_ref[...]).astype(BF16)
    k_rot = (misc * cc_ref[...] + misc_sw * ss_ref[...]).astype(BF16)
    mk_ref[0] = _dot(jnp.concatenate([ckvn, k_rot], axis=1), wk2_ref[...]).astype(BF16)
    vT = _dot_nt(wvT_ref[...], ckvn)
    for j in range(tm // ATTN_BLOCK):
        mv_ref[0, j] = vT[:, j * ATTN_BLOCK:(j + 1) * ATTN_BLOCK].astype(BF16)


def _rope_tables(S):
    inv_freq = ROPE_BASE ** (-jnp.arange(ROPE_HALF, dtype=F32) / ROPE_HALF)
    ang = jnp.arange(S, dtype=F32)[:, None] * inv_freq[None, :]
    cos, sin = jnp.cos(ang), jnp.sin(ang)
    pad = jnp.zeros((S, LANES - MLA_ROPE_DIM), F32)
    cc = jnp.concatenate([cos, cos, pad], axis=1)
    ss = jnp.concatenate([-sin, sin, pad], axis=1)
    return cos.T, sin.T, cc, ss


def _odd_in(x, g, w_in, q_norm, w_uq, kv_norm, w_ukv):
    B, S, D = x.shape
    tm = PROJ_TILE
    cuts = np.cumsum([MIX_WIDTH] * 4 + [MLSTM_HEADS, MLSTM_HEADS, MLA_Q_RANK, MLA_KV_RANK]).tolist()
    w_main = w_in[:, :cuts[3]]
    w_i = w_in[:, cuts[3]:cuts[4]]
    w_f = w_in[:, cuts[4]:cuts[5]]
    w_cq = w_in[:, cuts[5]:cuts[6]]
    w_ckv = w_in[:, cuts[6]:cuts[7]]
    w_kr = w_in[:, cuts[7]:]
    w_kr_sw = jnp.concatenate([w_kr[:, ROPE_HALF:], w_kr[:, :ROPE_HALF]], axis=1)
    zpad = lambda n: jnp.zeros((D, n), F32)
    w_misc = jnp.concatenate([w_kr, w_i, w_f, zpad(LANES - MLA_ROPE_DIM - 2 * MLSTM_HEADS)], axis=1)
    w_misc_sw = jnp.concatenate([w_kr_sw, zpad(LANES - MLA_ROPE_DIM)], axis=1)
    wn = jnp.concatenate([w_main, w_cq, w_ckv, w_misc, w_misc_sw], axis=1).astype(BF16)
    qd = MLA_NOPE_DIM + MLA_ROPE_DIM
    w_uq_h = w_uq.reshape(MLA_Q_RANK, MLA_HEADS, qd)
    w_uq_h = jnp.concatenate([w_uq_h, jnp.zeros((MLA_Q_RANK, MLA_HEADS, LANES - qd), F32)], axis=2)
    wuqT = w_uq_h.reshape(MLA_Q_RANK, MLA_HEADS * LANES).T.astype(BF16)
    w_ukv_h = w_ukv.reshape(MLA_KV_RANK, MLA_HEADS, MLA_NOPE_DIM + MLA_V_DIM)
    w_k = jnp.concatenate([w_ukv_h[:, :, :MLA_NOPE_DIM],
                           jnp.zeros((MLA_KV_RANK, MLA_HEADS, LANES - MLA_NOPE_DIM), F32)], axis=2)
    place = jnp.zeros((LANES, MLA_HEADS, LANES), F32)
    eye = jnp.eye(MLA_ROPE_DIM, dtype=F32)
    place = place.at[:MLA_ROPE_DIM, :, MLA_NOPE_DIM:MLA_NOPE_DIM + MLA_ROPE_DIM].set(
        jnp.broadcast_to(eye[:, None, :], (MLA_ROPE_DIM, MLA_HEADS, MLA_ROPE_DIM)))
    wk2 = jnp.concatenate([w_k, place], axis=0).reshape(MLA_KV_RANK + LANES, MLA_HEADS * LANES).astype(BF16)
    wvT = w_ukv_h[:, :, MLA_NOPE_DIM:].reshape(MLA_KV_RANK, MLA_HEADS * MLA_V_DIM).T.astype(BF16)
    cosT, sinT, cc, ss = _rope_tables(S)
    row = lambda b, i: (b, i, 0)
    const = lambda b, i: (0, 0)
    nb = S // ATTN_BLOCK
    return pl.pallas_call(
        _odd_in_kernel,
        grid=(B, S // tm),
        in_specs=[
            pl.BlockSpec((1, tm, D), row),
            pl.BlockSpec((1, D), const),
            pl.BlockSpec((D, ODD_COLS), const),
            pl.BlockSpec((1, MLA_Q_RANK), const),
            pl.BlockSpec((MLA_HEADS * LANES, MLA_Q_RANK), const),
            pl.BlockSpec((1, MLA_KV_RANK), const),
            pl.BlockSpec((MLA_KV_RANK + LANES, MLA_HEADS * LANES), const),
            pl.BlockSpec((MLA_HEADS * MLA_V_DIM, MLA_KV_RANK), const),
            pl.BlockSpec((ROPE_HALF, tm), lambda b, i: (0, i)),
            pl.BlockSpec((ROPE_HALF, tm), lambda b, i: (0, i)),
            pl.BlockSpec((tm, LANES), lambda b, i: (i, 0)),
            pl.BlockSpec((tm, LANES), lambda b, i: (i, 0)),
        ],
        out_specs=[
            pl.BlockSpec((1, tm, 2 * MIX_WIDTH), row),
            pl.BlockSpec((1, tm, MIX_WIDTH), row),
            pl.BlockSpec((1, tm, MIX_WIDTH), row),
            pl.BlockSpec((1, tm, LANES), row),
            pl.BlockSpec((1, MLA_HEADS * LANES, tm), lambda b, i: (b, 0, i)),
            pl.BlockSpec((1, tm, MLA_HEADS * LANES), row),
            pl.BlockSpec((1, tm // ATTN_BLOCK, MLA_HEADS * MLA_V_DIM, ATTN_BLOCK), lambda b, i: (b, i, 0, 0)),
        ],
        out_shape=[
            jax.ShapeDtypeStruct((B, S, 2 * MIX_WIDTH), F32),
            jax.ShapeDtypeStruct((B, S, MIX_WIDTH), BF16),
            jax.ShapeDtypeStruct((B, S, MIX_WIDTH), F32),
            jax.ShapeDtypeStruct((B, S, LANES), F32),
            jax.ShapeDtypeStruct((B, MLA_HEADS * LANES, S), BF16),
            jax.ShapeDtypeStruct((B, S, MLA_HEADS * LANES), BF16),
            jax.ShapeDtypeStruct((B, nb, MLA_HEADS * MLA_V_DIM, ATTN_BLOCK), BF16),
        ],
        compiler_params=_params(("parallel", "parallel")),
        name="odd_in",
    )(x, g.reshape(1, D), wn, q_norm.reshape(1, -1), wuqT, kv_norm.reshape(1, -1), wk2, wvT,
      cosT, sinT, cc, ss)


def _log_sigmoid(x):
    return jnp.minimum(x, 0.0) - jnp.log(1.0 + jnp.exp(-jnp.abs(x)))


def _mlstm_kernel(qk_ref, v_ref, op_ref, misc_ref, cw_ref, gb_ref, hn_ref, o_ref,
                  prev_ref, cn_ref, m_ref):
    c = pl.program_id(1)
    L = qk_ref.shape[1]
    row = lax.broadcasted_iota(jnp.int32, (L, 1), 0)
    lane = lax.broadcasted_iota(jnp.int32, (L, LANES), 1)
    is_f = (lane >= MISC_F) & (lane < MISC_F + MLSTM_HEADS)
    causal = lax.broadcasted_iota(jnp.int32, (L, L), 1) <= lax.broadcasted_iota(jnp.int32, (L, L), 0)

    @pl.when(c == 0)
    def _():
        prev_ref[...] = jnp.zeros_like(prev_ref)
        cn_ref[...] = jnp.zeros_like(cn_ref)
        m_ref[...] = jnp.zeros_like(m_ref)

    seqs = range(qk_ref.shape[0])
    heads = [(bb, h) for bb in seqs for h in range(MLSTM_HEADS)]
    d = MLSTM_HEAD_DIM

    qk = []
    for bb in seqs:
        x = qk_ref[bb]
        prev = prev_ref[bb]
        conv = x * cw_ref[CONV_WIDTH - 1:CONV_WIDTH, :]
        for j in range(1, CONV_WIDTH):
            shifted = pltpu.roll(jnp.where(row >= L - j, prev, x), j, axis=0)
            conv = conv + shifted * cw_ref[CONV_WIDTH - 1 - j:CONV_WIDTH - j, :]
        prev_ref[bb] = x
        qk.append(conv * _sigmoid(conv))

    sel_r = lax.broadcasted_iota(jnp.int32, (LANES, 2 * MIX_WIDTH), 0)
    sel_c = lax.broadcasted_iota(jnp.int32, (LANES, 2 * MIX_WIDTH), 1)
    spread = (sel_r == MISC_I + sel_c // LANES).astype(F32)
    pick = (lax.broadcasted_iota(jnp.int32, (8, LANES), 1)
            == MISC_I + lax.broadcasted_iota(jnp.int32, (8, LANES), 0)).astype(F32)
    mean_mat = jnp.full((d, d), 1.0 / d, F32)
    cols, rows = [], []
    for bb in seqs:
        gates = misc_ref[bb] + gb_ref[...]
        z = jnp.where(is_f, _log_sigmoid(gates), gates)
        cum = jnp.dot(causal.astype(F32), z, precision=HIGHEST, preferred_element_type=F32)
        z = jnp.where(is_f, cum, z)
        cols.append(jnp.dot(z, spread, precision=HIGHEST, preferred_element_type=F32))
        rows.append(_dot_nt(pick, z, precision=HIGHEST))

    q, k, v_aug, i_b, b_b, m_prev, m_t, w_inter, scores, inter = ({} for _ in range(10))
    ones_blk = jnp.ones((L, LANES), BF16)
    for key in heads:
        bb, h = key
        lo, hi = h * d, (h + 1) * d
        q[key] = qk[bb][:, lo:hi].astype(BF16)
        k[key] = qk[bb][:, MIX_WIDTH + lo:MIX_WIDTH + hi] * (d ** -0.5)
        v_aug[key] = jnp.concatenate([v_ref[bb, :, lo:hi], ones_blk], axis=1)
        scores[key] = _dot_nt(q[key], k[key].astype(BF16))
        inter[key] = _dot(q[key], cn_ref[bb, h].astype(BF16))
    intra = {}
    for key in heads:
        bb, h = key
        i_b[key] = cols[bb][:, h * LANES:(h + 1) * LANES]
        b_b[key] = cols[bb][:, (MLSTM_HEADS + h) * LANES:(MLSTM_HEADS + h + 1) * LANES]
        i_row = rows[bb][h:h + 1, :]
        b_row = rows[bb][MLSTM_HEADS + h:MLSTM_HEADS + h + 1, :]
        m_prev[key] = m_ref[bb, h:h + 1, :]
        intra[key] = jnp.where(causal, b_b[key] - b_row + i_row, NEG_INF)
    for key in heads:
        m_inter = b_b[key] + m_prev[key]
        m_t[key] = jnp.maximum(m_inter, jnp.max(intra[key], axis=1, keepdims=True))
        w_inter[key] = jnp.exp(m_inter - m_t[key])
    intra_o = {}
    for key in heads:
        a = jnp.exp(intra[key] - m_t[key]) * scores[key]
        intra_o[key] = _dot(a.astype(BF16), v_aug[key])
    for key in heads:
        bb, h = key
        lo, hi = h * d, (h + 1) * d
        num = w_inter[key] * inter[key][:, :d] + intra_o[key][:, :d]
        den = w_inter[key] * inter[key][:, d:] + intra_o[key][:, d:]
        hh = num / jnp.maximum(jnp.abs(den), jnp.exp(-m_t[key]))
        ms = jnp.dot(hh * hh, mean_mat, precision=HIGHEST, preferred_element_type=F32)
        hh = hh * lax.rsqrt(ms + NORM_EPS) * hn_ref[:, lo:hi]
        o_ref[bb, :, lo:hi] = (hh * _sigmoid(op_ref[bb, :, lo:hi])).astype(o_ref.dtype)
    for key in heads:
        bb, h = key
        b_end = b_b[key][L - 1:L, :]
        g = b_end - b_b[key] + i_b[key]
        m_new = jnp.maximum(b_end + m_prev[key], jnp.max(g, axis=0, keepdims=True))
        decay = jnp.exp(b_end + m_prev[key] - m_new)
        kw = k[key] * jnp.exp(g - m_new)
        cn_ref[bb, h] = (jnp.concatenate([decay, decay], axis=1) * cn_ref[bb, h]
                         + _dot(kw.T.astype(BF16), v_aug[key]))
        m_ref[bb, h:h + 1, :] = m_new


def _mlstm(qk_raw, vc, o_pre, misc, conv_w, b_i, b_f, head_norm):
    B, S, _ = qk_raw.shape
    L = MLSTM_CHUNK
    nb = MLSTM_BATCH
    gb = jnp.zeros((1, LANES), F32).at[0, MISC_I:MISC_I + MLSTM_HEADS].set(b_i)
    gb = gb.at[0, MISC_F:MISC_F + MLSTM_HEADS].set(b_f)
    row = lambda b, c: (b, c, 0)
    const = lambda b, c: (0, 0)
    return pl.pallas_call(
        _mlstm_kernel,
        grid=(B // nb, S // L),
        in_specs=[
            pl.BlockSpec((nb, L, 2 * MIX_WIDTH), row),
            pl.BlockSpec((nb, L, MIX_WIDTH), row),
            pl.BlockSpec((nb, L, MIX_WIDTH), row),
            pl.BlockSpec((nb, L, LANES), row),
            pl.BlockSpec((CONV_WIDTH, 2 * MIX_WIDTH), const),
            pl.BlockSpec((1, LANES), const),
            pl.BlockSpec((1, MIX_WIDTH), const),
        ],
        out_specs=pl.BlockSpec((nb, L, MIX_WIDTH), row),
        out_shape=jax.ShapeDtypeStruct((B, S, MIX_WIDTH), BF16),
        scratch_shapes=[
            pltpu.VMEM((nb, L, 2 * MIX_WIDTH), F32),
            pltpu.VMEM((nb, MLSTM_HEADS, MLSTM_HEAD_DIM, 2 * LANES), F32),
            pltpu.VMEM((nb, 8, LANES), F32),
        ],
        compiler_params=_params(("parallel", "arbitrary")),
        name="mlstm",
    )(qk_raw, vc, o_pre, misc, conv_w, gb, head_norm.reshape(1, MIX_WIDTH))


ZERO_WEIGHT_LOG2 = 160.0
NORM_SLACK = 1.02


def _alibi_first_tile(qn2, kn2, slopes):
    B, H, S = qn2.shape
    nt = S // ATTN_BLOCK
    k_max = jnp.sqrt(jnp.max(kn2[:, :, :H], axis=1))
    q_max = jnp.sqrt(jnp.max(qn2.reshape(B, H, nt, ATTN_BLOCK), axis=3))
    c = MOBA_HEAD_DIM ** -0.5 * LOG2E
    reach = ((ZERO_WEIGHT_LOG2 + 2.0 * NORM_SLACK * c * q_max * k_max[:, :, None])
             / (jnp.asarray(slopes)[None, :, None] * LOG2E))
    tiles = jnp.minimum(jnp.ceil((reach - 1.0) / ATTN_BLOCK), nt)
    tiles = jnp.max(tiles.reshape(B, H // HEADS_PER_STEP, HEADS_PER_STEP, nt), axis=2)
    first = jnp.arange(nt, dtype=F32)[None, None, :] - tiles
    return jnp.maximum(first, 0.0).astype(jnp.int32)


def _even_layer(h, norm_mix, w_in, pool_w, pool_scale, w_out, norm_ffn, wg, wu, wd):
    B, S, D = h.shape
    slopes = (2.0 ** (-8.0 * np.arange(1, MOBA_HEADS + 1) / MOBA_HEADS)).astype(np.float32)
    ka, ub, kmean, kn2, qT, vT = _even_in(h, norm_mix, w_in, slopes)
    nb = S // MOBA_BLOCK
    kmean = kmean.reshape(B, nb, MOBA_HEADS, LANES)[..., :MOBA_HEAD_DIM].transpose(0, 2, 1, 3)
    qaT, qn2 = _moba_gate(kmean, qT)
    aT = _attention(qaT, ka, vT, MOBA_HEAD_DIM, MOBA_HEADS, "moba_attn",
                    first_tile=_alibi_first_tile(qn2, kn2, slopes))
    b_out = _pool(ub, pool_w, pool_scale)
    h = _mix_out(h, aT, b_out, w_out, a_first=True)
    return _ffn(h.reshape(B * S, D), norm_ffn, wg.astype(BF16), wu.astype(BF16), wd.astype(BF16))


def _odd_layer(h, norm_mix, w_in, conv_w, b_i, b_f, head_norm, q_norm, w_uq, kv_norm, w_ukv,
               w_out, norm_ffn, router_w, router_b, wg, wu, wd, ple):
    B, S, D = h.shape
    qk_raw, vc, o_pre, misc, mqT, mk, mvT = _odd_in(h, norm_mix, w_in, q_norm, w_uq, kv_norm, w_ukv)
    c_out = _mlstm(qk_raw, vc, o_pre, misc, conv_w, b_i, b_f, head_norm)
    dT = _attention(mqT, mk, mvT, MLA_V_DIM, MLA_HEADS, "mla_attn")
    h = _mix_out(h, dT, c_out, w_out, a_first=False)
    return _moe(h.reshape(B * S, D), norm_ffn, router_w, router_b,
                wg.astype(BF16), wu.astype(BF16), wd.astype(BF16), ple)


def kernel(x, p, ev_norm_mix, ev_w_in, pool_w, pool_scale, ev_w_out, ev_norm_ffn, ffn_w_gate, ffn_w_up, ffn_w_down, od_norm_mix, od_w_in, conv_w, gate_b_i, gate_b_f, mlstm_norm, mla_q_norm, mla_w_uq, mla_kv_norm, mla_w_ukv, od_w_out, od_norm_ffn, router_w, router_b, moe_w_gate, moe_w_up, moe_w_down, ple_norm, ple_w_gate, ple_w_proj, final_norm):
    B, S, D = x.shape
    depth = p.shape[0]
    assert D == D_MODEL and S % PROJ_TILE == 0 and B % MLSTM_BATCH == 0
    assert MOBA_TOPK <= S // MOBA_BLOCK <= MOBA_MAX_BLOCKS
    h = x
    assert depth % 2 == 0
    for layer in range(depth):
        j = layer // 2
        ple = (p[layer].reshape(B * S, PLE_DIM), ple_norm[layer], ple_w_gate[layer], ple_w_proj[layer])
        if layer % 2 == 0:
            h2d = _even_layer(h, ev_norm_mix[j], ev_w_in[j], pool_w[j], pool_scale[j], ev_w_out[j],
                              ev_norm_ffn[j], ffn_w_gate[j], ffn_w_up[j], ffn_w_down[j])
            h2d = _ple(h2d, *ple)
        else:
            last = layer == depth - 1
            h2d = _odd_layer(h, od_norm_mix[j], od_w_in[j], conv_w[j], gate_b_i[j], gate_b_f[j],
                             mlstm_norm[j], mla_q_norm[j], mla_w_uq[j], mla_kv_norm[j], mla_w_ukv[j],
                             od_w_out[j], od_norm_ffn[j], router_w[j], router_b[j],
                             moe_w_gate[j], moe_w_up[j], moe_w_down[j],
                             ple + ((final_norm,) if last else ()))
        h = h2d.reshape(B, S, D)
    return h
```

```python
import functools
import math

import numpy as np
import jax
import jax.numpy as jnp
from jax import lax
from jax.experimental import pallas as pl
from jax.experimental.pallas import tpu as pltpu

F32 = jnp.float32
BF16 = jnp.bfloat16
HIGHEST = lax.Precision.HIGHEST

D_MODEL = 1024
PLE_DIM = 256
NORM_EPS = 1e-6
NEG_INF = -1e30

MOBA_HEADS = 8
MOBA_HEAD_DIM = 64
MOBA_BLOCK = 256
MOBA_TOPK = 3
POOL_WINDOWS = (2, 4, 8, 16)
POOL_GROUP_DIM = 128
POOL_HALO = 16
MLSTM_HEADS = 4
MLSTM_HEAD_DIM = 128
MLSTM_CHUNK = 128
MLSTM_BATCH = 2
assert MLSTM_CHUNK == 128
CONV_WIDTH = 4
MLA_HEADS = 4
MLA_Q_RANK = 256
MLA_KV_RANK = 128
MLA_NOPE_DIM = 64
MLA_ROPE_DIM = 32
MLA_V_DIM = 128
ROPE_BASE = 10000.0
FFN_DIM = 2816
N_EXPERTS = 8
EXPERT_DIM = 3584
MIX_WIDTH = 512

ATTN_BLOCK = 512
ROW_TILE = 512
PROJ_TILE = 1024
LANES = 128
VMEM_LIMIT = 56 * 1024 * 1024

MISC_ROPE = 0
MISC_I = 32
MISC_F = 36


def _params(sem, vmem=VMEM_LIMIT):
    return pltpu.CompilerParams(dimension_semantics=sem, vmem_limit_bytes=vmem)


def _rms(x, g):
    ms = jnp.mean(x * x, axis=-1, keepdims=True)
    return x * lax.rsqrt(ms + NORM_EPS) * g


def _sigmoid(x):
    return 1.0 / (1.0 + jnp.exp(-x))


def _dot(a, b):
    return jnp.dot(a, b, preferred_element_type=F32)


def _dot_nt(a, b, precision=None):
    return lax.dot_general(a, b, (((1,), (1,)), ((), ())), precision=precision,
                           preferred_element_type=F32)


KAUG_SEL = MOBA_HEAD_DIM
KAUG_POS = KAUG_SEL + 32
MOBA_MAX_BLOCKS = KAUG_POS - KAUG_SEL


def _bf16_terms(x, n):
    out = []
    for _ in range(n):
        bits = np.float32(x).view(np.uint32)
        kept = np.uint32((int(bits) + 0x7FFF + ((int(bits) >> 16) & 1)) & 0xFFFF0000)
        term = float(kept.view(np.float32))
        out.append(term)
        x -= term
    return tuple(out)


LOG2E = math.log2(math.e)
LOG2E_TERMS = _bf16_terms(LOG2E, 3)


def _even_in_kernel(x_ref, g_ref, wn_ref, wqT_ref, wvT_ref, ext_ref,
                    ka_ref, ub_ref, km_ref, kn_ref, qT_ref, vT_ref):
    tm = x_ref.shape[1]
    xn = _rms(x_ref[0], g_ref[...]).astype(BF16)
    n = _dot(xn, wn_ref[...])
    ka = n[:, :MOBA_HEADS * LANES]
    ka_ref[0] = (ka + ext_ref[...].astype(F32)).astype(BF16)
    ub_ref[0] = n[:, MOBA_HEADS * LANES:]
    for j in range(tm // MOBA_BLOCK):
        km_ref[0, j] = jnp.mean(ka[j * MOBA_BLOCK:(j + 1) * MOBA_BLOCK], axis=0, keepdims=True)
    slot = lax.broadcasted_iota(jnp.int32, (MOBA_HEADS * LANES, LANES), 0) // LANES
    head = lax.broadcasted_iota(jnp.int32, (MOBA_HEADS * LANES, LANES), 1)
    kn_ref[0] = _dot((ka * ka).astype(BF16), (slot == head).astype(BF16))
    qT_ref[0] = _dot_nt(wqT_ref[...], xn)
    vT = _dot_nt(wvT_ref[...], xn)
    for j in range(tm // ATTN_BLOCK):
        vT_ref[0, j] = vT[:, j * ATTN_BLOCK:(j + 1) * ATTN_BLOCK].astype(BF16)


def _moba_key_extras(S, slopes):
    pos = np.arange(S)
    blk, off = pos // MOBA_BLOCK, pos % MOBA_BLOCK
    ext = np.zeros((S, MOBA_HEADS, LANES), np.float32)
    ext[pos, :, KAUG_SEL + blk] = 1.0
    for term in range(len(LOG2E_TERMS)):
        ext[:, :, KAUG_POS + 2 * term] = slopes[None, :] * (MOBA_BLOCK * blk)[:, None]
        ext[:, :, KAUG_POS + 2 * term + 1] = slopes[None, :] * off[:, None]
    return jnp.asarray(ext.reshape(S, MOBA_HEADS * LANES), dtype=BF16)


def _even_in(x, g, w_in, slopes):
    B, S, D = x.shape
    tm = PROJ_TILE
    nb = S // MOBA_BLOCK
    wq, wk, wv, wu = (w_in[:, i * MIX_WIDTH:(i + 1) * MIX_WIDTH] for i in range(4))
    wk_slots = jnp.concatenate(
        [wk.reshape(D, MOBA_HEADS, MOBA_HEAD_DIM),
         jnp.zeros((D, MOBA_HEADS, LANES - MOBA_HEAD_DIM), F32)], axis=2).reshape(D, MOBA_HEADS * LANES)
    wn = jnp.concatenate([wk_slots, wu], axis=1).astype(BF16)
    wqT = wq.T.astype(BF16)
    wvT = wv.T.astype(BF16)
    const = lambda b, i: (0, 0)
    return pl.pallas_call(
        _even_in_kernel,
        grid=(B, S // tm),
        in_specs=[
            pl.BlockSpec((1, tm, D), lambda b, i: (b, i, 0)),
            pl.BlockSpec((1, D), const),
            pl.BlockSpec((D, MOBA_HEADS * LANES + MIX_WIDTH), const),
            pl.BlockSpec((MIX_WIDTH, D), const),
            pl.BlockSpec((MIX_WIDTH, D), const),
            pl.BlockSpec((tm, MOBA_HEADS * LANES), lambda b, i: (i, 0)),
        ],
        out_specs=[
            pl.BlockSpec((1, tm, MOBA_HEADS * LANES), lambda b, i: (b, i, 0)),
            pl.BlockSpec((1, tm, MIX_WIDTH), lambda b, i: (b, i, 0)),
            pl.BlockSpec((1, tm // MOBA_BLOCK, 1, MOBA_HEADS * LANES), lambda b, i: (b, i, 0, 0)),
            pl.BlockSpec((1, tm, LANES), lambda b, i: (b, i, 0)),
            pl.BlockSpec((1, MIX_WIDTH, tm), lambda b, i: (b, 0, i)),
            pl.BlockSpec((1, tm // ATTN_BLOCK, MIX_WIDTH, ATTN_BLOCK), lambda b, i: (b, i, 0, 0)),
        ],
        out_shape=[
            jax.ShapeDtypeStruct((B, S, MOBA_HEADS * LANES), BF16),
            jax.ShapeDtypeStruct((B, S, MIX_WIDTH), F32),
            jax.ShapeDtypeStruct((B, nb, 1, MOBA_HEADS * LANES), F32),
            jax.ShapeDtypeStruct((B, S, LANES), F32),
            jax.ShapeDtypeStruct((B, MIX_WIDTH, S), F32),
            jax.ShapeDtypeStruct((B, S // ATTN_BLOCK, MIX_WIDTH, ATTN_BLOCK), BF16),
        ],
        compiler_params=_params(("parallel", "parallel")),
        name="even_in",
    )(x, g.reshape(1, D), wn, wqT, wvT, _moba_key_extras(S, slopes))


def _moba_gate_kernel(km_ref, qT_ref, qa_ref, qn_ref):
    i = pl.program_id(1)
    nb = km_ref.shape[2]
    tq = qT_ref.shape[2]
    row = lax.broadcasted_iota(jnp.int32, (nb, tq), 0)
    own = (i * tq + lax.broadcasted_iota(jnp.int32, (nb, tq), 1)) // MOBA_BLOCK
    past = row < own
    tail_row = lax.broadcasted_iota(jnp.int32, (LANES - KAUG_POS, tq), 0)
    tail = jnp.zeros(tail_row.shape, F32)
    for term, value in enumerate(LOG2E_TERMS):
        tail = jnp.where(tail_row // 2 == term, F32(value), tail)
    tail = tail.astype(BF16)
    pad = jnp.zeros((MOBA_MAX_BLOCKS - nb, tq), BF16) if nb < MOBA_MAX_BLOCKS else None
    for h in range(MOBA_HEADS):
        q_h = qT_ref[0, h * MOBA_HEAD_DIM:(h + 1) * MOBA_HEAD_DIM, :]
        qn_ref[0, h:h + 1, :] = jnp.sum(q_h * q_h, axis=0, keepdims=True)
        gate = jnp.dot(km_ref[0, h], q_h, precision=HIGHEST, preferred_element_type=F32)
        gate = jnp.where(past, gate, NEG_INF)
        chosen = jnp.zeros(gate.shape, F32)
        for _ in range(MOBA_TOPK):
            mx = jnp.max(gate, axis=0, keepdims=True)
            first = jnp.min(jnp.where(gate == mx, row, nb), axis=0, keepdims=True)
            pick = row == first
            chosen = jnp.where(pick, 1.0, chosen)
            gate = jnp.where(pick, -jnp.inf, gate)
        keep = jnp.where(past, chosen, (row == own).astype(F32))
        sel = jnp.where(keep > 0.0, 0.0, NEG_INF).astype(BF16)
        base = h * LANES
        qa_ref[0, base:base + KAUG_SEL] = (q_h * (MOBA_HEAD_DIM ** -0.5 * LOG2E)).astype(BF16)
        qa_ref[0, base + KAUG_SEL:base + KAUG_SEL + nb] = sel
        if pad is not None:
            qa_ref[0, base + KAUG_SEL + nb:base + KAUG_POS] = pad
        qa_ref[0, base + KAUG_POS:base + LANES] = tail


def _moba_gate(kmean, qT):
    B, H, nb, dh = kmean.shape
    S = qT.shape[2]
    tq = ATTN_BLOCK
    return pl.pallas_call(
        _moba_gate_kernel,
        grid=(B, S // tq),
        in_specs=[
            pl.BlockSpec((1, H, nb, dh), lambda b, i: (b, 0, 0, 0)),
            pl.BlockSpec((1, H * dh, tq), lambda b, i: (b, 0, i)),
        ],
        out_specs=[pl.BlockSpec((1, H * LANES, tq), lambda b, i: (b, 0, i)),
                   pl.BlockSpec((1, H, tq), lambda b, i: (b, 0, i))],
        out_shape=[jax.ShapeDtypeStruct((B, H * LANES, S), BF16), jax.ShapeDtypeStruct((B, H, S), F32)],
        compiler_params=_params(("parallel", "parallel")),
        name="moba_gate",
    )(kmean, qT)


HEADS_PER_STEP = 4
SUM_ROWS = 16


def _attn_kernel(lo_ref, q_ref, k_ref, v_ref, o_ref, sa_ref, sb_ref, m_ref, acc_ref):
    i = pl.program_id(2)
    lo = lo_ref[(pl.program_id(0) * pl.num_programs(1) + pl.program_id(1)) * pl.num_programs(2) + i]
    n_past = i - lo
    tq = q_ref.shape[2]
    tk = ATTN_BLOCK
    hp = HEADS_PER_STEP
    dv = v_ref.shape[2] // hp
    m_ref[...] = jnp.full(m_ref.shape, NEG_INF, F32)
    acc_ref[...] = jnp.zeros(acc_ref.shape, F32)
    ones_rows = jnp.ones((SUM_ROWS, tk), BF16)

    def scores(kvt, s_ref, diag):
        start = pl.multiple_of(kvt * tk, tk)
        k_tile = k_ref[0, pl.ds(start, tk), :]
        for g in range(hp):
            s = _dot(k_tile[:, g * LANES:(g + 1) * LANES], q_ref[0, g * LANES:(g + 1) * LANES, :])
            if diag:
                key = lax.broadcasted_iota(jnp.int32, (tk, tq), 0)
                qry = lax.broadcasted_iota(jnp.int32, (tk, tq), 1)
                s = jnp.where(key <= qry, s, NEG_INF)
            s_ref[g] = s

    def consume(kvt, s_ref):
        v_tile = v_ref[0, kvt]
        for g in range(hp):
            s = s_ref[g]
            m_run = m_ref[g]
            m_new = jnp.maximum(m_run, jnp.max(s, axis=0, keepdims=True))
            p = jnp.exp2(s - m_new).astype(BF16)
            v_aug = jnp.concatenate([v_tile[g * dv:(g + 1) * dv], ones_rows], axis=0)
            acc_ref[g] = jnp.exp2(m_run - m_new) * acc_ref[g] + _dot(v_aug, p)
            m_ref[g] = m_new

    tile_at = lambda t: jnp.where(t == 0, i, lo + t - 1)
    scores(i, sa_ref, True)

    def pair(p, carry):
        t = 2 * p
        scores(tile_at(t + 1), sb_ref, False)
        consume(tile_at(t), sa_ref)
        scores(tile_at(t + 2), sa_ref, False)
        consume(tile_at(t + 1), sb_ref)
        return carry

    lax.fori_loop(0, n_past // 2, pair, 0)
    last = 2 * (n_past // 2)

    @pl.when(n_past % 2 == 1)
    def _():
        scores(tile_at(last + 1), sb_ref, False)
        consume(tile_at(last), sa_ref)
        consume(tile_at(last + 1), sb_ref)

    @pl.when(n_past % 2 == 0)
    def _():
        consume(tile_at(last), sa_ref)

    for g in range(hp):
        o_ref[0, g * dv:(g + 1) * dv, :] = (acc_ref[g, :dv] / acc_ref[g, dv:dv + 1]).astype(o_ref.dtype)


def _attention(qT, k, vT, dv, heads, name, first_tile=None):
    B, _, S = qT.shape
    tq = ATTN_BLOCK
    nt = S // ATTN_BLOCK
    hp = HEADS_PER_STEP
    if first_tile is None:
        first_tile = jnp.zeros((B, heads // hp, nt), jnp.int32)
    return pl.pallas_call(
        _attn_kernel,
        grid_spec=pltpu.PrefetchScalarGridSpec(
            num_scalar_prefetch=1,
            grid=(B, heads // hp, S // tq),
            in_specs=[
                pl.BlockSpec((1, hp * LANES, tq), lambda b, h, i, lo: (b, h, i)),
                pl.BlockSpec((1, S, hp * LANES), lambda b, h, i, lo: (b, 0, h)),
                pl.BlockSpec((1, nt, hp * dv, ATTN_BLOCK), lambda b, h, i, lo: (b, 0, h, 0)),
            ],
            out_specs=pl.BlockSpec((1, hp * dv, tq), lambda b, h, i, lo: (b, h, i)),
            scratch_shapes=[pltpu.VMEM((hp, ATTN_BLOCK, tq), F32), pltpu.VMEM((hp, ATTN_BLOCK, tq), F32),
                            pltpu.VMEM((hp, 1, tq), F32), pltpu.VMEM((hp, dv + SUM_ROWS, tq), F32)],
        ),
        out_shape=jax.ShapeDtypeStruct((B, heads * dv, S), BF16),
        compiler_params=_params(("parallel", "parallel", "arbitrary")),
        name=name,
    )(first_tile.reshape(-1), qT, k, vT)


def _pool_kernel(x_ref, halo_ref, w_ref, sc_ref, o_ref, xs_ref):
    i = pl.program_id(1)
    tm = x_ref.shape[1]
    x = x_ref[0]
    xs_ref[0:POOL_HALO] = jnp.where(i > 0, halo_ref[0], 0.0)
    xs_ref[POOL_HALO:POOL_HALO + tm] = x
    t = i * tm + lax.broadcasted_iota(jnp.int32, (tm, 1), 0)
    outs = []
    for g, win in enumerate(POOL_WINDOWS):
        lo, hi = g * POOL_GROUP_DIM, (g + 1) * POOL_GROUP_DIM
        xg = x[:, lo:hi]
        acc = xg
        for d in range(1, win):
            acc = acc + xs_ref[POOL_HALO - d:POOL_HALO - d + tm, lo:hi]
        count = jnp.minimum(t + 1, win).astype(F32)
        outs.append(_dot((acc / count - xg).astype(BF16), w_ref[g]))
    o_ref[0] = (jnp.concatenate(outs, axis=1) * sc_ref[...]).astype(o_ref.dtype)


def _pool(ub, pool_w, pool_scale):
    B, S, W = ub.shape
    tm = PROJ_TILE
    per = tm // POOL_HALO
    return pl.pallas_call(
        _pool_kernel,
        grid=(B, S // tm),
        in_specs=[
            pl.BlockSpec((1, tm, W), lambda b, i: (b, i, 0)),
            pl.BlockSpec((1, POOL_HALO, W), lambda b, i: (b, jnp.maximum(i * per - 1, 0), 0)),
            pl.BlockSpec(pool_w.shape, lambda b, i: (0, 0, 0)),
            pl.BlockSpec((1, W), lambda b, i: (0, 0)),
        ],
        out_specs=pl.BlockSpec((1, tm, W), lambda b, i: (b, i, 0)),
        out_shape=jax.ShapeDtypeStruct((B, S, W), BF16),
        scratch_shapes=[pltpu.VMEM((POOL_HALO + tm, W), F32)],
        compiler_params=_params(("parallel", "parallel")),
        name="pool",
    )(ub, ub, pool_w.astype(BF16), pool_scale.reshape(1, W))


def _mix_out_kernel(h_ref, aT_ref, b_ref, w_ref, o_ref, *, a_first):
    a = aT_ref[0].astype(F32).T.astype(BF16)
    b = b_ref[0].astype(BF16)
    lo, hi = (a, b) if a_first else (b, a)
    y = _dot(lo, w_ref[:MIX_WIDTH]) + _dot(hi, w_ref[MIX_WIDTH:])
    o_ref[0] = h_ref[0] + y


def _mix_out(h, aT, b, w_out, a_first):
    B, S, D = h.shape
    tm = PROJ_TILE
    return pl.pallas_call(
        functools.partial(_mix_out_kernel, a_first=a_first),
        grid=(B, S // tm),
        in_specs=[
            pl.BlockSpec((1, tm, D), lambda b_, i: (b_, i, 0)),
            pl.BlockSpec((1, MIX_WIDTH, tm), lambda b_, i: (b_, 0, i)),
            pl.BlockSpec((1, tm, MIX_WIDTH), lambda b_, i: (b_, i, 0)),
            pl.BlockSpec((2 * MIX_WIDTH, D), lambda b_, i: (0, 0)),
        ],
        out_specs=pl.BlockSpec((1, tm, D), lambda b_, i: (b_, i, 0)),
        out_shape=jax.ShapeDtypeStruct((B, S, D), F32),
        compiler_params=_params(("parallel", "parallel")),
        name="mix_out",
    )(h, aT, b, w_out.astype(BF16))


def _swiglu_step(xn, wg, wu, wd):
    gt = _dot(xn, wg)
    up = _dot(xn, wu)
    return _dot((gt * _sigmoid(gt) * up).astype(BF16), wd)


def _ffn_kernel(h_ref, g_ref, wg_ref, wu_ref, wd_ref, o_ref):
    h = h_ref[...]
    xn = _rms(h, g_ref[...]).astype(BF16)
    o_ref[...] = h + _swiglu_step(xn, wg_ref[...], wu_ref[...], wd_ref[...])


def _ffn(h2d, g, wg, wu, wd, tm=ROW_TILE):
    T, D = h2d.shape
    F = wg.shape[1]
    resident = lambda shape: pl.BlockSpec(shape, lambda i: (0, 0), pipeline_mode=pl.Buffered(1))
    return pl.pallas_call(
        _ffn_kernel,
        grid=(T // tm,),
        in_specs=[
            pl.BlockSpec((tm, D), lambda i: (i, 0)),
            pl.BlockSpec((1, D), lambda i: (0, 0)),
            resident((D, F)),
            resident((D, F)),
            resident((F, D)),
        ],
        out_specs=pl.BlockSpec((tm, D), lambda i: (i, 0)),
        out_shape=jax.ShapeDtypeStruct((T, D), F32),
        compiler_params=_params(("parallel",)),
        name="ffn",
    )(h2d, g.reshape(1, D), wg, wu, wd)


ROUTE_E0, ROUTE_E1, ROUTE_W0, ROUTE_W1, ROUTE_R0, ROUTE_R1 = range(6)
MOE_TILE = 512
MOE_TF = EXPERT_DIM // 2


def _lane_pick(tile, lane, idx):
    return jnp.sum(jnp.where(lane == idx, tile, 0.0), axis=1, keepdims=True)


ROUTER_ROWS = 16


def _router_kernel(h_ref, g_ref, wT_ref, b_ref, route_ref, cnt_ref):
    tm = h_ref.shape[0]

    @pl.when(pl.program_id(0) == 0)
    def _():
        cnt_ref[...] = jnp.zeros_like(cnt_ref)

    xn = _rms(h_ref[...], g_ref[...])
    logits = _dot_nt(wT_ref[...], xn, precision=HIGHEST) + b_ref[...]
    row = lax.broadcasted_iota(jnp.int32, logits.shape, 0)
    logits = jnp.where(row < N_EXPERTS, logits, -jnp.inf)
    v0 = jnp.max(logits, axis=0, keepdims=True)
    i0 = jnp.min(jnp.where(logits == v0, row, ROUTER_ROWS), axis=0, keepdims=True)
    rest = jnp.where(row == i0, -jnp.inf, logits)
    v1 = jnp.max(rest, axis=0, keepdims=True)
    i1 = jnp.min(jnp.where(rest == v1, row, ROUTER_ROWS), axis=0, keepdims=True)
    e1 = jnp.exp(v1 - v0)
    w0 = 1.0 / (1.0 + e1)
    sel = (row == i0).astype(F32) + (row == i1).astype(F32)
    earlier = (lax.broadcasted_iota(jnp.int32, (tm, tm), 0)
               < lax.broadcasted_iota(jnp.int32, (tm, tm), 1))
    counts = cnt_ref[:, 0:1]
    rank = _dot(sel.astype(BF16), earlier.astype(BF16)) + counts
    cnt_ref[...] = jnp.broadcast_to(counts + jnp.sum(sel, axis=1, keepdims=True), cnt_ref.shape)
    pick = lambda idx: jnp.sum(jnp.where(row == idx, rank, 0.0), axis=0, keepdims=True)
    rows = [None] * 6
    rows[ROUTE_E0], rows[ROUTE_E1] = i0.astype(F32), i1.astype(F32)
    rows[ROUTE_W0], rows[ROUTE_W1] = w0, e1 * w0
    rows[ROUTE_R0], rows[ROUTE_R1] = pick(i0), pick(i1)
    routeT = jnp.concatenate(rows + [jnp.zeros((LANES - len(rows), tm), F32)], axis=0)
    route_ref[...] = routeT.T


def _router(h2d, g, router_w, router_b):
    T, D = h2d.shape
    tm = ROW_TILE
    wT = jnp.zeros((ROUTER_ROWS, D), F32).at[:N_EXPERTS].set(router_w.T)
    b = jnp.zeros((ROUTER_ROWS, 1), F32).at[:N_EXPERTS, 0].set(router_b)
    return pl.pallas_call(
        _router_kernel,
        grid=(T // tm,),
        in_specs=[
            pl.BlockSpec((tm, D), lambda i: (i, 0)),
            pl.BlockSpec((1, D), lambda i: (0, 0)),
            pl.BlockSpec((ROUTER_ROWS, D), lambda i: (0, 0)),
            pl.BlockSpec((ROUTER_ROWS, 1), lambda i: (0, 0)),
        ],
        out_specs=[pl.BlockSpec((tm, LANES), lambda i: (i, 0)),
                   pl.BlockSpec((ROUTER_ROWS, LANES), lambda i: (0, 0))],
        out_shape=[jax.ShapeDtypeStruct((T, LANES), F32), jax.ShapeDtypeStruct((ROUTER_ROWS, LANES), F32)],
        compiler_params=_params(("arbitrary",)),
        name="router",
    )(h2d, g.reshape(1, D), wT, b)


SUBLANES = 8
ROW_LANES = D_MODEL // SUBLANES
assert ROW_LANES == LANES


def _row_tile(ref, r):
    return ref.at[pl.ds(pl.multiple_of(r * SUBLANES, SUBLANES), SUBLANES), :]


def _to_row_tiles(ref, x):
    tm = x.shape[0]
    for s in range(SUBLANES):
        ref[pl.ds(s, tm, stride=SUBLANES), :] = x[:, s * LANES:(s + 1) * LANES]


def _from_row_tiles(ref):
    tm = ref.shape[0] // SUBLANES
    return jnp.concatenate([ref[pl.ds(s, tm, stride=SUBLANES), :] for s in range(SUBLANES)], axis=1)


def _row_copies(pos_ref, base, r, src_of, dst_of, sem):
    return [pltpu.make_async_copy(src_of(k, pos_ref[base + 2 * r + k]),
                                  dst_of(k, pos_ref[base + 2 * r + k]), sem) for k in range(2)]


def _start_rows(tm, make):
    def issue(r, c):
        for k, cp in enumerate(make(r)):
            cp.start(priority=k)
        return c

    lax.fori_loop(0, tm, issue, 0, unroll=8)


def _wait_rows(tm, make):
    def drain(r, c):
        for cp in make(r):
            cp.wait()
        return c

    lax.fori_loop(0, tm, drain, 0, unroll=8)


def _dispatch_kernel(pos_ref, pad_ref, h_ref, g_ref, xs_ref, xn_ref, zero_ref, sem, zsem):
    tm = h_ref.shape[0]
    i = pl.program_id(0)
    slot = i % 2

    @pl.when(i == 0)
    def _():
        zero_ref[...] = jnp.zeros_like(zero_ref)
        for e in range(N_EXPERTS):
            first, count = pad_ref[e], pad_ref[N_EXPERTS + e]
            fill = lambda r, first=first: pltpu.make_async_copy(zero_ref, _row_tile(xs_ref, first + r), zsem)

            def start(r, c, fill=fill):
                fill(r).start()
                return c

            def wait(r, c, fill=fill):
                fill(r).wait()
                return c

            lax.fori_loop(0, count, start, 0)
            lax.fori_loop(0, count, wait, 0)

    def scatters(tile, buf):
        return lambda r: _row_copies(
            pos_ref, tile * (2 * tm), r, lambda k, p: _row_tile(xn_ref.at[buf], r),
            lambda k, p: _row_tile(xs_ref, p), sem.at[buf])

    _to_row_tiles(xn_ref.at[slot], _rms(h_ref[...], g_ref[...]))
    _start_rows(tm, scatters(i, slot))

    @pl.when(i > 0)
    def _():
        _wait_rows(tm, scatters(i - 1, 1 - slot))

    @pl.when(i == pl.num_programs(0) - 1)
    def _():
        _wait_rows(tm, scatters(i, slot))


def _dispatch(pos, pad, h2d, g, n_rows):
    T, D = h2d.shape
    tm = ROW_TILE
    return pl.pallas_call(
        _dispatch_kernel,
        grid_spec=pltpu.PrefetchScalarGridSpec(
            num_scalar_prefetch=2,
            grid=(T // tm,),
            in_specs=[pl.BlockSpec((tm, D), lambda i, pos, pad: (i, 0)),
                      pl.BlockSpec((1, D), lambda i, pos, pad: (0, 0))],
            out_specs=pl.BlockSpec(memory_space=pl.ANY),
            scratch_shapes=[pltpu.VMEM((2, tm * SUBLANES, ROW_LANES), F32),
                            pltpu.VMEM((SUBLANES, ROW_LANES), F32),
                            pltpu.SemaphoreType.DMA((2,)), pltpu.SemaphoreType.DMA(())],
        ),
        out_shape=jax.ShapeDtypeStruct((n_rows * SUBLANES, ROW_LANES), F32),
        compiler_params=_params(("arbitrary",)),
        name="moe_dispatch",
    )(pos, pad, h2d, g.reshape(1, D))


def _moe_ffn_kernel(te_ref, nv_ref, x_ref, wg_ref, wu_ref, wd_ref, o_ref, xb_ref, acc_ref):
    j = pl.program_id(0)
    f = pl.program_id(1)
    valid = j < nv_ref[0]
    last = pl.num_programs(1) - 1
    assert MOE_TF * 2 == EXPERT_DIM

    @pl.when(valid & (f == 0))
    def _():
        x = _from_row_tiles(x_ref).astype(BF16)
        xb_ref[...] = x
        acc_ref[...] = _swiglu_step(x, wg_ref[0], wu_ref[0], wd_ref[0])

    @pl.when(valid & (f == last))
    def _():
        _to_row_tiles(o_ref, acc_ref[...] + _swiglu_step(xb_ref[...], wg_ref[0], wu_ref[0], wd_ref[0]))

    @pl.when(jnp.logical_not(valid) & (f == last))
    def _():
        o_ref[...] = jnp.zeros_like(o_ref)


def _moe_ffn(tile_expert, n_valid, xs, wg, wu, wd):
    N, D = xs.shape[0] // SUBLANES, D_MODEL
    tm, tf = MOE_TILE, MOE_TF
    F = wg.shape[2]
    return pl.pallas_call(
        _moe_ffn_kernel,
        grid_spec=pltpu.PrefetchScalarGridSpec(
            num_scalar_prefetch=2,
            grid=(N // tm, F // tf),
            in_specs=[pl.BlockSpec((tm * SUBLANES, ROW_LANES), lambda j, f, te, nv: (j, 0)),
                      pl.BlockSpec((1, D, tf), lambda j, f, te, nv: (te[j], 0, f)),
                      pl.BlockSpec((1, D, tf), lambda j, f, te, nv: (te[j], 0, f)),
                      pl.BlockSpec((1, tf, D), lambda j, f, te, nv: (te[j], f, 0))],
            out_specs=pl.BlockSpec((tm * SUBLANES, ROW_LANES), lambda j, f, te, nv: (j, 0)),
            scratch_shapes=[pltpu.VMEM((tm, D), BF16), pltpu.VMEM((tm, D), F32)],
        ),
        out_shape=jax.ShapeDtypeStruct(xs.shape, F32),
        compiler_params=_params(("arbitrary", "arbitrary")),
        name="moe_ffn",
    )(tile_expert, n_valid, xs, wg, wu, wd)


def _combine_kernel(*refs, final):
    if final:
        pos_ref, h_ref, route_ref, p_ref, g_ref, wg_ref, wp_ref, fg_ref, ys_ref, o_ref, y_ref, sem = refs
    else:
        pos_ref, h_ref, route_ref, p_ref, g_ref, wg_ref, wp_ref, ys_ref, o_ref, y_ref, sem = refs
    tm = h_ref.shape[0]
    i = pl.program_id(0)
    slot = i % 2

    def gathers(tile, buf):
        return lambda r: _row_copies(
            pos_ref, tile * (2 * tm), r, lambda k, p: _row_tile(ys_ref, p),
            lambda k, p: _row_tile(y_ref.at[buf, k], r), sem.at[buf])

    @pl.when(i == 0)
    def _():
        _start_rows(tm, gathers(0, 0))

    @pl.when(i + 1 < pl.num_programs(0))
    def _():
        _start_rows(tm, gathers(i + 1, 1 - slot))

    _wait_rows(tm, gathers(i, slot))
    route = route_ref[...]
    lane = lax.broadcasted_iota(jnp.int32, route.shape, 1)
    h = (h_ref[...] + _lane_pick(route, lane, ROUTE_W0) * _from_row_tiles(y_ref.at[slot, 0])
         + _lane_pick(route, lane, ROUTE_W1) * _from_row_tiles(y_ref.at[slot, 1]))
    o_ref[...] = _ple_math(h, p_ref[...], g_ref[...], wg_ref[...], wp_ref[...],
                           fg_ref[...] if final else None)


def _combine(pos, h2d, route, ys, p2d, ple_g, ple_w_gate, ple_w_proj, final_g=None):
    T, D = h2d.shape
    tm = ROW_TILE
    const = lambda i, pos: (0, 0)
    final = final_g is not None
    ple_specs = [pl.BlockSpec((tm, PLE_DIM), lambda i, pos: (i, 0)), pl.BlockSpec((1, D), const),
                 pl.BlockSpec((D, D), const), pl.BlockSpec((PLE_DIM, D), const)]
    ple_args = [p2d, ple_g.reshape(1, D), ple_w_gate.astype(BF16), ple_w_proj.astype(BF16)]
    if final:
        ple_specs.append(pl.BlockSpec((1, D), const))
        ple_args.append(final_g.reshape(1, D))
    return pl.pallas_call(
        functools.partial(_combine_kernel, final=final),
        grid_spec=pltpu.PrefetchScalarGridSpec(
            num_scalar_prefetch=1,
            grid=(T // tm,),
            in_specs=[pl.BlockSpec((tm, D), lambda i, pos: (i, 0)),
                      pl.BlockSpec((tm, LANES), lambda i, pos: (i, 0)),
                      *ple_specs,
                      pl.BlockSpec(memory_space=pl.ANY)],
            out_specs=pl.BlockSpec((tm, D), lambda i, pos: (i, 0)),
            scratch_shapes=[pltpu.VMEM((2, 2, tm * SUBLANES, ROW_LANES), F32),
                            pltpu.SemaphoreType.DMA((2,))],
        ),
        out_shape=jax.ShapeDtypeStruct((T, D), F32),
        compiler_params=_params(("arbitrary",)),
        name="moe_combine",
    )(pos, h2d, route, *ple_args, ys)


def _moe(h2d, g, router_w, router_b, wg, wu, wd, ple):
    T, D = h2d.shape
    tm = MOE_TILE
    route, counts = _router(h2d, g, router_w, router_b)
    cnt = counts[:N_EXPERTS, 0].astype(jnp.int32)
    padded = (cnt + tm - 1) // tm * tm
    ends = jnp.cumsum(padded)
    start = ends - padded
    e01 = route[:, ROUTE_E0:ROUTE_E1 + 1].astype(jnp.int32)
    r01 = route[:, ROUTE_R0:ROUTE_R1 + 1].astype(jnp.int32)
    pos = (start[e01] + r01).reshape(2 * T)
    n_rows = 2 * T + N_EXPERTS * tm
    tile_row = jnp.arange(n_rows // tm, dtype=jnp.int32) * tm
    tile_expert = jnp.minimum(jnp.sum(tile_row[:, None] >= ends[None, :], axis=1), N_EXPERTS - 1).astype(jnp.int32)
    n_valid = (ends[-1:] // tm).astype(jnp.int32)
    pad = jnp.concatenate([start + cnt, padded - cnt]).astype(jnp.int32)
    xs = _dispatch(pos, pad, h2d, g, n_rows)
    ys = _moe_ffn(tile_expert, n_valid, xs, wg, wu, wd)
    return _combine(pos, h2d, route, ys, *ple)


def _ple_math(h, p, g, w_gate, w_proj, final_g=None):
    gate = _sigmoid(_dot(_rms(h, g).astype(BF16), w_gate))
    out = h + gate * _dot(p.astype(BF16), w_proj)
    return out if final_g is None else _rms(out, final_g)


def _ple_kernel(h_ref, p_ref, g_ref, wg_ref, wp_ref, o_ref):
    o_ref[...] = _ple_math(h_ref[...], p_ref[...], g_ref[...], wg_ref[...], wp_ref[...])


def _ple(h2d, p2d, g, w_gate, w_proj):
    T, D = h2d.shape
    tm = PROJ_TILE
    return pl.pallas_call(
        _ple_kernel,
        grid=(T // tm,),
        in_specs=[
            pl.BlockSpec((tm, D), lambda i: (i, 0)),
            pl.BlockSpec((tm, PLE_DIM), lambda i: (i, 0)),
            pl.BlockSpec((1, D), lambda i: (0, 0)),
            pl.BlockSpec((D, D), lambda i: (0, 0)),
            pl.BlockSpec((PLE_DIM, D), lambda i: (0, 0)),
        ],
        out_specs=pl.BlockSpec((tm, D), lambda i: (i, 0)),
        out_shape=jax.ShapeDtypeStruct((T, D), F32),
        compiler_params=_params(("parallel",)),
        name="ple",
    )(h2d, p2d, g.reshape(1, D), w_gate.astype(BF16), w_proj.astype(BF16))


ODD_MAIN = 4 * MIX_WIDTH
ODD_COLS = ODD_MAIN + MLA_Q_RANK + MLA_KV_RANK + 2 * LANES
MLA_QK_SCALE = (MLA_NOPE_DIM + MLA_ROPE_DIM) ** -0.5 * LOG2E
ROPE_HALF = MLA_ROPE_DIM // 2


def _odd_in_kernel(x_ref, g_ref, wn_ref, qn_ref, wuqT_ref, kvn_ref, wk2_ref, wvT_ref,
                   cosT_ref, sinT_ref, cc_ref, ss_ref,
                   qk_ref, vc_ref, op_ref, misc_ref, mq_ref, mk_ref, mv_ref):
    tm = x_ref.shape[1]
    xn = _rms(x_ref[0], g_ref[...]).astype(BF16)
    u = _dot(xn, wn_ref[...])
    qk_ref[0] = u[:, :2 * MIX_WIDTH]
    vc_ref[0] = u[:, 2 * MIX_WIDTH:3 * MIX_WIDTH].astype(BF16)
    op_ref[0] = u[:, 3 * MIX_WIDTH:ODD_MAIN]
    c0 = ODD_MAIN
    c_q = u[:, c0:c0 + MLA_Q_RANK]
    c0 += MLA_Q_RANK
    c_kv = u[:, c0:c0 + MLA_KV_RANK]
    c0 += MLA_KV_RANK
    misc = u[:, c0:c0 + LANES]
    misc_sw = u[:, c0 + LANES:c0 + 2 * LANES]
    misc_ref[0] = misc
    cqn = _rms(c_q, qn_ref[...]).astype(BF16)
    qT = _dot_nt(wuqT_ref[...], cqn)
    cosT = cosT_ref[...]
    sinT = sinT_ref[...]
    for h in range(MLA_HEADS):
        r = h * LANES
        mq_ref[0, r:r + MLA_NOPE_DIM] = (qT[r:r + MLA_NOPE_DIM] * MLA_QK_SCALE).astype(BF16)
        x1 = qT[r + MLA_NOPE_DIM:r + MLA_NOPE_DIM + ROPE_HALF]
        x2 = qT[r + MLA_NOPE_DIM + ROPE_HALF:r + MLA_NOPE_DIM + MLA_ROPE_DIM]
        mq_ref[0, r + MLA_NOPE_DIM:r + MLA_NOPE_DIM + ROPE_HALF] = (
            (x1 * cosT - x2 * sinT) * MLA_QK_SCALE).astype(BF16)
        mq_ref[0, r + MLA_NOPE_DIM + ROPE_HALF:r + MLA_NOPE_DIM + MLA_ROPE_DIM] = (
            (x1 * sinT + x2 * cosT) * MLA_QK_SCALE).astype(BF16)
        mq_ref[0, r + MLA_NOPE_DIM + MLA_ROPE_DIM:r + LANES] = jnp.zeros(
            (LANES - MLA_NOPE_DIM - MLA_ROPE_DIM, tm), BF16)
    ckvn = _rms(c_kv, kvn_ref[...]).astype(BF16)
    k_rot = (misc * cc_ref[...] + misc_sw * ss_ref[...]).astype(BF16)
    mk_ref[0] = _dot(jnp.concatenate([ckvn, k_rot], axis=1), wk2_ref[...]).astype(BF16)
    vT = _dot_nt(wvT_ref[...], ckvn)
    for j in range(tm // ATTN_BLOCK):
        mv_ref[0, j] = vT[:, j * ATTN_BLOCK:(j + 1) * ATTN_BLOCK].astype(BF16)


def _rope_tables(S):
    inv_freq = ROPE_BASE ** (-jnp.arange(ROPE_HALF, dtype=F32) / ROPE_HALF)
    ang = jnp.arange(S, dtype=F32)[:, None] * inv_freq[None, :]
    cos, sin = jnp.cos(ang), jnp.sin(ang)
    pad = jnp.zeros((S, LANES - MLA_ROPE_DIM), F32)
    cc = jnp.concatenate([cos, cos, pad], axis=1)
    ss = jnp.concatenate([-sin, sin, pad], axis=1)
    return cos.T, sin.T, cc, ss


def _odd_in(x, g, w_in, q_norm, w_uq, kv_norm, w_ukv):
    B, S, D = x.shape
    tm = PROJ_TILE
    cuts = np.cumsum([MIX_WIDTH] * 4 + [MLSTM_HEADS, MLSTM_HEADS, MLA_Q_RANK, MLA_KV_RANK]).tolist()
    w_main = w_in[:, :cuts[3]]
    w_i = w_in[:, cuts[3]:cuts[4]]
    w_f = w_in[:, cuts[4]:cuts[5]]
    w_cq = w_in[:, cuts[5]:cuts[6]]
    w_ckv = w_in[:, cuts[6]:cuts[7]]
    w_kr = w_in[:, cuts[7]:]
    w_kr_sw = jnp.concatenate([w_kr[:, ROPE_HALF:], w_kr[:, :ROPE_HALF]], axis=1)
    zpad = lambda n: jnp.zeros((D, n), F32)
    w_misc = jnp.concatenate([w_kr, w_i, w_f, zpad(LANES - MLA_ROPE_DIM - 2 * MLSTM_HEADS)], axis=1)
    w_misc_sw = jnp.concatenate([w_kr_sw, zpad(LANES - MLA_ROPE_DIM)], axis=1)
    wn = jnp.concatenate([w_main, w_cq, w_ckv, w_misc, w_misc_sw], axis=1).astype(BF16)
    qd = MLA_NOPE_DIM + MLA_ROPE_DIM
    w_uq_h = w_uq.reshape(MLA_Q_RANK, MLA_HEADS, qd)
    w_uq_h = jnp.concatenate([w_uq_h, jnp.zeros((MLA_Q_RANK, MLA_HEADS, LANES - qd), F32)], axis=2)
    wuqT = w_uq_h.reshape(MLA_Q_RANK, MLA_HEADS * LANES).T.astype(BF16)
    w_ukv_h = w_ukv.reshape(MLA_KV_RANK, MLA_HEADS, MLA_NOPE_DIM + MLA_V_DIM)
    w_k = jnp.concatenate([w_ukv_h[:, :, :MLA_NOPE_DIM],
                           jnp.zeros((MLA_KV_RANK, MLA_HEADS, LANES - MLA_NOPE_DIM), F32)], axis=2)
    place = jnp.zeros((LANES, MLA_HEADS, LANES), F32)
    eye = jnp.eye(MLA_ROPE_DIM, dtype=F32)
    place = place.at[:MLA_ROPE_DIM, :, MLA_NOPE_DIM:MLA_NOPE_DIM + MLA_ROPE_DIM].set(
        jnp.broadcast_to(eye[:, None, :], (MLA_ROPE_DIM, MLA_HEADS, MLA_ROPE_DIM)))
    wk2 = jnp.concatenate([w_k, place], axis=0).reshape(MLA_KV_RANK + LANES, MLA_HEADS * LANES).astype(BF16)
    wvT = w_ukv_h[:, :, MLA_NOPE_DIM:].reshape(MLA_KV_RANK, MLA_HEADS * MLA_V_DIM).T.astype(BF16)
    cosT, sinT, cc, ss = _rope_tables(S)
    row = lambda b, i: (b, i, 0)
    const = lambda b, i: (0, 0)
    nb = S // ATTN_BLOCK
    return pl.pallas_call(
        _odd_in_kernel,
        grid=(B, S // tm),
        in_specs=[
            pl.BlockSpec((1, tm, D), row),
            pl.BlockSpec((1, D), const),
            pl.BlockSpec((D, ODD_COLS), const),
            pl.BlockSpec((1, MLA_Q_RANK), const),
            pl.BlockSpec((MLA_HEADS * LANES, MLA_Q_RANK), const),
            pl.BlockSpec((1, MLA_KV_RANK), const),
            pl.BlockSpec((MLA_KV_RANK + LANES, MLA_HEADS * LANES), const),
            pl.BlockSpec((MLA_HEADS * MLA_V_DIM, MLA_KV_RANK), const),
            pl.BlockSpec((ROPE_HALF, tm), lambda b, i: (0, i)),
            pl.BlockSpec((ROPE_HALF, tm), lambda b, i: (0, i)),
            pl.BlockSpec((tm, LANES), lambda b, i: (i, 0)),
            pl.BlockSpec((tm, LANES), lambda b, i: (i, 0)),
        ],
        out_specs=[
            pl.BlockSpec((1, tm, 2 * MIX_WIDTH), row),
            pl.BlockSpec((1, tm, MIX_WIDTH), row),
            pl.BlockSpec((1, tm, MIX_WIDTH), row),
            pl.BlockSpec((1, tm, LANES), row),
            pl.BlockSpec((1, MLA_HEADS * LANES, tm), lambda b, i: (b, 0, i)),
            pl.BlockSpec((1, tm, MLA_HEADS * LANES), row),
            pl.BlockSpec((1, tm // ATTN_BLOCK, MLA_HEADS * MLA_V_DIM, ATTN_BLOCK), lambda b, i: (b, i, 0, 0)),
        ],
        out_shape=[
            jax.ShapeDtypeStruct((B, S, 2 * MIX_WIDTH), F32),
            jax.ShapeDtypeStruct((B, S, MIX_WIDTH), BF16),
            jax.ShapeDtypeStruct((B, S, MIX_WIDTH), F32),
            jax.ShapeDtypeStruct((B, S, LANES), F32),
            jax.ShapeDtypeStruct((B, MLA_HEADS * LANES, S), BF16),
            jax.ShapeDtypeStruct((B, S, MLA_HEADS * LANES), BF16),
            jax.ShapeDtypeStruct((B, nb, MLA_HEADS * MLA_V_DIM, ATTN_BLOCK), BF16),
        ],
        compiler_params=_params(("parallel", "parallel")),
        name="odd_in",
    )(x, g.reshape(1, D), wn, q_norm.reshape(1, -1), wuqT, kv_norm.reshape(1, -1), wk2, wvT,
      cosT, sinT, cc, ss)


def _log_sigmoid(x):
    return jnp.minimum(x, 0.0) - jnp.log(1.0 + jnp.exp(-jnp.abs(x)))


def _mlstm_kernel(qk_ref, v_ref, op_ref, misc_ref, cw_ref, gb_ref, hn_ref, o_ref,
                  prev_ref, cn_ref, m_ref):
    c = pl.program_id(1)
    L = qk_ref.shape[1]
    row = lax.broadcasted_iota(jnp.int32, (L, 1), 0)
    lane = lax.broadcasted_iota(jnp.int32, (L, LANES), 1)
    is_f = (lane >= MISC_F) & (lane < MISC_F + MLSTM_HEADS)
    causal = lax.broadcasted_iota(jnp.int32, (L, L), 1) <= lax.broadcasted_iota(jnp.int32, (L, L), 0)

    @pl.when(c == 0)
    def _():
        prev_ref[...] = jnp.zeros_like(prev_ref)
        cn_ref[...] = jnp.zeros_like(cn_ref)
        m_ref[...] = jnp.zeros_like(m_ref)

    seqs = range(qk_ref.shape[0])
    heads = [(bb, h) for bb in seqs for h in range(MLSTM_HEADS)]
    d = MLSTM_HEAD_DIM

    qk = []
    for bb in seqs:
        x = qk_ref[bb]
        prev = prev_ref[bb]
        conv = x * cw_ref[CONV_WIDTH - 1:CONV_WIDTH, :]
        for j in range(1, CONV_WIDTH):
            shifted = pltpu.roll(jnp.where(row >= L - j, prev, x), j, axis=0)
            conv = conv + shifted * cw_ref[CONV_WIDTH - 1 - j:CONV_WIDTH - j, :]
        prev_ref[bb] = x
        qk.append(conv * _sigmoid(conv))

    sel_r = lax.broadcasted_iota(jnp.int32, (LANES, 2 * MIX_WIDTH), 0)
    sel_c = lax.broadcasted_iota(jnp.int32, (LANES, 2 * MIX_WIDTH), 1)
    spread = (sel_r == MISC_I + sel_c // LANES).astype(F32)
    pick = (lax.broadcasted_iota(jnp.int32, (8, LANES), 1)
            == MISC_I + lax.broadcasted_iota(jnp.int32, (8, LANES), 0)).astype(F32)
    mean_mat = jnp.full((d, d), 1.0 / d, F32)
    cols, rows = [], []
    for bb in seqs:
        gates = misc_ref[bb] + gb_ref[...]
        z = jnp.where(is_f, _log_sigmoid(gates), gates)
        cum = jnp.dot(causal.astype(F32), z, precision=HIGHEST, preferred_element_type=F32)
        z = jnp.where(is_f, cum, z)
        cols.append(jnp.dot(z, spread, precision=HIGHEST, preferred_element_type=F32))
        rows.append(_dot_nt(pick, z, precision=HIGHEST))

    q, k, v_aug, i_b, b_b, m_prev, m_t, w_inter, scores, inter = ({} for _ in range(10))
    ones_blk = jnp.ones((L, LANES), BF16)
    for key in heads:
        bb, h = key
        lo, hi = h * d, (h + 1) * d
        q[key] = qk[bb][:, lo:hi].astype(BF16)
        k[key] = qk[bb][:, MIX_WIDTH + lo:MIX_WIDTH + hi] * (d ** -0.5)
        v_aug[key] = jnp.concatenate([v_ref[bb, :, lo:hi], ones_blk], axis=1)
        scores[key] = _dot_nt(q[key], k[key].astype(BF16))
        inter[key] = _dot(q[key], cn_ref[bb, h].astype(BF16))
    intra = {}
    for key in heads:
        bb, h = key
        i_b[key] = cols[bb][:, h * LANES:(h + 1) * LANES]
        b_b[key] = cols[bb][:, (MLSTM_HEADS + h) * LANES:(MLSTM_HEADS + h + 1) * LANES]
        i_row = rows[bb][h:h + 1, :]
        b_row = rows[bb][MLSTM_HEADS + h:MLSTM_HEADS + h + 1, :]
        m_prev[key] = m_ref[bb, h:h + 1, :]
        intra[key] = jnp.where(causal, b_b[key] - b_row + i_row, NEG_INF)
    for key in heads:
        m_inter = b_b[key] + m_prev[key]
        m_t[key] = jnp.maximum(m_inter, jnp.max(intra[key], axis=1, keepdims=True))
        w_inter[key] = jnp.exp(m_inter - m_t[key])
    intra_o = {}
    for key in heads:
        a = jnp.exp(intra[key] - m_t[key]) * scores[key]
        intra_o[key] = _dot(a.astype(BF16), v_aug[key])
    for key in heads:
        bb, h = key
        lo, hi = h * d, (h + 1) * d
        num = w_inter[key] * inter[key][:, :d] + intra_o[key][:, :d]
        den = w_inter[key] * inter[key][:, d:] + intra_o[key][:, d:]
        hh = num / jnp.maximum(jnp.abs(den), jnp.exp(-m_t[key]))
        ms = jnp.dot(hh * hh, mean_mat, precision=HIGHEST, preferred_element_type=F32)
        hh = hh * lax.rsqrt(ms + NORM_EPS) * hn_ref[:, lo:hi]
        o_ref[bb, :, lo:hi] = (hh * _sigmoid(op_ref[bb, :, lo:hi])).astype(o_ref.dtype)
    for key in heads:
        bb, h = key
        b_end = b_b[key][L - 1:L, :]
        g = b_end - b_b[key] + i_b[key]
        m_new = jnp.maximum(b_end + m_prev[key], jnp.max(g, axis=0, keepdims=True))
        decay = jnp.exp(b_end + m_prev[key] - m_new)
        kw = k[key] * jnp.exp(g - m_new)
        cn_ref[bb, h] = (jnp.concatenate([decay, decay], axis=1) * cn_ref[bb, h]
                         + _dot(kw.T.astype(BF16), v_aug[key]))
        m_ref[bb, h:h + 1, :] = m_new


def _mlstm(qk_raw, vc, o_pre, misc, conv_w, b_i, b_f, head_norm):
    B, S, _ = qk_raw.shape
    L = MLSTM_CHUNK
    nb = MLSTM_BATCH
    gb = jnp.zeros((1, LANES), F32).at[0, MISC_I:MISC_I + MLSTM_HEADS].set(b_i)
    gb = gb.at[0, MISC_F:MISC_F + MLSTM_HEADS].set(b_f)
    row = lambda b, c: (b, c, 0)
    const = lambda b, c: (0, 0)
    return pl.pallas_call(
        _mlstm_kernel,
        grid=(B // nb, S // L),
        in_specs=[
            pl.BlockSpec((nb, L, 2 * MIX_WIDTH), row),
            pl.BlockSpec((nb, L, MIX_WIDTH), row),
            pl.BlockSpec((nb, L, MIX_WIDTH), row),
            pl.BlockSpec((nb, L, LANES), row),
            pl.BlockSpec((CONV_WIDTH, 2 * MIX_WIDTH), const),
            pl.BlockSpec((1, LANES), const),
            pl.BlockSpec((1, MIX_WIDTH), const),
        ],
        out_specs=pl.BlockSpec((nb, L, MIX_WIDTH), row),
        out_shape=jax.ShapeDtypeStruct((B, S, MIX_WIDTH), BF16),
        scratch_shapes=[
            pltpu.VMEM((nb, L, 2 * MIX_WIDTH), F32),
            pltpu.VMEM((nb, MLSTM_HEADS, MLSTM_HEAD_DIM, 2 * LANES), F32),
            pltpu.VMEM((nb, 8, LANES), F32),
        ],
        compiler_params=_params(("parallel", "arbitrary")),
        name="mlstm",
    )(qk_raw, vc, o_pre, misc, conv_w, gb, head_norm.reshape(1, MIX_WIDTH))


ZERO_WEIGHT_LOG2 = 160.0
NORM_SLACK = 1.02


def _alibi_first_tile(qn2, kn2, slopes):
    B, H, S = qn2.shape
    nt = S // ATTN_BLOCK
    k_max = jnp.sqrt(jnp.max(kn2[:, :, :H], axis=1))
    q_max = jnp.sqrt(jnp.max(qn2.reshape(B, H, nt, ATTN_BLOCK), axis=3))
    c = MOBA_HEAD_DIM ** -0.5 * LOG2E
    reach = ((ZERO_WEIGHT_LOG2 + 2.0 * NORM_SLACK * c * q_max * k_max[:, :, None])
             / (jnp.asarray(slopes)[None, :, None] * LOG2E))
    tiles = jnp.minimum(jnp.ceil((reach - 1.0) / ATTN_BLOCK), nt)
    tiles = jnp.max(tiles.reshape(B, H // HEADS_PER_STEP, HEADS_PER_STEP, nt), axis=2)
    first = jnp.arange(nt, dtype=F32)[None, None, :] - tiles
    return jnp.maximum(first, 0.0).astype(jnp.int32)


def _even_layer(h, norm_mix, w_in, pool_w, pool_scale, w_out, norm_ffn, wg, wu, wd):
    B, S, D = h.shape
    slopes = (2.0 ** (-8.0 * np.arange(1, MOBA_HEADS + 1) / MOBA_HEADS)).astype(np.float32)
    ka, ub, kmean, kn2, qT, vT = _even_in(h, norm_mix, w_in, slopes)
    nb = S // MOBA_BLOCK
    kmean = kmean.reshape(B, nb, MOBA_HEADS, LANES)[..., :MOBA_HEAD_DIM].transpose(0, 2, 1, 3)
    qaT, qn2 = _moba_gate(kmean, qT)
    aT = _attention(qaT, ka, vT, MOBA_HEAD_DIM, MOBA_HEADS, "moba_attn",
                    first_tile=_alibi_first_tile(qn2, kn2, slopes))
    b_out = _pool(ub, pool_w, pool_scale)
    h = _mix_out(h, aT, b_out, w_out, a_first=True)
    return _ffn(h.reshape(B * S, D), norm_ffn, wg.astype(BF16), wu.astype(BF16), wd.astype(BF16))


def _odd_layer(h, norm_mix, w_in, conv_w, b_i, b_f, head_norm, q_norm, w_uq, kv_norm, w_ukv,
               w_out, norm_ffn, router_w, router_b, wg, wu, wd, ple):
    B, S, D = h.shape
    qk_raw, vc, o_pre, misc, mqT, mk, mvT = _odd_in(h, norm_mix, w_in, q_norm, w_uq, kv_norm, w_ukv)
    c_out = _mlstm(qk_raw, vc, o_pre, misc, conv_w, b_i, b_f, head_norm)
    dT = _attention(mqT, mk, mvT, MLA_V_DIM, MLA_HEADS, "mla_attn")
    h = _mix_out(h, dT, c_out, w_out, a_first=False)
    return _moe(h.reshape(B * S, D), norm_ffn, router_w, router_b,
                wg.astype(BF16), wu.astype(BF16), wd.astype(BF16), ple)


def kernel(x, p, ev_norm_mix, ev_w_in, pool_w, pool_scale, ev_w_out, ev_norm_ffn, ffn_w_gate, ffn_w_up, ffn_w_down, od_norm_mix, od_w_in, conv_w, gate_b_i, gate_b_f, mlstm_norm, mla_q_norm, mla_w_uq, mla_kv_norm, mla_w_ukv, od_w_out, od_norm_ffn, router_w, router_b, moe_w_gate, moe_w_up, moe_w_down, ple_norm, ple_w_gate, ple_w_proj, final_norm):
    B, S, D = x.shape
    depth = p.shape[0]
    assert D == D_MODEL and S % PROJ_TILE == 0 and B % MLSTM_BATCH == 0
    assert MOBA_TOPK <= S // MOBA_BLOCK <= MOBA_MAX_BLOCKS
    h = x
    assert depth % 2 == 0
    for layer in range(depth):
        j = layer // 2
        ple = (p[layer].reshape(B * S, PLE_DIM), ple_norm[layer], ple_w_gate[layer], ple_w_proj[layer])
        if layer % 2 == 0:
            h2d = _even_layer(h, ev_norm_mix[j], ev_w_in[j], pool_w[j], pool_scale[j], ev_w_out[j],
                              ev_norm_ffn[j], ffn_w_gate[j], ffn_w_up[j], ffn_w_down[j])
            h2d = _ple(h2d, *ple)
        else:
            last = layer == depth - 1
            h2d = _odd_layer(h, od_norm_mix[j], od_w_in[j], conv_w[j], gate_b_i[j], gate_b_f[j],
                             mlstm_norm[j], mla_q_norm[j], mla_w_uq[j], mla_kv_norm[j], mla_w_ukv[j],
                             od_w_out[j], od_norm_ffn[j], router_w[j], router_b[j],
                             moe_w_gate[j], moe_w_up[j], moe_w_down[j],
                             ple + ((final_norm,) if last else ()))
        h = h2d.reshape(B, S, D)
    return h
```

```python
import functools
import math

import numpy as np
import jax
import jax.numpy as jnp
from jax import lax
from jax.experimental import pallas as pl
from jax.experimental.pallas import tpu as pltpu

F32 = jnp.float32
BF16 = jnp.bfloat16
HIGHEST = lax.Precision.HIGHEST

D_MODEL = 1024
PLE_DIM = 256
NORM_EPS = 1e-6
NEG_INF = -1e30

MOBA_HEADS = 8
MOBA_HEAD_DIM = 64
MOBA_BLOCK = 256
MOBA_TOPK = 3
POOL_WINDOWS = (2, 4, 8, 16)
POOL_GROUP_DIM = 128
POOL_HALO = 16
MLSTM_HEADS = 4
MLSTM_HEAD_DIM = 128
MLSTM_CHUNK = 128
MLSTM_BATCH = 2
assert MLSTM_CHUNK == 128
CONV_WIDTH = 4
MLA_HEADS = 4
MLA_Q_RANK = 256
MLA_KV_RANK = 128
MLA_NOPE_DIM = 64
MLA_ROPE_DIM = 32
MLA_V_DIM = 128
ROPE_BASE = 10000.0
FFN_DIM = 2816
N_EXPERTS = 8
EXPERT_DIM = 3584
MIX_WIDTH = 512

ATTN_BLOCK = 512
ROW_TILE = 512
PROJ_TILE = 1024
LANES = 128
VMEM_LIMIT = 56 * 1024 * 1024

MISC_ROPE = 0
MISC_I = 32
MISC_F = 36


def _params(sem, vmem=VMEM_LIMIT):
    return pltpu.CompilerParams(dimension_semantics=sem, vmem_limit_bytes=vmem)


def _rms(x, g):
    ms = jnp.mean(x * x, axis=-1, keepdims=True)
    return x * lax.rsqrt(ms + NORM_EPS) * g


def _sigmoid(x):
    return 1.0 / (1.0 + jnp.exp(-x))


def _dot(a, b):
    return jnp.dot(a, b, preferred_element_type=F32)


def _dot_nt(a, b, precision=None):
    return lax.dot_general(a, b, (((1,), (1,)), ((), ())), precision=precision,
                           preferred_element_type=F32)


KAUG_SEL = MOBA_HEAD_DIM
KAUG_POS = KAUG_SEL + 32
MOBA_MAX_BLOCKS = KAUG_POS - KAUG_SEL


def _bf16_terms(x, n):
    out = []
    for _ in range(n):
        bits = np.float32(x).view(np.uint32)
        kept = np.uint32((int(bits) + 0x7FFF + ((int(bits) >> 16) & 1)) & 0xFFFF0000)
        term = float(kept.view(np.float32))
        out.append(term)
        x -= term
    return tuple(out)


LOG2E = math.log2(math.e)
LOG2E_TERMS = _bf16_terms(LOG2E, 3)


def _even_in_kernel(x_ref, g_ref, wn_ref, wqT_ref, wvT_ref, ext_ref,
                    ka_ref, ub_ref, km_ref, kn_ref, qT_ref, vT_ref):
    tm = x_ref.shape[1]
    xn = _rms(x_ref[0], g_ref[...]).astype(BF16)
    n = _dot(xn, wn_ref[...])
    ka = n[:, :MOBA_HEADS * LANES]
    ka_ref[0] = (ka + ext_ref[...].astype(F32)).astype(BF16)
    ub_ref[0] = n[:, MOBA_HEADS * LANES:]
    for j in range(tm // MOBA_BLOCK):
        km_ref[0, j] = jnp.mean(ka[j * MOBA_BLOCK:(j + 1) * MOBA_BLOCK], axis=0, keepdims=True)
    slot = lax.broadcasted_iota(jnp.int32, (MOBA_HEADS * LANES, LANES), 0) // LANES
    head = lax.broadcasted_iota(jnp.int32, (MOBA_HEADS * LANES, LANES), 1)
    kn_ref[0] = _dot((ka * ka).astype(BF16), (slot == head).astype(BF16))
    qT_ref[0] = _dot_nt(wqT_ref[...], xn)
    vT = _dot_nt(wvT_ref[...], xn)
    for j in range(tm // ATTN_BLOCK):
        vT_ref[0, j] = vT[:, j * ATTN_BLOCK:(j + 1) * ATTN_BLOCK].astype(BF16)


def _moba_key_extras(S, slopes):
    pos = np.arange(S)
    blk, off = pos // MOBA_BLOCK, pos % MOBA_BLOCK
    ext = np.zeros((S, MOBA_HEADS, LANES), np.float32)
    ext[pos, :, KAUG_SEL + blk] = 1.0
    for term in range(len(LOG2E_TERMS)):
        ext[:, :, KAUG_POS + 2 * term] = slopes[None, :] * (MOBA_BLOCK * blk)[:, None]
        ext[:, :, KAUG_POS + 2 * term + 1] = slopes[None, :] * off[:, None]
    return jnp.asarray(ext.reshape(S, MOBA_HEADS * LANES), dtype=BF16)


def _even_in(x, g, w_in, slopes):
    B, S, D = x.shape
    tm = PROJ_TILE
    nb = S // MOBA_BLOCK
    wq, wk, wv, wu = (w_in[:, i * MIX_WIDTH:(i + 1) * MIX_WIDTH] for i in range(4))
    wk_slots = jnp.concatenate(
        [wk.reshape(D, MOBA_HEADS, MOBA_HEAD_DIM),
         jnp.zeros((D, MOBA_HEADS, LANES - MOBA_HEAD_DIM), F32)], axis=2).reshape(D, MOBA_HEADS * LANES)
    wn = jnp.concatenate([wk_slots, wu], axis=1).astype(BF16)
    wqT = wq.T.astype(BF16)
    wvT = wv.T.astype(BF16)
    const = lambda b, i: (0, 0)
    return pl.pallas_call(
        _even_in_kernel,
        grid=(B, S // tm),
        in_specs=[
            pl.BlockSpec((1, tm, D), lambda b, i: (b, i, 0)),
            pl.BlockSpec((1, D), const),
            pl.BlockSpec((D, MOBA_HEADS * LANES + MIX_WIDTH), const),
            pl.BlockSpec((MIX_WIDTH, D), const),
            pl.BlockSpec((MIX_WIDTH, D), const),
            pl.BlockSpec((tm, MOBA_HEADS * LANES), lambda b, i: (i, 0)),
        ],
        out_specs=[
            pl.BlockSpec((1, tm, MOBA_HEADS * LANES), lambda b, i: (b, i, 0)),
            pl.BlockSpec((1, tm, MIX_WIDTH), lambda b, i: (b, i, 0)),
            pl.BlockSpec((1, tm // MOBA_BLOCK, 1, MOBA_HEADS * LANES), lambda b, i: (b, i, 0, 0)),
            pl.BlockSpec((1, tm, LANES), lambda b, i: (b, i, 0)),
            pl.BlockSpec((1, MIX_WIDTH, tm), lambda b, i: (b, 0, i)),
            pl.BlockSpec((1, tm // ATTN_BLOCK, MIX_WIDTH, ATTN_BLOCK), lambda b, i: (b, i, 0, 0)),
        ],
        out_shape=[
            jax.ShapeDtypeStruct((B, S, MOBA_HEADS * LANES), BF16),
            jax.ShapeDtypeStruct((B, S, MIX_WIDTH), F32),
            jax.ShapeDtypeStruct((B, nb, 1, MOBA_HEADS * LANES), F32),
            jax.ShapeDtypeStruct((B, S, LANES), F32),
            jax.ShapeDtypeStruct((B, MIX_WIDTH, S), F32),
            jax.ShapeDtypeStruct((B, S // ATTN_BLOCK, MIX_WIDTH, ATTN_BLOCK), BF16),
        ],
        compiler_params=_params(("parallel", "parallel")),
        name="even_in",
    )(x, g.reshape(1, D), wn, wqT, wvT, _moba_key_extras(S, slopes))


def _moba_gate_kernel(km_ref, qT_ref, qa_ref, qn_ref):
    i = pl.program_id(1)
    nb = km_ref.shape[2]
    tq = qT_ref.shape[2]
    row = lax.broadcasted_iota(jnp.int32, (nb, tq), 0)
    own = (i * tq + lax.broadcasted_iota(jnp.int32, (nb, tq), 1)) // MOBA_BLOCK
    past = row < own
    tail_row = lax.broadcasted_iota(jnp.int32, (LANES - KAUG_POS, tq), 0)
    tail = jnp.zeros(tail_row.shape, F32)
    for term, value in enumerate(LOG2E_TERMS):
        tail = jnp.where(tail_row // 2 == term, F32(value), tail)
    tail = tail.astype(BF16)
    pad = jnp.zeros((MOBA_MAX_BLOCKS - nb, tq), BF16) if nb < MOBA_MAX_BLOCKS else None
    for h in range(MOBA_HEADS):
        q_h = qT_ref[0, h * MOBA_HEAD_DIM:(h + 1) * MOBA_HEAD_DIM, :]
        qn_ref[0, h:h + 1, :] = jnp.sum(q_h * q_h, axis=0, keepdims=True)
        gate = jnp.dot(km_ref[0, h], q_h, precision=HIGHEST, preferred_element_type=F32)
        gate = jnp.where(past, gate, NEG_INF)
        chosen = jnp.zeros(gate.shape, F32)
        for _ in range(MOBA_TOPK):
            mx = jnp.max(gate, axis=0, keepdims=True)
            first = jnp.min(jnp.where(gate == mx, row, nb), axis=0, keepdims=True)
            pick = row == first
            chosen = jnp.where(pick, 1.0, chosen)
            gate = jnp.where(pick, -jnp.inf, gate)
        keep = jnp.where(past, chosen, (row == own).astype(F32))
        sel = jnp.where(keep > 0.0, 0.0, NEG_INF).astype(BF16)
        base = h * LANES
        qa_ref[0, base:base + KAUG_SEL] = (q_h * (MOBA_HEAD_DIM ** -0.5 * LOG2E)).astype(BF16)
        qa_ref[0, base + KAUG_SEL:base + KAUG_SEL + nb] = sel
        if pad is not None:
            qa_ref[0, base + KAUG_SEL + nb:base + KAUG_POS] = pad
        qa_ref[0, base + KAUG_POS:base + LANES] = tail


def _moba_gate(kmean, qT):
    B, H, nb, dh = kmean.shape
    S = qT.shape[2]
    tq = ATTN_BLOCK
    return pl.pallas_call(
        _moba_gate_kernel,
        grid=(B, S // tq),
        in_specs=[
            pl.BlockSpec((1, H, nb, dh), lambda b, i: (b, 0, 0, 0)),
            pl.BlockSpec((1, H * dh, tq), lambda b, i: (b, 0, i)),
        ],
        out_specs=[pl.BlockSpec((1, H * LANES, tq), lambda b, i: (b, 0, i)),
                   pl.BlockSpec((1, H, tq), lambda b, i: (b, 0, i))],
        out_shape=[jax.ShapeDtypeStruct((B, H * LANES, S), BF16), jax.ShapeDtypeStruct((B, H, S), F32)],
        compiler_params=_params(("parallel", "parallel")),
        name="moba_gate",
    )(kmean, qT)


HEADS_PER_STEP = 4
SUM_ROWS = 16


def _attn_kernel(lo_ref, q_ref, k_ref, v_ref, o_ref, sa_ref, sb_ref, m_ref, acc_ref):
    i = pl.program_id(2)
    lo = lo_ref[(pl.program_id(0) * pl.num_programs(1) + pl.program_id(1)) * pl.num_programs(2) + i]
    n_past = i - lo
    tq = q_ref.shape[2]
    tk = ATTN_BLOCK
    hp = HEADS_PER_STEP
    dv = v_ref.shape[2] // hp
    m_ref[...] = jnp.full(m_ref.shape, NEG_INF, F32)
    acc_ref[...] = jnp.zeros(acc_ref.shape, F32)
    ones_rows = jnp.ones((SUM_ROWS, tk), BF16)

    def scores(kvt, s_ref, diag):
        start = pl.multiple_of(kvt * tk, tk)
        k_tile = k_ref[0, pl.ds(start, tk), :]
        for g in range(hp):
            s = _dot(k_tile[:, g * LANES:(g + 1) * LANES], q_ref[0, g * LANES:(g + 1) * LANES, :])
            if diag:
                key = lax.broadcasted_iota(jnp.int32, (tk, tq), 0)
                qry = lax.broadcasted_iota(jnp.int32, (tk, tq), 1)
                s = jnp.where(key <= qry, s, NEG_INF)
            s_ref[g] = s

    def consume(kvt, s_ref):
        v_tile = v_ref[0, kvt]
        for g in range(hp):
            s = s_ref[g]
            m_run = m_ref[g]
            m_new = jnp.maximum(m_run, jnp.max(s, axis=0, keepdims=True))
            p = jnp.exp2(s - m_new).astype(BF16)
            v_aug = jnp.concatenate([v_tile[g * dv:(g + 1) * dv], ones_rows], axis=0)
            acc_ref[g] = jnp.exp2(m_run - m_new) * acc_ref[g] + _dot(v_aug, p)
            m_ref[g] = m_new

    tile_at = lambda t: jnp.where(t == 0, i, lo + t - 1)
    scores(i, sa_ref, True)

    def pair(p, carry):
        t = 2 * p
        scores(tile_at(t + 1), sb_ref, False)
        consume(tile_at(t), sa_ref)
        scores(tile_at(t + 2), sa_ref, False)
        consume(tile_at(t + 1), sb_ref)
        return carry

    lax.fori_loop(0, n_past // 2, pair, 0)
    last = 2 * (n_past // 2)

    @pl.when(n_past % 2 == 1)
    def _():
        scores(tile_at(last + 1), sb_ref, False)
        consume(tile_at(last), sa_ref)
        consume(tile_at(last + 1), sb_ref)

    @pl.when(n_past % 2 == 0)
    def _():
        consume(tile_at(last), sa_ref)

    for g in range(hp):
        o_ref[0, g * dv:(g + 1) * dv, :] = (acc_ref[g, :dv] / acc_ref[g, dv:dv + 1]).astype(o_ref.dtype)


def _attention(qT, k, vT, dv, heads, name, first_tile=None):
    B, _, S = qT.shape
    tq = ATTN_BLOCK
    nt = S // ATTN_BLOCK
    hp = HEADS_PER_STEP
    if first_tile is None:
        first_tile = jnp.zeros((B, heads // hp, nt), jnp.int32)
    return pl.pallas_call(
        _attn_kernel,
        grid_spec=pltpu.PrefetchScalarGridSpec(
            num_scalar_prefetch=1,
            grid=(B, heads // hp, S // tq),
            in_specs=[
                pl.BlockSpec((1, hp * LANES, tq), lambda b, h, i, lo: (b, h, i)),
                pl.BlockSpec((1, S, hp * LANES), lambda b, h, i, lo: (b, 0, h)),
                pl.BlockSpec((1, nt, hp * dv, ATTN_BLOCK), lambda b, h, i, lo: (b, 0, h, 0)),
            ],
            out_specs=pl.BlockSpec((1, hp * dv, tq), lambda b, h, i, lo: (b, h, i)),
            scratch_shapes=[pltpu.VMEM((hp, ATTN_BLOCK, tq), F32), pltpu.VMEM((hp, ATTN_BLOCK, tq), F32),
                            pltpu.VMEM((hp, 1, tq), F32), pltpu.VMEM((hp, dv + SUM_ROWS, tq), F32)],
        ),
        out_shape=jax.ShapeDtypeStruct((B, heads * dv, S), BF16),
        compiler_params=_params(("parallel", "parallel", "arbitrary")),
        name=name,
    )(first_tile.reshape(-1), qT, k, vT)


def _pool_kernel(x_ref, halo_ref, w_ref, sc_ref, o_ref, xs_ref):
    i = pl.program_id(1)
    tm = x_ref.shape[1]
    x = x_ref[0]
    xs_ref[0:POOL_HALO] = jnp.where(i > 0, halo_ref[0], 0.0)
    xs_ref[POOL_HALO:POOL_HALO + tm] = x
    t = i * tm + lax.broadcasted_iota(jnp.int32, (tm, 1), 0)
    outs = []
    for g, win in enumerate(POOL_WINDOWS):
        lo, hi = g * POOL_GROUP_DIM, (g + 1) * POOL_GROUP_DIM
        xg = x[:, lo:hi]
        acc = xg
        for d in range(1, win):
            acc = acc + xs_ref[POOL_HALO - d:POOL_HALO - d + tm, lo:hi]
        count = jnp.minimum(t + 1, win).astype(F32)
        outs.append(_dot((acc / count - xg).astype(BF16), w_ref[g]))
    o_ref[0] = (jnp.concatenate(outs, axis=1) * sc_ref[...]).astype(o_ref.dtype)


def _pool(ub, pool_w, pool_scale):
    B, S, W = ub.shape
    tm = PROJ_TILE
    per = tm // POOL_HALO
    return pl.pallas_call(
        _pool_kernel,
        grid=(B, S // tm),
        in_specs=[
            pl.BlockSpec((1, tm, W), lambda b, i: (b, i, 0)),
            pl.BlockSpec((1, POOL_HALO, W), lambda b, i: (b, jnp.maximum(i * per - 1, 0), 0)),
            pl.BlockSpec(pool_w.shape, lambda b, i: (0, 0, 0)),
            pl.BlockSpec((1, W), lambda b, i: (0, 0)),
        ],
        out_specs=pl.BlockSpec((1, tm, W), lambda b, i: (b, i, 0)),
        out_shape=jax.ShapeDtypeStruct((B, S, W), BF16),
        scratch_shapes=[pltpu.VMEM((POOL_HALO + tm, W), F32)],
        compiler_params=_params(("parallel", "parallel")),
        name="pool",
    )(ub, ub, pool_w.astype(BF16), pool_scale.reshape(1, W))


def _mix_out_kernel(h_ref, aT_ref, b_ref, w_ref, o_ref, *, a_first):
    a = aT_ref[0].astype(F32).T.astype(BF16)
    b = b_ref[0].astype(BF16)
    lo, hi = (a, b) if a_first else (b, a)
    y = _dot(lo, w_ref[:MIX_WIDTH]) + _dot(hi, w_ref[MIX_WIDTH:])
    o_ref[0] = h_ref[0] + y


def _mix_out(h, aT, b, w_out, a_first):
    B, S, D = h.shape
    tm = PROJ_TILE
    return pl.pallas_call(
        functools.partial(_mix_out_kernel, a_first=a_first),
        grid=(B, S // tm),
        in_specs=[
            pl.BlockSpec((1, tm, D), lambda b_, i: (b_, i, 0)),
            pl.BlockSpec((1, MIX_WIDTH, tm), lambda b_, i: (b_, 0, i)),
            pl.BlockSpec((1, tm, MIX_WIDTH), lambda b_, i: (b_, i, 0)),
            pl.BlockSpec((2 * MIX_WIDTH, D), lambda b_, i: (0, 0)),
        ],
        out_specs=pl.BlockSpec((1, tm, D), lambda b_, i: (b_, i, 0)),
        out_shape=jax.ShapeDtypeStruct((B, S, D), F32),
        compiler_params=_params(("parallel", "parallel")),
        name="mix_out",
    )(h, aT, b, w_out.astype(BF16))


def _swiglu_step(xn, wg, wu, wd):
    gt = _dot(xn, wg)
    up = _dot(xn, wu)
    return _dot((gt * _sigmoid(gt) * up).astype(BF16), wd)


def _ffn_kernel(h_ref, g_ref, wg_ref, wu_ref, wd_ref, o_ref):
    h = h_ref[...]
    xn = _rms(h, g_ref[...]).astype(BF16)
    o_ref[...] = h + _swiglu_step(xn, wg_ref[...], wu_ref[...], wd_ref[...])


def _ffn(h2d, g, wg, wu, wd, tm=ROW_TILE):
    T, D = h2d.shape
    F = wg.shape[1]
    resident = lambda shape: pl.BlockSpec(shape, lambda i: (0, 0), pipeline_mode=pl.Buffered(1))
    return pl.pallas_call(
        _ffn_kernel,
        grid=(T // tm,),
        in_specs=[
            pl.BlockSpec((tm, D), lambda i: (i, 0)),
            pl.BlockSpec((1, D), lambda i: (0, 0)),
            resident((D, F)),
            resident((D, F)),
            resident((F, D)),
        ],
        out_specs=pl.BlockSpec((tm, D), lambda i: (i, 0)),
        out_shape=jax.ShapeDtypeStruct((T, D), F32),
        compiler_params=_params(("parallel",)),
        name="ffn",
    )(h2d, g.reshape(1, D), wg, wu, wd)


ROUTE_E0, ROUTE_E1, ROUTE_W0, ROUTE_W1, ROUTE_R0, ROUTE_R1 = range(6)
MOE_TILE = 512
MOE_TF = EXPERT_DIM // 2


def _lane_pick(tile, lane, idx):
    return jnp.sum(jnp.where(lane == idx, tile, 0.0), axis=1, keepdims=True)


ROUTER_ROWS = 16


def _router_kernel(h_ref, g_ref, wT_ref, b_ref, route_ref, cnt_ref):
    tm = h_ref.shape[0]

    @pl.when(pl.program_id(0) == 0)
    def _():
        cnt_ref[...] = jnp.zeros_like(cnt_ref)

    xn = _rms(h_ref[...], g_ref[...])
    logits = _dot_nt(wT_ref[...], xn, precision=HIGHEST) + b_ref[...]
    row = lax.broadcasted_iota(jnp.int32, logits.shape, 0)
    logits = jnp.where(row < N_EXPERTS, logits, -jnp.inf)
    v0 = jnp.max(logits, axis=0, keepdims=True)
    i0 = jnp.min(jnp.where(logits == v0, row, ROUTER_ROWS), axis=0, keepdims=True)
    rest = jnp.where(row == i0, -jnp.inf, logits)
    v1 = jnp.max(rest, axis=0, keepdims=True)
    i1 = jnp.min(jnp.where(rest == v1, row, ROUTER_ROWS), axis=0, keepdims=True)
    e1 = jnp.exp(v1 - v0)
    w0 = 1.0 / (1.0 + e1)
    sel = (row == i0).astype(F32) + (row == i1).astype(F32)
    earlier = (lax.broadcasted_iota(jnp.int32, (tm, tm), 0)
               < lax.broadcasted_iota(jnp.int32, (tm, tm), 1))
    counts = cnt_ref[:, 0:1]
    rank = _dot(sel.astype(BF16), earlier.astype(BF16)) + counts
    cnt_ref[...] = jnp.broadcast_to(counts + jnp.sum(sel, axis=1, keepdims=True), cnt_ref.shape)
    pick = lambda idx: jnp.sum(jnp.where(row == idx, rank, 0.0), axis=0, keepdims=True)
    rows = [None] * 6
    rows[ROUTE_E0], rows[ROUTE_E1] = i0.astype(F32), i1.astype(F32)
    rows[ROUTE_W0], rows[ROUTE_W1] = w0, e1 * w0
    rows[ROUTE_R0], rows[ROUTE_R1] = pick(i0), pick(i1)
    routeT = jnp.concatenate(rows + [jnp.zeros((LANES - len(rows), tm), F32)], axis=0)
    route_ref[...] = routeT.T


def _router(h2d, g, router_w, router_b):
    T, D = h2d.shape
    tm = ROW_TILE
    wT = jnp.zeros((ROUTER_ROWS, D), F32).at[:N_EXPERTS].set(router_w.T)
    b = jnp.zeros((ROUTER_ROWS, 1), F32).at[:N_EXPERTS, 0].set(router_b)
    return pl.pallas_call(
        _router_kernel,
        grid=(T // tm,),
        in_specs=[
            pl.BlockSpec((tm, D), lambda i: (i, 0)),
            pl.BlockSpec((1, D), lambda i: (0, 0)),
            pl.BlockSpec((ROUTER_ROWS, D), lambda i: (0, 0)),
            pl.BlockSpec((ROUTER_ROWS, 1), lambda i: (0, 0)),
        ],
        out_specs=[pl.BlockSpec((tm, LANES), lambda i: (i, 0)),
                   pl.BlockSpec((ROUTER_ROWS, LANES), lambda i: (0, 0))],
        out_shape=[jax.ShapeDtypeStruct((T, LANES), F32), jax.ShapeDtypeStruct((ROUTER_ROWS, LANES), F32)],
        compiler_params=_params(("arbitrary",)),
        name="router",
    )(h2d, g.reshape(1, D), wT, b)


SUBLANES = 8
ROW_LANES = D_MODEL // SUBLANES
assert ROW_LANES == LANES


def _row_tile(ref, r):
    return ref.at[pl.ds(pl.multiple_of(r * SUBLANES, SUBLANES), SUBLANES), :]


def _to_row_tiles(ref, x):
    tm = x.shape[0]
    for s in range(SUBLANES):
        ref[pl.ds(s, tm, stride=SUBLANES), :] = x[:, s * LANES:(s + 1) * LANES]


def _from_row_tiles(ref):
    tm = ref.shape[0] // SUBLANES
    return jnp.concatenate([ref[pl.ds(s, tm, stride=SUBLANES), :] for s in range(SUBLANES)], axis=1)


def _row_copies(pos_ref, base, r, src_of, dst_of, sem):
    return [pltpu.make_async_copy(src_of(k, pos_ref[base + 2 * r + k]),
                                  dst_of(k, pos_ref[base + 2 * r + k]), sem) for k in range(2)]


def _start_rows(tm, make):
    def issue(r, c):
        for k, cp in enumerate(make(r)):
            cp.start(priority=k)
        return c

    lax.fori_loop(0, tm, issue, 0, unroll=8)


def _wait_rows(tm, make):
    def drain(r, c):
        for cp in make(r):
            cp.wait()
        return c

    lax.fori_loop(0, tm, drain, 0, unroll=8)


def _dispatch_kernel(pos_ref, pad_ref, h_ref, g_ref, xs_ref, xn_ref, zero_ref, sem, zsem):
    tm = h_ref.shape[0]
    i = pl.program_id(0)
    slot = i % 2

    @pl.when(i == 0)
    def _():
        zero_ref[...] = jnp.zeros_like(zero_ref)
        gaps = pad_ref.shape[0] // 2
        for e in range(gaps):
            first, count = pad_ref[e], pad_ref[gaps + e]
            fill = lambda r, first=first: pltpu.make_async_copy(zero_ref, _row_tile(xs_ref, first + r), zsem)

            def start(r, c, fill=fill):
                fill(r).start()
                return c

            def wait(r, c, fill=fill):
                fill(r).wait()
                return c

            lax.fori_loop(0, count, start, 0)
            lax.fori_loop(0, count, wait, 0)

    def scatters(tile, buf):
        return lambda r: _row_copies(
            pos_ref, tile * (2 * tm), r, lambda k, p: _row_tile(xn_ref.at[buf], r),
            lambda k, p: _row_tile(xs_ref, p), sem.at[buf])

    _to_row_tiles(xn_ref.at[slot], _rms(h_ref[...], g_ref[...]))
    _start_rows(tm, scatters(i, slot))

    @pl.when(i > 0)
    def _():
        _wait_rows(tm, scatters(i - 1, 1 - slot))

    @pl.when(i == pl.num_programs(0) - 1)
    def _():
        _wait_rows(tm, scatters(i, slot))


def _dispatch(pos, pad, h2d, g, n_rows):
    T, D = h2d.shape
    tm = ROW_TILE
    return pl.pallas_call(
        _dispatch_kernel,
        grid_spec=pltpu.PrefetchScalarGridSpec(
            num_scalar_prefetch=2,
            grid=(T // tm,),
            in_specs=[pl.BlockSpec((tm, D), lambda i, pos, pad: (i, 0)),
                      pl.BlockSpec((1, D), lambda i, pos, pad: (0, 0))],
            out_specs=pl.BlockSpec(memory_space=pl.ANY),
            scratch_shapes=[pltpu.VMEM((2, tm * SUBLANES, ROW_LANES), F32),
                            pltpu.VMEM((SUBLANES, ROW_LANES), F32),
                            pltpu.SemaphoreType.DMA((2,)), pltpu.SemaphoreType.DMA(())],
        ),
        out_shape=jax.ShapeDtypeStruct((n_rows * SUBLANES, ROW_LANES), F32),
        compiler_params=_params(("arbitrary",)),
        name="moe_dispatch",
    )(pos, pad, h2d, g.reshape(1, D))


def _moe_ffn_kernel(te_ref, nv_ref, x_ref, wg_ref, wu_ref, wd_ref, o_ref, xb_ref, acc_ref):
    j = pl.program_id(0)
    f = pl.program_id(1)
    valid = j < nv_ref[0]
    last = pl.num_programs(1) - 1
    assert MOE_TF * 2 == EXPERT_DIM

    @pl.when(valid & (f == 0))
    def _():
        x = _from_row_tiles(x_ref).astype(BF16)
        xb_ref[...] = x
        acc_ref[...] = _swiglu_step(x, wg_ref[0], wu_ref[0], wd_ref[0])

    @pl.when(valid & (f == last))
    def _():
        _to_row_tiles(o_ref, acc_ref[...] + _swiglu_step(xb_ref[...], wg_ref[0], wu_ref[0], wd_ref[0]))

    @pl.when(jnp.logical_not(valid) & (f == last))
    def _():
        o_ref[...] = jnp.zeros_like(o_ref)


def _moe_ffn(tile_expert, n_valid, xs, wg, wu, wd):
    N, D = xs.shape[0] // SUBLANES, D_MODEL
    tm, tf = MOE_TILE, MOE_TF
    F = wg.shape[2]
    return pl.pallas_call(
        _moe_ffn_kernel,
        grid_spec=pltpu.PrefetchScalarGridSpec(
            num_scalar_prefetch=2,
            grid=(N // tm, F // tf),
            in_specs=[pl.BlockSpec((tm * SUBLANES, ROW_LANES), lambda j, f, te, nv: (j, 0)),
                      pl.BlockSpec((1, D, tf), lambda j, f, te, nv: (te[j], 0, f)),
                      pl.BlockSpec((1, D, tf), lambda j, f, te, nv: (te[j], 0, f)),
                      pl.BlockSpec((1, tf, D), lambda j, f, te, nv: (te[j], f, 0))],
            out_specs=pl.BlockSpec((tm * SUBLANES, ROW_LANES), lambda j, f, te, nv: (j, 0)),
            scratch_shapes=[pltpu.VMEM((tm, D), BF16), pltpu.VMEM((tm, D), F32)],
        ),
        out_shape=jax.ShapeDtypeStruct(xs.shape, F32),
        compiler_params=_params(("arbitrary", "arbitrary")),
        name="moe_ffn",
    )(tile_expert, n_valid, xs, wg, wu, wd)


def _combine_kernel(*refs, final):
    if final:
        pos_ref, h_ref, route_ref, p_ref, g_ref, wg_ref, wp_ref, fg_ref, ys_ref, o_ref, y_ref, sem = refs
    else:
        pos_ref, h_ref, route_ref, p_ref, g_ref, wg_ref, wp_ref, ys_ref, o_ref, y_ref, sem = refs
    tm = h_ref.shape[0]
    i = pl.program_id(0)
    slot = i % 2

    def gathers(tile, buf):
        return lambda r: _row_copies(
            pos_ref, tile * (2 * tm), r, lambda k, p: _row_tile(ys_ref, p),
            lambda k, p: _row_tile(y_ref.at[buf, k], r), sem.at[buf])

    @pl.when(i == 0)
    def _():
        _start_rows(tm, gathers(0, 0))

    @pl.when(i + 1 < pl.num_programs(0))
    def _():
        _start_rows(tm, gathers(i + 1, 1 - slot))

    _wait_rows(tm, gathers(i, slot))
    route = route_ref[...]
    lane = lax.broadcasted_iota(jnp.int32, route.shape, 1)
    h = (h_ref[...] + _lane_pick(route, lane, ROUTE_W0) * _from_row_tiles(y_ref.at[slot, 0])
         + _lane_pick(route, lane, ROUTE_W1) * _from_row_tiles(y_ref.at[slot, 1]))
    o_ref[...] = _ple_math(h, p_ref[...], g_ref[...], wg_ref[...], wp_ref[...],
                           fg_ref[...] if final else None)


def _combine(pos, h2d, route, ys, p2d, ple_g, ple_w_gate, ple_w_proj, final_g=None):
    T, D = h2d.shape
    tm = ROW_TILE
    const = lambda i, pos: (0, 0)
    final = final_g is not None
    ple_specs = [pl.BlockSpec((tm, PLE_DIM), lambda i, pos: (i, 0)), pl.BlockSpec((1, D), const),
                 pl.BlockSpec((D, D), const), pl.BlockSpec((PLE_DIM, D), const)]
    ple_args = [p2d, ple_g.reshape(1, D), ple_w_gate.astype(BF16), ple_w_proj.astype(BF16)]
    if final:
        ple_specs.append(pl.BlockSpec((1, D), const))
        ple_args.append(final_g.reshape(1, D))
    return pl.pallas_call(
        functools.partial(_combine_kernel, final=final),
        grid_spec=pltpu.PrefetchScalarGridSpec(
            num_scalar_prefetch=1,
            grid=(T // tm,),
            in_specs=[pl.BlockSpec((tm, D), lambda i, pos: (i, 0)),
                      pl.BlockSpec((tm, LANES), lambda i, pos: (i, 0)),
                      *ple_specs,
                      pl.BlockSpec(memory_space=pl.ANY)],
            out_specs=pl.BlockSpec((tm, D), lambda i, pos: (i, 0)),
            scratch_shapes=[pltpu.VMEM((2, 2, tm * SUBLANES, ROW_LANES), F32),
                            pltpu.SemaphoreType.DMA((2,))],
        ),
        out_shape=jax.ShapeDtypeStruct((T, D), F32),
        compiler_params=_params(("arbitrary",)),
        name="moe_combine",
    )(pos, h2d, route, *ple_args, ys)


def _moe(h2d, g, router_w, router_b, wg, wu, wd, ple):
    T, D = h2d.shape
    tm = MOE_TILE
    route, counts = _router(h2d, g, router_w, router_b)
    cnt = counts[:N_EXPERTS, 0].astype(jnp.int32)
    padded = (cnt + tm - 1) // tm * tm
    ends = jnp.cumsum(padded)
    start = ends - padded
    e01 = route[:, ROUTE_E0:ROUTE_E1 + 1].astype(jnp.int32)
    r01 = route[:, ROUTE_R0:ROUTE_R1 + 1].astype(jnp.int32)
    pos = (start[e01] + r01).reshape(2 * T)
    n_rows = 2 * T + N_EXPERTS * tm
    tile_row = jnp.arange(n_rows // tm, dtype=jnp.int32) * tm
    tile_expert = jnp.minimum(jnp.sum(tile_row[:, None] >= ends[None, :], axis=1), N_EXPERTS - 1).astype(jnp.int32)
    n_valid = (ends[-1:] // tm).astype(jnp.int32)
    pad = jnp.concatenate([start + cnt, ends[-1:], padded - cnt, n_rows - ends[-1:]]).astype(jnp.int32)
    xs = _dispatch(pos, pad, h2d, g, n_rows)
    ys = _moe_ffn(tile_expert, n_valid, xs, wg, wu, wd)
    return _combine(pos, h2d, route, ys, *ple)


def _ple_math(h, p, g, w_gate, w_proj, final_g=None):
    gate = _sigmoid(_dot(_rms(h, g).astype(BF16), w_gate))
    out = h + gate * _dot(p.astype(BF16), w_proj)
    return out if final_g is None else _rms(out, final_g)


def _ple_kernel(h_ref, p_ref, g_ref, wg_ref, wp_ref, o_ref):
    o_ref[...] = _ple_math(h_ref[...], p_ref[...], g_ref[...], wg_ref[...], wp_ref[...])


def _ple(h2d, p2d, g, w_gate, w_proj):
    T, D = h2d.shape
    tm = PROJ_TILE
    return pl.pallas_call(
        _ple_kernel,
        grid=(T // tm,),
        in_specs=[
            pl.BlockSpec((tm, D), lambda i: (i, 0)),
            pl.BlockSpec((tm, PLE_DIM), lambda i: (i, 0)),
            pl.BlockSpec((1, D), lambda i: (0, 0)),
            pl.BlockSpec((D, D), lambda i: (0, 0)),
            pl.BlockSpec((PLE_DIM, D), lambda i: (0, 0)),
        ],
        out_specs=pl.BlockSpec((tm, D), lambda i: (i, 0)),
        out_shape=jax.ShapeDtypeStruct((T, D), F32),
        compiler_params=_params(("parallel",)),
        name="ple",
    )(h2d, p2d, g.reshape(1, D), w_gate.astype(BF16), w_proj.astype(BF16))


ODD_MAIN = 4 * MIX_WIDTH
ODD_COLS = ODD_MAIN + MLA_Q_RANK + MLA_KV_RANK + 2 * LANES
MLA_QK_SCALE = (MLA_NOPE_DIM + MLA_ROPE_DIM) ** -0.5 * LOG2E
ROPE_HALF = MLA_ROPE_DIM // 2


def _odd_in_kernel(x_ref, g_ref, wn_ref, qn_ref, wuqT_ref, kvn_ref, wk2_ref, wvT_ref,
                   cosT_ref, sinT_ref, cc_ref, ss_ref,
                   qk_ref, vc_ref, op_ref, misc_ref, mq_ref, mk_ref, mv_ref):
    tm = x_ref.shape[1]
    xn = _rms(x_ref[0], g_ref[...]).astype(BF16)
    u = _dot(xn, wn_ref[...])
    qk_ref[0] = u[:, :2 * MIX_WIDTH]
    vc_ref[0] = u[:, 2 * MIX_WIDTH:3 * MIX_WIDTH].astype(BF16)
    op_ref[0] = u[:, 3 * MIX_WIDTH:ODD_MAIN]
    c0 = ODD_MAIN
    c_q = u[:, c0:c0 + MLA_Q_RANK]
    c0 += MLA_Q_RANK
    c_kv = u[:, c0:c0 + MLA_KV_RANK]
    c0 += MLA_KV_RANK
    misc = u[:, c0:c0 + LANES]
    misc_sw = u[:, c0 + LANES:c0 + 2 * LANES]
    misc_ref[0] = misc
    cqn = _rms(c_q, qn_ref[...]).astype(BF16)
    qT = _dot_nt(wuqT_ref[...], cqn)
    cosT = cosT_ref[...]
    sinT = sinT_ref[...]
    for h in range(MLA_HEADS):
        r = h * LANES
        mq_ref[0, r:r + MLA_NOPE_DIM] = (qT[r:r + MLA_NOPE_DIM] * MLA_QK_SCALE).astype(BF16)
        x1 = qT[r + MLA_NOPE_DIM:r + MLA_NOPE_DIM + ROPE_HALF]
        x2 = qT[r + MLA_NOPE_DIM + ROPE_HALF:r + MLA_NOPE_DIM + MLA_ROPE_DIM]
        mq_ref[0, r + MLA_NOPE_DIM:r + MLA_NOPE_DIM + ROPE_HALF] = (
            (x1 * cosT - x2 * sinT) * MLA_QK_SCALE).astype(BF16)
        mq_ref[0, r + MLA_NOPE_DIM + ROPE_HALF:r + MLA_NOPE_DIM + MLA_ROPE_DIM] = (
            (x1 * sinT + x2 * cosT) * MLA_QK_SCALE).astype(BF16)
        mq_ref[0, r + MLA_NOPE_DIM + MLA_ROPE_DIM:r + LANES] = jnp.zeros(
            (LANES - MLA_NOPE_DIM - MLA_ROPE_DIM, tm), BF16)
    ckvn = _rms(c_kv, kvn_ref[...]).astype(BF16)
    k_rot = (misc * cc_ref[...] + misc_sw * ss_ref[...]).astype(BF16)
    mk_ref[0] = _dot(jnp.concatenate([ckvn, k_rot], axis=1), wk2_ref[...]).astype(BF16)
    vT = _dot_nt(wvT_ref[...], ckvn)
    for j in range(tm // ATTN_BLOCK):
        mv_ref[0, j] = vT[:, j * ATTN_BLOCK:(j + 1) * ATTN_BLOCK].astype(BF16)


def _rope_tables(S):
    inv_freq = ROPE_BASE ** (-jnp.arange(ROPE_HALF, dtype=F32) / ROPE_HALF)
    ang = jnp.arange(S, dtype=F32)[:, None] * inv_freq[None, :]
    cos, sin = jnp.cos(ang), jnp.sin(ang)
    pad = jnp.zeros((S, LANES - MLA_ROPE_DIM), F32)
    cc = jnp.concatenate([cos, cos, pad], axis=1)
    ss = jnp.concatenate([-sin, sin, pad], axis=1)
    return cos.T, sin.T, cc, ss


def _odd_in(x, g, w_in, q_norm, w_uq, kv_norm, w_ukv):
    B, S, D = x.shape
    tm = PROJ_TILE
    cuts = np.cumsum([MIX_WIDTH] * 4 + [MLSTM_HEADS, MLSTM_HEADS, MLA_Q_RANK, MLA_KV_RANK]).tolist()
    w_main = w_in[:, :cuts[3]]
    w_i = w_in[:, cuts[3]:cuts[4]]
    w_f = w_in[:, cuts[4]:cuts[5]]
    w_cq = w_in[:, cuts[5]:cuts[6]]
    w_ckv = w_in[:, cuts[6]:cuts[7]]
    w_kr = w_in[:, cuts[7]:]
    w_kr_sw = jnp.concatenate([w_kr[:, ROPE_HALF:], w_kr[:, :ROPE_HALF]], axis=1)
    zpad = lambda n: jnp.zeros((D, n), F32)
    w_misc = jnp.concatenate([w_kr, w_i, w_f, zpad(LANES - MLA_ROPE_DIM - 2 * MLSTM_HEADS)], axis=1)
    w_misc_sw = jnp.concatenate([w_kr_sw, zpad(LANES - MLA_ROPE_DIM)], axis=1)
    wn = jnp.concatenate([w_main, w_cq, w_ckv, w_misc, w_misc_sw], axis=1).astype(BF16)
    qd = MLA_NOPE_DIM + MLA_ROPE_DIM
    w_uq_h = w_uq.reshape(MLA_Q_RANK, MLA_HEADS, qd)
    w_uq_h = jnp.concatenate([w_uq_h, jnp.zeros((MLA_Q_RANK, MLA_HEADS, LANES - qd), F32)], axis=2)
    wuqT = w_uq_h.reshape(MLA_Q_RANK, MLA_HEADS * LANES).T.astype(BF16)
    w_ukv_h = w_ukv.reshape(MLA_KV_RANK, MLA_HEADS, MLA_NOPE_DIM + MLA_V_DIM)
    w_k = jnp.concatenate([w_ukv_h[:, :, :MLA_NOPE_DIM],
                           jnp.zeros((MLA_KV_RANK, MLA_HEADS, LANES - MLA_NOPE_DIM), F32)], axis=2)
    place = jnp.zeros((LANES, MLA_HEADS, LANES), F32)
    eye = jnp.eye(MLA_ROPE_DIM, dtype=F32)
    place = place.at[:MLA_ROPE_DIM, :, MLA_NOPE_DIM:MLA_NOPE_DIM + MLA_ROPE_DIM].set(
        jnp.broadcast_to(eye[:, None, :], (MLA_ROPE_DIM, MLA_HEADS, MLA_ROPE_DIM)))
    wk2 = jnp.concatenate([w_k, place], axis=0).reshape(MLA_KV_RANK + LANES, MLA_HEADS * LANES).astype(BF16)
    wvT = w_ukv_h[:, :, MLA_NOPE_DIM:].reshape(MLA_KV_RANK, MLA_HEADS * MLA_V_DIM).T.astype(BF16)
    cosT, sinT, cc, ss = _rope_tables(S)
    row = lambda b, i: (b, i, 0)
    const = lambda b, i: (0, 0)
    nb = S // ATTN_BLOCK
    return pl.pallas_call(
        _odd_in_kernel,
        grid=(B, S // tm),
        in_specs=[
            pl.BlockSpec((1, tm, D), row),
            pl.BlockSpec((1, D), const),
            pl.BlockSpec((D, ODD_COLS), const),
            pl.BlockSpec((1, MLA_Q_RANK), const),
            pl.BlockSpec((MLA_HEADS * LANES, MLA_Q_RANK), const),
            pl.BlockSpec((1, MLA_KV_RANK), const),
            pl.BlockSpec((MLA_KV_RANK + LANES, MLA_HEADS * LANES), const),
            pl.BlockSpec((MLA_HEADS * MLA_V_DIM, MLA_KV_RANK), const),
            pl.BlockSpec((ROPE_HALF, tm), lambda b, i: (0, i)),
            pl.BlockSpec((ROPE_HALF, tm), lambda b, i: (0, i)),
            pl.BlockSpec((tm, LANES), lambda b, i: (i, 0)),
            pl.BlockSpec((tm, LANES), lambda b, i: (i, 0)),
        ],
        out_specs=[
            pl.BlockSpec((1, tm, 2 * MIX_WIDTH), row),
            pl.BlockSpec((1, tm, MIX_WIDTH), row),
            pl.BlockSpec((1, tm, MIX_WIDTH), row),
            pl.BlockSpec((1, tm, LANES), row),
            pl.BlockSpec((1, MLA_HEADS * LANES, tm), lambda b, i: (b, 0, i)),
            pl.BlockSpec((1, tm, MLA_HEADS * LANES), row),
            pl.BlockSpec((1, tm // ATTN_BLOCK, MLA_HEADS * MLA_V_DIM, ATTN_BLOCK), lambda b, i: (b, i, 0, 0)),
        ],
        out_shape=[
            jax.ShapeDtypeStruct((B, S, 2 * MIX_WIDTH), F32),
            jax.ShapeDtypeStruct((B, S, MIX_WIDTH), BF16),
            jax.ShapeDtypeStruct((B, S, MIX_WIDTH), F32),
            jax.ShapeDtypeStruct((B, S, LANES), F32),
            jax.ShapeDtypeStruct((B, MLA_HEADS * LANES, S), BF16),
            jax.ShapeDtypeStruct((B, S, MLA_HEADS * LANES), BF16),
            jax.ShapeDtypeStruct((B, nb, MLA_HEADS * MLA_V_DIM, ATTN_BLOCK), BF16),
        ],
        compiler_params=_params(("parallel", "parallel")),
        name="odd_in",
    )(x, g.reshape(1, D), wn, q_norm.reshape(1, -1), wuqT, kv_norm.reshape(1, -1), wk2, wvT,
      cosT, sinT, cc, ss)


def _log_sigmoid(x):
    return jnp.minimum(x, 0.0) - jnp.log(1.0 + jnp.exp(-jnp.abs(x)))


def _mlstm_kernel(qk_ref, v_ref, op_ref, misc_ref, cw_ref, gb_ref, hn_ref, o_ref,
                  prev_ref, cn_ref, m_ref):
    c = pl.program_id(1)
    L = qk_ref.shape[1]
    row = lax.broadcasted_iota(jnp.int32, (L, 1), 0)
    lane = lax.broadcasted_iota(jnp.int32, (L, LANES), 1)
    is_f = (lane >= MISC_F) & (lane < MISC_F + MLSTM_HEADS)
    causal = lax.broadcasted_iota(jnp.int32, (L, L), 1) <= lax.broadcasted_iota(jnp.int32, (L, L), 0)

    @pl.when(c == 0)
    def _():
        prev_ref[...] = jnp.zeros_like(prev_ref)
        cn_ref[...] = jnp.zeros_like(cn_ref)
        m_ref[...] = jnp.zeros_like(m_ref)

    seqs = range(qk_ref.shape[0])
    heads = [(bb, h) for bb in seqs for h in range(MLSTM_HEADS)]
    d = MLSTM_HEAD_DIM

    qk = []
    for bb in seqs:
        x = qk_ref[bb]
        prev = prev_ref[bb]
        conv = x * cw_ref[CONV_WIDTH - 1:CONV_WIDTH, :]
        for j in range(1, CONV_WIDTH):
            shifted = pltpu.roll(jnp.where(row >= L - j, prev, x), j, axis=0)
            conv = conv + shifted * cw_ref[CONV_WIDTH - 1 - j:CONV_WIDTH - j, :]
        prev_ref[bb] = x
        qk.append(conv * _sigmoid(conv))

    sel_r = lax.broadcasted_iota(jnp.int32, (LANES, 2 * MIX_WIDTH), 0)
    sel_c = lax.broadcasted_iota(jnp.int32, (LANES, 2 * MIX_WIDTH), 1)
    spread = (sel_r == MISC_I + sel_c // LANES).astype(F32)
    pick = (lax.broadcasted_iota(jnp.int32, (8, LANES), 1)
            == MISC_I + lax.broadcasted_iota(jnp.int32, (8, LANES), 0)).astype(F32)
    mean_mat = jnp.full((d, d), 1.0 / d, F32)
    cols, rows = [], []
    for bb in seqs:
        gates = misc_ref[bb] + gb_ref[...]
        z = jnp.where(is_f, _log_sigmoid(gates), gates)
        cum = jnp.dot(causal.astype(F32), z, precision=HIGHEST, preferred_element_type=F32)
        z = jnp.where(is_f, cum, z)
        cols.append(jnp.dot(z, spread, precision=HIGHEST, preferred_element_type=F32))
        rows.append(_dot_nt(pick, z, precision=HIGHEST))

    q, k, v_aug, i_b, b_b, m_prev, m_t, w_inter, scores, inter = ({} for _ in range(10))
    ones_blk = jnp.ones((L, LANES), BF16)
    for key in heads:
        bb, h = key
        lo, hi = h * d, (h + 1) * d
        q[key] = qk[bb][:, lo:hi].astype(BF16)
        k[key] = qk[bb][:, MIX_WIDTH + lo:MIX_WIDTH + hi] * (d ** -0.5)
        v_aug[key] = jnp.concatenate([v_ref[bb, :, lo:hi], ones_blk], axis=1)
        scores[key] = _dot_nt(q[key], k[key].astype(BF16))
        inter[key] = _dot(q[key], cn_ref[bb, h].astype(BF16))
    intra = {}
    for key in heads:
        bb, h = key
        i_b[key] = cols[bb][:, h * LANES:(h + 1) * LANES]
        b_b[key] = cols[bb][:, (MLSTM_HEADS + h) * LANES:(MLSTM_HEADS + h + 1) * LANES]
        i_row = rows[bb][h:h + 1, :]
        b_row = rows[bb][MLSTM_HEADS + h:MLSTM_HEADS + h + 1, :]
        m_prev[key] = m_ref[bb, h:h + 1, :]
        intra[key] = jnp.where(causal, b_b[key] - b_row + i_row, NEG_INF)
    for key in heads:
        m_inter = b_b[key] + m_prev[key]
        m_t[key] = jnp.maximum(m_inter, jnp.max(intra[key], axis=1, keepdims=True))
        w_inter[key] = jnp.exp(m_inter - m_t[key])
    intra_o = {}
    for key in heads:
        a = jnp.exp(intra[key] - m_t[key]) * scores[key]
        intra_o[key] = _dot(a.astype(BF16), v_aug[key])
    for key in heads:
        bb, h = key
        lo, hi = h * d, (h + 1) * d
        num = w_inter[key] * inter[key][:, :d] + intra_o[key][:, :d]
        den = w_inter[key] * inter[key][:, d:] + intra_o[key][:, d:]
        hh = num / jnp.maximum(jnp.abs(den), jnp.exp(-m_t[key]))
        ms = jnp.dot(hh * hh, mean_mat, precision=HIGHEST, preferred_element_type=F32)
        hh = hh * lax.rsqrt(ms + NORM_EPS) * hn_ref[:, lo:hi]
        o_ref[bb, :, lo:hi] = (hh * _sigmoid(op_ref[bb, :, lo:hi])).astype(o_ref.dtype)
    for key in heads:
        bb, h = key
        b_end = b_b[key][L - 1:L, :]
        g = b_end - b_b[key] + i_b[key]
        m_new = jnp.maximum(b_end + m_prev[key], jnp.max(g, axis=0, keepdims=True))
        decay = jnp.exp(b_end + m_prev[key] - m_new)
        kw = k[key] * jnp.exp(g - m_new)
        cn_ref[bb, h] = (jnp.concatenate([decay, decay], axis=1) * cn_ref[bb, h]
                         + _dot(kw.T.astype(BF16), v_aug[key]))
        m_ref[bb, h:h + 1, :] = m_new


def _mlstm(qk_raw, vc, o_pre, misc, conv_w, b_i, b_f, head_norm):
    B, S, _ = qk_raw.shape
    L = MLSTM_CHUNK
    nb = MLSTM_BATCH
    gb = jnp.zeros((1, LANES), F32).at[0, MISC_I:MISC_I + MLSTM_HEADS].set(b_i)
    gb = gb.at[0, MISC_F:MISC_F + MLSTM_HEADS].set(b_f)
    row = lambda b, c: (b, c, 0)
    const = lambda b, c: (0, 0)
    return pl.pallas_call(
        _mlstm_kernel,
        grid=(B // nb, S // L),
        in_specs=[
            pl.BlockSpec((nb, L, 2 * MIX_WIDTH), row),
            pl.BlockSpec((nb, L, MIX_WIDTH), row),
            pl.BlockSpec((nb, L, MIX_WIDTH), row),
            pl.BlockSpec((nb, L, LANES), row),
            pl.BlockSpec((CONV_WIDTH, 2 * MIX_WIDTH), const),
            pl.BlockSpec((1, LANES), const),
            pl.BlockSpec((1, MIX_WIDTH), const),
        ],
        out_specs=pl.BlockSpec((nb, L, MIX_WIDTH), row),
        out_shape=jax.ShapeDtypeStruct((B, S, MIX_WIDTH), BF16),
        scratch_shapes=[
            pltpu.VMEM((nb, L, 2 * MIX_WIDTH), F32),
            pltpu.VMEM((nb, MLSTM_HEADS, MLSTM_HEAD_DIM, 2 * LANES), F32),
            pltpu.VMEM((nb, 8, LANES), F32),
        ],
        compiler_params=_params(("parallel", "arbitrary")),
        name="mlstm",
    )(qk_raw, vc, o_pre, misc, conv_w, gb, head_norm.reshape(1, MIX_WIDTH))


ZERO_WEIGHT_LOG2 = 160.0
NORM_SLACK = 1.02


def _alibi_first_tile(qn2, kn2, slopes):
    B, H, S = qn2.shape
    nt = S // ATTN_BLOCK
    k_max = jnp.sqrt(jnp.max(kn2[:, :, :H], axis=1))
    q_max = jnp.sqrt(jnp.max(qn2.reshape(B, H, nt, ATTN_BLOCK), axis=3))
    c = MOBA_HEAD_DIM ** -0.5 * LOG2E
    reach = ((ZERO_WEIGHT_LOG2 + 2.0 * NORM_SLACK * c * q_max * k_max[:, :, None])
             / (jnp.asarray(slopes)[None, :, None] * LOG2E))
    tiles = jnp.minimum(jnp.ceil((reach - 1.0) / ATTN_BLOCK), nt)
    tiles = jnp.max(tiles.reshape(B, H // HEADS_PER_STEP, HEADS_PER_STEP, nt), axis=2)
    first = jnp.arange(nt, dtype=F32)[None, None, :] - tiles
    return jnp.maximum(first, 0.0).astype(jnp.int32)


def _even_layer(h, norm_mix, w_in, pool_w, pool_scale, w_out, norm_ffn, wg, wu, wd):
    B, S, D = h.shape
    slopes = (2.0 ** (-8.0 * np.arange(1, MOBA_HEADS + 1) / MOBA_HEADS)).astype(np.float32)
    ka, ub, kmean, kn2, qT, vT = _even_in(h, norm_mix, w_in, slopes)
    nb = S // MOBA_BLOCK
    kmean = kmean.reshape(B, nb, MOBA_HEADS, LANES)[..., :MOBA_HEAD_DIM].transpose(0, 2, 1, 3)
    qaT, qn2 = _moba_gate(kmean, qT)
    aT = _attention(qaT, ka, vT, MOBA_HEAD_DIM, MOBA_HEADS, "moba_attn",
                    first_tile=_alibi_first_tile(qn2, kn2, slopes))
    b_out = _pool(ub, pool_w, pool_scale)
    h = _mix_out(h, aT, b_out, w_out, a_first=True)
    return _ffn(h.reshape(B * S, D), norm_ffn, wg.astype(BF16), wu.astype(BF16), wd.astype(BF16))


def _odd_layer(h, norm_mix, w_in, conv_w, b_i, b_f, head_norm, q_norm, w_uq, kv_norm, w_ukv,
               w_out, norm_ffn, router_w, router_b, wg, wu, wd, ple):
    B, S, D = h.shape
    qk_raw, vc, o_pre, misc, mqT, mk, mvT = _odd_in(h, norm_mix, w_in, q_norm, w_uq, kv_norm, w_ukv)
    c_out = _mlstm(qk_raw, vc, o_pre, misc, conv_w, b_i, b_f, head_norm)
    dT = _attention(mqT, mk, mvT, MLA_V_DIM, MLA_HEADS, "mla_attn")
    h = _mix_out(h, dT, c_out, w_out, a_first=False)
    return _moe(h.reshape(B * S, D), norm_ffn, router_w, router_b,
                wg.astype(BF16), wu.astype(BF16), wd.astype(BF16), ple)


def kernel(x, p, ev_norm_mix, ev_w_in, pool_w, pool_scale, ev_w_out, ev_norm_ffn, ffn_w_gate, ffn_w_up, ffn_w_down, od_norm_mix, od_w_in, conv_w, gate_b_i, gate_b_f, mlstm_norm, mla_q_norm, mla_w_uq, mla_kv_norm, mla_w_ukv, od_w_out, od_norm_ffn, router_w, router_b, moe_w_gate, moe_w_up, moe_w_down, ple_norm, ple_w_gate, ple_w_proj, final_norm):
    B, S, D = x.shape
    depth = p.shape[0]
    assert D == D_MODEL and S % PROJ_TILE == 0 and B % MLSTM_BATCH == 0
    assert MOBA_TOPK <= S // MOBA_BLOCK <= MOBA_MAX_BLOCKS
    h = x
    assert depth % 2 == 0
    for layer in range(depth):
        j = layer // 2
        ple = (p[layer].reshape(B * S, PLE_DIM), ple_norm[layer], ple_w_gate[layer], ple_w_proj[layer])
        if layer % 2 == 0:
            h2d = _even_layer(h, ev_norm_mix[j], ev_w_in[j], pool_w[j], pool_scale[j], ev_w_out[j],
                              ev_norm_ffn[j], ffn_w_gate[j], ffn_w_up[j], ffn_w_down[j])
            h2d = _ple(h2d, *ple)
        else:
            last = layer == depth - 1
            h2d = _odd_layer(h, od_norm_mix[j], od_w_in[j], conv_w[j], gate_b_i[j], gate_b_f[j],
                             mlstm_norm[j], mla_q_norm[j], mla_w_uq[j], mla_kv_norm[j], mla_w_ukv[j],
                             od_w_out[j], od_norm_ffn[j], router_w[j], router_b[j],
                             moe_w_gate[j], moe_w_up[j], moe_w_down[j],
                             ple + ((final_norm,) if last else ()))
        h = h2d.reshape(B, S, D)
    return h
```
